```python
import jax, jax.numpy as jnp
from jax import lax
import numpy as np

D_MODEL = 2048
BATCH = 8
SEQ = 8192
DEPTH = 1

HEAD_DIM = 64
N_HEADS_A = D_MODEL // (2 * HEAD_DIM)
N_KV_A = N_HEADS_A // 8
N_HEADS_B = D_MODEL // (2 * HEAD_DIM)
WINDOW_A = 128
DILATED_BRANCHES = ((128, 1), (512, 4), (2048, 16))
D_FF = 4 * D_MODEL
BLOCK = 128
EPS = 1e-5
NEG_INF = -1e30

Q_A = N_HEADS_A * HEAD_DIM
KV_A = N_KV_A * HEAD_DIM
Q_B = N_HEADS_B * HEAD_DIM
D_IN = Q_A + 2 * KV_A + 3 * Q_B
D_MIX = Q_A + Q_B

kernel_name = "hybrid_swa_sink_dilated_alibi_block"


def alibi_slopes(n):
    return jnp.asarray(2.0 ** (-8.0 * (np.arange(n) + 1) / n), dtype=jnp.float32)


def rmsnorm(x, g):
    x32 = x.astype(jnp.float32)
    y = x32 * lax.rsqrt(jnp.mean(x32 * x32, axis=-1, keepdims=True) + EPS)
    return y.astype(x.dtype) * g


def _with_prev_block(t, nb):
    b, L, G, Dh = t.shape
    tb = t.reshape(b, nb, BLOCK, G, Dh)
    prev = jnp.concatenate([jnp.zeros_like(tb[:, :1]), tb[:, :-1]], axis=1)
    return jnp.concatenate([prev, tb], axis=2)


def banded_attention(q, k, v, max_steps, step_dist, slopes, sinks):
    b, L, H, Dh = q.shape
    G = k.shape[2]
    R = H // G
    nb = L // BLOCK
    qb = q.reshape(b, nb, BLOCK, G, R, Dh)
    kb = _with_prev_block(k, nb)
    vb = _with_prev_block(v, nb)
    s = jnp.einsum('bnqgrd,bnkgd->bngrqk', qb, kb).astype(jnp.float32) * (Dh ** -0.5)
    qi = jnp.arange(BLOCK)[:, None]
    kj = jnp.arange(2 * BLOCK)[None, :]
    steps = qi + BLOCK - kj
    kpos = jnp.arange(nb)[:, None, None] * BLOCK + kj[None] - BLOCK
    valid = (steps >= 0) & (steps <= max_steps) & (kpos >= 0)
    alibi = slopes.reshape(G, R, 1, 1) * (step_dist * steps).astype(jnp.float32)
    s = jnp.where(valid[None, :, None, None], s - alibi[None, None], NEG_INF)
    m = jnp.max(s, axis=-1)
    if sinks is not None:
        sink = sinks.astype(jnp.float32).reshape(G, R, 1)
        m = jnp.maximum(m, sink)
    p = jnp.exp(s - m[..., None])
    denom = jnp.sum(p, axis=-1)
    if sinks is not None:
        denom = denom + jnp.exp(sink - m)
    o = jnp.einsum('bngrqk,bnkgd->bnqgrd', p, vb.astype(jnp.float32))
    o = o / jnp.moveaxis(denom, -1, 2)[..., None]
    lse = jnp.moveaxis(m + jnp.log(denom), -1, 2)
    return o.reshape(b, L, H, Dh), lse.reshape(b, L, H)


def _strided(t, dil, Lp):
    b, S, H, Dh = t.shape
    L = S // dil
    t = t.reshape(b, L, dil, H, Dh).transpose(0, 2, 1, 3, 4).reshape(b * dil, L, H, Dh)
    return jnp.pad(t, ((0, 0), (0, Lp - L), (0, 0), (0, 0)))


def dilated_mixture(q, k, v, slopes):
    b, S, H, Dh = q.shape
    outs, lses = [], []
    for window, dil in DILATED_BRANCHES:
        L = S // dil
        Lp = -(-L // BLOCK) * BLOCK
        o, lse = banded_attention(_strided(q, dil, Lp), _strided(k, dil, Lp),
                                  _strided(v, dil, Lp), window // dil, dil, slopes, None)
        outs.append(o[:, :L].reshape(b, dil, L, H, Dh).transpose(0, 2, 1, 3, 4).reshape(b, S, H, Dh))
        lses.append(lse[:, :L].reshape(b, dil, L, H).transpose(0, 2, 1, 3).reshape(b, S, H))
    w = jax.nn.softmax(jnp.stack(lses), axis=0)
    return jnp.einsum('nbsh,nbshd->bshd', w, jnp.stack(outs)).astype(q.dtype)


def _fwd_setup_inputs(seed: int = 0) -> dict:
    key = jax.random.key(seed)
    ks = jax.random.split(key, 12)
    f32 = jnp.float32
    x = jax.random.normal(ks[0], (BATCH, SEQ, D_MODEL), f32)
    g_attn = 1.0 + 0.02 * jax.random.normal(ks[1], (DEPTH, D_MODEL), f32)
    w_in = jax.random.normal(ks[2], (DEPTH, D_MODEL, D_IN), f32) * D_MODEL ** -0.5
    b_in = 0.02 * jax.random.normal(ks[3], (DEPTH, D_IN), f32)
    sinks_a = jax.random.normal(ks[4], (DEPTH, N_HEADS_A), f32)
    g_out_a = 1.0 + 0.02 * jax.random.normal(ks[5], (DEPTH, Q_A), f32)
    g_out_b = 1.0 + 0.02 * jax.random.normal(ks[6], (DEPTH, Q_B), f32)
    w_out = jax.random.normal(ks[7], (DEPTH, D_MIX, D_MODEL), f32) * D_MIX ** -0.5
    g_mlp = 1.0 + 0.02 * jax.random.normal(ks[8], (DEPTH, D_MODEL), f32)
    w_1 = jax.random.normal(ks[9], (DEPTH, D_MODEL, D_FF), f32) * D_MODEL ** -0.5
    w_2 = jax.random.normal(ks[10], (DEPTH, D_FF, D_MODEL), f32) * D_FF ** -0.5
    g_final = 1.0 + 0.02 * jax.random.normal(ks[11], (D_MODEL,), f32)
    return {"x": x, "g_attn": g_attn, "w_in": w_in, "b_in": b_in, "sinks_a": sinks_a,
            "g_out_a": g_out_a, "g_out_b": g_out_b, "w_out": w_out, "g_mlp": g_mlp,
            "w_1": w_1, "w_2": w_2, "g_final": g_final}


def _fwd_reference(x, g_attn, w_in, b_in, sinks_a, g_out_a, g_out_b, w_out, g_mlp, w_1, w_2, g_final):
    b, S, _ = x.shape
    slopes_a = alibi_slopes(N_HEADS_A)
    slopes_b = alibi_slopes(N_HEADS_B)
    for l in range(DEPTH):
        h = rmsnorm(x, g_attn[l])
        proj = jnp.einsum('bsd,de->bse', h, w_in[l]) + b_in[l]
        o1 = Q_A
        o2 = o1 + KV_A
        o3 = o2 + KV_A
        o4 = o3 + Q_B
        o5 = o4 + Q_B
        qa = proj[..., :o1].reshape(b, S, N_HEADS_A, HEAD_DIM)
        ka = proj[..., o1:o2].reshape(b, S, N_KV_A, HEAD_DIM)
        va = proj[..., o2:o3].reshape(b, S, N_KV_A, HEAD_DIM)
        qb = proj[..., o3:o4].reshape(b, S, N_HEADS_B, HEAD_DIM)
        kb = proj[..., o4:o5].reshape(b, S, N_HEADS_B, HEAD_DIM)
        vb = proj[..., o5:].reshape(b, S, N_HEADS_B, HEAD_DIM)
        oa, _ = banded_attention(qa, ka, va, WINDOW_A - 1, 1, slopes_a, sinks_a[l])
        ya = rmsnorm(oa.astype(x.dtype).reshape(b, S, Q_A), g_out_a[l])
        ob = dilated_mixture(qb, kb, vb, slopes_b)
        yb = rmsnorm(ob.reshape(b, S, Q_B), g_out_b[l])
        mix = jnp.concatenate([ya, yb], axis=-1)
        x = x + jnp.einsum('bse,ed->bsd', mix, w_out[l])
        h = rmsnorm(x, g_mlp[l])
        u = jax.nn.relu(jnp.einsum('bsd,df->bsf', h, w_1[l]))
        x = x + jnp.einsum('bsf,fd->bsd', u * u, w_2[l])
    return rmsnorm(x, g_final)


import jax as _jax
import jax.numpy as _jnp

TWIN_FORMAT = 'train_step'
FWD_PARAMS = ['x', 'g_attn', 'w_in', 'b_in', 'sinks_a', 'g_out_a', 'g_out_b', 'w_out', 'g_mlp', 'w_1', 'w_2', 'g_final']
TWIN_WEIGHTS = ['g_attn', 'w_in', 'b_in', 'sinks_a', 'g_out_a', 'g_out_b', 'w_out', 'g_mlp', 'w_1', 'w_2', 'g_final']
TWIN_DIFF_INPUT = 'x'
TWIN_INPUTS = ['x', 'g_attn', 'w_in', 'b_in', 'sinks_a', 'g_out_a', 'g_out_b', 'w_out', 'g_mlp', 'w_1', 'w_2', 'g_final', 'loss_target', 'm_g_attn', 'm_w_in', 'm_b_in', 'm_sinks_a', 'm_g_out_a', 'm_g_out_b', 'm_w_out', 'm_g_mlp', 'm_w_1', 'm_w_2', 'm_g_final', 'v_g_attn', 'v_w_in', 'v_b_in', 'v_sinks_a', 'v_g_out_a', 'v_g_out_b', 'v_w_out', 'v_g_mlp', 'v_w_1', 'v_w_2', 'v_g_final']
TWIN_OUTPUTS = ['loss', 'grad_x', 'grad_g_attn', 'grad_w_in', 'grad_b_in', 'grad_sinks_a', 'grad_g_out_a', 'grad_g_out_b', 'grad_w_out', 'grad_g_mlp', 'grad_w_1', 'grad_w_2', 'grad_g_final', 'delta_g_attn', 'delta_w_in', 'delta_b_in', 'delta_sinks_a', 'delta_g_out_a', 'delta_g_out_b', 'delta_w_out', 'delta_g_mlp', 'delta_w_1', 'delta_w_2', 'delta_g_final', 'new_m_g_attn', 'new_m_w_in', 'new_m_b_in', 'new_m_sinks_a', 'new_m_g_out_a', 'new_m_g_out_b', 'new_m_w_out', 'new_m_g_mlp', 'new_m_w_1', 'new_m_w_2', 'new_m_g_final', 'new_v_g_attn', 'new_v_w_in', 'new_v_b_in', 'new_v_sinks_a', 'new_v_g_out_a', 'new_v_g_out_b', 'new_v_w_out', 'new_v_g_mlp', 'new_v_w_1', 'new_v_w_2', 'new_v_g_final']
TWIN_LEAF_KINDS = {'loss': 'loss', 'grad_x': 'grad_x', 'grad_g_attn': 'grad_w', 'grad_w_in': 'grad_w', 'grad_b_in': 'grad_w', 'grad_sinks_a': 'grad_w', 'grad_g_out_a': 'grad_w', 'grad_g_out_b': 'grad_w', 'grad_w_out': 'grad_w', 'grad_g_mlp': 'grad_w', 'grad_w_1': 'grad_w', 'grad_w_2': 'grad_w', 'grad_g_final': 'grad_w', 'delta_g_attn': 'delta_w', 'delta_w_in': 'delta_w', 'delta_b_in': 'delta_w', 'delta_sinks_a': 'delta_w', 'delta_g_out_a': 'delta_w', 'delta_g_out_b': 'delta_w', 'delta_w_out': 'delta_w', 'delta_g_mlp': 'delta_w', 'delta_w_1': 'delta_w', 'delta_w_2': 'delta_w', 'delta_g_final': 'delta_w', 'new_m_g_attn': 'new_m', 'new_m_w_in': 'new_m', 'new_m_b_in': 'new_m', 'new_m_sinks_a': 'new_m', 'new_m_g_out_a': 'new_m', 'new_m_g_out_b': 'new_m', 'new_m_w_out': 'new_m', 'new_m_g_mlp': 'new_m', 'new_m_w_1': 'new_m', 'new_m_w_2': 'new_m', 'new_m_g_final': 'new_m', 'new_v_g_attn': 'new_v', 'new_v_w_in': 'new_v', 'new_v_b_in': 'new_v', 'new_v_sinks_a': 'new_v', 'new_v_g_out_a': 'new_v', 'new_v_g_out_b': 'new_v', 'new_v_w_out': 'new_v', 'new_v_g_mlp': 'new_v', 'new_v_w_1': 'new_v', 'new_v_w_2': 'new_v', 'new_v_g_final': 'new_v'}


def _forward(args):
    return _fwd_reference(*[args[k] for k in FWD_PARAMS])


def _output_shape():
    def fwd():
        inp = _fwd_setup_inputs(0)
        return _fwd_reference(*[inp[k] for k in FWD_PARAMS])
    out = _jax.eval_shape(fwd)
    return out.shape, out.dtype

N_MICROBATCH = 1
ADAM_LR = 0.001
ADAM_B1 = 0.9
ADAM_B2 = 0.999
ADAM_EPS = 1e-08
ADAM_WD = 0.01
ADAM_STEP = 10
PER_EXAMPLE_BATCH_AXIS = {'x': 0, 'loss_target': 0}
SHARED_INPUTS = []
_WEIGHT_DTYPES = {'g_attn': _jnp.float32, 'w_in': _jnp.float32, 'b_in': _jnp.float32, 'sinks_a': _jnp.float32, 'g_out_a': _jnp.float32, 'g_out_b': _jnp.float32, 'w_out': _jnp.float32, 'g_mlp': _jnp.float32, 'w_1': _jnp.float32, 'w_2': _jnp.float32, 'g_final': _jnp.float32}
MOMENT_SCALE = {'g_attn': 1.403577e-01, 'w_in': 9.203741e-02, 'b_in': 3.370140e-01, 'sinks_a': 8.291831e-02, 'g_out_a': 9.870841e-02, 'g_out_b': 9.761685e-02, 'w_out': 9.528336e-02, 'g_mlp': 9.885861e-02, 'w_1': 4.742964e-02, 'w_2': 9.579916e-02, 'g_final': 3.214682e+01}


def _to_microbatches(a, axis):
    t = _jnp.moveaxis(a, axis, 0)
    t = t.reshape((N_MICROBATCH, t.shape[0] // N_MICROBATCH) + t.shape[1:])
    return _jnp.moveaxis(t, 1, axis + 1)


def setup_inputs(seed: int = 0) -> dict:
    inp = _fwd_setup_inputs(seed)
    key = _jax.random.fold_in(_jax.random.key(seed), 7919)
    shape, _ = _output_shape()
    out = dict(inp)
    out["loss_target"] = _jax.random.normal(_jax.random.fold_in(key, 0), shape, _jnp.float32)
    for i, name in enumerate(TWIN_WEIGHTS):
        w = inp[name].astype(_jnp.float32)
        if MOMENT_SCALE is None:
            s = _jnp.sqrt(_jnp.mean(_jnp.square(w)) + 1e-30)
        else:
            s = MOMENT_SCALE[name]
        km, kv = _jax.random.split(_jax.random.fold_in(key, i + 1))
        out[name] = w
        out["m_" + name] = s * _jax.random.normal(km, w.shape, _jnp.float32)
        out["v_" + name] = (s * s) * _jax.random.uniform(kv, w.shape, _jnp.float32, 0.5, 1.5)
    if N_MICROBATCH > 1:
        for name, axis in PER_EXAMPLE_BATCH_AXIS.items():
            out[name] = _to_microbatches(out[name], axis)
    return {'x': out['x'], 'g_attn': out['g_attn'], 'w_in': out['w_in'], 'b_in': out['b_in'], 'sinks_a': out['sinks_a'], 'g_out_a': out['g_out_a'], 'g_out_b': out['g_out_b'], 'w_out': out['w_out'], 'g_mlp': out['g_mlp'], 'w_1': out['w_1'], 'w_2': out['w_2'], 'g_final': out['g_final'], 'loss_target': out['loss_target'], 'm_g_attn': out['m_g_attn'], 'm_w_in': out['m_w_in'], 'm_b_in': out['m_b_in'], 'm_sinks_a': out['m_sinks_a'], 'm_g_out_a': out['m_g_out_a'], 'm_g_out_b': out['m_g_out_b'], 'm_w_out': out['m_w_out'], 'm_g_mlp': out['m_g_mlp'], 'm_w_1': out['m_w_1'], 'm_w_2': out['m_w_2'], 'm_g_final': out['m_g_final'], 'v_g_attn': out['v_g_attn'], 'v_w_in': out['v_w_in'], 'v_b_in': out['v_b_in'], 'v_sinks_a': out['v_sinks_a'], 'v_g_out_a': out['v_g_out_a'], 'v_g_out_b': out['v_g_out_b'], 'v_w_out': out['v_w_out'], 'v_g_mlp': out['v_g_mlp'], 'v_w_1': out['v_w_1'], 'v_w_2': out['v_w_2'], 'v_g_final': out['v_g_final']}


def _loss(weights, diff, rest, loss_target):
    with _jax.named_scope("forward"):
        args = {**rest, TWIN_DIFF_INPUT: diff, **{k: w.astype(_WEIGHT_DTYPES[k]) for k, w in weights.items()}}
        y = _forward(args)
    with _jax.named_scope("loss_head"):
        err = _jnp.square(y.astype(_jnp.float32) - loss_target)
        return 0.5 * _jnp.sum(_jnp.mean(err, axis=-1)) if err.ndim else 0.5 * err


def _adamw(w, g, m, v):
    m = ADAM_B1 * m + (1.0 - ADAM_B1) * g
    v = ADAM_B2 * v + (1.0 - ADAM_B2) * _jnp.square(g)
    m_hat = m / (1.0 - ADAM_B1 ** ADAM_STEP)
    v_hat = v / (1.0 - ADAM_B2 ** ADAM_STEP)
    delta = -ADAM_LR * (m_hat / (_jnp.sqrt(v_hat) + ADAM_EPS) + ADAM_WD * w)
    return delta, m, v


def reference(x, g_attn, w_in, b_in, sinks_a, g_out_a, g_out_b, w_out, g_mlp, w_1, w_2, g_final, loss_target, m_g_attn, m_w_in, m_b_in, m_sinks_a, m_g_out_a, m_g_out_b, m_w_out, m_g_mlp, m_w_1, m_w_2, m_g_final, v_g_attn, v_w_in, v_b_in, v_sinks_a, v_g_out_a, v_g_out_b, v_w_out, v_g_mlp, v_w_1, v_w_2, v_g_final):
    given = dict(x=x, g_attn=g_attn, w_in=w_in, b_in=b_in, sinks_a=sinks_a, g_out_a=g_out_a, g_out_b=g_out_b, w_out=w_out, g_mlp=g_mlp, w_1=w_1, w_2=w_2, g_final=g_final, loss_target=loss_target, m_g_attn=m_g_attn, m_w_in=m_w_in, m_b_in=m_b_in, m_sinks_a=m_sinks_a, m_g_out_a=m_g_out_a, m_g_out_b=m_g_out_b, m_w_out=m_w_out, m_g_mlp=m_g_mlp, m_w_1=m_w_1, m_w_2=m_w_2, m_g_final=m_g_final, v_g_attn=v_g_attn, v_w_in=v_w_in, v_b_in=v_b_in, v_sinks_a=v_sinks_a, v_g_out_a=v_g_out_a, v_g_out_b=v_g_out_b, v_w_out=v_w_out, v_g_mlp=v_g_mlp, v_w_1=v_w_1, v_w_2=v_w_2, v_g_final=v_g_final)
    weights = {n: given[n] for n in TWIN_WEIGHTS}
    shared = {n: given[n] for n in SHARED_INPUTS}
    per_example = {n: given[n] for n in ['x']}
    grad_fn = _jax.value_and_grad(_loss, argnums=(0, 1))

    def one_microbatch(ex, loss_target):
        ex = dict(ex)
        diff = ex.pop(TWIN_DIFF_INPUT)
        return grad_fn(weights, diff, {**shared, **ex}, loss_target)

    if N_MICROBATCH == 1:
        loss, (grad_w, grad_x) = one_microbatch(per_example, given["loss_target"])
    else:
        def body(carry, xs):
            loss_sum, grad_sum = carry
            l_k, (gw_k, gx_k) = one_microbatch(xs[0], xs[1])
            with _jax.named_scope("update"):
                return (loss_sum + l_k, _jax.tree.map(_jnp.add, grad_sum, gw_k)), gx_k

        init = (_jnp.zeros((), _jnp.float32), _jax.tree.map(_jnp.zeros_like, weights))
        (loss, grad_w), grad_x = _jax.lax.scan(body, init, (per_example, given["loss_target"]))
    with _jax.named_scope("update"):
        delta_w, new_m, new_v = {}, {}, {}
        for n in TWIN_WEIGHTS:
            delta_w[n], new_m[n], new_v[n] = _adamw(weights[n], grad_w[n], given["m_" + n], given["v_" + n])
    return (loss, grad_x, *[grad_w[n] for n in TWIN_WEIGHTS], *[delta_w[n] for n in TWIN_WEIGHTS],
            *[new_m[n] for n in TWIN_WEIGHTS], *[new_v[n] for n in TWIN_WEIGHTS])
```

```python
import functools

import jax
import jax.numpy as jnp
import numpy as np
from jax import lax
from jax.experimental import pallas as pl
from jax.experimental.pallas import tpu as pltpu

F32 = jnp.float32
BF16 = jnp.bfloat16

HEAD_DIM = 64
BLOCK = 128
PAIR = 2 * HEAD_DIM
N_KV_GROUPS = 2
WINDOW_A = 128
DILATED_BRANCHES = ((128, 1), (512, 4), (2048, 16))
EPS = 1e-5
NEG_INF = -1e30
ATT_SCALE = HEAD_DIM ** -0.5

ADAM_LR = 0.001
ADAM_B1 = 0.9
ADAM_B2 = 0.999
ADAM_EPS = 1e-08
ADAM_WD = 0.01
ADAM_STEP = 10

N_CHIPS = 4
N_DEV = 8
MESH = pl.DeviceIdType.MESH
GRAD_WIRE_DTYPE = jnp.float32

VMEM_CAPACITY_V7X = 64 * 1024 * 1024
VMEM_LIMIT_MAX = 56 * 1024 * 1024
VMEM_LIMIT_MIN = 32 * 1024 * 1024

HBM_SPEC = pl.BlockSpec(memory_space=pltpu.HBM)
VMEM_SPEC = pl.BlockSpec(memory_space=pltpu.VMEM)
SMEM_SPEC = pl.BlockSpec(memory_space=pltpu.SMEM)


def _nbytes(shape, dtype):
    return int(np.prod([s for s in shape if s is not None])) * jnp.dtype(dtype).itemsize


def _params(semantics, block_bytes):
    limit = min(max(2 * block_bytes + (4 << 20), VMEM_LIMIT_MIN), VMEM_LIMIT_MAX)
    return pltpu.CompilerParams(dimension_semantics=semantics, vmem_limit_bytes=limit)


def alibi_slopes(n):
    return [float(v) for v in np.asarray(2.0 ** (-8.0 * (np.arange(n) + 1) / n), dtype=np.float32)]


def _matmul(name, a, b, extras, *, mode, grid, a_spec, b_spec, extra_specs, out_shapes, out_specs,
            epilogue, prologue=None, acc_shape=None):
    dims = {"nn": ((1,), (0,)), "nt": ((1,), (1,)), "tn": ((0,), (0,))}[mode]
    nk = grid[2]
    n_ex, n_out = len(extras), len(out_shapes)

    def body(a_ref, b_ref, *rest):
        ex, outs = rest[:n_ex], rest[n_ex:n_ex + n_out]
        av = a_ref[...]
        if prologue is not None:
            av = prologue(av)
        part = lax.dot_general(av, b_ref[...], (dims, ((), ())), preferred_element_type=F32)

        def finish(acc):
            res = epilogue(acc, *[e[...] for e in ex])
            for o, r in zip(outs, res):
                o[...] = r.astype(o.dtype)

        if nk == 1:
            finish(part)
        else:
            acc_ref = rest[-1]
            k = pl.program_id(2)

            @pl.when(k == 0)
            def _():
                acc_ref[...] = part

            @pl.when(k > 0)
            def _():
                acc_ref[...] += part

            @pl.when(k == nk - 1)
            def _():
                finish(acc_ref[...])

    blocks = [(a_spec.block_shape, a.dtype), (b_spec.block_shape, b.dtype)]
    blocks += [(s.block_shape, e.dtype) for s, e in zip(extra_specs, extras)]
    blocks += [(s.block_shape, o.dtype) for s, o in zip(out_specs, out_shapes)]
    nbytes = sum(_nbytes(s, d) for s, d in blocks)
    scratch = []
    if nk > 1:
        scratch.append(pltpu.VMEM(acc_shape, F32))
        nbytes += _nbytes(acc_shape, F32)
    res = pl.pallas_call(
        body, name=name, grid=grid,
        in_specs=[a_spec, b_spec, *extra_specs], out_specs=list(out_specs), out_shape=list(out_shapes),
        scratch_shapes=scratch,
        compiler_params=_params(("parallel", "parallel", "arbitrary"), nbytes),
    )(a, b, *extras)
    return res


def _mm_specs(mode, tm, tn, tk, b_block=None, b_map=None):
    if mode == "tn":
        a_spec = pl.BlockSpec((tk, tm), lambda i, j, k: (k, i))
    else:
        a_spec = pl.BlockSpec((tm, tk), lambda i, j, k: (i, k))
    if b_block is not None:
        b_spec = pl.BlockSpec(b_block, b_map)
    elif mode == "nt":
        b_spec = pl.BlockSpec((tn, tk), lambda i, j, k: (j, k))
    else:
        b_spec = pl.BlockSpec((tk, tn), lambda i, j, k: (k, j))
    return a_spec, b_spec


def _tile(n, want):
    if n <= want:
        return n
    t = (want // 128) * 128
    while t > 128 and n % t:
        t -= 128
    assert n % t == 0, (n, want)
    return t


def _row_tile(s):
    return 256 if s % 256 == 0 else s


def _norm_fwd(name, x, g):
    s, d = x.shape
    tm = _row_tile(s)

    def body(x_ref, g_ref, h_ref, r_ref):
        xv = x_ref[...]
        r = lax.rsqrt(jnp.mean(xv * xv, axis=-1, keepdims=True) + EPS)
        h_ref[...] = ((xv * r) * g_ref[...]).astype(BF16)
        r_ref[...] = r

    row = pl.BlockSpec((tm, d), lambda i: (i, 0))
    return pl.pallas_call(
        body, name=name, grid=(s // tm,),
        in_specs=[row, pl.BlockSpec((1, d), lambda i: (0, 0))],
        out_specs=[row, pl.BlockSpec((tm, 1), lambda i: (i, 0))],
        out_shape=[jax.ShapeDtypeStruct((s, d), BF16), jax.ShapeDtypeStruct((s, 1), F32)],
        compiler_params=_params(("parallel",), tm * d * 6),
    )(x, g)


def _norm_bwd(name, dh, x, r, g, dres):
    s, d = x.shape
    tm = _row_tile(s)

    def body(dh_ref, x_ref, r_ref, g_ref, dres_ref, dx_ref, dxb_ref, dg_ref):
        rv = r_ref[...]
        xn = x_ref[...] * rv
        dhv = dh_ref[...]
        dxn = dhv * g_ref[...]
        dx = dres_ref[...] + rv * (dxn - xn * jnp.mean(dxn * xn, axis=-1, keepdims=True))
        dx_ref[...] = dx
        dxb_ref[...] = dx.astype(BF16)
        part = jnp.sum(dhv * xn, axis=0, keepdims=True)

        @pl.when(pl.program_id(0) == 0)
        def _():
            dg_ref[...] = part

        @pl.when(pl.program_id(0) > 0)
        def _():
            dg_ref[...] += part

    row = pl.BlockSpec((tm, d), lambda i: (i, 0))
    vec = pl.BlockSpec((1, d), lambda i: (0, 0))
    return pl.pallas_call(
        body, name=name, grid=(s // tm,),
        in_specs=[row, row, pl.BlockSpec((tm, 1), lambda i: (i, 0)), vec, row],
        out_specs=[row, row, vec],
        out_shape=[jax.ShapeDtypeStruct((s, d), F32), jax.ShapeDtypeStruct((s, d), BF16),
                   jax.ShapeDtypeStruct((1, d), F32)],
        compiler_params=_params(("arbitrary",), tm * d * 18),
    )(dh, x, r, g, dres)


def _loss_head(x3, target, g):
    s, d = x3.shape
    tm = _row_tile(s)

    def body(x_ref, t_ref, g_ref, dx_ref, dxb_ref, loss_ref, dg_ref):
        xv = x_ref[...]
        gv = g_ref[...]
        r = lax.rsqrt(jnp.mean(xv * xv, axis=-1, keepdims=True) + EPS)
        xn = xv * r
        err = xn * gv - t_ref[...]
        loss = 0.5 * jnp.sum(jnp.mean(err * err, axis=-1, keepdims=True), axis=0, keepdims=True)
        dy = err / d
        dxn = dy * gv
        dx = r * (dxn - xn * jnp.mean(dxn * xn, axis=-1, keepdims=True))
        dx_ref[...] = dx
        dxb_ref[...] = dx.astype(BF16)
        dg = jnp.sum(dy * xn, axis=0, keepdims=True)
        loss_row = jnp.broadcast_to(loss, (1, 128))

        @pl.when(pl.program_id(0) == 0)
        def _():
            dg_ref[...] = dg
            loss_ref[...] = loss_row

        @pl.when(pl.program_id(0) > 0)
        def _():
            dg_ref[...] += dg
            loss_ref[...] += loss_row

    row = pl.BlockSpec((tm, d), lambda i: (i, 0))
    vec = pl.BlockSpec((1, d), lambda i: (0, 0))
    return pl.pallas_call(
        body, name="loss_head", grid=(s // tm,),
        in_specs=[row, row, vec],
        out_specs=[row, row, pl.BlockSpec((1, 128), lambda i: (0, 0)), vec],
        out_shape=[jax.ShapeDtypeStruct((s, d), F32), jax.ShapeDtypeStruct((s, d), BF16),
                   jax.ShapeDtypeStruct((1, 128), F32), jax.ShapeDtypeStruct((1, d), F32)],
        compiler_params=_params(("arbitrary",), tm * d * 14),
    )(x3, target, g)


def _low_lanes(rows):
    return lax.broadcasted_iota(jnp.int32, (rows, PAIR), 1) < HEAD_DIM


def _mix_fwd(oa, obs, lses, ga, gb):
    s, qa = oa.shape
    qb = obs[0].shape[1]
    tm = _row_tile(s)

    def body(oa_ref, o1_ref, o2_ref, o3_ref, l1_ref, l2_ref, l3_ref, ga_ref, gb_ref,
             mix_ref, ob_ref, lse_ref, ra_ref, rb_ref):
        oav = oa_ref[...]
        ra = lax.rsqrt(jnp.mean(oav * oav, axis=-1, keepdims=True) + EPS)
        ra_ref[...] = ra
        mix_ref[:, 0:qa] = ((oav * ra) * ga_ref[...]).astype(BF16)
        l1, l2, l3 = l1_ref[...], l2_ref[...], l3_ref[...]
        mx = jnp.maximum(jnp.maximum(l1, l2), l3)
        e1, e2, e3 = jnp.exp(l1 - mx), jnp.exp(l2 - mx), jnp.exp(l3 - mx)
        tot = e1 + e2 + e3
        lse_ref[...] = mx + jnp.log(tot)
        ws = (e1 / tot, e2 / tot, e3 / tot)
        low = _low_lanes(tm)
        ssq = jnp.zeros((tm, 1), F32)
        for i in range(qb // PAIR):
            sl = slice(i * PAIR, (i + 1) * PAIR)
            acc = jnp.zeros((tm, PAIR), F32)
            for w, o_ref in zip(ws, (o1_ref, o2_ref, o3_ref)):
                wexp = jnp.where(low, w[:, 2 * i:2 * i + 1], w[:, 2 * i + 1:2 * i + 2])
                acc = acc + wexp * o_ref[:, sl]
            ob_ref[:, sl] = acc
            ssq = ssq + jnp.sum(acc * acc, axis=-1, keepdims=True)
        rb = lax.rsqrt(ssq / qb + EPS)
        rb_ref[...] = rb
        mix_ref[:, qa:qa + qb] = ((ob_ref[...] * rb) * gb_ref[...]).astype(BF16)

    def row(w):
        return pl.BlockSpec((tm, w), lambda i: (i, 0))

    def vec(w):
        return pl.BlockSpec((1, w), lambda i: (0, 0))

    return pl.pallas_call(
        body, name="mix_fwd", grid=(s // tm,),
        in_specs=[row(qa), row(qb), row(qb), row(qb), row(128), row(128), row(128), vec(qa), vec(qb)],
        out_specs=[row(qa + qb), row(qb), row(128), row(1), row(1)],
        out_shape=[jax.ShapeDtypeStruct((s, qa + qb), BF16), jax.ShapeDtypeStruct((s, qb), F32),
                   jax.ShapeDtypeStruct((s, 128), F32), jax.ShapeDtypeStruct((s, 1), F32),
                   jax.ShapeDtypeStruct((s, 1), F32)],
        compiler_params=_params(("parallel",), tm * (qa + 4 * qb) * 4 + tm * (qa + qb) * 2 + tm * 2048),
    )(oa, *obs, *lses, ga, gb)


def _head_rowsums(prod, rows):
    low = _low_lanes(rows)
    lane = lax.broadcasted_iota(jnp.int32, (rows, 128), 1)
    out = jnp.zeros((rows, 128), F32)
    for i in range(prod.shape[1] // PAIR):
        tile = prod[:, i * PAIR:(i + 1) * PAIR]
        lo = jnp.sum(jnp.where(low, tile, 0.0), axis=-1, keepdims=True)
        hi = jnp.sum(jnp.where(low, 0.0, tile), axis=-1, keepdims=True)
        out = jnp.where(lane == 2 * i, lo, out)
        out = jnp.where(lane == 2 * i + 1, hi, out)
    return out


def _mix_bwd(dmix, oa, ob, ra, rb, ga, gb):
    s, qa = oa.shape
    qb = ob.shape[1]
    tm = _row_tile(s)

    def one(dy, o, r, g):
        xn = o * r
        dxn = dy * g
        do = r * (dxn - xn * jnp.mean(dxn * xn, axis=-1, keepdims=True))
        return do, jnp.sum(dy * xn, axis=0, keepdims=True), _head_rowsums(do * o, tm)

    def body(dmix_ref, oa_ref, ob_ref, ra_ref, rb_ref, ga_ref, gb_ref,
             doa_ref, dob_ref, dla_ref, dlb_ref, dga_ref, dgb_ref):
        doa, dga, dla = one(dmix_ref[:, 0:qa], oa_ref[...], ra_ref[...], ga_ref[...])
        dob, dgb, dlb = one(dmix_ref[:, qa:qa + qb], ob_ref[...], rb_ref[...], gb_ref[...])
        doa_ref[...] = doa.astype(BF16)
        dob_ref[...] = dob.astype(BF16)
        dla_ref[...] = dla
        dlb_ref[...] = dlb

        @pl.when(pl.program_id(0) == 0)
        def _():
            dga_ref[...] = dga
            dgb_ref[...] = dgb

        @pl.when(pl.program_id(0) > 0)
        def _():
            dga_ref[...] += dga
            dgb_ref[...] += dgb

    def row(w):
        return pl.BlockSpec((tm, w), lambda i: (i, 0))

    def vec(w):
        return pl.BlockSpec((1, w), lambda i: (0, 0))

    return pl.pallas_call(
        body, name="mix_bwd", grid=(s // tm,),
        in_specs=[row(qa + qb), row(qa), row(qb), row(1), row(1), vec(qa), vec(qb)],
        out_specs=[row(qa), row(qb), row(128), row(128), vec(qa), vec(qb)],
        out_shape=[jax.ShapeDtypeStruct((s, qa), BF16), jax.ShapeDtypeStruct((s, qb), BF16),
                   jax.ShapeDtypeStruct((s, 128), F32), jax.ShapeDtypeStruct((s, 128), F32),
                   jax.ShapeDtypeStruct((1, qa), F32), jax.ShapeDtypeStruct((1, qb), F32)],
        compiler_params=_params(("arbitrary",), tm * (qa + qb) * 12),
    )(dmix, oa, ob, ra, rb, ga, gb)


def _assemble_dproj(dqa, dkva, dqs, dks, dvs):
    s, qa = dqa.shape
    kva = dkva.shape[1]
    qb = dqs[0].shape[1]
    width = qa + kva + 3 * qb
    tm = _row_tile(s)

    def body(dqa_ref, dkva_ref, q1, q2, q3, k1, k2, k3, v1, v2, v3, dp_ref, db_ref):
        parts = [dqa_ref[...], dkva_ref[...], q1[...] + q2[...] + q3[...],
                 k1[...] + k2[...] + k3[...], v1[...] + v2[...] + v3[...]]
        off = 0
        for p in parts:
            w = p.shape[1]
            dp_ref[:, off:off + w] = p.astype(BF16)
            col = jnp.sum(p, axis=0, keepdims=True)

            @pl.when(pl.program_id(0) == 0)
            def _(col=col, off=off, w=w):
                db_ref[:, off:off + w] = col

            @pl.when(pl.program_id(0) > 0)
            def _(col=col, off=off, w=w):
                db_ref[:, off:off + w] += col

            off += w

    def row(w):
        return pl.BlockSpec((tm, w), lambda i: (i, 0))

    return pl.pallas_call(
        body, name="assemble_dproj", grid=(s // tm,),
        in_specs=[row(qa), row(kva)] + [row(qb)] * 9,
        out_specs=[row(width), pl.BlockSpec((1, width), lambda i: (0, 0))],
        out_shape=[jax.ShapeDtypeStruct((s, width), BF16), jax.ShapeDtypeStruct((1, width), F32)],
        compiler_params=_params(("arbitrary",), tm * (qa + kva + 9 * qb) * 4 + tm * width * 2),
    )(dqa, dkva, *dqs, *dks, *dvs)


def _band_masks(max_steps, dil, first_block):
    qi = lax.broadcasted_iota(jnp.int32, (BLOCK, 2 * BLOCK), 0)
    kj = lax.broadcasted_iota(jnp.int32, (BLOCK, 2 * BLOCK), 1)
    steps = qi + BLOCK - kj
    valid = (steps >= 0) & (steps <= max_steps) & ((kj >= BLOCK) | jnp.logical_not(first_block))
    dist = (steps * dil).astype(F32)
    return valid, dist


def _swap_halves(t):
    return pltpu.roll(t, HEAD_DIM, 1)


def _dup_group(t_bf16, group):
    t = t_bf16.astype(F32)
    low = lax.broadcasted_iota(jnp.int32, t.shape, 1) < HEAD_DIM
    keep = low if group == 0 else jnp.logical_not(low)
    return jnp.where(keep, t, _swap_halves(t)).astype(BF16)


def _attn_views(arrs, dil):
    return [a.reshape(a.shape[0] // dil, dil * a.shape[1]) for a in arrs]


def _attn_fwd(name, q, kv, *, dil, max_steps, slopes, sinks=None):
    grouped = sinks is not None
    s, w = q.shape
    n_pairs = w // PAIR
    nb = s // dil // BLOCK
    heads_per_group = 2 * n_pairs // N_KV_GROUPS

    def body(*refs):
        if grouped:
            sink_ref, q_ref, kvp_ref, kvc_ref, o_ref, lse_ref = refs
        else:
            q_ref, kp_ref, kc_ref, vp_ref, vc_ref, o_ref, lse_ref = refs
        n = pl.program_id(1)
        valid, dist = _band_masks(max_steps, dil, n == 0)
        low = _low_lanes(BLOCK)
        lane = lax.broadcasted_iota(jnp.int32, (BLOCK, 128), 1)
        lse_acc = jnp.zeros((BLOCK, 128), F32)
        if grouped:
            kv_all = jnp.concatenate([kvp_ref[...], kvc_ref[...]], axis=0)
            k_dup = [_dup_group(kv_all[:, 0:PAIR], g) for g in range(N_KV_GROUPS)]
            v_dup = [_dup_group(kv_all[:, PAIR:2 * PAIR], g) for g in range(N_KV_GROUPS)]
        for i in range(n_pairs):
            sl = slice(i * PAIR, (i + 1) * PAIR)
            q2 = q_ref[:, sl]
            if grouped:
                kk, vv = k_dup[2 * i // heads_per_group], v_dup[2 * i // heads_per_group]
            else:
                kk = jnp.concatenate([kp_ref[:, sl], kc_ref[:, sl]], axis=0)
                vv = jnp.concatenate([vp_ref[:, sl], vc_ref[:, sl]], axis=0)
            outs = []
            for half in (0, 1):
                h = 2 * i + half
                qm = jnp.where(low if half == 0 else jnp.logical_not(low), q2, jnp.zeros_like(q2))
                sc = lax.dot_general(qm, kk, (((1,), (1,)), ((), ())), preferred_element_type=F32)
                sc = jnp.where(valid, sc * ATT_SCALE - slopes[h] * dist, NEG_INF)
                m = jnp.max(sc, axis=-1, keepdims=True)
                if grouped:
                    m = jnp.maximum(m, sink_ref[h])
                p = jnp.exp(sc - m)
                den = jnp.sum(p, axis=-1, keepdims=True)
                if grouped:
                    den = den + jnp.exp(sink_ref[h] - m)
                o = jnp.dot(p.astype(BF16), vv, preferred_element_type=F32)
                outs.append(o / den)
                lse_acc = jnp.where(lane == h, m + jnp.log(den), lse_acc)
            o_ref[:, sl] = jnp.where(low, outs[0], outs[1])
        lse_ref[...] = lse_acc

    def cur(width):
        return pl.BlockSpec((BLOCK, width), lambda r, n: (n, r))

    def prev(width):
        return pl.BlockSpec((BLOCK, width), lambda r, n: (jnp.maximum(n - 1, 0), r))

    if grouped:
        kvw = kv.shape[1]
        operands = [sinks, q, kv, kv]
        in_specs = [SMEM_SPEC, cur(w), prev(kvw), cur(kvw)]
    else:
        qv, kview, vview = _attn_views([q, kv[0], kv[1]], dil)
        operands = [qv, kview, kview, vview, vview]
        in_specs = [cur(w), prev(w), cur(w), prev(w), cur(w)]
    o, lse = pl.pallas_call(
        body, name=name, grid=(dil, nb), in_specs=in_specs,
        out_specs=[cur(w), cur(128)],
        out_shape=[jax.ShapeDtypeStruct((s // dil, dil * w), F32),
                   jax.ShapeDtypeStruct((s // dil, dil * 128), F32)],
        compiler_params=_params(("parallel", "parallel"), BLOCK * w * 16),
    )(*operands)
    return o.reshape(s, w), lse.reshape(s, 128)


def _attn_bwd(name, q, kv, do, lse, delta, *, dil, max_steps, slopes, sinks=None):
    grouped = sinks is not None
    s, w = q.shape
    n_pairs = w // PAIR
    nb = s // dil // BLOCK
    heads_per_group = 2 * n_pairs // N_KV_GROUPS
    pairs_per_group = n_pairs // N_KV_GROUPS

    def body(*refs):
        if grouped:
            (sink_ref, q_ref, kvp_ref, kvc_ref, do_ref, lse_ref, dl_ref,
             dq_ref, dkv_ref, dsink_ref, acc_ref) = refs
        else:
            (q_ref, kp_ref, kc_ref, vp_ref, vc_ref, do_ref, lse_ref, dl_ref,
             dq_ref, dk_ref, dv_ref, acck_ref, accv_ref) = refs
        n = pl.program_id(1)

        @pl.when(n == 0)
        def _():
            if grouped:
                acc_ref[...] = jnp.zeros_like(acc_ref)

                @pl.when(pl.program_id(0) == 0)
                def _():
                    dsink_ref[...] = jnp.zeros_like(dsink_ref)
            else:
                acck_ref[...] = jnp.zeros_like(acck_ref)
                accv_ref[...] = jnp.zeros_like(accv_ref)

        @pl.when(n == nb)
        def _():
            if grouped:
                dkv_ref[...] = acc_ref[...]
            else:
                dk_ref[...] = acck_ref[...]
                dv_ref[...] = accv_ref[...]

        @pl.when(n < nb)
        def _():
            valid, dist = _band_masks(max_steps, dil, n == 0)
            low = _low_lanes(BLOCK)
            low_kv = _low_lanes(2 * BLOCK)
            lane1 = lax.broadcasted_iota(jnp.int32, (1, 128), 1)
            if grouped:
                kv_all = jnp.concatenate([kvp_ref[...], kvc_ref[...]], axis=0)
                k_dup = [_dup_group(kv_all[:, 0:PAIR], g) for g in range(N_KV_GROUPS)]
                v_dup = [_dup_group(kv_all[:, PAIR:2 * PAIR], g) for g in range(N_KV_GROUPS)]
                dk_grp = [jnp.zeros((2 * BLOCK, PAIR), F32) for _ in range(N_KV_GROUPS)]
                dv_grp = [jnp.zeros((2 * BLOCK, PAIR), F32) for _ in range(N_KV_GROUPS)]
                dsink = jnp.zeros((1, 128), F32)
            for i in range(n_pairs):
                sl = slice(i * PAIR, (i + 1) * PAIR)
                q2 = q_ref[:, sl]
                do2 = do_ref[:, sl]
                if grouped:
                    grp = 2 * i // heads_per_group
                    kk, vv = k_dup[grp], v_dup[grp]
                else:
                    kk = jnp.concatenate([kp_ref[:, sl], kc_ref[:, sl]], axis=0)
                    vv = jnp.concatenate([vp_ref[:, sl], vc_ref[:, sl]], axis=0)
                dkk = jnp.zeros((2 * BLOCK, PAIR), F32)
                dvv = jnp.zeros((2 * BLOCK, PAIR), F32)
                dqs = []
                for half in (0, 1):
                    h = 2 * i + half
                    keep = low if half == 0 else jnp.logical_not(low)
                    qm = jnp.where(keep, q2, jnp.zeros_like(q2))
                    dom = jnp.where(keep, do2, jnp.zeros_like(do2))
                    lse_h = lse_ref[:, h:h + 1]
                    dl_h = dl_ref[:, h:h + 1]
                    sc = lax.dot_general(qm, kk, (((1,), (1,)), ((), ())), preferred_element_type=F32)
                    sc = jnp.where(valid, sc * ATT_SCALE - slopes[h] * dist, NEG_INF)
                    p = jnp.exp(sc - lse_h)
                    dp = lax.dot_general(dom, vv, (((1,), (1,)), ((), ())), preferred_element_type=F32)
                    ds = (p * (dp - dl_h)).astype(BF16)
                    dqs.append(jnp.dot(ds, kk, preferred_element_type=F32))
                    dkk = dkk + lax.dot_general(ds, qm, (((0,), (0,)), ((), ())),
                                                preferred_element_type=F32)
                    dvv = dvv + lax.dot_general(p.astype(BF16), dom, (((0,), (0,)), ((), ())),
                                                preferred_element_type=F32)
                    if grouped:
                        contrib = -jnp.sum(jnp.exp(sink_ref[h] - lse_h) * dl_h, axis=0, keepdims=True)
                        dsink = jnp.where(lane1 == h, dsink + contrib, dsink)
                dq_ref[:, sl] = jnp.where(low, dqs[0], dqs[1]) * ATT_SCALE
                if grouped:
                    dk_grp[grp] = dk_grp[grp] + dkk
                    dv_grp[grp] = dv_grp[grp] + dvv
                else:
                    dkk = dkk * ATT_SCALE
                    dk_ref[:, sl] = acck_ref[:, sl] + dkk[0:BLOCK]
                    acck_ref[:, sl] = dkk[BLOCK:2 * BLOCK]
                    dv_ref[:, sl] = accv_ref[:, sl] + dvv[0:BLOCK]
                    accv_ref[:, sl] = dvv[BLOCK:2 * BLOCK]
            if grouped:
                folded = [t + _swap_halves(t) for t in dk_grp + dv_grp]
                dk_tile = jnp.where(low_kv, folded[0], folded[1]) * ATT_SCALE
                dv_tile = jnp.where(low_kv, folded[2], folded[3])
                part = jnp.concatenate([dk_tile, dv_tile], axis=1)
                dkv_ref[...] = acc_ref[...] + part[0:BLOCK]
                acc_ref[...] = part[BLOCK:2 * BLOCK]
                dsink_ref[...] += dsink

    last = nb - 1

    def cur(width):
        return pl.BlockSpec((BLOCK, width), lambda r, n: (jnp.minimum(n, last), r))

    def prev(width):
        return pl.BlockSpec((BLOCK, width), lambda r, n: (jnp.maximum(jnp.minimum(n, last) - 1, 0), r))

    def done(width):
        return pl.BlockSpec((BLOCK, width), lambda r, n: (jnp.maximum(n - 1, 0), r))

    lse_v, dl_v = _attn_views([lse, delta], dil)
    if grouped:
        assert pairs_per_group * N_KV_GROUPS == n_pairs and heads_per_group % 2 == 0
        kvw = kv.shape[1]
        operands = [sinks, q, kv, kv, do, lse_v, dl_v]
        in_specs = [SMEM_SPEC, cur(w), prev(kvw), cur(kvw), cur(w), cur(128), cur(128)]
        out_specs = [cur(w), done(kvw), pl.BlockSpec((1, 128), lambda r, n: (0, 0))]
        out_shape = [jax.ShapeDtypeStruct((s, w), F32), jax.ShapeDtypeStruct((s, kvw), F32),
                     jax.ShapeDtypeStruct((1, 128), F32)]
        scratch = [pltpu.VMEM((BLOCK, kvw), F32)]
    else:
        qv, kview, vview, dov = _attn_views([q, kv[0], kv[1], do], dil)
        operands = [qv, kview, kview, vview, vview, dov, lse_v, dl_v]
        in_specs = [cur(w), prev(w), cur(w), prev(w), cur(w), cur(w), cur(128), cur(128)]
        out_specs = [cur(w), done(w), done(w)]
        out_shape = [jax.ShapeDtypeStruct((s // dil, dil * w), F32)] * 3
        scratch = [pltpu.VMEM((BLOCK, w), F32), pltpu.VMEM((BLOCK, w), F32)]
    res = pl.pallas_call(
        body, name=name, grid=(dil, nb + 1), in_specs=in_specs, out_specs=out_specs,
        out_shape=out_shape, scratch_shapes=scratch,
        compiler_params=_params(("arbitrary", "arbitrary"), BLOCK * w * 32),
    )(*operands)
    if grouped:
        return res
    return [t.reshape(s, w) for t in res]


def _adamw(name, w, g, m, v):
    rows, cols = w.shape
    tm = 256 if rows % 256 == 0 else rows

    def body(w_ref, g_ref, m_ref, v_ref, d_ref, nm_ref, nv_ref):
        gv = g_ref[...]
        mn = ADAM_B1 * m_ref[...] + (1.0 - ADAM_B1) * gv
        vn = ADAM_B2 * v_ref[...] + (1.0 - ADAM_B2) * (gv * gv)
        m_hat = mn / (1.0 - ADAM_B1 ** ADAM_STEP)
        v_hat = vn / (1.0 - ADAM_B2 ** ADAM_STEP)
        d_ref[...] = -ADAM_LR * (m_hat / (jnp.sqrt(v_hat) + ADAM_EPS) + ADAM_WD * w_ref[...])
        nm_ref[...] = mn
        nv_ref[...] = vn

    spec = pl.BlockSpec((tm, cols), lambda i: (i, 0))
    return pl.pallas_call(
        body, name=name, grid=(rows // tm,), in_specs=[spec] * 4, out_specs=[spec] * 3,
        out_shape=[jax.ShapeDtypeStruct(w.shape, F32)] * 3,
        compiler_params=_params(("parallel",), tm * cols * 28),
    )(w, g, m, v)


def _mesh_position():
    return lax.axis_index("x"), lax.axis_index("y"), lax.axis_index("c")


def _other_chips(x, y):
    return [(1 - x, y), (x, 1 - y), (1 - x, 1 - y)]


def _gather_weights(shards):
    n_w = len(shards)
    halves = [s.shape[0] // 2 for s in shards]

    def body(*refs):
        ins, outs = refs[:n_w], refs[n_w:2 * n_w]
        local_sems, send_sems, recv_sems, fsend_sems, frecv_sems = refs[2 * n_w:]
        x, y, c = _mesh_position()
        chip = 2 * x + y
        sibling = (x, y, 1 - c)
        others = _other_chips(x, y)

        def region(w, owner_chip, half):
            start = owner_chip * (2 * halves[w]) + half * halves[w]
            return outs[w].at[pl.ds(pl.multiple_of(start, 16), halves[w])]

        def remote(src, dst, ssem, rsem, dev):
            return pltpu.make_async_remote_copy(src_ref=src, dst_ref=dst, send_sem=ssem, recv_sem=rsem,
                                                device_id=dev, device_id_type=MESH)

        local = []
        for w in range(n_w):
            cp = pltpu.make_async_copy(ins[w], outs[w].at[pl.ds(pl.multiple_of(chip * 2 * halves[w], 16),
                                                                2 * halves[w])], local_sems.at[w])
            cp.start()
            local.append(cp)
        sends = []
        for w in range(n_w):
            mine = ins[w].at[pl.ds(pl.multiple_of(c * halves[w], 16), halves[w])]
            for k, (px, py) in enumerate(others):
                cp = remote(mine, region(w, chip, c), send_sems.at[w, k], recv_sems.at[w, k], (px, py, c))
                cp.start()
                sends.append(cp)
        for k, (px, py) in enumerate(others):
            for w in range(n_w):
                landed = region(w, 2 * px + py, c)
                remote(landed, landed, send_sems.at[w, k], recv_sems.at[w, k], (px, py, c)).wait_recv()
                cp = remote(landed, landed, fsend_sems.at[w, k], frecv_sems.at[w, k], sibling)
                cp.start()
                sends.append(cp)
        for k, (px, py) in enumerate(others):
            for w in range(n_w):
                passed = region(w, 2 * px + py, 1 - c)
                remote(passed, passed, fsend_sems.at[w, k], frecv_sems.at[w, k], sibling).wait_recv()
        for cp in sends:
            cp.wait_send()
        for cp in local:
            cp.wait()

    return pl.pallas_call(
        body, name="gather_weights",
        in_specs=[HBM_SPEC] * n_w, out_specs=[HBM_SPEC] * n_w,
        out_shape=[jax.ShapeDtypeStruct((N_CHIPS * s.shape[0], s.shape[1]), s.dtype) for s in shards],
        scratch_shapes=[pltpu.SemaphoreType.DMA((n_w,))] + [pltpu.SemaphoreType.DMA((n_w, 3))] * 4,
    )(*shards)


REDUCE_CHUNK_ROWS = 256


def _reduce_gradients(grads, small):
    n_w = len(grads)
    rows = [g.shape[0] // N_CHIPS for g in grads]
    halves = [r // 2 for r in rows]
    cols = grads[0].shape[1]
    wire = grads[0].dtype
    ch = REDUCE_CHUNK_ROWS
    for h in halves:
        assert h % 16 == 0
    rows_s = small.shape[0]

    def body(*refs):
        g_in = refs[:n_w]
        small_ref = refs[n_w]
        outs = refs[n_w + 1:2 * n_w + 1]
        small_out = refs[2 * n_w + 1]
        from_sib = refs[2 * n_w + 2:3 * n_w + 2]
        chip_sum = refs[3 * n_w + 2:4 * n_w + 2]
        from_chips = refs[4 * n_w + 2:5 * n_w + 2]
        (small_all, buf_a, buf_b, buf_o, sib_send, sib_recv, chip_send, chip_recv,
         fin_send, fin_recv, small_send, small_recv, io_sem) = refs[5 * n_w + 2:]
        x, y, c = _mesh_position()
        chip = 2 * x + y
        me = 4 * x + 2 * y + c
        sibling = (x, y, 1 - c)
        others = _other_chips(x, y)

        def remote(src, dst, ssem, rsem, dev):
            return pltpu.make_async_remote_copy(src_ref=src, dst_ref=dst, send_sem=ssem, recv_sem=rsem,
                                                device_id=dev, device_id_type=MESH)

        def part(w, owner_chip, half):
            start = owner_chip * rows[w] + half * halves[w]
            return g_in[w].at[pl.ds(pl.multiple_of(start, 16), halves[w])]

        pending = []
        small_all[me] = small_ref[...]
        for j in range(N_DEV - 1):
            peer = (me + 1 + j) % N_DEV
            cp = remote(small_all.at[me], small_all.at[me], small_send.at[j], small_recv.at[j],
                        (peer // 4, (peer // 2) % 2, peer % 2))
            cp.start()
            pending.append(cp)

        for w in range(n_w):
            for k in range(N_CHIPS):
                cp = remote(part(w, k, 1 - c), from_sib[w].at[k], sib_send.at[w, k], sib_recv.at[w, k], sibling)
                cp.start()
                pending.append(cp)

        def add_stream(w, srcs, dst, n_rows):
            def chunk(start, size):
                total = None
                for i, src in enumerate(srcs):
                    buf = buf_a if i % 2 == 0 else buf_b
                    cp = pltpu.make_async_copy(src.at[pl.ds(start, size)], buf.at[pl.ds(0, size)], io_sem)
                    cp.start()
                    cp.wait()
                    val = buf[pl.ds(0, size), :].astype(F32)
                    total = val if total is None else total + val
                return total

            n_full = n_rows // ch
            rem = n_rows - n_full * ch

            def store(total, start, size):
                if dst.dtype == F32:
                    buf_o[pl.ds(0, size), :] = total
                    cp = pltpu.make_async_copy(buf_o.at[pl.ds(0, size)], dst.at[pl.ds(start, size)], io_sem)
                else:
                    buf_a[pl.ds(0, size), :] = total.astype(buf_a.dtype)
                    cp = pltpu.make_async_copy(buf_a.at[pl.ds(0, size)], dst.at[pl.ds(start, size)], io_sem)
                cp.start()
                cp.wait()

            def loop_body(i, carry):
                start = pl.multiple_of(i * ch, ch)
                store(chunk(start, ch), start, ch)
                return carry

            lax.fori_loop(0, n_full, loop_body, 0)
            if rem:
                store(chunk(n_full * ch, rem), n_full * ch, rem)

        order = [2, 0, 1]
        for w in range(n_w):
            for k in range(N_CHIPS):
                remote(part(w, k, 1 - c), from_sib[w].at[k], sib_send.at[w, k], sib_recv.at[w, k],
                       sibling).wait_recv()
        for k in order:
            px, py = others[k]
            owner = 2 * px + py
            for w in range(n_w):
                add_stream(w, [part(w, owner, c), from_sib[w].at[owner]], chip_sum[w].at[owner], halves[w])
                cp = remote(chip_sum[w].at[owner], from_chips[w].at[k], chip_send.at[w, k],
                            chip_recv.at[w, k], (px, py, c))
                cp.start()
                pending.append(cp)
        for w in range(n_w):
            add_stream(w, [part(w, chip, c), from_sib[w].at[chip]], chip_sum[w].at[chip], halves[w])

        for w in range(n_w):
            for k in range(3):
                px, py = others[k]
                remote(chip_sum[w].at[chip], from_chips[w].at[k], chip_send.at[w, k], chip_recv.at[w, k],
                       (px, py, c)).wait_recv()
            mine = outs[w].at[pl.ds(pl.multiple_of(c * halves[w], 16), halves[w])]
            add_stream(w, [chip_sum[w].at[chip], from_chips[w].at[0], from_chips[w].at[1],
                           from_chips[w].at[2]], mine, halves[w])
            cp = remote(mine, mine, fin_send.at[w], fin_recv.at[w], sibling)
            cp.start()
            pending.append(cp)
        for w in range(n_w):
            theirs = outs[w].at[pl.ds(pl.multiple_of((1 - c) * halves[w], 16), halves[w])]
            remote(theirs, theirs, fin_send.at[w], fin_recv.at[w], sibling).wait_recv()

        for j in range(N_DEV - 1):
            peer = (me + N_DEV - 1 - j) % N_DEV
            remote(small_all.at[peer], small_all.at[peer], small_send.at[j], small_recv.at[j],
                   sibling).wait_recv()
        total = small_all[0]
        for d in range(1, N_DEV):
            total = total + small_all[d]
        small_out[...] = total
        for cp in pending:
            cp.wait_send()

    hbm_scratch = ([jax.ShapeDtypeStruct((N_CHIPS, h, cols), wire) for h in halves] * 2
                   + [jax.ShapeDtypeStruct((3, h, cols), wire) for h in halves])
    out_shape = ([jax.ShapeDtypeStruct((r, cols), F32) for r in rows]
                 + [jax.ShapeDtypeStruct((rows_s, 128), F32)] + hbm_scratch)
    res = pl.pallas_call(
        body, name="reduce_gradients",
        in_specs=[HBM_SPEC] * n_w + [VMEM_SPEC],
        out_specs=[HBM_SPEC] * n_w + [VMEM_SPEC] + [HBM_SPEC] * (3 * n_w),
        out_shape=out_shape,
        scratch_shapes=[
            pltpu.VMEM((N_DEV, rows_s, 128), F32),
            pltpu.VMEM((ch, cols), wire), pltpu.VMEM((ch, cols), wire), pltpu.VMEM((ch, cols), F32),
            pltpu.SemaphoreType.DMA((n_w, N_CHIPS)), pltpu.SemaphoreType.DMA((n_w, N_CHIPS)),
            pltpu.SemaphoreType.DMA((n_w, 3)), pltpu.SemaphoreType.DMA((n_w, 3)),
            pltpu.SemaphoreType.DMA((n_w,)), pltpu.SemaphoreType.DMA((n_w,)),
            pltpu.SemaphoreType.DMA((N_DEV - 1,)), pltpu.SemaphoreType.DMA((N_DEV - 1,)),
            pltpu.SemaphoreType.DMA,
        ],
        compiler_params=pltpu.CompilerParams(vmem_limit_bytes=VMEM_LIMIT_MIN),
    )(*grads, small)
    return res[:n_w], res[n_w]


def _pack_small(parts, rows):
    flat = jnp.concatenate([p.reshape(-1) for p in parts])
    flat = jnp.pad(flat, (0, rows * 128 - flat.shape[0]))
    return flat.reshape(rows, 128)


def _unpack_small(packed, shapes):
    flat = packed.reshape(-1)
    out, off = [], 0
    for shp in shapes:
        n = int(np.prod(shp))
        out.append(flat[off:off + n].reshape(shp))
        off += n
    return out


def kernel(x, g_attn, w_in, b_in, sinks_a, g_out_a, g_out_b, w_out, g_mlp, w_1, w_2, g_final, loss_target, m_g_attn, m_w_in, m_b_in, m_sinks_a, m_g_out_a, m_g_out_b, m_w_out, m_g_mlp, m_w_1, m_w_2, m_g_final, v_g_attn, v_w_in, v_b_in, v_sinks_a, v_g_out_a, v_g_out_b, v_w_out, v_g_mlp, v_w_1, v_w_2, v_g_final):
    s, d = x.shape[1], x.shape[2]
    d_in = b_in.shape[1]
    qa = g_out_a.shape[1]
    qb = g_out_b.shape[1]
    kva = 2 * N_KV_GROUPS * HEAD_DIM
    assert d_in == qa + kva + 3 * qb and qa + qb == w_out.shape[1] * N_CHIPS
    d_ff = w_1.shape[2] * N_CHIPS
    ff_shard = w_1.shape[2]
    in_shard = w_in.shape[2]
    n_heads_a, n_heads_b = qa // HEAD_DIM, qb // HEAD_DIM
    slopes_a, slopes_b = alibi_slopes(n_heads_a), alibi_slopes(n_heads_b)

    x2d = x[0]
    target = loss_target[0]

    shards = [w_in[0].T.astype(BF16), w_out[0].astype(BF16), w_1[0].astype(BF16), w_2[0].astype(BF16)]
    w_in_t, w_out_g, w_1_g, w_2_g = _gather_weights(shards)
    w_1_g = w_1_g.reshape(N_CHIPS, d, ff_shard)

    tm = _tile(s, 1024)

    h1, r1 = _norm_fwd("norm_attn", x2d, g_attn)

    def project(name, row_off, width):
        tn = _tile(width, 256)
        off = row_off // tn
        assert row_off % tn == 0
        a_spec, _ = _mm_specs("nt", tm, tn, d)
        b_spec = pl.BlockSpec((tn, d), lambda i, j, k: (j + off, 0))
        bias_spec = pl.BlockSpec((1, tn), lambda i, j, k: (0, j + off))
        return _matmul(name, h1, w_in_t, [b_in], mode="nt", grid=(s // tm, width // tn, 1),
                       a_spec=a_spec, b_spec=b_spec, extra_specs=[bias_spec],
                       out_shapes=[jax.ShapeDtypeStruct((s, width), BF16)],
                       out_specs=[pl.BlockSpec((tm, tn), lambda i, j, k: (i, j))],
                       epilogue=lambda acc, b: (acc + b,))[0]

    q_a = project("proj_qa", 0, qa)
    kv_a = project("proj_kva", qa, kva)
    q_b = project("proj_qb", qa + kva, qb)
    k_b = project("proj_kb", qa + kva + qb, qb)
    v_b = project("proj_vb", qa + kva + 2 * qb, qb)

    sinks = sinks_a.reshape(-1)
    o_a, lse_a = _attn_fwd("attn_a_fwd", q_a, kv_a, dil=1, max_steps=WINDOW_A - 1, slopes=slopes_a, sinks=sinks)
    o_bs, lse_bs = [], []
    for window, dil in DILATED_BRANCHES:
        o, l = _attn_fwd(f"attn_b{dil}_fwd", q_b, (k_b, v_b), dil=dil, max_steps=window // dil, slopes=slopes_b)
        o_bs.append(o)
        lse_bs.append(l)
    mix, o_b, lse_b, r_a, r_b = _mix_fwd(o_a, o_bs, lse_bs, g_out_a, g_out_b)

    tn = _tile(d, 512)
    a_spec, b_spec = _mm_specs("nn", tm, tn, d)
    tile_mn = pl.BlockSpec((tm, tn), lambda i, j, k: (i, j))
    x2 = _matmul("out_proj", mix, w_out_g, [x2d], mode="nn", grid=(s // tm, d // tn, 1),
                 a_spec=a_spec, b_spec=b_spec, extra_specs=[tile_mn],
                 out_shapes=[jax.ShapeDtypeStruct((s, d), F32)], out_specs=[tile_mn],
                 epilogue=lambda acc, res: (acc + res,))[0]

    h2, r2 = _norm_fwd("norm_mlp", x2, g_mlp)

    tn = _tile(ff_shard, 512)
    per = ff_shard // tn
    a_spec, _ = _mm_specs("nn", tm, tn, d)
    u = _matmul("mlp_up", h2, w_1_g, [], mode="nn", grid=(s // tm, d_ff // tn, 1),
                a_spec=a_spec, b_spec=pl.BlockSpec((None, d, tn), lambda i, j, k: (j // per, 0, j % per)),
                extra_specs=[], out_shapes=[jax.ShapeDtypeStruct((s, d_ff), BF16)], out_specs=[tile_mn],
                epilogue=lambda acc: (jnp.maximum(acc, 0.0),))[0]

    tn = _tile(d, 1024)
    tk = _tile(d_ff, 1024)
    a_spec, b_spec = _mm_specs("nn", tm, tn, tk)
    tile_mn = pl.BlockSpec((tm, tn), lambda i, j, k: (i, j))
    x3 = _matmul("mlp_down", u, w_2_g, [x2], mode="nn", grid=(s // tm, d // tn, d_ff // tk),
                 a_spec=a_spec, b_spec=b_spec, extra_specs=[tile_mn],
                 out_shapes=[jax.ShapeDtypeStruct((s, d), F32)], out_specs=[tile_mn],
                 prologue=lambda a: a * a, epilogue=lambda acc, res: (acc + res,), acc_shape=(tm, tn))[0]

    dx3, dx3b, loss_part, dg_final = _loss_head(x3, target, g_final.reshape(1, d))

    tn = _tile(d_ff, 512)
    a_spec, b_spec = _mm_specs("nt", tm, tn, d)
    tile_mn = pl.BlockSpec((tm, tn), lambda i, j, k: (i, j))
    dpre = _matmul("mlp_down_dx", dx3b, w_2_g, [u], mode="nt", grid=(s // tm, d_ff // tn, 1),
                   a_spec=a_spec, b_spec=b_spec, extra_specs=[tile_mn],
                   out_shapes=[jax.ShapeDtypeStruct((s, d_ff), BF16)], out_specs=[tile_mn],
                   epilogue=lambda acc, uu: (acc * (2.0 * uu.astype(F32)),))[0]

    wire = GRAD_WIRE_DTYPE
    tk_s = _tile(s, 512)
    tmw = _tile(d_ff, 1024)
    a_spec, b_spec = _mm_specs("tn", tmw, d, tk_s)
    dw_2 = _matmul("mlp_down_dw", u, dx3b, [], mode="tn", grid=(d_ff // tmw, 1, s // tk_s),
                   a_spec=a_spec, b_spec=b_spec, extra_specs=[],
                   out_shapes=[jax.ShapeDtypeStruct((d_ff, d), wire)],
                   out_specs=[pl.BlockSpec((tmw, d), lambda i, j, k: (i, j))],
                   prologue=lambda a: a * a, epilogue=lambda acc: (acc,), acc_shape=(tmw, d))[0]

    tn = _tile(d, 1024)
    tk = _tile(ff_shard, 1024)
    per = ff_shard // tk
    a_spec, _ = _mm_specs("nt", tm, tn, tk)
    tile_mn = pl.BlockSpec((tm, tn), lambda i, j, k: (i, j))
    dh2 = _matmul("mlp_up_dx", dpre, w_1_g, [], mode="nt", grid=(s // tm, d // tn, d_ff // tk),
                  a_spec=a_spec, b_spec=pl.BlockSpec((None, tn, tk), lambda i, j, k: (k // per, j, k % per)),
                  extra_specs=[], out_shapes=[jax.ShapeDtypeStruct((s, d), F32)], out_specs=[tile_mn],
                  epilogue=lambda acc: (acc,), acc_shape=(tm, tn))[0]

    tmw = _tile(d, 1024)
    tnw = _tile(ff_shard, 2048)
    per = ff_shard // tnw
    a_spec, b_spec = _mm_specs("tn", tmw, tnw, tk_s)
    dw_1 = _matmul("mlp_up_dw", h2, dpre, [], mode="tn", grid=(d // tmw, d_ff // tnw, s // tk_s),
                   a_spec=a_spec, b_spec=b_spec, extra_specs=[],
                   out_shapes=[jax.ShapeDtypeStruct((N_CHIPS, d, ff_shard), wire)],
                   out_specs=[pl.BlockSpec((None, tmw, tnw), lambda i, j, k: (j // per, i, j % per))],
                   epilogue=lambda acc: (acc,), acc_shape=(tmw, tnw))[0]

    dx2, dx2b, dg_mlp = _norm_bwd("norm_mlp_bwd", dh2, x2, r2, g_mlp, dx3)

    tn = _tile(d, 512)
    a_spec, b_spec = _mm_specs("nt", tm, tn, d)
    tile_mn = pl.BlockSpec((tm, tn), lambda i, j, k: (i, j))
    dmix = _matmul("out_proj_dx", dx2b, w_out_g, [], mode="nt", grid=(s // tm, d // tn, 1),
                   a_spec=a_spec, b_spec=b_spec, extra_specs=[],
                   out_shapes=[jax.ShapeDtypeStruct((s, d), F32)], out_specs=[tile_mn],
                   epilogue=lambda acc: (acc,))[0]

    tmw = _tile(d, 1024)
    a_spec, b_spec = _mm_specs("tn", tmw, d, tk_s)
    dw_out = _matmul("out_proj_dw", mix, dx2b, [], mode="tn", grid=(d // tmw, 1, s // tk_s),
                     a_spec=a_spec, b_spec=b_spec, extra_specs=[],
                     out_shapes=[jax.ShapeDtypeStruct((d, d), wire)],
                     out_specs=[pl.BlockSpec((tmw, d), lambda i, j, k: (i, j))],
                     epilogue=lambda acc: (acc,), acc_shape=(tmw, d))[0]

    do_a, do_b, delta_a, delta_b, dg_out_a, dg_out_b = _mix_bwd(dmix, o_a, o_b, r_a, r_b, g_out_a, g_out_b)

    dq_a, dkv_a, dsinks = _attn_bwd("attn_a_bwd", q_a, kv_a, do_a, lse_a, delta_a, dil=1,
                                    max_steps=WINDOW_A - 1, slopes=slopes_a, sinks=sinks)
    dqs, dks, dvs = [], [], []
    for window, dil in DILATED_BRANCHES:
        dq, dk, dv = _attn_bwd(f"attn_b{dil}_bwd", q_b, (k_b, v_b), do_b, lse_b, delta_b, dil=dil,
                               max_steps=window // dil, slopes=slopes_b)
        dqs.append(dq)
        dks.append(dk)
        dvs.append(dv)
    dproj, db_in = _assemble_dproj(dq_a, dkv_a, dqs, dks, dvs)

    tn = _tile(d, 512)
    a_spec, b_spec = _mm_specs("nn", tm, tn, d_in)
    tile_mn = pl.BlockSpec((tm, tn), lambda i, j, k: (i, j))
    dh1 = _matmul("in_proj_dx", dproj, w_in_t, [], mode="nn", grid=(s // tm, d // tn, 1),
                  a_spec=a_spec, b_spec=b_spec, extra_specs=[],
                  out_shapes=[jax.ShapeDtypeStruct((s, d), F32)], out_specs=[tile_mn],
                  epilogue=lambda acc: (acc,))[0]

    tmw = d_in // 2 if (d_in // 2) % 128 == 0 else d_in
    tnw = _tile(d, 1024)
    a_spec, b_spec = _mm_specs("tn", tmw, tnw, tk_s)
    dw_in_t = _matmul("in_proj_dw", dproj, h1, [], mode="tn", grid=(d_in // tmw, d // tnw, s // tk_s),
                      a_spec=a_spec, b_spec=b_spec, extra_specs=[],
                      out_shapes=[jax.ShapeDtypeStruct((d_in, d), wire)],
                      out_specs=[pl.BlockSpec((tmw, tnw), lambda i, j, k: (i, j))],
                      epilogue=lambda acc: (acc,), acc_shape=(tmw, tnw))[0]

    grad_x, _, dg_attn = _norm_bwd("norm_attn_bwd", dh1, x2d, r1, g_attn, dx2)

    small_parts = [dg_attn, db_in, dsinks[:, :n_heads_a], dg_out_a, dg_out_b, dg_mlp, dg_final]
    small_shapes = [g_attn.shape, b_in.shape, sinks_a.shape, g_out_a.shape, g_out_b.shape, g_mlp.shape,
                    g_final.shape]
    n_small = sum(int(np.prod(shp)) for shp in small_shapes)
    rows_s = -(-n_small // (8 * 128)) * 8
    (gw_in_t, gw_out, gw_1, gw_2), small_sum = _reduce_gradients(
        [dw_in_t, dw_out, dw_1.reshape(N_CHIPS * d, ff_shard), dw_2], _pack_small(small_parts, rows_s))
    gw_in = gw_in_t.T
    g_small = _unpack_small(small_sum, small_shapes)

    upd_in = _adamw("adamw_w_in", w_in[0], gw_in, m_w_in[0], v_w_in[0])
    upd_out = _adamw("adamw_w_out", w_out[0], gw_out, m_w_out[0], v_w_out[0])
    upd_1 = _adamw("adamw_w_1", w_1[0], gw_1, m_w_1[0], v_w_1[0])
    upd_2 = _adamw("adamw_w_2", w_2[0], gw_2, m_w_2[0], v_w_2[0])
    small_w = [g_attn, b_in, sinks_a, g_out_a, g_out_b, g_mlp, g_final]
    small_m = [m_g_attn, m_b_in, m_sinks_a, m_g_out_a, m_g_out_b, m_g_mlp, m_g_final]
    small_v = [v_g_attn, v_b_in, v_sinks_a, v_g_out_a, v_g_out_b, v_g_mlp, v_g_final]
    upd_small = _adamw("adamw_small", _pack_small(small_w, rows_s), small_sum,
                       _pack_small(small_m, rows_s), _pack_small(small_v, rows_s))
    d_small, m_small, v_small = [_unpack_small(t, small_shapes) for t in upd_small]

    loss = lax.psum(loss_part[0, 0], ("x", "y", "c"))

    def ordered(small, big):
        w_in_v, w_out_v, w_1_v, w_2_v = big
        return [small[0], w_in_v[None], small[1], small[2], small[3], small[4], w_out_v[None], small[5],
                w_1_v[None], w_2_v[None], small[6]]

    grads = ordered(g_small, (gw_in, gw_out, gw_1, gw_2))
    deltas = ordered(d_small, (upd_in[0], upd_out[0], upd_1[0], upd_2[0]))
    new_m = ordered(m_small, (upd_in[1], upd_out[1], upd_1[1], upd_2[1]))
    new_v = ordered(v_small, (upd_in[2], upd_out[2], upd_1[2], upd_2[2]))
    return (loss, grad_x[None], *grads, *deltas, *new_m, *new_v)
```

```python
import functools

import jax
import jax.numpy as jnp
import numpy as np
from jax import lax
from jax.experimental import pallas as pl
from jax.experimental.pallas import tpu as pltpu

F32 = jnp.float32
BF16 = jnp.bfloat16

HEAD_DIM = 64
BLOCK = 128
PAIR = 2 * HEAD_DIM
N_KV_GROUPS = 2
WINDOW_A = 128
DILATED_BRANCHES = ((128, 1), (512, 4), (2048, 16))
EPS = 1e-5
NEG_INF = -1e30
ATT_SCALE = HEAD_DIM ** -0.5

ADAM_LR = 0.001
ADAM_B1 = 0.9
ADAM_B2 = 0.999
ADAM_EPS = 1e-08
ADAM_WD = 0.01
ADAM_STEP = 10

N_CHIPS = 4
N_DEV = 8
MESH = pl.DeviceIdType.MESH
GRAD_WIRE_DTYPE = jnp.bfloat16

VMEM_CAPACITY_V7X = 64 * 1024 * 1024
VMEM_LIMIT_MAX = 56 * 1024 * 1024
VMEM_LIMIT_MIN = 32 * 1024 * 1024

HBM_SPEC = pl.BlockSpec(memory_space=pltpu.HBM)
VMEM_SPEC = pl.BlockSpec(memory_space=pltpu.VMEM)
SMEM_SPEC = pl.BlockSpec(memory_space=pltpu.SMEM)


def _nbytes(shape, dtype):
    return int(np.prod([s for s in shape if s is not None])) * jnp.dtype(dtype).itemsize


def _params(semantics, block_bytes):
    limit = min(max(2 * block_bytes + (4 << 20), VMEM_LIMIT_MIN), VMEM_LIMIT_MAX)
    return pltpu.CompilerParams(dimension_semantics=semantics, vmem_limit_bytes=limit)


def alibi_slopes(n):
    return [float(v) for v in np.asarray(2.0 ** (-8.0 * (np.arange(n) + 1) / n), dtype=np.float32)]


def _matmul(name, a, b, extras, *, mode, grid, a_spec, b_spec, extra_specs, out_shapes, out_specs,
            epilogue, prologue=None, acc_shape=None):
    dims = {"nn": ((1,), (0,)), "nt": ((1,), (1,)), "tn": ((0,), (0,))}[mode]
    nk = grid[2]
    n_ex, n_out = len(extras), len(out_shapes)

    def body(a_ref, b_ref, *rest):
        ex, outs = rest[:n_ex], rest[n_ex:n_ex + n_out]
        av = a_ref[...]
        if prologue is not None:
            av = prologue(av)
        part = lax.dot_general(av, b_ref[...], (dims, ((), ())), preferred_element_type=F32)

        def finish(acc):
            res = epilogue(acc, *[e[...] for e in ex])
            for o, r in zip(outs, res):
                o[...] = r.astype(o.dtype)

        if nk == 1:
            finish(part)
        else:
            acc_ref = rest[-1]
            k = pl.program_id(2)

            @pl.when(k == 0)
            def _():
                acc_ref[...] = part

            @pl.when(k > 0)
            def _():
                acc_ref[...] += part

            @pl.when(k == nk - 1)
            def _():
                finish(acc_ref[...])

    blocks = [(a_spec.block_shape, a.dtype), (b_spec.block_shape, b.dtype)]
    blocks += [(s.block_shape, e.dtype) for s, e in zip(extra_specs, extras)]
    blocks += [(s.block_shape, o.dtype) for s, o in zip(out_specs, out_shapes)]
    nbytes = sum(_nbytes(s, d) for s, d in blocks)
    scratch = []
    if nk > 1:
        scratch.append(pltpu.VMEM(acc_shape, F32))
        nbytes += _nbytes(acc_shape, F32)
    res = pl.pallas_call(
        body, name=name, grid=grid,
        in_specs=[a_spec, b_spec, *extra_specs], out_specs=list(out_specs), out_shape=list(out_shapes),
        scratch_shapes=scratch,
        compiler_params=_params(("parallel", "parallel", "arbitrary"), nbytes),
    )(a, b, *extras)
    return res


def _mm_specs(mode, tm, tn, tk, b_block=None, b_map=None):
    if mode == "tn":
        a_spec = pl.BlockSpec((tk, tm), lambda i, j, k: (k, i))
    else:
        a_spec = pl.BlockSpec((tm, tk), lambda i, j, k: (i, k))
    if b_block is not None:
        b_spec = pl.BlockSpec(b_block, b_map)
    elif mode == "nt":
        b_spec = pl.BlockSpec((tn, tk), lambda i, j, k: (j, k))
    else:
        b_spec = pl.BlockSpec((tk, tn), lambda i, j, k: (k, j))
    return a_spec, b_spec


def _project_by_class(name, h, w_t, bias, row_off, width, dilations):
    s, d = h.shape
    tm = _tile(s, 1024)
    tn = _tile(width, 256)
    off = row_off // tn
    assert row_off % tn == 0 and tn % 128 == 0
    n_out = len(dilations)

    def body(h_ref, w_ref, b_ref, *rest):
        outs, perm_ref = rest[:n_out], rest[n_out]
        acc = lax.dot_general(h_ref[...], w_ref[...], (((1,), (1,)), ((), ())), preferred_element_type=F32)
        acc = acc + b_ref[...]
        for j in range(tn // 128):
            cols = slice(j * 128, (j + 1) * 128)
            for o_ref, dil in zip(outs, dilations):
                _to_classes(o_ref, cols, acc[:, cols], perm_ref, dil)

    blocks = tm * d * 2 + tn * d * 2 + 3 * tm * tn * 2 + tm * 128 * 4
    return pl.pallas_call(
        body, name=name, grid=(s // tm, width // tn),
        in_specs=[pl.BlockSpec((tm, d), lambda i, j: (i, 0)), pl.BlockSpec((tn, d), lambda i, j: (j + off, 0)),
                  pl.BlockSpec((1, tn), lambda i, j: (0, j + off))],
        out_specs=[pl.BlockSpec((dil, tm // dil, tn), lambda i, j: (0, i, j)) for dil in dilations],
        out_shape=[_class_shape(dil, s, width, BF16) for dil in dilations],
        scratch_shapes=[pltpu.VMEM((tm, 128), F32)],
        compiler_params=_params(("parallel", "parallel"), blocks),
    )(h, w_t, bias)


def _tile(n, want):
    if n <= want:
        return n
    t = (want // 128) * 128
    while t > 128 and n % t:
        t -= 128
    assert n % t == 0, (n, want)
    return t


def _row_tile(s):
    return 256 if s % 256 == 0 else s


def _norm_fwd(name, x, g):
    s, d = x.shape
    tm = _row_tile(s)

    def body(x_ref, g_ref, h_ref, r_ref):
        xv = x_ref[...]
        r = lax.rsqrt(jnp.mean(xv * xv, axis=-1, keepdims=True) + EPS)
        h_ref[...] = ((xv * r) * g_ref[...]).astype(BF16)
        r_ref[...] = r

    row = pl.BlockSpec((tm, d), lambda i: (i, 0))
    return pl.pallas_call(
        body, name=name, grid=(s // tm,),
        in_specs=[row, pl.BlockSpec((1, d), lambda i: (0, 0))],
        out_specs=[row, pl.BlockSpec((tm, 1), lambda i: (i, 0))],
        out_shape=[jax.ShapeDtypeStruct((s, d), BF16), jax.ShapeDtypeStruct((s, 1), F32)],
        compiler_params=_params(("parallel",), tm * d * 6),
    )(x, g)


def _norm_bwd(name, dh, x, r, g, dres):
    s, d = x.shape
    tm = _row_tile(s)

    def body(dh_ref, x_ref, r_ref, g_ref, dres_ref, dx_ref, dxb_ref, dg_ref):
        rv = r_ref[...]
        xn = x_ref[...] * rv
        dhv = dh_ref[...]
        dxn = dhv * g_ref[...]
        dx = dres_ref[...] + rv * (dxn - xn * jnp.mean(dxn * xn, axis=-1, keepdims=True))
        dx_ref[...] = dx
        dxb_ref[...] = dx.astype(BF16)
        part = jnp.sum(dhv * xn, axis=0, keepdims=True)

        @pl.when(pl.program_id(0) == 0)
        def _():
            dg_ref[...] = part

        @pl.when(pl.program_id(0) > 0)
        def _():
            dg_ref[...] += part

    row = pl.BlockSpec((tm, d), lambda i: (i, 0))
    vec = pl.BlockSpec((1, d), lambda i: (0, 0))
    return pl.pallas_call(
        body, name=name, grid=(s // tm,),
        in_specs=[row, row, pl.BlockSpec((tm, 1), lambda i: (i, 0)), vec, row],
        out_specs=[row, row, vec],
        out_shape=[jax.ShapeDtypeStruct((s, d), F32), jax.ShapeDtypeStruct((s, d), BF16),
                   jax.ShapeDtypeStruct((1, d), F32)],
        compiler_params=_params(("arbitrary",), tm * d * 18),
    )(dh, x, r, g, dres)


def _loss_head(x3, target, g):
    s, d = x3.shape
    tm = _row_tile(s)

    def body(x_ref, t_ref, g_ref, dx_ref, dxb_ref, loss_ref, dg_ref):
        xv = x_ref[...]
        gv = g_ref[...]
        r = lax.rsqrt(jnp.mean(xv * xv, axis=-1, keepdims=True) + EPS)
        xn = xv * r
        err = xn * gv - t_ref[...]
        loss = 0.5 * jnp.sum(jnp.mean(err * err, axis=-1, keepdims=True), axis=0, keepdims=True)
        dy = err / d
        dxn = dy * gv
        dx = r * (dxn - xn * jnp.mean(dxn * xn, axis=-1, keepdims=True))
        dx_ref[...] = dx
        dxb_ref[...] = dx.astype(BF16)
        dg = jnp.sum(dy * xn, axis=0, keepdims=True)
        loss_row = jnp.broadcast_to(loss, (1, 128))

        @pl.when(pl.program_id(0) == 0)
        def _():
            dg_ref[...] = dg
            loss_ref[...] = loss_row

        @pl.when(pl.program_id(0) > 0)
        def _():
            dg_ref[...] += dg
            loss_ref[...] += loss_row

    row = pl.BlockSpec((tm, d), lambda i: (i, 0))
    vec = pl.BlockSpec((1, d), lambda i: (0, 0))
    return pl.pallas_call(
        body, name="loss_head", grid=(s // tm,),
        in_specs=[row, row, vec],
        out_specs=[row, row, pl.BlockSpec((1, 128), lambda i: (0, 0)), vec],
        out_shape=[jax.ShapeDtypeStruct((s, d), F32), jax.ShapeDtypeStruct((s, d), BF16),
                   jax.ShapeDtypeStruct((1, 128), F32), jax.ShapeDtypeStruct((1, d), F32)],
        compiler_params=_params(("arbitrary",), tm * d * 14),
    )(x3, target, g)


def _low_lanes(rows):
    return lax.broadcasted_iota(jnp.int32, (rows, PAIR), 1) < HEAD_DIM


def _to_classes(dst_ref, cols, value, perm_ref, dil):
    rows = value.shape[0]
    if dil == 1:
        dst_ref[0, :, cols] = value.astype(dst_ref.dtype)
        return
    perm_ref[...] = value
    for r in range(dil):
        dst_ref[r, :, cols] = perm_ref[pl.ds(r, rows // dil, stride=dil), :].astype(dst_ref.dtype)


def _from_classes(src_ref, cols, perm_ref, dil):
    if dil == 1:
        return src_ref[0, :, cols]
    rows = perm_ref.shape[0]
    for r in range(dil):
        perm_ref[pl.ds(r, rows // dil, stride=dil), :] = src_ref[r, :, cols]
    return perm_ref[...]


def _class_spec(dil, tm, width):
    return pl.BlockSpec((dil, tm // dil, width), lambda i: (0, i, 0))


def _class_shape(dil, s, width, dtype):
    return jax.ShapeDtypeStruct((dil, s // dil, width), dtype)


DILATIONS = tuple(d for _, d in DILATED_BRANCHES)


def _mix_fwd(oa, obs, lses, ga, gb):
    s, qa = oa.shape
    qb = obs[0].shape[2]
    tm = _row_tile(s)
    all_lanes = slice(0, 128)

    def body(oa_ref, o1_ref, o2_ref, o3_ref, l1_ref, l2_ref, l3_ref, ga_ref, gb_ref,
             mix_ref, ob_ref, t1_ref, t2_ref, t3_ref, ra_ref, rb_ref, perm_ref):
        oav = oa_ref[...]
        ra = lax.rsqrt(jnp.mean(oav * oav, axis=-1, keepdims=True) + EPS)
        ra_ref[...] = ra
        mix_ref[:, 0:qa] = ((oav * ra) * ga_ref[...]).astype(BF16)
        l1, l2, l3 = [_from_classes(l_ref, all_lanes, perm_ref, dil)
                      for l_ref, dil in zip((l1_ref, l2_ref, l3_ref), DILATIONS)]
        mx = jnp.maximum(jnp.maximum(l1, l2), l3)
        e1, e2, e3 = jnp.exp(l1 - mx), jnp.exp(l2 - mx), jnp.exp(l3 - mx)
        tot = e1 + e2 + e3
        lse = mx + jnp.log(tot)
        for t_ref, dil in zip((t1_ref, t2_ref, t3_ref), DILATIONS):
            _to_classes(t_ref, all_lanes, lse, perm_ref, dil)
        ws = (e1 / tot, e2 / tot, e3 / tot)
        low = _low_lanes(tm)
        ssq = jnp.zeros((tm, 1), F32)
        for i in range(qb // PAIR):
            sl = slice(i * PAIR, (i + 1) * PAIR)
            acc = jnp.zeros((tm, PAIR), F32)
            for w, o_ref, dil in zip(ws, (o1_ref, o2_ref, o3_ref), DILATIONS):
                wexp = jnp.where(low, w[:, 2 * i:2 * i + 1], w[:, 2 * i + 1:2 * i + 2])
                acc = acc + wexp * _from_classes(o_ref, sl, perm_ref, dil)
            ob_ref[:, sl] = acc
            ssq = ssq + jnp.sum(acc * acc, axis=-1, keepdims=True)
        rb = lax.rsqrt(ssq / qb + EPS)
        rb_ref[...] = rb
        mix_ref[:, qa:qa + qb] = ((ob_ref[...] * rb) * gb_ref[...]).astype(BF16)

    def row(w):
        return pl.BlockSpec((tm, w), lambda i: (i, 0))

    def vec(w):
        return pl.BlockSpec((1, w), lambda i: (0, 0))

    return pl.pallas_call(
        body, name="mix_fwd", grid=(s // tm,),
        in_specs=([row(qa)] + [_class_spec(d, tm, qb) for d in DILATIONS]
                  + [_class_spec(d, tm, 128) for d in DILATIONS] + [vec(qa), vec(qb)]),
        out_specs=([row(qa + qb), row(qb)] + [_class_spec(d, tm, 128) for d in DILATIONS] + [row(1), row(1)]),
        out_shape=([jax.ShapeDtypeStruct((s, qa + qb), BF16), jax.ShapeDtypeStruct((s, qb), F32)]
                   + [_class_shape(d, s, 128, F32) for d in DILATIONS]
                   + [jax.ShapeDtypeStruct((s, 1), F32), jax.ShapeDtypeStruct((s, 1), F32)]),
        scratch_shapes=[pltpu.VMEM((tm, 128), F32)],
        compiler_params=_params(("parallel",), tm * (qa + 4 * qb) * 4 + tm * (qa + qb) * 2 + tm * 4096),
    )(oa, *obs, *lses, ga, gb)


def _head_rowsums(prod, rows):
    low = _low_lanes(rows)
    lane = lax.broadcasted_iota(jnp.int32, (rows, 128), 1)
    out = jnp.zeros((rows, 128), F32)
    for i in range(prod.shape[1] // PAIR):
        tile = prod[:, i * PAIR:(i + 1) * PAIR]
        lo = jnp.sum(jnp.where(low, tile, 0.0), axis=-1, keepdims=True)
        hi = jnp.sum(jnp.where(low, 0.0, tile), axis=-1, keepdims=True)
        out = jnp.where(lane == 2 * i, lo, out)
        out = jnp.where(lane == 2 * i + 1, hi, out)
    return out


def _mix_bwd(dmix, oa, ob, ra, rb, ga, gb):
    s, qa = oa.shape
    qb = ob.shape[1]
    tm = _row_tile(s)

    def one(dy, o, r, g):
        xn = o * r
        dxn = dy * g
        do = r * (dxn - xn * jnp.mean(dxn * xn, axis=-1, keepdims=True))
        return do, jnp.sum(dy * xn, axis=0, keepdims=True), _head_rowsums(do * o, tm)

    def body(dmix_ref, oa_ref, ob_ref, ra_ref, rb_ref, ga_ref, gb_ref,
             doa_ref, dob1_ref, dob2_ref, dob3_ref, dla_ref, dlb1_ref, dlb2_ref, dlb3_ref,
             dga_ref, dgb_ref, perm_ref):
        doa, dga, dla = one(dmix_ref[:, 0:qa], oa_ref[...], ra_ref[...], ga_ref[...])
        dob, dgb, dlb = one(dmix_ref[:, qa:qa + qb], ob_ref[...], rb_ref[...], gb_ref[...])
        doa_ref[...] = doa.astype(BF16)
        dla_ref[...] = dla
        for dob_ref, dlb_ref, dil in zip((dob1_ref, dob2_ref, dob3_ref), (dlb1_ref, dlb2_ref, dlb3_ref),
                                         DILATIONS):
            _to_classes(dlb_ref, slice(0, 128), dlb, perm_ref, dil)
            for i in range(qb // PAIR):
                sl = slice(i * PAIR, (i + 1) * PAIR)
                _to_classes(dob_ref, sl, dob[:, sl], perm_ref, dil)

        @pl.when(pl.program_id(0) == 0)
        def _():
            dga_ref[...] = dga
            dgb_ref[...] = dgb

        @pl.when(pl.program_id(0) > 0)
        def _():
            dga_ref[...] += dga
            dgb_ref[...] += dgb

    def row(w):
        return pl.BlockSpec((tm, w), lambda i: (i, 0))

    def vec(w):
        return pl.BlockSpec((1, w), lambda i: (0, 0))

    return pl.pallas_call(
        body, name="mix_bwd", grid=(s // tm,),
        in_specs=[row(qa + qb), row(qa), row(qb), row(1), row(1), vec(qa), vec(qb)],
        out_specs=([row(qa)] + [_class_spec(d, tm, qb) for d in DILATIONS] + [row(128)]
                   + [_class_spec(d, tm, 128) for d in DILATIONS] + [vec(qa), vec(qb)]),
        out_shape=([jax.ShapeDtypeStruct((s, qa), BF16)] + [_class_shape(d, s, qb, BF16) for d in DILATIONS]
                   + [jax.ShapeDtypeStruct((s, 128), F32)] + [_class_shape(d, s, 128, F32) for d in DILATIONS]
                   + [jax.ShapeDtypeStruct((1, qa), F32), jax.ShapeDtypeStruct((1, qb), F32)]),
        scratch_shapes=[pltpu.VMEM((tm, 128), F32)],
        compiler_params=_params(("arbitrary",), tm * (qa + qb) * 16),
    )(dmix, oa, ob, ra, rb, ga, gb)


def _assemble_dproj(dqa, dkva, dqs, dks, dvs):
    s, qa = dqa.shape
    kva = dkva.shape[1]
    qb = dqs[0].shape[2]
    width = qa + kva + 3 * qb
    tm = _row_tile(s)

    def body(dqa_ref, dkva_ref, q1, q2, q3, k1, k2, k3, v1, v2, v3, dp_ref, db_ref, perm_ref):
        first = pl.program_id(0) == 0

        def emit(off, val):
            dp_ref[:, off:off + PAIR] = val.astype(BF16)
            col = jnp.sum(val, axis=0, keepdims=True)

            @pl.when(first)
            def _():
                db_ref[:, off:off + PAIR] = col

            @pl.when(jnp.logical_not(first))
            def _():
                db_ref[:, off:off + PAIR] += col

        for i in range(qa // PAIR):
            emit(i * PAIR, dqa_ref[:, i * PAIR:(i + 1) * PAIR])
        for i in range(kva // PAIR):
            emit(qa + i * PAIR, dkva_ref[:, i * PAIR:(i + 1) * PAIR])
        for j, branch_refs in enumerate(((q1, q2, q3), (k1, k2, k3), (v1, v2, v3))):
            for i in range(qb // PAIR):
                sl = slice(i * PAIR, (i + 1) * PAIR)
                total = None
                for ref, dil in zip(branch_refs, DILATIONS):
                    val = _from_classes(ref, sl, perm_ref, dil)
                    total = val if total is None else total + val
                emit(qa + kva + j * qb + i * PAIR, total)

    def row(w):
        return pl.BlockSpec((tm, w), lambda i: (i, 0))

    return pl.pallas_call(
        body, name="assemble_dproj", grid=(s // tm,),
        in_specs=[row(qa), row(kva)] + [_class_spec(d, tm, qb) for d in DILATIONS] * 3,
        out_specs=[row(width), pl.BlockSpec((1, width), lambda i: (0, 0))],
        out_shape=[jax.ShapeDtypeStruct((s, width), BF16), jax.ShapeDtypeStruct((1, width), F32)],
        scratch_shapes=[pltpu.VMEM((tm, 128), F32)],
        compiler_params=_params(("arbitrary",), tm * (qa + kva + 9 * qb) * 4 + tm * width * 2),
    )(dqa, dkva, *dqs, *dks, *dvs)


def _band_masks(max_steps, dil, first_block):
    qi = lax.broadcasted_iota(jnp.int32, (BLOCK, 2 * BLOCK), 0)
    kj = lax.broadcasted_iota(jnp.int32, (BLOCK, 2 * BLOCK), 1)
    steps = qi + BLOCK - kj
    valid = (steps >= 0) & (steps <= max_steps) & ((kj >= BLOCK) | jnp.logical_not(first_block))
    dist = (steps * dil).astype(F32)
    return valid, dist


def _swap_halves(t):
    return pltpu.roll(t, HEAD_DIM, 1)


def _dup_group(t_bf16, group):
    t = t_bf16.astype(F32)
    low = lax.broadcasted_iota(jnp.int32, t.shape, 1) < HEAD_DIM
    keep = low if group == 0 else jnp.logical_not(low)
    return jnp.where(keep, t, _swap_halves(t)).astype(BF16)


def _attn_fwd(name, q, kv, *, dil, max_steps, slopes, sinks=None):
    grouped = sinks is not None
    _, length, w = q.shape
    n_pairs = w // PAIR
    nb = length // BLOCK
    heads_per_group = 2 * n_pairs // N_KV_GROUPS

    def body(*refs):
        if grouped:
            sink_ref, q_ref, kvp_ref, kvc_ref, o_ref, lse_ref = refs
        else:
            q_ref, kp_ref, kc_ref, vp_ref, vc_ref, o_ref, lse_ref = refs
        n = pl.program_id(1)
        valid, dist = _band_masks(max_steps, dil, n == 0)
        low = _low_lanes(BLOCK)
        lane = lax.broadcasted_iota(jnp.int32, (BLOCK, 128), 1)
        lse_acc = jnp.zeros((BLOCK, 128), F32)
        if grouped:
            kv_all = jnp.concatenate([kvp_ref[...], kvc_ref[...]], axis=0)
            k_dup = [_dup_group(kv_all[:, 0:PAIR], g) for g in range(N_KV_GROUPS)]
            v_dup = [_dup_group(kv_all[:, PAIR:2 * PAIR], g) for g in range(N_KV_GROUPS)]
        for i in range(n_pairs):
            sl = slice(i * PAIR, (i + 1) * PAIR)
            q2 = q_ref[:, sl]
            if grouped:
                kk, vv = k_dup[2 * i // heads_per_group], v_dup[2 * i // heads_per_group]
            else:
                kk = jnp.concatenate([kp_ref[:, sl], kc_ref[:, sl]], axis=0)
                vv = jnp.concatenate([vp_ref[:, sl], vc_ref[:, sl]], axis=0)
            outs = []
            for half in (0, 1):
                h = 2 * i + half
                qm = jnp.where(low if half == 0 else jnp.logical_not(low), q2, jnp.zeros_like(q2))
                sc = lax.dot_general(qm, kk, (((1,), (1,)), ((), ())), preferred_element_type=F32)
                sc = jnp.where(valid, sc * ATT_SCALE - slopes[h] * dist, NEG_INF)
                m = jnp.max(sc, axis=-1, keepdims=True)
                if grouped:
                    m = jnp.maximum(m, sink_ref[h])
                p = jnp.exp(sc - m)
                den = jnp.sum(p, axis=-1, keepdims=True)
                if grouped:
                    den = den + jnp.exp(sink_ref[h] - m)
                o = jnp.dot(p.astype(BF16), vv, preferred_element_type=F32)
                outs.append(o / den)
                lse_acc = jnp.where(lane == h, m + jnp.log(den), lse_acc)
            o_ref[:, sl] = jnp.where(low, outs[0], outs[1])
        lse_ref[...] = lse_acc

    def cur(width):
        return pl.BlockSpec((None, BLOCK, width), lambda r, n: (r, n, 0))

    def prev(width):
        return pl.BlockSpec((None, BLOCK, width), lambda r, n: (r, jnp.maximum(n - 1, 0), 0))

    if grouped:
        kvw = kv.shape[2]
        operands = [sinks, q, kv, kv]
        in_specs = [SMEM_SPEC, cur(w), prev(kvw), cur(kvw)]
    else:
        operands = [q, kv[0], kv[0], kv[1], kv[1]]
        in_specs = [cur(w), prev(w), cur(w), prev(w), cur(w)]
    return pl.pallas_call(
        body, name=name, grid=(dil, nb), in_specs=in_specs,
        out_specs=[cur(w), cur(128)],
        out_shape=[jax.ShapeDtypeStruct((dil, length, w), F32),
                   jax.ShapeDtypeStruct((dil, length, 128), F32)],
        compiler_params=_params(("parallel", "parallel"), BLOCK * w * 16),
    )(*operands)


def _attn_bwd(name, q, kv, do, lse, delta, *, dil, max_steps, slopes, sinks=None):
    grouped = sinks is not None
    _, length, w = q.shape
    n_pairs = w // PAIR
    nb = length // BLOCK
    heads_per_group = 2 * n_pairs // N_KV_GROUPS
    pairs_per_group = n_pairs // N_KV_GROUPS

    def body(*refs):
        if grouped:
            (sink_ref, q_ref, kvp_ref, kvc_ref, do_ref, lse_ref, dl_ref,
             dq_ref, dkv_ref, dsink_ref, acc_ref) = refs
        else:
            (q_ref, kp_ref, kc_ref, vp_ref, vc_ref, do_ref, lse_ref, dl_ref,
             dq_ref, dk_ref, dv_ref, acck_ref, accv_ref) = refs
        n = pl.program_id(1)

        @pl.when(n == 0)
        def _():
            if grouped:
                acc_ref[...] = jnp.zeros_like(acc_ref)

                @pl.when(pl.program_id(0) == 0)
                def _():
                    dsink_ref[...] = jnp.zeros_like(dsink_ref)
            else:
                acck_ref[...] = jnp.zeros_like(acck_ref)
                accv_ref[...] = jnp.zeros_like(accv_ref)

        @pl.when(n == nb)
        def _():
            if grouped:
                dkv_ref[...] = acc_ref[...]
            else:
                dk_ref[...] = acck_ref[...]
                dv_ref[...] = accv_ref[...]

        @pl.when(n < nb)
        def _():
            valid, dist = _band_masks(max_steps, dil, n == 0)
            low = _low_lanes(BLOCK)
            low_kv = _low_lanes(2 * BLOCK)
            lane1 = lax.broadcasted_iota(jnp.int32, (1, 128), 1)
            if grouped:
                kv_all = jnp.concatenate([kvp_ref[...], kvc_ref[...]], axis=0)
                k_dup = [_dup_group(kv_all[:, 0:PAIR], g) for g in range(N_KV_GROUPS)]
                v_dup = [_dup_group(kv_all[:, PAIR:2 * PAIR], g) for g in range(N_KV_GROUPS)]
                dk_grp = [jnp.zeros((2 * BLOCK, PAIR), F32) for _ in range(N_KV_GROUPS)]
                dv_grp = [jnp.zeros((2 * BLOCK, PAIR), F32) for _ in range(N_KV_GROUPS)]
                dsink = jnp.zeros((1, 128), F32)
            for i in range(n_pairs):
                sl = slice(i * PAIR, (i + 1) * PAIR)
                q2 = q_ref[:, sl]
                do2 = do_ref[:, sl]
                if grouped:
                    grp = 2 * i // heads_per_group
                    kk, vv = k_dup[grp], v_dup[grp]
                else:
                    kk = jnp.concatenate([kp_ref[:, sl], kc_ref[:, sl]], axis=0)
                    vv = jnp.concatenate([vp_ref[:, sl], vc_ref[:, sl]], axis=0)
                dkk = jnp.zeros((2 * BLOCK, PAIR), F32)
                dvv = jnp.zeros((2 * BLOCK, PAIR), F32)
                dqs = []
                for half in (0, 1):
                    h = 2 * i + half
                    keep = low if half == 0 else jnp.logical_not(low)
                    qm = jnp.where(keep, q2, jnp.zeros_like(q2))
                    dom = jnp.where(keep, do2, jnp.zeros_like(do2))
                    lse_h = lse_ref[:, h:h + 1]
                    dl_h = dl_ref[:, h:h + 1]
                    sc = lax.dot_general(qm, kk, (((1,), (1,)), ((), ())), preferred_element_type=F32)
                    sc = jnp.where(valid, sc * ATT_SCALE - slopes[h] * dist, NEG_INF)
                    p = jnp.exp(sc - lse_h)
                    dp = lax.dot_general(dom, vv, (((1,), (1,)), ((), ())), preferred_element_type=F32)
                    ds = (p * (dp - dl_h)).astype(BF16)
                    dqs.append(jnp.dot(ds, kk, preferred_element_type=F32))
                    dkk = dkk + lax.dot_general(ds, qm, (((0,), (0,)), ((), ())),
                                                preferred_element_type=F32)
                    dvv = dvv + lax.dot_general(p.astype(BF16), dom, (((0,), (0,)), ((), ())),
                                                preferred_element_type=F32)
                    if grouped:
                        contrib = -jnp.sum(jnp.exp(sink_ref[h] - lse_h) * dl_h, axis=0, keepdims=True)
                        dsink = jnp.where(lane1 == h, dsink + contrib, dsink)
                dq_ref[:, sl] = jnp.where(low, dqs[0], dqs[1]) * ATT_SCALE
                if grouped:
                    dk_grp[grp] = dk_grp[grp] + dkk
                    dv_grp[grp] = dv_grp[grp] + dvv
                else:
                    dkk = dkk * ATT_SCALE
                    dk_ref[:, sl] = acck_ref[:, sl] + dkk[0:BLOCK]
                    acck_ref[:, sl] = dkk[BLOCK:2 * BLOCK]
                    dv_ref[:, sl] = accv_ref[:, sl] + dvv[0:BLOCK]
                    accv_ref[:, sl] = dvv[BLOCK:2 * BLOCK]
            if grouped:
                folded = [t + _swap_halves(t) for t in dk_grp + dv_grp]
                dk_tile = jnp.where(low_kv, folded[0], folded[1]) * ATT_SCALE
                dv_tile = jnp.where(low_kv, folded[2], folded[3])
                part = jnp.concatenate([dk_tile, dv_tile], axis=1)
                dkv_ref[...] = acc_ref[...] + part[0:BLOCK]
                acc_ref[...] = part[BLOCK:2 * BLOCK]
                dsink_ref[...] += dsink

    last = nb - 1

    def cur(width):
        return pl.BlockSpec((None, BLOCK, width), lambda r, n: (r, jnp.minimum(n, last), 0))

    def prev(width):
        return pl.BlockSpec((None, BLOCK, width),
                            lambda r, n: (r, jnp.maximum(jnp.minimum(n, last) - 1, 0), 0))

    def done(width):
        return pl.BlockSpec((None, BLOCK, width), lambda r, n: (r, jnp.maximum(n - 1, 0), 0))

    if grouped:
        assert pairs_per_group * N_KV_GROUPS == n_pairs and heads_per_group % 2 == 0
        kvw = kv.shape[2]
        operands = [sinks, q, kv, kv, do, lse, delta]
        in_specs = [SMEM_SPEC, cur(w), prev(kvw), cur(kvw), cur(w), cur(128), cur(128)]
        out_specs = [cur(w), done(kvw), pl.BlockSpec((1, 128), lambda r, n: (0, 0))]
        out_shape = [jax.ShapeDtypeStruct((dil, length, w), F32), jax.ShapeDtypeStruct((dil, length, kvw), F32),
                     jax.ShapeDtypeStruct((1, 128), F32)]
        scratch = [pltpu.VMEM((BLOCK, kvw), F32)]
    else:
        operands = [q, kv[0], kv[0], kv[1], kv[1], do, lse, delta]
        in_specs = [cur(w), prev(w), cur(w), prev(w), cur(w), cur(w), cur(128), cur(128)]
        out_specs = [cur(w), done(w), done(w)]
        out_shape = [jax.ShapeDtypeStruct((dil, length, w), F32)] * 3
        scratch = [pltpu.VMEM((BLOCK, w), F32), pltpu.VMEM((BLOCK, w), F32)]
    return pl.pallas_call(
        body, name=name, grid=(dil, nb + 1), in_specs=in_specs, out_specs=out_specs,
        out_shape=out_shape, scratch_shapes=scratch,
        compiler_params=_params(("arbitrary", "arbitrary"), BLOCK * w * 32),
    )(*operands)


def _adamw(name, w, g, m, v):
    rows, cols = w.shape
    tm = 256 if rows % 256 == 0 else rows

    def body(w_ref, g_ref, m_ref, v_ref, d_ref, nm_ref, nv_ref):
        gv = g_ref[...]
        mn = ADAM_B1 * m_ref[...] + (1.0 - ADAM_B1) * gv
        vn = ADAM_B2 * v_ref[...] + (1.0 - ADAM_B2) * (gv * gv)
        m_hat = mn / (1.0 - ADAM_B1 ** ADAM_STEP)
        v_hat = vn / (1.0 - ADAM_B2 ** ADAM_STEP)
        d_ref[...] = -ADAM_LR * (m_hat / (jnp.sqrt(v_hat) + ADAM_EPS) + ADAM_WD * w_ref[...])
        nm_ref[...] = mn
        nv_ref[...] = vn

    spec = pl.BlockSpec((tm, cols), lambda i: (i, 0))
    return pl.pallas_call(
        body, name=name, grid=(rows // tm,), in_specs=[spec] * 4, out_specs=[spec] * 3,
        out_shape=[jax.ShapeDtypeStruct(w.shape, F32)] * 3,
        compiler_params=_params(("parallel",), tm * cols * 28),
    )(w, g, m, v)


def _mesh_position():
    return lax.axis_index("x"), lax.axis_index("y"), lax.axis_index("c")


def _other_chips(x, y):
    return [(1 - x, y), (x, 1 - y), (1 - x, 1 - y)]


def _gather_weights(shards):
    n_w = len(shards)
    halves = [s.shape[0] // 2 for s in shards]

    def body(*refs):
        ins, outs = refs[:n_w], refs[n_w:2 * n_w]
        local_sems, send_sems, recv_sems, fsend_sems, frecv_sems = refs[2 * n_w:]
        x, y, c = _mesh_position()
        chip = 2 * x + y
        sibling = (x, y, 1 - c)
        others = _other_chips(x, y)

        def region(w, owner_chip, half):
            start = owner_chip * (2 * halves[w]) + half * halves[w]
            return outs[w].at[pl.ds(pl.multiple_of(start, 16), halves[w])]

        def remote(src, dst, ssem, rsem, dev):
            return pltpu.make_async_remote_copy(src_ref=src, dst_ref=dst, send_sem=ssem, recv_sem=rsem,
                                                device_id=dev, device_id_type=MESH)

        local = []
        for w in range(n_w):
            cp = pltpu.make_async_copy(ins[w], outs[w].at[pl.ds(pl.multiple_of(chip * 2 * halves[w], 16),
                                                                2 * halves[w])], local_sems.at[w])
            cp.start()
            local.append(cp)
        sends = []
        for w in range(n_w):
            mine = ins[w].at[pl.ds(pl.multiple_of(c * halves[w], 16), halves[w])]
            for k, (px, py) in enumerate(others):
                cp = remote(mine, region(w, chip, c), send_sems.at[w, k], recv_sems.at[w, k], (px, py, c))
                cp.start()
                sends.append(cp)
        for k, (px, py) in enumerate(others):
            for w in range(n_w):
                landed = region(w, 2 * px + py, c)
                remote(landed, landed, send_sems.at[w, k], recv_sems.at[w, k], (px, py, c)).wait_recv()
                cp = remote(landed, landed, fsend_sems.at[w, k], frecv_sems.at[w, k], sibling)
                cp.start()
                sends.append(cp)
        for k, (px, py) in enumerate(others):
            for w in range(n_w):
                passed = region(w, 2 * px + py, 1 - c)
                remote(passed, passed, fsend_sems.at[w, k], frecv_sems.at[w, k], sibling).wait_recv()
        for cp in sends:
            cp.wait_send()
        for cp in local:
            cp.wait()

    return pl.pallas_call(
        body, name="gather_weights",
        in_specs=[HBM_SPEC] * n_w, out_specs=[HBM_SPEC] * n_w,
        out_shape=[jax.ShapeDtypeStruct((N_CHIPS * s.shape[0], s.shape[1]), s.dtype) for s in shards],
        scratch_shapes=[pltpu.SemaphoreType.DMA((n_w,))] + [pltpu.SemaphoreType.DMA((n_w, 3))] * 4,
    )(*shards)


REDUCE_CHUNK_ROWS = 256


def _reduce_gradients(grads, small):
    n_w = len(grads)
    rows = [g.shape[0] // N_CHIPS for g in grads]
    halves = [r // 2 for r in rows]
    cols = grads[0].shape[1]
    wire = grads[0].dtype
    ch = REDUCE_CHUNK_ROWS
    for h in halves:
        assert h % 16 == 0
    rows_s = small.shape[0]

    def body(*refs):
        g_in = refs[:n_w]
        small_ref = refs[n_w]
        outs = refs[n_w + 1:2 * n_w + 1]
        small_out = refs[2 * n_w + 1]
        from_sib = refs[2 * n_w + 2:3 * n_w + 2]
        chip_sum = refs[3 * n_w + 2:4 * n_w + 2]
        from_chips = refs[4 * n_w + 2:5 * n_w + 2]
        (small_all, buf_a, buf_b, buf_o, sib_send, sib_recv, chip_send, chip_recv,
         fin_send, fin_recv, small_send, small_recv, io_sem) = refs[5 * n_w + 2:]
        x, y, c = _mesh_position()
        chip = 2 * x + y
        me = 4 * x + 2 * y + c
        sibling = (x, y, 1 - c)
        others = _other_chips(x, y)

        def remote(src, dst, ssem, rsem, dev):
            return pltpu.make_async_remote_copy(src_ref=src, dst_ref=dst, send_sem=ssem, recv_sem=rsem,
                                                device_id=dev, device_id_type=MESH)

        def part(w, owner_chip, half):
            start = owner_chip * rows[w] + half * halves[w]
            return g_in[w].at[pl.ds(pl.multiple_of(start, 16), halves[w])]

        pending = []
        small_all[me] = small_ref[...]
        for j in range(N_DEV - 1):
            peer = (me + 1 + j) % N_DEV
            cp = remote(small_all.at[me], small_all.at[me], small_send.at[j], small_recv.at[j],
                        (peer // 4, (peer // 2) % 2, peer % 2))
            cp.start()
            pending.append(cp)

        for w in range(n_w):
            for k in range(N_CHIPS):
                cp = remote(part(w, k, 1 - c), from_sib[w].at[k], sib_send.at[w, k], sib_recv.at[w, k], sibling)
                cp.start()
                pending.append(cp)

        def add_stream(w, srcs, dst, n_rows):
            def chunk(start, size):
                total = None
                for i, src in enumerate(srcs):
                    buf = buf_a if i % 2 == 0 else buf_b
                    cp = pltpu.make_async_copy(src.at[pl.ds(start, size)], buf.at[pl.ds(0, size)], io_sem)
                    cp.start()
                    cp.wait()
                    val = buf[pl.ds(0, size), :].astype(F32)
                    total = val if total is None else total + val
                return total

            n_full = n_rows // ch
            rem = n_rows - n_full * ch

            def store(total, start, size):
                if dst.dtype == F32:
                    buf_o[pl.ds(0, size), :] = total
                    cp = pltpu.make_async_copy(buf_o.at[pl.ds(0, size)], dst.at[pl.ds(start, size)], io_sem)
                else:
                    buf_a[pl.ds(0, size), :] = total.astype(buf_a.dtype)
                    cp = pltpu.make_async_copy(buf_a.at[pl.ds(0, size)], dst.at[pl.ds(start, size)], io_sem)
                cp.start()
                cp.wait()

            def loop_body(i, carry):
                start = pl.multiple_of(i * ch, ch)
                store(chunk(start, ch), start, ch)
                return carry

            lax.fori_loop(0, n_full, loop_body, 0)
            if rem:
                store(chunk(n_full * ch, rem), n_full * ch, rem)

        order = [2, 0, 1]
        for w in range(n_w):
            for k in range(N_CHIPS):
                remote(part(w, k, 1 - c), from_sib[w].at[k], sib_send.at[w, k], sib_recv.at[w, k],
                       sibling).wait_recv()
        for k in order:
            px, py = others[k]
            owner = 2 * px + py
            for w in range(n_w):
                add_stream(w, [part(w, owner, c), from_sib[w].at[owner]], chip_sum[w].at[owner], halves[w])
                cp = remote(chip_sum[w].at[owner], from_chips[w].at[k], chip_send.at[w, k],
                            chip_recv.at[w, k], (px, py, c))
                cp.start()
                pending.append(cp)
        for w in range(n_w):
            add_stream(w, [part(w, chip, c), from_sib[w].at[chip]], chip_sum[w].at[chip], halves[w])

        for w in range(n_w):
            for k in range(3):
                px, py = others[k]
                remote(chip_sum[w].at[chip], from_chips[w].at[k], chip_send.at[w, k], chip_recv.at[w, k],
                       (px, py, c)).wait_recv()
            mine = outs[w].at[pl.ds(pl.multiple_of(c * halves[w], 16), halves[w])]
            add_stream(w, [chip_sum[w].at[chip], from_chips[w].at[0], from_chips[w].at[1],
                           from_chips[w].at[2]], mine, halves[w])
            cp = remote(mine, mine, fin_send.at[w], fin_recv.at[w], sibling)
            cp.start()
            pending.append(cp)
        for w in range(n_w):
            theirs = outs[w].at[pl.ds(pl.multiple_of((1 - c) * halves[w], 16), halves[w])]
            remote(theirs, theirs, fin_send.at[w], fin_recv.at[w], sibling).wait_recv()

        for j in range(N_DEV - 1):
            peer = (me + N_DEV - 1 - j) % N_DEV
            remote(small_all.at[peer], small_all.at[peer], small_send.at[j], small_recv.at[j],
                   sibling).wait_recv()
        total = small_all[0]
        for d in range(1, N_DEV):
            total = total + small_all[d]
        small_out[...] = total
        for cp in pending:
            cp.wait_send()

    hbm_scratch = ([jax.ShapeDtypeStruct((N_CHIPS, h, cols), wire) for h in halves] * 2
                   + [jax.ShapeDtypeStruct((3, h, cols), wire) for h in halves])
    out_shape = ([jax.ShapeDtypeStruct((r, cols), F32) for r in rows]
                 + [jax.ShapeDtypeStruct((rows_s, 128), F32)] + hbm_scratch)
    res = pl.pallas_call(
        body, name="reduce_gradients",
        in_specs=[HBM_SPEC] * n_w + [VMEM_SPEC],
        out_specs=[HBM_SPEC] * n_w + [VMEM_SPEC] + [HBM_SPEC] * (3 * n_w),
        out_shape=out_shape,
        scratch_shapes=[
            pltpu.VMEM((N_DEV, rows_s, 128), F32),
            pltpu.VMEM((ch, cols), wire), pltpu.VMEM((ch, cols), wire), pltpu.VMEM((ch, cols), F32),
            pltpu.SemaphoreType.DMA((n_w, N_CHIPS)), pltpu.SemaphoreType.DMA((n_w, N_CHIPS)),
            pltpu.SemaphoreType.DMA((n_w, 3)), pltpu.SemaphoreType.DMA((n_w, 3)),
            pltpu.SemaphoreType.DMA((n_w,)), pltpu.SemaphoreType.DMA((n_w,)),
            pltpu.SemaphoreType.DMA((N_DEV - 1,)), pltpu.SemaphoreType.DMA((N_DEV - 1,)),
            pltpu.SemaphoreType.DMA,
        ],
        compiler_params=pltpu.CompilerParams(vmem_limit_bytes=VMEM_LIMIT_MIN),
    )(*grads, small)
    return res[:n_w], res[n_w]


def _pack_small(parts, rows):
    flat = jnp.concatenate([p.reshape(-1) for p in parts])
    flat = jnp.pad(flat, (0, rows * 128 - flat.shape[0]))
    return flat.reshape(rows, 128)


def _unpack_small(packed, shapes):
    flat = packed.reshape(-1)
    out, off = [], 0
    for shp in shapes:
        n = int(np.prod(shp))
        out.append(flat[off:off + n].reshape(shp))
        off += n
    return out


def kernel(x, g_attn, w_in, b_in, sinks_a, g_out_a, g_out_b, w_out, g_mlp, w_1, w_2, g_final, loss_target, m_g_attn, m_w_in, m_b_in, m_sinks_a, m_g_out_a, m_g_out_b, m_w_out, m_g_mlp, m_w_1, m_w_2, m_g_final, v_g_attn, v_w_in, v_b_in, v_sinks_a, v_g_out_a, v_g_out_b, v_w_out, v_g_mlp, v_w_1, v_w_2, v_g_final):
    s, d = x.shape[1], x.shape[2]
    d_in = b_in.shape[1]
    qa = g_out_a.shape[1]
    qb = g_out_b.shape[1]
    kva = 2 * N_KV_GROUPS * HEAD_DIM
    assert d_in == qa + kva + 3 * qb and qa + qb == w_out.shape[1] * N_CHIPS
    d_ff = w_1.shape[2] * N_CHIPS
    ff_shard = w_1.shape[2]
    in_shard = w_in.shape[2]
    n_heads_a, n_heads_b = qa // HEAD_DIM, qb // HEAD_DIM
    slopes_a, slopes_b = alibi_slopes(n_heads_a), alibi_slopes(n_heads_b)

    x2d = x[0]
    target = loss_target[0]

    shards = [w_in[0].T.astype(BF16), w_out[0].astype(BF16), w_1[0].astype(BF16), w_2[0].astype(BF16)]
    w_in_t, w_out_g, w_1_g, w_2_g = _gather_weights(shards)
    w_1_g = w_1_g.reshape(N_CHIPS, d, ff_shard)

    tm = _tile(s, 1024)

    h1, r1 = _norm_fwd("norm_attn", x2d, g_attn)

    q_a, = _project_by_class("proj_qa", h1, w_in_t, b_in, 0, qa, (1,))
    kv_a, = _project_by_class("proj_kva", h1, w_in_t, b_in, qa, kva, (1,))
    q_bs = _project_by_class("proj_qb", h1, w_in_t, b_in, qa + kva, qb, DILATIONS)
    k_bs = _project_by_class("proj_kb", h1, w_in_t, b_in, qa + kva + qb, qb, DILATIONS)
    v_bs = _project_by_class("proj_vb", h1, w_in_t, b_in, qa + kva + 2 * qb, qb, DILATIONS)

    sinks = sinks_a.reshape(-1)
    o_a, lse_a = _attn_fwd("attn_a_fwd", q_a, kv_a, dil=1, max_steps=WINDOW_A - 1, slopes=slopes_a, sinks=sinks)
    o_a = o_a[0]
    o_bs, lse_bs = [], []
    for n, (window, dil) in enumerate(DILATED_BRANCHES):
        o, l = _attn_fwd(f"attn_b{dil}_fwd", q_bs[n], (k_bs[n], v_bs[n]), dil=dil, max_steps=window // dil,
                         slopes=slopes_b)
        o_bs.append(o)
        lse_bs.append(l)
    mix, o_b, *lse_tot, r_a, r_b = _mix_fwd(o_a, o_bs, lse_bs, g_out_a, g_out_b)

    tn = _tile(d, 512)
    a_spec, b_spec = _mm_specs("nn", tm, tn, d)
    tile_mn = pl.BlockSpec((tm, tn), lambda i, j, k: (i, j))
    x2 = _matmul("out_proj", mix, w_out_g, [x2d], mode="nn", grid=(s // tm, d // tn, 1),
                 a_spec=a_spec, b_spec=b_spec, extra_specs=[tile_mn],
                 out_shapes=[jax.ShapeDtypeStruct((s, d), F32)], out_specs=[tile_mn],
                 epilogue=lambda acc, res: (acc + res,))[0]

    h2, r2 = _norm_fwd("norm_mlp", x2, g_mlp)

    tn = _tile(ff_shard, 512)
    per = ff_shard // tn
    a_spec, _ = _mm_specs("nn", tm, tn, d)
    u = _matmul("mlp_up", h2, w_1_g, [], mode="nn", grid=(s // tm, d_ff // tn, 1),
                a_spec=a_spec, b_spec=pl.BlockSpec((None, d, tn), lambda i, j, k: (j // per, 0, j % per)),
                extra_specs=[], out_shapes=[jax.ShapeDtypeStruct((s, d_ff), BF16)], out_specs=[tile_mn],
                epilogue=lambda acc: (jnp.maximum(acc, 0.0),))[0]

    tn = _tile(d, 1024)
    tk = _tile(d_ff, 1024)
    a_spec, b_spec = _mm_specs("nn", tm, tn, tk)
    tile_mn = pl.BlockSpec((tm, tn), lambda i, j, k: (i, j))
    x3 = _matmul("mlp_down", u, w_2_g, [x2], mode="nn", grid=(s // tm, d // tn, d_ff // tk),
                 a_spec=a_spec, b_spec=b_spec, extra_specs=[tile_mn],
                 out_shapes=[jax.ShapeDtypeStruct((s, d), F32)], out_specs=[tile_mn],
                 prologue=lambda a: a * a, epilogue=lambda acc, res: (acc + res,), acc_shape=(tm, tn))[0]

    dx3, dx3b, loss_part, dg_final = _loss_head(x3, target, g_final.reshape(1, d))

    tn = _tile(d_ff, 512)
    a_spec, b_spec = _mm_specs("nt", tm, tn, d)
    tile_mn = pl.BlockSpec((tm, tn), lambda i, j, k: (i, j))
    dpre = _matmul("mlp_down_dx", dx3b, w_2_g, [u], mode="nt", grid=(s // tm, d_ff // tn, 1),
                   a_spec=a_spec, b_spec=b_spec, extra_specs=[tile_mn],
                   out_shapes=[jax.ShapeDtypeStruct((s, d_ff), BF16)], out_specs=[tile_mn],
                   epilogue=lambda acc, uu: (acc * (2.0 * uu.astype(F32)),))[0]

    wire = GRAD_WIRE_DTYPE
    tk_s = _tile(s, 512)
    tmw = _tile(d_ff, 1024)
    a_spec, b_spec = _mm_specs("tn", tmw, d, tk_s)
    dw_2 = _matmul("mlp_down_dw", u, dx3b, [], mode="tn", grid=(d_ff // tmw, 1, s // tk_s),
                   a_spec=a_spec, b_spec=b_spec, extra_specs=[],
                   out_shapes=[jax.ShapeDtypeStruct((d_ff, d), wire)],
                   out_specs=[pl.BlockSpec((tmw, d), lambda i, j, k: (i, j))],
                   prologue=lambda a: a * a, epilogue=lambda acc: (acc,), acc_shape=(tmw, d))[0]

    tn = _tile(d, 1024)
    tk = _tile(ff_shard, 1024)
    per = ff_shard // tk
    a_spec, _ = _mm_specs("nt", tm, tn, tk)
    tile_mn = pl.BlockSpec((tm, tn), lambda i, j, k: (i, j))
    dh2 = _matmul("mlp_up_dx", dpre, w_1_g, [], mode="nt", grid=(s // tm, d // tn, d_ff // tk),
                  a_spec=a_spec, b_spec=pl.BlockSpec((None, tn, tk), lambda i, j, k: (k // per, j, k % per)),
                  extra_specs=[], out_shapes=[jax.ShapeDtypeStruct((s, d), F32)], out_specs=[tile_mn],
                  epilogue=lambda acc: (acc,), acc_shape=(tm, tn))[0]

    tmw = _tile(d, 1024)
    tnw = _tile(ff_shard, 2048)
    per = ff_shard // tnw
    a_spec, b_spec = _mm_specs("tn", tmw, tnw, tk_s)
    dw_1 = _matmul("mlp_up_dw", h2, dpre, [], mode="tn", grid=(d // tmw, d_ff // tnw, s // tk_s),
                   a_spec=a_spec, b_spec=b_spec, extra_specs=[],
                   out_shapes=[jax.ShapeDtypeStruct((N_CHIPS, d, ff_shard), wire)],
                   out_specs=[pl.BlockSpec((None, tmw, tnw), lambda i, j, k: (j // per, i, j % per))],
                   epilogue=lambda acc: (acc,), acc_shape=(tmw, tnw))[0]

    dx2, dx2b, dg_mlp = _norm_bwd("norm_mlp_bwd", dh2, x2, r2, g_mlp, dx3)

    tn = _tile(d, 512)
    a_spec, b_spec = _mm_specs("nt", tm, tn, d)
    tile_mn = pl.BlockSpec((tm, tn), lambda i, j, k: (i, j))
    dmix = _matmul("out_proj_dx", dx2b, w_out_g, [], mode="nt", grid=(s // tm, d // tn, 1),
                   a_spec=a_spec, b_spec=b_spec, extra_specs=[],
                   out_shapes=[jax.ShapeDtypeStruct((s, d), F32)], out_specs=[tile_mn],
                   epilogue=lambda acc: (acc,))[0]

    tmw = _tile(d, 1024)
    a_spec, b_spec = _mm_specs("tn", tmw, d, tk_s)
    dw_out = _matmul("out_proj_dw", mix, dx2b, [], mode="tn", grid=(d // tmw, 1, s // tk_s),
                     a_spec=a_spec, b_spec=b_spec, extra_specs=[],
                     out_shapes=[jax.ShapeDtypeStruct((d, d), wire)],
                     out_specs=[pl.BlockSpec((tmw, d), lambda i, j, k: (i, j))],
                     epilogue=lambda acc: (acc,), acc_shape=(tmw, d))[0]

    mix_grads = _mix_bwd(dmix, o_a, o_b, r_a, r_b, g_out_a, g_out_b)
    do_a, do_bs, delta_a, delta_bs = mix_grads[0], mix_grads[1:4], mix_grads[4], mix_grads[5:8]
    dg_out_a, dg_out_b = mix_grads[8:]

    dq_a, dkv_a, dsinks = _attn_bwd("attn_a_bwd", q_a, kv_a, do_a[None], lse_a, delta_a[None], dil=1,
                                    max_steps=WINDOW_A - 1, slopes=slopes_a, sinks=sinks)
    dqs, dks, dvs = [], [], []
    for n, (window, dil) in enumerate(DILATED_BRANCHES):
        dq, dk, dv = _attn_bwd(f"attn_b{dil}_bwd", q_bs[n], (k_bs[n], v_bs[n]), do_bs[n], lse_tot[n],
                               delta_bs[n], dil=dil, max_steps=window // dil, slopes=slopes_b)
        dqs.append(dq)
        dks.append(dk)
        dvs.append(dv)
    dproj, db_in = _assemble_dproj(dq_a[0], dkv_a[0], dqs, dks, dvs)

    tn = _tile(d, 512)
    a_spec, b_spec = _mm_specs("nn", tm, tn, d_in)
    tile_mn = pl.BlockSpec((tm, tn), lambda i, j, k: (i, j))
    dh1 = _matmul("in_proj_dx", dproj, w_in_t, [], mode="nn", grid=(s // tm, d // tn, 1),
                  a_spec=a_spec, b_spec=b_spec, extra_specs=[],
                  out_shapes=[jax.ShapeDtypeStruct((s, d), F32)], out_specs=[tile_mn],
                  epilogue=lambda acc: (acc,))[0]

    tmw = d_in // 2 if (d_in // 2) % 128 == 0 else d_in
    tnw = _tile(d, 1024)
    a_spec, b_spec = _mm_specs("tn", tmw, tnw, tk_s)
    dw_in_t = _matmul("in_proj_dw", dproj, h1, [], mode="tn", grid=(d_in // tmw, d // tnw, s // tk_s),
                      a_spec=a_spec, b_spec=b_spec, extra_specs=[],
                      out_shapes=[jax.ShapeDtypeStruct((d_in, d), wire)],
                      out_specs=[pl.BlockSpec((tmw, tnw), lambda i, j, k: (i, j))],
                      epilogue=lambda acc: (acc,), acc_shape=(tmw, tnw))[0]

    grad_x, _, dg_attn = _norm_bwd("norm_attn_bwd", dh1, x2d, r1, g_attn, dx2)

    small_parts = [dg_attn, db_in, dsinks[:, :n_heads_a], dg_out_a, dg_out_b, dg_mlp, dg_final]
    small_shapes = [g_attn.shape, b_in.shape, sinks_a.shape, g_out_a.shape, g_out_b.shape, g_mlp.shape,
                    g_final.shape]
    n_small = sum(int(np.prod(shp)) for shp in small_shapes)
    rows_s = -(-n_small // (8 * 128)) * 8
    (gw_in_t, gw_out, gw_1, gw_2), small_sum = _reduce_gradients(
        [dw_in_t, dw_out, dw_1.reshape(N_CHIPS * d, ff_shard), dw_2], _pack_small(small_parts, rows_s))
    gw_in = gw_in_t.T
    g_small = _unpack_small(small_sum, small_shapes)

    upd_in = _adamw("adamw_w_in", w_in[0], gw_in, m_w_in[0], v_w_in[0])
    upd_out = _adamw("adamw_w_out", w_out[0], gw_out, m_w_out[0], v_w_out[0])
    upd_1 = _adamw("adamw_w_1", w_1[0], gw_1, m_w_1[0], v_w_1[0])
    upd_2 = _adamw("adamw_w_2", w_2[0], gw_2, m_w_2[0], v_w_2[0])
    small_w = [g_attn, b_in, sinks_a, g_out_a, g_out_b, g_mlp, g_final]
    small_m = [m_g_attn, m_b_in, m_sinks_a, m_g_out_a, m_g_out_b, m_g_mlp, m_g_final]
    small_v = [v_g_attn, v_b_in, v_sinks_a, v_g_out_a, v_g_out_b, v_g_mlp, v_g_final]
    upd_small = _adamw("adamw_small", _pack_small(small_w, rows_s), small_sum,
                       _pack_small(small_m, rows_s), _pack_small(small_v, rows_s))
    d_small, m_small, v_small = [_unpack_small(t, small_shapes) for t in upd_small]

    loss = lax.psum(loss_part[0, 0], ("x", "y", "c"))

    def ordered(small, big):
        w_in_v, w_out_v, w_1_v, w_2_v = big
        return [small[0], w_in_v[None], small[1], small[2], small[3], small[4], w_out_v[None], small[5],
                w_1_v[None], w_2_v[None], small[6]]

    grads = ordered(g_small, (gw_in, gw_out, gw_1, gw_2))
    deltas = ordered(d_small, (upd_in[0], upd_out[0], upd_1[0], upd_2[0]))
    new_m = ordered(m_small, (upd_in[1], upd_out[1], upd_1[1], upd_2[1]))
    new_v = ordered(v_small, (upd_in[2], upd_out[2], upd_1[2], upd_2[2]))
    return (loss, grad_x[None], *grads, *deltas, *new_m, *new_v)
```

```python
import functools

import jax
import jax.numpy as jnp
import numpy as np
from jax import lax
from jax.experimental import pallas as pl
from jax.experimental.pallas import tpu as pltpu

F32 = jnp.float32
BF16 = jnp.bfloat16

HEAD_DIM = 64
BLOCK = 128
PAIR = 2 * HEAD_DIM
N_KV_GROUPS = 2
WINDOW_A = 128
DILATED_BRANCHES = ((128, 1), (512, 4), (2048, 16))
EPS = 1e-5
NEG_INF = -1e30
ATT_SCALE = HEAD_DIM ** -0.5

ADAM_LR = 0.001
ADAM_B1 = 0.9
ADAM_B2 = 0.999
ADAM_EPS = 1e-08
ADAM_WD = 0.01
ADAM_STEP = 10

N_CHIPS = 4
N_DEV = 8
MESH = pl.DeviceIdType.MESH
GRAD_WIRE_DTYPE = jnp.bfloat16

VMEM_CAPACITY_V7X = 64 * 1024 * 1024
VMEM_LIMIT_MAX = 56 * 1024 * 1024
VMEM_LIMIT_MIN = 32 * 1024 * 1024

HBM_SPEC = pl.BlockSpec(memory_space=pltpu.HBM)
VMEM_SPEC = pl.BlockSpec(memory_space=pltpu.VMEM)
SMEM_SPEC = pl.BlockSpec(memory_space=pltpu.SMEM)


def _nbytes(shape, dtype):
    return int(np.prod([s for s in shape if s is not None])) * jnp.dtype(dtype).itemsize


def _params(semantics, block_bytes):
    limit = min(max(2 * block_bytes + (4 << 20), VMEM_LIMIT_MIN), VMEM_LIMIT_MAX)
    return pltpu.CompilerParams(dimension_semantics=semantics, vmem_limit_bytes=limit)


def alibi_slopes(n):
    return [float(v) for v in np.asarray(2.0 ** (-8.0 * (np.arange(n) + 1) / n), dtype=np.float32)]


def _matmul(name, a, b, extras, *, mode, grid, a_spec, b_spec, extra_specs, out_shapes, out_specs,
            epilogue, prologue=None, acc_shape=None):
    dims = {"nn": ((1,), (0,)), "nt": ((1,), (1,)), "tn": ((0,), (0,))}[mode]
    nk = grid[2]
    n_ex, n_out = len(extras), len(out_shapes)

    def body(a_ref, b_ref, *rest):
        ex, outs = rest[:n_ex], rest[n_ex:n_ex + n_out]
        av = a_ref[...]
        if prologue is not None:
            av = prologue(av)
        part = lax.dot_general(av, b_ref[...], (dims, ((), ())), preferred_element_type=F32)

        def finish(acc):
            res = epilogue(acc, *[e[...] for e in ex])
            for o, r in zip(outs, res):
                o[...] = r.astype(o.dtype)

        if nk == 1:
            finish(part)
        else:
            acc_ref = rest[-1]
            k = pl.program_id(2)

            @pl.when(k == 0)
            def _():
                acc_ref[...] = part

            @pl.when(k > 0)
            def _():
                acc_ref[...] += part

            @pl.when(k == nk - 1)
            def _():
                finish(acc_ref[...])

    blocks = [(a_spec.block_shape, a.dtype), (b_spec.block_shape, b.dtype)]
    blocks += [(s.block_shape, e.dtype) for s, e in zip(extra_specs, extras)]
    blocks += [(s.block_shape, o.dtype) for s, o in zip(out_specs, out_shapes)]
    nbytes = sum(_nbytes(s, d) for s, d in blocks)
    scratch = []
    if nk > 1:
        scratch.append(pltpu.VMEM(acc_shape, F32))
        nbytes += _nbytes(acc_shape, F32)
    res = pl.pallas_call(
        body, name=name, grid=grid,
        in_specs=[a_spec, b_spec, *extra_specs], out_specs=list(out_specs), out_shape=list(out_shapes),
        scratch_shapes=scratch,
        compiler_params=_params(("parallel", "parallel", "arbitrary"), nbytes),
    )(a, b, *extras)
    return res


def _mm_specs(mode, tm, tn, tk, b_block=None, b_map=None):
    if mode == "tn":
        a_spec = pl.BlockSpec((tk, tm), lambda i, j, k: (k, i))
    else:
        a_spec = pl.BlockSpec((tm, tk), lambda i, j, k: (i, k))
    if b_block is not None:
        b_spec = pl.BlockSpec(b_block, b_map)
    elif mode == "nt":
        b_spec = pl.BlockSpec((tn, tk), lambda i, j, k: (j, k))
    else:
        b_spec = pl.BlockSpec((tk, tn), lambda i, j, k: (k, j))
    return a_spec, b_spec


def _project_by_class(name, h, w_t, bias, row_off, width, dilations):
    s, d = h.shape
    tm = _tile(s, 1024)
    tn = 512 if width % 512 == 0 and row_off % 512 == 0 else _tile(width, 256)
    off = row_off // tn
    assert row_off % tn == 0 and tn % 128 == 0
    n_out = len(dilations)

    def body(h_ref, w_ref, b_ref, *rest):
        outs, perm_ref = rest[:n_out], rest[n_out]
        acc = lax.dot_general(h_ref[...], w_ref[...], (((1,), (1,)), ((), ())), preferred_element_type=F32)
        acc = acc + b_ref[...]
        for j in range(tn // 128):
            cols = slice(j * 128, (j + 1) * 128)
            for o_ref, dil in zip(outs, dilations):
                _to_classes(o_ref, cols, acc[:, cols], perm_ref, dil)

    blocks = tm * d * 2 + tn * d * 2 + 3 * tm * tn * 2 + tm * 128 * 4
    return pl.pallas_call(
        body, name=name, grid=(s // tm, width // tn),
        in_specs=[pl.BlockSpec((tm, d), lambda i, j: (i, 0)), pl.BlockSpec((tn, d), lambda i, j: (j + off, 0)),
                  pl.BlockSpec((1, tn), lambda i, j: (0, j + off))],
        out_specs=[pl.BlockSpec((dil, tm // dil, tn), lambda i, j: (0, i, j)) for dil in dilations],
        out_shape=[_class_shape(dil, s, width, BF16) for dil in dilations],
        scratch_shapes=[pltpu.VMEM((tm, 128), F32)],
        compiler_params=_params(("parallel", "parallel"), blocks),
    )(h, w_t, bias)


def _tile(n, want):
    if n <= want:
        return n
    t = (want // 128) * 128
    while t > 128 and n % t:
        t -= 128
    assert n % t == 0, (n, want)
    return t


def _row_tile(s):
    return 256 if s % 256 == 0 else s


def _norm_fwd(name, x, g):
    s, d = x.shape
    tm = _row_tile(s)

    def body(x_ref, g_ref, h_ref, r_ref):
        xv = x_ref[...]
        r = lax.rsqrt(jnp.mean(xv * xv, axis=-1, keepdims=True) + EPS)
        h_ref[...] = ((xv * r) * g_ref[...]).astype(BF16)
        r_ref[...] = r

    row = pl.BlockSpec((tm, d), lambda i: (i, 0))
    return pl.pallas_call(
        body, name=name, grid=(s // tm,),
        in_specs=[row, pl.BlockSpec((1, d), lambda i: (0, 0))],
        out_specs=[row, pl.BlockSpec((tm, 1), lambda i: (i, 0))],
        out_shape=[jax.ShapeDtypeStruct((s, d), BF16), jax.ShapeDtypeStruct((s, 1), F32)],
        compiler_params=_params(("parallel",), tm * d * 6),
    )(x, g)


def _norm_bwd(name, dh, x, r, g, dres):
    s, d = x.shape
    tm = _row_tile(s)

    def body(dh_ref, x_ref, r_ref, g_ref, dres_ref, dx_ref, dxb_ref, dg_ref):
        rv = r_ref[...]
        xn = x_ref[...] * rv
        dhv = dh_ref[...]
        dxn = dhv * g_ref[...]
        dx = dres_ref[...] + rv * (dxn - xn * jnp.mean(dxn * xn, axis=-1, keepdims=True))
        dx_ref[...] = dx
        dxb_ref[...] = dx.astype(BF16)
        part = jnp.sum(dhv * xn, axis=0, keepdims=True)

        @pl.when(pl.program_id(0) == 0)
        def _():
            dg_ref[...] = part

        @pl.when(pl.program_id(0) > 0)
        def _():
            dg_ref[...] += part

    row = pl.BlockSpec((tm, d), lambda i: (i, 0))
    vec = pl.BlockSpec((1, d), lambda i: (0, 0))
    return pl.pallas_call(
        body, name=name, grid=(s // tm,),
        in_specs=[row, row, pl.BlockSpec((tm, 1), lambda i: (i, 0)), vec, row],
        out_specs=[row, row, vec],
        out_shape=[jax.ShapeDtypeStruct((s, d), F32), jax.ShapeDtypeStruct((s, d), BF16),
                   jax.ShapeDtypeStruct((1, d), F32)],
        compiler_params=_params(("arbitrary",), tm * d * 18),
    )(dh, x, r, g, dres)


def _loss_head(x3, target, g):
    s, d = x3.shape
    tm = _row_tile(s)

    def body(x_ref, t_ref, g_ref, dx_ref, dxb_ref, loss_ref, dg_ref):
        xv = x_ref[...]
        gv = g_ref[...]
        r = lax.rsqrt(jnp.mean(xv * xv, axis=-1, keepdims=True) + EPS)
        xn = xv * r
        err = xn * gv - t_ref[...]
        loss = 0.5 * jnp.sum(jnp.mean(err * err, axis=-1, keepdims=True), axis=0, keepdims=True)
        dy = err / d
        dxn = dy * gv
        dx = r * (dxn - xn * jnp.mean(dxn * xn, axis=-1, keepdims=True))
        dx_ref[...] = dx
        dxb_ref[...] = dx.astype(BF16)
        dg = jnp.sum(dy * xn, axis=0, keepdims=True)
        loss_row = jnp.broadcast_to(loss, (1, 128))

        @pl.when(pl.program_id(0) == 0)
        def _():
            dg_ref[...] = dg
            loss_ref[...] = loss_row

        @pl.when(pl.program_id(0) > 0)
        def _():
            dg_ref[...] += dg
            loss_ref[...] += loss_row

    row = pl.BlockSpec((tm, d), lambda i: (i, 0))
    vec = pl.BlockSpec((1, d), lambda i: (0, 0))
    return pl.pallas_call(
        body, name="loss_head", grid=(s // tm,),
        in_specs=[row, row, vec],
        out_specs=[row, row, pl.BlockSpec((1, 128), lambda i: (0, 0)), vec],
        out_shape=[jax.ShapeDtypeStruct((s, d), F32), jax.ShapeDtypeStruct((s, d), BF16),
                   jax.ShapeDtypeStruct((1, 128), F32), jax.ShapeDtypeStruct((1, d), F32)],
        compiler_params=_params(("arbitrary",), tm * d * 14),
    )(x3, target, g)


def _low_lanes(rows):
    return lax.broadcasted_iota(jnp.int32, (rows, PAIR), 1) < HEAD_DIM


def _to_classes(dst_ref, cols, value, perm_ref, dil):
    rows = value.shape[0]
    if dil == 1:
        dst_ref[0, :, cols] = value.astype(dst_ref.dtype)
        return
    perm_ref[...] = value
    for r in range(dil):
        dst_ref[r, :, cols] = perm_ref[pl.ds(r, rows // dil, stride=dil), :].astype(dst_ref.dtype)


def _from_classes(src_ref, cols, perm_ref, dil):
    if dil == 1:
        return src_ref[0, :, cols]
    rows = perm_ref.shape[0]
    for r in range(dil):
        perm_ref[pl.ds(r, rows // dil, stride=dil), :] = src_ref[r, :, cols]
    return perm_ref[...]


def _class_spec(dil, tm, width):
    return pl.BlockSpec((dil, tm // dil, width), lambda i: (0, i, 0))


def _class_shape(dil, s, width, dtype):
    return jax.ShapeDtypeStruct((dil, s // dil, width), dtype)


DILATIONS = tuple(d for _, d in DILATED_BRANCHES)


def _mix_fwd(oa, obs, lses, ga, gb):
    s, qa = oa.shape
    qb = obs[0].shape[2]
    tm = _row_tile(s)
    all_lanes = slice(0, 128)

    def body(oa_ref, o1_ref, o2_ref, o3_ref, l1_ref, l2_ref, l3_ref, ga_ref, gb_ref,
             mix_ref, ob_ref, t1_ref, t2_ref, t3_ref, ra_ref, rb_ref, perm_ref):
        oav = oa_ref[...]
        ra = lax.rsqrt(jnp.mean(oav * oav, axis=-1, keepdims=True) + EPS)
        ra_ref[...] = ra
        mix_ref[:, 0:qa] = ((oav * ra) * ga_ref[...]).astype(BF16)
        l1, l2, l3 = [_from_classes(l_ref, all_lanes, perm_ref, dil)
                      for l_ref, dil in zip((l1_ref, l2_ref, l3_ref), DILATIONS)]
        mx = jnp.maximum(jnp.maximum(l1, l2), l3)
        e1, e2, e3 = jnp.exp(l1 - mx), jnp.exp(l2 - mx), jnp.exp(l3 - mx)
        tot = e1 + e2 + e3
        lse = mx + jnp.log(tot)
        for t_ref, dil in zip((t1_ref, t2_ref, t3_ref), DILATIONS):
            _to_classes(t_ref, all_lanes, lse, perm_ref, dil)
        ws = (e1 / tot, e2 / tot, e3 / tot)
        low = _low_lanes(tm)
        ssq = jnp.zeros((tm, 1), F32)
        for i in range(qb // PAIR):
            sl = slice(i * PAIR, (i + 1) * PAIR)
            acc = jnp.zeros((tm, PAIR), F32)
            for w, o_ref, dil in zip(ws, (o1_ref, o2_ref, o3_ref), DILATIONS):
                wexp = jnp.where(low, w[:, 2 * i:2 * i + 1], w[:, 2 * i + 1:2 * i + 2])
                acc = acc + wexp * _from_classes(o_ref, sl, perm_ref, dil)
            ob_ref[:, sl] = acc
            ssq = ssq + jnp.sum(acc * acc, axis=-1, keepdims=True)
        rb = lax.rsqrt(ssq / qb + EPS)
        rb_ref[...] = rb
        mix_ref[:, qa:qa + qb] = ((ob_ref[...] * rb) * gb_ref[...]).astype(BF16)

    def row(w):
        return pl.BlockSpec((tm, w), lambda i: (i, 0))

    def vec(w):
        return pl.BlockSpec((1, w), lambda i: (0, 0))

    return pl.pallas_call(
        body, name="mix_fwd", grid=(s // tm,),
        in_specs=([row(qa)] + [_class_spec(d, tm, qb) for d in DILATIONS]
                  + [_class_spec(d, tm, 128) for d in DILATIONS] + [vec(qa), vec(qb)]),
        out_specs=([row(qa + qb), row(qb)] + [_class_spec(d, tm, 128) for d in DILATIONS] + [row(1), row(1)]),
        out_shape=([jax.ShapeDtypeStruct((s, qa + qb), BF16), jax.ShapeDtypeStruct((s, qb), F32)]
                   + [_class_shape(d, s, 128, F32) for d in DILATIONS]
                   + [jax.ShapeDtypeStruct((s, 1), F32), jax.ShapeDtypeStruct((s, 1), F32)]),
        scratch_shapes=[pltpu.VMEM((tm, 128), F32)],
        compiler_params=_params(("parallel",), tm * (qa + 4 * qb) * 4 + tm * (qa + qb) * 2 + tm * 4096),
    )(oa, *obs, *lses, ga, gb)


def _head_rowsums(prod, rows):
    low = _low_lanes(rows)
    lane = lax.broadcasted_iota(jnp.int32, (rows, 128), 1)
    out = jnp.zeros((rows, 128), F32)
    for i in range(prod.shape[1] // PAIR):
        tile = prod[:, i * PAIR:(i + 1) * PAIR]
        lo = jnp.sum(jnp.where(low, tile, 0.0), axis=-1, keepdims=True)
        hi = jnp.sum(jnp.where(low, 0.0, tile), axis=-1, keepdims=True)
        out = jnp.where(lane == 2 * i, lo, out)
        out = jnp.where(lane == 2 * i + 1, hi, out)
    return out


def _mix_bwd(dmix, oa, ob, ra, rb, ga, gb):
    s, qa = oa.shape
    qb = ob.shape[1]
    tm = _row_tile(s)

    def one(dy, o, r, g):
        xn = o * r
        dxn = dy * g
        do = r * (dxn - xn * jnp.mean(dxn * xn, axis=-1, keepdims=True))
        return do, jnp.sum(dy * xn, axis=0, keepdims=True), _head_rowsums(do * o, tm)

    def body(dmix_ref, oa_ref, ob_ref, ra_ref, rb_ref, ga_ref, gb_ref,
             doa_ref, dob1_ref, dob2_ref, dob3_ref, dla_ref, dlb1_ref, dlb2_ref, dlb3_ref,
             dga_ref, dgb_ref, perm_ref):
        doa, dga, dla = one(dmix_ref[:, 0:qa], oa_ref[...], ra_ref[...], ga_ref[...])
        dob, dgb, dlb = one(dmix_ref[:, qa:qa + qb], ob_ref[...], rb_ref[...], gb_ref[...])
        doa_ref[...] = doa.astype(BF16)
        dla_ref[...] = dla
        for dob_ref, dlb_ref, dil in zip((dob1_ref, dob2_ref, dob3_ref), (dlb1_ref, dlb2_ref, dlb3_ref),
                                         DILATIONS):
            _to_classes(dlb_ref, slice(0, 128), dlb, perm_ref, dil)
            for i in range(qb // PAIR):
                sl = slice(i * PAIR, (i + 1) * PAIR)
                _to_classes(dob_ref, sl, dob[:, sl], perm_ref, dil)

        @pl.when(pl.program_id(0) == 0)
        def _():
            dga_ref[...] = dga
            dgb_ref[...] = dgb

        @pl.when(pl.program_id(0) > 0)
        def _():
            dga_ref[...] += dga
            dgb_ref[...] += dgb

    def row(w):
        return pl.BlockSpec((tm, w), lambda i: (i, 0))

    def vec(w):
        return pl.BlockSpec((1, w), lambda i: (0, 0))

    return pl.pallas_call(
        body, name="mix_bwd", grid=(s // tm,),
        in_specs=[row(qa + qb), row(qa), row(qb), row(1), row(1), vec(qa), vec(qb)],
        out_specs=([row(qa)] + [_class_spec(d, tm, qb) for d in DILATIONS] + [row(128)]
                   + [_class_spec(d, tm, 128) for d in DILATIONS] + [vec(qa), vec(qb)]),
        out_shape=([jax.ShapeDtypeStruct((s, qa), BF16)] + [_class_shape(d, s, qb, BF16) for d in DILATIONS]
                   + [jax.ShapeDtypeStruct((s, 128), F32)] + [_class_shape(d, s, 128, F32) for d in DILATIONS]
                   + [jax.ShapeDtypeStruct((1, qa), F32), jax.ShapeDtypeStruct((1, qb), F32)]),
        scratch_shapes=[pltpu.VMEM((tm, 128), F32)],
        compiler_params=_params(("arbitrary",), tm * (qa + qb) * 16),
    )(dmix, oa, ob, ra, rb, ga, gb)


def _assemble_dproj(dqa, dkva, dqs, dks, dvs):
    s, qa = dqa.shape
    kva = dkva.shape[1]
    qb = dqs[0].shape[2]
    width = qa + kva + 3 * qb
    tm = _row_tile(s)

    def body(dqa_ref, dkva_ref, q1, q2, q3, k1, k2, k3, v1, v2, v3, dp_ref, db_ref, perm_ref):
        first = pl.program_id(0) == 0

        def emit(off, val):
            dp_ref[:, off:off + PAIR] = val.astype(BF16)
            col = jnp.sum(val, axis=0, keepdims=True)

            @pl.when(first)
            def _():
                db_ref[:, off:off + PAIR] = col

            @pl.when(jnp.logical_not(first))
            def _():
                db_ref[:, off:off + PAIR] += col

        for i in range(qa // PAIR):
            emit(i * PAIR, dqa_ref[:, i * PAIR:(i + 1) * PAIR])
        for i in range(kva // PAIR):
            emit(qa + i * PAIR, dkva_ref[:, i * PAIR:(i + 1) * PAIR])
        for j, branch_refs in enumerate(((q1, q2, q3), (k1, k2, k3), (v1, v2, v3))):
            for i in range(qb // PAIR):
                sl = slice(i * PAIR, (i + 1) * PAIR)
                total = None
                for ref, dil in zip(branch_refs, DILATIONS):
                    val = _from_classes(ref, sl, perm_ref, dil)
                    total = val if total is None else total + val
                emit(qa + kva + j * qb + i * PAIR, total)

    def row(w):
        return pl.BlockSpec((tm, w), lambda i: (i, 0))

    return pl.pallas_call(
        body, name="assemble_dproj", grid=(s // tm,),
        in_specs=[row(qa), row(kva)] + [_class_spec(d, tm, qb) for d in DILATIONS] * 3,
        out_specs=[row(width), pl.BlockSpec((1, width), lambda i: (0, 0))],
        out_shape=[jax.ShapeDtypeStruct((s, width), BF16), jax.ShapeDtypeStruct((1, width), F32)],
        scratch_shapes=[pltpu.VMEM((tm, 128), F32)],
        compiler_params=_params(("arbitrary",), tm * (qa + kva + 9 * qb) * 4 + tm * width * 2),
    )(dqa, dkva, *dqs, *dks, *dvs)


def _band_masks(max_steps, dil, first_block):
    qi = lax.broadcasted_iota(jnp.int32, (BLOCK, 2 * BLOCK), 0)
    kj = lax.broadcasted_iota(jnp.int32, (BLOCK, 2 * BLOCK), 1)
    steps = qi + BLOCK - kj
    valid = (steps >= 0) & (steps <= max_steps) & ((kj >= BLOCK) | jnp.logical_not(first_block))
    dist = (steps * dil).astype(F32)
    return valid, dist


def _swap_halves(t):
    return pltpu.roll(t, HEAD_DIM, 1)


def _dup_group(t_bf16, group):
    t = t_bf16.astype(F32)
    low = lax.broadcasted_iota(jnp.int32, t.shape, 1) < HEAD_DIM
    keep = low if group == 0 else jnp.logical_not(low)
    return jnp.where(keep, t, _swap_halves(t)).astype(BF16)


def _attn_fwd(name, q, kv, *, dil, max_steps, slopes, sinks=None):
    grouped = sinks is not None
    _, length, w = q.shape
    n_pairs = w // PAIR
    nb = length // BLOCK
    heads_per_group = 2 * n_pairs // N_KV_GROUPS

    def body(*refs):
        if grouped:
            sink_ref, q_ref, kvp_ref, kvc_ref, o_ref, lse_ref = refs
        else:
            q_ref, kp_ref, kc_ref, vp_ref, vc_ref, o_ref, lse_ref = refs
        n = pl.program_id(1)
        valid, dist = _band_masks(max_steps, dil, n == 0)
        low = _low_lanes(BLOCK)
        lane = lax.broadcasted_iota(jnp.int32, (BLOCK, 128), 1)
        lse_acc = jnp.zeros((BLOCK, 128), F32)
        if grouped:
            kv_all = jnp.concatenate([kvp_ref[...], kvc_ref[...]], axis=0)
            k_dup = [_dup_group(kv_all[:, 0:PAIR], g) for g in range(N_KV_GROUPS)]
            v_dup = [_dup_group(kv_all[:, PAIR:2 * PAIR], g) for g in range(N_KV_GROUPS)]
        for i in range(n_pairs):
            sl = slice(i * PAIR, (i + 1) * PAIR)
            q2 = q_ref[:, sl]
            if grouped:
                kk, vv = k_dup[2 * i // heads_per_group], v_dup[2 * i // heads_per_group]
            else:
                kk = jnp.concatenate([kp_ref[:, sl], kc_ref[:, sl]], axis=0)
                vv = jnp.concatenate([vp_ref[:, sl], vc_ref[:, sl]], axis=0)
            outs = []
            for half in (0, 1):
                h = 2 * i + half
                qm = jnp.where(low if half == 0 else jnp.logical_not(low), q2, jnp.zeros_like(q2))
                sc = lax.dot_general(qm, kk, (((1,), (1,)), ((), ())), preferred_element_type=F32)
                sc = jnp.where(valid, sc * ATT_SCALE - slopes[h] * dist, NEG_INF)
                m = jnp.max(sc, axis=-1, keepdims=True)
                if grouped:
                    m = jnp.maximum(m, sink_ref[h])
                p = jnp.exp(sc - m)
                den = jnp.sum(p, axis=-1, keepdims=True)
                if grouped:
                    den = den + jnp.exp(sink_ref[h] - m)
                o = jnp.dot(p.astype(BF16), vv, preferred_element_type=F32)
                outs.append(o / den)
                lse_acc = jnp.where(lane == h, m + jnp.log(den), lse_acc)
            o_ref[:, sl] = jnp.where(low, outs[0], outs[1])
        lse_ref[...] = lse_acc

    def cur(width):
        return pl.BlockSpec((None, BLOCK, width), lambda r, n: (r, n, 0))

    def prev(width):
        return pl.BlockSpec((None, BLOCK, width), lambda r, n: (r, jnp.maximum(n - 1, 0), 0))

    if grouped:
        kvw = kv.shape[2]
        operands = [sinks, q, kv, kv]
        in_specs = [SMEM_SPEC, cur(w), prev(kvw), cur(kvw)]
    else:
        operands = [q, kv[0], kv[0], kv[1], kv[1]]
        in_specs = [cur(w), prev(w), cur(w), prev(w), cur(w)]
    return pl.pallas_call(
        body, name=name, grid=(dil, nb), in_specs=in_specs,
        out_specs=[cur(w), cur(128)],
        out_shape=[jax.ShapeDtypeStruct((dil, length, w), F32),
                   jax.ShapeDtypeStruct((dil, length, 128), F32)],
        compiler_params=_params(("parallel", "parallel"), BLOCK * w * 16),
    )(*operands)


def _attn_bwd(name, q, kv, do, lse, delta, *, dil, max_steps, slopes, sinks=None):
    grouped = sinks is not None
    _, length, w = q.shape
    n_pairs = w // PAIR
    nb = length // BLOCK
    heads_per_group = 2 * n_pairs // N_KV_GROUPS
    pairs_per_group = n_pairs // N_KV_GROUPS

    def body(*refs):
        if grouped:
            (sink_ref, q_ref, kvp_ref, kvc_ref, do_ref, lse_ref, dl_ref,
             dq_ref, dkv_ref, dsink_ref, acc_ref) = refs
        else:
            (q_ref, kp_ref, kc_ref, vp_ref, vc_ref, do_ref, lse_ref, dl_ref,
             dq_ref, dk_ref, dv_ref, acck_ref, accv_ref) = refs
        n = pl.program_id(1)

        @pl.when(n == 0)
        def _():
            if grouped:
                acc_ref[...] = jnp.zeros_like(acc_ref)

                @pl.when(pl.program_id(0) == 0)
                def _():
                    dsink_ref[...] = jnp.zeros_like(dsink_ref)
            else:
                acck_ref[...] = jnp.zeros_like(acck_ref)
                accv_ref[...] = jnp.zeros_like(accv_ref)

        @pl.when(n == nb)
        def _():
            if grouped:
                dkv_ref[...] = acc_ref[...]
            else:
                dk_ref[...] = acck_ref[...]
                dv_ref[...] = accv_ref[...]

        @pl.when(n < nb)
        def _():
            valid, dist = _band_masks(max_steps, dil, n == 0)
            low = _low_lanes(BLOCK)
            low_kv = _low_lanes(2 * BLOCK)
            lane1 = lax.broadcasted_iota(jnp.int32, (1, 128), 1)
            if grouped:
                kv_all = jnp.concatenate([kvp_ref[...], kvc_ref[...]], axis=0)
                k_dup = [_dup_group(kv_all[:, 0:PAIR], g) for g in range(N_KV_GROUPS)]
                v_dup = [_dup_group(kv_all[:, PAIR:2 * PAIR], g) for g in range(N_KV_GROUPS)]
                dk_grp = [jnp.zeros((2 * BLOCK, PAIR), F32) for _ in range(N_KV_GROUPS)]
                dv_grp = [jnp.zeros((2 * BLOCK, PAIR), F32) for _ in range(N_KV_GROUPS)]
                dsink = jnp.zeros((1, 128), F32)
            for i in range(n_pairs):
                sl = slice(i * PAIR, (i + 1) * PAIR)
                q2 = q_ref[:, sl]
                do2 = do_ref[:, sl]
                if grouped:
                    grp = 2 * i // heads_per_group
                    kk, vv = k_dup[grp], v_dup[grp]
                else:
                    kk = jnp.concatenate([kp_ref[:, sl], kc_ref[:, sl]], axis=0)
                    vv = jnp.concatenate([vp_ref[:, sl], vc_ref[:, sl]], axis=0)
                dkk = jnp.zeros((2 * BLOCK, PAIR), F32)
                dvv = jnp.zeros((2 * BLOCK, PAIR), F32)
                dqs = []
                for half in (0, 1):
                    h = 2 * i + half
                    keep = low if half == 0 else jnp.logical_not(low)
                    qm = jnp.where(keep, q2, jnp.zeros_like(q2))
                    dom = jnp.where(keep, do2, jnp.zeros_like(do2))
                    lse_h = lse_ref[:, h:h + 1]
                    dl_h = dl_ref[:, h:h + 1]
                    sc = lax.dot_general(qm, kk, (((1,), (1,)), ((), ())), preferred_element_type=F32)
                    sc = jnp.where(valid, sc * ATT_SCALE - slopes[h] * dist, NEG_INF)
                    p = jnp.exp(sc - lse_h)
                    dp = lax.dot_general(dom, vv, (((1,), (1,)), ((), ())), preferred_element_type=F32)
                    ds = (p * (dp - dl_h)).astype(BF16)
                    dqs.append(jnp.dot(ds, kk, preferred_element_type=F32))
                    dkk = dkk + lax.dot_general(ds, qm, (((0,), (0,)), ((), ())),
                                                preferred_element_type=F32)
                    dvv = dvv + lax.dot_general(p.astype(BF16), dom, (((0,), (0,)), ((), ())),
                                                preferred_element_type=F32)
                    if grouped:
                        contrib = -jnp.sum(jnp.exp(sink_ref[h] - lse_h) * dl_h, axis=0, keepdims=True)
                        dsink = jnp.where(lane1 == h, dsink + contrib, dsink)
                dq_ref[:, sl] = jnp.where(low, dqs[0], dqs[1]) * ATT_SCALE
                if grouped:
                    dk_grp[grp] = dk_grp[grp] + dkk
                    dv_grp[grp] = dv_grp[grp] + dvv
                else:
                    dkk = dkk * ATT_SCALE
                    dk_ref[:, sl] = acck_ref[:, sl] + dkk[0:BLOCK]
                    acck_ref[:, sl] = dkk[BLOCK:2 * BLOCK]
                    dv_ref[:, sl] = accv_ref[:, sl] + dvv[0:BLOCK]
                    accv_ref[:, sl] = dvv[BLOCK:2 * BLOCK]
            if grouped:
                folded = [t + _swap_halves(t) for t in dk_grp + dv_grp]
                dk_tile = jnp.where(low_kv, folded[0], folded[1]) * ATT_SCALE
                dv_tile = jnp.where(low_kv, folded[2], folded[3])
                part = jnp.concatenate([dk_tile, dv_tile], axis=1)
                dkv_ref[...] = acc_ref[...] + part[0:BLOCK]
                acc_ref[...] = part[BLOCK:2 * BLOCK]
                dsink_ref[...] += dsink

    last = nb - 1

    def cur(width):
        return pl.BlockSpec((None, BLOCK, width), lambda r, n: (r, jnp.minimum(n, last), 0))

    def prev(width):
        return pl.BlockSpec((None, BLOCK, width),
                            lambda r, n: (r, jnp.maximum(jnp.minimum(n, last) - 1, 0), 0))

    def done(width):
        return pl.BlockSpec((None, BLOCK, width), lambda r, n: (r, jnp.maximum(n - 1, 0), 0))

    if grouped:
        assert pairs_per_group * N_KV_GROUPS == n_pairs and heads_per_group % 2 == 0
        kvw = kv.shape[2]
        operands = [sinks, q, kv, kv, do, lse, delta]
        in_specs = [SMEM_SPEC, cur(w), prev(kvw), cur(kvw), cur(w), cur(128), cur(128)]
        out_specs = [cur(w), done(kvw), pl.BlockSpec((1, 128), lambda r, n: (0, 0))]
        out_shape = [jax.ShapeDtypeStruct((dil, length, w), F32), jax.ShapeDtypeStruct((dil, length, kvw), F32),
                     jax.ShapeDtypeStruct((1, 128), F32)]
        scratch = [pltpu.VMEM((BLOCK, kvw), F32)]
    else:
        operands = [q, kv[0], kv[0], kv[1], kv[1], do, lse, delta]
        in_specs = [cur(w), prev(w), cur(w), prev(w), cur(w), cur(w), cur(128), cur(128)]
        out_specs = [cur(w), done(w), done(w)]
        out_shape = [jax.ShapeDtypeStruct((dil, length, w), F32)] * 3
        scratch = [pltpu.VMEM((BLOCK, w), F32), pltpu.VMEM((BLOCK, w), F32)]
    return pl.pallas_call(
        body, name=name, grid=(dil, nb + 1), in_specs=in_specs, out_specs=out_specs,
        out_shape=out_shape, scratch_shapes=scratch,
        compiler_params=_params(("arbitrary", "arbitrary"), BLOCK * w * 32),
    )(*operands)


def _adamw(name, w, g, m, v):
    rows, cols = w.shape
    tm = 256 if rows % 256 == 0 else rows

    def body(w_ref, g_ref, m_ref, v_ref, d_ref, nm_ref, nv_ref):
        gv = g_ref[...]
        mn = ADAM_B1 * m_ref[...] + (1.0 - ADAM_B1) * gv
        vn = ADAM_B2 * v_ref[...] + (1.0 - ADAM_B2) * (gv * gv)
        m_hat = mn / (1.0 - ADAM_B1 ** ADAM_STEP)
        v_hat = vn / (1.0 - ADAM_B2 ** ADAM_STEP)
        d_ref[...] = -ADAM_LR * (m_hat / (jnp.sqrt(v_hat) + ADAM_EPS) + ADAM_WD * w_ref[...])
        nm_ref[...] = mn
        nv_ref[...] = vn

    spec = pl.BlockSpec((tm, cols), lambda i: (i, 0))
    return pl.pallas_call(
        body, name=name, grid=(rows // tm,), in_specs=[spec] * 4, out_specs=[spec] * 3,
        out_shape=[jax.ShapeDtypeStruct(w.shape, F32)] * 3,
        compiler_params=_params(("parallel",), tm * cols * 28),
    )(w, g, m, v)


def _mesh_position():
    return lax.axis_index("x"), lax.axis_index("y"), lax.axis_index("c")


def _other_chips(x, y):
    return [(1 - x, y), (x, 1 - y), (1 - x, 1 - y)]


def _gather_weights(shards):
    n_w = len(shards)
    halves = [s.shape[0] // 2 for s in shards]

    def body(*refs):
        ins, outs = refs[:n_w], refs[n_w:2 * n_w]
        local_sems, send_sems, recv_sems, fsend_sems, frecv_sems = refs[2 * n_w:]
        x, y, c = _mesh_position()
        chip = 2 * x + y
        sibling = (x, y, 1 - c)
        others = _other_chips(x, y)

        def region(w, owner_chip, half):
            start = owner_chip * (2 * halves[w]) + half * halves[w]
            return outs[w].at[pl.ds(pl.multiple_of(start, 16), halves[w])]

        def remote(src, dst, ssem, rsem, dev):
            return pltpu.make_async_remote_copy(src_ref=src, dst_ref=dst, send_sem=ssem, recv_sem=rsem,
                                                device_id=dev, device_id_type=MESH)

        local = []
        for w in range(n_w):
            cp = pltpu.make_async_copy(ins[w], outs[w].at[pl.ds(pl.multiple_of(chip * 2 * halves[w], 16),
                                                                2 * halves[w])], local_sems.at[w])
            cp.start()
            local.append(cp)
        sends = []
        for w in range(n_w):
            mine = ins[w].at[pl.ds(pl.multiple_of(c * halves[w], 16), halves[w])]
            for k, (px, py) in enumerate(others):
                cp = remote(mine, region(w, chip, c), send_sems.at[w, k], recv_sems.at[w, k], (px, py, c))
                cp.start()
                sends.append(cp)
        for k, (px, py) in enumerate(others):
            for w in range(n_w):
                landed = region(w, 2 * px + py, c)
                remote(landed, landed, send_sems.at[w, k], recv_sems.at[w, k], (px, py, c)).wait_recv()
                cp = remote(landed, landed, fsend_sems.at[w, k], frecv_sems.at[w, k], sibling)
                cp.start()
                sends.append(cp)
        for k, (px, py) in enumerate(others):
            for w in range(n_w):
                passed = region(w, 2 * px + py, 1 - c)
                remote(passed, passed, fsend_sems.at[w, k], frecv_sems.at[w, k], sibling).wait_recv()
        for cp in sends:
            cp.wait_send()
        for cp in local:
            cp.wait()

    return pl.pallas_call(
        body, name="gather_weights",
        in_specs=[HBM_SPEC] * n_w, out_specs=[HBM_SPEC] * n_w,
        out_shape=[jax.ShapeDtypeStruct((N_CHIPS * s.shape[0], s.shape[1]), s.dtype) for s in shards],
        scratch_shapes=[pltpu.SemaphoreType.DMA((n_w,))] + [pltpu.SemaphoreType.DMA((n_w, 3))] * 4,
    )(*shards)


REDUCE_CHUNK_ROWS = 256


def _reduce_gradients(grads, small):
    n_w = len(grads)
    rows = [g.shape[0] // N_CHIPS for g in grads]
    halves = [r // 2 for r in rows]
    cols = grads[0].shape[1]
    wire = grads[0].dtype
    ch = REDUCE_CHUNK_ROWS
    for h in halves:
        assert h % 16 == 0
    rows_s = small.shape[0]

    def body(*refs):
        g_in = refs[:n_w]
        small_ref = refs[n_w]
        outs = refs[n_w + 1:2 * n_w + 1]
        small_out = refs[2 * n_w + 1]
        from_sib = refs[2 * n_w + 2:3 * n_w + 2]
        chip_sum = refs[3 * n_w + 2:4 * n_w + 2]
        from_chips = refs[4 * n_w + 2:5 * n_w + 2]
        (small_all, buf_a, buf_b, buf_o, sib_send, sib_recv, chip_send, chip_recv,
         fin_send, fin_recv, small_send, small_recv, io_sem) = refs[5 * n_w + 2:]
        x, y, c = _mesh_position()
        chip = 2 * x + y
        me = 4 * x + 2 * y + c
        sibling = (x, y, 1 - c)
        others = _other_chips(x, y)

        def remote(src, dst, ssem, rsem, dev):
            return pltpu.make_async_remote_copy(src_ref=src, dst_ref=dst, send_sem=ssem, recv_sem=rsem,
                                                device_id=dev, device_id_type=MESH)

        def part(w, owner_chip, half):
            start = owner_chip * rows[w] + half * halves[w]
            return g_in[w].at[pl.ds(pl.multiple_of(start, 16), halves[w])]

        pending = []
        small_all[me] = small_ref[...]
        for j in range(N_DEV - 1):
            peer = (me + 1 + j) % N_DEV
            cp = remote(small_all.at[me], small_all.at[me], small_send.at[j], small_recv.at[j],
                        (peer // 4, (peer // 2) % 2, peer % 2))
            cp.start()
            pending.append(cp)

        for w in range(n_w):
            for k in range(N_CHIPS):
                cp = remote(part(w, k, 1 - c), from_sib[w].at[k], sib_send.at[w, k], sib_recv.at[w, k], sibling)
                cp.start()
                pending.append(cp)

        def add_stream(w, srcs, dst, n_rows):
            def chunk(start, size):
                total = None
                for i, src in enumerate(srcs):
                    buf = buf_a if i % 2 == 0 else buf_b
                    cp = pltpu.make_async_copy(src.at[pl.ds(start, size)], buf.at[pl.ds(0, size)], io_sem)
                    cp.start()
                    cp.wait()
                    val = buf[pl.ds(0, size), :].astype(F32)
                    total = val if total is None else total + val
                return total

            n_full = n_rows // ch
            rem = n_rows - n_full * ch

            def store(total, start, size):
                if dst.dtype == F32:
                    buf_o[pl.ds(0, size), :] = total
                    cp = pltpu.make_async_copy(buf_o.at[pl.ds(0, size)], dst.at[pl.ds(start, size)], io_sem)
                else:
                    buf_a[pl.ds(0, size), :] = total.astype(buf_a.dtype)
                    cp = pltpu.make_async_copy(buf_a.at[pl.ds(0, size)], dst.at[pl.ds(start, size)], io_sem)
                cp.start()
                cp.wait()

            def loop_body(i, carry):
                start = pl.multiple_of(i * ch, ch)
                store(chunk(start, ch), start, ch)
                return carry

            lax.fori_loop(0, n_full, loop_body, 0)
            if rem:
                store(chunk(n_full * ch, rem), n_full * ch, rem)

        order = [2, 0, 1]
        for w in range(n_w):
            for k in range(N_CHIPS):
                remote(part(w, k, 1 - c), from_sib[w].at[k], sib_send.at[w, k], sib_recv.at[w, k],
                       sibling).wait_recv()
        for k in order:
            px, py = others[k]
            owner = 2 * px + py
            for w in range(n_w):
                add_stream(w, [part(w, owner, c), from_sib[w].at[owner]], chip_sum[w].at[owner], halves[w])
                cp = remote(chip_sum[w].at[owner], from_chips[w].at[k], chip_send.at[w, k],
                            chip_recv.at[w, k], (px, py, c))
                cp.start()
                pending.append(cp)
        for w in range(n_w):
            add_stream(w, [part(w, chip, c), from_sib[w].at[chip]], chip_sum[w].at[chip], halves[w])

        for w in range(n_w):
            for k in range(3):
                px, py = others[k]
                remote(chip_sum[w].at[chip], from_chips[w].at[k], chip_send.at[w, k], chip_recv.at[w, k],
                       (px, py, c)).wait_recv()
            mine = outs[w].at[pl.ds(pl.multiple_of(c * halves[w], 16), halves[w])]
            add_stream(w, [chip_sum[w].at[chip], from_chips[w].at[0], from_chips[w].at[1],
                           from_chips[w].at[2]], mine, halves[w])
            cp = remote(mine, mine, fin_send.at[w], fin_recv.at[w], sibling)
            cp.start()
            pending.append(cp)
        for w in range(n_w):
            theirs = outs[w].at[pl.ds(pl.multiple_of((1 - c) * halves[w], 16), halves[w])]
            remote(theirs, theirs, fin_send.at[w], fin_recv.at[w], sibling).wait_recv()

        for j in range(N_DEV - 1):
            peer = (me + N_DEV - 1 - j) % N_DEV
            remote(small_all.at[peer], small_all.at[peer], small_send.at[j], small_recv.at[j],
                   sibling).wait_recv()
        total = small_all[0]
        for d in range(1, N_DEV):
            total = total + small_all[d]
        small_out[...] = total
        for cp in pending:
            cp.wait_send()

    hbm_scratch = ([jax.ShapeDtypeStruct((N_CHIPS, h, cols), wire) for h in halves] * 2
                   + [jax.ShapeDtypeStruct((3, h, cols), wire) for h in halves])
    out_shape = ([jax.ShapeDtypeStruct((r, cols), F32) for r in rows]
                 + [jax.ShapeDtypeStruct((rows_s, 128), F32)] + hbm_scratch)
    res = pl.pallas_call(
        body, name="reduce_gradients",
        in_specs=[HBM_SPEC] * n_w + [VMEM_SPEC],
        out_specs=[HBM_SPEC] * n_w + [VMEM_SPEC] + [HBM_SPEC] * (3 * n_w),
        out_shape=out_shape,
        scratch_shapes=[
            pltpu.VMEM((N_DEV, rows_s, 128), F32),
            pltpu.VMEM((ch, cols), wire), pltpu.VMEM((ch, cols), wire), pltpu.VMEM((ch, cols), F32),
            pltpu.SemaphoreType.DMA((n_w, N_CHIPS)), pltpu.SemaphoreType.DMA((n_w, N_CHIPS)),
            pltpu.SemaphoreType.DMA((n_w, 3)), pltpu.SemaphoreType.DMA((n_w, 3)),
            pltpu.SemaphoreType.DMA((n_w,)), pltpu.SemaphoreType.DMA((n_w,)),
            pltpu.SemaphoreType.DMA((N_DEV - 1,)), pltpu.SemaphoreType.DMA((N_DEV - 1,)),
            pltpu.SemaphoreType.DMA,
        ],
        compiler_params=pltpu.CompilerParams(vmem_limit_bytes=VMEM_LIMIT_MIN),
    )(*grads, small)
    return res[:n_w], res[n_w]


def _pack_small(parts, rows):
    flat = jnp.concatenate([p.reshape(-1) for p in parts])
    flat = jnp.pad(flat, (0, rows * 128 - flat.shape[0]))
    return flat.reshape(rows, 128)


def _unpack_small(packed, shapes):
    flat = packed.reshape(-1)
    out, off = [], 0
    for shp in shapes:
        n = int(np.prod(shp))
        out.append(flat[off:off + n].reshape(shp))
        off += n
    return out


def kernel(x, g_attn, w_in, b_in, sinks_a, g_out_a, g_out_b, w_out, g_mlp, w_1, w_2, g_final, loss_target, m_g_attn, m_w_in, m_b_in, m_sinks_a, m_g_out_a, m_g_out_b, m_w_out, m_g_mlp, m_w_1, m_w_2, m_g_final, v_g_attn, v_w_in, v_b_in, v_sinks_a, v_g_out_a, v_g_out_b, v_w_out, v_g_mlp, v_w_1, v_w_2, v_g_final):
    s, d = x.shape[1], x.shape[2]
    d_in = b_in.shape[1]
    qa = g_out_a.shape[1]
    qb = g_out_b.shape[1]
    kva = 2 * N_KV_GROUPS * HEAD_DIM
    assert d_in == qa + kva + 3 * qb and qa + qb == w_out.shape[1] * N_CHIPS
    d_ff = w_1.shape[2] * N_CHIPS
    ff_shard = w_1.shape[2]
    in_shard = w_in.shape[2]
    n_heads_a, n_heads_b = qa // HEAD_DIM, qb // HEAD_DIM
    slopes_a, slopes_b = alibi_slopes(n_heads_a), alibi_slopes(n_heads_b)

    x2d = x[0]
    target = loss_target[0]

    shards = [w_in[0].T.astype(BF16), w_out[0].astype(BF16), w_1[0].astype(BF16), w_2[0].astype(BF16)]
    w_in_t, w_out_g, w_1_g, w_2_g = _gather_weights(shards)
    w_1_g = w_1_g.reshape(N_CHIPS, d, ff_shard)

    tm = _tile(s, 1024)

    h1, r1 = _norm_fwd("norm_attn", x2d, g_attn)

    q_a, = _project_by_class("proj_qa", h1, w_in_t, b_in, 0, qa, (1,))
    kv_a, = _project_by_class("proj_kva", h1, w_in_t, b_in, qa, kva, (1,))
    q_bs = _project_by_class("proj_qb", h1, w_in_t, b_in, qa + kva, qb, DILATIONS)
    k_bs = _project_by_class("proj_kb", h1, w_in_t, b_in, qa + kva + qb, qb, DILATIONS)
    v_bs = _project_by_class("proj_vb", h1, w_in_t, b_in, qa + kva + 2 * qb, qb, DILATIONS)

    sinks = sinks_a.reshape(-1)
    o_a, lse_a = _attn_fwd("attn_a_fwd", q_a, kv_a, dil=1, max_steps=WINDOW_A - 1, slopes=slopes_a, sinks=sinks)
    o_a = o_a[0]
    o_bs, lse_bs = [], []
    for n, (window, dil) in enumerate(DILATED_BRANCHES):
        o, l = _attn_fwd(f"attn_b{dil}_fwd", q_bs[n], (k_bs[n], v_bs[n]), dil=dil, max_steps=window // dil,
                         slopes=slopes_b)
        o_bs.append(o)
        lse_bs.append(l)
    mix, o_b, *lse_tot, r_a, r_b = _mix_fwd(o_a, o_bs, lse_bs, g_out_a, g_out_b)

    tn = _tile(d, 512)
    a_spec, b_spec = _mm_specs("nn", tm, tn, d)
    tile_mn = pl.BlockSpec((tm, tn), lambda i, j, k: (i, j))
    x2 = _matmul("out_proj", mix, w_out_g, [x2d], mode="nn", grid=(s // tm, d // tn, 1),
                 a_spec=a_spec, b_spec=b_spec, extra_specs=[tile_mn],
                 out_shapes=[jax.ShapeDtypeStruct((s, d), F32)], out_specs=[tile_mn],
                 epilogue=lambda acc, res: (acc + res,))[0]

    h2, r2 = _norm_fwd("norm_mlp", x2, g_mlp)

    tn = _tile(ff_shard, 512)
    per = ff_shard // tn
    a_spec, _ = _mm_specs("nn", tm, tn, d)
    u = _matmul("mlp_up", h2, w_1_g, [], mode="nn", grid=(s // tm, d_ff // tn, 1),
                a_spec=a_spec, b_spec=pl.BlockSpec((None, d, tn), lambda i, j, k: (j // per, 0, j % per)),
                extra_specs=[], out_shapes=[jax.ShapeDtypeStruct((s, d_ff), BF16)], out_specs=[tile_mn],
                epilogue=lambda acc: (jnp.maximum(acc, 0.0),))[0]

    tn = _tile(d, 1024)
    tk = _tile(d_ff, 2048)
    a_spec, b_spec = _mm_specs("nn", tm, tn, tk)
    tile_mn = pl.BlockSpec((tm, tn), lambda i, j, k: (i, j))
    x3 = _matmul("mlp_down", u, w_2_g, [x2], mode="nn", grid=(s // tm, d // tn, d_ff // tk),
                 a_spec=a_spec, b_spec=b_spec, extra_specs=[tile_mn],
                 out_shapes=[jax.ShapeDtypeStruct((s, d), F32)], out_specs=[tile_mn],
                 prologue=lambda a: a * a, epilogue=lambda acc, res: (acc + res,), acc_shape=(tm, tn))[0]

    dx3, dx3b, loss_part, dg_final = _loss_head(x3, target, g_final.reshape(1, d))

    tn = _tile(d_ff, 512)
    a_spec, b_spec = _mm_specs("nt", tm, tn, d)
    tile_mn = pl.BlockSpec((tm, tn), lambda i, j, k: (i, j))
    dpre = _matmul("mlp_down_dx", dx3b, w_2_g, [u], mode="nt", grid=(s // tm, d_ff // tn, 1),
                   a_spec=a_spec, b_spec=b_spec, extra_specs=[tile_mn],
                   out_shapes=[jax.ShapeDtypeStruct((s, d_ff), BF16)], out_specs=[tile_mn],
                   epilogue=lambda acc, uu: (acc * (2.0 * uu.astype(F32)),))[0]

    wire = GRAD_WIRE_DTYPE
    tk_s = _tile(s, 2048)
    tmw = _tile(d_ff, 1024)
    a_spec, b_spec = _mm_specs("tn", tmw, d, tk_s)
    dw_2 = _matmul("mlp_down_dw", u, dx3b, [], mode="tn", grid=(d_ff // tmw, 1, s // tk_s),
                   a_spec=a_spec, b_spec=b_spec, extra_specs=[],
                   out_shapes=[jax.ShapeDtypeStruct((d_ff, d), wire)],
                   out_specs=[pl.BlockSpec((tmw, d), lambda i, j, k: (i, j))],
                   prologue=lambda a: a * a, epilogue=lambda acc: (acc,), acc_shape=(tmw, d))[0]

    tn = _tile(d, 1024)
    tk = _tile(ff_shard, 2048)
    per = ff_shard // tk
    a_spec, _ = _mm_specs("nt", tm, tn, tk)
    tile_mn = pl.BlockSpec((tm, tn), lambda i, j, k: (i, j))
    dh2 = _matmul("mlp_up_dx", dpre, w_1_g, [], mode="nt", grid=(s // tm, d // tn, d_ff // tk),
                  a_spec=a_spec, b_spec=pl.BlockSpec((None, tn, tk), lambda i, j, k: (k // per, j, k % per)),
                  extra_specs=[], out_shapes=[jax.ShapeDtypeStruct((s, d), F32)], out_specs=[tile_mn],
                  epilogue=lambda acc: (acc,), acc_shape=(tm, tn))[0]

    tmw = _tile(d, 1024)
    tnw = _tile(ff_shard, 2048)
    per = ff_shard // tnw
    a_spec, b_spec = _mm_specs("tn", tmw, tnw, tk_s)
    dw_1 = _matmul("mlp_up_dw", h2, dpre, [], mode="tn", grid=(d // tmw, d_ff // tnw, s // tk_s),
                   a_spec=a_spec, b_spec=b_spec, extra_specs=[],
                   out_shapes=[jax.ShapeDtypeStruct((N_CHIPS, d, ff_shard), wire)],
                   out_specs=[pl.BlockSpec((None, tmw, tnw), lambda i, j, k: (j // per, i, j % per))],
                   epilogue=lambda acc: (acc,), acc_shape=(tmw, tnw))[0]

    dx2, dx2b, dg_mlp = _norm_bwd("norm_mlp_bwd", dh2, x2, r2, g_mlp, dx3)

    tn = _tile(d, 512)
    a_spec, b_spec = _mm_specs("nt", tm, tn, d)
    tile_mn = pl.BlockSpec((tm, tn), lambda i, j, k: (i, j))
    dmix = _matmul("out_proj_dx", dx2b, w_out_g, [], mode="nt", grid=(s // tm, d // tn, 1),
                   a_spec=a_spec, b_spec=b_spec, extra_specs=[],
                   out_shapes=[jax.ShapeDtypeStruct((s, d), F32)], out_specs=[tile_mn],
                   epilogue=lambda acc: (acc,))[0]

    tmw = _tile(d, 1024)
    a_spec, b_spec = _mm_specs("tn", tmw, d, tk_s)
    dw_out = _matmul("out_proj_dw", mix, dx2b, [], mode="tn", grid=(d // tmw, 1, s // tk_s),
                     a_spec=a_spec, b_spec=b_spec, extra_specs=[],
                     out_shapes=[jax.ShapeDtypeStruct((d, d), wire)],
                     out_specs=[pl.BlockSpec((tmw, d), lambda i, j, k: (i, j))],
                     epilogue=lambda acc: (acc,), acc_shape=(tmw, d))[0]

    mix_grads = _mix_bwd(dmix, o_a, o_b, r_a, r_b, g_out_a, g_out_b)
    do_a, do_bs, delta_a, delta_bs = mix_grads[0], mix_grads[1:4], mix_grads[4], mix_grads[5:8]
    dg_out_a, dg_out_b = mix_grads[8:]

    dq_a, dkv_a, dsinks = _attn_bwd("attn_a_bwd", q_a, kv_a, do_a[None], lse_a, delta_a[None], dil=1,
                                    max_steps=WINDOW_A - 1, slopes=slopes_a, sinks=sinks)
    dqs, dks, dvs = [], [], []
    for n, (window, dil) in enumerate(DILATED_BRANCHES):
        dq, dk, dv = _attn_bwd(f"attn_b{dil}_bwd", q_bs[n], (k_bs[n], v_bs[n]), do_bs[n], lse_tot[n],
                               delta_bs[n], dil=dil, max_steps=window // dil, slopes=slopes_b)
        dqs.append(dq)
        dks.append(dk)
        dvs.append(dv)
    dproj, db_in = _assemble_dproj(dq_a[0], dkv_a[0], dqs, dks, dvs)

    tn = _tile(d, 512)
    a_spec, b_spec = _mm_specs("nn", tm, tn, d_in)
    tile_mn = pl.BlockSpec((tm, tn), lambda i, j, k: (i, j))
    dh1 = _matmul("in_proj_dx", dproj, w_in_t, [], mode="nn", grid=(s // tm, d // tn, 1),
                  a_spec=a_spec, b_spec=b_spec, extra_specs=[],
                  out_shapes=[jax.ShapeDtypeStruct((s, d), F32)], out_specs=[tile_mn],
                  epilogue=lambda acc: (acc,))[0]

    tmw = d_in // 2 if (d_in // 2) % 128 == 0 else d_in
    tnw = _tile(d, 1024)
    tk_s = _tile(s, 1024)
    a_spec, b_spec = _mm_specs("tn", tmw, tnw, tk_s)
    dw_in_t = _matmul("in_proj_dw", dproj, h1, [], mode="tn", grid=(d_in // tmw, d // tnw, s // tk_s),
                      a_spec=a_spec, b_spec=b_spec, extra_specs=[],
                      out_shapes=[jax.ShapeDtypeStruct((d_in, d), wire)],
                      out_specs=[pl.BlockSpec((tmw, tnw), lambda i, j, k: (i, j))],
                      epilogue=lambda acc: (acc,), acc_shape=(tmw, tnw))[0]

    grad_x, _, dg_attn = _norm_bwd("norm_attn_bwd", dh1, x2d, r1, g_attn, dx2)

    small_parts = [dg_attn, db_in, dsinks[:, :n_heads_a], dg_out_a, dg_out_b, dg_mlp, dg_final]
    small_shapes = [g_attn.shape, b_in.shape, sinks_a.shape, g_out_a.shape, g_out_b.shape, g_mlp.shape,
                    g_final.shape]
    n_small = sum(int(np.prod(shp)) for shp in small_shapes)
    rows_s = -(-n_small // (8 * 128)) * 8
    (gw_in_t, gw_out, gw_1, gw_2), small_sum = _reduce_gradients(
        [dw_in_t, dw_out, dw_1.reshape(N_CHIPS * d, ff_shard), dw_2], _pack_small(small_parts, rows_s))
    gw_in = gw_in_t.T
    g_small = _unpack_small(small_sum, small_shapes)

    upd_in = _adamw("adamw_w_in", w_in[0], gw_in, m_w_in[0], v_w_in[0])
    upd_out = _adamw("adamw_w_out", w_out[0], gw_out, m_w_out[0], v_w_out[0])
    upd_1 = _adamw("adamw_w_1", w_1[0], gw_1, m_w_1[0], v_w_1[0])
    upd_2 = _adamw("adamw_w_2", w_2[0], gw_2, m_w_2[0], v_w_2[0])
    small_w = [g_attn, b_in, sinks_a, g_out_a, g_out_b, g_mlp, g_final]
    small_m = [m_g_attn, m_b_in, m_sinks_a, m_g_out_a, m_g_out_b, m_g_mlp, m_g_final]
    small_v = [v_g_attn, v_b_in, v_sinks_a, v_g_out_a, v_g_out_b, v_g_mlp, v_g_final]
    upd_small = _adamw("adamw_small", _pack_small(small_w, rows_s), small_sum,
                       _pack_small(small_m, rows_s), _pack_small(small_v, rows_s))
    d_small, m_small, v_small = [_unpack_small(t, small_shapes) for t in upd_small]

    loss = lax.psum(loss_part[0, 0], ("x", "y", "c"))

    def ordered(small, big):
        w_in_v, w_out_v, w_1_v, w_2_v = big
        return [small[0], w_in_v[None], small[1], small[2], small[3], small[4], w_out_v[None], small[5],
                w_1_v[None], w_2_v[None], small[6]]

    grads = ordered(g_small, (gw_in, gw_out, gw_1, gw_2))
    deltas = ordered(d_small, (upd_in[0], upd_out[0], upd_1[0], upd_2[0]))
    new_m = ordered(m_small, (upd_in[1], upd_out[1], upd_1[1], upd_2[1]))
    new_v = ordered(v_small, (upd_in[2], upd_out[2], upd_1[2], upd_2[2]))
    return (loss, grad_x[None], *grads, *deltas, *new_m, *new_v)
```

```python
import functools

import jax
import jax.numpy as jnp
import numpy as np
from jax import lax
from jax.experimental import pallas as pl
from jax.experimental.pallas import tpu as pltpu

F32 = jnp.float32
BF16 = jnp.bfloat16

HEAD_DIM = 64
BLOCK = 128
PAIR = 2 * HEAD_DIM
N_KV_GROUPS = 2
WINDOW_A = 128
DILATED_BRANCHES = ((128, 1), (512, 4), (2048, 16))
EPS = 1e-5
NEG_INF = -1e30
ATT_SCALE = HEAD_DIM ** -0.5

ADAM_LR = 0.001
ADAM_B1 = 0.9
ADAM_B2 = 0.999
ADAM_EPS = 1e-08
ADAM_WD = 0.01
ADAM_STEP = 10

N_CHIPS = 4
N_DEV = 8
MESH = pl.DeviceIdType.MESH
GRAD_WIRE_DTYPE = jnp.bfloat16

VMEM_CAPACITY_V7X = 64 * 1024 * 1024
VMEM_LIMIT_MAX = 56 * 1024 * 1024
VMEM_LIMIT_MIN = 32 * 1024 * 1024

HBM_SPEC = pl.BlockSpec(memory_space=pltpu.HBM)
VMEM_SPEC = pl.BlockSpec(memory_space=pltpu.VMEM)
SMEM_SPEC = pl.BlockSpec(memory_space=pltpu.SMEM)


def _nbytes(shape, dtype):
    return int(np.prod([s for s in shape if s is not None])) * jnp.dtype(dtype).itemsize


def _params(semantics, block_bytes):
    limit = min(max(2 * block_bytes + (4 << 20), VMEM_LIMIT_MIN), VMEM_LIMIT_MAX)
    return pltpu.CompilerParams(dimension_semantics=semantics, vmem_limit_bytes=limit)


class _Hook:
    def __init__(self, operands, out_shape, sems, start, finish, mid=None, aliases=None):
        self.operands, self.out_shape, self.sems = list(operands), list(out_shape), list(sems)
        self.start, self.mid, self.finish = start, mid, finish
        self.aliases = dict(aliases or {})


def _merge_hooks(hooks):
    hooks = [h for h in hooks if h is not None]
    if len(hooks) <= 1:
        return hooks[0] if hooks else None
    n_op = np.cumsum([0] + [len(h.operands) for h in hooks])
    n_out = np.cumsum([0] + [len(h.out_shape) for h in hooks])
    n_sem = np.cumsum([0] + [len(h.sems) for h in hooks])

    def run(which):
        def fn(ops, outs, sems):
            for i, h in enumerate(hooks):
                f = getattr(h, which)
                if f is not None:
                    f(ops[n_op[i]:n_op[i + 1]], outs[n_out[i]:n_out[i + 1]], sems[n_sem[i]:n_sem[i + 1]])
        return fn

    aliases = {}
    for i, h in enumerate(hooks):
        aliases.update({int(n_op[i]) + a: int(n_out[i]) + b for a, b in h.aliases.items()})
    return _Hook(sum([h.operands for h in hooks], []), sum([h.out_shape for h in hooks], []),
                 sum([h.sems for h in hooks], []), run("start"), run("finish"),
                 run("mid") if any(h.mid for h in hooks) else None, aliases)


HOOK_MID_FRACTION = 0.6


def _call(body, hook, *, name, grid, in_specs, out_specs, out_shape, scratch_shapes=(), compiler_params):
    in_specs, out_specs, out_shape = list(in_specs), list(out_specs), list(out_shape)
    scratch_shapes = list(scratch_shapes)
    if hook is None:
        call = pl.pallas_call(body, name=name, grid=grid, in_specs=in_specs, out_specs=out_specs,
                              out_shape=out_shape, scratch_shapes=scratch_shapes,
                              compiler_params=compiler_params)
        return lambda *operands: (call(*operands), [])
    n_in, n_hin, n_out, n_hout, n_scr = (len(in_specs), len(hook.operands), len(out_specs),
                                         len(hook.out_shape), len(scratch_shapes))
    total = int(np.prod(grid))
    t_mid = min(int(total * HOOK_MID_FRACTION), total - 1)

    def wrapped(*refs):
        ins, h_in = refs[:n_in], refs[n_in:n_in + n_hin]
        o0 = n_in + n_hin
        outs, h_out = refs[o0:o0 + n_out], refs[o0 + n_out:o0 + n_out + n_hout]
        s0 = o0 + n_out + n_hout
        scr, h_sems = refs[s0:s0 + n_scr], refs[s0 + n_scr:]
        t = pl.program_id(0)
        for axis in range(1, len(grid)):
            t = t * grid[axis] + pl.program_id(axis)

        @pl.when(t == 0)
        def _():
            hook.start(h_in, h_out, h_sems)

        body(*ins, *outs, *scr)
        if hook.mid is not None:
            @pl.when(t == t_mid)
            def _():
                hook.mid(h_in, h_out, h_sems)

        @pl.when(t == total - 1)
        def _():
            hook.finish(h_in, h_out, h_sems)

    params = pltpu.CompilerParams(dimension_semantics=("arbitrary",) * len(grid),
                                  vmem_limit_bytes=compiler_params.vmem_limit_bytes)
    call = pl.pallas_call(
        wrapped, name=name, grid=grid,
        in_specs=in_specs + [HBM_SPEC] * n_hin, out_specs=out_specs + [HBM_SPEC] * n_hout,
        out_shape=out_shape + hook.out_shape, scratch_shapes=scratch_shapes + hook.sems,
        input_output_aliases={n_in + a: n_out + b for a, b in hook.aliases.items()},
        compiler_params=params)

    def run(*operands):
        res = call(*operands, *hook.operands)
        return res[:n_out], res[n_out:]

    return run


def _remote(src, dst, send_sem, recv_sem, device):
    return pltpu.make_async_remote_copy(src_ref=src, dst_ref=dst, send_sem=send_sem, recv_sem=recv_sem,
                                        device_id=device, device_id_type=MESH)


def alibi_slopes(n):
    return [float(v) for v in np.asarray(2.0 ** (-8.0 * (np.arange(n) + 1) / n), dtype=np.float32)]


def _matmul(name, a, b, extras, *, mode, grid, a_spec, b_spec, extra_specs, out_shapes, out_specs,
            epilogue, prologue=None, acc_shape=None, hook=None):
    dims = {"nn": ((1,), (0,)), "nt": ((1,), (1,)), "tn": ((0,), (0,))}[mode]
    nk = grid[2]
    n_ex, n_out = len(extras), len(out_shapes)

    def body(a_ref, b_ref, *rest):
        ex, outs = rest[:n_ex], rest[n_ex:n_ex + n_out]
        av = a_ref[...]
        if prologue is not None:
            av = prologue(av)
        part = lax.dot_general(av, b_ref[...], (dims, ((), ())), preferred_element_type=F32)

        def finish(acc):
            res = epilogue(acc, *[e[...] for e in ex])
            for o, r in zip(outs, res):
                o[...] = r.astype(o.dtype)

        if nk == 1:
            finish(part)
        else:
            acc_ref = rest[-1]
            k = pl.program_id(2)

            @pl.when(k == 0)
            def _():
                acc_ref[...] = part

            @pl.when(k > 0)
            def _():
                acc_ref[...] += part

            @pl.when(k == nk - 1)
            def _():
                finish(acc_ref[...])

    blocks = [(a_spec.block_shape, a.dtype), (b_spec.block_shape, b.dtype)]
    blocks += [(s.block_shape, e.dtype) for s, e in zip(extra_specs, extras)]
    blocks += [(s.block_shape, o.dtype) for s, o in zip(out_specs, out_shapes)]
    nbytes = sum(_nbytes(s, d) for s, d in blocks)
    scratch = []
    if nk > 1:
        scratch.append(pltpu.VMEM(acc_shape, F32))
        nbytes += _nbytes(acc_shape, F32)
    res, hook_res = _call(
        body, hook, name=name, grid=grid,
        in_specs=[a_spec, b_spec, *extra_specs], out_specs=list(out_specs), out_shape=list(out_shapes),
        scratch_shapes=scratch,
        compiler_params=_params(("parallel", "parallel", "arbitrary"), nbytes),
    )(a, b, *extras)
    return res if hook is None else (res, hook_res)


def _mm_specs(mode, tm, tn, tk, b_block=None, b_map=None):
    if mode == "tn":
        a_spec = pl.BlockSpec((tk, tm), lambda i, j, k: (k, i))
    else:
        a_spec = pl.BlockSpec((tm, tk), lambda i, j, k: (i, k))
    if b_block is not None:
        b_spec = pl.BlockSpec(b_block, b_map)
    elif mode == "nt":
        b_spec = pl.BlockSpec((tn, tk), lambda i, j, k: (j, k))
    else:
        b_spec = pl.BlockSpec((tk, tn), lambda i, j, k: (k, j))
    return a_spec, b_spec


def _project_by_class(name, h, w_t, bias, row_off, width, dilations, hook=None):
    s, d = h.shape
    tm = _tile(s, 1024)
    tn = 512 if width % 512 == 0 and row_off % 512 == 0 else _tile(width, 256)
    off = row_off // tn
    assert row_off % tn == 0 and tn % 128 == 0
    n_out = len(dilations)

    def body(h_ref, w_ref, b_ref, *rest):
        outs, perm_ref = rest[:n_out], rest[n_out]
        acc = lax.dot_general(h_ref[...], w_ref[...], (((1,), (1,)), ((), ())), preferred_element_type=F32)
        acc = acc + b_ref[...]
        for j in range(tn // 128):
            cols = slice(j * 128, (j + 1) * 128)
            for o_ref, dil in zip(outs, dilations):
                _to_classes(o_ref, cols, acc[:, cols], perm_ref, dil)

    blocks = tm * d * 2 + tn * d * 2 + 3 * tm * tn * 2 + tm * 128 * 4
    res, hook_res = _call(
        body, hook, name=name, grid=(s // tm, width // tn),
        in_specs=[pl.BlockSpec((tm, d), lambda i, j: (i, 0)), pl.BlockSpec((tn, d), lambda i, j: (j + off, 0)),
                  pl.BlockSpec((1, tn), lambda i, j: (0, j + off))],
        out_specs=[pl.BlockSpec((dil, tm // dil, tn), lambda i, j: (0, i, j)) for dil in dilations],
        out_shape=[_class_shape(dil, s, width, BF16) for dil in dilations],
        scratch_shapes=[pltpu.VMEM((tm, 128), F32)],
        compiler_params=_params(("parallel", "parallel"), blocks),
    )(h, w_t, bias)
    return res if hook is None else (res, hook_res)


def _tile(n, want):
    if n <= want:
        return n
    t = (want // 128) * 128
    while t > 128 and n % t:
        t -= 128
    assert n % t == 0, (n, want)
    return t


def _row_tile(s):
    return 256 if s % 256 == 0 else s


def _norm_fwd(name, x, g):
    s, d = x.shape
    tm = _row_tile(s)

    def body(x_ref, g_ref, h_ref, r_ref):
        xv = x_ref[...]
        r = lax.rsqrt(jnp.mean(xv * xv, axis=-1, keepdims=True) + EPS)
        h_ref[...] = ((xv * r) * g_ref[...]).astype(BF16)
        r_ref[...] = r

    row = pl.BlockSpec((tm, d), lambda i: (i, 0))
    return pl.pallas_call(
        body, name=name, grid=(s // tm,),
        in_specs=[row, pl.BlockSpec((1, d), lambda i: (0, 0))],
        out_specs=[row, pl.BlockSpec((tm, 1), lambda i: (i, 0))],
        out_shape=[jax.ShapeDtypeStruct((s, d), BF16), jax.ShapeDtypeStruct((s, 1), F32)],
        compiler_params=_params(("parallel",), tm * d * 6),
    )(x, g)


def _norm_bwd(name, dh, x, r, g, dres, hook=None):
    s, d = x.shape
    tm = _row_tile(s)

    def body(dh_ref, x_ref, r_ref, g_ref, dres_ref, dx_ref, dxb_ref, dg_ref):
        rv = r_ref[...]
        xn = x_ref[...] * rv
        dhv = dh_ref[...]
        dxn = dhv * g_ref[...]
        dx = dres_ref[...] + rv * (dxn - xn * jnp.mean(dxn * xn, axis=-1, keepdims=True))
        dx_ref[...] = dx
        dxb_ref[...] = dx.astype(BF16)
        part = jnp.sum(dhv * xn, axis=0, keepdims=True)

        @pl.when(pl.program_id(0) == 0)
        def _():
            dg_ref[...] = part

        @pl.when(pl.program_id(0) > 0)
        def _():
            dg_ref[...] += part

    row = pl.BlockSpec((tm, d), lambda i: (i, 0))
    vec = pl.BlockSpec((1, d), lambda i: (0, 0))
    res, hook_res = _call(
        body, hook, name=name, grid=(s // tm,),
        in_specs=[row, row, pl.BlockSpec((tm, 1), lambda i: (i, 0)), vec, row],
        out_specs=[row, row, vec],
        out_shape=[jax.ShapeDtypeStruct((s, d), F32), jax.ShapeDtypeStruct((s, d), BF16),
                   jax.ShapeDtypeStruct((1, d), F32)],
        compiler_params=_params(("arbitrary",), tm * d * 18),
    )(dh, x, r, g, dres)
    return res if hook is None else (res, hook_res)


def _loss_head(x3, target, g):
    s, d = x3.shape
    tm = _row_tile(s)

    def body(x_ref, t_ref, g_ref, dx_ref, dxb_ref, loss_ref, dg_ref):
        xv = x_ref[...]
        gv = g_ref[...]
        r = lax.rsqrt(jnp.mean(xv * xv, axis=-1, keepdims=True) + EPS)
        xn = xv * r
        err = xn * gv - t_ref[...]
        loss = 0.5 * jnp.sum(jnp.mean(err * err, axis=-1, keepdims=True), axis=0, keepdims=True)
        dy = err / d
        dxn = dy * gv
        dx = r * (dxn - xn * jnp.mean(dxn * xn, axis=-1, keepdims=True))
        dx_ref[...] = dx
        dxb_ref[...] = dx.astype(BF16)
        dg = jnp.sum(dy * xn, axis=0, keepdims=True)
        loss_row = jnp.broadcast_to(loss, (1, 128))

        @pl.when(pl.program_id(0) == 0)
        def _():
            dg_ref[...] = dg
            loss_ref[...] = loss_row

        @pl.when(pl.program_id(0) > 0)
        def _():
            dg_ref[...] += dg
            loss_ref[...] += loss_row

    row = pl.BlockSpec((tm, d), lambda i: (i, 0))
    vec = pl.BlockSpec((1, d), lambda i: (0, 0))
    return pl.pallas_call(
        body, name="loss_head", grid=(s // tm,),
        in_specs=[row, row, vec],
        out_specs=[row, row, pl.BlockSpec((1, 128), lambda i: (0, 0)), vec],
        out_shape=[jax.ShapeDtypeStruct((s, d), F32), jax.ShapeDtypeStruct((s, d), BF16),
                   jax.ShapeDtypeStruct((1, 128), F32), jax.ShapeDtypeStruct((1, d), F32)],
        compiler_params=_params(("arbitrary",), tm * d * 14),
    )(x3, target, g)


def _low_lanes(rows):
    return lax.broadcasted_iota(jnp.int32, (rows, PAIR), 1) < HEAD_DIM


def _to_classes(dst_ref, cols, value, perm_ref, dil):
    rows = value.shape[0]
    if dil == 1:
        dst_ref[0, :, cols] = value.astype(dst_ref.dtype)
        return
    perm_ref[...] = value
    for r in range(dil):
        dst_ref[r, :, cols] = perm_ref[pl.ds(r, rows // dil, stride=dil), :].astype(dst_ref.dtype)


def _from_classes(src_ref, cols, perm_ref, dil):
    if dil == 1:
        return src_ref[0, :, cols]
    rows = perm_ref.shape[0]
    for r in range(dil):
        perm_ref[pl.ds(r, rows // dil, stride=dil), :] = src_ref[r, :, cols]
    return perm_ref[...]


def _class_spec(dil, tm, width):
    return pl.BlockSpec((dil, tm // dil, width), lambda i: (0, i, 0))


def _class_shape(dil, s, width, dtype):
    return jax.ShapeDtypeStruct((dil, s // dil, width), dtype)


DILATIONS = tuple(d for _, d in DILATED_BRANCHES)


def _mix_fwd(oa, obs, lses, ga, gb):
    s, qa = oa.shape
    qb = obs[0].shape[2]
    tm = _row_tile(s)
    all_lanes = slice(0, 128)

    def body(oa_ref, o1_ref, o2_ref, o3_ref, l1_ref, l2_ref, l3_ref, ga_ref, gb_ref,
             mix_ref, ob_ref, t1_ref, t2_ref, t3_ref, ra_ref, rb_ref, perm_ref):
        oav = oa_ref[...]
        ra = lax.rsqrt(jnp.mean(oav * oav, axis=-1, keepdims=True) + EPS)
        ra_ref[...] = ra
        mix_ref[:, 0:qa] = ((oav * ra) * ga_ref[...]).astype(BF16)
        l1, l2, l3 = [_from_classes(l_ref, all_lanes, perm_ref, dil)
                      for l_ref, dil in zip((l1_ref, l2_ref, l3_ref), DILATIONS)]
        mx = jnp.maximum(jnp.maximum(l1, l2), l3)
        e1, e2, e3 = jnp.exp(l1 - mx), jnp.exp(l2 - mx), jnp.exp(l3 - mx)
        tot = e1 + e2 + e3
        lse = mx + jnp.log(tot)
        for t_ref, dil in zip((t1_ref, t2_ref, t3_ref), DILATIONS):
            _to_classes(t_ref, all_lanes, lse, perm_ref, dil)
        ws = (e1 / tot, e2 / tot, e3 / tot)
        low = _low_lanes(tm)
        ssq = jnp.zeros((tm, 1), F32)
        for i in range(qb // PAIR):
            sl = slice(i * PAIR, (i + 1) * PAIR)
            acc = jnp.zeros((tm, PAIR), F32)
            for w, o_ref, dil in zip(ws, (o1_ref, o2_ref, o3_ref), DILATIONS):
                wexp = jnp.where(low, w[:, 2 * i:2 * i + 1], w[:, 2 * i + 1:2 * i + 2])
                acc = acc + wexp * _from_classes(o_ref, sl, perm_ref, dil)
            ob_ref[:, sl] = acc
            ssq = ssq + jnp.sum(acc * acc, axis=-1, keepdims=True)
        rb = lax.rsqrt(ssq / qb + EPS)
        rb_ref[...] = rb
        mix_ref[:, qa:qa + qb] = ((ob_ref[...] * rb) * gb_ref[...]).astype(BF16)

    def row(w):
        return pl.BlockSpec((tm, w), lambda i: (i, 0))

    def vec(w):
        return pl.BlockSpec((1, w), lambda i: (0, 0))

    return pl.pallas_call(
        body, name="mix_fwd", grid=(s // tm,),
        in_specs=([row(qa)] + [_class_spec(d, tm, qb) for d in DILATIONS]
                  + [_class_spec(d, tm, 128) for d in DILATIONS] + [vec(qa), vec(qb)]),
        out_specs=([row(qa + qb), row(qb)] + [_class_spec(d, tm, 128) for d in DILATIONS] + [row(1), row(1)]),
        out_shape=([jax.ShapeDtypeStruct((s, qa + qb), BF16), jax.ShapeDtypeStruct((s, qb), F32)]
                   + [_class_shape(d, s, 128, F32) for d in DILATIONS]
                   + [jax.ShapeDtypeStruct((s, 1), F32), jax.ShapeDtypeStruct((s, 1), F32)]),
        scratch_shapes=[pltpu.VMEM((tm, 128), F32)],
        compiler_params=_params(("parallel",), tm * (qa + 4 * qb) * 4 + tm * (qa + qb) * 2 + tm * 4096),
    )(oa, *obs, *lses, ga, gb)


def _head_rowsums(prod, rows):
    low = _low_lanes(rows)
    lane = lax.broadcasted_iota(jnp.int32, (rows, 128), 1)
    out = jnp.zeros((rows, 128), F32)
    for i in range(prod.shape[1] // PAIR):
        tile = prod[:, i * PAIR:(i + 1) * PAIR]
        lo = jnp.sum(jnp.where(low, tile, 0.0), axis=-1, keepdims=True)
        hi = jnp.sum(jnp.where(low, 0.0, tile), axis=-1, keepdims=True)
        out = jnp.where(lane == 2 * i, lo, out)
        out = jnp.where(lane == 2 * i + 1, hi, out)
    return out


def _mix_bwd(dmix, oa, ob, ra, rb, ga, gb, hook=None):
    s, qa = oa.shape
    qb = ob.shape[1]
    tm = _row_tile(s)

    def one(dy, o, r, g):
        xn = o * r
        dxn = dy * g
        do = r * (dxn - xn * jnp.mean(dxn * xn, axis=-1, keepdims=True))
        return do, jnp.sum(dy * xn, axis=0, keepdims=True), _head_rowsums(do * o, tm)

    def body(dmix_ref, oa_ref, ob_ref, ra_ref, rb_ref, ga_ref, gb_ref,
             doa_ref, dob1_ref, dob2_ref, dob3_ref, dla_ref, dlb1_ref, dlb2_ref, dlb3_ref,
             dga_ref, dgb_ref, perm_ref):
        doa, dga, dla = one(dmix_ref[:, 0:qa], oa_ref[...], ra_ref[...], ga_ref[...])
        dob, dgb, dlb = one(dmix_ref[:, qa:qa + qb], ob_ref[...], rb_ref[...], gb_ref[...])
        doa_ref[...] = doa.astype(BF16)
        dla_ref[...] = dla
        for dob_ref, dlb_ref, dil in zip((dob1_ref, dob2_ref, dob3_ref), (dlb1_ref, dlb2_ref, dlb3_ref),
                                         DILATIONS):
            _to_classes(dlb_ref, slice(0, 128), dlb, perm_ref, dil)
            for i in range(qb // PAIR):
                sl = slice(i * PAIR, (i + 1) * PAIR)
                _to_classes(dob_ref, sl, dob[:, sl], perm_ref, dil)

        @pl.when(pl.program_id(0) == 0)
        def _():
            dga_ref[...] = dga
            dgb_ref[...] = dgb

        @pl.when(pl.program_id(0) > 0)
        def _():
            dga_ref[...] += dga
            dgb_ref[...] += dgb

    def row(w):
        return pl.BlockSpec((tm, w), lambda i: (i, 0))

    def vec(w):
        return pl.BlockSpec((1, w), lambda i: (0, 0))

    res, hook_res = _call(
        body, hook, name="mix_bwd", grid=(s // tm,),
        in_specs=[row(qa + qb), row(qa), row(qb), row(1), row(1), vec(qa), vec(qb)],
        out_specs=([row(qa)] + [_class_spec(d, tm, qb) for d in DILATIONS] + [row(128)]
                   + [_class_spec(d, tm, 128) for d in DILATIONS] + [vec(qa), vec(qb)]),
        out_shape=([jax.ShapeDtypeStruct((s, qa), BF16)] + [_class_shape(d, s, qb, BF16) for d in DILATIONS]
                   + [jax.ShapeDtypeStruct((s, 128), F32)] + [_class_shape(d, s, 128, F32) for d in DILATIONS]
                   + [jax.ShapeDtypeStruct((1, qa), F32), jax.ShapeDtypeStruct((1, qb), F32)]),
        scratch_shapes=[pltpu.VMEM((tm, 128), F32)],
        compiler_params=_params(("arbitrary",), tm * (qa + qb) * 16),
    )(dmix, oa, ob, ra, rb, ga, gb)
    return res if hook is None else (res, hook_res)


def _assemble_dproj(dqa, dkva, dqs, dks, dvs):
    s, qa = dqa.shape
    kva = dkva.shape[1]
    qb = dqs[0].shape[2]
    width = qa + kva + 3 * qb
    tm = _row_tile(s)

    def body(dqa_ref, dkva_ref, q1, q2, q3, k1, k2, k3, v1, v2, v3, dp_ref, db_ref, perm_ref):
        first = pl.program_id(0) == 0

        def emit(off, val):
            dp_ref[:, off:off + PAIR] = val.astype(BF16)
            col = jnp.sum(val, axis=0, keepdims=True)

            @pl.when(first)
            def _():
                db_ref[:, off:off + PAIR] = col

            @pl.when(jnp.logical_not(first))
            def _():
                db_ref[:, off:off + PAIR] += col

        for i in range(qa // PAIR):
            emit(i * PAIR, dqa_ref[:, i * PAIR:(i + 1) * PAIR])
        for i in range(kva // PAIR):
            emit(qa + i * PAIR, dkva_ref[:, i * PAIR:(i + 1) * PAIR])
        for j, branch_refs in enumerate(((q1, q2, q3), (k1, k2, k3), (v1, v2, v3))):
            for i in range(qb // PAIR):
                sl = slice(i * PAIR, (i + 1) * PAIR)
                total = None
                for ref, dil in zip(branch_refs, DILATIONS):
                    val = _from_classes(ref, sl, perm_ref, dil)
                    total = val if total is None else total + val
                emit(qa + kva + j * qb + i * PAIR, total)

    def row(w):
        return pl.BlockSpec((tm, w), lambda i: (i, 0))

    return pl.pallas_call(
        body, name="assemble_dproj", grid=(s // tm,),
        in_specs=[row(qa), row(kva)] + [_class_spec(d, tm, qb) for d in DILATIONS] * 3,
        out_specs=[row(width), pl.BlockSpec((1, width), lambda i: (0, 0))],
        out_shape=[jax.ShapeDtypeStruct((s, width), BF16), jax.ShapeDtypeStruct((1, width), F32)],
        scratch_shapes=[pltpu.VMEM((tm, 128), F32)],
        compiler_params=_params(("arbitrary",), tm * (qa + kva + 9 * qb) * 4 + tm * width * 2),
    )(dqa, dkva, *dqs, *dks, *dvs)


def _band_masks(max_steps, dil, first_block):
    qi = lax.broadcasted_iota(jnp.int32, (BLOCK, 2 * BLOCK), 0)
    kj = lax.broadcasted_iota(jnp.int32, (BLOCK, 2 * BLOCK), 1)
    steps = qi + BLOCK - kj
    valid = (steps >= 0) & (steps <= max_steps) & ((kj >= BLOCK) | jnp.logical_not(first_block))
    dist = (steps * dil).astype(F32)
    return valid, dist


def _swap_halves(t):
    return pltpu.roll(t, HEAD_DIM, 1)


def _dup_group(t_bf16, group):
    t = t_bf16.astype(F32)
    low = lax.broadcasted_iota(jnp.int32, t.shape, 1) < HEAD_DIM
    keep = low if group == 0 else jnp.logical_not(low)
    return jnp.where(keep, t, _swap_halves(t)).astype(BF16)


def _attn_fwd(name, q, kv, *, dil, max_steps, slopes, sinks=None, hook=None):
    grouped = sinks is not None
    _, length, w = q.shape
    n_pairs = w // PAIR
    nb = length // BLOCK
    heads_per_group = 2 * n_pairs // N_KV_GROUPS

    def body(*refs):
        if grouped:
            sink_ref, q_ref, kvp_ref, kvc_ref, o_ref, lse_ref = refs
        else:
            q_ref, kp_ref, kc_ref, vp_ref, vc_ref, o_ref, lse_ref = refs
        n = pl.program_id(1)
        valid, dist = _band_masks(max_steps, dil, n == 0)
        low = _low_lanes(BLOCK)
        lane = lax.broadcasted_iota(jnp.int32, (BLOCK, 128), 1)
        lse_acc = jnp.zeros((BLOCK, 128), F32)
        if grouped:
            kv_all = jnp.concatenate([kvp_ref[...], kvc_ref[...]], axis=0)
            k_dup = [_dup_group(kv_all[:, 0:PAIR], g) for g in range(N_KV_GROUPS)]
            v_dup = [_dup_group(kv_all[:, PAIR:2 * PAIR], g) for g in range(N_KV_GROUPS)]
        for i in range(n_pairs):
            sl = slice(i * PAIR, (i + 1) * PAIR)
            q2 = q_ref[:, sl]
            if grouped:
                kk, vv = k_dup[2 * i // heads_per_group], v_dup[2 * i // heads_per_group]
            else:
                kk = jnp.concatenate([kp_ref[:, sl], kc_ref[:, sl]], axis=0)
                vv = jnp.concatenate([vp_ref[:, sl], vc_ref[:, sl]], axis=0)
            outs = []
            for half in (0, 1):
                h = 2 * i + half
                qm = jnp.where(low if half == 0 else jnp.logical_not(low), q2, jnp.zeros_like(q2))
                sc = lax.dot_general(qm, kk, (((1,), (1,)), ((), ())), preferred_element_type=F32)
                sc = jnp.where(valid, sc * ATT_SCALE - slopes[h] * dist, NEG_INF)
                m = jnp.max(sc, axis=-1, keepdims=True)
                if grouped:
                    m = jnp.maximum(m, sink_ref[h])
                p = jnp.exp(sc - m)
                den = jnp.sum(p, axis=-1, keepdims=True)
                if grouped:
                    den = den + jnp.exp(sink_ref[h] - m)
                o = jnp.dot(p.astype(BF16), vv, preferred_element_type=F32)
                outs.append(o / den)
                lse_acc = jnp.where(lane == h, m + jnp.log(den), lse_acc)
            o_ref[:, sl] = jnp.where(low, outs[0], outs[1])
        lse_ref[...] = lse_acc

    def cur(width):
        return pl.BlockSpec((None, BLOCK, width), lambda r, n: (r, n, 0))

    def prev(width):
        return pl.BlockSpec((None, BLOCK, width), lambda r, n: (r, jnp.maximum(n - 1, 0), 0))

    if grouped:
        kvw = kv.shape[2]
        operands = [sinks, q, kv, kv]
        in_specs = [SMEM_SPEC, cur(w), prev(kvw), cur(kvw)]
    else:
        operands = [q, kv[0], kv[0], kv[1], kv[1]]
        in_specs = [cur(w), prev(w), cur(w), prev(w), cur(w)]
    res, hook_res = _call(
        body, hook, name=name, grid=(dil, nb), in_specs=in_specs,
        out_specs=[cur(w), cur(128)],
        out_shape=[jax.ShapeDtypeStruct((dil, length, w), F32),
                   jax.ShapeDtypeStruct((dil, length, 128), F32)],
        compiler_params=_params(("parallel", "parallel"), BLOCK * w * 16),
    )(*operands)
    return res if hook is None else (res, hook_res)


def _attn_bwd(name, q, kv, do, lse, delta, *, dil, max_steps, slopes, sinks=None, hook=None):
    grouped = sinks is not None
    _, length, w = q.shape
    n_pairs = w // PAIR
    nb = length // BLOCK
    heads_per_group = 2 * n_pairs // N_KV_GROUPS
    pairs_per_group = n_pairs // N_KV_GROUPS

    def body(*refs):
        if grouped:
            (sink_ref, q_ref, kvp_ref, kvc_ref, do_ref, lse_ref, dl_ref,
             dq_ref, dkv_ref, dsink_ref, acc_ref) = refs
        else:
            (q_ref, kp_ref, kc_ref, vp_ref, vc_ref, do_ref, lse_ref, dl_ref,
             dq_ref, dk_ref, dv_ref, acck_ref, accv_ref) = refs
        n = pl.program_id(1)

        @pl.when(n == 0)
        def _():
            if grouped:
                acc_ref[...] = jnp.zeros_like(acc_ref)

                @pl.when(pl.program_id(0) == 0)
                def _():
                    dsink_ref[...] = jnp.zeros_like(dsink_ref)
            else:
                acck_ref[...] = jnp.zeros_like(acck_ref)
                accv_ref[...] = jnp.zeros_like(accv_ref)

        @pl.when(n == nb)
        def _():
            if grouped:
                dkv_ref[...] = acc_ref[...]
            else:
                dk_ref[...] = acck_ref[...]
                dv_ref[...] = accv_ref[...]

        @pl.when(n < nb)
        def _():
            valid, dist = _band_masks(max_steps, dil, n == 0)
            low = _low_lanes(BLOCK)
            low_kv = _low_lanes(2 * BLOCK)
            lane1 = lax.broadcasted_iota(jnp.int32, (1, 128), 1)
            if grouped:
                kv_all = jnp.concatenate([kvp_ref[...], kvc_ref[...]], axis=0)
                k_dup = [_dup_group(kv_all[:, 0:PAIR], g) for g in range(N_KV_GROUPS)]
                v_dup = [_dup_group(kv_all[:, PAIR:2 * PAIR], g) for g in range(N_KV_GROUPS)]
                dk_grp = [jnp.zeros((2 * BLOCK, PAIR), F32) for _ in range(N_KV_GROUPS)]
                dv_grp = [jnp.zeros((2 * BLOCK, PAIR), F32) for _ in range(N_KV_GROUPS)]
                dsink = jnp.zeros((1, 128), F32)
            for i in range(n_pairs):
                sl = slice(i * PAIR, (i + 1) * PAIR)
                q2 = q_ref[:, sl]
                do2 = do_ref[:, sl]
                if grouped:
                    grp = 2 * i // heads_per_group
                    kk, vv = k_dup[grp], v_dup[grp]
                else:
                    kk = jnp.concatenate([kp_ref[:, sl], kc_ref[:, sl]], axis=0)
                    vv = jnp.concatenate([vp_ref[:, sl], vc_ref[:, sl]], axis=0)
                dkk = jnp.zeros((2 * BLOCK, PAIR), F32)
                dvv = jnp.zeros((2 * BLOCK, PAIR), F32)
                dqs = []
                for half in (0, 1):
                    h = 2 * i + half
                    keep = low if half == 0 else jnp.logical_not(low)
                    qm = jnp.where(keep, q2, jnp.zeros_like(q2))
                    dom = jnp.where(keep, do2, jnp.zeros_like(do2))
                    lse_h = lse_ref[:, h:h + 1]
                    dl_h = dl_ref[:, h:h + 1]
                    sc = lax.dot_general(qm, kk, (((1,), (1,)), ((), ())), preferred_element_type=F32)
                    sc = jnp.where(valid, sc * ATT_SCALE - slopes[h] * dist, NEG_INF)
                    p = jnp.exp(sc - lse_h)
                    dp = lax.dot_general(dom, vv, (((1,), (1,)), ((), ())), preferred_element_type=F32)
                    ds = (p * (dp - dl_h)).astype(BF16)
                    dqs.append(jnp.dot(ds, kk, preferred_element_type=F32))
                    dkk = dkk + lax.dot_general(ds, qm, (((0,), (0,)), ((), ())),
                                                preferred_element_type=F32)
                    dvv = dvv + lax.dot_general(p.astype(BF16), dom, (((0,), (0,)), ((), ())),
                                                preferred_element_type=F32)
                    if grouped:
                        contrib = -jnp.sum(jnp.exp(sink_ref[h] - lse_h) * dl_h, axis=0, keepdims=True)
                        dsink = jnp.where(lane1 == h, dsink + contrib, dsink)
                dq_ref[:, sl] = jnp.where(low, dqs[0], dqs[1]) * ATT_SCALE
                if grouped:
                    dk_grp[grp] = dk_grp[grp] + dkk
                    dv_grp[grp] = dv_grp[grp] + dvv
                else:
                    dkk = dkk * ATT_SCALE
                    dk_ref[:, sl] = acck_ref[:, sl] + dkk[0:BLOCK]
                    acck_ref[:, sl] = dkk[BLOCK:2 * BLOCK]
                    dv_ref[:, sl] = accv_ref[:, sl] + dvv[0:BLOCK]
                    accv_ref[:, sl] = dvv[BLOCK:2 * BLOCK]
            if grouped:
                folded = [t + _swap_halves(t) for t in dk_grp + dv_grp]
                dk_tile = jnp.where(low_kv, folded[0], folded[1]) * ATT_SCALE
                dv_tile = jnp.where(low_kv, folded[2], folded[3])
                part = jnp.concatenate([dk_tile, dv_tile], axis=1)
                dkv_ref[...] = acc_ref[...] + part[0:BLOCK]
                acc_ref[...] = part[BLOCK:2 * BLOCK]
                dsink_ref[...] += dsink

    last = nb - 1

    def cur(width):
        return pl.BlockSpec((None, BLOCK, width), lambda r, n: (r, jnp.minimum(n, last), 0))

    def prev(width):
        return pl.BlockSpec((None, BLOCK, width),
                            lambda r, n: (r, jnp.maximum(jnp.minimum(n, last) - 1, 0), 0))

    def done(width):
        return pl.BlockSpec((None, BLOCK, width), lambda r, n: (r, jnp.maximum(n - 1, 0), 0))

    if grouped:
        assert pairs_per_group * N_KV_GROUPS == n_pairs and heads_per_group % 2 == 0
        kvw = kv.shape[2]
        operands = [sinks, q, kv, kv, do, lse, delta]
        in_specs = [SMEM_SPEC, cur(w), prev(kvw), cur(kvw), cur(w), cur(128), cur(128)]
        out_specs = [cur(w), done(kvw), pl.BlockSpec((1, 128), lambda r, n: (0, 0))]
        out_shape = [jax.ShapeDtypeStruct((dil, length, w), F32), jax.ShapeDtypeStruct((dil, length, kvw), F32),
                     jax.ShapeDtypeStruct((1, 128), F32)]
        scratch = [pltpu.VMEM((BLOCK, kvw), F32)]
    else:
        operands = [q, kv[0], kv[0], kv[1], kv[1], do, lse, delta]
        in_specs = [cur(w), prev(w), cur(w), prev(w), cur(w), cur(w), cur(128), cur(128)]
        out_specs = [cur(w), done(w), done(w)]
        out_shape = [jax.ShapeDtypeStruct((dil, length, w), F32)] * 3
        scratch = [pltpu.VMEM((BLOCK, w), F32), pltpu.VMEM((BLOCK, w), F32)]
    res, hook_res = _call(
        body, hook, name=name, grid=(dil, nb + 1), in_specs=in_specs, out_specs=out_specs,
        out_shape=out_shape, scratch_shapes=scratch,
        compiler_params=_params(("arbitrary", "arbitrary"), BLOCK * w * 32),
    )(*operands)
    return res if hook is None else (res, hook_res)


def _adamw(name, w, g, m, v):
    rows, cols = w.shape
    tm = 256 if rows % 256 == 0 else rows

    def body(w_ref, g_ref, m_ref, v_ref, d_ref, nm_ref, nv_ref):
        gv = g_ref[...]
        mn = ADAM_B1 * m_ref[...] + (1.0 - ADAM_B1) * gv
        vn = ADAM_B2 * v_ref[...] + (1.0 - ADAM_B2) * (gv * gv)
        m_hat = mn / (1.0 - ADAM_B1 ** ADAM_STEP)
        v_hat = vn / (1.0 - ADAM_B2 ** ADAM_STEP)
        d_ref[...] = -ADAM_LR * (m_hat / (jnp.sqrt(v_hat) + ADAM_EPS) + ADAM_WD * w_ref[...])
        nm_ref[...] = mn
        nv_ref[...] = vn

    spec = pl.BlockSpec((tm, cols), lambda i: (i, 0))
    return pl.pallas_call(
        body, name=name, grid=(rows // tm,), in_specs=[spec] * 4, out_specs=[spec] * 3,
        out_shape=[jax.ShapeDtypeStruct(w.shape, F32)] * 3,
        compiler_params=_params(("parallel",), tm * cols * 28),
    )(w, g, m, v)


def _mesh_position():
    return lax.axis_index("x"), lax.axis_index("y"), lax.axis_index("c")


def _other_chips(x, y):
    return [(1 - x, y), (x, 1 - y), (1 - x, 1 - y)]


def _gather_hook(shard, gathered, lo, hi):
    rows, cols = shard.shape
    half, n = rows // 2, hi - lo
    assert lo % 16 == 0 and n % 16 == 0 and half % 16 == 0
    first = gathered is None

    def region(out, owner_chip, which_half):
        return out.at[pl.ds(pl.multiple_of(owner_chip * rows + which_half * half + lo, 16), n)]

    def parts(ops, outs, sems):
        x, y, c = _mesh_position()
        return ops[0], outs[0], sems, x, y, c, 2 * x + y, (x, y, 1 - c), _other_chips(x, y)

    def local_copy(src, out, chip, sem):
        return pltpu.make_async_copy(src, out.at[pl.ds(pl.multiple_of(chip * rows, 16), rows)], sem)

    def start(ops, outs, sems):
        src, out, (send, recv, fsend, frecv, local), x, y, c, chip, sibling, others = parts(ops, outs, sems)
        if first:
            local_copy(src, out, chip, local.at[0]).start()
        mine = src.at[pl.ds(pl.multiple_of(c * half + lo, 16), n)]
        for k, (px, py) in enumerate(others):
            _remote(mine, region(out, chip, c), send.at[k], recv.at[k], (px, py, c)).start()

    def mid(ops, outs, sems):
        src, out, (send, recv, fsend, frecv, local), x, y, c, chip, sibling, others = parts(ops, outs, sems)
        for k, (px, py) in enumerate(others):
            landed = region(out, 2 * px + py, c)
            _remote(landed, landed, send.at[k], recv.at[k], (px, py, c)).wait_recv()
            _remote(landed, landed, fsend.at[k], frecv.at[k], sibling).start()

    def finish(ops, outs, sems):
        src, out, (send, recv, fsend, frecv, local), x, y, c, chip, sibling, others = parts(ops, outs, sems)
        mine = src.at[pl.ds(pl.multiple_of(c * half + lo, 16), n)]
        for k, (px, py) in enumerate(others):
            passed = region(out, 2 * px + py, 1 - c)
            _remote(passed, passed, fsend.at[k], frecv.at[k], sibling).wait_recv()
        for k, (px, py) in enumerate(others):
            landed = region(out, 2 * px + py, c)
            _remote(landed, landed, fsend.at[k], frecv.at[k], sibling).wait_send()
            _remote(mine, region(out, chip, c), send.at[k], recv.at[k], (px, py, c)).wait_send()
        if first:
            local_copy(src, out, chip, local.at[0]).wait()

    sems = [pltpu.SemaphoreType.DMA((3,))] * 4 + [pltpu.SemaphoreType.DMA((1,))]
    out_shape = [jax.ShapeDtypeStruct((N_CHIPS * rows, cols), shard.dtype)]
    if first:
        return _Hook([shard], out_shape, sems, start, finish, mid)
    return _Hook([shard, gathered], out_shape, sems, start, finish, mid, aliases={1: 0})


def _exchange_hook(grad):
    rows, cols = grad.shape[0] // N_CHIPS, grad.shape[1]
    half = rows // 2
    assert half % 16 == 0

    def copies(ops, outs, sems):
        x, y, c = _mesh_position()
        send, recv = sems
        return [_remote(ops[0].at[pl.ds(pl.multiple_of(k * rows + (1 - c) * half, 16), half)], outs[0].at[k],
                        send.at[k], recv.at[k], (x, y, 1 - c)) for k in range(N_CHIPS)]

    def start(ops, outs, sems):
        for cp in copies(ops, outs, sems):
            cp.start()

    def finish(ops, outs, sems):
        for cp in copies(ops, outs, sems):
            cp.wait_recv()
            cp.wait_send()

    return _Hook([grad], [jax.ShapeDtypeStruct((N_CHIPS, half, cols), grad.dtype)],
                 [pltpu.SemaphoreType.DMA((N_CHIPS,))] * 2, start, finish)


def _scatter_hook(chip_sum):
    _, half, cols = chip_sum.shape

    def copies(ops, outs, sems):
        x, y, c = _mesh_position()
        send, recv = sems
        return [_remote(ops[0].at[2 * px + py], outs[0].at[k], send.at[k], recv.at[k], (px, py, c))
                for k, (px, py) in enumerate(_other_chips(x, y))]

    def start(ops, outs, sems):
        for cp in copies(ops, outs, sems):
            cp.start()

    def finish(ops, outs, sems):
        for cp in copies(ops, outs, sems):
            cp.wait_recv()
            cp.wait_send()

    return _Hook([chip_sum], [jax.ShapeDtypeStruct((3, half, cols), chip_sum.dtype)],
                 [pltpu.SemaphoreType.DMA((3,))] * 2, start, finish)


def _sum_tile(half):
    return 256 if half % 256 == 0 else half


def _chip_add(name, grad, from_sibling, core):
    n_chips, half, cols = from_sibling.shape
    rows = 2 * half
    tr = _sum_tile(half)

    def body(core_ref, g_ref, s_ref, o_ref):
        o_ref[...] = (g_ref[...].astype(F32) + s_ref[...].astype(F32)).astype(o_ref.dtype)

    tile = pl.BlockSpec((None, tr, cols), lambda k, i, core_ref: (k, i, 0))
    return pl.pallas_call(
        body, name=name,
        grid_spec=pltpu.PrefetchScalarGridSpec(
            num_scalar_prefetch=1, grid=(n_chips, half // tr),
            in_specs=[pl.BlockSpec((tr, cols), lambda k, i, core_ref:
                                   (k * (rows // tr) + core_ref[0] * (half // tr) + i, 0)), tile],
            out_specs=tile),
        out_shape=jax.ShapeDtypeStruct(from_sibling.shape, from_sibling.dtype),
        compiler_params=_params(("parallel", "parallel"), 3 * tr * cols * 4),
    )(core, grad, from_sibling)


def _final_add(name, chip_sum, from_chips, chip):
    _, half, cols = chip_sum.shape
    tr = _sum_tile(half)

    def body(chip_ref, own_ref, others_ref, o_ref):
        total = own_ref[...].astype(F32)
        for k in range(3):
            total = total + others_ref[k].astype(F32)
        o_ref[...] = total

    return pl.pallas_call(
        body, name=name,
        grid_spec=pltpu.PrefetchScalarGridSpec(
            num_scalar_prefetch=1, grid=(half // tr,),
            in_specs=[pl.BlockSpec((None, tr, cols), lambda i, chip_ref: (chip_ref[0], i, 0)),
                      pl.BlockSpec((3, tr, cols), lambda i, chip_ref: (0, i, 0))],
            out_specs=pl.BlockSpec((tr, cols), lambda i, chip_ref: (i, 0))),
        out_shape=jax.ShapeDtypeStruct((half, cols), F32),
        compiler_params=_params(("parallel",), 6 * tr * cols * 4),
    )(chip, chip_sum, from_chips)


def _gather_weights(shards):
    n_w = len(shards)
    halves = [s.shape[0] // 2 for s in shards]

    def body(*refs):
        ins, outs = refs[:n_w], refs[n_w:2 * n_w]
        local_sems, send_sems, recv_sems, fsend_sems, frecv_sems = refs[2 * n_w:]
        x, y, c = _mesh_position()
        chip = 2 * x + y
        sibling = (x, y, 1 - c)
        others = _other_chips(x, y)

        def region(w, owner_chip, half):
            start = owner_chip * (2 * halves[w]) + half * halves[w]
            return outs[w].at[pl.ds(pl.multiple_of(start, 16), halves[w])]

        def remote(src, dst, ssem, rsem, dev):
            return pltpu.make_async_remote_copy(src_ref=src, dst_ref=dst, send_sem=ssem, recv_sem=rsem,
                                                device_id=dev, device_id_type=MESH)

        local = []
        for w in range(n_w):
            cp = pltpu.make_async_copy(ins[w], outs[w].at[pl.ds(pl.multiple_of(chip * 2 * halves[w], 16),
                                                                2 * halves[w])], local_sems.at[w])
            cp.start()
            local.append(cp)
        sends = []
        for w in range(n_w):
            mine = ins[w].at[pl.ds(pl.multiple_of(c * halves[w], 16), halves[w])]
            for k, (px, py) in enumerate(others):
                cp = remote(mine, region(w, chip, c), send_sems.at[w, k], recv_sems.at[w, k], (px, py, c))
                cp.start()
                sends.append(cp)
        for k, (px, py) in enumerate(others):
            for w in range(n_w):
                landed = region(w, 2 * px + py, c)
                remote(landed, landed, send_sems.at[w, k], recv_sems.at[w, k], (px, py, c)).wait_recv()
                cp = remote(landed, landed, fsend_sems.at[w, k], frecv_sems.at[w, k], sibling)
                cp.start()
                sends.append(cp)
        for k, (px, py) in enumerate(others):
            for w in range(n_w):
                passed = region(w, 2 * px + py, 1 - c)
                remote(passed, passed, fsend_sems.at[w, k], frecv_sems.at[w, k], sibling).wait_recv()
        for cp in sends:
            cp.wait_send()
        for cp in local:
            cp.wait()

    return pl.pallas_call(
        body, name="gather_weights",
        in_specs=[HBM_SPEC] * n_w, out_specs=[HBM_SPEC] * n_w,
        out_shape=[jax.ShapeDtypeStruct((N_CHIPS * s.shape[0], s.shape[1]), s.dtype) for s in shards],
        scratch_shapes=[pltpu.SemaphoreType.DMA((n_w,))] + [pltpu.SemaphoreType.DMA((n_w, 3))] * 4,
    )(*shards)


REDUCE_CHUNK_ROWS = 256


def _reduce_gradients(grads, shares, small):
    n_w = len(grads)
    rows = [g.shape[0] // N_CHIPS for g in grads]
    halves = [r // 2 for r in rows]
    cols = grads[0].shape[1]
    wire = grads[0].dtype
    ch = REDUCE_CHUNK_ROWS
    for h in halves:
        assert h % 16 == 0
    rows_s = small.shape[0]
    n_s = len(shares)

    def body(*refs):
        g_in = refs[:n_w]
        small_ref = refs[n_w]
        share_in = refs[n_w + 1:n_w + 1 + n_s]
        refs = refs[:n_w + 1] + refs[n_w + 1 + n_s:]
        outs = refs[n_w + 1:2 * n_w + 1]
        small_out = refs[2 * n_w + 1]
        from_sib = refs[2 * n_w + 2:3 * n_w + 2]
        chip_sum = refs[3 * n_w + 2:4 * n_w + 2]
        from_chips = refs[4 * n_w + 2:5 * n_w + 2]
        share_out = refs[5 * n_w + 2:5 * n_w + 2 + n_s]
        (small_all, buf_a, buf_b, buf_o, sib_send, sib_recv, chip_send, chip_recv,
         fin_send, fin_recv, small_send, small_recv, io_sem,
         share_send, share_recv, share_local) = refs[5 * n_w + 2 + n_s:]
        x, y, c = _mesh_position()
        chip = 2 * x + y
        me = 4 * x + 2 * y + c
        sibling = (x, y, 1 - c)
        others = _other_chips(x, y)

        def remote(src, dst, ssem, rsem, dev):
            return pltpu.make_async_remote_copy(src_ref=src, dst_ref=dst, send_sem=ssem, recv_sem=rsem,
                                                device_id=dev, device_id_type=MESH)

        def part(w, owner_chip, half):
            start = owner_chip * rows[w] + half * halves[w]
            return g_in[w].at[pl.ds(pl.multiple_of(start, 16), halves[w])]

        pending = []
        small_all[me] = small_ref[...]
        for j in range(N_DEV - 1):
            peer = (me + 1 + j) % N_DEV
            cp = remote(small_all.at[me], small_all.at[me], small_send.at[j], small_recv.at[j],
                        (peer // 4, (peer // 2) % 2, peer % 2))
            cp.start()
            pending.append(cp)

        local = []
        for i in range(n_s):
            half_rows = share_in[i].shape[0]
            place = share_out[i].at[pl.ds(pl.multiple_of(c * half_rows, 16), half_rows)]
            cp = pltpu.make_async_copy(share_in[i], place, share_local.at[i])
            cp.start()
            local.append(cp)
            cp = remote(share_in[i], place, share_send.at[i], share_recv.at[i], sibling)
            cp.start()
            pending.append(cp)

        for w in range(n_w):
            for k in range(N_CHIPS):
                cp = remote(part(w, k, 1 - c), from_sib[w].at[k], sib_send.at[w, k], sib_recv.at[w, k], sibling)
                cp.start()
                pending.append(cp)

        def add_stream(w, srcs, dst, n_rows):
            def chunk(start, size):
                total = None
                for i, src in enumerate(srcs):
                    buf = buf_a if i % 2 == 0 else buf_b
                    cp = pltpu.make_async_copy(src.at[pl.ds(start, size)], buf.at[pl.ds(0, size)], io_sem)
                    cp.start()
                    cp.wait()
                    val = buf[pl.ds(0, size), :].astype(F32)
                    total = val if total is None else total + val
                return total

            n_full = n_rows // ch
            rem = n_rows - n_full * ch

            def store(total, start, size):
                if dst.dtype == F32:
                    buf_o[pl.ds(0, size), :] = total
                    cp = pltpu.make_async_copy(buf_o.at[pl.ds(0, size)], dst.at[pl.ds(start, size)], io_sem)
                else:
                    buf_a[pl.ds(0, size), :] = total.astype(buf_a.dtype)
                    cp = pltpu.make_async_copy(buf_a.at[pl.ds(0, size)], dst.at[pl.ds(start, size)], io_sem)
                cp.start()
                cp.wait()

            def loop_body(i, carry):
                start = pl.multiple_of(i * ch, ch)
                store(chunk(start, ch), start, ch)
                return carry

            lax.fori_loop(0, n_full, loop_body, 0)
            if rem:
                store(chunk(n_full * ch, rem), n_full * ch, rem)

        order = [2, 0, 1]
        for w in range(n_w):
            for k in range(N_CHIPS):
                remote(part(w, k, 1 - c), from_sib[w].at[k], sib_send.at[w, k], sib_recv.at[w, k],
                       sibling).wait_recv()
        for k in order:
            px, py = others[k]
            owner = 2 * px + py
            for w in range(n_w):
                add_stream(w, [part(w, owner, c), from_sib[w].at[owner]], chip_sum[w].at[owner], halves[w])
                cp = remote(chip_sum[w].at[owner], from_chips[w].at[k], chip_send.at[w, k],
                            chip_recv.at[w, k], (px, py, c))
                cp.start()
                pending.append(cp)
        for w in range(n_w):
            add_stream(w, [part(w, chip, c), from_sib[w].at[chip]], chip_sum[w].at[chip], halves[w])

        for w in range(n_w):
            for k in range(3):
                px, py = others[k]
                remote(chip_sum[w].at[chip], from_chips[w].at[k], chip_send.at[w, k], chip_recv.at[w, k],
                       (px, py, c)).wait_recv()
            mine = outs[w].at[pl.ds(pl.multiple_of(c * halves[w], 16), halves[w])]
            add_stream(w, [chip_sum[w].at[chip], from_chips[w].at[0], from_chips[w].at[1],
                           from_chips[w].at[2]], mine, halves[w])
            cp = remote(mine, mine, fin_send.at[w], fin_recv.at[w], sibling)
            cp.start()
            pending.append(cp)
        for w in range(n_w):
            theirs = outs[w].at[pl.ds(pl.multiple_of((1 - c) * halves[w], 16), halves[w])]
            remote(theirs, theirs, fin_send.at[w], fin_recv.at[w], sibling).wait_recv()

        for j in range(N_DEV - 1):
            peer = (me + N_DEV - 1 - j) % N_DEV
            remote(small_all.at[peer], small_all.at[peer], small_send.at[j], small_recv.at[j],
                   sibling).wait_recv()
        total = small_all[0]
        for d in range(1, N_DEV):
            total = total + small_all[d]
        small_out[...] = total
        for i in range(n_s):
            half_rows = share_in[i].shape[0]
            theirs = share_out[i].at[pl.ds(pl.multiple_of((1 - c) * half_rows, 16), half_rows)]
            remote(share_in[i], theirs, share_send.at[i], share_recv.at[i], sibling).wait_recv()
        for cp in pending:
            cp.wait_send()
        for cp in local:
            cp.wait()

    hbm_scratch = ([jax.ShapeDtypeStruct((N_CHIPS, h, cols), wire) for h in halves] * 2
                   + [jax.ShapeDtypeStruct((3, h, cols), wire) for h in halves])
    out_shape = ([jax.ShapeDtypeStruct((r, cols), F32) for r in rows]
                 + [jax.ShapeDtypeStruct((rows_s, 128), F32)] + hbm_scratch
                 + [jax.ShapeDtypeStruct((2 * sh.shape[0], sh.shape[1]), F32) for sh in shares])
    res = pl.pallas_call(
        body, name="reduce_gradients",
        in_specs=[HBM_SPEC] * n_w + [VMEM_SPEC] + [HBM_SPEC] * n_s,
        out_specs=[HBM_SPEC] * n_w + [VMEM_SPEC] + [HBM_SPEC] * (3 * n_w + n_s),
        out_shape=out_shape,
        scratch_shapes=[
            pltpu.VMEM((N_DEV, rows_s, 128), F32),
            pltpu.VMEM((ch, cols), wire), pltpu.VMEM((ch, cols), wire), pltpu.VMEM((ch, cols), F32),
            pltpu.SemaphoreType.DMA((n_w, N_CHIPS)), pltpu.SemaphoreType.DMA((n_w, N_CHIPS)),
            pltpu.SemaphoreType.DMA((n_w, 3)), pltpu.SemaphoreType.DMA((n_w, 3)),
            pltpu.SemaphoreType.DMA((n_w,)), pltpu.SemaphoreType.DMA((n_w,)),
            pltpu.SemaphoreType.DMA((N_DEV - 1,)), pltpu.SemaphoreType.DMA((N_DEV - 1,)),
            pltpu.SemaphoreType.DMA,
            pltpu.SemaphoreType.DMA((max(n_s, 1),)), pltpu.SemaphoreType.DMA((max(n_s, 1),)),
            pltpu.SemaphoreType.DMA((max(n_s, 1),)),
        ],
        compiler_params=pltpu.CompilerParams(vmem_limit_bytes=VMEM_LIMIT_MIN),
    )(*grads, small, *shares)
    return res[:n_w], res[n_w], res[len(res) - n_s:] if n_s else []


def _pack_small(parts, rows):
    flat = jnp.concatenate([p.reshape(-1) for p in parts])
    flat = jnp.pad(flat, (0, rows * 128 - flat.shape[0]))
    return flat.reshape(rows, 128)


def _unpack_small(packed, shapes):
    flat = packed.reshape(-1)
    out, off = [], 0
    for shp in shapes:
        n = int(np.prod(shp))
        out.append(flat[off:off + n].reshape(shp))
        off += n
    return out


def kernel(x, g_attn, w_in, b_in, sinks_a, g_out_a, g_out_b, w_out, g_mlp, w_1, w_2, g_final, loss_target, m_g_attn, m_w_in, m_b_in, m_sinks_a, m_g_out_a, m_g_out_b, m_w_out, m_g_mlp, m_w_1, m_w_2, m_g_final, v_g_attn, v_w_in, v_b_in, v_sinks_a, v_g_out_a, v_g_out_b, v_w_out, v_g_mlp, v_w_1, v_w_2, v_g_final):
    s, d = x.shape[1], x.shape[2]
    d_in = b_in.shape[1]
    qa = g_out_a.shape[1]
    qb = g_out_b.shape[1]
    kva = 2 * N_KV_GROUPS * HEAD_DIM
    assert d_in == qa + kva + 3 * qb and qa + qb == w_out.shape[1] * N_CHIPS
    d_ff = w_1.shape[2] * N_CHIPS
    ff_shard = w_1.shape[2]
    in_shard = w_in.shape[2]
    n_heads_a, n_heads_b = qa // HEAD_DIM, qb // HEAD_DIM
    slopes_a, slopes_b = alibi_slopes(n_heads_a), alibi_slopes(n_heads_b)

    x2d = x[0]
    target = loss_target[0]

    shards = [w_in[0].T.astype(BF16), w_out[0].astype(BF16), w_1[0].astype(BF16), w_2[0].astype(BF16)]
    w_in_t, = _gather_weights(shards[:1])
    core_index = lax.axis_index("c").astype(jnp.int32).reshape(1)
    chip_index = (2 * lax.axis_index("x") + lax.axis_index("y")).astype(jnp.int32).reshape(1)

    tm = _tile(s, 1024)

    h1, r1 = _norm_fwd("norm_attn", x2d, g_attn)

    q_a, = _project_by_class("proj_qa", h1, w_in_t, b_in, 0, qa, (1,))
    kv_a, = _project_by_class("proj_kva", h1, w_in_t, b_in, qa, kva, (1,))
    q_bs, (w_out_g,) = _project_by_class("proj_qb", h1, w_in_t, b_in, qa + kva, qb, DILATIONS,
                                         hook=_gather_hook(shards[1], None, 0, shards[1].shape[0] // 2))
    k_bs = _project_by_class("proj_kb", h1, w_in_t, b_in, qa + kva + qb, qb, DILATIONS)
    v_bs = _project_by_class("proj_vb", h1, w_in_t, b_in, qa + kva + 2 * qb, qb, DILATIONS)

    quarter = shards[2].shape[0] // 8
    sinks = sinks_a.reshape(-1)
    (o_a, lse_a), (w_1_g,) = _attn_fwd("attn_a_fwd", q_a, kv_a, dil=1, max_steps=WINDOW_A - 1, slopes=slopes_a,
                                       sinks=sinks, hook=_gather_hook(shards[2], None, 0, quarter))
    o_a = o_a[0]
    o_bs, lse_bs = [], []
    for n, (window, dil) in enumerate(DILATED_BRANCHES):
        (o, l), (w_1_g,) = _attn_fwd(f"attn_b{dil}_fwd", q_bs[n], (k_bs[n], v_bs[n]), dil=dil,
                                     max_steps=window // dil, slopes=slopes_b,
                                     hook=_gather_hook(shards[2], w_1_g, (n + 1) * quarter, (n + 2) * quarter))
        o_bs.append(o)
        lse_bs.append(l)
    w_1_g = w_1_g.reshape(N_CHIPS, d, ff_shard)
    mix, o_b, *lse_tot, r_a, r_b = _mix_fwd(o_a, o_bs, lse_bs, g_out_a, g_out_b)

    tn = _tile(d, 512)
    a_spec, b_spec = _mm_specs("nn", tm, tn, d)
    tile_mn = pl.BlockSpec((tm, tn), lambda i, j, k: (i, j))
    x2 = _matmul("out_proj", mix, w_out_g, [x2d], mode="nn", grid=(s // tm, d // tn, 1),
                 a_spec=a_spec, b_spec=b_spec, extra_specs=[tile_mn],
                 out_shapes=[jax.ShapeDtypeStruct((s, d), F32)], out_specs=[tile_mn],
                 epilogue=lambda acc, res: (acc + res,))[0]

    h2, r2 = _norm_fwd("norm_mlp", x2, g_mlp)

    tn = _tile(ff_shard, 512)
    per = ff_shard // tn
    a_spec, _ = _mm_specs("nn", tm, tn, d)
    (u,), (w_2_g,) = _matmul(
        "mlp_up", h2, w_1_g, [], mode="nn", grid=(s // tm, d_ff // tn, 1),
        a_spec=a_spec, b_spec=pl.BlockSpec((None, d, tn), lambda i, j, k: (j // per, 0, j % per)),
        extra_specs=[], out_shapes=[jax.ShapeDtypeStruct((s, d_ff), BF16)], out_specs=[tile_mn],
        epilogue=lambda acc: (jnp.maximum(acc, 0.0),),
        hook=_gather_hook(shards[3], None, 0, shards[3].shape[0] // 2))

    tn = _tile(d, 1024)
    tk = _tile(d_ff, 2048)
    a_spec, b_spec = _mm_specs("nn", tm, tn, tk)
    tile_mn = pl.BlockSpec((tm, tn), lambda i, j, k: (i, j))
    x3 = _matmul("mlp_down", u, w_2_g, [x2], mode="nn", grid=(s // tm, d // tn, d_ff // tk),
                 a_spec=a_spec, b_spec=b_spec, extra_specs=[tile_mn],
                 out_shapes=[jax.ShapeDtypeStruct((s, d), F32)], out_specs=[tile_mn],
                 prologue=lambda a: a * a, epilogue=lambda acc, res: (acc + res,), acc_shape=(tm, tn))[0]

    dx3, dx3b, loss_part, dg_final = _loss_head(x3, target, g_final.reshape(1, d))

    tn = _tile(d_ff, 512)
    a_spec, b_spec = _mm_specs("nt", tm, tn, d)
    tile_mn = pl.BlockSpec((tm, tn), lambda i, j, k: (i, j))
    dpre = _matmul("mlp_down_dx", dx3b, w_2_g, [u], mode="nt", grid=(s // tm, d_ff // tn, 1),
                   a_spec=a_spec, b_spec=b_spec, extra_specs=[tile_mn],
                   out_shapes=[jax.ShapeDtypeStruct((s, d_ff), BF16)], out_specs=[tile_mn],
                   epilogue=lambda acc, uu: (acc * (2.0 * uu.astype(F32)),))[0]

    wire = GRAD_WIRE_DTYPE
    tk_s = _tile(s, 2048)
    tmw = _tile(d_ff, 1024)
    a_spec, b_spec = _mm_specs("tn", tmw, d, tk_s)
    dw_2 = _matmul("mlp_down_dw", u, dx3b, [], mode="tn", grid=(d_ff // tmw, 1, s // tk_s),
                   a_spec=a_spec, b_spec=b_spec, extra_specs=[],
                   out_shapes=[jax.ShapeDtypeStruct((d_ff, d), wire)],
                   out_specs=[pl.BlockSpec((tmw, d), lambda i, j, k: (i, j))],
                   prologue=lambda a: a * a, epilogue=lambda acc: (acc,), acc_shape=(tmw, d))[0]

    tn = _tile(d, 1024)
    tk = _tile(ff_shard, 2048)
    per = ff_shard // tk
    a_spec, _ = _mm_specs("nt", tm, tn, tk)
    tile_mn = pl.BlockSpec((tm, tn), lambda i, j, k: (i, j))
    dh2 = _matmul("mlp_up_dx", dpre, w_1_g, [], mode="nt", grid=(s // tm, d // tn, d_ff // tk),
                  a_spec=a_spec, b_spec=pl.BlockSpec((None, tn, tk), lambda i, j, k: (k // per, j, k % per)),
                  extra_specs=[], out_shapes=[jax.ShapeDtypeStruct((s, d), F32)], out_specs=[tile_mn],
                  epilogue=lambda acc: (acc,), acc_shape=(tm, tn))[0]

    tmw = _tile(d, 1024)
    tnw = _tile(ff_shard, 2048)
    per = ff_shard // tnw
    a_spec, b_spec = _mm_specs("tn", tmw, tnw, tk_s)
    dw_1 = _matmul("mlp_up_dw", h2, dpre, [], mode="tn", grid=(d // tmw, d_ff // tnw, s // tk_s),
                   a_spec=a_spec, b_spec=b_spec, extra_specs=[],
                   out_shapes=[jax.ShapeDtypeStruct((N_CHIPS, d, ff_shard), wire)],
                   out_specs=[pl.BlockSpec((None, tmw, tnw), lambda i, j, k: (j // per, i, j % per))],
                   epilogue=lambda acc: (acc,), acc_shape=(tmw, tnw))[0]

    dw_1 = dw_1.reshape(N_CHIPS * d, ff_shard)
    (dx2, dx2b, dg_mlp), (sib_2, sib_1) = _norm_bwd(
        "norm_mlp_bwd", dh2, x2, r2, g_mlp, dx3, hook=_merge_hooks([_exchange_hook(dw_2), _exchange_hook(dw_1)]))
    chip_sum_2 = _chip_add("chip_add_w_2", dw_2, sib_2, core_index)
    chip_sum_1 = _chip_add("chip_add_w_1", dw_1, sib_1, core_index)

    tn = _tile(d, 512)
    a_spec, b_spec = _mm_specs("nt", tm, tn, d)
    tile_mn = pl.BlockSpec((tm, tn), lambda i, j, k: (i, j))
    dmix = _matmul("out_proj_dx", dx2b, w_out_g, [], mode="nt", grid=(s // tm, d // tn, 1),
                   a_spec=a_spec, b_spec=b_spec, extra_specs=[],
                   out_shapes=[jax.ShapeDtypeStruct((s, d), F32)], out_specs=[tile_mn],
                   epilogue=lambda acc: (acc,))[0]

    tmw = _tile(d, 1024)
    a_spec, b_spec = _mm_specs("tn", tmw, d, tk_s)
    dw_out = _matmul("out_proj_dw", mix, dx2b, [], mode="tn", grid=(d // tmw, 1, s // tk_s),
                     a_spec=a_spec, b_spec=b_spec, extra_specs=[],
                     out_shapes=[jax.ShapeDtypeStruct((d, d), wire)],
                     out_specs=[pl.BlockSpec((tmw, d), lambda i, j, k: (i, j))],
                     epilogue=lambda acc: (acc,), acc_shape=(tmw, d))[0]

    mix_grads, (sib_out,) = _mix_bwd(dmix, o_a, o_b, r_a, r_b, g_out_a, g_out_b, hook=_exchange_hook(dw_out))
    do_a, do_bs, delta_a, delta_bs = mix_grads[0], mix_grads[1:4], mix_grads[4], mix_grads[5:8]
    dg_out_a, dg_out_b = mix_grads[8:]
    chip_sum_out = _chip_add("chip_add_w_out", dw_out, sib_out, core_index)

    (dq_a, dkv_a, dsinks), (chips_2,) = _attn_bwd(
        "attn_a_bwd", q_a, kv_a, do_a[None], lse_a, delta_a[None], dil=1, max_steps=WINDOW_A - 1,
        slopes=slopes_a, sinks=sinks, hook=_scatter_hook(chip_sum_2))
    dqs, dks, dvs = [], [], []
    scatter = {1: chip_sum_1, 4: chip_sum_out}
    arrived = {}
    for n, (window, dil) in enumerate(DILATED_BRANCHES):
        res = _attn_bwd(f"attn_b{dil}_bwd", q_bs[n], (k_bs[n], v_bs[n]), do_bs[n], lse_tot[n],
                        delta_bs[n], dil=dil, max_steps=window // dil, slopes=slopes_b,
                        hook=_scatter_hook(scatter[dil]) if dil in scatter else None)
        if dil in scatter:
            res, (arrived[dil],) = res
        dq, dk, dv = res
        dqs.append(dq)
        dks.append(dk)
        dvs.append(dv)
    half_2 = _final_add("final_add_w_2", chip_sum_2, chips_2, chip_index)
    half_1 = _final_add("final_add_w_1", chip_sum_1, arrived[1], chip_index)
    half_out = _final_add("final_add_w_out", chip_sum_out, arrived[4], chip_index)
    dproj, db_in = _assemble_dproj(dq_a[0], dkv_a[0], dqs, dks, dvs)

    tn = _tile(d, 512)
    a_spec, b_spec = _mm_specs("nn", tm, tn, d_in)
    tile_mn = pl.BlockSpec((tm, tn), lambda i, j, k: (i, j))
    dh1 = _matmul("in_proj_dx", dproj, w_in_t, [], mode="nn", grid=(s // tm, d // tn, 1),
                  a_spec=a_spec, b_spec=b_spec, extra_specs=[],
                  out_shapes=[jax.ShapeDtypeStruct((s, d), F32)], out_specs=[tile_mn],
                  epilogue=lambda acc: (acc,))[0]

    tmw = d_in // 2 if (d_in // 2) % 128 == 0 else d_in
    tnw = _tile(d, 1024)
    tk_s = _tile(s, 1024)
    a_spec, b_spec = _mm_specs("tn", tmw, tnw, tk_s)
    dw_in_t = _matmul("in_proj_dw", dproj, h1, [], mode="tn", grid=(d_in // tmw, d // tnw, s // tk_s),
                      a_spec=a_spec, b_spec=b_spec, extra_specs=[],
                      out_shapes=[jax.ShapeDtypeStruct((d_in, d), wire)],
                      out_specs=[pl.BlockSpec((tmw, tnw), lambda i, j, k: (i, j))],
                      epilogue=lambda acc: (acc,), acc_shape=(tmw, tnw))[0]

    grad_x, _, dg_attn = _norm_bwd("norm_attn_bwd", dh1, x2d, r1, g_attn, dx2)

    small_parts = [dg_attn, db_in, dsinks[:, :n_heads_a], dg_out_a, dg_out_b, dg_mlp, dg_final]
    small_shapes = [g_attn.shape, b_in.shape, sinks_a.shape, g_out_a.shape, g_out_b.shape, g_mlp.shape,
                    g_final.shape]
    n_small = sum(int(np.prod(shp)) for shp in small_shapes)
    rows_s = -(-n_small // (8 * 128)) * 8
    (gw_in_t,), small_sum, (gw_out, gw_1, gw_2) = _reduce_gradients(
        [dw_in_t], [half_out, half_1, half_2], _pack_small(small_parts, rows_s))
    gw_in = gw_in_t.T
    g_small = _unpack_small(small_sum, small_shapes)

    upd_in = _adamw("adamw_w_in", w_in[0], gw_in, m_w_in[0], v_w_in[0])
    upd_out = _adamw("adamw_w_out", w_out[0], gw_out, m_w_out[0], v_w_out[0])
    upd_1 = _adamw("adamw_w_1", w_1[0], gw_1, m_w_1[0], v_w_1[0])
    upd_2 = _adamw("adamw_w_2", w_2[0], gw_2, m_w_2[0], v_w_2[0])
    small_w = [g_attn, b_in, sinks_a, g_out_a, g_out_b, g_mlp, g_final]
    small_m = [m_g_attn, m_b_in, m_sinks_a, m_g_out_a, m_g_out_b, m_g_mlp, m_g_final]
    small_v = [v_g_attn, v_b_in, v_sinks_a, v_g_out_a, v_g_out_b, v_g_mlp, v_g_final]
    upd_small = _adamw("adamw_small", _pack_small(small_w, rows_s), small_sum,
                       _pack_small(small_m, rows_s), _pack_small(small_v, rows_s))
    d_small, m_small, v_small = [_unpack_small(t, small_shapes) for t in upd_small]

    loss = lax.psum(loss_part[0, 0], ("x", "y", "c"))

    def ordered(small, big):
        w_in_v, w_out_v, w_1_v, w_2_v = big
        return [small[0], w_in_v[None], small[1], small[2], small[3], small[4], w_out_v[None], small[5],
                w_1_v[None], w_2_v[None], small[6]]

    grads = ordered(g_small, (gw_in, gw_out, gw_1, gw_2))
    deltas = ordered(d_small, (upd_in[0], upd_out[0], upd_1[0], upd_2[0]))
    new_m = ordered(m_small, (upd_in[1], upd_out[1], upd_1[1], upd_2[1]))
    new_v = ordered(v_small, (upd_in[2], upd_out[2], upd_1[2], upd_2[2]))
    return (loss, grad_x[None], *grads, *deltas, *new_m, *new_v)
```

```python
import functools

import jax
import jax.numpy as jnp
import numpy as np
from jax import lax
from jax.experimental import pallas as pl
from jax.experimental.pallas import tpu as pltpu

F32 = jnp.float32
BF16 = jnp.bfloat16

HEAD_DIM = 64
BLOCK = 128
PAIR = 2 * HEAD_DIM
N_KV_GROUPS = 2
WINDOW_A = 128
DILATED_BRANCHES = ((128, 1), (512, 4), (2048, 16))
EPS = 1e-5
NEG_INF = -1e30
ATT_SCALE = HEAD_DIM ** -0.5

ADAM_LR = 0.001
ADAM_B1 = 0.9
ADAM_B2 = 0.999
ADAM_EPS = 1e-08
ADAM_WD = 0.01
ADAM_STEP = 10

N_CHIPS = 4
N_DEV = 8
MESH = pl.DeviceIdType.MESH
GRAD_WIRE_DTYPE = jnp.bfloat16

VMEM_CAPACITY_V7X = 64 * 1024 * 1024
VMEM_LIMIT_MAX = 56 * 1024 * 1024
VMEM_LIMIT_MIN = 32 * 1024 * 1024

HBM_SPEC = pl.BlockSpec(memory_space=pltpu.HBM)
VMEM_SPEC = pl.BlockSpec(memory_space=pltpu.VMEM)
SMEM_SPEC = pl.BlockSpec(memory_space=pltpu.SMEM)


def _nbytes(shape, dtype):
    return int(np.prod([s for s in shape if s is not None])) * jnp.dtype(dtype).itemsize


def _params(semantics, block_bytes):
    limit = min(max(2 * block_bytes + (4 << 20), VMEM_LIMIT_MIN), VMEM_LIMIT_MAX)
    return pltpu.CompilerParams(dimension_semantics=semantics, vmem_limit_bytes=limit)


class _Hook:
    def __init__(self, operands, out_shape, sems, start, finish, mid=None, aliases=None):
        self.operands, self.out_shape, self.sems = list(operands), list(out_shape), list(sems)
        self.start, self.mid, self.finish = start, mid, finish
        self.aliases = dict(aliases or {})


def _merge_hooks(hooks):
    hooks = [h for h in hooks if h is not None]
    if len(hooks) <= 1:
        return hooks[0] if hooks else None
    n_op = np.cumsum([0] + [len(h.operands) for h in hooks])
    n_out = np.cumsum([0] + [len(h.out_shape) for h in hooks])
    n_sem = np.cumsum([0] + [len(h.sems) for h in hooks])

    def run(which):
        def fn(ops, outs, sems):
            for i, h in enumerate(hooks):
                f = getattr(h, which)
                if f is not None:
                    f(ops[n_op[i]:n_op[i + 1]], outs[n_out[i]:n_out[i + 1]], sems[n_sem[i]:n_sem[i + 1]])
        return fn

    aliases = {}
    for i, h in enumerate(hooks):
        aliases.update({int(n_op[i]) + a: int(n_out[i]) + b for a, b in h.aliases.items()})
    return _Hook(sum([h.operands for h in hooks], []), sum([h.out_shape for h in hooks], []),
                 sum([h.sems for h in hooks], []), run("start"), run("finish"),
                 run("mid") if any(h.mid for h in hooks) else None, aliases)


HOOK_MID_FRACTION = 0.6


def _call(body, hook, *, name, grid, in_specs, out_specs, out_shape, scratch_shapes=(), compiler_params):
    in_specs, out_specs, out_shape = list(in_specs), list(out_specs), list(out_shape)
    scratch_shapes = list(scratch_shapes)
    if hook is None:
        call = pl.pallas_call(body, name=name, grid=grid, in_specs=in_specs, out_specs=out_specs,
                              out_shape=out_shape, scratch_shapes=scratch_shapes,
                              compiler_params=compiler_params)
        return lambda *operands: (call(*operands), [])
    n_in, n_hin, n_out, n_hout, n_scr = (len(in_specs), len(hook.operands), len(out_specs),
                                         len(hook.out_shape), len(scratch_shapes))
    total = int(np.prod(grid))
    t_mid = min(int(total * HOOK_MID_FRACTION), total - 1)

    def wrapped(*refs):
        ins, h_in = refs[:n_in], refs[n_in:n_in + n_hin]
        o0 = n_in + n_hin
        outs, h_out = refs[o0:o0 + n_out], refs[o0 + n_out:o0 + n_out + n_hout]
        s0 = o0 + n_out + n_hout
        scr, h_sems = refs[s0:s0 + n_scr], refs[s0 + n_scr:]
        t = pl.program_id(0)
        for axis in range(1, len(grid)):
            t = t * grid[axis] + pl.program_id(axis)

        @pl.when(t == 0)
        def _():
            hook.start(h_in, h_out, h_sems)

        body(*ins, *outs, *scr)
        if hook.mid is not None:
            @pl.when(t == t_mid)
            def _():
                hook.mid(h_in, h_out, h_sems)

        @pl.when(t == total - 1)
        def _():
            hook.finish(h_in, h_out, h_sems)

    params = pltpu.CompilerParams(dimension_semantics=("arbitrary",) * len(grid),
                                  vmem_limit_bytes=compiler_params.vmem_limit_bytes)
    call = pl.pallas_call(
        wrapped, name=name, grid=grid,
        in_specs=in_specs + [HBM_SPEC] * n_hin, out_specs=out_specs + [HBM_SPEC] * n_hout,
        out_shape=out_shape + hook.out_shape, scratch_shapes=scratch_shapes + hook.sems,
        input_output_aliases={n_in + a: n_out + b for a, b in hook.aliases.items()},
        compiler_params=params)

    def run(*operands):
        res = call(*operands, *hook.operands)
        return res[:n_out], res[n_out:]

    return run


def _remote(src, dst, send_sem, recv_sem, device):
    return pltpu.make_async_remote_copy(src_ref=src, dst_ref=dst, send_sem=send_sem, recv_sem=recv_sem,
                                        device_id=device, device_id_type=MESH)


def alibi_slopes(n):
    return [float(v) for v in np.asarray(2.0 ** (-8.0 * (np.arange(n) + 1) / n), dtype=np.float32)]


def _matmul(name, a, b, extras, *, mode, grid, a_spec, b_spec, extra_specs, out_shapes, out_specs,
            epilogue, prologue=None, acc_shape=None, hook=None):
    dims = {"nn": ((1,), (0,)), "nt": ((1,), (1,)), "tn": ((0,), (0,))}[mode]
    nk = grid[2]
    n_ex, n_out = len(extras), len(out_shapes)

    def body(a_ref, b_ref, *rest):
        ex, outs = rest[:n_ex], rest[n_ex:n_ex + n_out]
        av = a_ref[...]
        if prologue is not None:
            av = prologue(av)
        part = lax.dot_general(av, b_ref[...], (dims, ((), ())), preferred_element_type=F32)

        def finish(acc):
            res = epilogue(acc, *[e[...] for e in ex])
            for o, r in zip(outs, res):
                o[...] = r.astype(o.dtype)

        if nk == 1:
            finish(part)
        else:
            acc_ref = rest[-1]
            k = pl.program_id(2)

            @pl.when(k == 0)
            def _():
                acc_ref[...] = part

            @pl.when(k > 0)
            def _():
                acc_ref[...] += part

            @pl.when(k == nk - 1)
            def _():
                finish(acc_ref[...])

    blocks = [(a_spec.block_shape, a.dtype), (b_spec.block_shape, b.dtype)]
    blocks += [(s.block_shape, e.dtype) for s, e in zip(extra_specs, extras)]
    blocks += [(s.block_shape, o.dtype) for s, o in zip(out_specs, out_shapes)]
    nbytes = sum(_nbytes(s, d) for s, d in blocks)
    scratch = []
    if nk > 1:
        scratch.append(pltpu.VMEM(acc_shape, F32))
        nbytes += _nbytes(acc_shape, F32)
    res, hook_res = _call(
        body, hook, name=name, grid=grid,
        in_specs=[a_spec, b_spec, *extra_specs], out_specs=list(out_specs), out_shape=list(out_shapes),
        scratch_shapes=scratch,
        compiler_params=_params(("parallel", "parallel", "arbitrary"), nbytes),
    )(a, b, *extras)
    return res if hook is None else (res, hook_res)


def _mm_specs(mode, tm, tn, tk, b_block=None, b_map=None):
    if mode == "tn":
        a_spec = pl.BlockSpec((tk, tm), lambda i, j, k: (k, i))
    else:
        a_spec = pl.BlockSpec((tm, tk), lambda i, j, k: (i, k))
    if b_block is not None:
        b_spec = pl.BlockSpec(b_block, b_map)
    elif mode == "nt":
        b_spec = pl.BlockSpec((tn, tk), lambda i, j, k: (j, k))
    else:
        b_spec = pl.BlockSpec((tk, tn), lambda i, j, k: (k, j))
    return a_spec, b_spec


def _project_by_class(name, h, w_t, bias, row_off, width, dilations, hook=None):
    s, d = h.shape
    tm = _tile(s, 1024)
    tn = 512 if width % 512 == 0 and row_off % 512 == 0 else _tile(width, 256)
    off = row_off // tn
    assert row_off % tn == 0 and tn % 128 == 0
    n_out = len(dilations)

    def body(h_ref, w_ref, b_ref, *rest):
        outs, perm_ref = rest[:n_out], rest[n_out]
        acc = lax.dot_general(h_ref[...], w_ref[...], (((1,), (1,)), ((), ())), preferred_element_type=F32)
        acc = acc + b_ref[...]
        for j in range(tn // 128):
            cols = slice(j * 128, (j + 1) * 128)
            for o_ref, dil in zip(outs, dilations):
                _to_classes(o_ref, cols, acc[:, cols], perm_ref, dil)

    blocks = tm * d * 2 + tn * d * 2 + 3 * tm * tn * 2 + tm * 128 * 4
    res, hook_res = _call(
        body, hook, name=name, grid=(s // tm, width // tn),
        in_specs=[pl.BlockSpec((tm, d), lambda i, j: (i, 0)), pl.BlockSpec((tn, d), lambda i, j: (j + off, 0)),
                  pl.BlockSpec((1, tn), lambda i, j: (0, j + off))],
        out_specs=[pl.BlockSpec((dil, tm // dil, tn), lambda i, j: (0, i, j)) for dil in dilations],
        out_shape=[_class_shape(dil, s, width, BF16) for dil in dilations],
        scratch_shapes=[pltpu.VMEM((tm, 128), F32)],
        compiler_params=_params(("parallel", "parallel"), blocks),
    )(h, w_t, bias)
    return res if hook is None else (res, hook_res)


def _tile(n, want):
    if n <= want:
        return n
    t = (want // 128) * 128
    while t > 128 and n % t:
        t -= 128
    assert n % t == 0, (n, want)
    return t


def _row_tile(s):
    return 256 if s % 256 == 0 else s


def _norm_fwd(name, x, g, hook=None):
    s, d = x.shape
    tm = _row_tile(s)

    def body(x_ref, g_ref, h_ref, r_ref):
        xv = x_ref[...]
        r = lax.rsqrt(jnp.mean(xv * xv, axis=-1, keepdims=True) + EPS)
        h_ref[...] = ((xv * r) * g_ref[...]).astype(BF16)
        r_ref[...] = r

    row = pl.BlockSpec((tm, d), lambda i: (i, 0))
    res, hook_res = _call(
        body, hook, name=name, grid=(s // tm,),
        in_specs=[row, pl.BlockSpec((1, d), lambda i: (0, 0))],
        out_specs=[row, pl.BlockSpec((tm, 1), lambda i: (i, 0))],
        out_shape=[jax.ShapeDtypeStruct((s, d), BF16), jax.ShapeDtypeStruct((s, 1), F32)],
        compiler_params=_params(("parallel",), tm * d * 6),
    )(x, g)
    return res if hook is None else (res, hook_res)


def _norm_bwd(name, dh, x, r, g, dres, hook=None):
    s, d = x.shape
    tm = _row_tile(s)

    def body(dh_ref, x_ref, r_ref, g_ref, dres_ref, dx_ref, dxb_ref, dg_ref):
        rv = r_ref[...]
        xn = x_ref[...] * rv
        dhv = dh_ref[...]
        dxn = dhv * g_ref[...]
        dx = dres_ref[...] + rv * (dxn - xn * jnp.mean(dxn * xn, axis=-1, keepdims=True))
        dx_ref[...] = dx
        dxb_ref[...] = dx.astype(BF16)
        part = jnp.sum(dhv * xn, axis=0, keepdims=True)

        @pl.when(pl.program_id(0) == 0)
        def _():
            dg_ref[...] = part

        @pl.when(pl.program_id(0) > 0)
        def _():
            dg_ref[...] += part

    row = pl.BlockSpec((tm, d), lambda i: (i, 0))
    vec = pl.BlockSpec((1, d), lambda i: (0, 0))
    res, hook_res = _call(
        body, hook, name=name, grid=(s // tm,),
        in_specs=[row, row, pl.BlockSpec((tm, 1), lambda i: (i, 0)), vec, row],
        out_specs=[row, row, vec],
        out_shape=[jax.ShapeDtypeStruct((s, d), F32), jax.ShapeDtypeStruct((s, d), BF16),
                   jax.ShapeDtypeStruct((1, d), F32)],
        compiler_params=_params(("arbitrary",), tm * d * 18),
    )(dh, x, r, g, dres)
    return res if hook is None else (res, hook_res)


def _loss_head(x3, target, g):
    s, d = x3.shape
    tm = _row_tile(s)

    def body(x_ref, t_ref, g_ref, dx_ref, dxb_ref, loss_ref, dg_ref):
        xv = x_ref[...]
        gv = g_ref[...]
        r = lax.rsqrt(jnp.mean(xv * xv, axis=-1, keepdims=True) + EPS)
        xn = xv * r
        err = xn * gv - t_ref[...]
        loss = 0.5 * jnp.sum(jnp.mean(err * err, axis=-1, keepdims=True), axis=0, keepdims=True)
        dy = err / d
        dxn = dy * gv
        dx = r * (dxn - xn * jnp.mean(dxn * xn, axis=-1, keepdims=True))
        dx_ref[...] = dx
        dxb_ref[...] = dx.astype(BF16)
        dg = jnp.sum(dy * xn, axis=0, keepdims=True)
        loss_row = jnp.broadcast_to(loss, (1, 128))

        @pl.when(pl.program_id(0) == 0)
        def _():
            dg_ref[...] = dg
            loss_ref[...] = loss_row

        @pl.when(pl.program_id(0) > 0)
        def _():
            dg_ref[...] += dg
            loss_ref[...] += loss_row

    row = pl.BlockSpec((tm, d), lambda i: (i, 0))
    vec = pl.BlockSpec((1, d), lambda i: (0, 0))
    return pl.pallas_call(
        body, name="loss_head", grid=(s // tm,),
        in_specs=[row, row, vec],
        out_specs=[row, row, pl.BlockSpec((1, 128), lambda i: (0, 0)), vec],
        out_shape=[jax.ShapeDtypeStruct((s, d), F32), jax.ShapeDtypeStruct((s, d), BF16),
                   jax.ShapeDtypeStruct((1, 128), F32), jax.ShapeDtypeStruct((1, d), F32)],
        compiler_params=_params(("arbitrary",), tm * d * 14),
    )(x3, target, g)


def _low_lanes(rows):
    return lax.broadcasted_iota(jnp.int32, (rows, PAIR), 1) < HEAD_DIM


def _to_classes(dst_ref, cols, value, perm_ref, dil):
    rows = value.shape[0]
    if dil == 1:
        dst_ref[0, :, cols] = value.astype(dst_ref.dtype)
        return
    perm_ref[...] = value
    for r in range(dil):
        dst_ref[r, :, cols] = perm_ref[pl.ds(r, rows // dil, stride=dil), :].astype(dst_ref.dtype)


def _from_classes(src_ref, cols, perm_ref, dil):
    if dil == 1:
        return src_ref[0, :, cols]
    rows = perm_ref.shape[0]
    for r in range(dil):
        perm_ref[pl.ds(r, rows // dil, stride=dil), :] = src_ref[r, :, cols]
    return perm_ref[...]


def _class_spec(dil, tm, width):
    return pl.BlockSpec((dil, tm // dil, width), lambda i: (0, i, 0))


def _class_shape(dil, s, width, dtype):
    return jax.ShapeDtypeStruct((dil, s // dil, width), dtype)


DILATIONS = tuple(d for _, d in DILATED_BRANCHES)


def _mix_fwd(oa, obs, lses, ga, gb):
    s, qa = oa.shape
    qb = obs[0].shape[2]
    tm = _row_tile(s)
    all_lanes = slice(0, 128)

    def body(oa_ref, o1_ref, o2_ref, o3_ref, l1_ref, l2_ref, l3_ref, ga_ref, gb_ref,
             mix_ref, ob_ref, t1_ref, t2_ref, t3_ref, ra_ref, rb_ref, perm_ref):
        oav = oa_ref[...]
        ra = lax.rsqrt(jnp.mean(oav * oav, axis=-1, keepdims=True) + EPS)
        ra_ref[...] = ra
        mix_ref[:, 0:qa] = ((oav * ra) * ga_ref[...]).astype(BF16)
        l1, l2, l3 = [_from_classes(l_ref, all_lanes, perm_ref, dil)
                      for l_ref, dil in zip((l1_ref, l2_ref, l3_ref), DILATIONS)]
        mx = jnp.maximum(jnp.maximum(l1, l2), l3)
        e1, e2, e3 = jnp.exp(l1 - mx), jnp.exp(l2 - mx), jnp.exp(l3 - mx)
        tot = e1 + e2 + e3
        lse = mx + jnp.log(tot)
        for t_ref, dil in zip((t1_ref, t2_ref, t3_ref), DILATIONS):
            _to_classes(t_ref, all_lanes, lse, perm_ref, dil)
        ws = (e1 / tot, e2 / tot, e3 / tot)
        low = _low_lanes(tm)
        ssq = jnp.zeros((tm, 1), F32)
        for i in range(qb // PAIR):
            sl = slice(i * PAIR, (i + 1) * PAIR)
            acc = jnp.zeros((tm, PAIR), F32)
            for w, o_ref, dil in zip(ws, (o1_ref, o2_ref, o3_ref), DILATIONS):
                wexp = jnp.where(low, w[:, 2 * i:2 * i + 1], w[:, 2 * i + 1:2 * i + 2])
                acc = acc + wexp * _from_classes(o_ref, sl, perm_ref, dil)
            ob_ref[:, sl] = acc
            ssq = ssq + jnp.sum(acc * acc, axis=-1, keepdims=True)
        rb = lax.rsqrt(ssq / qb + EPS)
        rb_ref[...] = rb
        mix_ref[:, qa:qa + qb] = ((ob_ref[...] * rb) * gb_ref[...]).astype(BF16)

    def row(w):
        return pl.BlockSpec((tm, w), lambda i: (i, 0))

    def vec(w):
        return pl.BlockSpec((1, w), lambda i: (0, 0))

    return pl.pallas_call(
        body, name="mix_fwd", grid=(s // tm,),
        in_specs=([row(qa)] + [_class_spec(d, tm, qb) for d in DILATIONS]
                  + [_class_spec(d, tm, 128) for d in DILATIONS] + [vec(qa), vec(qb)]),
        out_specs=([row(qa + qb), row(qb)] + [_class_spec(d, tm, 128) for d in DILATIONS] + [row(1), row(1)]),
        out_shape=([jax.ShapeDtypeStruct((s, qa + qb), BF16), jax.ShapeDtypeStruct((s, qb), F32)]
                   + [_class_shape(d, s, 128, F32) for d in DILATIONS]
                   + [jax.ShapeDtypeStruct((s, 1), F32), jax.ShapeDtypeStruct((s, 1), F32)]),
        scratch_shapes=[pltpu.VMEM((tm, 128), F32)],
        compiler_params=_params(("parallel",), tm * (qa + 4 * qb) * 4 + tm * (qa + qb) * 2 + tm * 4096),
    )(oa, *obs, *lses, ga, gb)


def _head_rowsums(prod, rows):
    low = _low_lanes(rows)
    lane = lax.broadcasted_iota(jnp.int32, (rows, 128), 1)
    out = jnp.zeros((rows, 128), F32)
    for i in range(prod.shape[1] // PAIR):
        tile = prod[:, i * PAIR:(i + 1) * PAIR]
        lo = jnp.sum(jnp.where(low, tile, 0.0), axis=-1, keepdims=True)
        hi = jnp.sum(jnp.where(low, 0.0, tile), axis=-1, keepdims=True)
        out = jnp.where(lane == 2 * i, lo, out)
        out = jnp.where(lane == 2 * i + 1, hi, out)
    return out


def _mix_bwd(dmix, oa, ob, ra, rb, ga, gb, hook=None):
    s, qa = oa.shape
    qb = ob.shape[1]
    tm = _row_tile(s)

    def one(dy, o, r, g):
        xn = o * r
        dxn = dy * g
        do = r * (dxn - xn * jnp.mean(dxn * xn, axis=-1, keepdims=True))
        return do, jnp.sum(dy * xn, axis=0, keepdims=True), _head_rowsums(do * o, tm)

    def body(dmix_ref, oa_ref, ob_ref, ra_ref, rb_ref, ga_ref, gb_ref,
             doa_ref, dob1_ref, dob2_ref, dob3_ref, dla_ref, dlb1_ref, dlb2_ref, dlb3_ref,
             dga_ref, dgb_ref, perm_ref):
        doa, dga, dla = one(dmix_ref[:, 0:qa], oa_ref[...], ra_ref[...], ga_ref[...])
        dob, dgb, dlb = one(dmix_ref[:, qa:qa + qb], ob_ref[...], rb_ref[...], gb_ref[...])
        doa_ref[...] = doa.astype(BF16)
        dla_ref[...] = dla
        for dob_ref, dlb_ref, dil in zip((dob1_ref, dob2_ref, dob3_ref), (dlb1_ref, dlb2_ref, dlb3_ref),
                                         DILATIONS):
            _to_classes(dlb_ref, slice(0, 128), dlb, perm_ref, dil)
            for i in range(qb // PAIR):
                sl = slice(i * PAIR, (i + 1) * PAIR)
                _to_classes(dob_ref, sl, dob[:, sl], perm_ref, dil)

        @pl.when(pl.program_id(0) == 0)
        def _():
            dga_ref[...] = dga
            dgb_ref[...] = dgb

        @pl.when(pl.program_id(0) > 0)
        def _():
            dga_ref[...] += dga
            dgb_ref[...] += dgb

    def row(w):
        return pl.BlockSpec((tm, w), lambda i: (i, 0))

    def vec(w):
        return pl.BlockSpec((1, w), lambda i: (0, 0))

    res, hook_res = _call(
        body, hook, name="mix_bwd", grid=(s // tm,),
        in_specs=[row(qa + qb), row(qa), row(qb), row(1), row(1), vec(qa), vec(qb)],
        out_specs=([row(qa)] + [_class_spec(d, tm, qb) for d in DILATIONS] + [row(128)]
                   + [_class_spec(d, tm, 128) for d in DILATIONS] + [vec(qa), vec(qb)]),
        out_shape=([jax.ShapeDtypeStruct((s, qa), BF16)] + [_class_shape(d, s, qb, BF16) for d in DILATIONS]
                   + [jax.ShapeDtypeStruct((s, 128), F32)] + [_class_shape(d, s, 128, F32) for d in DILATIONS]
                   + [jax.ShapeDtypeStruct((1, qa), F32), jax.ShapeDtypeStruct((1, qb), F32)]),
        scratch_shapes=[pltpu.VMEM((tm, 128), F32)],
        compiler_params=_params(("arbitrary",), tm * (qa + qb) * 16),
    )(dmix, oa, ob, ra, rb, ga, gb)
    return res if hook is None else (res, hook_res)


def _assemble_dproj(dqa, dkva, dqs, dks, dvs):
    s, qa = dqa.shape
    kva = dkva.shape[1]
    qb = dqs[0].shape[2]
    width = qa + kva + 3 * qb
    tm = _row_tile(s)

    def body(dqa_ref, dkva_ref, q1, q2, q3, k1, k2, k3, v1, v2, v3, dp_ref, db_ref, perm_ref):
        first = pl.program_id(0) == 0

        def emit(off, val):
            dp_ref[:, off:off + PAIR] = val.astype(BF16)
            col = jnp.sum(val, axis=0, keepdims=True)

            @pl.when(first)
            def _():
                db_ref[:, off:off + PAIR] = col

            @pl.when(jnp.logical_not(first))
            def _():
                db_ref[:, off:off + PAIR] += col

        for i in range(qa // PAIR):
            emit(i * PAIR, dqa_ref[:, i * PAIR:(i + 1) * PAIR])
        for i in range(kva // PAIR):
            emit(qa + i * PAIR, dkva_ref[:, i * PAIR:(i + 1) * PAIR])
        for j, branch_refs in enumerate(((q1, q2, q3), (k1, k2, k3), (v1, v2, v3))):
            for i in range(qb // PAIR):
                sl = slice(i * PAIR, (i + 1) * PAIR)
                total = None
                for ref, dil in zip(branch_refs, DILATIONS):
                    val = _from_classes(ref, sl, perm_ref, dil)
                    total = val if total is None else total + val
                emit(qa + kva + j * qb + i * PAIR, total)

    def row(w):
        return pl.BlockSpec((tm, w), lambda i: (i, 0))

    return pl.pallas_call(
        body, name="assemble_dproj", grid=(s // tm,),
        in_specs=[row(qa), row(kva)] + [_class_spec(d, tm, qb) for d in DILATIONS] * 3,
        out_specs=[row(width), pl.BlockSpec((1, width), lambda i: (0, 0))],
        out_shape=[jax.ShapeDtypeStruct((s, width), BF16), jax.ShapeDtypeStruct((1, width), F32)],
        scratch_shapes=[pltpu.VMEM((tm, 128), F32)],
        compiler_params=_params(("arbitrary",), tm * (qa + kva + 9 * qb) * 4 + tm * width * 2),
    )(dqa, dkva, *dqs, *dks, *dvs)


def _fill_bias(bias_ref, n_pairs, max_steps, dil, slopes):
    qi = lax.broadcasted_iota(jnp.int32, (BLOCK, 2 * BLOCK), 0)
    kj = lax.broadcasted_iota(jnp.int32, (BLOCK, 2 * BLOCK), 1)
    steps = qi + BLOCK - kj
    dist = (steps * dil).astype(F32)
    band = (steps >= 0) & (steps <= max_steps)
    for first in (0, 1):
        valid = band & (kj >= BLOCK) if first else band
        for i in range(n_pairs):
            bias_ref[first, i] = jnp.concatenate(
                [jnp.where(valid, -(slopes[2 * i + half] * dist), NEG_INF) for half in (0, 1)], axis=0)


def _bias_shape(n_pairs):
    return pltpu.VMEM((2, n_pairs, 2 * BLOCK, 2 * BLOCK), F32)


def _stack_heads(tile, low):
    zero = jnp.zeros_like(tile)
    return jnp.concatenate([jnp.where(low, tile, zero), jnp.where(low, zero, tile)], axis=0)


def _unstack_heads(stacked, low):
    return jnp.where(low, stacked[0:BLOCK], stacked[BLOCK:2 * BLOCK])


def _head_columns(ref, i):
    return jnp.concatenate([ref[:, 2 * i:2 * i + 1], ref[:, 2 * i + 1:2 * i + 2]], axis=0)


def _sink_column(sink_ref, i):
    return jnp.concatenate([jnp.full((BLOCK, 1), sink_ref[2 * i], F32),
                            jnp.full((BLOCK, 1), sink_ref[2 * i + 1], F32)], axis=0)


def _swap_halves(t):
    return pltpu.roll(t, HEAD_DIM, 1)


def _dup_group(t_bf16, group):
    t = t_bf16.astype(F32)
    low = lax.broadcasted_iota(jnp.int32, t.shape, 1) < HEAD_DIM
    keep = low if group == 0 else jnp.logical_not(low)
    return jnp.where(keep, t, _swap_halves(t)).astype(BF16)


def _attn_fwd(name, q, kv, *, dil, max_steps, slopes, sinks=None, hook=None):
    grouped = sinks is not None
    _, length, w = q.shape
    n_pairs = w // PAIR
    nb = length // BLOCK
    heads_per_group = 2 * n_pairs // N_KV_GROUPS

    def body(*refs):
        if grouped:
            sink_ref, q_ref, kvp_ref, kvc_ref, o_ref, lse_ref, bias_ref = refs
        else:
            q_ref, kp_ref, kc_ref, vp_ref, vc_ref, o_ref, lse_ref, bias_ref = refs
        n = pl.program_id(1)

        @pl.when((pl.program_id(0) == 0) & (n == 0))
        def _():
            _fill_bias(bias_ref, n_pairs, max_steps, dil, slopes)

        first = (n == 0).astype(jnp.int32)
        low = _low_lanes(BLOCK)
        lane = lax.broadcasted_iota(jnp.int32, (BLOCK, 128), 1)
        lse_acc = jnp.zeros((BLOCK, 128), F32)
        if grouped:
            kv_all = jnp.concatenate([kvp_ref[...], kvc_ref[...]], axis=0)
            k_dup = [_dup_group(kv_all[:, 0:PAIR], g) for g in range(N_KV_GROUPS)]
            v_dup = [_dup_group(kv_all[:, PAIR:2 * PAIR], g) for g in range(N_KV_GROUPS)]
        for i in range(n_pairs):
            sl = slice(i * PAIR, (i + 1) * PAIR)
            qs = _stack_heads(q_ref[:, sl] * ATT_SCALE, low)
            if grouped:
                kk, vv = k_dup[2 * i // heads_per_group], v_dup[2 * i // heads_per_group]
            else:
                kk = jnp.concatenate([kp_ref[:, sl], kc_ref[:, sl]], axis=0)
                vv = jnp.concatenate([vp_ref[:, sl], vc_ref[:, sl]], axis=0)
            sc = lax.dot_general(qs, kk, (((1,), (1,)), ((), ())), preferred_element_type=F32)
            sc = sc + bias_ref[first, i]
            m = jnp.max(sc, axis=-1, keepdims=True)
            if grouped:
                sink = _sink_column(sink_ref, i)
                m = jnp.maximum(m, sink)
            p = jnp.exp(sc - m)
            den = jnp.sum(p, axis=-1, keepdims=True)
            if grouped:
                den = den + jnp.exp(sink - m)
            o = jnp.dot(p.astype(BF16), vv, preferred_element_type=F32) / den
            o_ref[:, sl] = _unstack_heads(o, low)
            lse = m + jnp.log(den)
            lse_acc = jnp.where(lane == 2 * i, lse[0:BLOCK], lse_acc)
            lse_acc = jnp.where(lane == 2 * i + 1, lse[BLOCK:2 * BLOCK], lse_acc)
        lse_ref[...] = lse_acc

    def cur(width):
        return pl.BlockSpec((None, BLOCK, width), lambda r, n: (r, n, 0))

    def prev(width):
        return pl.BlockSpec((None, BLOCK, width), lambda r, n: (r, jnp.maximum(n - 1, 0), 0))

    if grouped:
        kvw = kv.shape[2]
        operands = [sinks, q, kv, kv]
        in_specs = [SMEM_SPEC, cur(w), prev(kvw), cur(kvw)]
    else:
        operands = [q, kv[0], kv[0], kv[1], kv[1]]
        in_specs = [cur(w), prev(w), cur(w), prev(w), cur(w)]
    res, hook_res = _call(
        body, hook, name=name, grid=(dil, nb), in_specs=in_specs,
        out_specs=[cur(w), cur(128)],
        out_shape=[jax.ShapeDtypeStruct((dil, length, w), F32),
                   jax.ShapeDtypeStruct((dil, length, 128), F32)],
        scratch_shapes=[_bias_shape(n_pairs)],
        compiler_params=_params(("arbitrary", "arbitrary"), BLOCK * w * 16 + n_pairs * BLOCK * BLOCK * 16),
    )(*operands)
    return res if hook is None else (res, hook_res)


def _attn_bwd(name, q, kv, do, lse, delta, *, dil, max_steps, slopes, sinks=None, hook=None):
    grouped = sinks is not None
    _, length, w = q.shape
    n_pairs = w // PAIR
    nb = length // BLOCK
    heads_per_group = 2 * n_pairs // N_KV_GROUPS
    pairs_per_group = n_pairs // N_KV_GROUPS

    def body(*refs):
        if grouped:
            (sink_ref, q_ref, kvp_ref, kvc_ref, do_ref, lse_ref, dl_ref,
             dq_ref, dkv_ref, dsink_ref, acc_ref, bias_ref) = refs
        else:
            (q_ref, kp_ref, kc_ref, vp_ref, vc_ref, do_ref, lse_ref, dl_ref,
             dq_ref, dk_ref, dv_ref, acck_ref, accv_ref, bias_ref) = refs
        n = pl.program_id(1)

        @pl.when((pl.program_id(0) == 0) & (n == 0))
        def _():
            _fill_bias(bias_ref, n_pairs, max_steps, dil, slopes)

        @pl.when(n == 0)
        def _():
            if grouped:
                acc_ref[...] = jnp.zeros_like(acc_ref)

                @pl.when(pl.program_id(0) == 0)
                def _():
                    dsink_ref[...] = jnp.zeros_like(dsink_ref)
            else:
                acck_ref[...] = jnp.zeros_like(acck_ref)
                accv_ref[...] = jnp.zeros_like(accv_ref)

        @pl.when(n == nb)
        def _():
            if grouped:
                dkv_ref[...] = acc_ref[...]
            else:
                dk_ref[...] = acck_ref[...]
                dv_ref[...] = accv_ref[...]

        @pl.when(n < nb)
        def _():
            first = (n == 0).astype(jnp.int32)
            low = _low_lanes(BLOCK)
            low_kv = _low_lanes(2 * BLOCK)
            lane1 = lax.broadcasted_iota(jnp.int32, (1, 128), 1)
            if grouped:
                kv_all = jnp.concatenate([kvp_ref[...], kvc_ref[...]], axis=0)
                k_dup = [_dup_group(kv_all[:, 0:PAIR], g) for g in range(N_KV_GROUPS)]
                v_dup = [_dup_group(kv_all[:, PAIR:2 * PAIR], g) for g in range(N_KV_GROUPS)]
                dk_grp = [jnp.zeros((2 * BLOCK, PAIR), F32) for _ in range(N_KV_GROUPS)]
                dv_grp = [jnp.zeros((2 * BLOCK, PAIR), F32) for _ in range(N_KV_GROUPS)]
                dsink = jnp.zeros((1, 128), F32)
            for i in range(n_pairs):
                sl = slice(i * PAIR, (i + 1) * PAIR)
                qs = _stack_heads(q_ref[:, sl] * ATT_SCALE, low)
                dos = _stack_heads(do_ref[:, sl], low)
                if grouped:
                    grp = 2 * i // heads_per_group
                    kk, vv = k_dup[grp], v_dup[grp]
                else:
                    kk = jnp.concatenate([kp_ref[:, sl], kc_ref[:, sl]], axis=0)
                    vv = jnp.concatenate([vp_ref[:, sl], vc_ref[:, sl]], axis=0)
                lse_col = _head_columns(lse_ref, i)
                dl_col = _head_columns(dl_ref, i)
                sc = lax.dot_general(qs, kk, (((1,), (1,)), ((), ())), preferred_element_type=F32)
                p = jnp.exp(sc + bias_ref[first, i] - lse_col)
                dp = lax.dot_general(dos, vv, (((1,), (1,)), ((), ())), preferred_element_type=F32)
                ds = (p * (dp - dl_col)).astype(BF16)
                dq = jnp.dot(ds, kk, preferred_element_type=F32)
                dkk = lax.dot_general(ds, qs, (((0,), (0,)), ((), ())), preferred_element_type=F32)
                dvv = lax.dot_general(p.astype(BF16), dos, (((0,), (0,)), ((), ())),
                                      preferred_element_type=F32)
                if grouped:
                    lost = jnp.exp(_sink_column(sink_ref, i) - lse_col) * dl_col
                    for half in (0, 1):
                        contrib = -jnp.sum(lost[half * BLOCK:(half + 1) * BLOCK], axis=0, keepdims=True)
                        dsink = jnp.where(lane1 == 2 * i + half, dsink + contrib, dsink)
                dq_ref[:, sl] = _unstack_heads(dq, low) * ATT_SCALE
                if grouped:
                    dk_grp[grp] = dk_grp[grp] + dkk
                    dv_grp[grp] = dv_grp[grp] + dvv
                else:
                    dk_ref[:, sl] = acck_ref[:, sl] + dkk[0:BLOCK]
                    acck_ref[:, sl] = dkk[BLOCK:2 * BLOCK]
                    dv_ref[:, sl] = accv_ref[:, sl] + dvv[0:BLOCK]
                    accv_ref[:, sl] = dvv[BLOCK:2 * BLOCK]
            if grouped:
                folded = [t + _swap_halves(t) for t in dk_grp + dv_grp]
                dk_tile = jnp.where(low_kv, folded[0], folded[1])
                dv_tile = jnp.where(low_kv, folded[2], folded[3])
                part = jnp.concatenate([dk_tile, dv_tile], axis=1)
                dkv_ref[...] = acc_ref[...] + part[0:BLOCK]
                acc_ref[...] = part[BLOCK:2 * BLOCK]
                dsink_ref[...] += dsink

    last = nb - 1

    def cur(width):
        return pl.BlockSpec((None, BLOCK, width), lambda r, n: (r, jnp.minimum(n, last), 0))

    def prev(width):
        return pl.BlockSpec((None, BLOCK, width),
                            lambda r, n: (r, jnp.maximum(jnp.minimum(n, last) - 1, 0), 0))

    def done(width):
        return pl.BlockSpec((None, BLOCK, width), lambda r, n: (r, jnp.maximum(n - 1, 0), 0))

    if grouped:
        assert pairs_per_group * N_KV_GROUPS == n_pairs and heads_per_group % 2 == 0
        kvw = kv.shape[2]
        operands = [sinks, q, kv, kv, do, lse, delta]
        in_specs = [SMEM_SPEC, cur(w), prev(kvw), cur(kvw), cur(w), cur(128), cur(128)]
        out_specs = [cur(w), done(kvw), pl.BlockSpec((1, 128), lambda r, n: (0, 0))]
        out_shape = [jax.ShapeDtypeStruct((dil, length, w), F32), jax.ShapeDtypeStruct((dil, length, kvw), F32),
                     jax.ShapeDtypeStruct((1, 128), F32)]
        scratch = [pltpu.VMEM((BLOCK, kvw), F32), _bias_shape(n_pairs)]
    else:
        operands = [q, kv[0], kv[0], kv[1], kv[1], do, lse, delta]
        in_specs = [cur(w), prev(w), cur(w), prev(w), cur(w), cur(w), cur(128), cur(128)]
        out_specs = [cur(w), done(w), done(w)]
        out_shape = [jax.ShapeDtypeStruct((dil, length, w), F32)] * 3
        scratch = [pltpu.VMEM((BLOCK, w), F32), pltpu.VMEM((BLOCK, w), F32), _bias_shape(n_pairs)]
    res, hook_res = _call(
        body, hook, name=name, grid=(dil, nb + 1), in_specs=in_specs, out_specs=out_specs,
        out_shape=out_shape, scratch_shapes=scratch,
        compiler_params=_params(("arbitrary", "arbitrary"), BLOCK * w * 32 + n_pairs * BLOCK * BLOCK * 16),
    )(*operands)
    return res if hook is None else (res, hook_res)


def _adamw(name, w, g, m, v):
    rows, cols = w.shape
    tm = 256 if rows % 256 == 0 else rows

    def body(w_ref, g_ref, m_ref, v_ref, d_ref, nm_ref, nv_ref):
        gv = g_ref[...]
        mn = ADAM_B1 * m_ref[...] + (1.0 - ADAM_B1) * gv
        vn = ADAM_B2 * v_ref[...] + (1.0 - ADAM_B2) * (gv * gv)
        m_hat = mn / (1.0 - ADAM_B1 ** ADAM_STEP)
        v_hat = vn / (1.0 - ADAM_B2 ** ADAM_STEP)
        d_ref[...] = -ADAM_LR * (m_hat / (jnp.sqrt(v_hat) + ADAM_EPS) + ADAM_WD * w_ref[...])
        nm_ref[...] = mn
        nv_ref[...] = vn

    spec = pl.BlockSpec((tm, cols), lambda i: (i, 0))
    return pl.pallas_call(
        body, name=name, grid=(rows // tm,), in_specs=[spec] * 4, out_specs=[spec] * 3,
        out_shape=[jax.ShapeDtypeStruct(w.shape, F32)] * 3,
        compiler_params=_params(("parallel",), tm * cols * 28),
    )(w, g, m, v)


def _mesh_position():
    return lax.axis_index("x"), lax.axis_index("y"), lax.axis_index("c")


def _other_chips(x, y):
    return [(1 - x, y), (x, 1 - y), (1 - x, 1 - y)]


def _gather_hook(shard, gathered, lo, hi):
    rows, cols = shard.shape
    half, n = rows // 2, hi - lo
    assert lo % 16 == 0 and n % 16 == 0 and half % 16 == 0
    first = gathered is None

    def region(out, owner_chip, which_half):
        return out.at[pl.ds(pl.multiple_of(owner_chip * rows + which_half * half + lo, 16), n)]

    def parts(ops, outs, sems):
        x, y, c = _mesh_position()
        return ops[0], outs[0], sems, x, y, c, 2 * x + y, (x, y, 1 - c), _other_chips(x, y)

    def local_copy(src, out, chip, sem):
        return pltpu.make_async_copy(src, out.at[pl.ds(pl.multiple_of(chip * rows, 16), rows)], sem)

    def start(ops, outs, sems):
        src, out, (send, recv, fsend, frecv, local), x, y, c, chip, sibling, others = parts(ops, outs, sems)
        if first:
            local_copy(src, out, chip, local.at[0]).start()
        mine = src.at[pl.ds(pl.multiple_of(c * half + lo, 16), n)]
        for k, (px, py) in enumerate(others):
            _remote(mine, region(out, chip, c), send.at[k], recv.at[k], (px, py, c)).start()

    def mid(ops, outs, sems):
        src, out, (send, recv, fsend, frecv, local), x, y, c, chip, sibling, others = parts(ops, outs, sems)
        for k, (px, py) in enumerate(others):
            landed = region(out, 2 * px + py, c)
            _remote(landed, landed, send.at[k], recv.at[k], (px, py, c)).wait_recv()
            _remote(landed, landed, fsend.at[k], frecv.at[k], sibling).start()

    def finish(ops, outs, sems):
        src, out, (send, recv, fsend, frecv, local), x, y, c, chip, sibling, others = parts(ops, outs, sems)
        mine = src.at[pl.ds(pl.multiple_of(c * half + lo, 16), n)]
        for k, (px, py) in enumerate(others):
            passed = region(out, 2 * px + py, 1 - c)
            _remote(passed, passed, fsend.at[k], frecv.at[k], sibling).wait_recv()
        for k, (px, py) in enumerate(others):
            landed = region(out, 2 * px + py, c)
            _remote(landed, landed, fsend.at[k], frecv.at[k], sibling).wait_send()
            _remote(mine, region(out, chip, c), send.at[k], recv.at[k], (px, py, c)).wait_send()
        if first:
            local_copy(src, out, chip, local.at[0]).wait()

    sems = [pltpu.SemaphoreType.DMA((3,))] * 4 + [pltpu.SemaphoreType.DMA((1,))]
    out_shape = [jax.ShapeDtypeStruct((N_CHIPS * rows, cols), shard.dtype)]
    if first:
        return _Hook([shard], out_shape, sems, start, finish, mid)
    return _Hook([shard, gathered], out_shape, sems, start, finish, mid, aliases={1: 0})


def _exchange_hook(grad):
    rows, cols = grad.shape[0] // N_CHIPS, grad.shape[1]
    half = rows // 2
    assert half % 16 == 0

    def copies(ops, outs, sems):
        x, y, c = _mesh_position()
        send, recv = sems
        return [_remote(ops[0].at[pl.ds(pl.multiple_of(k * rows + (1 - c) * half, 16), half)], outs[0].at[k],
                        send.at[k], recv.at[k], (x, y, 1 - c)) for k in range(N_CHIPS)]

    def start(ops, outs, sems):
        for cp in copies(ops, outs, sems):
            cp.start()

    def finish(ops, outs, sems):
        for cp in copies(ops, outs, sems):
            cp.wait_recv()
            cp.wait_send()

    return _Hook([grad], [jax.ShapeDtypeStruct((N_CHIPS, half, cols), grad.dtype)],
                 [pltpu.SemaphoreType.DMA((N_CHIPS,))] * 2, start, finish)


def _scatter_hook(chip_sum):
    _, half, cols = chip_sum.shape

    def copies(ops, outs, sems):
        x, y, c = _mesh_position()
        send, recv = sems
        return [_remote(ops[0].at[2 * px + py], outs[0].at[k], send.at[k], recv.at[k], (px, py, c))
                for k, (px, py) in enumerate(_other_chips(x, y))]

    def start(ops, outs, sems):
        for cp in copies(ops, outs, sems):
            cp.start()

    def finish(ops, outs, sems):
        for cp in copies(ops, outs, sems):
            cp.wait_recv()
            cp.wait_send()

    return _Hook([chip_sum], [jax.ShapeDtypeStruct((3, half, cols), chip_sum.dtype)],
                 [pltpu.SemaphoreType.DMA((3,))] * 2, start, finish)


def _sum_tile(half):
    return 256 if half % 256 == 0 else half


def _chip_add(name, grad, from_sibling, core):
    n_chips, half, cols = from_sibling.shape
    rows = 2 * half
    tr = _sum_tile(half)

    def body(core_ref, g_ref, s_ref, o_ref):
        o_ref[...] = (g_ref[...].astype(F32) + s_ref[...].astype(F32)).astype(o_ref.dtype)

    tile = pl.BlockSpec((None, tr, cols), lambda k, i, core_ref: (k, i, 0))
    return pl.pallas_call(
        body, name=name,
        grid_spec=pltpu.PrefetchScalarGridSpec(
            num_scalar_prefetch=1, grid=(n_chips, half // tr),
            in_specs=[pl.BlockSpec((tr, cols), lambda k, i, core_ref:
                                   (k * (rows // tr) + core_ref[0] * (half // tr) + i, 0)), tile],
            out_specs=tile),
        out_shape=jax.ShapeDtypeStruct(from_sibling.shape, from_sibling.dtype),
        compiler_params=_params(("parallel", "parallel"), 3 * tr * cols * 4),
    )(core, grad, from_sibling)


def _final_add(name, chip_sum, from_chips, chip):
    _, half, cols = chip_sum.shape
    tr = _sum_tile(half)

    def body(chip_ref, own_ref, others_ref, o_ref):
        total = own_ref[...].astype(F32)
        for k in range(3):
            total = total + others_ref[k].astype(F32)
        o_ref[...] = total

    return pl.pallas_call(
        body, name=name,
        grid_spec=pltpu.PrefetchScalarGridSpec(
            num_scalar_prefetch=1, grid=(half // tr,),
            in_specs=[pl.BlockSpec((None, tr, cols), lambda i, chip_ref: (chip_ref[0], i, 0)),
                      pl.BlockSpec((3, tr, cols), lambda i, chip_ref: (0, i, 0))],
            out_specs=pl.BlockSpec((tr, cols), lambda i, chip_ref: (i, 0))),
        out_shape=jax.ShapeDtypeStruct((half, cols), F32),
        compiler_params=_params(("parallel",), 6 * tr * cols * 4),
    )(chip, chip_sum, from_chips)


FINISH_CHUNK_ROWS = 256


def _finish_reduction(chip_sums, shares, small):
    n_w, n_s = len(chip_sums), len(shares)
    halves = [cs.shape[1] for cs in chip_sums]
    cols = shares[0].shape[1] if n_s else chip_sums[0].shape[2]
    wire = chip_sums[0].dtype if n_w else GRAD_WIRE_DTYPE
    rows_s = small.shape[0]
    ch = FINISH_CHUNK_ROWS

    def body(*refs):
        sums_in, small_ref, share_in = refs[:n_w], refs[n_w], refs[n_w + 1:n_w + 1 + n_s]
        o0 = n_w + 1 + n_s
        outs, small_out, share_out = refs[o0:o0 + n_w], refs[o0 + n_w], refs[o0 + n_w + 1:o0 + n_w + 1 + n_s]
        arrived = refs[o0 + n_w + 1 + n_s:o0 + 2 * n_w + 1 + n_s]
        (small_all, buf_in, buf_out, chip_send, chip_recv, fin_send, fin_recv, small_send, small_recv,
         share_send, share_recv, share_local, io_sem) = refs[o0 + 2 * n_w + 1 + n_s:]
        x, y, c = _mesh_position()
        chip = 2 * x + y
        me = 4 * x + 2 * y + c
        sibling = (x, y, 1 - c)
        others = _other_chips(x, y)
        pending, local = [], []

        for w in range(n_w):
            for k, (px, py) in enumerate(others):
                cp = _remote(sums_in[w].at[2 * px + py], arrived[w].at[k], chip_send.at[w, k],
                             chip_recv.at[w, k], (px, py, c))
                cp.start()
                pending.append(cp)
        small_all[me] = small_ref[...]
        for j in range(N_DEV - 1):
            peer = (me + 1 + j) % N_DEV
            cp = _remote(small_all.at[me], small_all.at[me], small_send.at[j], small_recv.at[j],
                         (peer // 4, (peer // 2) % 2, peer % 2))
            cp.start()
            pending.append(cp)

        def halves_of(out, rows):
            return [out.at[pl.ds(pl.multiple_of(which * rows, 16), rows)] for which in (c, 1 - c)]

        for i in range(n_s):
            mine, _ = halves_of(share_out[i], share_in[i].shape[0])
            cp = pltpu.make_async_copy(share_in[i], mine, share_local.at[i])
            cp.start()
            local.append(cp)
            cp = _remote(share_in[i], mine, share_send.at[i], share_recv.at[i], sibling)
            cp.start()
            pending.append(cp)

        def add_chunk(w, dst, start, size):
            total = None
            for src in [sums_in[w].at[chip]] + [arrived[w].at[k] for k in range(3)]:
                cp = pltpu.make_async_copy(src.at[pl.ds(start, size)], buf_in.at[pl.ds(0, size)], io_sem)
                cp.start()
                cp.wait()
                val = buf_in[pl.ds(0, size), :].astype(F32)
                total = val if total is None else total + val
            buf_out[pl.ds(0, size), :] = total
            cp = pltpu.make_async_copy(buf_out.at[pl.ds(0, size)], dst.at[pl.ds(start, size)], io_sem)
            cp.start()
            cp.wait()

        for w in range(n_w):
            for k, (px, py) in enumerate(others):
                _remote(sums_in[w].at[chip], arrived[w].at[k], chip_send.at[w, k], chip_recv.at[w, k],
                        (px, py, c)).wait_recv()
            mine, _ = halves_of(outs[w], halves[w])
            n_full = halves[w] // ch

            def loop_body(i, carry, w=w, mine=mine):
                add_chunk(w, mine, pl.multiple_of(i * ch, ch), ch)
                return carry

            lax.fori_loop(0, n_full, loop_body, 0)
            if halves[w] % ch:
                add_chunk(w, mine, n_full * ch, halves[w] - n_full * ch)
            cp = _remote(mine, mine, fin_send.at[w], fin_recv.at[w], sibling)
            cp.start()
            pending.append(cp)
        for w in range(n_w):
            _, theirs = halves_of(outs[w], halves[w])
            _remote(theirs, theirs, fin_send.at[w], fin_recv.at[w], sibling).wait_recv()
        for i in range(n_s):
            _, theirs = halves_of(share_out[i], share_in[i].shape[0])
            _remote(share_in[i], theirs, share_send.at[i], share_recv.at[i], sibling).wait_recv()

        for j in range(N_DEV - 1):
            peer = (me + N_DEV - 1 - j) % N_DEV
            _remote(small_all.at[peer], small_all.at[peer], small_send.at[j], small_recv.at[j],
                    sibling).wait_recv()
        total = small_all[0]
        for dev in range(1, N_DEV):
            total = total + small_all[dev]
        small_out[...] = total
        for cp in pending:
            cp.wait_send()
        for cp in local:
            cp.wait()

    def sems(n):
        return pltpu.SemaphoreType.DMA((max(n, 1),))

    out_shape = ([jax.ShapeDtypeStruct((2 * h, cols), F32) for h in halves]
                 + [jax.ShapeDtypeStruct((rows_s, 128), F32)]
                 + [jax.ShapeDtypeStruct((2 * sh.shape[0], cols), F32) for sh in shares]
                 + [jax.ShapeDtypeStruct((3, h, cols), wire) for h in halves])
    res = pl.pallas_call(
        body, name="finish_reduction",
        in_specs=[HBM_SPEC] * n_w + [VMEM_SPEC] + [HBM_SPEC] * n_s,
        out_specs=[HBM_SPEC] * n_w + [VMEM_SPEC] + [HBM_SPEC] * (n_s + n_w),
        out_shape=out_shape,
        scratch_shapes=[
            pltpu.VMEM((N_DEV, rows_s, 128), F32), pltpu.VMEM((ch, cols), wire), pltpu.VMEM((ch, cols), F32),
            pltpu.SemaphoreType.DMA((max(n_w, 1), 3)), pltpu.SemaphoreType.DMA((max(n_w, 1), 3)),
            sems(n_w), sems(n_w), sems(N_DEV - 1), sems(N_DEV - 1), sems(n_s), sems(n_s), sems(n_s),
            pltpu.SemaphoreType.DMA,
        ],
        compiler_params=pltpu.CompilerParams(vmem_limit_bytes=VMEM_LIMIT_MIN),
    )(*chip_sums, small, *shares)
    return res[:n_w], res[n_w + 1:n_w + 1 + n_s], res[n_w]


def _gather_weights(shards):
    n_w = len(shards)
    halves = [s.shape[0] // 2 for s in shards]

    def body(*refs):
        ins, outs = refs[:n_w], refs[n_w:2 * n_w]
        local_sems, send_sems, recv_sems, fsend_sems, frecv_sems = refs[2 * n_w:]
        x, y, c = _mesh_position()
        chip = 2 * x + y
        sibling = (x, y, 1 - c)
        others = _other_chips(x, y)

        def region(w, owner_chip, half):
            start = owner_chip * (2 * halves[w]) + half * halves[w]
            return outs[w].at[pl.ds(pl.multiple_of(start, 16), halves[w])]

        def remote(src, dst, ssem, rsem, dev):
            return pltpu.make_async_remote_copy(src_ref=src, dst_ref=dst, send_sem=ssem, recv_sem=rsem,
                                                device_id=dev, device_id_type=MESH)

        local = []
        for w in range(n_w):
            cp = pltpu.make_async_copy(ins[w], outs[w].at[pl.ds(pl.multiple_of(chip * 2 * halves[w], 16),
                                                                2 * halves[w])], local_sems.at[w])
            cp.start()
            local.append(cp)
        sends = []
        for w in range(n_w):
            mine = ins[w].at[pl.ds(pl.multiple_of(c * halves[w], 16), halves[w])]
            for k, (px, py) in enumerate(others):
                cp = remote(mine, region(w, chip, c), send_sems.at[w, k], recv_sems.at[w, k], (px, py, c))
                cp.start()
                sends.append(cp)
        for k, (px, py) in enumerate(others):
            for w in range(n_w):
                landed = region(w, 2 * px + py, c)
                remote(landed, landed, send_sems.at[w, k], recv_sems.at[w, k], (px, py, c)).wait_recv()
                cp = remote(landed, landed, fsend_sems.at[w, k], frecv_sems.at[w, k], sibling)
                cp.start()
                sends.append(cp)
        for k, (px, py) in enumerate(others):
            for w in range(n_w):
                passed = region(w, 2 * px + py, 1 - c)
                remote(passed, passed, fsend_sems.at[w, k], frecv_sems.at[w, k], sibling).wait_recv()
        for cp in sends:
            cp.wait_send()
        for cp in local:
            cp.wait()

    return pl.pallas_call(
        body, name="gather_weights",
        in_specs=[HBM_SPEC] * n_w, out_specs=[HBM_SPEC] * n_w,
        out_shape=[jax.ShapeDtypeStruct((N_CHIPS * s.shape[0], s.shape[1]), s.dtype) for s in shards],
        scratch_shapes=[pltpu.SemaphoreType.DMA((n_w,))] + [pltpu.SemaphoreType.DMA((n_w, 3))] * 4,
    )(*shards)


REDUCE_CHUNK_ROWS = 256


def _reduce_gradients(grads, shares, small):
    n_w = len(grads)
    rows = [g.shape[0] // N_CHIPS for g in grads]
    halves = [r // 2 for r in rows]
    cols = grads[0].shape[1]
    wire = grads[0].dtype
    ch = REDUCE_CHUNK_ROWS
    for h in halves:
        assert h % 16 == 0
    rows_s = small.shape[0]
    n_s = len(shares)

    def body(*refs):
        g_in = refs[:n_w]
        small_ref = refs[n_w]
        share_in = refs[n_w + 1:n_w + 1 + n_s]
        refs = refs[:n_w + 1] + refs[n_w + 1 + n_s:]
        outs = refs[n_w + 1:2 * n_w + 1]
        small_out = refs[2 * n_w + 1]
        from_sib = refs[2 * n_w + 2:3 * n_w + 2]
        chip_sum = refs[3 * n_w + 2:4 * n_w + 2]
        from_chips = refs[4 * n_w + 2:5 * n_w + 2]
        share_out = refs[5 * n_w + 2:5 * n_w + 2 + n_s]
        (small_all, buf_a, buf_b, buf_o, sib_send, sib_recv, chip_send, chip_recv,
         fin_send, fin_recv, small_send, small_recv, io_sem,
         share_send, share_recv, share_local) = refs[5 * n_w + 2 + n_s:]
        x, y, c = _mesh_position()
        chip = 2 * x + y
        me = 4 * x + 2 * y + c
        sibling = (x, y, 1 - c)
        others = _other_chips(x, y)

        def remote(src, dst, ssem, rsem, dev):
            return pltpu.make_async_remote_copy(src_ref=src, dst_ref=dst, send_sem=ssem, recv_sem=rsem,
                                                device_id=dev, device_id_type=MESH)

        def part(w, owner_chip, half):
            start = owner_chip * rows[w] + half * halves[w]
            return g_in[w].at[pl.ds(pl.multiple_of(start, 16), halves[w])]

        pending = []
        small_all[me] = small_ref[...]
        for j in range(N_DEV - 1):
            peer = (me + 1 + j) % N_DEV
            cp = remote(small_all.at[me], small_all.at[me], small_send.at[j], small_recv.at[j],
                        (peer // 4, (peer // 2) % 2, peer % 2))
            cp.start()
            pending.append(cp)

        local = []
        for i in range(n_s):
            half_rows = share_in[i].shape[0]
            place = share_out[i].at[pl.ds(pl.multiple_of(c * half_rows, 16), half_rows)]
            cp = pltpu.make_async_copy(share_in[i], place, share_local.at[i])
            cp.start()
            local.append(cp)
            cp = remote(share_in[i], place, share_send.at[i], share_recv.at[i], sibling)
            cp.start()
            pending.append(cp)

        for w in range(n_w):
            for k in range(N_CHIPS):
                cp = remote(part(w, k, 1 - c), from_sib[w].at[k], sib_send.at[w, k], sib_recv.at[w, k], sibling)
                cp.start()
                pending.append(cp)

        def add_stream(w, srcs, dst, n_rows):
            def chunk(start, size):
                total = None
                for i, src in enumerate(srcs):
                    buf = buf_a if i % 2 == 0 else buf_b
                    cp = pltpu.make_async_copy(src.at[pl.ds(start, size)], buf.at[pl.ds(0, size)], io_sem)
                    cp.start()
                    cp.wait()
                    val = buf[pl.ds(0, size), :].astype(F32)
                    total = val if total is None else total + val
                return total

            n_full = n_rows // ch
            rem = n_rows - n_full * ch

            def store(total, start, size):
                if dst.dtype == F32:
                    buf_o[pl.ds(0, size), :] = total
                    cp = pltpu.make_async_copy(buf_o.at[pl.ds(0, size)], dst.at[pl.ds(start, size)], io_sem)
                else:
                    buf_a[pl.ds(0, size), :] = total.astype(buf_a.dtype)
                    cp = pltpu.make_async_copy(buf_a.at[pl.ds(0, size)], dst.at[pl.ds(start, size)], io_sem)
                cp.start()
                cp.wait()

            def loop_body(i, carry):
                start = pl.multiple_of(i * ch, ch)
                store(chunk(start, ch), start, ch)
                return carry

            lax.fori_loop(0, n_full, loop_body, 0)
            if rem:
                store(chunk(n_full * ch, rem), n_full * ch, rem)

        order = [2, 0, 1]
        for w in range(n_w):
            for k in range(N_CHIPS):
                remote(part(w, k, 1 - c), from_sib[w].at[k], sib_send.at[w, k], sib_recv.at[w, k],
                       sibling).wait_recv()
        for k in order:
            px, py = others[k]
            owner = 2 * px + py
            for w in range(n_w):
                add_stream(w, [part(w, owner, c), from_sib[w].at[owner]], chip_sum[w].at[owner], halves[w])
                cp = remote(chip_sum[w].at[owner], from_chips[w].at[k], chip_send.at[w, k],
                            chip_recv.at[w, k], (px, py, c))
                cp.start()
                pending.append(cp)
        for w in range(n_w):
            add_stream(w, [part(w, chip, c), from_sib[w].at[chip]], chip_sum[w].at[chip], halves[w])

        for w in range(n_w):
            for k in range(3):
                px, py = others[k]
                remote(chip_sum[w].at[chip], from_chips[w].at[k], chip_send.at[w, k], chip_recv.at[w, k],
                       (px, py, c)).wait_recv()
            mine = outs[w].at[pl.ds(pl.multiple_of(c * halves[w], 16), halves[w])]
            add_stream(w, [chip_sum[w].at[chip], from_chips[w].at[0], from_chips[w].at[1],
                           from_chips[w].at[2]], mine, halves[w])
            cp = remote(mine, mine, fin_send.at[w], fin_recv.at[w], sibling)
            cp.start()
            pending.append(cp)
        for w in range(n_w):
            theirs = outs[w].at[pl.ds(pl.multiple_of((1 - c) * halves[w], 16), halves[w])]
            remote(theirs, theirs, fin_send.at[w], fin_recv.at[w], sibling).wait_recv()

        for j in range(N_DEV - 1):
            peer = (me + N_DEV - 1 - j) % N_DEV
            remote(small_all.at[peer], small_all.at[peer], small_send.at[j], small_recv.at[j],
                   sibling).wait_recv()
        total = small_all[0]
        for d in range(1, N_DEV):
            total = total + small_all[d]
        small_out[...] = total
        for i in range(n_s):
            half_rows = share_in[i].shape[0]
            theirs = share_out[i].at[pl.ds(pl.multiple_of((1 - c) * half_rows, 16), half_rows)]
            remote(share_in[i], theirs, share_send.at[i], share_recv.at[i], sibling).wait_recv()
        for cp in pending:
            cp.wait_send()
        for cp in local:
            cp.wait()

    hbm_scratch = ([jax.ShapeDtypeStruct((N_CHIPS, h, cols), wire) for h in halves] * 2
                   + [jax.ShapeDtypeStruct((3, h, cols), wire) for h in halves])
    out_shape = ([jax.ShapeDtypeStruct((r, cols), F32) for r in rows]
                 + [jax.ShapeDtypeStruct((rows_s, 128), F32)] + hbm_scratch
                 + [jax.ShapeDtypeStruct((2 * sh.shape[0], sh.shape[1]), F32) for sh in shares])
    res = pl.pallas_call(
        body, name="reduce_gradients",
        in_specs=[HBM_SPEC] * n_w + [VMEM_SPEC] + [HBM_SPEC] * n_s,
        out_specs=[HBM_SPEC] * n_w + [VMEM_SPEC] + [HBM_SPEC] * (3 * n_w + n_s),
        out_shape=out_shape,
        scratch_shapes=[
            pltpu.VMEM((N_DEV, rows_s, 128), F32),
            pltpu.VMEM((ch, cols), wire), pltpu.VMEM((ch, cols), wire), pltpu.VMEM((ch, cols), F32),
            pltpu.SemaphoreType.DMA((n_w, N_CHIPS)), pltpu.SemaphoreType.DMA((n_w, N_CHIPS)),
            pltpu.SemaphoreType.DMA((n_w, 3)), pltpu.SemaphoreType.DMA((n_w, 3)),
            pltpu.SemaphoreType.DMA((n_w,)), pltpu.SemaphoreType.DMA((n_w,)),
            pltpu.SemaphoreType.DMA((N_DEV - 1,)), pltpu.SemaphoreType.DMA((N_DEV - 1,)),
            pltpu.SemaphoreType.DMA,
            pltpu.SemaphoreType.DMA((max(n_s, 1),)), pltpu.SemaphoreType.DMA((max(n_s, 1),)),
            pltpu.SemaphoreType.DMA((max(n_s, 1),)),
        ],
        compiler_params=pltpu.CompilerParams(vmem_limit_bytes=VMEM_LIMIT_MIN),
    )(*grads, small, *shares)
    return res[:n_w], res[n_w], res[len(res) - n_s:] if n_s else []


def _pack_small(parts, rows):
    flat = jnp.concatenate([p.reshape(-1) for p in parts])
    flat = jnp.pad(flat, (0, rows * 128 - flat.shape[0]))
    return flat.reshape(rows, 128)


def _unpack_small(packed, shapes):
    flat = packed.reshape(-1)
    out, off = [], 0
    for shp in shapes:
        n = int(np.prod(shp))
        out.append(flat[off:off + n].reshape(shp))
        off += n
    return out


def kernel(x, g_attn, w_in, b_in, sinks_a, g_out_a, g_out_b, w_out, g_mlp, w_1, w_2, g_final, loss_target, m_g_attn, m_w_in, m_b_in, m_sinks_a, m_g_out_a, m_g_out_b, m_w_out, m_g_mlp, m_w_1, m_w_2, m_g_final, v_g_attn, v_w_in, v_b_in, v_sinks_a, v_g_out_a, v_g_out_b, v_w_out, v_g_mlp, v_w_1, v_w_2, v_g_final):
    s, d = x.shape[1], x.shape[2]
    d_in = b_in.shape[1]
    qa = g_out_a.shape[1]
    qb = g_out_b.shape[1]
    kva = 2 * N_KV_GROUPS * HEAD_DIM
    assert d_in == qa + kva + 3 * qb and qa + qb == w_out.shape[1] * N_CHIPS
    d_ff = w_1.shape[2] * N_CHIPS
    ff_shard = w_1.shape[2]
    in_shard = w_in.shape[2]
    n_heads_a, n_heads_b = qa // HEAD_DIM, qb // HEAD_DIM
    slopes_a, slopes_b = alibi_slopes(n_heads_a), alibi_slopes(n_heads_b)

    x2d = x[0]
    target = loss_target[0]

    shards = [w_in[0].T.astype(BF16), w_out[0].astype(BF16), w_1[0].astype(BF16), w_2[0].astype(BF16)]
    core_index = lax.axis_index("c").astype(jnp.int32).reshape(1)
    chip_index = (2 * lax.axis_index("x") + lax.axis_index("y")).astype(jnp.int32).reshape(1)

    tm = _tile(s, 1024)

    (h1, r1), (w_in_t,) = _norm_fwd("norm_attn", x2d, g_attn,
                                    hook=_gather_hook(shards[0], None, 0, shards[0].shape[0] // 2))

    q_a, = _project_by_class("proj_qa", h1, w_in_t, b_in, 0, qa, (1,))
    kv_a, = _project_by_class("proj_kva", h1, w_in_t, b_in, qa, kva, (1,))
    q_bs, (w_out_g,) = _project_by_class("proj_qb", h1, w_in_t, b_in, qa + kva, qb, DILATIONS,
                                         hook=_gather_hook(shards[1], None, 0, shards[1].shape[0] // 2))
    k_bs = _project_by_class("proj_kb", h1, w_in_t, b_in, qa + kva + qb, qb, DILATIONS)
    v_bs = _project_by_class("proj_vb", h1, w_in_t, b_in, qa + kva + 2 * qb, qb, DILATIONS)

    quarter = shards[2].shape[0] // 8
    sinks = sinks_a.reshape(-1)
    (o_a, lse_a), (w_1_g,) = _attn_fwd("attn_a_fwd", q_a, kv_a, dil=1, max_steps=WINDOW_A - 1, slopes=slopes_a,
                                       sinks=sinks, hook=_gather_hook(shards[2], None, 0, quarter))
    o_a = o_a[0]
    o_bs, lse_bs = [], []
    for n, (window, dil) in enumerate(DILATED_BRANCHES):
        (o, l), (w_1_g,) = _attn_fwd(f"attn_b{dil}_fwd", q_bs[n], (k_bs[n], v_bs[n]), dil=dil,
                                     max_steps=window // dil, slopes=slopes_b,
                                     hook=_gather_hook(shards[2], w_1_g, (n + 1) * quarter, (n + 2) * quarter))
        o_bs.append(o)
        lse_bs.append(l)
    w_1_g = w_1_g.reshape(N_CHIPS, d, ff_shard)
    mix, o_b, *lse_tot, r_a, r_b = _mix_fwd(o_a, o_bs, lse_bs, g_out_a, g_out_b)

    tn = _tile(d, 512)
    a_spec, b_spec = _mm_specs("nn", tm, tn, d)
    tile_mn = pl.BlockSpec((tm, tn), lambda i, j, k: (i, j))
    x2 = _matmul("out_proj", mix, w_out_g, [x2d], mode="nn", grid=(s // tm, d // tn, 1),
                 a_spec=a_spec, b_spec=b_spec, extra_specs=[tile_mn],
                 out_shapes=[jax.ShapeDtypeStruct((s, d), F32)], out_specs=[tile_mn],
                 epilogue=lambda acc, res: (acc + res,))[0]

    h2, r2 = _norm_fwd("norm_mlp", x2, g_mlp)

    tn = _tile(ff_shard, 512)
    per = ff_shard // tn
    a_spec, _ = _mm_specs("nn", tm, tn, d)
    (u,), (w_2_g,) = _matmul(
        "mlp_up", h2, w_1_g, [], mode="nn", grid=(s // tm, d_ff // tn, 1),
        a_spec=a_spec, b_spec=pl.BlockSpec((None, d, tn), lambda i, j, k: (j // per, 0, j % per)),
        extra_specs=[], out_shapes=[jax.ShapeDtypeStruct((s, d_ff), BF16)], out_specs=[tile_mn],
        epilogue=lambda acc: (jnp.maximum(acc, 0.0),),
        hook=_gather_hook(shards[3], None, 0, shards[3].shape[0] // 2))

    tn = _tile(d, 1024)
    tk = _tile(d_ff, 2048)
    a_spec, b_spec = _mm_specs("nn", tm, tn, tk)
    tile_mn = pl.BlockSpec((tm, tn), lambda i, j, k: (i, j))
    x3 = _matmul("mlp_down", u, w_2_g, [x2], mode="nn", grid=(s // tm, d // tn, d_ff // tk),
                 a_spec=a_spec, b_spec=b_spec, extra_specs=[tile_mn],
                 out_shapes=[jax.ShapeDtypeStruct((s, d), F32)], out_specs=[tile_mn],
                 prologue=lambda a: a * a, epilogue=lambda acc, res: (acc + res,), acc_shape=(tm, tn))[0]

    dx3, dx3b, loss_part, dg_final = _loss_head(x3, target, g_final.reshape(1, d))

    tn = _tile(d_ff, 512)
    a_spec, b_spec = _mm_specs("nt", tm, tn, d)
    tile_mn = pl.BlockSpec((tm, tn), lambda i, j, k: (i, j))
    dpre = _matmul("mlp_down_dx", dx3b, w_2_g, [u], mode="nt", grid=(s // tm, d_ff // tn, 1),
                   a_spec=a_spec, b_spec=b_spec, extra_specs=[tile_mn],
                   out_shapes=[jax.ShapeDtypeStruct((s, d_ff), BF16)], out_specs=[tile_mn],
                   epilogue=lambda acc, uu: (acc * (2.0 * uu.astype(F32)),))[0]

    wire = GRAD_WIRE_DTYPE
    tk_s = _tile(s, 2048)
    tmw = _tile(d_ff, 1024)
    a_spec, b_spec = _mm_specs("tn", tmw, d, tk_s)
    dw_2 = _matmul("mlp_down_dw", u, dx3b, [], mode="tn", grid=(d_ff // tmw, 1, s // tk_s),
                   a_spec=a_spec, b_spec=b_spec, extra_specs=[],
                   out_shapes=[jax.ShapeDtypeStruct((d_ff, d), wire)],
                   out_specs=[pl.BlockSpec((tmw, d), lambda i, j, k: (i, j))],
                   prologue=lambda a: a * a, epilogue=lambda acc: (acc,), acc_shape=(tmw, d))[0]

    tn = _tile(d, 1024)
    tk = _tile(ff_shard, 2048)
    per = ff_shard // tk
    a_spec, _ = _mm_specs("nt", tm, tn, tk)
    tile_mn = pl.BlockSpec((tm, tn), lambda i, j, k: (i, j))
    dh2 = _matmul("mlp_up_dx", dpre, w_1_g, [], mode="nt", grid=(s // tm, d // tn, d_ff // tk),
                  a_spec=a_spec, b_spec=pl.BlockSpec((None, tn, tk), lambda i, j, k: (k // per, j, k % per)),
                  extra_specs=[], out_shapes=[jax.ShapeDtypeStruct((s, d), F32)], out_specs=[tile_mn],
                  epilogue=lambda acc: (acc,), acc_shape=(tm, tn))[0]

    tmw = _tile(d, 1024)
    tnw = _tile(ff_shard, 2048)
    per = ff_shard // tnw
    a_spec, b_spec = _mm_specs("tn", tmw, tnw, tk_s)
    dw_1 = _matmul("mlp_up_dw", h2, dpre, [], mode="tn", grid=(d // tmw, d_ff // tnw, s // tk_s),
                   a_spec=a_spec, b_spec=b_spec, extra_specs=[],
                   out_shapes=[jax.ShapeDtypeStruct((N_CHIPS, d, ff_shard), wire)],
                   out_specs=[pl.BlockSpec((None, tmw, tnw), lambda i, j, k: (j // per, i, j % per))],
                   epilogue=lambda acc: (acc,), acc_shape=(tmw, tnw))[0]

    dw_1 = dw_1.reshape(N_CHIPS * d, ff_shard)
    (dx2, dx2b, dg_mlp), (sib_2, sib_1) = _norm_bwd(
        "norm_mlp_bwd", dh2, x2, r2, g_mlp, dx3, hook=_merge_hooks([_exchange_hook(dw_2), _exchange_hook(dw_1)]))
    chip_sum_2 = _chip_add("chip_add_w_2", dw_2, sib_2, core_index)
    chip_sum_1 = _chip_add("chip_add_w_1", dw_1, sib_1, core_index)

    tn = _tile(d, 512)
    a_spec, b_spec = _mm_specs("nt", tm, tn, d)
    tile_mn = pl.BlockSpec((tm, tn), lambda i, j, k: (i, j))
    dmix = _matmul("out_proj_dx", dx2b, w_out_g, [], mode="nt", grid=(s // tm, d // tn, 1),
                   a_spec=a_spec, b_spec=b_spec, extra_specs=[],
                   out_shapes=[jax.ShapeDtypeStruct((s, d), F32)], out_specs=[tile_mn],
                   epilogue=lambda acc: (acc,))[0]

    tmw = _tile(d, 1024)
    a_spec, b_spec = _mm_specs("tn", tmw, d, tk_s)
    dw_out = _matmul("out_proj_dw", mix, dx2b, [], mode="tn", grid=(d // tmw, 1, s // tk_s),
                     a_spec=a_spec, b_spec=b_spec, extra_specs=[],
                     out_shapes=[jax.ShapeDtypeStruct((d, d), wire)],
                     out_specs=[pl.BlockSpec((tmw, d), lambda i, j, k: (i, j))],
                     epilogue=lambda acc: (acc,), acc_shape=(tmw, d))[0]

    mix_grads, (sib_out,) = _mix_bwd(dmix, o_a, o_b, r_a, r_b, g_out_a, g_out_b, hook=_exchange_hook(dw_out))
    do_a, do_bs, delta_a, delta_bs = mix_grads[0], mix_grads[1:4], mix_grads[4], mix_grads[5:8]
    dg_out_a, dg_out_b = mix_grads[8:]
    chip_sum_out = _chip_add("chip_add_w_out", dw_out, sib_out, core_index)

    (dq_a, dkv_a, dsinks), (chips_2,) = _attn_bwd(
        "attn_a_bwd", q_a, kv_a, do_a[None], lse_a, delta_a[None], dil=1, max_steps=WINDOW_A - 1,
        slopes=slopes_a, sinks=sinks, hook=_scatter_hook(chip_sum_2))
    dqs, dks, dvs = [], [], []
    scatter = {1: chip_sum_1, 4: chip_sum_out}
    arrived = {}
    for n, (window, dil) in enumerate(DILATED_BRANCHES):
        res = _attn_bwd(f"attn_b{dil}_bwd", q_bs[n], (k_bs[n], v_bs[n]), do_bs[n], lse_tot[n],
                        delta_bs[n], dil=dil, max_steps=window // dil, slopes=slopes_b,
                        hook=_scatter_hook(scatter[dil]) if dil in scatter else None)
        if dil in scatter:
            res, (arrived[dil],) = res
        dq, dk, dv = res
        dqs.append(dq)
        dks.append(dk)
        dvs.append(dv)
    half_2 = _final_add("final_add_w_2", chip_sum_2, chips_2, chip_index)
    half_1 = _final_add("final_add_w_1", chip_sum_1, arrived[1], chip_index)
    half_out = _final_add("final_add_w_out", chip_sum_out, arrived[4], chip_index)
    dproj, db_in = _assemble_dproj(dq_a[0], dkv_a[0], dqs, dks, dvs)

    tmw = d_in // 2 if (d_in // 2) % 128 == 0 else d_in
    tnw = _tile(d, 1024)
    tk_s = _tile(s, 1024)
    a_spec, b_spec = _mm_specs("tn", tmw, tnw, tk_s)
    dw_in_t = _matmul("in_proj_dw", dproj, h1, [], mode="tn", grid=(d_in // tmw, d // tnw, s // tk_s),
                      a_spec=a_spec, b_spec=b_spec, extra_specs=[],
                      out_shapes=[jax.ShapeDtypeStruct((d_in, d), wire)],
                      out_specs=[pl.BlockSpec((tmw, tnw), lambda i, j, k: (i, j))],
                      epilogue=lambda acc: (acc,), acc_shape=(tmw, tnw))[0]

    tn = _tile(d, 512)
    a_spec, b_spec = _mm_specs("nn", tm, tn, d_in)
    tile_mn = pl.BlockSpec((tm, tn), lambda i, j, k: (i, j))
    (dh1,), (sib_in,) = _matmul("in_proj_dx", dproj, w_in_t, [], mode="nn", grid=(s // tm, d // tn, 1),
                                a_spec=a_spec, b_spec=b_spec, extra_specs=[],
                                out_shapes=[jax.ShapeDtypeStruct((s, d), F32)], out_specs=[tile_mn],
                                epilogue=lambda acc: (acc,), hook=_exchange_hook(dw_in_t))
    chip_sum_in = _chip_add("chip_add_w_in", dw_in_t, sib_in, core_index)

    grad_x, _, dg_attn = _norm_bwd("norm_attn_bwd", dh1, x2d, r1, g_attn, dx2)

    small_parts = [dg_attn, db_in, dsinks[:, :n_heads_a], dg_out_a, dg_out_b, dg_mlp, dg_final]
    small_shapes = [g_attn.shape, b_in.shape, sinks_a.shape, g_out_a.shape, g_out_b.shape, g_mlp.shape,
                    g_final.shape]
    n_small = sum(int(np.prod(shp)) for shp in small_shapes)
    rows_s = -(-n_small // (8 * 128)) * 8
    (gw_in_t,), (gw_out, gw_1, gw_2), small_sum = _finish_reduction(
        [chip_sum_in], [half_out, half_1, half_2], _pack_small(small_parts, rows_s))
    gw_in = gw_in_t.T
    g_small = _unpack_small(small_sum, small_shapes)

    upd_in = _adamw("adamw_w_in", w_in[0], gw_in, m_w_in[0], v_w_in[0])
    upd_out = _adamw("adamw_w_out", w_out[0], gw_out, m_w_out[0], v_w_out[0])
    upd_1 = _adamw("adamw_w_1", w_1[0], gw_1, m_w_1[0], v_w_1[0])
    upd_2 = _adamw("adamw_w_2", w_2[0], gw_2, m_w_2[0], v_w_2[0])
    small_w = [g_attn, b_in, sinks_a, g_out_a, g_out_b, g_mlp, g_final]
    small_m = [m_g_attn, m_b_in, m_sinks_a, m_g_out_a, m_g_out_b, m_g_mlp, m_g_final]
    small_v = [v_g_attn, v_b_in, v_sinks_a, v_g_out_a, v_g_out_b, v_g_mlp, v_g_final]
    upd_small = _adamw("adamw_small", _pack_small(small_w, rows_s), small_sum,
                       _pack_small(small_m, rows_s), _pack_small(small_v, rows_s))
    d_small, m_small, v_small = [_unpack_small(t, small_shapes) for t in upd_small]

    loss = lax.psum(loss_part[0, 0], ("x", "y", "c"))

    def ordered(small, big):
        w_in_v, w_out_v, w_1_v, w_2_v = big
        return [small[0], w_in_v[None], small[1], small[2], small[3], small[4], w_out_v[None], small[5],
                w_1_v[None], w_2_v[None], small[6]]

    grads = ordered(g_small, (gw_in, gw_out, gw_1, gw_2))
    deltas = ordered(d_small, (upd_in[0], upd_out[0], upd_1[0], upd_2[0]))
    new_m = ordered(m_small, (upd_in[1], upd_out[1], upd_1[1], upd_2[1]))
    new_v = ordered(v_small, (upd_in[2], upd_out[2], upd_1[2], upd_2[2]))
    return (loss, grad_x[None], *grads, *deltas, *new_m, *new_v)
```

```python
import functools

import jax
import jax.numpy as jnp
import numpy as np
from jax import lax
from jax.experimental import pallas as pl
from jax.experimental.pallas import tpu as pltpu

F32 = jnp.float32
BF16 = jnp.bfloat16

HEAD_DIM = 64
BLOCK = 128
PAIR = 2 * HEAD_DIM
N_KV_GROUPS = 2
WINDOW_A = 128
DILATED_BRANCHES = ((128, 1), (512, 4), (2048, 16))
EPS = 1e-5
NEG_INF = -1e30
ATT_SCALE = HEAD_DIM ** -0.5

ADAM_LR = 0.001
ADAM_B1 = 0.9
ADAM_B2 = 0.999
ADAM_EPS = 1e-08
ADAM_WD = 0.01
ADAM_STEP = 10

N_CHIPS = 4
N_DEV = 8
MESH = pl.DeviceIdType.MESH
GRAD_WIRE_DTYPE = jnp.bfloat16

VMEM_CAPACITY_V7X = 64 * 1024 * 1024
VMEM_LIMIT_MAX = 56 * 1024 * 1024
VMEM_LIMIT_MIN = 32 * 1024 * 1024

HBM_SPEC = pl.BlockSpec(memory_space=pltpu.HBM)
VMEM_SPEC = pl.BlockSpec(memory_space=pltpu.VMEM)
SMEM_SPEC = pl.BlockSpec(memory_space=pltpu.SMEM)


def _nbytes(shape, dtype):
    return int(np.prod([s for s in shape if s is not None])) * jnp.dtype(dtype).itemsize


def _params(semantics, block_bytes):
    limit = min(max(2 * block_bytes + (4 << 20), VMEM_LIMIT_MIN), VMEM_LIMIT_MAX)
    return pltpu.CompilerParams(dimension_semantics=semantics, vmem_limit_bytes=limit)


class _Hook:
    def __init__(self, operands, out_shape, sems, start, finish, mid=None, aliases=None):
        self.operands, self.out_shape, self.sems = list(operands), list(out_shape), list(sems)
        self.start, self.mid, self.finish = start, mid, finish
        self.aliases = dict(aliases or {})


def _merge_hooks(hooks):
    hooks = [h for h in hooks if h is not None]
    if len(hooks) <= 1:
        return hooks[0] if hooks else None
    n_op = np.cumsum([0] + [len(h.operands) for h in hooks])
    n_out = np.cumsum([0] + [len(h.out_shape) for h in hooks])
    n_sem = np.cumsum([0] + [len(h.sems) for h in hooks])

    def run(which):
        def fn(ops, outs, sems):
            for i, h in enumerate(hooks):
                f = getattr(h, which)
                if f is not None:
                    f(ops[n_op[i]:n_op[i + 1]], outs[n_out[i]:n_out[i + 1]], sems[n_sem[i]:n_sem[i + 1]])
        return fn

    aliases = {}
    for i, h in enumerate(hooks):
        aliases.update({int(n_op[i]) + a: int(n_out[i]) + b for a, b in h.aliases.items()})
    return _Hook(sum([h.operands for h in hooks], []), sum([h.out_shape for h in hooks], []),
                 sum([h.sems for h in hooks], []), run("start"), run("finish"),
                 run("mid") if any(h.mid for h in hooks) else None, aliases)


HOOK_MID_FRACTION = 0.6


def _call(body, hook, *, name, grid, in_specs, out_specs, out_shape, scratch_shapes=(), compiler_params):
    in_specs, out_specs, out_shape = list(in_specs), list(out_specs), list(out_shape)
    scratch_shapes = list(scratch_shapes)
    if hook is None:
        call = pl.pallas_call(body, name=name, grid=grid, in_specs=in_specs, out_specs=out_specs,
                              out_shape=out_shape, scratch_shapes=scratch_shapes,
                              compiler_params=compiler_params)
        return lambda *operands: (call(*operands), [])
    n_in, n_hin, n_out, n_hout, n_scr = (len(in_specs), len(hook.operands), len(out_specs),
                                         len(hook.out_shape), len(scratch_shapes))
    total = int(np.prod(grid))
    t_mid = min(int(total * HOOK_MID_FRACTION), total - 1)

    def wrapped(*refs):
        ins, h_in = refs[:n_in], refs[n_in:n_in + n_hin]
        o0 = n_in + n_hin
        outs, h_out = refs[o0:o0 + n_out], refs[o0 + n_out:o0 + n_out + n_hout]
        s0 = o0 + n_out + n_hout
        scr, h_sems = refs[s0:s0 + n_scr], refs[s0 + n_scr:]
        t = pl.program_id(0)
        for axis in range(1, len(grid)):
            t = t * grid[axis] + pl.program_id(axis)

        @pl.when(t == 0)
        def _():
            hook.start(h_in, h_out, h_sems)

        body(*ins, *outs, *scr)
        if hook.mid is not None:
            @pl.when(t == t_mid)
            def _():
                hook.mid(h_in, h_out, h_sems)

        @pl.when(t == total - 1)
        def _():
            hook.finish(h_in, h_out, h_sems)

    params = pltpu.CompilerParams(dimension_semantics=("arbitrary",) * len(grid),
                                  vmem_limit_bytes=compiler_params.vmem_limit_bytes)
    call = pl.pallas_call(
        wrapped, name=name, grid=grid,
        in_specs=in_specs + [HBM_SPEC] * n_hin, out_specs=out_specs + [HBM_SPEC] * n_hout,
        out_shape=out_shape + hook.out_shape, scratch_shapes=scratch_shapes + hook.sems,
        input_output_aliases={n_in + a: n_out + b for a, b in hook.aliases.items()},
        compiler_params=params)

    def run(*operands):
        res = call(*operands, *hook.operands)
        return res[:n_out], res[n_out:]

    return run


def _remote(src, dst, send_sem, recv_sem, device):
    return pltpu.make_async_remote_copy(src_ref=src, dst_ref=dst, send_sem=send_sem, recv_sem=recv_sem,
                                        device_id=device, device_id_type=MESH)


def alibi_slopes(n):
    return [float(v) for v in np.asarray(2.0 ** (-8.0 * (np.arange(n) + 1) / n), dtype=np.float32)]


def _matmul(name, a, b, extras, *, mode, grid, a_spec, b_spec, extra_specs, out_shapes, out_specs,
            epilogue, prologue=None, acc_shape=None, hook=None):
    dims = {"nn": ((1,), (0,)), "nt": ((1,), (1,)), "tn": ((0,), (0,))}[mode]
    nk = grid[2]
    n_ex, n_out = len(extras), len(out_shapes)

    def body(a_ref, b_ref, *rest):
        ex, outs = rest[:n_ex], rest[n_ex:n_ex + n_out]
        av = a_ref[...]
        if prologue is not None:
            av = prologue(av)
        part = lax.dot_general(av, b_ref[...], (dims, ((), ())), preferred_element_type=F32)

        def finish(acc):
            res = epilogue(acc, *[e[...] for e in ex])
            for o, r in zip(outs, res):
                o[...] = r.astype(o.dtype)

        if nk == 1:
            finish(part)
        else:
            acc_ref = rest[-1]
            k = pl.program_id(2)

            @pl.when(k == 0)
            def _():
                acc_ref[...] = part

            @pl.when(k > 0)
            def _():
                acc_ref[...] += part

            @pl.when(k == nk - 1)
            def _():
                finish(acc_ref[...])

    blocks = [(a_spec.block_shape, a.dtype), (b_spec.block_shape, b.dtype)]
    blocks += [(s.block_shape, e.dtype) for s, e in zip(extra_specs, extras)]
    blocks += [(s.block_shape, o.dtype) for s, o in zip(out_specs, out_shapes)]
    nbytes = sum(_nbytes(s, d) for s, d in blocks)
    scratch = []
    if nk > 1:
        scratch.append(pltpu.VMEM(acc_shape, F32))
        nbytes += _nbytes(acc_shape, F32)
    res, hook_res = _call(
        body, hook, name=name, grid=grid,
        in_specs=[a_spec, b_spec, *extra_specs], out_specs=list(out_specs), out_shape=list(out_shapes),
        scratch_shapes=scratch,
        compiler_params=_params(("parallel", "parallel", "arbitrary"), nbytes),
    )(a, b, *extras)
    return res if hook is None else (res, hook_res)


def _mm_specs(mode, tm, tn, tk, b_block=None, b_map=None):
    if mode == "tn":
        a_spec = pl.BlockSpec((tk, tm), lambda i, j, k: (k, i))
    else:
        a_spec = pl.BlockSpec((tm, tk), lambda i, j, k: (i, k))
    if b_block is not None:
        b_spec = pl.BlockSpec(b_block, b_map)
    elif mode == "nt":
        b_spec = pl.BlockSpec((tn, tk), lambda i, j, k: (j, k))
    else:
        b_spec = pl.BlockSpec((tk, tn), lambda i, j, k: (k, j))
    return a_spec, b_spec


def _project_by_class(name, h, w_t, bias, row_off, width, dilations, hook=None):
    s, d = h.shape
    tm = _tile(s, 1024)
    tn = 512 if width % 512 == 0 and row_off % 512 == 0 else _tile(width, 256)
    off = row_off // tn
    assert row_off % tn == 0 and tn % 128 == 0
    n_out = len(dilations)

    def body(h_ref, w_ref, b_ref, *rest):
        outs, perm_ref = rest[:n_out], rest[n_out]
        acc = lax.dot_general(h_ref[...], w_ref[...], (((1,), (1,)), ((), ())), preferred_element_type=F32)
        acc = acc + b_ref[...]
        for j in range(tn // 128):
            cols = slice(j * 128, (j + 1) * 128)
            for o_ref, dil in zip(outs, dilations):
                _to_classes(o_ref, cols, acc[:, cols], perm_ref, dil)

    blocks = tm * d * 2 + tn * d * 2 + 3 * tm * tn * 2 + tm * 128 * 4
    res, hook_res = _call(
        body, hook, name=name, grid=(s // tm, width // tn),
        in_specs=[pl.BlockSpec((tm, d), lambda i, j: (i, 0)), pl.BlockSpec((tn, d), lambda i, j: (j + off, 0)),
                  pl.BlockSpec((1, tn), lambda i, j: (0, j + off))],
        out_specs=[pl.BlockSpec((dil, tm // dil, tn), lambda i, j: (0, i, j)) for dil in dilations],
        out_shape=[_class_shape(dil, s, width, BF16) for dil in dilations],
        scratch_shapes=[pltpu.VMEM((tm, 128), F32)],
        compiler_params=_params(("parallel", "parallel"), blocks),
    )(h, w_t, bias)
    return res if hook is None else (res, hook_res)


def _tile(n, want):
    if n <= want:
        return n
    t = (want // 128) * 128
    while t > 128 and n % t:
        t -= 128
    assert n % t == 0, (n, want)
    return t


def _row_tile(s):
    return 256 if s % 256 == 0 else s


def _norm_fwd(name, x, g, hook=None):
    s, d = x.shape
    tm = _row_tile(s)

    def body(x_ref, g_ref, h_ref, r_ref):
        xv = x_ref[...]
        r = lax.rsqrt(jnp.mean(xv * xv, axis=-1, keepdims=True) + EPS)
        h_ref[...] = ((xv * r) * g_ref[...]).astype(BF16)
        r_ref[...] = r

    row = pl.BlockSpec((tm, d), lambda i: (i, 0))
    res, hook_res = _call(
        body, hook, name=name, grid=(s // tm,),
        in_specs=[row, pl.BlockSpec((1, d), lambda i: (0, 0))],
        out_specs=[row, pl.BlockSpec((tm, 1), lambda i: (i, 0))],
        out_shape=[jax.ShapeDtypeStruct((s, d), BF16), jax.ShapeDtypeStruct((s, 1), F32)],
        compiler_params=_params(("parallel",), tm * d * 6),
    )(x, g)
    return res if hook is None else (res, hook_res)


def _norm_bwd(name, dh, x, r, g, dres, hook=None):
    s, d = x.shape
    tm = _row_tile(s)

    def body(dh_ref, x_ref, r_ref, g_ref, dres_ref, dx_ref, dxb_ref, dg_ref):
        rv = r_ref[...]
        xn = x_ref[...] * rv
        dhv = dh_ref[...]
        dxn = dhv * g_ref[...]
        dx = dres_ref[...] + rv * (dxn - xn * jnp.mean(dxn * xn, axis=-1, keepdims=True))
        dx_ref[...] = dx
        dxb_ref[...] = dx.astype(BF16)
        part = jnp.sum(dhv * xn, axis=0, keepdims=True)

        @pl.when(pl.program_id(0) == 0)
        def _():
            dg_ref[...] = part

        @pl.when(pl.program_id(0) > 0)
        def _():
            dg_ref[...] += part

    row = pl.BlockSpec((tm, d), lambda i: (i, 0))
    vec = pl.BlockSpec((1, d), lambda i: (0, 0))
    res, hook_res = _call(
        body, hook, name=name, grid=(s // tm,),
        in_specs=[row, row, pl.BlockSpec((tm, 1), lambda i: (i, 0)), vec, row],
        out_specs=[row, row, vec],
        out_shape=[jax.ShapeDtypeStruct((s, d), F32), jax.ShapeDtypeStruct((s, d), BF16),
                   jax.ShapeDtypeStruct((1, d), F32)],
        compiler_params=_params(("arbitrary",), tm * d * 18),
    )(dh, x, r, g, dres)
    return res if hook is None else (res, hook_res)


def _loss_head(x3, target, g):
    s, d = x3.shape
    tm = _row_tile(s)

    def body(x_ref, t_ref, g_ref, dx_ref, dxb_ref, loss_ref, dg_ref):
        xv = x_ref[...]
        gv = g_ref[...]
        r = lax.rsqrt(jnp.mean(xv * xv, axis=-1, keepdims=True) + EPS)
        xn = xv * r
        err = xn * gv - t_ref[...]
        loss = 0.5 * jnp.sum(jnp.mean(err * err, axis=-1, keepdims=True), axis=0, keepdims=True)
        dy = err / d
        dxn = dy * gv
        dx = r * (dxn - xn * jnp.mean(dxn * xn, axis=-1, keepdims=True))
        dx_ref[...] = dx
        dxb_ref[...] = dx.astype(BF16)
        dg = jnp.sum(dy * xn, axis=0, keepdims=True)
        loss_row = jnp.broadcast_to(loss, (1, 128))

        @pl.when(pl.program_id(0) == 0)
        def _():
            dg_ref[...] = dg
            loss_ref[...] = loss_row

        @pl.when(pl.program_id(0) > 0)
        def _():
            dg_ref[...] += dg
            loss_ref[...] += loss_row

    row = pl.BlockSpec((tm, d), lambda i: (i, 0))
    vec = pl.BlockSpec((1, d), lambda i: (0, 0))
    return pl.pallas_call(
        body, name="loss_head", grid=(s // tm,),
        in_specs=[row, row, vec],
        out_specs=[row, row, pl.BlockSpec((1, 128), lambda i: (0, 0)), vec],
        out_shape=[jax.ShapeDtypeStruct((s, d), F32), jax.ShapeDtypeStruct((s, d), BF16),
                   jax.ShapeDtypeStruct((1, 128), F32), jax.ShapeDtypeStruct((1, d), F32)],
        compiler_params=_params(("arbitrary",), tm * d * 14),
    )(x3, target, g)


def _low_lanes(rows):
    return lax.broadcasted_iota(jnp.int32, (rows, PAIR), 1) < HEAD_DIM


def _to_classes(dst_ref, cols, value, perm_ref, dil):
    rows = value.shape[0]
    if dil == 1:
        dst_ref[0, :, cols] = value.astype(dst_ref.dtype)
        return
    perm_ref[...] = value
    for r in range(dil):
        dst_ref[r, :, cols] = perm_ref[pl.ds(r, rows // dil, stride=dil), :].astype(dst_ref.dtype)


def _from_classes(src_ref, cols, perm_ref, dil):
    if dil == 1:
        return src_ref[0, :, cols]
    rows = perm_ref.shape[0]
    for r in range(dil):
        perm_ref[pl.ds(r, rows // dil, stride=dil), :] = src_ref[r, :, cols]
    return perm_ref[...]


def _class_spec(dil, tm, width):
    return pl.BlockSpec((dil, tm // dil, width), lambda i: (0, i, 0))


def _class_shape(dil, s, width, dtype):
    return jax.ShapeDtypeStruct((dil, s // dil, width), dtype)


DILATIONS = tuple(d for _, d in DILATED_BRANCHES)


def _mix_fwd(oa, obs, lses, ga, gb):
    s, qa = oa.shape
    qb = obs[0].shape[2]
    tm = _row_tile(s)
    all_lanes = slice(0, 128)

    def body(oa_ref, o1_ref, o2_ref, o3_ref, l1_ref, l2_ref, l3_ref, ga_ref, gb_ref,
             mix_ref, ob_ref, t1_ref, t2_ref, t3_ref, ra_ref, rb_ref, perm_ref):
        oav = oa_ref[...]
        ra = lax.rsqrt(jnp.mean(oav * oav, axis=-1, keepdims=True) + EPS)
        ra_ref[...] = ra
        mix_ref[:, 0:qa] = ((oav * ra) * ga_ref[...]).astype(BF16)
        l1, l2, l3 = [_from_classes(l_ref, all_lanes, perm_ref, dil)
                      for l_ref, dil in zip((l1_ref, l2_ref, l3_ref), DILATIONS)]
        mx = jnp.maximum(jnp.maximum(l1, l2), l3)
        e1, e2, e3 = jnp.exp(l1 - mx), jnp.exp(l2 - mx), jnp.exp(l3 - mx)
        tot = e1 + e2 + e3
        lse = mx + jnp.log(tot)
        for t_ref, dil in zip((t1_ref, t2_ref, t3_ref), DILATIONS):
            _to_classes(t_ref, all_lanes, lse, perm_ref, dil)
        ws = (e1 / tot, e2 / tot, e3 / tot)
        low = _low_lanes(tm)
        ssq = jnp.zeros((tm, 1), F32)
        for i in range(qb // PAIR):
            sl = slice(i * PAIR, (i + 1) * PAIR)
            acc = jnp.zeros((tm, PAIR), F32)
            for w, o_ref, dil in zip(ws, (o1_ref, o2_ref, o3_ref), DILATIONS):
                wexp = jnp.where(low, w[:, 2 * i:2 * i + 1], w[:, 2 * i + 1:2 * i + 2])
                acc = acc + wexp * _from_classes(o_ref, sl, perm_ref, dil)
            ob_ref[:, sl] = acc
            ssq = ssq + jnp.sum(acc * acc, axis=-1, keepdims=True)
        rb = lax.rsqrt(ssq / qb + EPS)
        rb_ref[...] = rb
        mix_ref[:, qa:qa + qb] = ((ob_ref[...] * rb) * gb_ref[...]).astype(BF16)

    def row(w):
        return pl.BlockSpec((tm, w), lambda i: (i, 0))

    def vec(w):
        return pl.BlockSpec((1, w), lambda i: (0, 0))

    return pl.pallas_call(
        body, name="mix_fwd", grid=(s // tm,),
        in_specs=([row(qa)] + [_class_spec(d, tm, qb) for d in DILATIONS]
                  + [_class_spec(d, tm, 128) for d in DILATIONS] + [vec(qa), vec(qb)]),
        out_specs=([row(qa + qb), row(qb)] + [_class_spec(d, tm, 128) for d in DILATIONS] + [row(1), row(1)]),
        out_shape=([jax.ShapeDtypeStruct((s, qa + qb), BF16), jax.ShapeDtypeStruct((s, qb), F32)]
                   + [_class_shape(d, s, 128, F32) for d in DILATIONS]
                   + [jax.ShapeDtypeStruct((s, 1), F32), jax.ShapeDtypeStruct((s, 1), F32)]),
        scratch_shapes=[pltpu.VMEM((tm, 128), F32)],
        compiler_params=_params(("parallel",), tm * (qa + 4 * qb) * 4 + tm * (qa + qb) * 2 + tm * 4096),
    )(oa, *obs, *lses, ga, gb)


def _head_rowsums(prod, rows):
    low = _low_lanes(rows)
    lane = lax.broadcasted_iota(jnp.int32, (rows, 128), 1)
    out = jnp.zeros((rows, 128), F32)
    for i in range(prod.shape[1] // PAIR):
        tile = prod[:, i * PAIR:(i + 1) * PAIR]
        lo = jnp.sum(jnp.where(low, tile, 0.0), axis=-1, keepdims=True)
        hi = jnp.sum(jnp.where(low, 0.0, tile), axis=-1, keepdims=True)
        out = jnp.where(lane == 2 * i, lo, out)
        out = jnp.where(lane == 2 * i + 1, hi, out)
    return out


def _mix_bwd(dmix, oa, ob, ra, rb, ga, gb, hook=None):
    s, qa = oa.shape
    qb = ob.shape[1]
    tm = _row_tile(s)

    def one(dy, o, r, g):
        xn = o * r
        dxn = dy * g
        do = r * (dxn - xn * jnp.mean(dxn * xn, axis=-1, keepdims=True))
        return do, jnp.sum(dy * xn, axis=0, keepdims=True), _head_rowsums(do * o, tm)

    def body(dmix_ref, oa_ref, ob_ref, ra_ref, rb_ref, ga_ref, gb_ref,
             doa_ref, dob1_ref, dob2_ref, dob3_ref, dla_ref, dlb1_ref, dlb2_ref, dlb3_ref,
             dga_ref, dgb_ref, perm_ref):
        doa, dga, dla = one(dmix_ref[:, 0:qa], oa_ref[...], ra_ref[...], ga_ref[...])
        dob, dgb, dlb = one(dmix_ref[:, qa:qa + qb], ob_ref[...], rb_ref[...], gb_ref[...])
        doa_ref[...] = doa.astype(BF16)
        dla_ref[...] = dla
        for dob_ref, dlb_ref, dil in zip((dob1_ref, dob2_ref, dob3_ref), (dlb1_ref, dlb2_ref, dlb3_ref),
                                         DILATIONS):
            _to_classes(dlb_ref, slice(0, 128), dlb, perm_ref, dil)
            for i in range(qb // PAIR):
                sl = slice(i * PAIR, (i + 1) * PAIR)
                _to_classes(dob_ref, sl, dob[:, sl], perm_ref, dil)

        @pl.when(pl.program_id(0) == 0)
        def _():
            dga_ref[...] = dga
            dgb_ref[...] = dgb

        @pl.when(pl.program_id(0) > 0)
        def _():
            dga_ref[...] += dga
            dgb_ref[...] += dgb

    def row(w):
        return pl.BlockSpec((tm, w), lambda i: (i, 0))

    def vec(w):
        return pl.BlockSpec((1, w), lambda i: (0, 0))

    res, hook_res = _call(
        body, hook, name="mix_bwd", grid=(s // tm,),
        in_specs=[row(qa + qb), row(qa), row(qb), row(1), row(1), vec(qa), vec(qb)],
        out_specs=([row(qa)] + [_class_spec(d, tm, qb) for d in DILATIONS] + [row(128)]
                   + [_class_spec(d, tm, 128) for d in DILATIONS] + [vec(qa), vec(qb)]),
        out_shape=([jax.ShapeDtypeStruct((s, qa), BF16)] + [_class_shape(d, s, qb, BF16) for d in DILATIONS]
                   + [jax.ShapeDtypeStruct((s, 128), F32)] + [_class_shape(d, s, 128, F32) for d in DILATIONS]
                   + [jax.ShapeDtypeStruct((1, qa), F32), jax.ShapeDtypeStruct((1, qb), F32)]),
        scratch_shapes=[pltpu.VMEM((tm, 128), F32)],
        compiler_params=_params(("arbitrary",), tm * (qa + qb) * 16),
    )(dmix, oa, ob, ra, rb, ga, gb)
    return res if hook is None else (res, hook_res)


def _assemble_dproj(dqa, dkva, dqs, dks, dvs):
    s, qa = dqa.shape
    kva = dkva.shape[1]
    qb = dqs[0].shape[2]
    width = qa + kva + 3 * qb
    tm = _row_tile(s)

    def body(dqa_ref, dkva_ref, q1, q2, q3, k1, k2, k3, v1, v2, v3, dp_ref, db_ref, perm_ref):
        first = pl.program_id(0) == 0

        def emit(off, val):
            dp_ref[:, off:off + PAIR] = val.astype(BF16)
            col = jnp.sum(val, axis=0, keepdims=True)

            @pl.when(first)
            def _():
                db_ref[:, off:off + PAIR] = col

            @pl.when(jnp.logical_not(first))
            def _():
                db_ref[:, off:off + PAIR] += col

        for i in range(qa // PAIR):
            emit(i * PAIR, dqa_ref[:, i * PAIR:(i + 1) * PAIR])
        for i in range(kva // PAIR):
            emit(qa + i * PAIR, dkva_ref[:, i * PAIR:(i + 1) * PAIR])
        for j, branch_refs in enumerate(((q1, q2, q3), (k1, k2, k3), (v1, v2, v3))):
            for i in range(qb // PAIR):
                sl = slice(i * PAIR, (i + 1) * PAIR)
                total = None
                for ref, dil in zip(branch_refs, DILATIONS):
                    val = _from_classes(ref, sl, perm_ref, dil)
                    total = val if total is None else total + val
                emit(qa + kva + j * qb + i * PAIR, total)

    def row(w):
        return pl.BlockSpec((tm, w), lambda i: (i, 0))

    return pl.pallas_call(
        body, name="assemble_dproj", grid=(s // tm,),
        in_specs=[row(qa), row(kva)] + [_class_spec(d, tm, qb) for d in DILATIONS] * 3,
        out_specs=[row(width), pl.BlockSpec((1, width), lambda i: (0, 0))],
        out_shape=[jax.ShapeDtypeStruct((s, width), BF16), jax.ShapeDtypeStruct((1, width), F32)],
        scratch_shapes=[pltpu.VMEM((tm, 128), F32)],
        compiler_params=_params(("arbitrary",), tm * (qa + kva + 9 * qb) * 4 + tm * width * 2),
    )(dqa, dkva, *dqs, *dks, *dvs)


def _fill_bias(bias_ref, n_pairs, max_steps, dil, slopes, sink_ref=None):
    qi = lax.broadcasted_iota(jnp.int32, (BLOCK, 2 * BLOCK), 0)
    kj = lax.broadcasted_iota(jnp.int32, (BLOCK, 2 * BLOCK), 1)
    steps = qi + BLOCK - kj
    dist = (steps * dil).astype(F32)
    band = (steps >= 0) & (steps <= max_steps)
    assert sink_ref is None or max_steps < BLOCK
    for first in (0, 1):
        valid = band & (kj >= BLOCK) if first else band
        for i in range(n_pairs):
            tables = []
            for half in (0, 1):
                table = jnp.where(valid, -(slopes[2 * i + half] * dist), NEG_INF)
                if sink_ref is not None:
                    table = jnp.where(kj == 0, sink_ref[2 * i + half], table)
                tables.append(table)
            bias_ref[first, i] = jnp.concatenate(tables, axis=0)


def _without_sink_row(tile):
    row = lax.broadcasted_iota(jnp.int32, tile.shape, 0)
    return jnp.where(row == 0, jnp.zeros_like(tile), tile)


def _bias_shape(n_pairs):
    return pltpu.VMEM((2, n_pairs, 2 * BLOCK, 2 * BLOCK), F32)


def _stack_heads(tile, low):
    zero = jnp.zeros_like(tile)
    return jnp.concatenate([jnp.where(low, tile, zero), jnp.where(low, zero, tile)], axis=0)


def _unstack_heads(stacked, low):
    return jnp.where(low, stacked[0:BLOCK], stacked[BLOCK:2 * BLOCK])


def _head_columns(ref, i):
    return jnp.concatenate([ref[:, 2 * i:2 * i + 1], ref[:, 2 * i + 1:2 * i + 2]], axis=0)


def _swap_halves(t):
    return pltpu.roll(t, HEAD_DIM, 1)


def _dup_group(t_bf16, group):
    t = t_bf16.astype(F32)
    low = lax.broadcasted_iota(jnp.int32, t.shape, 1) < HEAD_DIM
    keep = low if group == 0 else jnp.logical_not(low)
    return jnp.where(keep, t, _swap_halves(t)).astype(BF16)


def _attn_fwd(name, q, kv, *, dil, max_steps, slopes, sinks=None, hook=None):
    grouped = sinks is not None
    _, length, w = q.shape
    n_pairs = w // PAIR
    nb = length // BLOCK
    heads_per_group = 2 * n_pairs // N_KV_GROUPS

    def body(*refs):
        if grouped:
            sink_ref, q_ref, kvp_ref, kvc_ref, o_ref, lse_ref, bias_ref = refs
        else:
            q_ref, kp_ref, kc_ref, vp_ref, vc_ref, o_ref, lse_ref, bias_ref = refs
        n = pl.program_id(1)

        @pl.when((pl.program_id(0) == 0) & (n == 0))
        def _():
            _fill_bias(bias_ref, n_pairs, max_steps, dil, slopes, sink_ref if grouped else None)

        first = (n == 0).astype(jnp.int32)
        low = _low_lanes(BLOCK)
        lane = lax.broadcasted_iota(jnp.int32, (BLOCK, 128), 1)
        lse_acc = jnp.zeros((BLOCK, 128), F32)
        if grouped:
            kv_all = jnp.concatenate([kvp_ref[...], kvc_ref[...]], axis=0)
            k_dup = [_without_sink_row(_dup_group(kv_all[:, 0:PAIR], g)) for g in range(N_KV_GROUPS)]
            v_dup = [_without_sink_row(_dup_group(kv_all[:, PAIR:2 * PAIR], g)) for g in range(N_KV_GROUPS)]
        for i in range(n_pairs):
            sl = slice(i * PAIR, (i + 1) * PAIR)
            qs = _stack_heads(q_ref[:, sl] * ATT_SCALE, low)
            if grouped:
                kk, vv = k_dup[2 * i // heads_per_group], v_dup[2 * i // heads_per_group]
            else:
                kk = jnp.concatenate([kp_ref[:, sl], kc_ref[:, sl]], axis=0)
                vv = jnp.concatenate([vp_ref[:, sl], vc_ref[:, sl]], axis=0)
            sc = lax.dot_general(qs, kk, (((1,), (1,)), ((), ())), preferred_element_type=F32)
            sc = sc + bias_ref[first, i]
            m = jnp.max(sc, axis=-1, keepdims=True)
            p = jnp.exp(sc - m)
            den = jnp.sum(p, axis=-1, keepdims=True)
            o = jnp.dot(p.astype(BF16), vv, preferred_element_type=F32) / den
            o_ref[:, sl] = _unstack_heads(o, low)
            lse = m + jnp.log(den)
            lse_acc = jnp.where(lane == 2 * i, lse[0:BLOCK], lse_acc)
            lse_acc = jnp.where(lane == 2 * i + 1, lse[BLOCK:2 * BLOCK], lse_acc)
        lse_ref[...] = lse_acc

    def cur(width):
        return pl.BlockSpec((None, BLOCK, width), lambda r, n: (r, n, 0))

    def prev(width):
        return pl.BlockSpec((None, BLOCK, width), lambda r, n: (r, jnp.maximum(n - 1, 0), 0))

    if grouped:
        kvw = kv.shape[2]
        operands = [sinks, q, kv, kv]
        in_specs = [SMEM_SPEC, cur(w), prev(kvw), cur(kvw)]
    else:
        operands = [q, kv[0], kv[0], kv[1], kv[1]]
        in_specs = [cur(w), prev(w), cur(w), prev(w), cur(w)]
    res, hook_res = _call(
        body, hook, name=name, grid=(dil, nb), in_specs=in_specs,
        out_specs=[cur(w), cur(128)],
        out_shape=[jax.ShapeDtypeStruct((dil, length, w), F32),
                   jax.ShapeDtypeStruct((dil, length, 128), F32)],
        scratch_shapes=[_bias_shape(n_pairs)],
        compiler_params=_params(("arbitrary", "arbitrary"), BLOCK * w * 16 + n_pairs * BLOCK * BLOCK * 16),
    )(*operands)
    return res if hook is None else (res, hook_res)


def _attn_bwd(name, q, kv, do, lse, delta, *, dil, max_steps, slopes, sinks=None, hook=None):
    grouped = sinks is not None
    _, length, w = q.shape
    n_pairs = w // PAIR
    nb = length // BLOCK
    heads_per_group = 2 * n_pairs // N_KV_GROUPS
    pairs_per_group = n_pairs // N_KV_GROUPS

    def body(*refs):
        if grouped:
            (sink_ref, q_ref, kvp_ref, kvc_ref, do_ref, lse_ref, dl_ref,
             dq_ref, dkv_ref, dsink_ref, acc_ref, bias_ref) = refs
        else:
            (q_ref, kp_ref, kc_ref, vp_ref, vc_ref, do_ref, lse_ref, dl_ref,
             dq_ref, dk_ref, dv_ref, acck_ref, accv_ref, bias_ref) = refs
        n = pl.program_id(1)

        @pl.when((pl.program_id(0) == 0) & (n == 0))
        def _():
            _fill_bias(bias_ref, n_pairs, max_steps, dil, slopes, sink_ref if grouped else None)

        @pl.when(n == 0)
        def _():
            if grouped:
                acc_ref[...] = jnp.zeros_like(acc_ref)

                @pl.when(pl.program_id(0) == 0)
                def _():
                    dsink_ref[...] = jnp.zeros_like(dsink_ref)
            else:
                acck_ref[...] = jnp.zeros_like(acck_ref)
                accv_ref[...] = jnp.zeros_like(accv_ref)

        @pl.when(n == nb)
        def _():
            if grouped:
                dkv_ref[...] = acc_ref[...]
            else:
                dk_ref[...] = acck_ref[...]
                dv_ref[...] = accv_ref[...]

        @pl.when(n < nb)
        def _():
            first = (n == 0).astype(jnp.int32)
            low = _low_lanes(BLOCK)
            low_kv = _low_lanes(2 * BLOCK)
            lane1 = lax.broadcasted_iota(jnp.int32, (1, 128), 1)
            if grouped:
                kv_all = jnp.concatenate([kvp_ref[...], kvc_ref[...]], axis=0)
                k_dup = [_without_sink_row(_dup_group(kv_all[:, 0:PAIR], g)) for g in range(N_KV_GROUPS)]
                v_dup = [_without_sink_row(_dup_group(kv_all[:, PAIR:2 * PAIR], g)) for g in range(N_KV_GROUPS)]
                dk_grp =[jnp.zeros((2 * BLOCK, PAIR), F32) for _ in range(N_KV_GROUPS)]
                dv_grp = [jnp.zeros((2 * BLOCK, PAIR), F32) for _ in range(N_KV_GROUPS)]
                dsink = jnp.zeros((1, 128), F32)
            for i in range(n_pairs):
                sl = slice(i * PAIR, (i + 1) * PAIR)
                qs = _stack_heads(q_ref[:, sl] * ATT_SCALE, low)
                dos = _stack_heads(do_ref[:, sl], low)
                if grouped:
                    grp = 2 * i // heads_per_group
                    kk, vv = k_dup[grp], v_dup[grp]
                else:
                    kk = jnp.concatenate([kp_ref[:, sl], kc_ref[:, sl]], axis=0)
                    vv = jnp.concatenate([vp_ref[:, sl], vc_ref[:, sl]], axis=0)
                lse_col = _head_columns(lse_ref, i)
                dl_col = _head_columns(dl_ref, i)
                sc = lax.dot_general(qs, kk, (((1,), (1,)), ((), ())), preferred_element_type=F32)
                p = jnp.exp(sc + bias_ref[first, i] - lse_col)
                dp = lax.dot_general(dos, vv, (((1,), (1,)), ((), ())), preferred_element_type=F32)
                ds_f32 = p * (dp - dl_col)
                ds = ds_f32.astype(BF16)
                dq = jnp.dot(ds, kk, preferred_element_type=F32)
                dkk = lax.dot_general(ds, qs, (((0,), (0,)), ((), ())), preferred_element_type=F32)
                dvv = lax.dot_general(p.astype(BF16), dos, (((0,), (0,)), ((), ())),
                                      preferred_element_type=F32)
                if grouped:
                    for half in (0, 1):
                        contrib = jnp.sum(ds_f32[half * BLOCK:(half + 1) * BLOCK, 0:1], axis=0, keepdims=True)
                        dsink = jnp.where(lane1 == 2 * i + half, dsink + contrib, dsink)
                dq_ref[:, sl] = _unstack_heads(dq, low) * ATT_SCALE
                if grouped:
                    dk_grp[grp] = dk_grp[grp] + dkk
                    dv_grp[grp] = dv_grp[grp] + dvv
                else:
                    dk_ref[:, sl] = acck_ref[:, sl] + dkk[0:BLOCK]
                    acck_ref[:, sl] = dkk[BLOCK:2 * BLOCK]
                    dv_ref[:, sl] = accv_ref[:, sl] + dvv[0:BLOCK]
                    accv_ref[:, sl] = dvv[BLOCK:2 * BLOCK]
            if grouped:
                folded = [_without_sink_row(t + _swap_halves(t)) for t in dk_grp + dv_grp]
                dk_tile = jnp.where(low_kv, folded[0], folded[1])
                dv_tile = jnp.where(low_kv, folded[2], folded[3])
                part = jnp.concatenate([dk_tile, dv_tile], axis=1)
                dkv_ref[...] = acc_ref[...] + part[0:BLOCK]
                acc_ref[...] = part[BLOCK:2 * BLOCK]
                dsink_ref[...] += dsink

    last = nb - 1

    def cur(width):
        return pl.BlockSpec((None, BLOCK, width), lambda r, n: (r, jnp.minimum(n, last), 0))

    def prev(width):
        return pl.BlockSpec((None, BLOCK, width),
                            lambda r, n: (r, jnp.maximum(jnp.minimum(n, last) - 1, 0), 0))

    def done(width):
        return pl.BlockSpec((None, BLOCK, width), lambda r, n: (r, jnp.maximum(n - 1, 0), 0))

    if grouped:
        assert pairs_per_group * N_KV_GROUPS == n_pairs and heads_per_group % 2 == 0
        kvw = kv.shape[2]
        operands = [sinks, q, kv, kv, do, lse, delta]
        in_specs = [SMEM_SPEC, cur(w), prev(kvw), cur(kvw), cur(w), cur(128), cur(128)]
        out_specs = [cur(w), done(kvw), pl.BlockSpec((1, 128), lambda r, n: (0, 0))]
        out_shape = [jax.ShapeDtypeStruct((dil, length, w), F32), jax.ShapeDtypeStruct((dil, length, kvw), F32),
                     jax.ShapeDtypeStruct((1, 128), F32)]
        scratch = [pltpu.VMEM((BLOCK, kvw), F32), _bias_shape(n_pairs)]
    else:
        operands = [q, kv[0], kv[0], kv[1], kv[1], do, lse, delta]
        in_specs = [cur(w), prev(w), cur(w), prev(w), cur(w), cur(w), cur(128), cur(128)]
        out_specs = [cur(w), done(w), done(w)]
        out_shape = [jax.ShapeDtypeStruct((dil, length, w), F32)] * 3
        scratch = [pltpu.VMEM((BLOCK, w), F32), pltpu.VMEM((BLOCK, w), F32), _bias_shape(n_pairs)]
    res, hook_res = _call(
        body, hook, name=name, grid=(dil, nb + 1), in_specs=in_specs, out_specs=out_specs,
        out_shape=out_shape, scratch_shapes=scratch,
        compiler_params=_params(("arbitrary", "arbitrary"), BLOCK * w * 32 + n_pairs * BLOCK * BLOCK * 16),
    )(*operands)
    return res if hook is None else (res, hook_res)


def _adamw(name, w, g, m, v):
    rows, cols = w.shape
    tm = 256 if rows % 256 == 0 else rows

    def body(w_ref, g_ref, m_ref, v_ref, d_ref, nm_ref, nv_ref):
        gv = g_ref[...]
        mn = ADAM_B1 * m_ref[...] + (1.0 - ADAM_B1) * gv
        vn = ADAM_B2 * v_ref[...] + (1.0 - ADAM_B2) * (gv * gv)
        m_hat = mn / (1.0 - ADAM_B1 ** ADAM_STEP)
        v_hat = vn / (1.0 - ADAM_B2 ** ADAM_STEP)
        d_ref[...] = -ADAM_LR * (m_hat / (jnp.sqrt(v_hat) + ADAM_EPS) + ADAM_WD * w_ref[...])
        nm_ref[...] = mn
        nv_ref[...] = vn

    spec = pl.BlockSpec((tm, cols), lambda i: (i, 0))
    return pl.pallas_call(
        body, name=name, grid=(rows // tm,), in_specs=[spec] * 4, out_specs=[spec] * 3,
        out_shape=[jax.ShapeDtypeStruct(w.shape, F32)] * 3,
        compiler_params=_params(("parallel",), tm * cols * 28),
    )(w, g, m, v)


def _mesh_position():
    return lax.axis_index("x"), lax.axis_index("y"), lax.axis_index("c")


def _other_chips(x, y):
    return [(1 - x, y), (x, 1 - y), (1 - x, 1 - y)]


def _gather_hook(shard, gathered, lo, hi):
    rows, cols = shard.shape
    half, n = rows // 2, hi - lo
    assert lo % 16 == 0 and n % 16 == 0 and half % 16 == 0
    first = gathered is None

    def region(out, owner_chip, which_half):
        return out.at[pl.ds(pl.multiple_of(owner_chip * rows + which_half * half + lo, 16), n)]

    def parts(ops, outs, sems):
        x, y, c = _mesh_position()
        return ops[0], outs[0], sems, x, y, c, 2 * x + y, (x, y, 1 - c), _other_chips(x, y)

    def local_copy(src, out, chip, sem):
        return pltpu.make_async_copy(src, out.at[pl.ds(pl.multiple_of(chip * rows, 16), rows)], sem)

    def start(ops, outs, sems):
        src, out, (send, recv, fsend, frecv, local), x, y, c, chip, sibling, others = parts(ops, outs, sems)
        if first:
            local_copy(src, out, chip, local.at[0]).start()
        mine = src.at[pl.ds(pl.multiple_of(c * half + lo, 16), n)]
        for k, (px, py) in enumerate(others):
            _remote(mine, region(out, chip, c), send.at[k], recv.at[k], (px, py, c)).start()

    def mid(ops, outs, sems):
        src, out, (send, recv, fsend, frecv, local), x, y, c, chip, sibling, others = parts(ops, outs, sems)
        for k, (px, py) in enumerate(others):
            landed = region(out, 2 * px + py, c)
            _remote(landed, landed, send.at[k], recv.at[k], (px, py, c)).wait_recv()
            _remote(landed, landed, fsend.at[k], frecv.at[k], sibling).start()

    def finish(ops, outs, sems):
        src, out, (send, recv, fsend, frecv, local), x, y, c, chip, sibling, others = parts(ops, outs, sems)
        mine = src.at[pl.ds(pl.multiple_of(c * half + lo, 16), n)]
        for k, (px, py) in enumerate(others):
            passed = region(out, 2 * px + py, 1 - c)
            _remote(passed, passed, fsend.at[k], frecv.at[k], sibling).wait_recv()
        for k, (px, py) in enumerate(others):
            landed = region(out, 2 * px + py, c)
            _remote(landed, landed, fsend.at[k], frecv.at[k], sibling).wait_send()
            _remote(mine, region(out, chip, c), send.at[k], recv.at[k], (px, py, c)).wait_send()
        if first:
            local_copy(src, out, chip, local.at[0]).wait()

    sems = [pltpu.SemaphoreType.DMA((3,))] * 4 + [pltpu.SemaphoreType.DMA((1,))]
    out_shape = [jax.ShapeDtypeStruct((N_CHIPS * rows, cols), shard.dtype)]
    if first:
        return _Hook([shard], out_shape, sems, start, finish, mid)
    return _Hook([shard, gathered], out_shape, sems, start, finish, mid, aliases={1: 0})


def _exchange_hook(grad):
    rows, cols = grad.shape[0] // N_CHIPS, grad.shape[1]
    half = rows // 2
    assert half % 16 == 0

    def copies(ops, outs, sems):
        x, y, c = _mesh_position()
        send, recv = sems
        return [_remote(ops[0].at[pl.ds(pl.multiple_of(k * rows + (1 - c) * half, 16), half)], outs[0].at[k],
                        send.at[k], recv.at[k], (x, y, 1 - c)) for k in range(N_CHIPS)]

    def start(ops, outs, sems):
        for cp in copies(ops, outs, sems):
            cp.start()

    def finish(ops, outs, sems):
        for cp in copies(ops, outs, sems):
            cp.wait_recv()
            cp.wait_send()

    return _Hook([grad], [jax.ShapeDtypeStruct((N_CHIPS, half, cols), grad.dtype)],
                 [pltpu.SemaphoreType.DMA((N_CHIPS,))] * 2, start, finish)


def _scatter_hook(chip_sum):
    _, half, cols = chip_sum.shape

    def copies(ops, outs, sems):
        x, y, c = _mesh_position()
        send, recv = sems
        return [_remote(ops[0].at[2 * px + py], outs[0].at[k], send.at[k], recv.at[k], (px, py, c))
                for k, (px, py) in enumerate(_other_chips(x, y))]

    def start(ops, outs, sems):
        for cp in copies(ops, outs, sems):
            cp.start()

    def finish(ops, outs, sems):
        for cp in copies(ops, outs, sems):
            cp.wait_recv()
            cp.wait_send()

    return _Hook([chip_sum], [jax.ShapeDtypeStruct((3, half, cols), chip_sum.dtype)],
                 [pltpu.SemaphoreType.DMA((3,))] * 2, start, finish)


def _sum_tile(half):
    return 256 if half % 256 == 0 else half


def _chip_add(name, grad, from_sibling, core):
    n_chips, half, cols = from_sibling.shape
    rows = 2 * half
    tr = _sum_tile(half)

    def body(core_ref, g_ref, s_ref, o_ref):
        o_ref[...] = (g_ref[...].astype(F32) + s_ref[...].astype(F32)).astype(o_ref.dtype)

    tile = pl.BlockSpec((None, tr, cols), lambda k, i, core_ref: (k, i, 0))
    return pl.pallas_call(
        body, name=name,
        grid_spec=pltpu.PrefetchScalarGridSpec(
            num_scalar_prefetch=1, grid=(n_chips, half // tr),
            in_specs=[pl.BlockSpec((tr, cols), lambda k, i, core_ref:
                                   (k * (rows // tr) + core_ref[0] * (half // tr) + i, 0)), tile],
            out_specs=tile),
        out_shape=jax.ShapeDtypeStruct(from_sibling.shape, from_sibling.dtype),
        compiler_params=_params(("parallel", "parallel"), 3 * tr * cols * 4),
    )(core, grad, from_sibling)


def _final_add(name, chip_sum, from_chips, chip):
    _, half, cols = chip_sum.shape
    tr = _sum_tile(half)

    def body(chip_ref, own_ref, others_ref, o_ref):
        total = own_ref[...].astype(F32)
        for k in range(3):
            total = total + others_ref[k].astype(F32)
        o_ref[...] = total

    return pl.pallas_call(
        body, name=name,
        grid_spec=pltpu.PrefetchScalarGridSpec(
            num_scalar_prefetch=1, grid=(half // tr,),
            in_specs=[pl.BlockSpec((None, tr, cols), lambda i, chip_ref: (chip_ref[0], i, 0)),
                      pl.BlockSpec((3, tr, cols), lambda i, chip_ref: (0, i, 0))],
            out_specs=pl.BlockSpec((tr, cols), lambda i, chip_ref: (i, 0))),
        out_shape=jax.ShapeDtypeStruct((half, cols), F32),
        compiler_params=_params(("parallel",), 6 * tr * cols * 4),
    )(chip, chip_sum, from_chips)


FINISH_CHUNK_ROWS = 256


def _finish_reduction(chip_sums, shares, small):
    n_w, n_s = len(chip_sums), len(shares)
    halves = [cs.shape[1] for cs in chip_sums]
    cols = shares[0].shape[1] if n_s else chip_sums[0].shape[2]
    wire = chip_sums[0].dtype if n_w else GRAD_WIRE_DTYPE
    rows_s = small.shape[0]
    ch = FINISH_CHUNK_ROWS

    def body(*refs):
        sums_in, small_ref, share_in = refs[:n_w], refs[n_w], refs[n_w + 1:n_w + 1 + n_s]
        o0 = n_w + 1 + n_s
        outs, small_out, share_out = refs[o0:o0 + n_w], refs[o0 + n_w], refs[o0 + n_w + 1:o0 + n_w + 1 + n_s]
        arrived = refs[o0 + n_w + 1 + n_s:o0 + 2 * n_w + 1 + n_s]
        (small_all, buf_in, buf_out, chip_send, chip_recv, fin_send, fin_recv, small_send, small_recv,
         share_send, share_recv, share_local, io_sem) = refs[o0 + 2 * n_w + 1 + n_s:]
        x, y, c = _mesh_position()
        chip = 2 * x + y
        me = 4 * x + 2 * y + c
        sibling = (x, y, 1 - c)
        others = _other_chips(x, y)
        pending, local = [], []

        for w in range(n_w):
            for k, (px, py) in enumerate(others):
                cp = _remote(sums_in[w].at[2 * px + py], arrived[w].at[k], chip_send.at[w, k],
                             chip_recv.at[w, k], (px, py, c))
                cp.start()
                pending.append(cp)
        small_all[me] = small_ref[...]
        for j in range(N_DEV - 1):
            peer = (me + 1 + j) % N_DEV
            cp = _remote(small_all.at[me], small_all.at[me], small_send.at[j], small_recv.at[j],
                         (peer // 4, (peer // 2) % 2, peer % 2))
            cp.start()
            pending.append(cp)

        def halves_of(out, rows):
            return [out.at[pl.ds(pl.multiple_of(which * rows, 16), rows)] for which in (c, 1 - c)]

        for i in range(n_s):
            mine, _ = halves_of(share_out[i], share_in[i].shape[0])
            cp = pltpu.make_async_copy(share_in[i], mine, share_local.at[i])
            cp.start()
            local.append(cp)
            cp = _remote(share_in[i], mine, share_send.at[i], share_recv.at[i], sibling)
            cp.start()
            pending.append(cp)

        def add_chunk(w, dst, start, size):
            total = None
            for src in [sums_in[w].at[chip]] + [arrived[w].at[k] for k in range(3)]:
                cp = pltpu.make_async_copy(src.at[pl.ds(start, size)], buf_in.at[pl.ds(0, size)], io_sem)
                cp.start()
                cp.wait()
                val = buf_in[pl.ds(0, size), :].astype(F32)
                total = val if total is None else total + val
            buf_out[pl.ds(0, size), :] = total
            cp = pltpu.make_async_copy(buf_out.at[pl.ds(0, size)], dst.at[pl.ds(start, size)], io_sem)
            cp.start()
            cp.wait()

        for w in range(n_w):
            for k, (px, py) in enumerate(others):
                _remote(sums_in[w].at[chip], arrived[w].at[k], chip_send.at[w, k], chip_recv.at[w, k],
                        (px, py, c)).wait_recv()
            mine, _ = halves_of(outs[w], halves[w])
            n_full = halves[w] // ch

            def loop_body(i, carry, w=w, mine=mine):
                add_chunk(w, mine, pl.multiple_of(i * ch, ch), ch)
                return carry

            lax.fori_loop(0, n_full, loop_body, 0)
            if halves[w] % ch:
                add_chunk(w, mine, n_full * ch, halves[w] - n_full * ch)
            cp = _remote(mine, mine, fin_send.at[w], fin_recv.at[w], sibling)
            cp.start()
            pending.append(cp)
        for w in range(n_w):
            _, theirs = halves_of(outs[w], halves[w])
            _remote(theirs, theirs, fin_send.at[w], fin_recv.at[w], sibling).wait_recv()
        for i in range(n_s):
            _, theirs = halves_of(share_out[i], share_in[i].shape[0])
            _remote(share_in[i], theirs, share_send.at[i], share_recv.at[i], sibling).wait_recv()

        for j in range(N_DEV - 1):
            peer = (me + N_DEV - 1 - j) % N_DEV
            _remote(small_all.at[peer], small_all.at[peer], small_send.at[j], small_recv.at[j],
                    sibling).wait_recv()
        total = small_all[0]
        for dev in range(1, N_DEV):
            total = total + small_all[dev]
        small_out[...] = total
        for cp in pending:
            cp.wait_send()
        for cp in local:
            cp.wait()

    def sems(n):
        return pltpu.SemaphoreType.DMA((max(n, 1),))

    out_shape = ([jax.ShapeDtypeStruct((2 * h, cols), F32) for h in halves]
                 + [jax.ShapeDtypeStruct((rows_s, 128), F32)]
                 + [jax.ShapeDtypeStruct((2 * sh.shape[0], cols), F32) for sh in shares]
                 + [jax.ShapeDtypeStruct((3, h, cols), wire) for h in halves])
    res = pl.pallas_call(
        body, name="finish_reduction",
        in_specs=[HBM_SPEC] * n_w + [VMEM_SPEC] + [HBM_SPEC] * n_s,
        out_specs=[HBM_SPEC] * n_w + [VMEM_SPEC] + [HBM_SPEC] * (n_s + n_w),
        out_shape=out_shape,
        scratch_shapes=[
            pltpu.VMEM((N_DEV, rows_s, 128), F32), pltpu.VMEM((ch, cols), wire), pltpu.VMEM((ch, cols), F32),
            pltpu.SemaphoreType.DMA((max(n_w, 1), 3)), pltpu.SemaphoreType.DMA((max(n_w, 1), 3)),
            sems(n_w), sems(n_w), sems(N_DEV - 1), sems(N_DEV - 1), sems(n_s), sems(n_s), sems(n_s),
            pltpu.SemaphoreType.DMA,
        ],
        compiler_params=pltpu.CompilerParams(vmem_limit_bytes=VMEM_LIMIT_MIN),
    )(*chip_sums, small, *shares)
    return res[:n_w], res[n_w + 1:n_w + 1 + n_s], res[n_w]


def _gather_weights(shards):
    n_w = len(shards)
    halves = [s.shape[0] // 2 for s in shards]

    def body(*refs):
        ins, outs = refs[:n_w], refs[n_w:2 * n_w]
        local_sems, send_sems, recv_sems, fsend_sems, frecv_sems = refs[2 * n_w:]
        x, y, c = _mesh_position()
        chip = 2 * x + y
        sibling = (x, y, 1 - c)
        others = _other_chips(x, y)

        def region(w, owner_chip, half):
            start = owner_chip * (2 * halves[w]) + half * halves[w]
            return outs[w].at[pl.ds(pl.multiple_of(start, 16), halves[w])]

        def remote(src, dst, ssem, rsem, dev):
            return pltpu.make_async_remote_copy(src_ref=src, dst_ref=dst, send_sem=ssem, recv_sem=rsem,
                                                device_id=dev, device_id_type=MESH)

        local = []
        for w in range(n_w):
            cp = pltpu.make_async_copy(ins[w], outs[w].at[pl.ds(pl.multiple_of(chip * 2 * halves[w], 16),
                                                                2 * halves[w])], local_sems.at[w])
            cp.start()
            local.append(cp)
        sends = []
        for w in range(n_w):
            mine = ins[w].at[pl.ds(pl.multiple_of(c * halves[w], 16), halves[w])]
            for k, (px, py) in enumerate(others):
                cp = remote(mine, region(w, chip, c), send_sems.at[w, k], recv_sems.at[w, k], (px, py, c))
                cp.start()
                sends.append(cp)
        for k, (px, py) in enumerate(others):
            for w in range(n_w):
                landed = region(w, 2 * px + py, c)
                remote(landed, landed, send_sems.at[w, k], recv_sems.at[w, k], (px, py, c)).wait_recv()
                cp = remote(landed, landed, fsend_sems.at[w, k], frecv_sems.at[w, k], sibling)
                cp.start()
                sends.append(cp)
        for k, (px, py) in enumerate(others):
            for w in range(n_w):
                passed = region(w, 2 * px + py, 1 - c)
                remote(passed, passed, fsend_sems.at[w, k], frecv_sems.at[w, k], sibling).wait_recv()
        for cp in sends:
            cp.wait_send()
        for cp in local:
            cp.wait()

    return pl.pallas_call(
        body, name="gather_weights",
        in_specs=[HBM_SPEC] * n_w, out_specs=[HBM_SPEC] * n_w,
        out_shape=[jax.ShapeDtypeStruct((N_CHIPS * s.shape[0], s.shape[1]), s.dtype) for s in shards],
        scratch_shapes=[pltpu.SemaphoreType.DMA((n_w,))] + [pltpu.SemaphoreType.DMA((n_w, 3))] * 4,
    )(*shards)


REDUCE_CHUNK_ROWS = 256


def _reduce_gradients(grads, shares, small):
    n_w = len(grads)
    rows = [g.shape[0] // N_CHIPS for g in grads]
    halves = [r // 2 for r in rows]
    cols = grads[0].shape[1]
    wire = grads[0].dtype
    ch = REDUCE_CHUNK_ROWS
    for h in halves:
        assert h % 16 == 0
    rows_s = small.shape[0]
    n_s = len(shares)

    def body(*refs):
        g_in = refs[:n_w]
        small_ref = refs[n_w]
        share_in = refs[n_w + 1:n_w + 1 + n_s]
        refs = refs[:n_w + 1] + refs[n_w + 1 + n_s:]
        outs = refs[n_w + 1:2 * n_w + 1]
        small_out = refs[2 * n_w + 1]
        from_sib = refs[2 * n_w + 2:3 * n_w + 2]
        chip_sum = refs[3 * n_w + 2:4 * n_w + 2]
        from_chips = refs[4 * n_w + 2:5 * n_w + 2]
        share_out = refs[5 * n_w + 2:5 * n_w + 2 + n_s]
        (small_all, buf_a, buf_b, buf_o, sib_send, sib_recv, chip_send, chip_recv,
         fin_send, fin_recv, small_send, small_recv, io_sem,
         share_send, share_recv, share_local) = refs[5 * n_w + 2 + n_s:]
        x, y, c = _mesh_position()
        chip = 2 * x + y
        me = 4 * x + 2 * y + c
        sibling = (x, y, 1 - c)
        others = _other_chips(x, y)

        def remote(src, dst, ssem, rsem, dev):
            return pltpu.make_async_remote_copy(src_ref=src, dst_ref=dst, send_sem=ssem, recv_sem=rsem,
                                                device_id=dev, device_id_type=MESH)

        def part(w, owner_chip, half):
            start = owner_chip * rows[w] + half * halves[w]
            return g_in[w].at[pl.ds(pl.multiple_of(start, 16), halves[w])]

        pending = []
        small_all[me] = small_ref[...]
        for j in range(N_DEV - 1):
            peer = (me + 1 + j) % N_DEV
            cp = remote(small_all.at[me], small_all.at[me], small_send.at[j], small_recv.at[j],
                        (peer // 4, (peer // 2) % 2, peer % 2))
            cp.start()
            pending.append(cp)

        local = []
        for i in range(n_s):
            half_rows = share_in[i].shape[0]
            place = share_out[i].at[pl.ds(pl.multiple_of(c * half_rows, 16), half_rows)]
            cp = pltpu.make_async_copy(share_in[i], place, share_local.at[i])
            cp.start()
            local.append(cp)
            cp = remote(share_in[i], place, share_send.at[i], share_recv.at[i], sibling)
            cp.start()
            pending.append(cp)

        for w in range(n_w):
            for k in range(N_CHIPS):
                cp = remote(part(w, k, 1 - c), from_sib[w].at[k], sib_send.at[w, k], sib_recv.at[w, k], sibling)
                cp.start()
                pending.append(cp)

        def add_stream(w, srcs, dst, n_rows):
            def chunk(start, size):
                total = None
                for i, src in enumerate(srcs):
                    buf = buf_a if i % 2 == 0 else buf_b
                    cp = pltpu.make_async_copy(src.at[pl.ds(start, size)], buf.at[pl.ds(0, size)], io_sem)
                    cp.start()
                    cp.wait()
                    val = buf[pl.ds(0, size), :].astype(F32)
                    total = val if total is None else total + val
                return total

            n_full = n_rows // ch
            rem = n_rows - n_full * ch

            def store(total, start, size):
                if dst.dtype == F32:
                    buf_o[pl.ds(0, size), :] = total
                    cp = pltpu.make_async_copy(buf_o.at[pl.ds(0, size)], dst.at[pl.ds(start, size)], io_sem)
                else:
                    buf_a[pl.ds(0, size), :] = total.astype(buf_a.dtype)
                    cp = pltpu.make_async_copy(buf_a.at[pl.ds(0, size)], dst.at[pl.ds(start, size)], io_sem)
                cp.start()
                cp.wait()

            def loop_body(i, carry):
                start = pl.multiple_of(i * ch, ch)
                store(chunk(start, ch), start, ch)
                return carry

            lax.fori_loop(0, n_full, loop_body, 0)
            if rem:
                store(chunk(n_full * ch, rem), n_full * ch, rem)

        order = [2, 0, 1]
        for w in range(n_w):
            for k in range(N_CHIPS):
                remote(part(w, k, 1 - c), from_sib[w].at[k], sib_send.at[w, k], sib_recv.at[w, k],
                       sibling).wait_recv()
        for k in order:
            px, py = others[k]
            owner = 2 * px + py
            for w in range(n_w):
                add_stream(w, [part(w, owner, c), from_sib[w].at[owner]], chip_sum[w].at[owner], halves[w])
                cp = remote(chip_sum[w].at[owner], from_chips[w].at[k], chip_send.at[w, k],
                            chip_recv.at[w, k], (px, py, c))
                cp.start()
                pending.append(cp)
        for w in range(n_w):
            add_stream(w, [part(w, chip, c), from_sib[w].at[chip]], chip_sum[w].at[chip], halves[w])

        for w in range(n_w):
            for k in range(3):
                px, py = others[k]
                remote(chip_sum[w].at[chip], from_chips[w].at[k], chip_send.at[w, k], chip_recv.at[w, k],
                       (px, py, c)).wait_recv()
            mine = outs[w].at[pl.ds(pl.multiple_of(c * halves[w], 16), halves[w])]
            add_stream(w, [chip_sum[w].at[chip], from_chips[w].at[0], from_chips[w].at[1],
                           from_chips[w].at[2]], mine, halves[w])
            cp = remote(mine, mine, fin_send.at[w], fin_recv.at[w], sibling)
            cp.start()
            pending.append(cp)
        for w in range(n_w):
            theirs = outs[w].at[pl.ds(pl.multiple_of((1 - c) * halves[w], 16), halves[w])]
            remote(theirs, theirs, fin_send.at[w], fin_recv.at[w], sibling).wait_recv()

        for j in range(N_DEV - 1):
            peer = (me + N_DEV - 1 - j) % N_DEV
            remote(small_all.at[peer], small_all.at[peer], small_send.at[j], small_recv.at[j],
                   sibling).wait_recv()
        total = small_all[0]
        for d in range(1, N_DEV):
            total = total + small_all[d]
        small_out[...] = total
        for i in range(n_s):
            half_rows = share_in[i].shape[0]
            theirs = share_out[i].at[pl.ds(pl.multiple_of((1 - c) * half_rows, 16), half_rows)]
            remote(share_in[i], theirs, share_send.at[i], share_recv.at[i], sibling).wait_recv()
        for cp in pending:
            cp.wait_send()
        for cp in local:
            cp.wait()

    hbm_scratch = ([jax.ShapeDtypeStruct((N_CHIPS, h, cols), wire) for h in halves] * 2
                   + [jax.ShapeDtypeStruct((3, h, cols), wire) for h in halves])
    out_shape = ([jax.ShapeDtypeStruct((r, cols), F32) for r in rows]
                 + [jax.ShapeDtypeStruct((rows_s, 128), F32)] + hbm_scratch
                 + [jax.ShapeDtypeStruct((2 * sh.shape[0], sh.shape[1]), F32) for sh in shares])
    res = pl.pallas_call(
        body, name="reduce_gradients",
        in_specs=[HBM_SPEC] * n_w + [VMEM_SPEC] + [HBM_SPEC] * n_s,
        out_specs=[HBM_SPEC] * n_w + [VMEM_SPEC] + [HBM_SPEC] * (3 * n_w + n_s),
        out_shape=out_shape,
        scratch_shapes=[
            pltpu.VMEM((N_DEV, rows_s, 128), F32),
            pltpu.VMEM((ch, cols), wire), pltpu.VMEM((ch, cols), wire), pltpu.VMEM((ch, cols), F32),
            pltpu.SemaphoreType.DMA((n_w, N_CHIPS)), pltpu.SemaphoreType.DMA((n_w, N_CHIPS)),
            pltpu.SemaphoreType.DMA((n_w, 3)), pltpu.SemaphoreType.DMA((n_w, 3)),
            pltpu.SemaphoreType.DMA((n_w,)), pltpu.SemaphoreType.DMA((n_w,)),
            pltpu.SemaphoreType.DMA((N_DEV - 1,)), pltpu.SemaphoreType.DMA((N_DEV - 1,)),
            pltpu.SemaphoreType.DMA,
            pltpu.SemaphoreType.DMA((max(n_s, 1),)), pltpu.SemaphoreType.DMA((max(n_s, 1),)),
            pltpu.SemaphoreType.DMA((max(n_s, 1),)),
        ],
        compiler_params=pltpu.CompilerParams(vmem_limit_bytes=VMEM_LIMIT_MIN),
    )(*grads, small, *shares)
    return res[:n_w], res[n_w], res[len(res) - n_s:] if n_s else []


def _pack_small(parts, rows):
    flat = jnp.concatenate([p.reshape(-1) for p in parts])
    flat = jnp.pad(flat, (0, rows * 128 - flat.shape[0]))
    return flat.reshape(rows, 128)


def _unpack_small(packed, shapes):
    flat = packed.reshape(-1)
    out, off = [], 0
    for shp in shapes:
        n = int(np.prod(shp))
        out.append(flat[off:off + n].reshape(shp))
        off += n
    return out


def kernel(x, g_attn, w_in, b_in, sinks_a, g_out_a, g_out_b, w_out, g_mlp, w_1, w_2, g_final, loss_target, m_g_attn, m_w_in, m_b_in, m_sinks_a, m_g_out_a, m_g_out_b, m_w_out, m_g_mlp, m_w_1, m_w_2, m_g_final, v_g_attn, v_w_in, v_b_in, v_sinks_a, v_g_out_a, v_g_out_b, v_w_out, v_g_mlp, v_w_1, v_w_2, v_g_final):
    s, d = x.shape[1], x.shape[2]
    d_in = b_in.shape[1]
    qa = g_out_a.shape[1]
    qb = g_out_b.shape[1]
    kva = 2 * N_KV_GROUPS * HEAD_DIM
    assert d_in == qa + kva + 3 * qb and qa + qb == w_out.shape[1] * N_CHIPS
    d_ff = w_1.shape[2] * N_CHIPS
    ff_shard = w_1.shape[2]
    in_shard = w_in.shape[2]
    n_heads_a, n_heads_b = qa // HEAD_DIM, qb // HEAD_DIM
    slopes_a, slopes_b = alibi_slopes(n_heads_a), alibi_slopes(n_heads_b)

    x2d = x[0]
    target = loss_target[0]

    shards = [w_in[0].T.astype(BF16), w_out[0].astype(BF16), w_1[0].astype(BF16), w_2[0].astype(BF16)]
    core_index = lax.axis_index("c").astype(jnp.int32).reshape(1)
    chip_index = (2 * lax.axis_index("x") + lax.axis_index("y")).astype(jnp.int32).reshape(1)

    tm = _tile(s, 1024)

    (h1, r1), (w_in_t,) = _norm_fwd("norm_attn", x2d, g_attn,
                                    hook=_gather_hook(shards[0], None, 0, shards[0].shape[0] // 2))

    q_a, = _project_by_class("proj_qa", h1, w_in_t, b_in, 0, qa, (1,))
    kv_a, = _project_by_class("proj_kva", h1, w_in_t, b_in, qa, kva, (1,))
    q_bs, (w_out_g,) = _project_by_class("proj_qb", h1, w_in_t, b_in, qa + kva, qb, DILATIONS,
                                         hook=_gather_hook(shards[1], None, 0, shards[1].shape[0] // 2))
    k_bs = _project_by_class("proj_kb", h1, w_in_t, b_in, qa + kva + qb, qb, DILATIONS)
    v_bs = _project_by_class("proj_vb", h1, w_in_t, b_in, qa + kva + 2 * qb, qb, DILATIONS)

    quarter = shards[2].shape[0] // 8
    sinks = sinks_a.reshape(-1)
    (o_a, lse_a), (w_1_g,) = _attn_fwd("attn_a_fwd", q_a, kv_a, dil=1, max_steps=WINDOW_A - 1, slopes=slopes_a,
                                       sinks=sinks, hook=_gather_hook(shards[2], None, 0, quarter))
    o_a = o_a[0]
    o_bs, lse_bs = [], []
    for n, (window, dil) in enumerate(DILATED_BRANCHES):
        (o, l), (w_1_g,) = _attn_fwd(f"attn_b{dil}_fwd", q_bs[n], (k_bs[n], v_bs[n]), dil=dil,
                                     max_steps=window // dil, slopes=slopes_b,
                                     hook=_gather_hook(shards[2], w_1_g, (n + 1) * quarter, (n + 2) * quarter))
        o_bs.append(o)
        lse_bs.append(l)
    w_1_g = w_1_g.reshape(N_CHIPS, d, ff_shard)
    mix, o_b, *lse_tot, r_a, r_b = _mix_fwd(o_a, o_bs, lse_bs, g_out_a, g_out_b)

    tn = _tile(d, 512)
    a_spec, b_spec = _mm_specs("nn", tm, tn, d)
    tile_mn = pl.BlockSpec((tm, tn), lambda i, j, k: (i, j))
    x2 = _matmul("out_proj", mix, w_out_g, [x2d], mode="nn", grid=(s // tm, d // tn, 1),
                 a_spec=a_spec, b_spec=b_spec, extra_specs=[tile_mn],
                 out_shapes=[jax.ShapeDtypeStruct((s, d), F32)], out_specs=[tile_mn],
                 epilogue=lambda acc, res: (acc + res,))[0]

    h2, r2 = _norm_fwd("norm_mlp", x2, g_mlp)

    tn = _tile(ff_shard, 512)
    per = ff_shard // tn
    a_spec, _ = _mm_specs("nn", tm, tn, d)
    (u,), (w_2_g,) = _matmul(
        "mlp_up", h2, w_1_g, [], mode="nn", grid=(s // tm, d_ff // tn, 1),
        a_spec=a_spec, b_spec=pl.BlockSpec((None, d, tn), lambda i, j, k: (j // per, 0, j % per)),
        extra_specs=[], out_shapes=[jax.ShapeDtypeStruct((s, d_ff), BF16)], out_specs=[tile_mn],
        epilogue=lambda acc: (jnp.maximum(acc, 0.0),),
        hook=_gather_hook(shards[3], None, 0, shards[3].shape[0] // 2))

    tn = _tile(d, 1024)
    tk = _tile(d_ff, 2048)
    a_spec, b_spec = _mm_specs("nn", tm, tn, tk)
    tile_mn = pl.BlockSpec((tm, tn), lambda i, j, k: (i, j))
    x3 = _matmul("mlp_down", u, w_2_g, [x2], mode="nn", grid=(s // tm, d // tn, d_ff // tk),
                 a_spec=a_spec, b_spec=b_spec, extra_specs=[tile_mn],
                 out_shapes=[jax.ShapeDtypeStruct((s, d), F32)], out_specs=[tile_mn],
                 prologue=lambda a: a * a, epilogue=lambda acc, res: (acc + res,), acc_shape=(tm, tn))[0]

    dx3, dx3b, loss_part, dg_final = _loss_head(x3, target, g_final.reshape(1, d))

    tn = _tile(d_ff, 512)
    a_spec, b_spec = _mm_specs("nt", tm, tn, d)
    tile_mn = pl.BlockSpec((tm, tn), lambda i, j, k: (i, j))
    dpre = _matmul("mlp_down_dx", dx3b, w_2_g, [u], mode="nt", grid=(s // tm, d_ff // tn, 1),
                   a_spec=a_spec, b_spec=b_spec, extra_specs=[tile_mn],
                   out_shapes=[jax.ShapeDtypeStruct((s, d_ff), BF16)], out_specs=[tile_mn],
                   epilogue=lambda acc, uu: (acc * (2.0 * uu.astype(F32)),))[0]

    wire = GRAD_WIRE_DTYPE
    tk_s = _tile(s, 2048)
    tmw = _tile(d_ff, 1024)
    a_spec, b_spec = _mm_specs("tn", tmw, d, tk_s)
    dw_2 = _matmul("mlp_down_dw", u, dx3b, [], mode="tn", grid=(d_ff // tmw, 1, s // tk_s),
                   a_spec=a_spec, b_spec=b_spec, extra_specs=[],
                   out_shapes=[jax.ShapeDtypeStruct((d_ff, d), wire)],
                   out_specs=[pl.BlockSpec((tmw, d), lambda i, j, k: (i, j))],
                   prologue=lambda a: a * a, epilogue=lambda acc: (acc,), acc_shape=(tmw, d))[0]

    tn = _tile(d, 1024)
    tk = _tile(ff_shard, 2048)
    per = ff_shard // tk
    a_spec, _ = _mm_specs("nt", tm, tn, tk)
    tile_mn = pl.BlockSpec((tm, tn), lambda i, j, k: (i, j))
    dh2 = _matmul("mlp_up_dx", dpre, w_1_g, [], mode="nt", grid=(s // tm, d // tn, d_ff // tk),
                  a_spec=a_spec, b_spec=pl.BlockSpec((None, tn, tk), lambda i, j, k: (k // per, j, k % per)),
                  extra_specs=[], out_shapes=[jax.ShapeDtypeStruct((s, d), F32)], out_specs=[tile_mn],
                  epilogue=lambda acc: (acc,), acc_shape=(tm, tn))[0]

    tmw = _tile(d, 1024)
    tnw = _tile(ff_shard, 2048)
    per = ff_shard // tnw
    a_spec, b_spec = _mm_specs("tn", tmw, tnw, tk_s)
    dw_1 = _matmul("mlp_up_dw", h2, dpre, [], mode="tn", grid=(d // tmw, d_ff // tnw, s // tk_s),
                   a_spec=a_spec, b_spec=b_spec, extra_specs=[],
                   out_shapes=[jax.ShapeDtypeStruct((N_CHIPS, d, ff_shard), wire)],
                   out_specs=[pl.BlockSpec((None, tmw, tnw), lambda i, j, k: (j // per, i, j % per))],
                   epilogue=lambda acc: (acc,), acc_shape=(tmw, tnw))[0]

    dw_1 = dw_1.reshape(N_CHIPS * d, ff_shard)
    (dx2, dx2b, dg_mlp), (sib_2, sib_1) = _norm_bwd(
        "norm_mlp_bwd", dh2, x2, r2, g_mlp, dx3, hook=_merge_hooks([_exchange_hook(dw_2), _exchange_hook(dw_1)]))
    chip_sum_2 = _chip_add("chip_add_w_2", dw_2, sib_2, core_index)
    chip_sum_1 = _chip_add("chip_add_w_1", dw_1, sib_1, core_index)

    tn = _tile(d, 512)
    a_spec, b_spec = _mm_specs("nt", tm, tn, d)
    tile_mn = pl.BlockSpec((tm, tn), lambda i, j, k: (i, j))
    dmix = _matmul("out_proj_dx", dx2b, w_out_g, [], mode="nt", grid=(s // tm, d // tn, 1),
                   a_spec=a_spec, b_spec=b_spec, extra_specs=[],
                   out_shapes=[jax.ShapeDtypeStruct((s, d), F32)], out_specs=[tile_mn],
                   epilogue=lambda acc: (acc,))[0]

    tmw = _tile(d, 1024)
    a_spec, b_spec = _mm_specs("tn", tmw, d, tk_s)
    dw_out = _matmul("out_proj_dw", mix, dx2b, [], mode="tn", grid=(d // tmw, 1, s // tk_s),
                     a_spec=a_spec, b_spec=b_spec, extra_specs=[],
                     out_shapes=[jax.ShapeDtypeStruct((d, d), wire)],
                     out_specs=[pl.BlockSpec((tmw, d), lambda i, j, k: (i, j))],
                     epilogue=lambda acc: (acc,), acc_shape=(tmw, d))[0]

    mix_grads, (sib_out,) = _mix_bwd(dmix, o_a, o_b, r_a, r_b, g_out_a, g_out_b, hook=_exchange_hook(dw_out))
    do_a, do_bs, delta_a, delta_bs = mix_grads[0], mix_grads[1:4], mix_grads[4], mix_grads[5:8]
    dg_out_a, dg_out_b = mix_grads[8:]
    chip_sum_out = _chip_add("chip_add_w_out", dw_out, sib_out, core_index)

    (dq_a, dkv_a, dsinks), (chips_2,) = _attn_bwd(
        "attn_a_bwd", q_a, kv_a, do_a[None], lse_a, delta_a[None], dil=1, max_steps=WINDOW_A - 1,
        slopes=slopes_a, sinks=sinks, hook=_scatter_hook(chip_sum_2))
    dqs, dks, dvs = [], [], []
    scatter = {1: chip_sum_1, 4: chip_sum_out}
    arrived = {}
    for n, (window, dil) in enumerate(DILATED_BRANCHES):
        res = _attn_bwd(f"attn_b{dil}_bwd", q_bs[n], (k_bs[n], v_bs[n]), do_bs[n], lse_tot[n],
                        delta_bs[n], dil=dil, max_steps=window // dil, slopes=slopes_b,
                        hook=_scatter_hook(scatter[dil]) if dil in scatter else None)
        if dil in scatter:
            res, (arrived[dil],) = res
        dq, dk, dv = res
        dqs.append(dq)
        dks.append(dk)
        dvs.append(dv)
    half_2 = _final_add("final_add_w_2", chip_sum_2, chips_2, chip_index)
    half_1 = _final_add("final_add_w_1", chip_sum_1, arrived[1], chip_index)
    half_out = _final_add("final_add_w_out", chip_sum_out, arrived[4], chip_index)
    dproj, db_in = _assemble_dproj(dq_a[0], dkv_a[0], dqs, dks, dvs)

    tmw = d_in // 2 if (d_in // 2) % 128 == 0 else d_in
    tnw = _tile(d, 1024)
    tk_s = _tile(s, 1024)
    a_spec, b_spec = _mm_specs("tn", tmw, tnw, tk_s)
    dw_in_t = _matmul("in_proj_dw", dproj, h1, [], mode="tn", grid=(d_in // tmw, d // tnw, s // tk_s),
                      a_spec=a_spec, b_spec=b_spec, extra_specs=[],
                      out_shapes=[jax.ShapeDtypeStruct((d_in, d), wire)],
                      out_specs=[pl.BlockSpec((tmw, tnw), lambda i, j, k: (i, j))],
                      epilogue=lambda acc: (acc,), acc_shape=(tmw, tnw))[0]

    tn = _tile(d, 512)
    a_spec, b_spec = _mm_specs("nn", tm, tn, d_in)
    tile_mn = pl.BlockSpec((tm, tn), lambda i, j, k: (i, j))
    (dh1,), (sib_in,) = _matmul("in_proj_dx", dproj, w_in_t, [], mode="nn", grid=(s // tm, d // tn, 1),
                                a_spec=a_spec, b_spec=b_spec, extra_specs=[],
                                out_shapes=[jax.ShapeDtypeStruct((s, d), F32)], out_specs=[tile_mn],
                                epilogue=lambda acc: (acc,), hook=_exchange_hook(dw_in_t))
    chip_sum_in = _chip_add("chip_add_w_in", dw_in_t, sib_in, core_index)

    grad_x, _, dg_attn = _norm_bwd("norm_attn_bwd", dh1, x2d, r1, g_attn, dx2)

    small_parts = [dg_attn, db_in, dsinks[:, :n_heads_a], dg_out_a, dg_out_b, dg_mlp, dg_final]
    small_shapes = [g_attn.shape, b_in.shape, sinks_a.shape, g_out_a.shape, g_out_b.shape, g_mlp.shape,
                    g_final.shape]
    n_small = sum(int(np.prod(shp)) for shp in small_shapes)
    rows_s = -(-n_small // (8 * 128)) * 8
    (gw_in_t,), (gw_out, gw_1, gw_2), small_sum = _finish_reduction(
        [chip_sum_in], [half_out, half_1, half_2], _pack_small(small_parts, rows_s))
    gw_in = gw_in_t.T
    g_small = _unpack_small(small_sum, small_shapes)

    upd_in = _adamw("adamw_w_in", w_in[0], gw_in, m_w_in[0], v_w_in[0])
    upd_out = _adamw("adamw_w_out", w_out[0], gw_out, m_w_out[0], v_w_out[0])
    upd_1 = _adamw("adamw_w_1", w_1[0], gw_1, m_w_1[0], v_w_1[0])
    upd_2 = _adamw("adamw_w_2", w_2[0], gw_2, m_w_2[0], v_w_2[0])
    small_w = [g_attn, b_in, sinks_a, g_out_a, g_out_b, g_mlp, g_final]
    small_m = [m_g_attn, m_b_in, m_sinks_a, m_g_out_a, m_g_out_b, m_g_mlp, m_g_final]
    small_v = [v_g_attn, v_b_in, v_sinks_a, v_g_out_a, v_g_out_b, v_g_mlp, v_g_final]
    upd_small = _adamw("adamw_small", _pack_small(small_w, rows_s), small_sum,
                       _pack_small(small_m, rows_s), _pack_small(small_v, rows_s))
    d_small, m_small, v_small = [_unpack_small(t, small_shapes) for t in upd_small]

    loss = lax.psum(loss_part[0, 0], ("x", "y", "c"))

    def ordered(small, big):
        w_in_v, w_out_v, w_1_v, w_2_v = big
        return [small[0], w_in_v[None], small[1], small[2], small[3], small[4], w_out_v[None], small[5],
                w_1_v[None], w_2_v[None], small[6]]

    grads = ordered(g_small, (gw_in, gw_out, gw_1, gw_2))
    deltas = ordered(d_small, (upd_in[0], upd_out[0], upd_1[0], upd_2[0]))
    new_m = ordered(m_small, (upd_in[1], upd_out[1], upd_1[1], upd_2[1]))
    new_v = ordered(v_small, (upd_in[2], upd_out[2], upd_1[2], upd_2[2]))
    return (loss, grad_x[None], *grads, *deltas, *new_m, *new_v)
```

```python
import functools

import jax
import jax.numpy as jnp
import numpy as np
from jax import lax
from jax.experimental import pallas as pl
from jax.experimental.pallas import tpu as pltpu

F32 = jnp.float32
BF16 = jnp.bfloat16

HEAD_DIM = 64
BLOCK = 128
PAIR = 2 * HEAD_DIM
N_KV_GROUPS = 2
WINDOW_A = 128
DILATED_BRANCHES = ((128, 1), (512, 4), (2048, 16))
EPS = 1e-5
NEG_INF = -1e30
ATT_SCALE = HEAD_DIM ** -0.5

ADAM_LR = 0.001
ADAM_B1 = 0.9
ADAM_B2 = 0.999
ADAM_EPS = 1e-08
ADAM_WD = 0.01
ADAM_STEP = 10

N_CHIPS = 4
N_DEV = 8
MESH = pl.DeviceIdType.MESH
GRAD_WIRE_DTYPE = jnp.bfloat16
BRANCH_DTYPE = jnp.bfloat16

VMEM_CAPACITY_V7X = 64 * 1024 * 1024
VMEM_LIMIT_MAX = 56 * 1024 * 1024
VMEM_LIMIT_MIN = 32 * 1024 * 1024

HBM_SPEC = pl.BlockSpec(memory_space=pltpu.HBM)
VMEM_SPEC = pl.BlockSpec(memory_space=pltpu.VMEM)
SMEM_SPEC = pl.BlockSpec(memory_space=pltpu.SMEM)


def _nbytes(shape, dtype):
    return int(np.prod([s for s in shape if s is not None])) * jnp.dtype(dtype).itemsize


def _params(semantics, block_bytes):
    limit = min(max(2 * block_bytes + (4 << 20), VMEM_LIMIT_MIN), VMEM_LIMIT_MAX)
    return pltpu.CompilerParams(dimension_semantics=semantics, vmem_limit_bytes=limit)


class _Hook:
    def __init__(self, operands, out_shape, sems, start, finish, mid=None, aliases=None):
        self.operands, self.out_shape, self.sems = list(operands), list(out_shape), list(sems)
        self.start, self.mid, self.finish = start, mid, finish
        self.aliases = dict(aliases or {})


def _merge_hooks(hooks):
    hooks = [h for h in hooks if h is not None]
    if len(hooks) <= 1:
        return hooks[0] if hooks else None
    n_op = np.cumsum([0] + [len(h.operands) for h in hooks])
    n_out = np.cumsum([0] + [len(h.out_shape) for h in hooks])
    n_sem = np.cumsum([0] + [len(h.sems) for h in hooks])

    def run(which):
        def fn(ops, outs, sems):
            for i, h in enumerate(hooks):
                f = getattr(h, which)
                if f is not None:
                    f(ops[n_op[i]:n_op[i + 1]], outs[n_out[i]:n_out[i + 1]], sems[n_sem[i]:n_sem[i + 1]])
        return fn

    aliases = {}
    for i, h in enumerate(hooks):
        aliases.update({int(n_op[i]) + a: int(n_out[i]) + b for a, b in h.aliases.items()})
    return _Hook(sum([h.operands for h in hooks], []), sum([h.out_shape for h in hooks], []),
                 sum([h.sems for h in hooks], []), run("start"), run("finish"),
                 run("mid") if any(h.mid for h in hooks) else None, aliases)


HOOK_MID_FRACTION = 0.6


def _call(body, hook, *, name, grid, in_specs, out_specs, out_shape, scratch_shapes=(), compiler_params):
    in_specs, out_specs, out_shape = list(in_specs), list(out_specs), list(out_shape)
    scratch_shapes = list(scratch_shapes)
    if hook is None:
        call = pl.pallas_call(body, name=name, grid=grid, in_specs=in_specs, out_specs=out_specs,
                              out_shape=out_shape, scratch_shapes=scratch_shapes,
                              compiler_params=compiler_params)
        return lambda *operands: (call(*operands), [])
    n_in, n_hin, n_out, n_hout, n_scr = (len(in_specs), len(hook.operands), len(out_specs),
                                         len(hook.out_shape), len(scratch_shapes))
    total = int(np.prod(grid))
    t_mid = min(int(total * HOOK_MID_FRACTION), total - 1)

    def wrapped(*refs):
        ins, h_in = refs[:n_in], refs[n_in:n_in + n_hin]
        o0 = n_in + n_hin
        outs, h_out = refs[o0:o0 + n_out], refs[o0 + n_out:o0 + n_out + n_hout]
        s0 = o0 + n_out + n_hout
        scr, h_sems = refs[s0:s0 + n_scr], refs[s0 + n_scr:]
        t = pl.program_id(0)
        for axis in range(1, len(grid)):
            t = t * grid[axis] + pl.program_id(axis)

        @pl.when(t == 0)
        def _():
            hook.start(h_in, h_out, h_sems)

        body(*ins, *outs, *scr)
        if hook.mid is not None:
            @pl.when(t == t_mid)
            def _():
                hook.mid(h_in, h_out, h_sems)

        @pl.when(t == total - 1)
        def _():
            hook.finish(h_in, h_out, h_sems)

    params = pltpu.CompilerParams(dimension_semantics=("arbitrary",) * len(grid),
                                  vmem_limit_bytes=compiler_params.vmem_limit_bytes)
    call = pl.pallas_call(
        wrapped, name=name, grid=grid,
        in_specs=in_specs + [HBM_SPEC] * n_hin, out_specs=out_specs + [HBM_SPEC] * n_hout,
        out_shape=out_shape + hook.out_shape, scratch_shapes=scratch_shapes + hook.sems,
        input_output_aliases={n_in + a: n_out + b for a, b in hook.aliases.items()},
        compiler_params=params)

    def run(*operands):
        res = call(*operands, *hook.operands)
        return res[:n_out], res[n_out:]

    return run


def _remote(src, dst, send_sem, recv_sem, device):
    return pltpu.make_async_remote_copy(src_ref=src, dst_ref=dst, send_sem=send_sem, recv_sem=recv_sem,
                                        device_id=device, device_id_type=MESH)


def alibi_slopes(n):
    return [float(v) for v in np.asarray(2.0 ** (-8.0 * (np.arange(n) + 1) / n), dtype=np.float32)]


def _matmul(name, a, b, extras, *, mode, grid, a_spec, b_spec, extra_specs, out_shapes, out_specs,
            epilogue, prologue=None, acc_shape=None, hook=None):
    dims = {"nn": ((1,), (0,)), "nt": ((1,), (1,)), "tn": ((0,), (0,))}[mode]
    nk = grid[2]
    n_ex, n_out = len(extras), len(out_shapes)

    def body(a_ref, b_ref, *rest):
        ex, outs = rest[:n_ex], rest[n_ex:n_ex + n_out]
        av = a_ref[...]
        if prologue is not None:
            av = prologue(av)
        part = lax.dot_general(av, b_ref[...], (dims, ((), ())), preferred_element_type=F32)

        def finish(acc):
            res = epilogue(acc, *[e[...] for e in ex])
            for o, r in zip(outs, res):
                o[...] = r.astype(o.dtype)

        if nk == 1:
            finish(part)
        else:
            acc_ref = rest[-1]
            k = pl.program_id(2)

            @pl.when(k == 0)
            def _():
                acc_ref[...] = part

            @pl.when(k > 0)
            def _():
                acc_ref[...] += part

            @pl.when(k == nk - 1)
            def _():
                finish(acc_ref[...])

    blocks = [(a_spec.block_shape, a.dtype), (b_spec.block_shape, b.dtype)]
    blocks += [(s.block_shape, e.dtype) for s, e in zip(extra_specs, extras)]
    blocks += [(s.block_shape, o.dtype) for s, o in zip(out_specs, out_shapes)]
    nbytes = sum(_nbytes(s, d) for s, d in blocks)
    scratch = []
    if nk > 1:
        scratch.append(pltpu.VMEM(acc_shape, F32))
        nbytes += _nbytes(acc_shape, F32)
    res, hook_res = _call(
        body, hook, name=name, grid=grid,
        in_specs=[a_spec, b_spec, *extra_specs], out_specs=list(out_specs), out_shape=list(out_shapes),
        scratch_shapes=scratch,
        compiler_params=_params(("parallel", "parallel", "arbitrary"), nbytes),
    )(a, b, *extras)
    return res if hook is None else (res, hook_res)


def _mm_specs(mode, tm, tn, tk, b_block=None, b_map=None):
    if mode == "tn":
        a_spec = pl.BlockSpec((tk, tm), lambda i, j, k: (k, i))
    else:
        a_spec = pl.BlockSpec((tm, tk), lambda i, j, k: (i, k))
    if b_block is not None:
        b_spec = pl.BlockSpec(b_block, b_map)
    elif mode == "nt":
        b_spec = pl.BlockSpec((tn, tk), lambda i, j, k: (j, k))
    else:
        b_spec = pl.BlockSpec((tk, tn), lambda i, j, k: (k, j))
    return a_spec, b_spec


def _project_by_class(name, h, w_t, bias, row_off, width, dilations, hook=None):
    s, d = h.shape
    tm = _tile(s, 1024)
    tn = 512 if width % 512 == 0 and row_off % 512 == 0 else _tile(width, 256)
    off = row_off // tn
    assert row_off % tn == 0 and tn % 128 == 0
    n_out = len(dilations)

    def body(h_ref, w_ref, b_ref, *rest):
        outs, perm_ref = rest[:n_out], rest[n_out]
        acc = lax.dot_general(h_ref[...], w_ref[...], (((1,), (1,)), ((), ())), preferred_element_type=F32)
        acc = acc + b_ref[...]
        for j in range(tn // 128):
            cols = slice(j * 128, (j + 1) * 128)
            for o_ref, dil in zip(outs, dilations):
                _to_classes(o_ref, cols, acc[:, cols], perm_ref, dil)

    blocks = tm * d * 2 + tn * d * 2 + 3 * tm * tn * 2 + tm * 128 * 4
    res, hook_res = _call(
        body, hook, name=name, grid=(s // tm, width // tn),
        in_specs=[pl.BlockSpec((tm, d), lambda i, j: (i, 0)), pl.BlockSpec((tn, d), lambda i, j: (j + off, 0)),
                  pl.BlockSpec((1, tn), lambda i, j: (0, j + off))],
        out_specs=[pl.BlockSpec((dil, tm // dil, tn), lambda i, j: (0, i, j)) for dil in dilations],
        out_shape=[_class_shape(dil, s, width, BF16) for dil in dilations],
        scratch_shapes=[pltpu.VMEM((tm, 128), F32)],
        compiler_params=_params(("parallel", "parallel"), blocks),
    )(h, w_t, bias)
    return res if hook is None else (res, hook_res)


def _tile(n, want):
    if n <= want:
        return n
    t = (want // 128) * 128
    while t > 128 and n % t:
        t -= 128
    assert n % t == 0, (n, want)
    return t


def _row_tile(s):
    return 256 if s % 256 == 0 else s


def _norm_fwd(name, x, g, hook=None):
    s, d = x.shape
    tm = _row_tile(s)

    def body(x_ref, g_ref, h_ref, r_ref):
        xv = x_ref[...]
        r = lax.rsqrt(jnp.mean(xv * xv, axis=-1, keepdims=True) + EPS)
        h_ref[...] = ((xv * r) * g_ref[...]).astype(BF16)
        r_ref[...] = r

    row = pl.BlockSpec((tm, d), lambda i: (i, 0))
    res, hook_res = _call(
        body, hook, name=name, grid=(s // tm,),
        in_specs=[row, pl.BlockSpec((1, d), lambda i: (0, 0))],
        out_specs=[row, pl.BlockSpec((tm, 1), lambda i: (i, 0))],
        out_shape=[jax.ShapeDtypeStruct((s, d), BF16), jax.ShapeDtypeStruct((s, 1), F32)],
        compiler_params=_params(("parallel",), tm * d * 6),
    )(x, g)
    return res if hook is None else (res, hook_res)


def _norm_bwd(name, dh, x, r, g, dres, hook=None):
    s, d = x.shape
    tm = _row_tile(s)

    def body(dh_ref, x_ref, r_ref, g_ref, dres_ref, dx_ref, dxb_ref, dg_ref):
        rv = r_ref[...]
        xn = x_ref[...] * rv
        dhv = dh_ref[...]
        dxn = dhv * g_ref[...]
        dx = dres_ref[...] + rv * (dxn - xn * jnp.mean(dxn * xn, axis=-1, keepdims=True))
        dx_ref[...] = dx
        dxb_ref[...] = dx.astype(BF16)
        part = jnp.sum(dhv * xn, axis=0, keepdims=True)

        @pl.when(pl.program_id(0) == 0)
        def _():
            dg_ref[...] = part

        @pl.when(pl.program_id(0) > 0)
        def _():
            dg_ref[...] += part

    row = pl.BlockSpec((tm, d), lambda i: (i, 0))
    vec = pl.BlockSpec((1, d), lambda i: (0, 0))
    res, hook_res = _call(
        body, hook, name=name, grid=(s // tm,),
        in_specs=[row, row, pl.BlockSpec((tm, 1), lambda i: (i, 0)), vec, row],
        out_specs=[row, row, vec],
        out_shape=[jax.ShapeDtypeStruct((s, d), F32), jax.ShapeDtypeStruct((s, d), BF16),
                   jax.ShapeDtypeStruct((1, d), F32)],
        compiler_params=_params(("arbitrary",), tm * d * 18),
    )(dh, x, r, g, dres)
    return res if hook is None else (res, hook_res)


def _loss_head(x3, target, g):
    s, d = x3.shape
    tm = _row_tile(s)

    def body(x_ref, t_ref, g_ref, dx_ref, dxb_ref, loss_ref, dg_ref):
        xv = x_ref[...]
        gv = g_ref[...]
        r = lax.rsqrt(jnp.mean(xv * xv, axis=-1, keepdims=True) + EPS)
        xn = xv * r
        err = xn * gv - t_ref[...]
        loss = 0.5 * jnp.sum(jnp.mean(err * err, axis=-1, keepdims=True), axis=0, keepdims=True)
        dy = err / d
        dxn = dy * gv
        dx = r * (dxn - xn * jnp.mean(dxn * xn, axis=-1, keepdims=True))
        dx_ref[...] = dx
        dxb_ref[...] = dx.astype(BF16)
        dg = jnp.sum(dy * xn, axis=0, keepdims=True)
        loss_row = jnp.broadcast_to(loss, (1, 128))

        @pl.when(pl.program_id(0) == 0)
        def _():
            dg_ref[...] = dg
            loss_ref[...] = loss_row

        @pl.when(pl.program_id(0) > 0)
        def _():
            dg_ref[...] += dg
            loss_ref[...] += loss_row

    row = pl.BlockSpec((tm, d), lambda i: (i, 0))
    vec = pl.BlockSpec((1, d), lambda i: (0, 0))
    return _call(
        body, None, name="loss_head", grid=(s // tm,),
        in_specs=[row, row, vec],
        out_specs=[row, row, pl.BlockSpec((1, 128), lambda i: (0, 0)), vec],
        out_shape=[jax.ShapeDtypeStruct((s, d), F32), jax.ShapeDtypeStruct((s, d), BF16),
                   jax.ShapeDtypeStruct((1, 128), F32), jax.ShapeDtypeStruct((1, d), F32)],
        compiler_params=_params(("arbitrary",), tm * d * 14),
    )(x3, target, g)[0]


def _low_lanes(rows):
    return lax.broadcasted_iota(jnp.int32, (rows, PAIR), 1) < HEAD_DIM


def _to_classes(dst_ref, cols, value, perm_ref, dil):
    rows = value.shape[0]
    if dil == 1:
        dst_ref[0, :, cols] = value.astype(dst_ref.dtype)
        return
    perm_ref[...] = value
    for r in range(dil):
        dst_ref[r, :, cols] = perm_ref[pl.ds(r, rows // dil, stride=dil), :].astype(dst_ref.dtype)


def _from_classes(src_ref, cols, perm_ref, dil):
    if dil == 1:
        return src_ref[0, :, cols].astype(F32)
    rows = perm_ref.shape[0]
    for r in range(dil):
        perm_ref[pl.ds(r, rows // dil, stride=dil), :] = src_ref[r, :, cols].astype(F32)
    return perm_ref[...]


def _class_spec(dil, tm, width):
    return pl.BlockSpec((dil, tm // dil, width), lambda i: (0, i, 0))


def _class_shape(dil, s, width, dtype):
    return jax.ShapeDtypeStruct((dil, s // dil, width), dtype)


DILATIONS = tuple(d for _, d in DILATED_BRANCHES)


def _mix_fwd(oa, obs, lses, ga, gb):
    s, qa = oa.shape
    qb = obs[0].shape[2]
    tm = _row_tile(s)
    all_lanes = slice(0, 128)

    def body(oa_ref, o1_ref, o2_ref, o3_ref, l1_ref, l2_ref, l3_ref, ga_ref, gb_ref,
             mix_ref, ob_ref, t1_ref, t2_ref, t3_ref, ra_ref, rb_ref, perm_ref):
        oav = oa_ref[...]
        ra = lax.rsqrt(jnp.mean(oav * oav, axis=-1, keepdims=True) + EPS)
        ra_ref[...] = ra
        mix_ref[:, 0:qa] = ((oav * ra) * ga_ref[...]).astype(BF16)
        l1, l2, l3 = [_from_classes(l_ref, all_lanes, perm_ref, dil)
                      for l_ref, dil in zip((l1_ref, l2_ref, l3_ref), DILATIONS)]
        mx = jnp.maximum(jnp.maximum(l1, l2), l3)
        e1, e2, e3 = jnp.exp(l1 - mx), jnp.exp(l2 - mx), jnp.exp(l3 - mx)
        tot = e1 + e2 + e3
        lse = mx + jnp.log(tot)
        for t_ref, dil in zip((t1_ref, t2_ref, t3_ref), DILATIONS):
            _to_classes(t_ref, all_lanes, lse, perm_ref, dil)
        ws = (e1 / tot, e2 / tot, e3 / tot)
        low = _low_lanes(tm)
        ssq = jnp.zeros((tm, 1), F32)
        for i in range(qb // PAIR):
            sl = slice(i * PAIR, (i + 1) * PAIR)
            acc = jnp.zeros((tm, PAIR), F32)
            for w, o_ref, dil in zip(ws, (o1_ref, o2_ref, o3_ref), DILATIONS):
                wexp = jnp.where(low, w[:, 2 * i:2 * i + 1], w[:, 2 * i + 1:2 * i + 2])
                acc = acc + wexp * _from_classes(o_ref, sl, perm_ref, dil)
            ob_ref[:, sl] = acc
            ssq = ssq + jnp.sum(acc * acc, axis=-1, keepdims=True)
        rb = lax.rsqrt(ssq / qb + EPS)
        rb_ref[...] = rb
        mix_ref[:, qa:qa + qb] = ((ob_ref[...] * rb) * gb_ref[...]).astype(BF16)

    def row(w):
        return pl.BlockSpec((tm, w), lambda i: (i, 0))

    def vec(w):
        return pl.BlockSpec((1, w), lambda i: (0, 0))

    return _call(
        body, None, name="mix_fwd", grid=(s // tm,),
        in_specs=([row(qa)] + [_class_spec(d, tm, qb) for d in DILATIONS]
                  + [_class_spec(d, tm, 128) for d in DILATIONS] + [vec(qa), vec(qb)]),
        out_specs=([row(qa + qb), row(qb)] + [_class_spec(d, tm, 128) for d in DILATIONS] + [row(1), row(1)]),
        out_shape=([jax.ShapeDtypeStruct((s, qa + qb), BF16), jax.ShapeDtypeStruct((s, qb), F32)]
                   + [_class_shape(d, s, 128, F32) for d in DILATIONS]
                   + [jax.ShapeDtypeStruct((s, 1), F32), jax.ShapeDtypeStruct((s, 1), F32)]),
        scratch_shapes=[pltpu.VMEM((tm, 128), F32)],
        compiler_params=_params(("parallel",), tm * (qa + 4 * qb) * 4 + tm * (qa + qb) * 2 + tm * 4096),
    )(oa, *obs, *lses, ga, gb)[0]


def _head_rowsums(prod, rows):
    low = _low_lanes(rows)
    lane = lax.broadcasted_iota(jnp.int32, (rows, 128), 1)
    out = jnp.zeros((rows, 128), F32)
    for i in range(prod.shape[1] // PAIR):
        tile = prod[:, i * PAIR:(i + 1) * PAIR]
        lo = jnp.sum(jnp.where(low, tile, 0.0), axis=-1, keepdims=True)
        hi = jnp.sum(jnp.where(low, 0.0, tile), axis=-1, keepdims=True)
        out = jnp.where(lane == 2 * i, lo, out)
        out = jnp.where(lane == 2 * i + 1, hi, out)
    return out


def _mix_bwd(dmix, oa, ob, ra, rb, ga, gb, hook=None):
    s, qa = oa.shape
    qb = ob.shape[1]
    tm = _row_tile(s)

    def one(dy, o, r, g):
        xn = o * r
        dxn = dy * g
        do = r * (dxn - xn * jnp.mean(dxn * xn, axis=-1, keepdims=True))
        return do, jnp.sum(dy * xn, axis=0, keepdims=True), _head_rowsums(do * o, tm)

    def body(dmix_ref, oa_ref, ob_ref, ra_ref, rb_ref, ga_ref, gb_ref,
             doa_ref, dob1_ref, dob2_ref, dob3_ref, dla_ref, dlb1_ref, dlb2_ref, dlb3_ref,
             dga_ref, dgb_ref, perm_ref):
        doa, dga, dla = one(dmix_ref[:, 0:qa], oa_ref[...], ra_ref[...], ga_ref[...])
        dob, dgb, dlb = one(dmix_ref[:, qa:qa + qb], ob_ref[...], rb_ref[...], gb_ref[...])
        doa_ref[...] = doa.astype(BF16)
        dla_ref[...] = dla
        for dob_ref, dlb_ref, dil in zip((dob1_ref, dob2_ref, dob3_ref), (dlb1_ref, dlb2_ref, dlb3_ref),
                                         DILATIONS):
            _to_classes(dlb_ref, slice(0, 128), dlb, perm_ref, dil)
            for i in range(qb // PAIR):
                sl = slice(i * PAIR, (i + 1) * PAIR)
                _to_classes(dob_ref, sl, dob[:, sl], perm_ref, dil)

        @pl.when(pl.program_id(0) == 0)
        def _():
            dga_ref[...] = dga
            dgb_ref[...] = dgb

        @pl.when(pl.program_id(0) > 0)
        def _():
            dga_ref[...] += dga
            dgb_ref[...] += dgb

    def row(w):
        return pl.BlockSpec((tm, w), lambda i: (i, 0))

    def vec(w):
        return pl.BlockSpec((1, w), lambda i: (0, 0))

    res, hook_res = _call(
        body, hook, name="mix_bwd", grid=(s // tm,),
        in_specs=[row(qa + qb), row(qa), row(qb), row(1), row(1), vec(qa), vec(qb)],
        out_specs=([row(qa)] + [_class_spec(d, tm, qb) for d in DILATIONS] + [row(128)]
                   + [_class_spec(d, tm, 128) for d in DILATIONS] + [vec(qa), vec(qb)]),
        out_shape=([jax.ShapeDtypeStruct((s, qa), BF16)] + [_class_shape(d, s, qb, BF16) for d in DILATIONS]
                   + [jax.ShapeDtypeStruct((s, 128), F32)] + [_class_shape(d, s, 128, F32) for d in DILATIONS]
                   + [jax.ShapeDtypeStruct((1, qa), F32), jax.ShapeDtypeStruct((1, qb), F32)]),
        scratch_shapes=[pltpu.VMEM((tm, 128), F32)],
        compiler_params=_params(("arbitrary",), tm * (qa + qb) * 16),
    )(dmix, oa, ob, ra, rb, ga, gb)
    return res if hook is None else (res, hook_res)


def _assemble_dproj(dqa, dkva, dqs, dks, dvs):
    s, qa = dqa.shape
    kva = dkva.shape[1]
    qb = dqs[0].shape[2]
    width = qa + kva + 3 * qb
    tm = _row_tile(s)

    def body(dqa_ref, dkva_ref, q1, q2, q3, k1, k2, k3, v1, v2, v3, dp_ref, db_ref, perm_ref):
        first = pl.program_id(0) == 0

        def emit(off, val):
            dp_ref[:, off:off + PAIR] = val.astype(BF16)
            col = jnp.sum(val, axis=0, keepdims=True)

            @pl.when(first)
            def _():
                db_ref[:, off:off + PAIR] = col

            @pl.when(jnp.logical_not(first))
            def _():
                db_ref[:, off:off + PAIR] += col

        for i in range(qa // PAIR):
            emit(i * PAIR, dqa_ref[:, i * PAIR:(i + 1) * PAIR])
        for i in range(kva // PAIR):
            emit(qa + i * PAIR, dkva_ref[:, i * PAIR:(i + 1) * PAIR])
        for j, branch_refs in enumerate(((q1, q2, q3), (k1, k2, k3), (v1, v2, v3))):
            for i in range(qb // PAIR):
                sl = slice(i * PAIR, (i + 1) * PAIR)
                total = None
                for ref, dil in zip(branch_refs, DILATIONS):
                    val = _from_classes(ref, sl, perm_ref, dil)
                    total = val if total is None else total + val
                emit(qa + kva + j * qb + i * PAIR, total)

    def row(w):
        return pl.BlockSpec((tm, w), lambda i: (i, 0))

    return _call(
        body, None, name="assemble_dproj", grid=(s // tm,),
        in_specs=[row(qa), row(kva)] + [_class_spec(d, tm, qb) for d in DILATIONS] * 3,
        out_specs=[row(width), pl.BlockSpec((1, width), lambda i: (0, 0))],
        out_shape=[jax.ShapeDtypeStruct((s, width), BF16), jax.ShapeDtypeStruct((1, width), F32)],
        scratch_shapes=[pltpu.VMEM((tm, 128), F32)],
        compiler_params=_params(("arbitrary",), tm * (qa + kva + 9 * qb) * 4 + tm * width * 2),
    )(dqa, dkva, *dqs, *dks, *dvs)[0]


def _fill_bias(bias_ref, n_pairs, max_steps, dil, slopes, sink_ref=None):
    qi = lax.broadcasted_iota(jnp.int32, (BLOCK, 2 * BLOCK), 0)
    kj = lax.broadcasted_iota(jnp.int32, (BLOCK, 2 * BLOCK), 1)
    steps = qi + BLOCK - kj
    dist = (steps * dil).astype(F32)
    band = (steps >= 0) & (steps <= max_steps)
    assert sink_ref is None or max_steps < BLOCK
    for first in (0, 1):
        valid = band & (kj >= BLOCK) if first else band
        for i in range(n_pairs):
            tables = []
            for half in (0, 1):
                table = jnp.where(valid, -(slopes[2 * i + half] * dist), NEG_INF)
                if sink_ref is not None:
                    table = jnp.where(kj == 0, sink_ref[2 * i + half], table)
                tables.append(table)
            bias_ref[first, i] = jnp.concatenate(tables, axis=0)


def _without_sink_row(tile):
    row = lax.broadcasted_iota(jnp.int32, tile.shape, 0)
    return jnp.where(row == 0, jnp.zeros_like(tile), tile)


def _bias_shape(n_pairs):
    return pltpu.VMEM((2, n_pairs, 2 * BLOCK, 2 * BLOCK), F32)


def _stack_heads(tile, low):
    zero = jnp.zeros_like(tile)
    return jnp.concatenate([jnp.where(low, tile, zero), jnp.where(low, zero, tile)], axis=0)


def _unstack_heads(stacked, low):
    return jnp.where(low, stacked[0:BLOCK], stacked[BLOCK:2 * BLOCK])


def _head_columns(ref, i):
    return jnp.concatenate([ref[:, 2 * i:2 * i + 1], ref[:, 2 * i + 1:2 * i + 2]], axis=0)


def _swap_halves(t):
    return pltpu.roll(t, HEAD_DIM, 1)


def _dup_group(t_bf16, group):
    t = t_bf16.astype(F32)
    low = lax.broadcasted_iota(jnp.int32, t.shape, 1) < HEAD_DIM
    keep = low if group == 0 else jnp.logical_not(low)
    return jnp.where(keep, t, _swap_halves(t)).astype(BF16)


def _attn_fwd(name, q, kv, *, dil, max_steps, slopes, sinks=None, hook=None):
    grouped = sinks is not None
    _, length, w = q.shape
    n_pairs = w // PAIR
    nb = length // BLOCK
    heads_per_group = 2 * n_pairs // N_KV_GROUPS

    def body(*refs):
        if grouped:
            sink_ref, q_ref, kvp_ref, kvc_ref, o_ref, lse_ref, bias_ref = refs
        else:
            q_ref, kp_ref, kc_ref, vp_ref, vc_ref, o_ref, lse_ref, bias_ref = refs
        n = pl.program_id(1)

        @pl.when((pl.program_id(0) == 0) & (n == 0))
        def _():
            _fill_bias(bias_ref, n_pairs, max_steps, dil, slopes, sink_ref if grouped else None)

        first = (n == 0).astype(jnp.int32)
        low = _low_lanes(BLOCK)
        lane = lax.broadcasted_iota(jnp.int32, (BLOCK, 128), 1)
        lse_acc = jnp.zeros((BLOCK, 128), F32)
        if grouped:
            kv_all = jnp.concatenate([kvp_ref[...], kvc_ref[...]], axis=0)
            k_dup = [_without_sink_row(_dup_group(kv_all[:, 0:PAIR], g)) for g in range(N_KV_GROUPS)]
            v_dup = [_without_sink_row(_dup_group(kv_all[:, PAIR:2 * PAIR], g)) for g in range(N_KV_GROUPS)]
        for i in range(n_pairs):
            sl = slice(i * PAIR, (i + 1) * PAIR)
            qs = _stack_heads(q_ref[:, sl] * ATT_SCALE, low)
            if grouped:
                kk, vv = k_dup[2 * i // heads_per_group], v_dup[2 * i // heads_per_group]
            else:
                kk = jnp.concatenate([kp_ref[:, sl], kc_ref[:, sl]], axis=0)
                vv = jnp.concatenate([vp_ref[:, sl], vc_ref[:, sl]], axis=0)
            sc = lax.dot_general(qs, kk, (((1,), (1,)), ((), ())), preferred_element_type=F32)
            sc = sc + bias_ref[first, i]
            m = jnp.max(sc, axis=-1, keepdims=True)
            p = jnp.exp(sc - m)
            den = jnp.sum(p, axis=-1, keepdims=True)
            o = jnp.dot(p.astype(BF16), vv, preferred_element_type=F32) / den
            o_ref[:, sl] = _unstack_heads(o, low).astype(o_ref.dtype)
            lse = m + jnp.log(den)
            lse_acc = jnp.where(lane == 2 * i, lse[0:BLOCK], lse_acc)
            lse_acc = jnp.where(lane == 2 * i + 1, lse[BLOCK:2 * BLOCK], lse_acc)
        lse_ref[...] = lse_acc

    def cur(width):
        return pl.BlockSpec((None, BLOCK, width), lambda r, n: (r, n, 0))

    def prev(width):
        return pl.BlockSpec((None, BLOCK, width), lambda r, n: (r, jnp.maximum(n - 1, 0), 0))

    if grouped:
        kvw = kv.shape[2]
        operands = [sinks, q, kv, kv]
        in_specs = [SMEM_SPEC, cur(w), prev(kvw), cur(kvw)]
    else:
        operands = [q, kv[0], kv[0], kv[1], kv[1]]
        in_specs = [cur(w), prev(w), cur(w), prev(w), cur(w)]
    res, hook_res = _call(
        body, hook, name=name, grid=(dil, nb), in_specs=in_specs,
        out_specs=[cur(w), cur(128)],
        out_shape=[jax.ShapeDtypeStruct((dil, length, w), F32 if grouped else BRANCH_DTYPE),
                   jax.ShapeDtypeStruct((dil, length, 128), F32)],
        scratch_shapes=[_bias_shape(n_pairs)],
        compiler_params=_params(("arbitrary", "arbitrary"), BLOCK * w * 16 + n_pairs * BLOCK * BLOCK * 16),
    )(*operands)
    return res if hook is None else (res, hook_res)


def _attn_bwd(name, q, kv, do, lse, delta, *, dil, max_steps, slopes, sinks=None, hook=None):
    grouped = sinks is not None
    _, length, w = q.shape
    n_pairs = w // PAIR
    nb = length // BLOCK
    heads_per_group = 2 * n_pairs // N_KV_GROUPS
    pairs_per_group = n_pairs // N_KV_GROUPS

    def body(*refs):
        if grouped:
            (sink_ref, q_ref, kvp_ref, kvc_ref, do_ref, lse_ref, dl_ref,
             dq_ref, dkv_ref, dsink_ref, acc_ref, bias_ref) = refs
        else:
            (q_ref, kp_ref, kc_ref, vp_ref, vc_ref, do_ref, lse_ref, dl_ref,
             dq_ref, dk_ref, dv_ref, acck_ref, accv_ref, bias_ref) = refs
        n = pl.program_id(1)

        @pl.when((pl.program_id(0) == 0) & (n == 0))
        def _():
            _fill_bias(bias_ref, n_pairs, max_steps, dil, slopes, sink_ref if grouped else None)

        @pl.when(n == 0)
        def _():
            if grouped:
                acc_ref[...] = jnp.zeros_like(acc_ref)

                @pl.when(pl.program_id(0) == 0)
                def _():
                    dsink_ref[...] = jnp.zeros_like(dsink_ref)
            else:
                acck_ref[...] = jnp.zeros_like(acck_ref)
                accv_ref[...] = jnp.zeros_like(accv_ref)

        @pl.when(n == nb)
        def _():
            if grouped:
                dkv_ref[...] = acc_ref[...]
            else:
                dk_ref[...] = acck_ref[...].astype(dk_ref.dtype)
                dv_ref[...] = accv_ref[...].astype(dv_ref.dtype)

        @pl.when(n < nb)
        def _():
            first = (n == 0).astype(jnp.int32)
            low = _low_lanes(BLOCK)
            low_kv = _low_lanes(2 * BLOCK)
            lane1 = lax.broadcasted_iota(jnp.int32, (1, 128), 1)
            if grouped:
                kv_all = jnp.concatenate([kvp_ref[...], kvc_ref[...]], axis=0)
                k_dup = [_without_sink_row(_dup_group(kv_all[:, 0:PAIR], g)) for g in range(N_KV_GROUPS)]
                v_dup = [_without_sink_row(_dup_group(kv_all[:, PAIR:2 * PAIR], g)) for g in range(N_KV_GROUPS)]
                dk_grp =[jnp.zeros((2 * BLOCK, PAIR), F32) for _ in range(N_KV_GROUPS)]
                dv_grp = [jnp.zeros((2 * BLOCK, PAIR), F32) for _ in range(N_KV_GROUPS)]
                dsink = jnp.zeros((1, 128), F32)
            for i in range(n_pairs):
                sl = slice(i * PAIR, (i + 1) * PAIR)
                qs = _stack_heads(q_ref[:, sl] * ATT_SCALE, low)
                dos = _stack_heads(do_ref[:, sl], low)
                if grouped:
                    grp = 2 * i // heads_per_group
                    kk, vv = k_dup[grp], v_dup[grp]
                else:
                    kk = jnp.concatenate([kp_ref[:, sl], kc_ref[:, sl]], axis=0)
                    vv = jnp.concatenate([vp_ref[:, sl], vc_ref[:, sl]], axis=0)
                lse_col = _head_columns(lse_ref, i)
                dl_col = _head_columns(dl_ref, i)
                sc = lax.dot_general(qs, kk, (((1,), (1,)), ((), ())), preferred_element_type=F32)
                p = jnp.exp(sc + bias_ref[first, i] - lse_col)
                dp = lax.dot_general(dos, vv, (((1,), (1,)), ((), ())), preferred_element_type=F32)
                ds_f32 = p * (dp - dl_col)
                ds = ds_f32.astype(BF16)
                dq = jnp.dot(ds, kk, preferred_element_type=F32)
                dkk = lax.dot_general(ds, qs, (((0,), (0,)), ((), ())), preferred_element_type=F32)
                dvv = lax.dot_general(p.astype(BF16), dos, (((0,), (0,)), ((), ())),
                                      preferred_element_type=F32)
                if grouped:
                    for half in (0, 1):
                        contrib = jnp.sum(ds_f32[half * BLOCK:(half + 1) * BLOCK, 0:1], axis=0, keepdims=True)
                        dsink = jnp.where(lane1 == 2 * i + half, dsink + contrib, dsink)
                dq_ref[:, sl] = (_unstack_heads(dq, low) * ATT_SCALE).astype(dq_ref.dtype)
                if grouped:
                    dk_grp[grp] = dk_grp[grp] + dkk
                    dv_grp[grp] = dv_grp[grp] + dvv
                else:
                    dk_ref[:, sl] = (acck_ref[:, sl] + dkk[0:BLOCK]).astype(dk_ref.dtype)
                    acck_ref[:, sl] = dkk[BLOCK:2 * BLOCK]
                    dv_ref[:, sl] = (accv_ref[:, sl] + dvv[0:BLOCK]).astype(dv_ref.dtype)
                    accv_ref[:, sl] = dvv[BLOCK:2 * BLOCK]
            if grouped:
                folded = [_without_sink_row(t + _swap_halves(t)) for t in dk_grp + dv_grp]
                dk_tile = jnp.where(low_kv, folded[0], folded[1])
                dv_tile = jnp.where(low_kv, folded[2], folded[3])
                part = jnp.concatenate([dk_tile, dv_tile], axis=1)
                dkv_ref[...] = acc_ref[...] + part[0:BLOCK]
                acc_ref[...] = part[BLOCK:2 * BLOCK]
                dsink_ref[...] += dsink

    last = nb - 1

    def cur(width):
        return pl.BlockSpec((None, BLOCK, width), lambda r, n: (r, jnp.minimum(n, last), 0))

    def prev(width):
        return pl.BlockSpec((None, BLOCK, width),
                            lambda r, n: (r, jnp.maximum(jnp.minimum(n, last) - 1, 0), 0))

    def done(width):
        return pl.BlockSpec((None, BLOCK, width), lambda r, n: (r, jnp.maximum(n - 1, 0), 0))

    if grouped:
        assert pairs_per_group * N_KV_GROUPS == n_pairs and heads_per_group % 2 == 0
        kvw = kv.shape[2]
        operands = [sinks, q, kv, kv, do, lse, delta]
        in_specs = [SMEM_SPEC, cur(w), prev(kvw), cur(kvw), cur(w), cur(128), cur(128)]
        out_specs = [cur(w), done(kvw), pl.BlockSpec((1, 128), lambda r, n: (0, 0))]
        out_shape = [jax.ShapeDtypeStruct((dil, length, w), F32), jax.ShapeDtypeStruct((dil, length, kvw), F32),
                     jax.ShapeDtypeStruct((1, 128), F32)]
        scratch = [pltpu.VMEM((BLOCK, kvw), F32), _bias_shape(n_pairs)]
    else:
        operands = [q, kv[0], kv[0], kv[1], kv[1], do, lse, delta]
        in_specs = [cur(w), prev(w), cur(w), prev(w), cur(w), cur(w), cur(128), cur(128)]
        out_specs = [cur(w), done(w), done(w)]
        out_shape = [jax.ShapeDtypeStruct((dil, length, w), BRANCH_DTYPE)] * 3
        scratch = [pltpu.VMEM((BLOCK, w), F32), pltpu.VMEM((BLOCK, w), F32), _bias_shape(n_pairs)]
    res, hook_res = _call(
        body, hook, name=name, grid=(dil, nb + 1), in_specs=in_specs, out_specs=out_specs,
        out_shape=out_shape, scratch_shapes=scratch,
        compiler_params=_params(("arbitrary", "arbitrary"), BLOCK * w * 32 + n_pairs * BLOCK * BLOCK * 16),
    )(*operands)
    return res if hook is None else (res, hook_res)


def _adamw(name, w, g, m, v):
    rows, cols = w.shape
    tm = 256 if rows % 256 == 0 else rows

    def body(w_ref, g_ref, m_ref, v_ref, d_ref, nm_ref, nv_ref):
        gv = g_ref[...]
        mn = ADAM_B1 * m_ref[...] + (1.0 - ADAM_B1) * gv
        vn = ADAM_B2 * v_ref[...] + (1.0 - ADAM_B2) * (gv * gv)
        m_hat = mn / (1.0 - ADAM_B1 ** ADAM_STEP)
        v_hat = vn / (1.0 - ADAM_B2 ** ADAM_STEP)
        d_ref[...] = -ADAM_LR * (m_hat / (jnp.sqrt(v_hat) + ADAM_EPS) + ADAM_WD * w_ref[...])
        nm_ref[...] = mn
        nv_ref[...] = vn

    spec = pl.BlockSpec((tm, cols), lambda i: (i, 0))
    return _call(
        body, None, name=name, grid=(rows // tm,), in_specs=[spec] * 4, out_specs=[spec] * 3,
        out_shape=[jax.ShapeDtypeStruct(w.shape, F32)] * 3,
        compiler_params=_params(("parallel",), tm * cols * 28),
    )(w, g, m, v)[0]


def _mesh_position():
    return lax.axis_index("x"), lax.axis_index("y"), lax.axis_index("c")


def _other_chips(x, y):
    return [(1 - x, y), (x, 1 - y), (1 - x, 1 - y)]


def _gather_hook(shard, gathered, lo, hi):
    rows, cols = shard.shape
    half, n = rows // 2, hi - lo
    assert lo % 16 == 0 and n % 16 == 0 and half % 16 == 0
    first = gathered is None

    def region(out, owner_chip, which_half):
        return out.at[pl.ds(pl.multiple_of(owner_chip * rows + which_half * half + lo, 16), n)]

    def parts(ops, outs, sems):
        x, y, c = _mesh_position()
        return ops[0], outs[0], sems, x, y, c, 2 * x + y, (x, y, 1 - c), _other_chips(x, y)

    def local_copy(src, out, chip, sem):
        return pltpu.make_async_copy(src, out.at[pl.ds(pl.multiple_of(chip * rows, 16), rows)], sem)

    def start(ops, outs, sems):
        src, out, (send, recv, fsend, frecv, local), x, y, c, chip, sibling, others = parts(ops, outs, sems)
        if first:
            local_copy(src, out, chip, local.at[0]).start()
        mine = src.at[pl.ds(pl.multiple_of(c * half + lo, 16), n)]
        for k, (px, py) in enumerate(others):
            _remote(mine, region(out, chip, c), send.at[k], recv.at[k], (px, py, c)).start()

    def mid(ops, outs, sems):
        src, out, (send, recv, fsend, frecv, local), x, y, c, chip, sibling, others = parts(ops, outs, sems)
        for k, (px, py) in enumerate(others):
            landed = region(out, 2 * px + py, c)
            _remote(landed, landed, send.at[k], recv.at[k], (px, py, c)).wait_recv()
            _remote(landed, landed, fsend.at[k], frecv.at[k], sibling).start()

    def finish(ops, outs, sems):
        src, out, (send, recv, fsend, frecv, local), x, y, c, chip, sibling, others = parts(ops, outs, sems)
        mine = src.at[pl.ds(pl.multiple_of(c * half + lo, 16), n)]
        for k, (px, py) in enumerate(others):
            passed = region(out, 2 * px + py, 1 - c)
            _remote(passed, passed, fsend.at[k], frecv.at[k], sibling).wait_recv()
        for k, (px, py) in enumerate(others):
            landed = region(out, 2 * px + py, c)
            _remote(landed, landed, fsend.at[k], frecv.at[k], sibling).wait_send()
            _remote(mine, region(out, chip, c), send.at[k], recv.at[k], (px, py, c)).wait_send()
        if first:
            local_copy(src, out, chip, local.at[0]).wait()

    sems = [pltpu.SemaphoreType.DMA((3,))] * 4 + [pltpu.SemaphoreType.DMA((1,))]
    out_shape = [jax.ShapeDtypeStruct((N_CHIPS * rows, cols), shard.dtype)]
    if first:
        return _Hook([shard], out_shape, sems, start, finish, mid)
    return _Hook([shard, gathered], out_shape, sems, start, finish, mid, aliases={1: 0})


def _exchange_hook(grad):
    rows, cols = grad.shape[0] // N_CHIPS, grad.shape[1]
    half = rows // 2
    assert half % 16 == 0

    def copies(ops, outs, sems):
        x, y, c = _mesh_position()
        send, recv = sems
        return [_remote(ops[0].at[pl.ds(pl.multiple_of(k * rows + (1 - c) * half, 16), half)], outs[0].at[k],
                        send.at[k], recv.at[k], (x, y, 1 - c)) for k in range(N_CHIPS)]

    def start(ops, outs, sems):
        for cp in copies(ops, outs, sems):
            cp.start()

    def finish(ops, outs, sems):
        for cp in copies(ops, outs, sems):
            cp.wait_recv()
            cp.wait_send()

    return _Hook([grad], [jax.ShapeDtypeStruct((N_CHIPS, half, cols), grad.dtype)],
                 [pltpu.SemaphoreType.DMA((N_CHIPS,))] * 2, start, finish)


def _scatter_hook(chip_sum):
    _, half, cols = chip_sum.shape

    def copies(ops, outs, sems):
        x, y, c = _mesh_position()
        send, recv = sems
        return [_remote(ops[0].at[2 * px + py], outs[0].at[k], send.at[k], recv.at[k], (px, py, c))
                for k, (px, py) in enumerate(_other_chips(x, y))]

    def start(ops, outs, sems):
        for cp in copies(ops, outs, sems):
            cp.start()

    def finish(ops, outs, sems):
        for cp in copies(ops, outs, sems):
            cp.wait_recv()
            cp.wait_send()

    return _Hook([chip_sum], [jax.ShapeDtypeStruct((3, half, cols), chip_sum.dtype)],
                 [pltpu.SemaphoreType.DMA((3,))] * 2, start, finish)


def _share_hook(half_sum):
    half, cols = half_sum.shape

    def copies(ops, outs, sems):
        x, y, c = _mesh_position()
        send, recv, local = sems
        mine = outs[0].at[pl.ds(pl.multiple_of(c * half, 16), half)]
        return (pltpu.make_async_copy(ops[0], mine, local.at[0]),
                _remote(ops[0], mine, send.at[0], recv.at[0], (x, y, 1 - c)))

    def start(ops, outs, sems):
        for cp in copies(ops, outs, sems):
            cp.start()

    def finish(ops, outs, sems):
        here, there = copies(ops, outs, sems)
        there.wait_recv()
        there.wait_send()
        here.wait()

    return _Hook([half_sum], [jax.ShapeDtypeStruct((2 * half, cols), F32)],
                 [pltpu.SemaphoreType.DMA((1,))] * 3, start, finish)


def _sum_tile(half):
    return 256 if half % 256 == 0 else half


def _chip_add(name, grad, from_sibling, core):
    n_chips, half, cols = from_sibling.shape
    rows = 2 * half
    tr = _sum_tile(half)

    def body(core_ref, g_ref, s_ref, o_ref):
        o_ref[...] = (g_ref[...].astype(F32) + s_ref[...].astype(F32)).astype(o_ref.dtype)

    tile = pl.BlockSpec((None, tr, cols), lambda k, i, core_ref: (k, i, 0))
    return pl.pallas_call(
        body, name=name,
        grid_spec=pltpu.PrefetchScalarGridSpec(
            num_scalar_prefetch=1, grid=(n_chips, half // tr),
            in_specs=[pl.BlockSpec((tr, cols), lambda k, i, core_ref:
                                   (k * (rows // tr) + core_ref[0] * (half // tr) + i, 0)), tile],
            out_specs=tile),
        out_shape=jax.ShapeDtypeStruct(from_sibling.shape, from_sibling.dtype),
        compiler_params=_params(("parallel", "parallel"), 3 * tr * cols * 4),
    )(core, grad, from_sibling)


def _final_add(name, chip_sum, from_chips, chip):
    _, half, cols = chip_sum.shape
    tr = _sum_tile(half)

    def body(chip_ref, own_ref, others_ref, o_ref):
        total = own_ref[...].astype(F32)
        for k in range(3):
            total = total + others_ref[k].astype(F32)
        o_ref[...] = total

    return pl.pallas_call(
        body, name=name,
        grid_spec=pltpu.PrefetchScalarGridSpec(
            num_scalar_prefetch=1, grid=(half // tr,),
            in_specs=[pl.BlockSpec((None, tr, cols), lambda i, chip_ref: (chip_ref[0], i, 0)),
                      pl.BlockSpec((3, tr, cols), lambda i, chip_ref: (0, i, 0))],
            out_specs=pl.BlockSpec((tr, cols), lambda i, chip_ref: (i, 0))),
        out_shape=jax.ShapeDtypeStruct((half, cols), F32),
        compiler_params=_params(("parallel",), 6 * tr * cols * 4),
    )(chip, chip_sum, from_chips)


FINISH_CHUNK_ROWS = 256


def _finish_reduction(chip_sums, shares, small):
    n_w, n_s = len(chip_sums), len(shares)
    halves = [cs.shape[1] for cs in chip_sums]
    cols = shares[0].shape[1] if n_s else chip_sums[0].shape[2]
    wire = chip_sums[0].dtype if n_w else GRAD_WIRE_DTYPE
    rows_s = small.shape[0]
    ch = FINISH_CHUNK_ROWS

    def body(*refs):
        sums_in, small_ref, share_in = refs[:n_w], refs[n_w], refs[n_w + 1:n_w + 1 + n_s]
        o0 = n_w + 1 + n_s
        outs, small_out, share_out = refs[o0:o0 + n_w], refs[o0 + n_w], refs[o0 + n_w + 1:o0 + n_w + 1 + n_s]
        arrived = refs[o0 + n_w + 1 + n_s:o0 + 2 * n_w + 1 + n_s]
        (small_all, buf_in, buf_out, chip_send, chip_recv, fin_send, fin_recv, small_send, small_recv,
         share_send, share_recv, share_local, io_sem) = refs[o0 + 2 * n_w + 1 + n_s:]
        x, y, c = _mesh_position()
        chip = 2 * x + y
        me = 4 * x + 2 * y + c
        sibling = (x, y, 1 - c)
        others = _other_chips(x, y)
        pending, local = [], []

        for w in range(n_w):
            for k, (px, py) in enumerate(others):
                cp = _remote(sums_in[w].at[2 * px + py], arrived[w].at[k], chip_send.at[w, k],
                             chip_recv.at[w, k], (px, py, c))
                cp.start()
                pending.append(cp)
        small_all[me] = small_ref[...]
        for j in range(N_DEV - 1):
            peer = (me + 1 + j) % N_DEV
            cp = _remote(small_all.at[me], small_all.at[me], small_send.at[j], small_recv.at[j],
                         (peer // 4, (peer // 2) % 2, peer % 2))
            cp.start()
            pending.append(cp)

        def halves_of(out, rows):
            return [out.at[pl.ds(pl.multiple_of(which * rows, 16), rows)] for which in (c, 1 - c)]

        for i in range(n_s):
            mine, _ = halves_of(share_out[i], share_in[i].shape[0])
            cp = pltpu.make_async_copy(share_in[i], mine, share_local.at[i])
            cp.start()
            local.append(cp)
            cp = _remote(share_in[i], mine, share_send.at[i], share_recv.at[i], sibling)
            cp.start()
            pending.append(cp)

        def add_chunk(w, dst, start, size):
            total = None
            for src in [sums_in[w].at[chip]] + [arrived[w].at[k] for k in range(3)]:
                cp = pltpu.make_async_copy(src.at[pl.ds(start, size)], buf_in.at[pl.ds(0, size)], io_sem)
                cp.start()
                cp.wait()
                val = buf_in[pl.ds(0, size), :].astype(F32)
                total = val if total is None else total + val
            buf_out[pl.ds(0, size), :] = total
            cp = pltpu.make_async_copy(buf_out.at[pl.ds(0, size)], dst.at[pl.ds(start, size)], io_sem)
            cp.start()
            cp.wait()

        for w in range(n_w):
            for k, (px, py) in enumerate(others):
                _remote(sums_in[w].at[chip], arrived[w].at[k], chip_send.at[w, k], chip_recv.at[w, k],
                        (px, py, c)).wait_recv()
            mine, _ = halves_of(outs[w], halves[w])
            n_full = halves[w] // ch

            def loop_body(i, carry, w=w, mine=mine):
                add_chunk(w, mine, pl.multiple_of(i * ch, ch), ch)
                return carry

            lax.fori_loop(0, n_full, loop_body, 0)
            if halves[w] % ch:
                add_chunk(w, mine, n_full * ch, halves[w] - n_full * ch)
            cp = _remote(mine, mine, fin_send.at[w], fin_recv.at[w], sibling)
            cp.start()
            pending.append(cp)
        for w in range(n_w):
            _, theirs = halves_of(outs[w], halves[w])
            _remote(theirs, theirs, fin_send.at[w], fin_recv.at[w], sibling).wait_recv()
        for i in range(n_s):
            _, theirs = halves_of(share_out[i], share_in[i].shape[0])
            _remote(share_in[i], theirs, share_send.at[i], share_recv.at[i], sibling).wait_recv()

        for j in range(N_DEV - 1):
            peer = (me + N_DEV - 1 - j) % N_DEV
            _remote(small_all.at[peer], small_all.at[peer], small_send.at[j], small_recv.at[j],
                    sibling).wait_recv()
        total = small_all[0]
        for dev in range(1, N_DEV):
            total = total + small_all[dev]
        small_out[...] = total
        for cp in pending:
            cp.wait_send()
        for cp in local:
            cp.wait()

    def sems(n):
        return pltpu.SemaphoreType.DMA((max(n, 1),))

    sds = jax.ShapeDtypeStruct
    out_shape = ([sds((2 * h, cols), F32) for h in halves] + [sds((rows_s, 128), F32)]
                 + [sds((2 * sh.shape[0], cols), F32) for sh in shares]
                 + [sds((3, h, cols), wire) for h in halves])
    res = pl.pallas_call(
        body, name="finish_reduction",
        in_specs=[HBM_SPEC] * n_w + [VMEM_SPEC] + [HBM_SPEC] * n_s,
        out_specs=[HBM_SPEC] * n_w + [VMEM_SPEC] + [HBM_SPEC] * (n_s + n_w),
        out_shape=out_shape,
        scratch_shapes=[
            pltpu.VMEM((N_DEV, rows_s, 128), F32), pltpu.VMEM((ch, cols), wire), pltpu.VMEM((ch, cols), F32),
            pltpu.SemaphoreType.DMA((max(n_w, 1), 3)), pltpu.SemaphoreType.DMA((max(n_w, 1), 3)),
            sems(n_w), sems(n_w), sems(N_DEV - 1), sems(N_DEV - 1), sems(n_s), sems(n_s), sems(n_s),
            pltpu.SemaphoreType.DMA,
        ],
        compiler_params=pltpu.CompilerParams(vmem_limit_bytes=VMEM_LIMIT_MIN),
    )(*chip_sums, small, *shares)
    return res[:n_w], res[n_w + 1:n_w + 1 + n_s], res[n_w]


def _gather_weights(shards):
    n_w = len(shards)
    halves = [s.shape[0] // 2 for s in shards]

    def body(*refs):
        ins, outs = refs[:n_w], refs[n_w:2 * n_w]
        local_sems, send_sems, recv_sems, fsend_sems, frecv_sems = refs[2 * n_w:]
        x, y, c = _mesh_position()
        chip = 2 * x + y
        sibling = (x, y, 1 - c)
        others = _other_chips(x, y)

        def region(w, owner_chip, half):
            start = owner_chip * (2 * halves[w]) + half * halves[w]
            return outs[w].at[pl.ds(pl.multiple_of(start, 16), halves[w])]

        def remote(src, dst, ssem, rsem, dev):
            return pltpu.make_async_remote_copy(src_ref=src, dst_ref=dst, send_sem=ssem, recv_sem=rsem,
                                                device_id=dev, device_id_type=MESH)

        local = []
        for w in range(n_w):
            cp = pltpu.make_async_copy(ins[w], outs[w].at[pl.ds(pl.multiple_of(chip * 2 * halves[w], 16),
                                                                2 * halves[w])], local_sems.at[w])
            cp.start()
            local.append(cp)
        sends = []
        for w in range(n_w):
            mine = ins[w].at[pl.ds(pl.multiple_of(c * halves[w], 16), halves[w])]
            for k, (px, py) in enumerate(others):
                cp = remote(mine, region(w, chip, c), send_sems.at[w, k], recv_sems.at[w, k], (px, py, c))
                cp.start()
                sends.append(cp)
        for k, (px, py) in enumerate(others):
            for w in range(n_w):
                landed = region(w, 2 * px + py, c)
                remote(landed, landed, send_sems.at[w, k], recv_sems.at[w, k], (px, py, c)).wait_recv()
                cp = remote(landed, landed, fsend_sems.at[w, k], frecv_sems.at[w, k], sibling)
                cp.start()
                sends.append(cp)
        for k, (px, py) in enumerate(others):
            for w in range(n_w):
                passed = region(w, 2 * px + py, 1 - c)
                remote(passed, passed, fsend_sems.at[w, k], frecv_sems.at[w, k], sibling).wait_recv()
        for cp in sends:
            cp.wait_send()
        for cp in local:
            cp.wait()

    return pl.pallas_call(
        body, name="gather_weights",
        in_specs=[HBM_SPEC] * n_w, out_specs=[HBM_SPEC] * n_w,
        out_shape=[jax.ShapeDtypeStruct((N_CHIPS * s.shape[0], s.shape[1]), s.dtype) for s in shards],
        scratch_shapes=[pltpu.SemaphoreType.DMA((n_w,))] + [pltpu.SemaphoreType.DMA((n_w, 3))] * 4,
    )(*shards)


REDUCE_CHUNK_ROWS = 256


def _reduce_gradients(grads, shares, small):
    n_w = len(grads)
    rows = [g.shape[0] // N_CHIPS for g in grads]
    halves = [r // 2 for r in rows]
    cols = grads[0].shape[1]
    wire = grads[0].dtype
    ch = REDUCE_CHUNK_ROWS
    for h in halves:
        assert h % 16 == 0
    rows_s = small.shape[0]
    n_s = len(shares)

    def body(*refs):
        g_in = refs[:n_w]
        small_ref = refs[n_w]
        share_in = refs[n_w + 1:n_w + 1 + n_s]
        refs = refs[:n_w + 1] + refs[n_w + 1 + n_s:]
        outs = refs[n_w + 1:2 * n_w + 1]
        small_out = refs[2 * n_w + 1]
        from_sib = refs[2 * n_w + 2:3 * n_w + 2]
        chip_sum = refs[3 * n_w + 2:4 * n_w + 2]
        from_chips = refs[4 * n_w + 2:5 * n_w + 2]
        share_out = refs[5 * n_w + 2:5 * n_w + 2 + n_s]
        (small_all, buf_a, buf_b, buf_o, sib_send, sib_recv, chip_send, chip_recv,
         fin_send, fin_recv, small_send, small_recv, io_sem,
         share_send, share_recv, share_local) = refs[5 * n_w + 2 + n_s:]
        x, y, c = _mesh_position()
        chip = 2 * x + y
        me = 4 * x + 2 * y + c
        sibling = (x, y, 1 - c)
        others = _other_chips(x, y)

        def remote(src, dst, ssem, rsem, dev):
            return pltpu.make_async_remote_copy(src_ref=src, dst_ref=dst, send_sem=ssem, recv_sem=rsem,
                                                device_id=dev, device_id_type=MESH)

        def part(w, owner_chip, half):
            start = owner_chip * rows[w] + half * halves[w]
            return g_in[w].at[pl.ds(pl.multiple_of(start, 16), halves[w])]

        pending = []
        small_all[me] = small_ref[...]
        for j in range(N_DEV - 1):
            peer = (me + 1 + j) % N_DEV
            cp = remote(small_all.at[me], small_all.at[me], small_send.at[j], small_recv.at[j],
                        (peer // 4, (peer // 2) % 2, peer % 2))
            cp.start()
            pending.append(cp)

        local = []
        for i in range(n_s):
            half_rows = share_in[i].shape[0]
            place = share_out[i].at[pl.ds(pl.multiple_of(c * half_rows, 16), half_rows)]
            cp = pltpu.make_async_copy(share_in[i], place, share_local.at[i])
            cp.start()
            local.append(cp)
            cp = remote(share_in[i], place, share_send.at[i], share_recv.at[i], sibling)
            cp.start()
            pending.append(cp)

        for w in range(n_w):
            for k in range(N_CHIPS):
                cp = remote(part(w, k, 1 - c), from_sib[w].at[k], sib_send.at[w, k], sib_recv.at[w, k], sibling)
                cp.start()
                pending.append(cp)

        def add_stream(w, srcs, dst, n_rows):
            def chunk(start, size):
                total = None
                for i, src in enumerate(srcs):
                    buf = buf_a if i % 2 == 0 else buf_b
                    cp = pltpu.make_async_copy(src.at[pl.ds(start, size)], buf.at[pl.ds(0, size)], io_sem)
                    cp.start()
                    cp.wait()
                    val = buf[pl.ds(0, size), :].astype(F32)
                    total = val if total is None else total + val
                return total

            n_full = n_rows // ch
            rem = n_rows - n_full * ch

            def store(total, start, size):
                if dst.dtype == F32:
                    buf_o[pl.ds(0, size), :] = total
                    cp = pltpu.make_async_copy(buf_o.at[pl.ds(0, size)], dst.at[pl.ds(start, size)], io_sem)
                else:
                    buf_a[pl.ds(0, size), :] = total.astype(buf_a.dtype)
                    cp = pltpu.make_async_copy(buf_a.at[pl.ds(0, size)], dst.at[pl.ds(start, size)], io_sem)
                cp.start()
                cp.wait()

            def loop_body(i, carry):
                start = pl.multiple_of(i * ch, ch)
                store(chunk(start, ch), start, ch)
                return carry

            lax.fori_loop(0, n_full, loop_body, 0)
            if rem:
                store(chunk(n_full * ch, rem), n_full * ch, rem)

        order = [2, 0, 1]
        for w in range(n_w):
            for k in range(N_CHIPS):
                remote(part(w, k, 1 - c), from_sib[w].at[k], sib_send.at[w, k], sib_recv.at[w, k],
                       sibling).wait_recv()
        for k in order:
            px, py = others[k]
            owner = 2 * px + py
            for w in range(n_w):
                add_stream(w, [part(w, owner, c), from_sib[w].at[owner]], chip_sum[w].at[owner], halves[w])
                cp = remote(chip_sum[w].at[owner], from_chips[w].at[k], chip_send.at[w, k],
                            chip_recv.at[w, k], (px, py, c))
                cp.start()
                pending.append(cp)
        for w in range(n_w):
            add_stream(w, [part(w, chip, c), from_sib[w].at[chip]], chip_sum[w].at[chip], halves[w])

        for w in range(n_w):
            for k in range(3):
                px, py = others[k]
                remote(chip_sum[w].at[chip], from_chips[w].at[k], chip_send.at[w, k], chip_recv.at[w, k],
                       (px, py, c)).wait_recv()
            mine = outs[w].at[pl.ds(pl.multiple_of(c * halves[w], 16), halves[w])]
            add_stream(w, [chip_sum[w].at[chip], from_chips[w].at[0], from_chips[w].at[1],
                           from_chips[w].at[2]], mine, halves[w])
            cp = remote(mine, mine, fin_send.at[w], fin_recv.at[w], sibling)
            cp.start()
            pending.append(cp)
        for w in range(n_w):
            theirs = outs[w].at[pl.ds(pl.multiple_of((1 - c) * halves[w], 16), halves[w])]
            remote(theirs, theirs, fin_send.at[w], fin_recv.at[w], sibling).wait_recv()

        for j in range(N_DEV - 1):
            peer = (me + N_DEV - 1 - j) % N_DEV
            remote(small_all.at[peer], small_all.at[peer], small_send.at[j], small_recv.at[j],
                   sibling).wait_recv()
        total = small_all[0]
        for d in range(1, N_DEV):
            total = total + small_all[d]
        small_out[...] = total
        for i in range(n_s):
            half_rows = share_in[i].shape[0]
            theirs = share_out[i].at[pl.ds(pl.multiple_of((1 - c) * half_rows, 16), half_rows)]
            remote(share_in[i], theirs, share_send.at[i], share_recv.at[i], sibling).wait_recv()
        for cp in pending:
            cp.wait_send()
        for cp in local:
            cp.wait()

    hbm_scratch = ([jax.ShapeDtypeStruct((N_CHIPS, h, cols), wire) for h in halves] * 2
                   + [jax.ShapeDtypeStruct((3, h, cols), wire) for h in halves])
    out_shape = ([jax.ShapeDtypeStruct((r, cols), F32) for r in rows]
                 + [jax.ShapeDtypeStruct((rows_s, 128), F32)] + hbm_scratch
                 + [jax.ShapeDtypeStruct((2 * sh.shape[0], sh.shape[1]), F32) for sh in shares])
    res = pl.pallas_call(
        body, name="reduce_gradients",
        in_specs=[HBM_SPEC] * n_w + [VMEM_SPEC] + [HBM_SPEC] * n_s,
        out_specs=[HBM_SPEC] * n_w + [VMEM_SPEC] + [HBM_SPEC] * (3 * n_w + n_s),
        out_shape=out_shape,
        scratch_shapes=[
            pltpu.VMEM((N_DEV, rows_s, 128), F32),
            pltpu.VMEM((ch, cols), wire), pltpu.VMEM((ch, cols), wire), pltpu.VMEM((ch, cols), F32),
            pltpu.SemaphoreType.DMA((n_w, N_CHIPS)), pltpu.SemaphoreType.DMA((n_w, N_CHIPS)),
            pltpu.SemaphoreType.DMA((n_w, 3)), pltpu.SemaphoreType.DMA((n_w, 3)),
            pltpu.SemaphoreType.DMA((n_w,)), pltpu.SemaphoreType.DMA((n_w,)),
            pltpu.SemaphoreType.DMA((N_DEV - 1,)), pltpu.SemaphoreType.DMA((N_DEV - 1,)),
            pltpu.SemaphoreType.DMA,
            pltpu.SemaphoreType.DMA((max(n_s, 1),)), pltpu.SemaphoreType.DMA((max(n_s, 1),)),
            pltpu.SemaphoreType.DMA((max(n_s, 1),)),
        ],
        compiler_params=pltpu.CompilerParams(vmem_limit_bytes=VMEM_LIMIT_MIN),
    )(*grads, small, *shares)
    return res[:n_w], res[n_w], res[len(res) - n_s:] if n_s else []


def _pack_small(parts, rows):
    flat = jnp.concatenate([p.reshape(-1) for p in parts])
    flat = jnp.pad(flat, (0, rows * 128 - flat.shape[0]))
    return flat.reshape(rows, 128)


def _unpack_small(packed, shapes):
    flat = packed.reshape(-1)
    out, off = [], 0
    for shp in shapes:
        n = int(np.prod(shp))
        out.append(flat[off:off + n].reshape(shp))
        off += n
    return out


def kernel(x, g_attn, w_in, b_in, sinks_a, g_out_a, g_out_b, w_out, g_mlp, w_1, w_2, g_final, loss_target, m_g_attn, m_w_in, m_b_in, m_sinks_a, m_g_out_a, m_g_out_b, m_w_out, m_g_mlp, m_w_1, m_w_2, m_g_final, v_g_attn, v_w_in, v_b_in, v_sinks_a, v_g_out_a, v_g_out_b, v_w_out, v_g_mlp, v_w_1, v_w_2, v_g_final):
    s, d = x.shape[1], x.shape[2]
    d_in = b_in.shape[1]
    qa = g_out_a.shape[1]
    qb = g_out_b.shape[1]
    kva = 2 * N_KV_GROUPS * HEAD_DIM
    assert d_in == qa + kva + 3 * qb and qa + qb == w_out.shape[1] * N_CHIPS
    d_ff = w_1.shape[2] * N_CHIPS
    ff_shard = w_1.shape[2]
    in_shard = w_in.shape[2]
    n_heads_a, n_heads_b = qa // HEAD_DIM, qb // HEAD_DIM
    slopes_a, slopes_b = alibi_slopes(n_heads_a), alibi_slopes(n_heads_b)

    x2d = x[0]
    target = loss_target[0]

    shards = [w_in[0].T.astype(BF16), w_out[0].astype(BF16), w_1[0].astype(BF16), w_2[0].astype(BF16)]
    core_index = lax.axis_index("c").astype(jnp.int32).reshape(1)
    chip_index = (2 * lax.axis_index("x") + lax.axis_index("y")).astype(jnp.int32).reshape(1)

    tm = _tile(s, 1024)

    (h1, r1), (w_in_t,) = _norm_fwd("norm_attn", x2d, g_attn,
                                    hook=_gather_hook(shards[0], None, 0, shards[0].shape[0] // 2))

    q_a, = _project_by_class("proj_qa", h1, w_in_t, b_in, 0, qa, (1,))
    kv_a, = _project_by_class("proj_kva", h1, w_in_t, b_in, qa, kva, (1,))
    q_bs, (w_out_g,) = _project_by_class("proj_qb", h1, w_in_t, b_in, qa + kva, qb, DILATIONS,
                                         hook=_gather_hook(shards[1], None, 0, shards[1].shape[0] // 2))
    k_bs = _project_by_class("proj_kb", h1, w_in_t, b_in, qa + kva + qb, qb, DILATIONS)
    v_bs = _project_by_class("proj_vb", h1, w_in_t, b_in, qa + kva + 2 * qb, qb, DILATIONS)

    quarter = shards[2].shape[0] // 8
    sinks = sinks_a.reshape(-1)
    (o_a, lse_a), (w_1_g,) = _attn_fwd("attn_a_fwd", q_a, kv_a, dil=1, max_steps=WINDOW_A - 1, slopes=slopes_a,
                                       sinks=sinks, hook=_gather_hook(shards[2], None, 0, quarter))
    o_a = o_a[0]
    o_bs, lse_bs = [], []
    for n, (window, dil) in enumerate(DILATED_BRANCHES):
        (o, l), (w_1_g,) = _attn_fwd(f"attn_b{dil}_fwd", q_bs[n], (k_bs[n], v_bs[n]), dil=dil,
                                     max_steps=window // dil, slopes=slopes_b,
                                     hook=_gather_hook(shards[2], w_1_g, (n + 1) * quarter, (n + 2) * quarter))
        o_bs.append(o)
        lse_bs.append(l)
    w_1_g = w_1_g.reshape(N_CHIPS, d, ff_shard)
    mix, o_b, *lse_tot, r_a, r_b = _mix_fwd(o_a, o_bs, lse_bs, g_out_a, g_out_b)

    tn = _tile(d, 512)
    a_spec, b_spec = _mm_specs("nn", tm, tn, d)
    tile_mn = pl.BlockSpec((tm, tn), lambda i, j, k: (i, j))
    x2 = _matmul("out_proj", mix, w_out_g, [x2d], mode="nn", grid=(s // tm, d // tn, 1),
                 a_spec=a_spec, b_spec=b_spec, extra_specs=[tile_mn],
                 out_shapes=[jax.ShapeDtypeStruct((s, d), F32)], out_specs=[tile_mn],
                 epilogue=lambda acc, res: (acc + res,))[0]

    h2, r2 = _norm_fwd("norm_mlp", x2, g_mlp)

    tn = _tile(ff_shard, 512)
    per = ff_shard // tn
    a_spec, _ = _mm_specs("nn", tm, tn, d)
    (u,), (w_2_g,) = _matmul(
        "mlp_up", h2, w_1_g, [], mode="nn", grid=(s // tm, d_ff // tn, 1),
        a_spec=a_spec, b_spec=pl.BlockSpec((None, d, tn), lambda i, j, k: (j // per, 0, j % per)),
        extra_specs=[], out_shapes=[jax.ShapeDtypeStruct((s, d_ff), BF16)], out_specs=[tile_mn],
        epilogue=lambda acc: (jnp.maximum(acc, 0.0),),
        hook=_gather_hook(shards[3], None, 0, shards[3].shape[0] // 2))

    tn = _tile(d, 1024)
    tk = _tile(d_ff, 2048)
    a_spec, b_spec = _mm_specs("nn", tm, tn, tk)
    tile_mn = pl.BlockSpec((tm, tn), lambda i, j, k: (i, j))
    x3 = _matmul("mlp_down", u, w_2_g, [x2], mode="nn", grid=(s // tm, d // tn, d_ff // tk),
                 a_spec=a_spec, b_spec=b_spec, extra_specs=[tile_mn],
                 out_shapes=[jax.ShapeDtypeStruct((s, d), F32)], out_specs=[tile_mn],
                 prologue=lambda a: a * a, epilogue=lambda acc, res: (acc + res,), acc_shape=(tm, tn))[0]

    dx3, dx3b, loss_part, dg_final = _loss_head(x3, target, g_final.reshape(1, d))

    tn = _tile(d_ff, 512)
    a_spec, b_spec = _mm_specs("nt", tm, tn, d)
    tile_mn = pl.BlockSpec((tm, tn), lambda i, j, k: (i, j))
    dpre = _matmul("mlp_down_dx", dx3b, w_2_g, [u], mode="nt", grid=(s // tm, d_ff // tn, 1),
                   a_spec=a_spec, b_spec=b_spec, extra_specs=[tile_mn],
                   out_shapes=[jax.ShapeDtypeStruct((s, d_ff), BF16)], out_specs=[tile_mn],
                   epilogue=lambda acc, uu: (acc * (2.0 * uu.astype(F32)),))[0]

    wire = GRAD_WIRE_DTYPE
    tk_s = _tile(s, 2048)
    tmw = _tile(d_ff, 1024)
    a_spec, b_spec = _mm_specs("tn", tmw, d, tk_s)
    dw_2 = _matmul("mlp_down_dw", u, dx3b, [], mode="tn", grid=(d_ff // tmw, 1, s // tk_s),
                   a_spec=a_spec, b_spec=b_spec, extra_specs=[],
                   out_shapes=[jax.ShapeDtypeStruct((d_ff, d), wire)],
                   out_specs=[pl.BlockSpec((tmw, d), lambda i, j, k: (i, j))],
                   prologue=lambda a: a * a, epilogue=lambda acc: (acc,), acc_shape=(tmw, d))[0]

    tn = _tile(d, 1024)
    tk = _tile(ff_shard, 2048)
    per = ff_shard // tk
    a_spec, _ = _mm_specs("nt", tm, tn, tk)
    tile_mn = pl.BlockSpec((tm, tn), lambda i, j, k: (i, j))
    dh2 = _matmul("mlp_up_dx", dpre, w_1_g, [], mode="nt", grid=(s // tm, d // tn, d_ff // tk),
                  a_spec=a_spec, b_spec=pl.BlockSpec((None, tn, tk), lambda i, j, k: (k // per, j, k % per)),
                  extra_specs=[], out_shapes=[jax.ShapeDtypeStruct((s, d), F32)], out_specs=[tile_mn],
                  epilogue=lambda acc: (acc,), acc_shape=(tm, tn))[0]

    tmw = _tile(d, 1024)
    tnw = _tile(ff_shard, 2048)
    per = ff_shard // tnw
    a_spec, b_spec = _mm_specs("tn", tmw, tnw, tk_s)
    dw_1 = _matmul("mlp_up_dw", h2, dpre, [], mode="tn", grid=(d // tmw, d_ff // tnw, s // tk_s),
                   a_spec=a_spec, b_spec=b_spec, extra_specs=[],
                   out_shapes=[jax.ShapeDtypeStruct((N_CHIPS, d, ff_shard), wire)],
                   out_specs=[pl.BlockSpec((None, tmw, tnw), lambda i, j, k: (j // per, i, j % per))],
                   epilogue=lambda acc: (acc,), acc_shape=(tmw, tnw))[0]

    dw_1 = dw_1.reshape(N_CHIPS * d, ff_shard)
    (dx2, dx2b, dg_mlp), (sib_2, sib_1) = _norm_bwd(
        "norm_mlp_bwd", dh2, x2, r2, g_mlp, dx3, hook=_merge_hooks([_exchange_hook(dw_2), _exchange_hook(dw_1)]))
    chip_sum_2 = _chip_add("chip_add_w_2", dw_2, sib_2, core_index)
    chip_sum_1 = _chip_add("chip_add_w_1", dw_1, sib_1, core_index)

    tn = _tile(d, 512)
    a_spec, b_spec = _mm_specs("nt", tm, tn, d)
    tile_mn = pl.BlockSpec((tm, tn), lambda i, j, k: (i, j))
    dmix = _matmul("out_proj_dx", dx2b, w_out_g, [], mode="nt", grid=(s // tm, d // tn, 1),
                   a_spec=a_spec, b_spec=b_spec, extra_specs=[],
                   out_shapes=[jax.ShapeDtypeStruct((s, d), F32)], out_specs=[tile_mn],
                   epilogue=lambda acc: (acc,))[0]

    tmw = _tile(d, 1024)
    a_spec, b_spec = _mm_specs("tn", tmw, d, tk_s)
    dw_out = _matmul("out_proj_dw", mix, dx2b, [], mode="tn", grid=(d // tmw, 1, s // tk_s),
                     a_spec=a_spec, b_spec=b_spec, extra_specs=[],
                     out_shapes=[jax.ShapeDtypeStruct((d, d), wire)],
                     out_specs=[pl.BlockSpec((tmw, d), lambda i, j, k: (i, j))],
                     epilogue=lambda acc: (acc,), acc_shape=(tmw, d))[0]

    mix_grads, (sib_out,) = _mix_bwd(dmix, o_a, o_b, r_a, r_b, g_out_a, g_out_b, hook=_exchange_hook(dw_out))
    do_a, do_bs, delta_a, delta_bs = mix_grads[0], mix_grads[1:4], mix_grads[4], mix_grads[5:8]
    dg_out_a, dg_out_b = mix_grads[8:]
    chip_sum_out = _chip_add("chip_add_w_out", dw_out, sib_out, core_index)

    (dq_a, dkv_a, dsinks), (chips_2,) = _attn_bwd(
        "attn_a_bwd", q_a, kv_a, do_a[None], lse_a, delta_a[None], dil=1, max_steps=WINDOW_A - 1,
        slopes=slopes_a, sinks=sinks, hook=_scatter_hook(chip_sum_2))
    dqs, dks, dvs = [], [], []
    scatter = {1: chip_sum_1, 4: chip_sum_out}
    arrived = {}
    for n, (window, dil) in enumerate(DILATED_BRANCHES):
        res = _attn_bwd(f"attn_b{dil}_bwd", q_bs[n], (k_bs[n], v_bs[n]), do_bs[n], lse_tot[n],
                        delta_bs[n], dil=dil, max_steps=window // dil, slopes=slopes_b,
                        hook=_scatter_hook(scatter[dil]) if dil in scatter else None)
        if dil in scatter:
            res, (arrived[dil],) = res
        dq, dk, dv = res
        dqs.append(dq)
        dks.append(dk)
        dvs.append(dv)
    half_2 = _final_add("final_add_w_2", chip_sum_2, chips_2, chip_index)
    half_1 = _final_add("final_add_w_1", chip_sum_1, arrived[1], chip_index)
    half_out = _final_add("final_add_w_out", chip_sum_out, arrived[4], chip_index)
    dproj, db_in = _assemble_dproj(dq_a[0], dkv_a[0], dqs, dks, dvs)

    tmw = d_in // 2 if (d_in // 2) % 128 == 0 else d_in
    tnw = _tile(d, 1024)
    tk_s = _tile(s, 1024)
    a_spec, b_spec = _mm_specs("tn", tmw, tnw, tk_s)
    (dw_in_t,), (gw_out, gw_1, gw_2) = _matmul(
        "in_proj_dw", dproj, h1, [], mode="tn", grid=(d_in // tmw, d // tnw, s // tk_s),
        a_spec=a_spec, b_spec=b_spec, extra_specs=[],
        out_shapes=[jax.ShapeDtypeStruct((d_in, d), wire)],
        out_specs=[pl.BlockSpec((tmw, tnw), lambda i, j, k: (i, j))],
        epilogue=lambda acc: (acc,), acc_shape=(tmw, tnw),
        hook=_merge_hooks([_share_hook(half_out), _share_hook(half_1), _share_hook(half_2)]))

    tn = _tile(d, 512)
    a_spec, b_spec = _mm_specs("nn", tm, tn, d_in)
    tile_mn = pl.BlockSpec((tm, tn), lambda i, j, k: (i, j))
    (dh1,), (sib_in,) = _matmul("in_proj_dx", dproj, w_in_t, [], mode="nn", grid=(s // tm, d // tn, 1),
                                a_spec=a_spec, b_spec=b_spec, extra_specs=[],
                                out_shapes=[jax.ShapeDtypeStruct((s, d), F32)], out_specs=[tile_mn],
                                epilogue=lambda acc: (acc,), hook=_exchange_hook(dw_in_t))
    chip_sum_in = _chip_add("chip_add_w_in", dw_in_t, sib_in, core_index)

    (grad_x, _, dg_attn), (chips_in,) = _norm_bwd("norm_attn_bwd", dh1, x2d, r1, g_attn, dx2,
                                                  hook=_scatter_hook(chip_sum_in))
    half_in = _final_add("final_add_w_in", chip_sum_in, chips_in, chip_index)

    small_parts = [dg_attn, db_in, dsinks[:, :n_heads_a], dg_out_a, dg_out_b, dg_mlp, dg_final]
    small_shapes = [g_attn.shape, b_in.shape, sinks_a.shape, g_out_a.shape, g_out_b.shape, g_mlp.shape,
                    g_final.shape]
    n_small = sum(int(np.prod(shp)) for shp in small_shapes)
    rows_s = -(-n_small // (8 * 128)) * 8
    _, (gw_in_t,), small_sum = _finish_reduction([], [half_in], _pack_small(small_parts, rows_s))
    gw_in = gw_in_t.T
    g_small = _unpack_small(small_sum, small_shapes)

    upd_in = _adamw("adamw_w_in", w_in[0], gw_in, m_w_in[0], v_w_in[0])
    upd_out = _adamw("adamw_w_out", w_out[0], gw_out, m_w_out[0], v_w_out[0])
    upd_1 = _adamw("adamw_w_1", w_1[0], gw_1, m_w_1[0], v_w_1[0])
    upd_2 = _adamw("adamw_w_2", w_2[0], gw_2, m_w_2[0], v_w_2[0])
    small_w = [g_attn, b_in, sinks_a, g_out_a, g_out_b, g_mlp, g_final]
    small_m = [m_g_attn, m_b_in, m_sinks_a, m_g_out_a, m_g_out_b, m_g_mlp, m_g_final]
    small_v = [v_g_attn, v_b_in, v_sinks_a, v_g_out_a, v_g_out_b, v_g_mlp, v_g_final]
    upd_small = _adamw("adamw_small", _pack_small(small_w, rows_s), small_sum,
                       _pack_small(small_m, rows_s), _pack_small(small_v, rows_s))
    d_small, m_small, v_small = [_unpack_small(t, small_shapes) for t in upd_small]

    loss = lax.psum(loss_part[0, 0], ("x", "y", "c"))

    def ordered(small, big):
        w_in_v, w_out_v, w_1_v, w_2_v = big
        return [small[0], w_in_v[None], small[1], small[2], small[3], small[4], w_out_v[None], small[5],
                w_1_v[None], w_2_v[None], small[6]]

    grads = ordered(g_small, (gw_in, gw_out, gw_1, gw_2))
    deltas = ordered(d_small, (upd_in[0], upd_out[0], upd_1[0], upd_2[0]))
    new_m = ordered(m_small, (upd_in[1], upd_out[1], upd_1[1], upd_2[1]))
    new_v = ordered(v_small, (upd_in[2], upd_out[2], upd_1[2], upd_2[2]))
    return (loss, grad_x[None], *grads, *deltas, *new_m, *new_v)
```

```python
import functools

import jax
import jax.numpy as jnp
import numpy as np
from jax import lax
from jax.experimental import pallas as pl
from jax.experimental.pallas import tpu as pltpu

F32 = jnp.float32
BF16 = jnp.bfloat16

HEAD_DIM = 64
BLOCK = 128
PAIR = 2 * HEAD_DIM
N_KV_GROUPS = 2
WINDOW_A = 128
DILATED_BRANCHES = ((128, 1), (512, 4), (2048, 16))
EPS = 1e-5
NEG_INF = -1e30
ATT_SCALE = HEAD_DIM ** -0.5

ADAM_LR = 0.001
ADAM_B1 = 0.9
ADAM_B2 = 0.999
ADAM_EPS = 1e-08
ADAM_WD = 0.01
ADAM_STEP = 10

N_CHIPS = 4
N_DEV = 8
MESH = pl.DeviceIdType.MESH
GRAD_WIRE_DTYPE = jnp.bfloat16
BRANCH_DTYPE = jnp.bfloat16

VMEM_CAPACITY_V7X = 64 * 1024 * 1024
VMEM_LIMIT_MAX = 56 * 1024 * 1024
VMEM_LIMIT_MIN = 48 * 1024 * 1024

HBM_SPEC = pl.BlockSpec(memory_space=pltpu.HBM)
VMEM_SPEC = pl.BlockSpec(memory_space=pltpu.VMEM)
SMEM_SPEC = pl.BlockSpec(memory_space=pltpu.SMEM)


def _nbytes(shape, dtype):
    return int(np.prod([s for s in shape if s is not None])) * jnp.dtype(dtype).itemsize


def _params(semantics, block_bytes):
    limit = min(max(2 * block_bytes + (4 << 20), VMEM_LIMIT_MIN), VMEM_LIMIT_MAX)
    return pltpu.CompilerParams(dimension_semantics=semantics, vmem_limit_bytes=limit)


class _Hook:
    def __init__(self, operands, out_shape, sems, start, finish, mid=None, aliases=None):
        self.operands, self.out_shape, self.sems = list(operands), list(out_shape), list(sems)
        self.start, self.mid, self.finish = start, mid, finish
        self.aliases = dict(aliases or {})


def _merge_hooks(hooks):
    hooks = [h for h in hooks if h is not None]
    if len(hooks) <= 1:
        return hooks[0] if hooks else None
    n_op = np.cumsum([0] + [len(h.operands) for h in hooks])
    n_out = np.cumsum([0] + [len(h.out_shape) for h in hooks])
    n_sem = np.cumsum([0] + [len(h.sems) for h in hooks])

    def run(which):
        def fn(ops, outs, sems):
            for i, h in enumerate(hooks):
                f = getattr(h, which)
                if f is not None:
                    f(ops[n_op[i]:n_op[i + 1]], outs[n_out[i]:n_out[i + 1]], sems[n_sem[i]:n_sem[i + 1]])
        return fn

    aliases = {}
    for i, h in enumerate(hooks):
        aliases.update({int(n_op[i]) + a: int(n_out[i]) + b for a, b in h.aliases.items()})
    return _Hook(sum([h.operands for h in hooks], []), sum([h.out_shape for h in hooks], []),
                 sum([h.sems for h in hooks], []), run("start"), run("finish"),
                 run("mid") if any(h.mid for h in hooks) else None, aliases)


HOOK_MID_FRACTION = 0.6


def _call(body, hook, *, name, grid, in_specs, out_specs, out_shape, scratch_shapes=(), compiler_params):
    in_specs, out_specs, out_shape = list(in_specs), list(out_specs), list(out_shape)
    scratch_shapes = list(scratch_shapes)
    if hook is None:
        call = pl.pallas_call(body, name=name, grid=grid, in_specs=in_specs, out_specs=out_specs,
                              out_shape=out_shape, scratch_shapes=scratch_shapes,
                              compiler_params=compiler_params)
        return lambda *operands: (call(*operands), [])
    n_in, n_hin, n_out, n_hout, n_scr = (len(in_specs), len(hook.operands), len(out_specs),
                                         len(hook.out_shape), len(scratch_shapes))
    total = int(np.prod(grid))
    t_mid = min(int(total * HOOK_MID_FRACTION), total - 1)

    def wrapped(*refs):
        ins, h_in = refs[:n_in], refs[n_in:n_in + n_hin]
        o0 = n_in + n_hin
        outs, h_out = refs[o0:o0 + n_out], refs[o0 + n_out:o0 + n_out + n_hout]
        s0 = o0 + n_out + n_hout
        scr, h_sems = refs[s0:s0 + n_scr], refs[s0 + n_scr:]
        t = pl.program_id(0)
        for axis in range(1, len(grid)):
            t = t * grid[axis] + pl.program_id(axis)

        @pl.when(t == 0)
        def _():
            hook.start(h_in, h_out, h_sems)

        body(*ins, *outs, *scr)
        if hook.mid is not None:
            @pl.when(t == t_mid)
            def _():
                hook.mid(h_in, h_out, h_sems)

        @pl.when(t == total - 1)
        def _():
            hook.finish(h_in, h_out, h_sems)

    params = pltpu.CompilerParams(dimension_semantics=("arbitrary",) * len(grid),
                                  vmem_limit_bytes=compiler_params.vmem_limit_bytes)
    call = pl.pallas_call(
        wrapped, name=name, grid=grid,
        in_specs=in_specs + [HBM_SPEC] * n_hin, out_specs=out_specs + [HBM_SPEC] * n_hout,
        out_shape=out_shape + hook.out_shape, scratch_shapes=scratch_shapes + hook.sems,
        input_output_aliases={n_in + a: n_out + b for a, b in hook.aliases.items()},
        compiler_params=params)

    def run(*operands):
        res = call(*operands, *hook.operands)
        return res[:n_out], res[n_out:]

    return run


def _remote(src, dst, send_sem, recv_sem, device):
    return pltpu.make_async_remote_copy(src_ref=src, dst_ref=dst, send_sem=send_sem, recv_sem=recv_sem,
                                        device_id=device, device_id_type=MESH)


def alibi_slopes(n):
    return [float(v) for v in np.asarray(2.0 ** (-8.0 * (np.arange(n) + 1) / n), dtype=np.float32)]


def _matmul(name, a, b, extras, *, mode, grid, a_spec, b_spec, extra_specs, out_shapes, out_specs,
            epilogue, prologue=None, acc_shape=None, hook=None):
    dims = {"nn": ((1,), (0,)), "nt": ((1,), (1,)), "tn": ((0,), (0,))}[mode]
    nk = grid[2]
    n_ex, n_out = len(extras), len(out_shapes)

    def body(a_ref, b_ref, *rest):
        ex, outs = rest[:n_ex], rest[n_ex:n_ex + n_out]
        av = a_ref[...]
        if prologue is not None:
            av = prologue(av)
        part = lax.dot_general(av, b_ref[...], (dims, ((), ())), preferred_element_type=F32)

        def finish(acc):
            res = epilogue(acc, *[e[...] for e in ex])
            for o, r in zip(outs, res):
                o[...] = r.astype(o.dtype)

        if nk == 1:
            finish(part)
        else:
            acc_ref = rest[-1]
            k = pl.program_id(2)

            @pl.when(k == 0)
            def _():
                acc_ref[...] = part

            @pl.when(k > 0)
            def _():
                acc_ref[...] += part

            @pl.when(k == nk - 1)
            def _():
                finish(acc_ref[...])

    blocks = [(a_spec.block_shape, a.dtype), (b_spec.block_shape, b.dtype)]
    blocks += [(s.block_shape, e.dtype) for s, e in zip(extra_specs, extras)]
    blocks += [(s.block_shape, o.dtype) for s, o in zip(out_specs, out_shapes)]
    nbytes = sum(_nbytes(s, d) for s, d in blocks)
    scratch = []
    if nk > 1:
        scratch.append(pltpu.VMEM(acc_shape, F32))
        nbytes += _nbytes(acc_shape, F32)
    res, hook_res = _call(
        body, hook, name=name, grid=grid,
        in_specs=[a_spec, b_spec, *extra_specs], out_specs=list(out_specs), out_shape=list(out_shapes),
        scratch_shapes=scratch,
        compiler_params=_params(("parallel", "parallel", "arbitrary"), nbytes),
    )(a, b, *extras)
    return res if hook is None else (res, hook_res)


def _mm_specs(mode, tm, tn, tk, b_block=None, b_map=None):
    if mode == "tn":
        a_spec = pl.BlockSpec((tk, tm), lambda i, j, k: (k, i))
    else:
        a_spec = pl.BlockSpec((tm, tk), lambda i, j, k: (i, k))
    if b_block is not None:
        b_spec = pl.BlockSpec(b_block, b_map)
    elif mode == "nt":
        b_spec = pl.BlockSpec((tn, tk), lambda i, j, k: (j, k))
    else:
        b_spec = pl.BlockSpec((tk, tn), lambda i, j, k: (k, j))
    return a_spec, b_spec


def _project_by_class(name, h, w_t, bias, row_off, width, dilations, hook=None):
    s, d = h.shape
    tm = _tile(s, 1024)
    tn = 512 if width % 512 == 0 and row_off % 512 == 0 else _tile(width, 256)
    off = row_off // tn
    assert row_off % tn == 0 and tn % 128 == 0
    n_out = len(dilations)

    def body(h_ref, w_ref, b_ref, *rest):
        outs, perm_ref = rest[:n_out], rest[n_out]
        acc = lax.dot_general(h_ref[...], w_ref[...], (((1,), (1,)), ((), ())), preferred_element_type=F32)
        acc = acc + b_ref[...]
        for j in range(tn // 128):
            cols = slice(j * 128, (j + 1) * 128)
            for o_ref, dil in zip(outs, dilations):
                _to_classes(o_ref, cols, acc[:, cols], perm_ref, dil)

    blocks = tm * d * 2 + tn * d * 2 + 3 * tm * tn * 2 + tm * 128 * 4
    res, hook_res = _call(
        body, hook, name=name, grid=(s // tm, width // tn),
        in_specs=[pl.BlockSpec((tm, d), lambda i, j: (i, 0)), pl.BlockSpec((tn, d), lambda i, j: (j + off, 0)),
                  pl.BlockSpec((1, tn), lambda i, j: (0, j + off))],
        out_specs=[pl.BlockSpec((dil, tm // dil, tn), lambda i, j: (0, i, j)) for dil in dilations],
        out_shape=[_class_shape(dil, s, width, BF16) for dil in dilations],
        scratch_shapes=[pltpu.VMEM((tm, 128), F32)],
        compiler_params=_params(("parallel", "parallel"), blocks),
    )(h, w_t, bias)
    return res if hook is None else (res, hook_res)


def _tile(n, want):
    if n <= want:
        return n
    t = (want // 128) * 128
    while t > 128 and n % t:
        t -= 128
    assert n % t == 0, (n, want)
    return t


def _row_tile(s):
    return 256 if s % 256 == 0 else s


def _norm_fwd(name, x, g, hook=None):
    s, d = x.shape
    tm = _row_tile(s)

    def body(x_ref, g_ref, h_ref, r_ref):
        xv = x_ref[...]
        r = lax.rsqrt(jnp.mean(xv * xv, axis=-1, keepdims=True) + EPS)
        h_ref[...] = ((xv * r) * g_ref[...]).astype(BF16)
        r_ref[...] = r

    row = pl.BlockSpec((tm, d), lambda i: (i, 0))
    res, hook_res = _call(
        body, hook, name=name, grid=(s // tm,),
        in_specs=[row, pl.BlockSpec((1, d), lambda i: (0, 0))],
        out_specs=[row, pl.BlockSpec((tm, 1), lambda i: (i, 0))],
        out_shape=[jax.ShapeDtypeStruct((s, d), BF16), jax.ShapeDtypeStruct((s, 1), F32)],
        compiler_params=_params(("parallel",), tm * d * 6),
    )(x, g)
    return res if hook is None else (res, hook_res)


def _norm_bwd(name, dh, x, r, g, dres, hook=None):
    s, d = x.shape
    tm = _row_tile(s)

    def body(dh_ref, x_ref, r_ref, g_ref, dres_ref, dx_ref, dxb_ref, dg_ref):
        rv = r_ref[...]
        xn = x_ref[...] * rv
        dhv = dh_ref[...]
        dxn = dhv * g_ref[...]
        dx = dres_ref[...] + rv * (dxn - xn * jnp.mean(dxn * xn, axis=-1, keepdims=True))
        dx_ref[...] = dx
        dxb_ref[...] = dx.astype(BF16)
        part = jnp.sum(dhv * xn, axis=0, keepdims=True)

        @pl.when(pl.program_id(0) == 0)
        def _():
            dg_ref[...] = part

        @pl.when(pl.program_id(0) > 0)
        def _():
            dg_ref[...] += part

    row = pl.BlockSpec((tm, d), lambda i: (i, 0))
    vec = pl.BlockSpec((1, d), lambda i: (0, 0))
    res, hook_res = _call(
        body, hook, name=name, grid=(s // tm,),
        in_specs=[row, row, pl.BlockSpec((tm, 1), lambda i: (i, 0)), vec, row],
        out_specs=[row, row, vec],
        out_shape=[jax.ShapeDtypeStruct((s, d), F32), jax.ShapeDtypeStruct((s, d), BF16),
                   jax.ShapeDtypeStruct((1, d), F32)],
        compiler_params=_params(("arbitrary",), tm * d * 18),
    )(dh, x, r, g, dres)
    return res if hook is None else (res, hook_res)


def _loss_head(x3, target, g):
    s, d = x3.shape
    tm = _row_tile(s)

    def body(x_ref, t_ref, g_ref, dx_ref, dxb_ref, loss_ref, dg_ref):
        xv = x_ref[...]
        gv = g_ref[...]
        r = lax.rsqrt(jnp.mean(xv * xv, axis=-1, keepdims=True) + EPS)
        xn = xv * r
        err = xn * gv - t_ref[...]
        loss = 0.5 * jnp.sum(jnp.mean(err * err, axis=-1, keepdims=True), axis=0, keepdims=True)
        dy = err / d
        dxn = dy * gv
        dx = r * (dxn - xn * jnp.mean(dxn * xn, axis=-1, keepdims=True))
        dx_ref[...] = dx
        dxb_ref[...] = dx.astype(BF16)
        dg = jnp.sum(dy * xn, axis=0, keepdims=True)
        loss_row = jnp.broadcast_to(loss, (1, 128))

        @pl.when(pl.program_id(0) == 0)
        def _():
            dg_ref[...] = dg
            loss_ref[...] = loss_row

        @pl.when(pl.program_id(0) > 0)
        def _():
            dg_ref[...] += dg
            loss_ref[...] += loss_row

    row = pl.BlockSpec((tm, d), lambda i: (i, 0))
    vec = pl.BlockSpec((1, d), lambda i: (0, 0))
    return _call(
        body, None, name="loss_head", grid=(s // tm,),
        in_specs=[row, row, vec],
        out_specs=[row, row, pl.BlockSpec((1, 128), lambda i: (0, 0)), vec],
        out_shape=[jax.ShapeDtypeStruct((s, d), F32), jax.ShapeDtypeStruct((s, d), BF16),
                   jax.ShapeDtypeStruct((1, 128), F32), jax.ShapeDtypeStruct((1, d), F32)],
        compiler_params=_params(("arbitrary",), tm * d * 14),
    )(x3, target, g)[0]


def _low_lanes(rows):
    return lax.broadcasted_iota(jnp.int32, (rows, PAIR), 1) < HEAD_DIM


def _to_classes(dst_ref, cols, value, perm_ref, dil):
    rows = value.shape[0]
    if dil == 1:
        dst_ref[0, :, cols] = value.astype(dst_ref.dtype)
        return
    perm_ref[...] = value
    for r in range(dil):
        dst_ref[r, :, cols] = perm_ref[pl.ds(r, rows // dil, stride=dil), :].astype(dst_ref.dtype)


def _from_classes(src_ref, cols, perm_ref, dil):
    if dil == 1:
        return src_ref[0, :, cols].astype(F32)
    rows = perm_ref.shape[0]
    for r in range(dil):
        perm_ref[pl.ds(r, rows // dil, stride=dil), :] = src_ref[r, :, cols].astype(F32)
    return perm_ref[...]


def _class_spec(dil, tm, width):
    return pl.BlockSpec((dil, tm // dil, width), lambda i: (0, i, 0))


def _class_shape(dil, s, width, dtype):
    return jax.ShapeDtypeStruct((dil, s // dil, width), dtype)


DILATIONS = tuple(d for _, d in DILATED_BRANCHES)


def _mix_fwd(oa, obs, lses, ga, gb):
    s, qa = oa.shape
    qb = obs[0].shape[2]
    tm = _row_tile(s)
    all_lanes = slice(0, 128)

    def body(oa_ref, o1_ref, o2_ref, o3_ref, l1_ref, l2_ref, l3_ref, ga_ref, gb_ref,
             mix_ref, ob_ref, t1_ref, t2_ref, t3_ref, ra_ref, rb_ref, perm_ref):
        oav = oa_ref[...]
        ra = lax.rsqrt(jnp.mean(oav * oav, axis=-1, keepdims=True) + EPS)
        ra_ref[...] = ra
        mix_ref[:, 0:qa] = ((oav * ra) * ga_ref[...]).astype(BF16)
        l1, l2, l3 = [_from_classes(l_ref, all_lanes, perm_ref, dil)
                      for l_ref, dil in zip((l1_ref, l2_ref, l3_ref), DILATIONS)]
        mx = jnp.maximum(jnp.maximum(l1, l2), l3)
        e1, e2, e3 = jnp.exp(l1 - mx), jnp.exp(l2 - mx), jnp.exp(l3 - mx)
        tot = e1 + e2 + e3
        lse = mx + jnp.log(tot)
        for t_ref, dil in zip((t1_ref, t2_ref, t3_ref), DILATIONS):
            _to_classes(t_ref, all_lanes, lse, perm_ref, dil)
        ws = (e1 / tot, e2 / tot, e3 / tot)
        low = _low_lanes(tm)
        ssq = jnp.zeros((tm, 1), F32)
        for i in range(qb // PAIR):
            sl = slice(i * PAIR, (i + 1) * PAIR)
            acc = jnp.zeros((tm, PAIR), F32)
            for w, o_ref, dil in zip(ws, (o1_ref, o2_ref, o3_ref), DILATIONS):
                wexp = jnp.where(low, w[:, 2 * i:2 * i + 1], w[:, 2 * i + 1:2 * i + 2])
                acc = acc + wexp * _from_classes(o_ref, sl, perm_ref, dil)
            ob_ref[:, sl] = acc
            ssq = ssq + jnp.sum(acc * acc, axis=-1, keepdims=True)
        rb = lax.rsqrt(ssq / qb + EPS)
        rb_ref[...] = rb
        mix_ref[:, qa:qa + qb] = ((ob_ref[...] * rb) * gb_ref[...]).astype(BF16)

    def row(w):
        return pl.BlockSpec((tm, w), lambda i: (i, 0))

    def vec(w):
        return pl.BlockSpec((1, w), lambda i: (0, 0))

    return _call(
        body, None, name="mix_fwd", grid=(s // tm,),
        in_specs=([row(qa)] + [_class_spec(d, tm, qb) for d in DILATIONS]
                  + [_class_spec(d, tm, 128) for d in DILATIONS] + [vec(qa), vec(qb)]),
        out_specs=([row(qa + qb), row(qb)] + [_class_spec(d, tm, 128) for d in DILATIONS] + [row(1), row(1)]),
        out_shape=([jax.ShapeDtypeStruct((s, qa + qb), BF16), jax.ShapeDtypeStruct((s, qb), F32)]
                   + [_class_shape(d, s, 128, F32) for d in DILATIONS]
                   + [jax.ShapeDtypeStruct((s, 1), F32), jax.ShapeDtypeStruct((s, 1), F32)]),
        scratch_shapes=[pltpu.VMEM((tm, 128), F32)],
        compiler_params=_params(("parallel",), tm * (qa + 4 * qb) * 4 + tm * (qa + qb) * 2 + tm * 4096),
    )(oa, *obs, *lses, ga, gb)[0]


def _head_rowsums(prod, rows):
    low = _low_lanes(rows)
    lane = lax.broadcasted_iota(jnp.int32, (rows, 128), 1)
    out = jnp.zeros((rows, 128), F32)
    for i in range(prod.shape[1] // PAIR):
        tile = prod[:, i * PAIR:(i + 1) * PAIR]
        lo = jnp.sum(jnp.where(low, tile, 0.0), axis=-1, keepdims=True)
        hi = jnp.sum(jnp.where(low, 0.0, tile), axis=-1, keepdims=True)
        out = jnp.where(lane == 2 * i, lo, out)
        out = jnp.where(lane == 2 * i + 1, hi, out)
    return out


def _mix_bwd(dmix, oa, ob, ra, rb, ga, gb, hook=None):
    s, qa = oa.shape
    qb = ob.shape[1]
    tm = _row_tile(s)

    def one(dy, o, r, g):
        xn = o * r
        dxn = dy * g
        do = r * (dxn - xn * jnp.mean(dxn * xn, axis=-1, keepdims=True))
        return do, jnp.sum(dy * xn, axis=0, keepdims=True), _head_rowsums(do * o, tm)

    def body(dmix_ref, oa_ref, ob_ref, ra_ref, rb_ref, ga_ref, gb_ref,
             doa_ref, dob1_ref, dob2_ref, dob3_ref, dla_ref, dlb1_ref, dlb2_ref, dlb3_ref,
             dga_ref, dgb_ref, perm_ref):
        doa, dga, dla = one(dmix_ref[:, 0:qa], oa_ref[...], ra_ref[...], ga_ref[...])
        dob, dgb, dlb = one(dmix_ref[:, qa:qa + qb], ob_ref[...], rb_ref[...], gb_ref[...])
        doa_ref[...] = doa.astype(BF16)
        dla_ref[...] = dla
        for dob_ref, dlb_ref, dil in zip((dob1_ref, dob2_ref, dob3_ref), (dlb1_ref, dlb2_ref, dlb3_ref),
                                         DILATIONS):
            _to_classes(dlb_ref, slice(0, 128), dlb, perm_ref, dil)
            for i in range(qb // PAIR):
                sl = slice(i * PAIR, (i + 1) * PAIR)
                _to_classes(dob_ref, sl, dob[:, sl], perm_ref, dil)

        @pl.when(pl.program_id(0) == 0)
        def _():
            dga_ref[...] = dga
            dgb_ref[...] = dgb

        @pl.when(pl.program_id(0) > 0)
        def _():
            dga_ref[...] += dga
            dgb_ref[...] += dgb

    def row(w):
        return pl.BlockSpec((tm, w), lambda i: (i, 0))

    def vec(w):
        return pl.BlockSpec((1, w), lambda i: (0, 0))

    res, hook_res = _call(
        body, hook, name="mix_bwd", grid=(s // tm,),
        in_specs=[row(qa + qb), row(qa), row(qb), row(1), row(1), vec(qa), vec(qb)],
        out_specs=([row(qa)] + [_class_spec(d, tm, qb) for d in DILATIONS] + [row(128)]
                   + [_class_spec(d, tm, 128) for d in DILATIONS] + [vec(qa), vec(qb)]),
        out_shape=([jax.ShapeDtypeStruct((s, qa), BF16)] + [_class_shape(d, s, qb, BF16) for d in DILATIONS]
                   + [jax.ShapeDtypeStruct((s, 128), F32)] + [_class_shape(d, s, 128, F32) for d in DILATIONS]
                   + [jax.ShapeDtypeStruct((1, qa), F32), jax.ShapeDtypeStruct((1, qb), F32)]),
        scratch_shapes=[pltpu.VMEM((tm, 128), F32)],
        compiler_params=_params(("arbitrary",), tm * (qa + qb) * 16),
    )(dmix, oa, ob, ra, rb, ga, gb)
    return res if hook is None else (res, hook_res)


def _assemble_dproj(dqa, dkva, dqs, dks, dvs):
    s, qa = dqa.shape
    kva = dkva.shape[1]
    qb = dqs[0].shape[2]
    width = qa + kva + 3 * qb
    tm = _row_tile(s)

    def body(dqa_ref, dkva_ref, q1, q2, q3, k1, k2, k3, v1, v2, v3, dp_ref, db_ref, perm_ref):
        first = pl.program_id(0) == 0

        def emit(off, val):
            dp_ref[:, off:off + PAIR] = val.astype(BF16)
            col = jnp.sum(val, axis=0, keepdims=True)

            @pl.when(first)
            def _():
                db_ref[:, off:off + PAIR] = col

            @pl.when(jnp.logical_not(first))
            def _():
                db_ref[:, off:off + PAIR] += col

        for i in range(qa // PAIR):
            emit(i * PAIR, dqa_ref[:, i * PAIR:(i + 1) * PAIR])
        for i in range(kva // PAIR):
            emit(qa + i * PAIR, dkva_ref[:, i * PAIR:(i + 1) * PAIR])
        for j, branch_refs in enumerate(((q1, q2, q3), (k1, k2, k3), (v1, v2, v3))):
            for i in range(qb // PAIR):
                sl = slice(i * PAIR, (i + 1) * PAIR)
                total = None
                for ref, dil in zip(branch_refs, DILATIONS):
                    val = _from_classes(ref, sl, perm_ref, dil)
                    total = val if total is None else total + val
                emit(qa + kva + j * qb + i * PAIR, total)

    def row(w):
        return pl.BlockSpec((tm, w), lambda i: (i, 0))

    return _call(
        body, None, name="assemble_dproj", grid=(s // tm,),
        in_specs=[row(qa), row(kva)] + [_class_spec(d, tm, qb) for d in DILATIONS] * 3,
        out_specs=[row(width), pl.BlockSpec((1, width), lambda i: (0, 0))],
        out_shape=[jax.ShapeDtypeStruct((s, width), BF16), jax.ShapeDtypeStruct((1, width), F32)],
        scratch_shapes=[pltpu.VMEM((tm, 128), F32)],
        compiler_params=_params(("arbitrary",), tm * (qa + kva + 9 * qb) * 4 + tm * width * 2),
    )(dqa, dkva, *dqs, *dks, *dvs)[0]


def _fill_bias(bias_ref, n_pairs, max_steps, dil, slopes, sink_ref=None):
    qi = lax.broadcasted_iota(jnp.int32, (BLOCK, 2 * BLOCK), 0)
    kj = lax.broadcasted_iota(jnp.int32, (BLOCK, 2 * BLOCK), 1)
    steps = qi + BLOCK - kj
    dist = (steps * dil).astype(F32)
    band = (steps >= 0) & (steps <= max_steps)
    assert sink_ref is None or max_steps < BLOCK
    for first in (0, 1):
        valid = band & (kj >= BLOCK) if first else band
        for i in range(n_pairs):
            tables = []
            for half in (0, 1):
                table = jnp.where(valid, -(slopes[2 * i + half] * dist), NEG_INF)
                if sink_ref is not None:
                    table = jnp.where(kj == 0, sink_ref[2 * i + half], table)
                tables.append(table)
            bias_ref[first, i] = jnp.concatenate(tables, axis=0)


def _without_sink_row(tile):
    row = lax.broadcasted_iota(jnp.int32, tile.shape, 0)
    return jnp.where(row == 0, jnp.zeros_like(tile), tile)


def _bias_shape(n_pairs):
    return pltpu.VMEM((2, n_pairs, 2 * BLOCK, 2 * BLOCK), F32)


def _stack_heads(tile, low):
    zero = jnp.zeros_like(tile)
    return jnp.concatenate([jnp.where(low, tile, zero), jnp.where(low, zero, tile)], axis=0)


def _unstack_heads(stacked, low):
    return jnp.where(low, stacked[0:BLOCK], stacked[BLOCK:2 * BLOCK])


def _head_columns(ref, i):
    return jnp.concatenate([ref[:, 2 * i:2 * i + 1], ref[:, 2 * i + 1:2 * i + 2]], axis=0)


def _swap_halves(t):
    return pltpu.roll(t, HEAD_DIM, 1)


def _dup_group(t_bf16, group):
    t = t_bf16.astype(F32)
    low = lax.broadcasted_iota(jnp.int32, t.shape, 1) < HEAD_DIM
    keep = low if group == 0 else jnp.logical_not(low)
    return jnp.where(keep, t, _swap_halves(t)).astype(BF16)


def _attn_fwd(name, q, kv, *, dil, max_steps, slopes, sinks=None, hook=None):
    grouped = sinks is not None
    _, length, w = q.shape
    n_pairs = w // PAIR
    nb = length // BLOCK
    heads_per_group = 2 * n_pairs // N_KV_GROUPS

    def body(*refs):
        if grouped:
            sink_ref, q_ref, kvp_ref, kvc_ref, o_ref, lse_ref, bias_ref = refs
        else:
            q_ref, kp_ref, kc_ref, vp_ref, vc_ref, o_ref, lse_ref, bias_ref = refs
        n = pl.program_id(1)

        @pl.when((pl.program_id(0) == 0) & (n == 0))
        def _():
            _fill_bias(bias_ref, n_pairs, max_steps, dil, slopes, sink_ref if grouped else None)

        first = (n == 0).astype(jnp.int32)
        low = _low_lanes(BLOCK)
        lane = lax.broadcasted_iota(jnp.int32, (BLOCK, 128), 1)
        lse_acc = jnp.zeros((BLOCK, 128), F32)
        if grouped:
            kv_all = jnp.concatenate([kvp_ref[...], kvc_ref[...]], axis=0)
            k_dup = [_without_sink_row(_dup_group(kv_all[:, 0:PAIR], g)) for g in range(N_KV_GROUPS)]
            v_dup = [_without_sink_row(_dup_group(kv_all[:, PAIR:2 * PAIR], g)) for g in range(N_KV_GROUPS)]
        for i in range(n_pairs):
            sl = slice(i * PAIR, (i + 1) * PAIR)
            qs = _stack_heads(q_ref[:, sl] * ATT_SCALE, low)
            if grouped:
                kk, vv = k_dup[2 * i // heads_per_group], v_dup[2 * i // heads_per_group]
            else:
                kk = jnp.concatenate([kp_ref[:, sl], kc_ref[:, sl]], axis=0)
                vv = jnp.concatenate([vp_ref[:, sl], vc_ref[:, sl]], axis=0)
            sc = lax.dot_general(qs, kk, (((1,), (1,)), ((), ())), preferred_element_type=F32)
            sc = sc + bias_ref[first, i]
            m = jnp.max(sc, axis=-1, keepdims=True)
            p = jnp.exp(sc - m)
            den = jnp.sum(p, axis=-1, keepdims=True)
            o = jnp.dot(p.astype(BF16), vv, preferred_element_type=F32) / den
            o_ref[:, sl] = _unstack_heads(o, low).astype(o_ref.dtype)
            lse = m + jnp.log(den)
            lse_acc = jnp.where(lane == 2 * i, lse[0:BLOCK], lse_acc)
            lse_acc = jnp.where(lane == 2 * i + 1, lse[BLOCK:2 * BLOCK], lse_acc)
        lse_ref[...] = lse_acc

    def cur(width):
        return pl.BlockSpec((None, BLOCK, width), lambda r, n: (r, n, 0))

    def prev(width):
        return pl.BlockSpec((None, BLOCK, width), lambda r, n: (r, jnp.maximum(n - 1, 0), 0))

    if grouped:
        kvw = kv.shape[2]
        operands = [sinks, q, kv, kv]
        in_specs = [SMEM_SPEC, cur(w), prev(kvw), cur(kvw)]
    else:
        operands = [q, kv[0], kv[0], kv[1], kv[1]]
        in_specs = [cur(w), prev(w), cur(w), prev(w), cur(w)]
    res, hook_res = _call(
        body, hook, name=name, grid=(dil, nb), in_specs=in_specs,
        out_specs=[cur(w), cur(128)],
        out_shape=[jax.ShapeDtypeStruct((dil, length, w), F32 if grouped else BRANCH_DTYPE),
                   jax.ShapeDtypeStruct((dil, length, 128), F32)],
        scratch_shapes=[_bias_shape(n_pairs)],
        compiler_params=_params(("arbitrary", "arbitrary"), BLOCK * w * 16 + n_pairs * BLOCK * BLOCK * 16),
    )(*operands)
    return res if hook is None else (res, hook_res)


def _attn_bwd(name, q, kv, do, lse, delta, *, dil, max_steps, slopes, sinks=None, hook=None):
    grouped = sinks is not None
    _, length, w = q.shape
    n_pairs = w // PAIR
    nb = length // BLOCK
    heads_per_group = 2 * n_pairs // N_KV_GROUPS
    pairs_per_group = n_pairs // N_KV_GROUPS

    def body(*refs):
        if grouped:
            (sink_ref, q_ref, kvp_ref, kvc_ref, do_ref, lse_ref, dl_ref,
             dq_ref, dkv_ref, dsink_ref, acc_ref, bias_ref) = refs
        else:
            (q_ref, kp_ref, kc_ref, vp_ref, vc_ref, do_ref, lse_ref, dl_ref,
             dq_ref, dk_ref, dv_ref, acck_ref, accv_ref, bias_ref) = refs
        n = pl.program_id(1)

        @pl.when((pl.program_id(0) == 0) & (n == 0))
        def _():
            _fill_bias(bias_ref, n_pairs, max_steps, dil, slopes, sink_ref if grouped else None)

        @pl.when(n == 0)
        def _():
            if grouped:
                acc_ref[...] = jnp.zeros_like(acc_ref)

                @pl.when(pl.program_id(0) == 0)
                def _():
                    dsink_ref[...] = jnp.zeros_like(dsink_ref)
            else:
                acck_ref[...] = jnp.zeros_like(acck_ref)
                accv_ref[...] = jnp.zeros_like(accv_ref)

        @pl.when(n == nb)
        def _():
            if grouped:
                dkv_ref[...] = acc_ref[...]
            else:
                dk_ref[...] = acck_ref[...].astype(dk_ref.dtype)
                dv_ref[...] = accv_ref[...].astype(dv_ref.dtype)

        @pl.when(n < nb)
        def _():
            first = (n == 0).astype(jnp.int32)
            low = _low_lanes(BLOCK)
            low_kv = _low_lanes(2 * BLOCK)
            lane1 = lax.broadcasted_iota(jnp.int32, (1, 128), 1)
            if grouped:
                kv_all = jnp.concatenate([kvp_ref[...], kvc_ref[...]], axis=0)
                k_dup = [_without_sink_row(_dup_group(kv_all[:, 0:PAIR], g)) for g in range(N_KV_GROUPS)]
                v_dup = [_without_sink_row(_dup_group(kv_all[:, PAIR:2 * PAIR], g)) for g in range(N_KV_GROUPS)]
                dk_grp =[jnp.zeros((2 * BLOCK, PAIR), F32) for _ in range(N_KV_GROUPS)]
                dv_grp = [jnp.zeros((2 * BLOCK, PAIR), F32) for _ in range(N_KV_GROUPS)]
                dsink = jnp.zeros((1, 128), F32)
            for i in range(n_pairs):
                sl = slice(i * PAIR, (i + 1) * PAIR)
                qs = _stack_heads(q_ref[:, sl] * ATT_SCALE, low)
                dos = _stack_heads(do_ref[:, sl], low)
                if grouped:
                    grp = 2 * i // heads_per_group
                    kk, vv = k_dup[grp], v_dup[grp]
                else:
                    kk = jnp.concatenate([kp_ref[:, sl], kc_ref[:, sl]], axis=0)
                    vv = jnp.concatenate([vp_ref[:, sl], vc_ref[:, sl]], axis=0)
                lse_col = _head_columns(lse_ref, i)
                dl_col = _head_columns(dl_ref, i)
                sc = lax.dot_general(qs, kk, (((1,), (1,)), ((), ())), preferred_element_type=F32)
                p = jnp.exp(sc + bias_ref[first, i] - lse_col)
                dp = lax.dot_general(dos, vv, (((1,), (1,)), ((), ())), preferred_element_type=F32)
                ds_f32 = p * (dp - dl_col)
                ds = ds_f32.astype(BF16)
                dq = jnp.dot(ds, kk, preferred_element_type=F32)
                dkk = lax.dot_general(ds, qs, (((0,), (0,)), ((), ())), preferred_element_type=F32)
                dvv = lax.dot_general(p.astype(BF16), dos, (((0,), (0,)), ((), ())),
                                      preferred_element_type=F32)
                if grouped:
                    for half in (0, 1):
                        contrib = jnp.sum(ds_f32[half * BLOCK:(half + 1) * BLOCK, 0:1], axis=0, keepdims=True)
                        dsink = jnp.where(lane1 == 2 * i + half, dsink + contrib, dsink)
                dq_ref[:, sl] = (_unstack_heads(dq, low) * ATT_SCALE).astype(dq_ref.dtype)
                if grouped:
                    dk_grp[grp] = dk_grp[grp] + dkk
                    dv_grp[grp] = dv_grp[grp] + dvv
                else:
                    dk_ref[:, sl] = (acck_ref[:, sl] + dkk[0:BLOCK]).astype(dk_ref.dtype)
                    acck_ref[:, sl] = dkk[BLOCK:2 * BLOCK]
                    dv_ref[:, sl] = (accv_ref[:, sl] + dvv[0:BLOCK]).astype(dv_ref.dtype)
                    accv_ref[:, sl] = dvv[BLOCK:2 * BLOCK]
            if grouped:
                folded = [_without_sink_row(t + _swap_halves(t)) for t in dk_grp + dv_grp]
                dk_tile = jnp.where(low_kv, folded[0], folded[1])
                dv_tile = jnp.where(low_kv, folded[2], folded[3])
                part = jnp.concatenate([dk_tile, dv_tile], axis=1)
                dkv_ref[...] = acc_ref[...] + part[0:BLOCK]
                acc_ref[...] = part[BLOCK:2 * BLOCK]
                dsink_ref[...] += dsink

    last = nb - 1

    def cur(width):
        return pl.BlockSpec((None, BLOCK, width), lambda r, n: (r, jnp.minimum(n, last), 0))

    def prev(width):
        return pl.BlockSpec((None, BLOCK, width),
                            lambda r, n: (r, jnp.maximum(jnp.minimum(n, last) - 1, 0), 0))

    def done(width):
        return pl.BlockSpec((None, BLOCK, width), lambda r, n: (r, jnp.maximum(n - 1, 0), 0))

    if grouped:
        assert pairs_per_group * N_KV_GROUPS == n_pairs and heads_per_group % 2 == 0
        kvw = kv.shape[2]
        operands = [sinks, q, kv, kv, do, lse, delta]
        in_specs = [SMEM_SPEC, cur(w), prev(kvw), cur(kvw), cur(w), cur(128), cur(128)]
        out_specs = [cur(w), done(kvw), pl.BlockSpec((1, 128), lambda r, n: (0, 0))]
        out_shape = [jax.ShapeDtypeStruct((dil, length, w), F32), jax.ShapeDtypeStruct((dil, length, kvw), F32),
                     jax.ShapeDtypeStruct((1, 128), F32)]
        scratch = [pltpu.VMEM((BLOCK, kvw), F32), _bias_shape(n_pairs)]
    else:
        operands = [q, kv[0], kv[0], kv[1], kv[1], do, lse, delta]
        in_specs = [cur(w), prev(w), cur(w), prev(w), cur(w), cur(w), cur(128), cur(128)]
        out_specs = [cur(w), done(w), done(w)]
        out_shape = [jax.ShapeDtypeStruct((dil, length, w), BRANCH_DTYPE)] * 3
        scratch = [pltpu.VMEM((BLOCK, w), F32), pltpu.VMEM((BLOCK, w), F32), _bias_shape(n_pairs)]
    res, hook_res = _call(
        body, hook, name=name, grid=(dil, nb + 1), in_specs=in_specs, out_specs=out_specs,
        out_shape=out_shape, scratch_shapes=scratch,
        compiler_params=_params(("arbitrary", "arbitrary"), BLOCK * w * 32 + n_pairs * BLOCK * BLOCK * 16),
    )(*operands)
    return res if hook is None else (res, hook_res)


def _adamw(name, w, g, m, v):
    rows, cols = w.shape
    tm = 256 if rows % 256 == 0 else rows

    def body(w_ref, g_ref, m_ref, v_ref, d_ref, nm_ref, nv_ref):
        gv = g_ref[...]
        mn = ADAM_B1 * m_ref[...] + (1.0 - ADAM_B1) * gv
        vn = ADAM_B2 * v_ref[...] + (1.0 - ADAM_B2) * (gv * gv)
        m_hat = mn / (1.0 - ADAM_B1 ** ADAM_STEP)
        v_hat = vn / (1.0 - ADAM_B2 ** ADAM_STEP)
        d_ref[...] = -ADAM_LR * (m_hat / (jnp.sqrt(v_hat) + ADAM_EPS) + ADAM_WD * w_ref[...])
        nm_ref[...] = mn
        nv_ref[...] = vn

    spec = pl.BlockSpec((tm, cols), lambda i: (i, 0))
    return _call(
        body, None, name=name, grid=(rows // tm,), in_specs=[spec] * 4, out_specs=[spec] * 3,
        out_shape=[jax.ShapeDtypeStruct(w.shape, F32)] * 3,
        compiler_params=_params(("parallel",), tm * cols * 28),
    )(w, g, m, v)[0]


def _mesh_position():
    return lax.axis_index("x"), lax.axis_index("y"), lax.axis_index("c")


def _other_chips(x, y):
    return [(1 - x, y), (x, 1 - y), (1 - x, 1 - y)]


def _gather_hook(shard, gathered, lo, hi):
    rows, cols = shard.shape
    half, n = rows // 2, hi - lo
    assert lo % 16 == 0 and n % 16 == 0 and half % 16 == 0
    first = gathered is None

    def region(out, owner_chip, which_half):
        return out.at[pl.ds(pl.multiple_of(owner_chip * rows + which_half * half + lo, 16), n)]

    def parts(ops, outs, sems):
        x, y, c = _mesh_position()
        return ops[0], outs[0], sems, x, y, c, 2 * x + y, (x, y, 1 - c), _other_chips(x, y)

    def local_copy(src, out, chip, sem):
        return pltpu.make_async_copy(src, out.at[pl.ds(pl.multiple_of(chip * rows, 16), rows)], sem)

    def start(ops, outs, sems):
        src, out, (send, recv, fsend, frecv, local), x, y, c, chip, sibling, others = parts(ops, outs, sems)
        if first:
            local_copy(src, out, chip, local.at[0]).start()
        mine = src.at[pl.ds(pl.multiple_of(c * half + lo, 16), n)]
        for k, (px, py) in enumerate(others):
            _remote(mine, region(out, chip, c), send.at[k], recv.at[k], (px, py, c)).start()

    def mid(ops, outs, sems):
        src, out, (send, recv, fsend, frecv, local), x, y, c, chip, sibling, others = parts(ops, outs, sems)
        for k, (px, py) in enumerate(others):
            landed = region(out, 2 * px + py, c)
            _remote(landed, landed, send.at[k], recv.at[k], (px, py, c)).wait_recv()
            _remote(landed, landed, fsend.at[k], frecv.at[k], sibling).start()

    def finish(ops, outs, sems):
        src, out, (send, recv, fsend, frecv, local), x, y, c, chip, sibling, others = parts(ops, outs, sems)
        mine = src.at[pl.ds(pl.multiple_of(c * half + lo, 16), n)]
        for k, (px, py) in enumerate(others):
            passed = region(out, 2 * px + py, 1 - c)
            _remote(passed, passed, fsend.at[k], frecv.at[k], sibling).wait_recv()
        for k, (px, py) in enumerate(others):
            landed = region(out, 2 * px + py, c)
            _remote(landed, landed, fsend.at[k], frecv.at[k], sibling).wait_send()
            _remote(mine, region(out, chip, c), send.at[k], recv.at[k], (px, py, c)).wait_send()
        if first:
            local_copy(src, out, chip, local.at[0]).wait()

    sems = [pltpu.SemaphoreType.DMA((3,))] * 4 + [pltpu.SemaphoreType.DMA((1,))]
    out_shape = [jax.ShapeDtypeStruct((N_CHIPS * rows, cols), shard.dtype)]
    if first:
        return _Hook([shard], out_shape, sems, start, finish, mid)
    return _Hook([shard, gathered], out_shape, sems, start, finish, mid, aliases={1: 0})


def _exchange_hook(grad):
    rows, cols = grad.shape[0] // N_CHIPS, grad.shape[1]
    half = rows // 2
    assert half % 16 == 0

    def copies(ops, outs, sems):
        x, y, c = _mesh_position()
        send, recv = sems
        return [_remote(ops[0].at[pl.ds(pl.multiple_of(k * rows + (1 - c) * half, 16), half)], outs[0].at[k],
                        send.at[k], recv.at[k], (x, y, 1 - c)) for k in range(N_CHIPS)]

    def start(ops, outs, sems):
        for cp in copies(ops, outs, sems):
            cp.start()

    def finish(ops, outs, sems):
        for cp in copies(ops, outs, sems):
            cp.wait_recv()
            cp.wait_send()

    return _Hook([grad], [jax.ShapeDtypeStruct((N_CHIPS, half, cols), grad.dtype)],
                 [pltpu.SemaphoreType.DMA((N_CHIPS,))] * 2, start, finish)


def _scatter_hook(chip_sum):
    _, half, cols = chip_sum.shape

    def copies(ops, outs, sems):
        x, y, c = _mesh_position()
        send, recv = sems
        return [_remote(ops[0].at[2 * px + py], outs[0].at[k], send.at[k], recv.at[k], (px, py, c))
                for k, (px, py) in enumerate(_other_chips(x, y))]

    def start(ops, outs, sems):
        for cp in copies(ops, outs, sems):
            cp.start()

    def finish(ops, outs, sems):
        for cp in copies(ops, outs, sems):
            cp.wait_recv()
            cp.wait_send()

    return _Hook([chip_sum], [jax.ShapeDtypeStruct((3, half, cols), chip_sum.dtype)],
                 [pltpu.SemaphoreType.DMA((3,))] * 2, start, finish)


def _sum_tile(half):
    return 256 if half % 256 == 0 else half


def _chip_add(name, grad, from_sibling, core):
    n_chips, half, cols = from_sibling.shape
    rows = 2 * half
    tr = _sum_tile(half)

    def body(core_ref, g_ref, s_ref, o_ref):
        o_ref[...] = (g_ref[...].astype(F32) + s_ref[...].astype(F32)).astype(o_ref.dtype)

    tile = pl.BlockSpec((None, tr, cols), lambda k, i, core_ref: (k, i, 0))
    return pl.pallas_call(
        body, name=name,
        grid_spec=pltpu.PrefetchScalarGridSpec(
            num_scalar_prefetch=1, grid=(n_chips, half // tr),
            in_specs=[pl.BlockSpec((tr, cols), lambda k, i, core_ref:
                                   (k * (rows // tr) + core_ref[0] * (half // tr) + i, 0)), tile],
            out_specs=tile),
        out_shape=jax.ShapeDtypeStruct(from_sibling.shape, from_sibling.dtype),
        compiler_params=_params(("parallel", "parallel"), 3 * tr * cols * 4),
    )(core, grad, from_sibling)


def _final_add(name, chip_sum, from_chips, chip):
    _, half, cols = chip_sum.shape
    tr = _sum_tile(half)

    def body(chip_ref, own_ref, others_ref, o_ref):
        total = own_ref[...].astype(F32)
        for k in range(3):
            total = total + others_ref[k].astype(F32)
        o_ref[...] = total

    return pl.pallas_call(
        body, name=name,
        grid_spec=pltpu.PrefetchScalarGridSpec(
            num_scalar_prefetch=1, grid=(half // tr,),
            in_specs=[pl.BlockSpec((None, tr, cols), lambda i, chip_ref: (chip_ref[0], i, 0)),
                      pl.BlockSpec((3, tr, cols), lambda i, chip_ref: (0, i, 0))],
            out_specs=pl.BlockSpec((tr, cols), lambda i, chip_ref: (i, 0))),
        out_shape=jax.ShapeDtypeStruct((half, cols), F32),
        compiler_params=_params(("parallel",), 6 * tr * cols * 4),
    )(chip, chip_sum, from_chips)


FINISH_CHUNK_ROWS = 256


def _finish_reduction(chip_sums, shares, small):
    n_w, n_s = len(chip_sums), len(shares)
    halves = [cs.shape[1] for cs in chip_sums]
    cols = shares[0].shape[1] if n_s else chip_sums[0].shape[2]
    wire = chip_sums[0].dtype if n_w else GRAD_WIRE_DTYPE
    rows_s = small.shape[0]
    ch = FINISH_CHUNK_ROWS

    def body(*refs):
        sums_in, small_ref, share_in = refs[:n_w], refs[n_w], refs[n_w + 1:n_w + 1 + n_s]
        o0 = n_w + 1 + n_s
        outs, small_out, share_out = refs[o0:o0 + n_w], refs[o0 + n_w], refs[o0 + n_w + 1:o0 + n_w + 1 + n_s]
        arrived = refs[o0 + n_w + 1 + n_s:o0 + 2 * n_w + 1 + n_s]
        (small_all, buf_in, buf_out, chip_send, chip_recv, fin_send, fin_recv, small_send, small_recv,
         share_send, share_recv, share_local, io_sem) = refs[o0 + 2 * n_w + 1 + n_s:]
        x, y, c = _mesh_position()
        chip = 2 * x + y
        me = 4 * x + 2 * y + c
        sibling = (x, y, 1 - c)
        others = _other_chips(x, y)
        pending, local = [], []

        for w in range(n_w):
            for k, (px, py) in enumerate(others):
                cp = _remote(sums_in[w].at[2 * px + py], arrived[w].at[k], chip_send.at[w, k],
                             chip_recv.at[w, k], (px, py, c))
                cp.start()
                pending.append(cp)
        small_all[me] = small_ref[...]
        for j in range(N_DEV - 1):
            peer = (me + 1 + j) % N_DEV
            cp = _remote(small_all.at[me], small_all.at[me], small_send.at[j], small_recv.at[j],
                         (peer // 4, (peer // 2) % 2, peer % 2))
            cp.start()
            pending.append(cp)

        def halves_of(out, rows):
            return [out.at[pl.ds(pl.multiple_of(which * rows, 16), rows)] for which in (c, 1 - c)]

        for i in range(n_s):
            mine, _ = halves_of(share_out[i], share_in[i].shape[0])
            cp = pltpu.make_async_copy(share_in[i], mine, share_local.at[i])
            cp.start()
            local.append(cp)
            cp = _remote(share_in[i], mine, share_send.at[i], share_recv.at[i], sibling)
            cp.start()
            pending.append(cp)

        def add_chunk(w, dst, start, size):
            total = None
            for src in [sums_in[w].at[chip]] + [arrived[w].at[k] for k in range(3)]:
                cp = pltpu.make_async_copy(src.at[pl.ds(start, size)], buf_in.at[pl.ds(0, size)], io_sem)
                cp.start()
                cp.wait()
                val = buf_in[pl.ds(0, size), :].astype(F32)
                total = val if total is None else total + val
            buf_out[pl.ds(0, size), :] = total
            cp = pltpu.make_async_copy(buf_out.at[pl.ds(0, size)], dst.at[pl.ds(start, size)], io_sem)
            cp.start()
            cp.wait()

        for w in range(n_w):
            for k, (px, py) in enumerate(others):
                _remote(sums_in[w].at[chip], arrived[w].at[k], chip_send.at[w, k], chip_recv.at[w, k],
                        (px, py, c)).wait_recv()
            mine, _ = halves_of(outs[w], halves[w])
            n_full = halves[w] // ch

            def loop_body(i, carry, w=w, mine=mine):
                add_chunk(w, mine, pl.multiple_of(i * ch, ch), ch)
                return carry

            lax.fori_loop(0, n_full, loop_body, 0)
            if halves[w] % ch:
                add_chunk(w, mine, n_full * ch, halves[w] - n_full * ch)
            cp = _remote(mine, mine, fin_send.at[w], fin_recv.at[w], sibling)
            cp.start()
            pending.append(cp)
        for w in range(n_w):
            _, theirs = halves_of(outs[w], halves[w])
            _remote(theirs, theirs, fin_send.at[w], fin_recv.at[w], sibling).wait_recv()
        for i in range(n_s):
            _, theirs = halves_of(share_out[i], share_in[i].shape[0])
            _remote(share_in[i], theirs, share_send.at[i], share_recv.at[i], sibling).wait_recv()

        for j in range(N_DEV - 1):
            peer = (me + N_DEV - 1 - j) % N_DEV
            _remote(small_all.at[peer], small_all.at[peer], small_send.at[j], small_recv.at[j],
                    sibling).wait_recv()
        total = small_all[0]
        for dev in range(1, N_DEV):
            total = total + small_all[dev]
        small_out[...] = total
        for cp in pending:
            cp.wait_send()
        for cp in local:
            cp.wait()

    def sems(n):
        return pltpu.SemaphoreType.DMA((max(n, 1),))

    sds = jax.ShapeDtypeStruct
    out_shape = ([sds((2 * h, cols), F32) for h in halves] + [sds((rows_s, 128), F32)]
                 + [sds((2 * sh.shape[0], cols), F32) for sh in shares]
                 + [sds((3, h, cols), wire) for h in halves])
    res = pl.pallas_call(
        body, name="finish_reduction",
        in_specs=[HBM_SPEC] * n_w + [VMEM_SPEC] + [HBM_SPEC] * n_s,
        out_specs=[HBM_SPEC] * n_w + [VMEM_SPEC] + [HBM_SPEC] * (n_s + n_w),
        out_shape=out_shape,
        scratch_shapes=[
            pltpu.VMEM((N_DEV, rows_s, 128), F32), pltpu.VMEM((ch, cols), wire), pltpu.VMEM((ch, cols), F32),
            pltpu.SemaphoreType.DMA((max(n_w, 1), 3)), pltpu.SemaphoreType.DMA((max(n_w, 1), 3)),
            sems(n_w), sems(n_w), sems(N_DEV - 1), sems(N_DEV - 1), sems(n_s), sems(n_s), sems(n_s),
            pltpu.SemaphoreType.DMA,
        ],
        compiler_params=pltpu.CompilerParams(vmem_limit_bytes=VMEM_LIMIT_MIN),
    )(*chip_sums, small, *shares)
    return res[:n_w], res[n_w + 1:n_w + 1 + n_s], res[n_w]


def _gather_weights(shards):
    n_w = len(shards)
    halves = [s.shape[0] // 2 for s in shards]

    def body(*refs):
        ins, outs = refs[:n_w], refs[n_w:2 * n_w]
        local_sems, send_sems, recv_sems, fsend_sems, frecv_sems = refs[2 * n_w:]
        x, y, c = _mesh_position()
        chip = 2 * x + y
        sibling = (x, y, 1 - c)
        others = _other_chips(x, y)

        def region(w, owner_chip, half):
            start = owner_chip * (2 * halves[w]) + half * halves[w]
            return outs[w].at[pl.ds(pl.multiple_of(start, 16), halves[w])]

        def remote(src, dst, ssem, rsem, dev):
            return pltpu.make_async_remote_copy(src_ref=src, dst_ref=dst, send_sem=ssem, recv_sem=rsem,
                                                device_id=dev, device_id_type=MESH)

        local = []
        for w in range(n_w):
            cp = pltpu.make_async_copy(ins[w], outs[w].at[pl.ds(pl.multiple_of(chip * 2 * halves[w], 16),
                                                                2 * halves[w])], local_sems.at[w])
            cp.start()
            local.append(cp)
        sends = []
        for w in range(n_w):
            mine = ins[w].at[pl.ds(pl.multiple_of(c * halves[w], 16), halves[w])]
            for k, (px, py) in enumerate(others):
                cp = remote(mine, region(w, chip, c), send_sems.at[w, k], recv_sems.at[w, k], (px, py, c))
                cp.start()
                sends.append(cp)
        for k, (px, py) in enumerate(others):
            for w in range(n_w):
                landed = region(w, 2 * px + py, c)
                remote(landed, landed, send_sems.at[w, k], recv_sems.at[w, k], (px, py, c)).wait_recv()
                cp = remote(landed, landed, fsend_sems.at[w, k], frecv_sems.at[w, k], sibling)
                cp.start()
                sends.append(cp)
        for k, (px, py) in enumerate(others):
            for w in range(n_w):
                passed = region(w, 2 * px + py, 1 - c)
                remote(passed, passed, fsend_sems.at[w, k], frecv_sems.at[w, k], sibling).wait_recv()
        for cp in sends:
            cp.wait_send()
        for cp in local:
            cp.wait()

    return pl.pallas_call(
        body, name="gather_weights",
        in_specs=[HBM_SPEC] * n_w, out_specs=[HBM_SPEC] * n_w,
        out_shape=[jax.ShapeDtypeStruct((N_CHIPS * s.shape[0], s.shape[1]), s.dtype) for s in shards],
        scratch_shapes=[pltpu.SemaphoreType.DMA((n_w,))] + [pltpu.SemaphoreType.DMA((n_w, 3))] * 4,
    )(*shards)


REDUCE_CHUNK_ROWS = 256


def _reduce_gradients(grads, shares, small):
    n_w = len(grads)
    rows = [g.shape[0] // N_CHIPS for g in grads]
    halves = [r // 2 for r in rows]
    cols = grads[0].shape[1]
    wire = grads[0].dtype
    ch = REDUCE_CHUNK_ROWS
    for h in halves:
        assert h % 16 == 0
    rows_s = small.shape[0]
    n_s = len(shares)

    def body(*refs):
        g_in = refs[:n_w]
        small_ref = refs[n_w]
        share_in = refs[n_w + 1:n_w + 1 + n_s]
        refs = refs[:n_w + 1] + refs[n_w + 1 + n_s:]
        outs = refs[n_w + 1:2 * n_w + 1]
        small_out = refs[2 * n_w + 1]
        from_sib = refs[2 * n_w + 2:3 * n_w + 2]
        chip_sum = refs[3 * n_w + 2:4 * n_w + 2]
        from_chips = refs[4 * n_w + 2:5 * n_w + 2]
        share_out = refs[5 * n_w + 2:5 * n_w + 2 + n_s]
        (small_all, buf_a, buf_b, buf_o, sib_send, sib_recv, chip_send, chip_recv,
         fin_send, fin_recv, small_send, small_recv, io_sem,
         share_send, share_recv, share_local) = refs[5 * n_w + 2 + n_s:]
        x, y, c = _mesh_position()
        chip = 2 * x + y
        me = 4 * x + 2 * y + c
        sibling = (x, y, 1 - c)
        others = _other_chips(x, y)

        def remote(src, dst, ssem, rsem, dev):
            return pltpu.make_async_remote_copy(src_ref=src, dst_ref=dst, send_sem=ssem, recv_sem=rsem,
                                                device_id=dev, device_id_type=MESH)

        def part(w, owner_chip, half):
            start = owner_chip * rows[w] + half * halves[w]
            return g_in[w].at[pl.ds(pl.multiple_of(start, 16), halves[w])]

        pending = []
        small_all[me] = small_ref[...]
        for j in range(N_DEV - 1):
            peer = (me + 1 + j) % N_DEV
            cp = remote(small_all.at[me], small_all.at[me], small_send.at[j], small_recv.at[j],
                        (peer // 4, (peer // 2) % 2, peer % 2))
            cp.start()
            pending.append(cp)

        local = []
        for i in range(n_s):
            half_rows = share_in[i].shape[0]
            place = share_out[i].at[pl.ds(pl.multiple_of(c * half_rows, 16), half_rows)]
            cp = pltpu.make_async_copy(share_in[i], place, share_local.at[i])
            cp.start()
            local.append(cp)
            cp = remote(share_in[i], place, share_send.at[i], share_recv.at[i], sibling)
            cp.start()
            pending.append(cp)

        for w in range(n_w):
            for k in range(N_CHIPS):
                cp = remote(part(w, k, 1 - c), from_sib[w].at[k], sib_send.at[w, k], sib_recv.at[w, k], sibling)
                cp.start()
                pending.append(cp)

        def add_stream(w, srcs, dst, n_rows):
            def chunk(start, size):
                total = None
                for i, src in enumerate(srcs):
                    buf = buf_a if i % 2 == 0 else buf_b
                    cp = pltpu.make_async_copy(src.at[pl.ds(start, size)], buf.at[pl.ds(0, size)], io_sem)
                    cp.start()
                    cp.wait()
                    val = buf[pl.ds(0, size), :].astype(F32)
                    total = val if total is None else total + val
                return total

            n_full = n_rows // ch
            rem = n_rows - n_full * ch

            def store(total, start, size):
                if dst.dtype == F32:
                    buf_o[pl.ds(0, size), :] = total
                    cp = pltpu.make_async_copy(buf_o.at[pl.ds(0, size)], dst.at[pl.ds(start, size)], io_sem)
                else:
                    buf_a[pl.ds(0, size), :] = total.astype(buf_a.dtype)
                    cp = pltpu.make_async_copy(buf_a.at[pl.ds(0, size)], dst.at[pl.ds(start, size)], io_sem)
                cp.start()
                cp.wait()

            def loop_body(i, carry):
                start = pl.multiple_of(i * ch, ch)
                store(chunk(start, ch), start, ch)
                return carry

            lax.fori_loop(0, n_full, loop_body, 0)
            if rem:
                store(chunk(n_full * ch, rem), n_full * ch, rem)

        order = [2, 0, 1]
        for w in range(n_w):
            for k in range(N_CHIPS):
                remote(part(w, k, 1 - c), from_sib[w].at[k], sib_send.at[w, k], sib_recv.at[w, k],
                       sibling).wait_recv()
        for k in order:
            px, py = others[k]
            owner = 2 * px + py
            for w in range(n_w):
                add_stream(w, [part(w, owner, c), from_sib[w].at[owner]], chip_sum[w].at[owner], halves[w])
                cp = remote(chip_sum[w].at[owner], from_chips[w].at[k], chip_send.at[w, k],
                            chip_recv.at[w, k], (px, py, c))
                cp.start()
                pending.append(cp)
        for w in range(n_w):
            add_stream(w, [part(w, chip, c), from_sib[w].at[chip]], chip_sum[w].at[chip], halves[w])

        for w in range(n_w):
            for k in range(3):
                px, py = others[k]
                remote(chip_sum[w].at[chip], from_chips[w].at[k], chip_send.at[w, k], chip_recv.at[w, k],
                       (px, py, c)).wait_recv()
            mine = outs[w].at[pl.ds(pl.multiple_of(c * halves[w], 16), halves[w])]
            add_stream(w, [chip_sum[w].at[chip], from_chips[w].at[0], from_chips[w].at[1],
                           from_chips[w].at[2]], mine, halves[w])
            cp = remote(mine, mine, fin_send.at[w], fin_recv.at[w], sibling)
            cp.start()
            pending.append(cp)
        for w in range(n_w):
            theirs = outs[w].at[pl.ds(pl.multiple_of((1 - c) * halves[w], 16), halves[w])]
            remote(theirs, theirs, fin_send.at[w], fin_recv.at[w], sibling).wait_recv()

        for j in range(N_DEV - 1):
            peer = (me + N_DEV - 1 - j) % N_DEV
            remote(small_all.at[peer], small_all.at[peer], small_send.at[j], small_recv.at[j],
                   sibling).wait_recv()
        total = small_all[0]
        for d in range(1, N_DEV):
            total = total + small_all[d]
        small_out[...] = total
        for i in range(n_s):
            half_rows = share_in[i].shape[0]
            theirs = share_out[i].at[pl.ds(pl.multiple_of((1 - c) * half_rows, 16), half_rows)]
            remote(share_in[i], theirs, share_send.at[i], share_recv.at[i], sibling).wait_recv()
        for cp in pending:
            cp.wait_send()
        for cp in local:
            cp.wait()

    hbm_scratch = ([jax.ShapeDtypeStruct((N_CHIPS, h, cols), wire) for h in halves] * 2
                   + [jax.ShapeDtypeStruct((3, h, cols), wire) for h in halves])
    out_shape = ([jax.ShapeDtypeStruct((r, cols), F32) for r in rows]
                 + [jax.ShapeDtypeStruct((rows_s, 128), F32)] + hbm_scratch
                 + [jax.ShapeDtypeStruct((2 * sh.shape[0], sh.shape[1]), F32) for sh in shares])
    res = pl.pallas_call(
        body, name="reduce_gradients",
        in_specs=[HBM_SPEC] * n_w + [VMEM_SPEC] + [HBM_SPEC] * n_s,
        out_specs=[HBM_SPEC] * n_w + [VMEM_SPEC] + [HBM_SPEC] * (3 * n_w + n_s),
        out_shape=out_shape,
        scratch_shapes=[
            pltpu.VMEM((N_DEV, rows_s, 128), F32),
            pltpu.VMEM((ch, cols), wire), pltpu.VMEM((ch, cols), wire), pltpu.VMEM((ch, cols), F32),
            pltpu.SemaphoreType.DMA((n_w, N_CHIPS)), pltpu.SemaphoreType.DMA((n_w, N_CHIPS)),
            pltpu.SemaphoreType.DMA((n_w, 3)), pltpu.SemaphoreType.DMA((n_w, 3)),
            pltpu.SemaphoreType.DMA((n_w,)), pltpu.SemaphoreType.DMA((n_w,)),
            pltpu.SemaphoreType.DMA((N_DEV - 1,)), pltpu.SemaphoreType.DMA((N_DEV - 1,)),
            pltpu.SemaphoreType.DMA,
            pltpu.SemaphoreType.DMA((max(n_s, 1),)), pltpu.SemaphoreType.DMA((max(n_s, 1),)),
            pltpu.SemaphoreType.DMA((max(n_s, 1),)),
        ],
        compiler_params=pltpu.CompilerParams(vmem_limit_bytes=VMEM_LIMIT_MIN),
    )(*grads, small, *shares)
    return res[:n_w], res[n_w], res[len(res) - n_s:] if n_s else []


def _pack_small(parts, rows):
    flat = jnp.concatenate([p.reshape(-1) for p in parts])
    flat = jnp.pad(flat, (0, rows * 128 - flat.shape[0]))
    return flat.reshape(rows, 128)


def _unpack_small(packed, shapes):
    flat = packed.reshape(-1)
    out, off = [], 0
    for shp in shapes:
        n = int(np.prod(shp))
        out.append(flat[off:off + n].reshape(shp))
        off += n
    return out


def kernel(x, g_attn, w_in, b_in, sinks_a, g_out_a, g_out_b, w_out, g_mlp, w_1, w_2, g_final, loss_target, m_g_attn, m_w_in, m_b_in, m_sinks_a, m_g_out_a, m_g_out_b, m_w_out, m_g_mlp, m_w_1, m_w_2, m_g_final, v_g_attn, v_w_in, v_b_in, v_sinks_a, v_g_out_a, v_g_out_b, v_w_out, v_g_mlp, v_w_1, v_w_2, v_g_final):
    s, d = x.shape[1], x.shape[2]
    d_in = b_in.shape[1]
    qa = g_out_a.shape[1]
    qb = g_out_b.shape[1]
    kva = 2 * N_KV_GROUPS * HEAD_DIM
    assert d_in == qa + kva + 3 * qb and qa + qb == w_out.shape[1] * N_CHIPS
    d_ff = w_1.shape[2] * N_CHIPS
    ff_shard = w_1.shape[2]
    in_shard = w_in.shape[2]
    n_heads_a, n_heads_b = qa // HEAD_DIM, qb // HEAD_DIM
    slopes_a, slopes_b = alibi_slopes(n_heads_a), alibi_slopes(n_heads_b)

    x2d = x[0]
    target = loss_target[0]

    shards = [w_in[0].T.astype(BF16), w_out[0].astype(BF16), w_1[0].astype(BF16), w_2[0].astype(BF16)]
    core_index = lax.axis_index("c").astype(jnp.int32).reshape(1)
    chip_index = (2 * lax.axis_index("x") + lax.axis_index("y")).astype(jnp.int32).reshape(1)

    tm = _tile(s, 1024)

    (h1, r1), (w_in_t,) = _norm_fwd("norm_attn", x2d, g_attn,
                                    hook=_gather_hook(shards[0], None, 0, shards[0].shape[0] // 2))

    q_a, = _project_by_class("proj_qa", h1, w_in_t, b_in, 0, qa, (1,))
    kv_a, = _project_by_class("proj_kva", h1, w_in_t, b_in, qa, kva, (1,))
    q_bs, (w_out_g,) = _project_by_class("proj_qb", h1, w_in_t, b_in, qa + kva, qb, DILATIONS,
                                         hook=_gather_hook(shards[1], None, 0, shards[1].shape[0] // 2))
    k_bs = _project_by_class("proj_kb", h1, w_in_t, b_in, qa + kva + qb, qb, DILATIONS)
    v_bs = _project_by_class("proj_vb", h1, w_in_t, b_in, qa + kva + 2 * qb, qb, DILATIONS)

    quarter = shards[2].shape[0] // 8
    sinks = sinks_a.reshape(-1)
    (o_a, lse_a), (w_1_g,) = _attn_fwd("attn_a_fwd", q_a, kv_a, dil=1, max_steps=WINDOW_A - 1, slopes=slopes_a,
                                       sinks=sinks, hook=_gather_hook(shards[2], None, 0, quarter))
    o_a = o_a[0]
    o_bs, lse_bs = [], []
    for n, (window, dil) in enumerate(DILATED_BRANCHES):
        (o, l), (w_1_g,) = _attn_fwd(f"attn_b{dil}_fwd", q_bs[n], (k_bs[n], v_bs[n]), dil=dil,
                                     max_steps=window // dil, slopes=slopes_b,
                                     hook=_gather_hook(shards[2], w_1_g, (n + 1) * quarter, (n + 2) * quarter))
        o_bs.append(o)
        lse_bs.append(l)
    w_1_g = w_1_g.reshape(N_CHIPS, d, ff_shard)
    mix, o_b, *lse_tot, r_a, r_b = _mix_fwd(o_a, o_bs, lse_bs, g_out_a, g_out_b)

    tn = _tile(d, 512)
    a_spec, b_spec = _mm_specs("nn", tm, tn, d)
    tile_mn = pl.BlockSpec((tm, tn), lambda i, j, k: (i, j))
    x2 = _matmul("out_proj", mix, w_out_g, [x2d], mode="nn", grid=(s // tm, d // tn, 1),
                 a_spec=a_spec, b_spec=b_spec, extra_specs=[tile_mn],
                 out_shapes=[jax.ShapeDtypeStruct((s, d), F32)], out_specs=[tile_mn],
                 epilogue=lambda acc, res: (acc + res,))[0]

    h2, r2 = _norm_fwd("norm_mlp", x2, g_mlp)

    tn = _tile(ff_shard, 512)
    per = ff_shard // tn
    a_spec, _ = _mm_specs("nn", tm, tn, d)
    (u,), (w_2_g,) = _matmul(
        "mlp_up", h2, w_1_g, [], mode="nn", grid=(s // tm, d_ff // tn, 1),
        a_spec=a_spec, b_spec=pl.BlockSpec((None, d, tn), lambda i, j, k: (j // per, 0, j % per)),
        extra_specs=[], out_shapes=[jax.ShapeDtypeStruct((s, d_ff), BF16)], out_specs=[tile_mn],
        epilogue=lambda acc: (jnp.maximum(acc, 0.0),),
        hook=_gather_hook(shards[3], None, 0, shards[3].shape[0] // 2))

    tn = _tile(d, 1024)
    tk = _tile(d_ff, 2048)
    a_spec, b_spec = _mm_specs("nn", tm, tn, tk)
    tile_mn = pl.BlockSpec((tm, tn), lambda i, j, k: (i, j))
    x3 = _matmul("mlp_down", u, w_2_g, [x2], mode="nn", grid=(s // tm, d // tn, d_ff // tk),
                 a_spec=a_spec, b_spec=b_spec, extra_specs=[tile_mn],
                 out_shapes=[jax.ShapeDtypeStruct((s, d), F32)], out_specs=[tile_mn],
                 prologue=lambda a: a * a, epilogue=lambda acc, res: (acc + res,), acc_shape=(tm, tn))[0]

    dx3, dx3b, loss_part, dg_final = _loss_head(x3, target, g_final.reshape(1, d))

    tn = _tile(d_ff, 512)
    a_spec, b_spec = _mm_specs("nt", tm, tn, d)
    tile_mn = pl.BlockSpec((tm, tn), lambda i, j, k: (i, j))
    dpre = _matmul("mlp_down_dx", dx3b, w_2_g, [u], mode="nt", grid=(s // tm, d_ff // tn, 1),
                   a_spec=a_spec, b_spec=b_spec, extra_specs=[tile_mn],
                   out_shapes=[jax.ShapeDtypeStruct((s, d_ff), BF16)], out_specs=[tile_mn],
                   epilogue=lambda acc, uu: (acc * (2.0 * uu.astype(F32)),))[0]

    wire = GRAD_WIRE_DTYPE
    tk_s = _tile(s, 2048)
    tmw = _tile(d_ff, 1024)
    a_spec, b_spec = _mm_specs("tn", tmw, d, tk_s)
    dw_2 = _matmul("mlp_down_dw", u, dx3b, [], mode="tn", grid=(d_ff // tmw, 1, s // tk_s),
                   a_spec=a_spec, b_spec=b_spec, extra_specs=[],
                   out_shapes=[jax.ShapeDtypeStruct((d_ff, d), wire)],
                   out_specs=[pl.BlockSpec((tmw, d), lambda i, j, k: (i, j))],
                   prologue=lambda a: a * a, epilogue=lambda acc: (acc,), acc_shape=(tmw, d))[0]

    tn = _tile(d, 1024)
    tk = _tile(ff_shard, 2048)
    per = ff_shard // tk
    a_spec, _ = _mm_specs("nt", tm, tn, tk)
    tile_mn = pl.BlockSpec((tm, tn), lambda i, j, k: (i, j))
    dh2 = _matmul("mlp_up_dx", dpre, w_1_g, [], mode="nt", grid=(s // tm, d // tn, d_ff // tk),
                  a_spec=a_spec, b_spec=pl.BlockSpec((None, tn, tk), lambda i, j, k: (k // per, j, k % per)),
                  extra_specs=[], out_shapes=[jax.ShapeDtypeStruct((s, d), F32)], out_specs=[tile_mn],
                  epilogue=lambda acc: (acc,), acc_shape=(tm, tn))[0]

    tmw = _tile(d, 1024)
    tnw = _tile(ff_shard, 2048)
    per = ff_shard // tnw
    a_spec, b_spec = _mm_specs("tn", tmw, tnw, tk_s)
    dw_1 = _matmul("mlp_up_dw", h2, dpre, [], mode="tn", grid=(d // tmw, d_ff // tnw, s // tk_s),
                   a_spec=a_spec, b_spec=b_spec, extra_specs=[],
                   out_shapes=[jax.ShapeDtypeStruct((N_CHIPS, d, ff_shard), wire)],
                   out_specs=[pl.BlockSpec((None, tmw, tnw), lambda i, j, k: (j // per, i, j % per))],
                   epilogue=lambda acc: (acc,), acc_shape=(tmw, tnw))[0]

    dw_1 = dw_1.reshape(N_CHIPS * d, ff_shard)
    (dx2, dx2b, dg_mlp), (sib_2, sib_1) = _norm_bwd(
        "norm_mlp_bwd", dh2, x2, r2, g_mlp, dx3, hook=_merge_hooks([_exchange_hook(dw_2), _exchange_hook(dw_1)]))
    chip_sum_2 = _chip_add("chip_add_w_2", dw_2, sib_2, core_index)
    chip_sum_1 = _chip_add("chip_add_w_1", dw_1, sib_1, core_index)

    tn = _tile(d, 512)
    a_spec, b_spec = _mm_specs("nt", tm, tn, d)
    tile_mn = pl.BlockSpec((tm, tn), lambda i, j, k: (i, j))
    dmix = _matmul("out_proj_dx", dx2b, w_out_g, [], mode="nt", grid=(s // tm, d // tn, 1),
                   a_spec=a_spec, b_spec=b_spec, extra_specs=[],
                   out_shapes=[jax.ShapeDtypeStruct((s, d), F32)], out_specs=[tile_mn],
                   epilogue=lambda acc: (acc,))[0]

    tmw = _tile(d, 1024)
    a_spec, b_spec = _mm_specs("tn", tmw, d, tk_s)
    dw_out = _matmul("out_proj_dw", mix, dx2b, [], mode="tn", grid=(d // tmw, 1, s // tk_s),
                     a_spec=a_spec, b_spec=b_spec, extra_specs=[],
                     out_shapes=[jax.ShapeDtypeStruct((d, d), wire)],
                     out_specs=[pl.BlockSpec((tmw, d), lambda i, j, k: (i, j))],
                     epilogue=lambda acc: (acc,), acc_shape=(tmw, d))[0]

    mix_grads, (sib_out,) = _mix_bwd(dmix, o_a, o_b, r_a, r_b, g_out_a, g_out_b, hook=_exchange_hook(dw_out))
    do_a, do_bs, delta_a, delta_bs = mix_grads[0], mix_grads[1:4], mix_grads[4], mix_grads[5:8]
    dg_out_a, dg_out_b = mix_grads[8:]
    chip_sum_out = _chip_add("chip_add_w_out", dw_out, sib_out, core_index)

    (dq_a, dkv_a, dsinks), (chips_2,) = _attn_bwd(
        "attn_a_bwd", q_a, kv_a, do_a[None], lse_a, delta_a[None], dil=1, max_steps=WINDOW_A - 1,
        slopes=slopes_a, sinks=sinks, hook=_scatter_hook(chip_sum_2))
    dqs, dks, dvs = [], [], []
    scatter = {1: chip_sum_1, 4: chip_sum_out}
    arrived = {}
    for n, (window, dil) in enumerate(DILATED_BRANCHES):
        res = _attn_bwd(f"attn_b{dil}_bwd", q_bs[n], (k_bs[n], v_bs[n]), do_bs[n], lse_tot[n],
                        delta_bs[n], dil=dil, max_steps=window // dil, slopes=slopes_b,
                        hook=_scatter_hook(scatter[dil]) if dil in scatter else None)
        if dil in scatter:
            res, (arrived[dil],) = res
        dq, dk, dv = res
        dqs.append(dq)
        dks.append(dk)
        dvs.append(dv)
    half_2 = _final_add("final_add_w_2", chip_sum_2, chips_2, chip_index)
    half_1 = _final_add("final_add_w_1", chip_sum_1, arrived[1], chip_index)
    half_out = _final_add("final_add_w_out", chip_sum_out, arrived[4], chip_index)
    dproj, db_in = _assemble_dproj(dq_a[0], dkv_a[0], dqs, dks, dvs)

    tmw = d_in // 2 if (d_in // 2) % 128 == 0 else d_in
    tnw = _tile(d, 1024)
    tk_s = _tile(s, 1024)
    a_spec, b_spec = _mm_specs("tn", tmw, tnw, tk_s)
    dw_in_t = _matmul("in_proj_dw", dproj, h1, [], mode="tn", grid=(d_in // tmw, d // tnw, s // tk_s),
                      a_spec=a_spec, b_spec=b_spec, extra_specs=[],
                      out_shapes=[jax.ShapeDtypeStruct((d_in, d), wire)],
                      out_specs=[pl.BlockSpec((tmw, tnw), lambda i, j, k: (i, j))],
                      epilogue=lambda acc: (acc,), acc_shape=(tmw, tnw))[0]

    tn = _tile(d, 512)
    a_spec, b_spec = _mm_specs("nn", tm, tn, d_in)
    tile_mn = pl.BlockSpec((tm, tn), lambda i, j, k: (i, j))
    (dh1,), (sib_in,) = _matmul("in_proj_dx", dproj, w_in_t, [], mode="nn", grid=(s // tm, d // tn, 1),
                                a_spec=a_spec, b_spec=b_spec, extra_specs=[],
                                out_shapes=[jax.ShapeDtypeStruct((s, d), F32)], out_specs=[tile_mn],
                                epilogue=lambda acc: (acc,), hook=_exchange_hook(dw_in_t))
    chip_sum_in = _chip_add("chip_add_w_in", dw_in_t, sib_in, core_index)

    (grad_x, _, dg_attn), (chips_in,) = _norm_bwd("norm_attn_bwd", dh1, x2d, r1, g_attn, dx2,
                                                  hook=_scatter_hook(chip_sum_in))
    half_in = _final_add("final_add_w_in", chip_sum_in, chips_in, chip_index)

    small_parts = [dg_attn, db_in, dsinks[:, :n_heads_a], dg_out_a, dg_out_b, dg_mlp, dg_final]
    small_shapes = [g_attn.shape, b_in.shape, sinks_a.shape, g_out_a.shape, g_out_b.shape, g_mlp.shape,
                    g_final.shape]
    n_small = sum(int(np.prod(shp)) for shp in small_shapes)
    rows_s = -(-n_small // (8 * 128)) * 8
    _, (gw_in_t, gw_out, gw_1, gw_2), small_sum = _finish_reduction(
        [], [half_in, half_out, half_1, half_2], _pack_small(small_parts, rows_s))
    gw_in = gw_in_t.T
    g_small = _unpack_small(small_sum, small_shapes)

    upd_in = _adamw("adamw_w_in", w_in[0], gw_in, m_w_in[0], v_w_in[0])
    upd_out = _adamw("adamw_w_out", w_out[0], gw_out, m_w_out[0], v_w_out[0])
    upd_1 = _adamw("adamw_w_1", w_1[0], gw_1, m_w_1[0], v_w_1[0])
    upd_2 = _adamw("adamw_w_2", w_2[0], gw_2, m_w_2[0], v_w_2[0])
    small_w = [g_attn, b_in, sinks_a, g_out_a, g_out_b, g_mlp, g_final]
    small_m = [m_g_attn, m_b_in, m_sinks_a, m_g_out_a, m_g_out_b, m_g_mlp, m_g_final]
    small_v = [v_g_attn, v_b_in, v_sinks_a, v_g_out_a, v_g_out_b, v_g_mlp, v_g_final]
    upd_small = _adamw("adamw_small", _pack_small(small_w, rows_s), small_sum,
                       _pack_small(small_m, rows_s), _pack_small(small_v, rows_s))
    d_small, m_small, v_small = [_unpack_small(t, small_shapes) for t in upd_small]

    loss = lax.psum(loss_part[0, 0], ("x", "y", "c"))

    def ordered(small, big):
        w_in_v, w_out_v, w_1_v, w_2_v = big
        return [small[0], w_in_v[None], small[1], small[2], small[3], small[4], w_out_v[None], small[5],
                w_1_v[None], w_2_v[None], small[6]]

    grads = ordered(g_small, (gw_in, gw_out, gw_1, gw_2))
    deltas = ordered(d_small, (upd_in[0], upd_out[0], upd_1[0], upd_2[0]))
    new_m = ordered(m_small, (upd_in[1], upd_out[1], upd_1[1], upd_2[1]))
    new_v = ordered(v_small, (upd_in[2], upd_out[2], upd_1[2], upd_2[2]))
    return (loss, grad_x[None], *grads, *deltas, *new_m, *new_v)
```

```python
import functools

import jax
import jax.numpy as jnp
import numpy as np
from jax import lax
from jax.experimental import pallas as pl
from jax.experimental.pallas import tpu as pltpu

F32 = jnp.float32
BF16 = jnp.bfloat16

HEAD_DIM = 64
BLOCK = 128
PAIR = 2 * HEAD_DIM
N_KV_GROUPS = 2
WINDOW_A = 128
DILATED_BRANCHES = ((128, 1), (512, 4), (2048, 16))
EPS = 1e-5
NEG_INF = -1e30
ATT_SCALE = HEAD_DIM ** -0.5

ADAM_LR = 0.001
ADAM_B1 = 0.9
ADAM_B2 = 0.999
ADAM_EPS = 1e-08
ADAM_WD = 0.01
ADAM_STEP = 10

N_CHIPS = 4
N_DEV = 8
MESH = pl.DeviceIdType.MESH
GRAD_WIRE_DTYPE = jnp.bfloat16
BRANCH_DTYPE = jnp.bfloat16

VMEM_CAPACITY_V7X = 64 * 1024 * 1024
VMEM_LIMIT_MAX = 56 * 1024 * 1024
VMEM_LIMIT_MIN = 48 * 1024 * 1024

HBM_SPEC = pl.BlockSpec(memory_space=pltpu.HBM)
VMEM_SPEC = pl.BlockSpec(memory_space=pltpu.VMEM)
SMEM_SPEC = pl.BlockSpec(memory_space=pltpu.SMEM)


def _nbytes(shape, dtype):
    return int(np.prod([s for s in shape if s is not None])) * jnp.dtype(dtype).itemsize


def _params(semantics, block_bytes):
    limit = min(max(2 * block_bytes + (4 << 20), VMEM_LIMIT_MIN), VMEM_LIMIT_MAX)
    return pltpu.CompilerParams(dimension_semantics=semantics, vmem_limit_bytes=limit)


class _Hook:
    def __init__(self, operands, out_shape, sems, start, finish, mid=None, aliases=None):
        self.operands, self.out_shape, self.sems = list(operands), list(out_shape), list(sems)
        self.start, self.mid, self.finish = start, mid, finish
        self.aliases = dict(aliases or {})


def _merge_hooks(hooks):
    hooks = [h for h in hooks if h is not None]
    if len(hooks) <= 1:
        return hooks[0] if hooks else None
    n_op = np.cumsum([0] + [len(h.operands) for h in hooks])
    n_out = np.cumsum([0] + [len(h.out_shape) for h in hooks])
    n_sem = np.cumsum([0] + [len(h.sems) for h in hooks])

    def run(which):
        def fn(ops, outs, sems):
            for i, h in enumerate(hooks):
                f = getattr(h, which)
                if f is not None:
                    f(ops[n_op[i]:n_op[i + 1]], outs[n_out[i]:n_out[i + 1]], sems[n_sem[i]:n_sem[i + 1]])
        return fn

    aliases = {}
    for i, h in enumerate(hooks):
        aliases.update({int(n_op[i]) + a: int(n_out[i]) + b for a, b in h.aliases.items()})
    return _Hook(sum([h.operands for h in hooks], []), sum([h.out_shape for h in hooks], []),
                 sum([h.sems for h in hooks], []), run("start"), run("finish"),
                 run("mid") if any(h.mid for h in hooks) else None, aliases)


HOOK_MID_FRACTION = 0.6


def _call(body, hook, *, name, grid, in_specs, out_specs, out_shape, scratch_shapes=(), compiler_params):
    in_specs, out_specs, out_shape = list(in_specs), list(out_specs), list(out_shape)
    scratch_shapes = list(scratch_shapes)
    if hook is None:
        call = pl.pallas_call(body, name=name, grid=grid, in_specs=in_specs, out_specs=out_specs,
                              out_shape=out_shape, scratch_shapes=scratch_shapes,
                              compiler_params=compiler_params)
        return lambda *operands: (call(*operands), [])
    n_in, n_hin, n_out, n_hout, n_scr = (len(in_specs), len(hook.operands), len(out_specs),
                                         len(hook.out_shape), len(scratch_shapes))
    total = int(np.prod(grid))
    t_mid = min(int(total * HOOK_MID_FRACTION), total - 1)

    def wrapped(*refs):
        ins, h_in = refs[:n_in], refs[n_in:n_in + n_hin]
        o0 = n_in + n_hin
        outs, h_out = refs[o0:o0 + n_out], refs[o0 + n_out:o0 + n_out + n_hout]
        s0 = o0 + n_out + n_hout
        scr, h_sems = refs[s0:s0 + n_scr], refs[s0 + n_scr:]
        t = pl.program_id(0)
        for axis in range(1, len(grid)):
            t = t * grid[axis] + pl.program_id(axis)

        @pl.when(t == 0)
        def _():
            hook.start(h_in, h_out, h_sems)

        body(*ins, *outs, *scr)
        if hook.mid is not None:
            @pl.when(t == t_mid)
            def _():
                hook.mid(h_in, h_out, h_sems)

        @pl.when(t == total - 1)
        def _():
            hook.finish(h_in, h_out, h_sems)

    params = pltpu.CompilerParams(dimension_semantics=("arbitrary",) * len(grid),
                                  vmem_limit_bytes=compiler_params.vmem_limit_bytes)
    call = pl.pallas_call(
        wrapped, name=name, grid=grid,
        in_specs=in_specs + [HBM_SPEC] * n_hin, out_specs=out_specs + [HBM_SPEC] * n_hout,
        out_shape=out_shape + hook.out_shape, scratch_shapes=scratch_shapes + hook.sems,
        input_output_aliases={n_in + a: n_out + b for a, b in hook.aliases.items()},
        compiler_params=params)

    def run(*operands):
        res = call(*operands, *hook.operands)
        return res[:n_out], res[n_out:]

    return run


def _remote(src, dst, send_sem, recv_sem, device):
    return pltpu.make_async_remote_copy(src_ref=src, dst_ref=dst, send_sem=send_sem, recv_sem=recv_sem,
                                        device_id=device, device_id_type=MESH)


def alibi_slopes(n):
    return [float(v) for v in np.asarray(2.0 ** (-8.0 * (np.arange(n) + 1) / n), dtype=np.float32)]


def _matmul(name, a, b, extras, *, mode, grid, a_spec, b_spec, extra_specs, out_shapes, out_specs,
            epilogue, prologue=None, acc_shape=None, hook=None):
    dims = {"nn": ((1,), (0,)), "nt": ((1,), (1,)), "tn": ((0,), (0,))}[mode]
    nk = grid[2]
    n_ex, n_out = len(extras), len(out_shapes)

    def body(a_ref, b_ref, *rest):
        ex, outs = rest[:n_ex], rest[n_ex:n_ex + n_out]
        av = a_ref[...]
        if prologue is not None:
            av = prologue(av)
        part = lax.dot_general(av, b_ref[...], (dims, ((), ())), preferred_element_type=F32)

        def finish(acc):
            res = epilogue(acc, *[e[...] for e in ex])
            for o, r in zip(outs, res):
                o[...] = r.astype(o.dtype)

        if nk == 1:
            finish(part)
        else:
            acc_ref = rest[-1]
            k = pl.program_id(2)

            @pl.when(k == 0)
            def _():
                acc_ref[...] = part

            @pl.when(k > 0)
            def _():
                acc_ref[...] += part

            @pl.when(k == nk - 1)
            def _():
                finish(acc_ref[...])

    blocks = [(a_spec.block_shape, a.dtype), (b_spec.block_shape, b.dtype)]
    blocks += [(s.block_shape, e.dtype) for s, e in zip(extra_specs, extras)]
    blocks += [(s.block_shape, o.dtype) for s, o in zip(out_specs, out_shapes)]
    nbytes = sum(_nbytes(s, d) for s, d in blocks)
    scratch = []
    if nk > 1:
        scratch.append(pltpu.VMEM(acc_shape, F32))
        nbytes += _nbytes(acc_shape, F32)
    res, hook_res = _call(
        body, hook, name=name, grid=grid,
        in_specs=[a_spec, b_spec, *extra_specs], out_specs=list(out_specs), out_shape=list(out_shapes),
        scratch_shapes=scratch,
        compiler_params=_params(("parallel", "parallel", "arbitrary"), nbytes),
    )(a, b, *extras)
    return res if hook is None else (res, hook_res)


def _mm_specs(mode, tm, tn, tk, b_block=None, b_map=None):
    if mode == "tn":
        a_spec = pl.BlockSpec((tk, tm), lambda i, j, k: (k, i))
    else:
        a_spec = pl.BlockSpec((tm, tk), lambda i, j, k: (i, k))
    if b_block is not None:
        b_spec = pl.BlockSpec(b_block, b_map)
    elif mode == "nt":
        b_spec = pl.BlockSpec((tn, tk), lambda i, j, k: (j, k))
    else:
        b_spec = pl.BlockSpec((tk, tn), lambda i, j, k: (k, j))
    return a_spec, b_spec


def _project_by_class(name, h, w_t, bias, row_off, width, dilations, hook=None):
    s, d = h.shape
    tm = _tile(s, 1024)
    tn = 512 if width % 512 == 0 and row_off % 512 == 0 else _tile(width, 256)
    off = row_off // tn
    assert row_off % tn == 0 and tn % 128 == 0
    n_out = len(dilations)

    def body(h_ref, w_ref, b_ref, *rest):
        outs, perm_ref = rest[:n_out], rest[n_out]
        acc = lax.dot_general(h_ref[...], w_ref[...], (((1,), (1,)), ((), ())), preferred_element_type=F32)
        acc = acc + b_ref[...]
        for j in range(tn // 128):
            cols = slice(j * 128, (j + 1) * 128)
            for o_ref, dil in zip(outs, dilations):
                _to_classes(o_ref, cols, acc[:, cols], perm_ref, dil)

    blocks = tm * d * 2 + tn * d * 2 + 3 * tm * tn * 2 + tm * 128 * 4
    res, hook_res = _call(
        body, hook, name=name, grid=(s // tm, width // tn),
        in_specs=[pl.BlockSpec((tm, d), lambda i, j: (i, 0)), pl.BlockSpec((tn, d), lambda i, j: (j + off, 0)),
                  pl.BlockSpec((1, tn), lambda i, j: (0, j + off))],
        out_specs=[pl.BlockSpec((dil, tm // dil, tn), lambda i, j: (0, i, j)) for dil in dilations],
        out_shape=[_class_shape(dil, s, width, BF16) for dil in dilations],
        scratch_shapes=[pltpu.VMEM((tm, 128), F32)],
        compiler_params=_params(("parallel", "parallel"), blocks),
    )(h, w_t, bias)
    return res if hook is None else (res, hook_res)


def _tile(n, want):
    if n <= want:
        return n
    t = (want // 128) * 128
    while t > 128 and n % t:
        t -= 128
    assert n % t == 0, (n, want)
    return t


def _row_tile(s):
    return 256 if s % 256 == 0 else s


def _norm_fwd(name, x, g, hook=None):
    s, d = x.shape
    tm = _row_tile(s)

    def body(x_ref, g_ref, h_ref, r_ref):
        xv = x_ref[...]
        r = lax.rsqrt(jnp.mean(xv * xv, axis=-1, keepdims=True) + EPS)
        h_ref[...] = ((xv * r) * g_ref[...]).astype(BF16)
        r_ref[...] = r

    row = pl.BlockSpec((tm, d), lambda i: (i, 0))
    res, hook_res = _call(
        body, hook, name=name, grid=(s // tm,),
        in_specs=[row, pl.BlockSpec((1, d), lambda i: (0, 0))],
        out_specs=[row, pl.BlockSpec((tm, 1), lambda i: (i, 0))],
        out_shape=[jax.ShapeDtypeStruct((s, d), BF16), jax.ShapeDtypeStruct((s, 1), F32)],
        compiler_params=_params(("parallel",), tm * d * 6),
    )(x, g)
    return res if hook is None else (res, hook_res)


def _norm_bwd(name, dh, x, r, g, dres, hook=None):
    s, d = x.shape
    tm = _row_tile(s)

    def body(dh_ref, x_ref, r_ref, g_ref, dres_ref, dx_ref, dxb_ref, dg_ref):
        rv = r_ref[...]
        xn = x_ref[...] * rv
        dhv = dh_ref[...]
        dxn = dhv * g_ref[...]
        dx = dres_ref[...] + rv * (dxn - xn * jnp.mean(dxn * xn, axis=-1, keepdims=True))
        dx_ref[...] = dx
        dxb_ref[...] = dx.astype(BF16)
        part = jnp.sum(dhv * xn, axis=0, keepdims=True)

        @pl.when(pl.program_id(0) == 0)
        def _():
            dg_ref[...] = part

        @pl.when(pl.program_id(0) > 0)
        def _():
            dg_ref[...] += part

    row = pl.BlockSpec((tm, d), lambda i: (i, 0))
    vec = pl.BlockSpec((1, d), lambda i: (0, 0))
    res, hook_res = _call(
        body, hook, name=name, grid=(s // tm,),
        in_specs=[row, row, pl.BlockSpec((tm, 1), lambda i: (i, 0)), vec, row],
        out_specs=[row, row, vec],
        out_shape=[jax.ShapeDtypeStruct((s, d), F32), jax.ShapeDtypeStruct((s, d), BF16),
                   jax.ShapeDtypeStruct((1, d), F32)],
        compiler_params=_params(("arbitrary",), tm * d * 18),
    )(dh, x, r, g, dres)
    return res if hook is None else (res, hook_res)


def _loss_head(x3, target, g):
    s, d = x3.shape
    tm = _row_tile(s)

    def body(x_ref, t_ref, g_ref, dx_ref, dxb_ref, loss_ref, dg_ref):
        xv = x_ref[...]
        gv = g_ref[...]
        r = lax.rsqrt(jnp.mean(xv * xv, axis=-1, keepdims=True) + EPS)
        xn = xv * r
        err = xn * gv - t_ref[...]
        loss = 0.5 * jnp.sum(jnp.mean(err * err, axis=-1, keepdims=True), axis=0, keepdims=True)
        dy = err / d
        dxn = dy * gv
        dx = r * (dxn - xn * jnp.mean(dxn * xn, axis=-1, keepdims=True))
        dx_ref[...] = dx
        dxb_ref[...] = dx.astype(BF16)
        dg = jnp.sum(dy * xn, axis=0, keepdims=True)
        loss_row = jnp.broadcast_to(loss, (1, 128))

        @pl.when(pl.program_id(0) == 0)
        def _():
            dg_ref[...] = dg
            loss_ref[...] = loss_row

        @pl.when(pl.program_id(0) > 0)
        def _():
            dg_ref[...] += dg
            loss_ref[...] += loss_row

    row = pl.BlockSpec((tm, d), lambda i: (i, 0))
    vec = pl.BlockSpec((1, d), lambda i: (0, 0))
    return _call(
        body, None, name="loss_head", grid=(s // tm,),
        in_specs=[row, row, vec],
        out_specs=[row, row, pl.BlockSpec((1, 128), lambda i: (0, 0)), vec],
        out_shape=[jax.ShapeDtypeStruct((s, d), F32), jax.ShapeDtypeStruct((s, d), BF16),
                   jax.ShapeDtypeStruct((1, 128), F32), jax.ShapeDtypeStruct((1, d), F32)],
        compiler_params=_params(("arbitrary",), tm * d * 14),
    )(x3, target, g)[0]


def _low_lanes(rows):
    return lax.broadcasted_iota(jnp.int32, (rows, PAIR), 1) < HEAD_DIM


def _to_classes(dst_ref, cols, value, perm_ref, dil):
    rows = value.shape[0]
    if dil == 1:
        dst_ref[0, :, cols] = value.astype(dst_ref.dtype)
        return
    perm_ref[...] = value
    for r in range(dil):
        dst_ref[r, :, cols] = perm_ref[pl.ds(r, rows // dil, stride=dil), :].astype(dst_ref.dtype)


def _from_classes(src_ref, cols, perm_ref, dil):
    if dil == 1:
        return src_ref[0, :, cols].astype(F32)
    rows = perm_ref.shape[0]
    for r in range(dil):
        perm_ref[pl.ds(r, rows // dil, stride=dil), :] = src_ref[r, :, cols].astype(F32)
    return perm_ref[...]


def _class_spec(dil, tm, width):
    return pl.BlockSpec((dil, tm // dil, width), lambda i: (0, i, 0))


def _class_shape(dil, s, width, dtype):
    return jax.ShapeDtypeStruct((dil, s // dil, width), dtype)


DILATIONS = tuple(d for _, d in DILATED_BRANCHES)


def _mix_fwd(oa, obs, lses, ga, gb):
    s, qa = oa.shape
    qb = obs[0].shape[2]
    tm = _row_tile(s)
    all_lanes = slice(0, 128)

    def body(oa_ref, o1_ref, o2_ref, o3_ref, l1_ref, l2_ref, l3_ref, ga_ref, gb_ref,
             mix_ref, ob_ref, t1_ref, t2_ref, t3_ref, ra_ref, rb_ref, perm_ref):
        oav = oa_ref[...]
        ra = lax.rsqrt(jnp.mean(oav * oav, axis=-1, keepdims=True) + EPS)
        ra_ref[...] = ra
        mix_ref[:, 0:qa] = ((oav * ra) * ga_ref[...]).astype(BF16)
        l1, l2, l3 = [_from_classes(l_ref, all_lanes, perm_ref, dil)
                      for l_ref, dil in zip((l1_ref, l2_ref, l3_ref), DILATIONS)]
        mx = jnp.maximum(jnp.maximum(l1, l2), l3)
        e1, e2, e3 = jnp.exp(l1 - mx), jnp.exp(l2 - mx), jnp.exp(l3 - mx)
        tot = e1 + e2 + e3
        lse = mx + jnp.log(tot)
        for t_ref, dil in zip((t1_ref, t2_ref, t3_ref), DILATIONS):
            _to_classes(t_ref, all_lanes, lse, perm_ref, dil)
        ws = (e1 / tot, e2 / tot, e3 / tot)
        low = _low_lanes(tm)
        ssq = jnp.zeros((tm, 1), F32)
        for i in range(qb // PAIR):
            sl = slice(i * PAIR, (i + 1) * PAIR)
            acc = jnp.zeros((tm, PAIR), F32)
            for w, o_ref, dil in zip(ws, (o1_ref, o2_ref, o3_ref), DILATIONS):
                wexp = jnp.where(low, w[:, 2 * i:2 * i + 1], w[:, 2 * i + 1:2 * i + 2])
                acc = acc + wexp * _from_classes(o_ref, sl, perm_ref, dil)
            ob_ref[:, sl] = acc
            ssq = ssq + jnp.sum(acc * acc, axis=-1, keepdims=True)
        rb = lax.rsqrt(ssq / qb + EPS)
        rb_ref[...] = rb
        mix_ref[:, qa:qa + qb] = ((ob_ref[...] * rb) * gb_ref[...]).astype(BF16)

    def row(w):
        return pl.BlockSpec((tm, w), lambda i: (i, 0))

    def vec(w):
        return pl.BlockSpec((1, w), lambda i: (0, 0))

    return _call(
        body, None, name="mix_fwd", grid=(s // tm,),
        in_specs=([row(qa)] + [_class_spec(d, tm, qb) for d in DILATIONS]
                  + [_class_spec(d, tm, 128) for d in DILATIONS] + [vec(qa), vec(qb)]),
        out_specs=([row(qa + qb), row(qb)] + [_class_spec(d, tm, 128) for d in DILATIONS] + [row(1), row(1)]),
        out_shape=([jax.ShapeDtypeStruct((s, qa + qb), BF16), jax.ShapeDtypeStruct((s, qb), F32)]
                   + [_class_shape(d, s, 128, F32) for d in DILATIONS]
                   + [jax.ShapeDtypeStruct((s, 1), F32), jax.ShapeDtypeStruct((s, 1), F32)]),
        scratch_shapes=[pltpu.VMEM((tm, 128), F32)],
        compiler_params=_params(("parallel",), tm * (qa + 4 * qb) * 4 + tm * (qa + qb) * 2 + tm * 4096),
    )(oa, *obs, *lses, ga, gb)[0]


def _head_rowsums(prod, rows):
    low = _low_lanes(rows)
    lane = lax.broadcasted_iota(jnp.int32, (rows, 128), 1)
    out = jnp.zeros((rows, 128), F32)
    for i in range(prod.shape[1] // PAIR):
        tile = prod[:, i * PAIR:(i + 1) * PAIR]
        lo = jnp.sum(jnp.where(low, tile, 0.0), axis=-1, keepdims=True)
        hi = jnp.sum(jnp.where(low, 0.0, tile), axis=-1, keepdims=True)
        out = jnp.where(lane == 2 * i, lo, out)
        out = jnp.where(lane == 2 * i + 1, hi, out)
    return out


def _mix_bwd(dmix, oa, ob, ra, rb, ga, gb, hook=None):
    s, qa = oa.shape
    qb = ob.shape[1]
    tm = _row_tile(s)

    def one(dy, o, r, g):
        xn = o * r
        dxn = dy * g
        do = r * (dxn - xn * jnp.mean(dxn * xn, axis=-1, keepdims=True))
        return do, jnp.sum(dy * xn, axis=0, keepdims=True), _head_rowsums(do * o, tm)

    def body(dmix_ref, oa_ref, ob_ref, ra_ref, rb_ref, ga_ref, gb_ref,
             doa_ref, dob1_ref, dob2_ref, dob3_ref, dla_ref, dlb1_ref, dlb2_ref, dlb3_ref,
             dga_ref, dgb_ref, perm_ref):
        doa, dga, dla = one(dmix_ref[:, 0:qa], oa_ref[...], ra_ref[...], ga_ref[...])
        dob, dgb, dlb = one(dmix_ref[:, qa:qa + qb], ob_ref[...], rb_ref[...], gb_ref[...])
        doa_ref[...] = doa.astype(BF16)
        dla_ref[...] = dla
        for dob_ref, dlb_ref, dil in zip((dob1_ref, dob2_ref, dob3_ref), (dlb1_ref, dlb2_ref, dlb3_ref),
                                         DILATIONS):
            _to_classes(dlb_ref, slice(0, 128), dlb, perm_ref, dil)
            for i in range(qb // PAIR):
                sl = slice(i * PAIR, (i + 1) * PAIR)
                _to_classes(dob_ref, sl, dob[:, sl], perm_ref, dil)

        @pl.when(pl.program_id(0) == 0)
        def _():
            dga_ref[...] = dga
            dgb_ref[...] = dgb

        @pl.when(pl.program_id(0) > 0)
        def _():
            dga_ref[...] += dga
            dgb_ref[...] += dgb

    def row(w):
        return pl.BlockSpec((tm, w), lambda i: (i, 0))

    def vec(w):
        return pl.BlockSpec((1, w), lambda i: (0, 0))

    res, hook_res = _call(
        body, hook, name="mix_bwd", grid=(s // tm,),
        in_specs=[row(qa + qb), row(qa), row(qb), row(1), row(1), vec(qa), vec(qb)],
        out_specs=([row(qa)] + [_class_spec(d, tm, qb) for d in DILATIONS] + [row(128)]
                   + [_class_spec(d, tm, 128) for d in DILATIONS] + [vec(qa), vec(qb)]),
        out_shape=([jax.ShapeDtypeStruct((s, qa), BF16)] + [_class_shape(d, s, qb, BF16) for d in DILATIONS]
                   + [jax.ShapeDtypeStruct((s, 128), F32)] + [_class_shape(d, s, 128, F32) for d in DILATIONS]
                   + [jax.ShapeDtypeStruct((1, qa), F32), jax.ShapeDtypeStruct((1, qb), F32)]),
        scratch_shapes=[pltpu.VMEM((tm, 128), F32)],
        compiler_params=_params(("arbitrary",), tm * (qa + qb) * 16),
    )(dmix, oa, ob, ra, rb, ga, gb)
    return res if hook is None else (res, hook_res)


def _assemble_dproj(dqa, dkva, dqs, dks, dvs):
    s, qa = dqa.shape
    kva = dkva.shape[1]
    qb = dqs[0].shape[2]
    width = qa + kva + 3 * qb
    tm = _row_tile(s)

    def body(dqa_ref, dkva_ref, q1, q2, q3, k1, k2, k3, v1, v2, v3, dp_ref, db_ref, perm_ref):
        first = pl.program_id(0) == 0

        def emit(off, val):
            dp_ref[:, off:off + PAIR] = val.astype(BF16)
            col = jnp.sum(val, axis=0, keepdims=True)

            @pl.when(first)
            def _():
                db_ref[:, off:off + PAIR] = col

            @pl.when(jnp.logical_not(first))
            def _():
                db_ref[:, off:off + PAIR] += col

        for i in range(qa // PAIR):
            emit(i * PAIR, dqa_ref[:, i * PAIR:(i + 1) * PAIR])
        for i in range(kva // PAIR):
            emit(qa + i * PAIR, dkva_ref[:, i * PAIR:(i + 1) * PAIR])
        for j, branch_refs in enumerate(((q1, q2, q3), (k1, k2, k3), (v1, v2, v3))):
            for i in range(qb // PAIR):
                sl = slice(i * PAIR, (i + 1) * PAIR)
                total = None
                for ref, dil in zip(branch_refs, DILATIONS):
                    val = _from_classes(ref, sl, perm_ref, dil)
                    total = val if total is None else total + val
                emit(qa + kva + j * qb + i * PAIR, total)

    def row(w):
        return pl.BlockSpec((tm, w), lambda i: (i, 0))

    return _call(
        body, None, name="assemble_dproj", grid=(s // tm,),
        in_specs=[row(qa), row(kva)] + [_class_spec(d, tm, qb) for d in DILATIONS] * 3,
        out_specs=[row(width), pl.BlockSpec((1, width), lambda i: (0, 0))],
        out_shape=[jax.ShapeDtypeStruct((s, width), BF16), jax.ShapeDtypeStruct((1, width), F32)],
        scratch_shapes=[pltpu.VMEM((tm, 128), F32)],
        compiler_params=_params(("arbitrary",), tm * (qa + kva + 9 * qb) * 4 + tm * width * 2),
    )(dqa, dkva, *dqs, *dks, *dvs)[0]


def _fill_bias(bias_ref, n_pairs, max_steps, dil, slopes, sink_ref=None):
    qi = lax.broadcasted_iota(jnp.int32, (BLOCK, 2 * BLOCK), 0)
    kj = lax.broadcasted_iota(jnp.int32, (BLOCK, 2 * BLOCK), 1)
    steps = qi + BLOCK - kj
    dist = (steps * dil).astype(F32)
    band = (steps >= 0) & (steps <= max_steps)
    assert sink_ref is None or max_steps < BLOCK
    for first in (0, 1):
        valid = band & (kj >= BLOCK) if first else band
        for i in range(n_pairs):
            tables = []
            for half in (0, 1):
                table = jnp.where(valid, -(slopes[2 * i + half] * dist), NEG_INF)
                if sink_ref is not None:
                    table = jnp.where(kj == 0, sink_ref[2 * i + half], table)
                tables.append(table)
            bias_ref[first, i] = jnp.concatenate(tables, axis=0)


def _without_sink_row(tile):
    row = lax.broadcasted_iota(jnp.int32, tile.shape, 0)
    return jnp.where(row == 0, jnp.zeros_like(tile), tile)


def _bias_shape(n_pairs):
    return pltpu.VMEM((2, n_pairs, 2 * BLOCK, 2 * BLOCK), F32)


def _stack_heads(tile, low):
    zero = jnp.zeros_like(tile)
    return jnp.concatenate([jnp.where(low, tile, zero), jnp.where(low, zero, tile)], axis=0)


def _unstack_heads(stacked, low):
    return jnp.where(low, stacked[0:BLOCK], stacked[BLOCK:2 * BLOCK])


def _head_columns(ref, i):
    return jnp.concatenate([ref[:, 2 * i:2 * i + 1], ref[:, 2 * i + 1:2 * i + 2]], axis=0)


def _swap_halves(t):
    return pltpu.roll(t, HEAD_DIM, 1)


def _dup_group(t_bf16, group):
    t = t_bf16.astype(F32)
    low = lax.broadcasted_iota(jnp.int32, t.shape, 1) < HEAD_DIM
    keep = low if group == 0 else jnp.logical_not(low)
    return jnp.where(keep, t, _swap_halves(t)).astype(BF16)


def _attn_fwd(name, q, kv, *, dil, max_steps, slopes, sinks=None, hook=None):
    grouped = sinks is not None
    _, length, w = q.shape
    n_pairs = w // PAIR
    nb = length // BLOCK
    heads_per_group = 2 * n_pairs // N_KV_GROUPS

    def body(*refs):
        if grouped:
            sink_ref, q_ref, kvp_ref, kvc_ref, o_ref, lse_ref, bias_ref = refs
        else:
            q_ref, kp_ref, kc_ref, vp_ref, vc_ref, o_ref, lse_ref, bias_ref = refs
        n = pl.program_id(1)

        @pl.when((pl.program_id(0) == 0) & (n == 0))
        def _():
            _fill_bias(bias_ref, n_pairs, max_steps, dil, slopes, sink_ref if grouped else None)

        first = (n == 0).astype(jnp.int32)
        low = _low_lanes(BLOCK)
        lane = lax.broadcasted_iota(jnp.int32, (BLOCK, 128), 1)
        lse_acc = jnp.zeros((BLOCK, 128), F32)
        if grouped:
            kv_all = jnp.concatenate([kvp_ref[...], kvc_ref[...]], axis=0)
            k_dup = [_without_sink_row(_dup_group(kv_all[:, 0:PAIR], g)) for g in range(N_KV_GROUPS)]
            v_dup = [_without_sink_row(_dup_group(kv_all[:, PAIR:2 * PAIR], g)) for g in range(N_KV_GROUPS)]
        for i in range(n_pairs):
            sl = slice(i * PAIR, (i + 1) * PAIR)
            qs = _stack_heads(q_ref[:, sl] * ATT_SCALE, low)
            if grouped:
                kk, vv = k_dup[2 * i // heads_per_group], v_dup[2 * i // heads_per_group]
            else:
                kk = jnp.concatenate([kp_ref[:, sl], kc_ref[:, sl]], axis=0)
                vv = jnp.concatenate([vp_ref[:, sl], vc_ref[:, sl]], axis=0)
            sc = lax.dot_general(qs, kk, (((1,), (1,)), ((), ())), preferred_element_type=F32)
            sc = sc + bias_ref[first, i]
            m = jnp.max(sc, axis=-1, keepdims=True)
            p = jnp.exp(sc - m)
            den = jnp.sum(p, axis=-1, keepdims=True)
            o = jnp.dot(p.astype(BF16), vv, preferred_element_type=F32) / den
            o_ref[:, sl] = _unstack_heads(o, low).astype(o_ref.dtype)
            lse = m + jnp.log(den)
            lse_acc = jnp.where(lane == 2 * i, lse[0:BLOCK], lse_acc)
            lse_acc = jnp.where(lane == 2 * i + 1, lse[BLOCK:2 * BLOCK], lse_acc)
        lse_ref[...] = lse_acc

    def cur(width):
        return pl.BlockSpec((None, BLOCK, width), lambda r, n: (r, n, 0))

    def prev(width):
        return pl.BlockSpec((None, BLOCK, width), lambda r, n: (r, jnp.maximum(n - 1, 0), 0))

    if grouped:
        kvw = kv.shape[2]
        operands = [sinks, q, kv, kv]
        in_specs = [SMEM_SPEC, cur(w), prev(kvw), cur(kvw)]
    else:
        operands = [q, kv[0], kv[0], kv[1], kv[1]]
        in_specs = [cur(w), prev(w), cur(w), prev(w), cur(w)]
    res, hook_res = _call(
        body, hook, name=name, grid=(dil, nb), in_specs=in_specs,
        out_specs=[cur(w), cur(128)],
        out_shape=[jax.ShapeDtypeStruct((dil, length, w), F32 if grouped else BRANCH_DTYPE),
                   jax.ShapeDtypeStruct((dil, length, 128), F32)],
        scratch_shapes=[_bias_shape(n_pairs)],
        compiler_params=_params(("arbitrary", "arbitrary"), BLOCK * w * 16 + n_pairs * BLOCK * BLOCK * 16),
    )(*operands)
    return res if hook is None else (res, hook_res)


def _attn_bwd(name, q, kv, do, lse, delta, *, dil, max_steps, slopes, sinks=None, hook=None):
    grouped = sinks is not None
    _, length, w = q.shape
    n_pairs = w // PAIR
    nb = length // BLOCK
    heads_per_group = 2 * n_pairs // N_KV_GROUPS
    pairs_per_group = n_pairs // N_KV_GROUPS

    def body(*refs):
        if grouped:
            (sink_ref, q_ref, kvp_ref, kvc_ref, do_ref, lse_ref, dl_ref,
             dq_ref, dkv_ref, dsink_ref, acc_ref, bias_ref) = refs
        else:
            (q_ref, kp_ref, kc_ref, vp_ref, vc_ref, do_ref, lse_ref, dl_ref,
             dq_ref, dk_ref, dv_ref, acck_ref, accv_ref, bias_ref) = refs
        n = pl.program_id(1)

        @pl.when((pl.program_id(0) == 0) & (n == 0))
        def _():
            _fill_bias(bias_ref, n_pairs, max_steps, dil, slopes, sink_ref if grouped else None)

        @pl.when(n == 0)
        def _():
            if grouped:
                acc_ref[...] = jnp.zeros_like(acc_ref)

                @pl.when(pl.program_id(0) == 0)
                def _():
                    dsink_ref[...] = jnp.zeros_like(dsink_ref)
            else:
                acck_ref[...] = jnp.zeros_like(acck_ref)
                accv_ref[...] = jnp.zeros_like(accv_ref)

        @pl.when(n == nb)
        def _():
            if grouped:
                dkv_ref[...] = acc_ref[...]
            else:
                dk_ref[...] = acck_ref[...].astype(dk_ref.dtype)
                dv_ref[...] = accv_ref[...].astype(dv_ref.dtype)

        @pl.when(n < nb)
        def _():
            first = (n == 0).astype(jnp.int32)
            low = _low_lanes(BLOCK)
            low_kv = _low_lanes(2 * BLOCK)
            lane1 = lax.broadcasted_iota(jnp.int32, (1, 128), 1)
            if grouped:
                kv_all = jnp.concatenate([kvp_ref[...], kvc_ref[...]], axis=0)
                k_dup = [_without_sink_row(_dup_group(kv_all[:, 0:PAIR], g)) for g in range(N_KV_GROUPS)]
                v_dup = [_without_sink_row(_dup_group(kv_all[:, PAIR:2 * PAIR], g)) for g in range(N_KV_GROUPS)]
                dk_grp =[jnp.zeros((2 * BLOCK, PAIR), F32) for _ in range(N_KV_GROUPS)]
                dv_grp = [jnp.zeros((2 * BLOCK, PAIR), F32) for _ in range(N_KV_GROUPS)]
                dsink = jnp.zeros((1, 128), F32)
            for i in range(n_pairs):
                sl = slice(i * PAIR, (i + 1) * PAIR)
                qs = _stack_heads(q_ref[:, sl] * ATT_SCALE, low)
                dos = _stack_heads(do_ref[:, sl], low)
                if grouped:
                    grp = 2 * i // heads_per_group
                    kk, vv = k_dup[grp], v_dup[grp]
                else:
                    kk = jnp.concatenate([kp_ref[:, sl], kc_ref[:, sl]], axis=0)
                    vv = jnp.concatenate([vp_ref[:, sl], vc_ref[:, sl]], axis=0)
                lse_col = _head_columns(lse_ref, i)
                dl_col = _head_columns(dl_ref, i)
                sc = lax.dot_general(qs, kk, (((1,), (1,)), ((), ())), preferred_element_type=F32)
                p = jnp.exp(sc + bias_ref[first, i] - lse_col)
                dp = lax.dot_general(dos, vv, (((1,), (1,)), ((), ())), preferred_element_type=F32)
                ds_f32 = p * (dp - dl_col)
                ds = ds_f32.astype(BF16)
                dq = jnp.dot(ds, kk, preferred_element_type=F32)
                dkk = lax.dot_general(ds, qs, (((0,), (0,)), ((), ())), preferred_element_type=F32)
                dvv = lax.dot_general(p.astype(BF16), dos, (((0,), (0,)), ((), ())),
                                      preferred_element_type=F32)
                if grouped:
                    for half in (0, 1):
                        contrib = jnp.sum(ds_f32[half * BLOCK:(half + 1) * BLOCK, 0:1], axis=0, keepdims=True)
                        dsink = jnp.where(lane1 == 2 * i + half, dsink + contrib, dsink)
                dq_ref[:, sl] = (_unstack_heads(dq, low) * ATT_SCALE).astype(dq_ref.dtype)
                if grouped:
                    dk_grp[grp] = dk_grp[grp] + dkk
                    dv_grp[grp] = dv_grp[grp] + dvv
                else:
                    dk_ref[:, sl] = (acck_ref[:, sl] + dkk[0:BLOCK]).astype(dk_ref.dtype)
                    acck_ref[:, sl] = dkk[BLOCK:2 * BLOCK]
                    dv_ref[:, sl] = (accv_ref[:, sl] + dvv[0:BLOCK]).astype(dv_ref.dtype)
                    accv_ref[:, sl] = dvv[BLOCK:2 * BLOCK]
            if grouped:
                folded = [_without_sink_row(t + _swap_halves(t)) for t in dk_grp + dv_grp]
                dk_tile = jnp.where(low_kv, folded[0], folded[1])
                dv_tile = jnp.where(low_kv, folded[2], folded[3])
                part = jnp.concatenate([dk_tile, dv_tile], axis=1)
                dkv_ref[...] = acc_ref[...] + part[0:BLOCK]
                acc_ref[...] = part[BLOCK:2 * BLOCK]
                dsink_ref[...] += dsink

    last = nb - 1

    def cur(width):
        return pl.BlockSpec((None, BLOCK, width), lambda r, n: (r, jnp.minimum(n, last), 0))

    def prev(width):
        return pl.BlockSpec((None, BLOCK, width),
                            lambda r, n: (r, jnp.maximum(jnp.minimum(n, last) - 1, 0), 0))

    def done(width):
        return pl.BlockSpec((None, BLOCK, width), lambda r, n: (r, jnp.maximum(n - 1, 0), 0))

    if grouped:
        assert pairs_per_group * N_KV_GROUPS == n_pairs and heads_per_group % 2 == 0
        kvw = kv.shape[2]
        operands = [sinks, q, kv, kv, do, lse, delta]
        in_specs = [SMEM_SPEC, cur(w), prev(kvw), cur(kvw), cur(w), cur(128), cur(128)]
        out_specs = [cur(w), done(kvw), pl.BlockSpec((1, 128), lambda r, n: (0, 0))]
        out_shape = [jax.ShapeDtypeStruct((dil, length, w), F32), jax.ShapeDtypeStruct((dil, length, kvw), F32),
                     jax.ShapeDtypeStruct((1, 128), F32)]
        scratch = [pltpu.VMEM((BLOCK, kvw), F32), _bias_shape(n_pairs)]
    else:
        operands = [q, kv[0], kv[0], kv[1], kv[1], do, lse, delta]
        in_specs = [cur(w), prev(w), cur(w), prev(w), cur(w), cur(w), cur(128), cur(128)]
        out_specs = [cur(w), done(w), done(w)]
        out_shape = [jax.ShapeDtypeStruct((dil, length, w), BRANCH_DTYPE)] * 3
        scratch = [pltpu.VMEM((BLOCK, w), F32), pltpu.VMEM((BLOCK, w), F32), _bias_shape(n_pairs)]
    res, hook_res = _call(
        body, hook, name=name, grid=(dil, nb + 1), in_specs=in_specs, out_specs=out_specs,
        out_shape=out_shape, scratch_shapes=scratch,
        compiler_params=_params(("arbitrary", "arbitrary"), BLOCK * w * 32 + n_pairs * BLOCK * BLOCK * 16),
    )(*operands)
    return res if hook is None else (res, hook_res)


def _adamw(name, w, g, m, v):
    rows, cols = w.shape
    tm = 256 if rows % 256 == 0 else rows

    def body(w_ref, g_ref, m_ref, v_ref, d_ref, nm_ref, nv_ref):
        gv = g_ref[...]
        mn = ADAM_B1 * m_ref[...] + (1.0 - ADAM_B1) * gv
        vn = ADAM_B2 * v_ref[...] + (1.0 - ADAM_B2) * (gv * gv)
        m_hat = mn / (1.0 - ADAM_B1 ** ADAM_STEP)
        v_hat = vn / (1.0 - ADAM_B2 ** ADAM_STEP)
        d_ref[...] = -ADAM_LR * (m_hat / (jnp.sqrt(v_hat) + ADAM_EPS) + ADAM_WD * w_ref[...])
        nm_ref[...] = mn
        nv_ref[...] = vn

    spec = pl.BlockSpec((tm, cols), lambda i: (i, 0))
    return _call(
        body, None, name=name, grid=(rows // tm,), in_specs=[spec] * 4, out_specs=[spec] * 3,
        out_shape=[jax.ShapeDtypeStruct(w.shape, F32)] * 3,
        compiler_params=_params(("parallel",), tm * cols * 28),
    )(w, g, m, v)[0]


def _mesh_position():
    return lax.axis_index("x"), lax.axis_index("y"), lax.axis_index("c")


def _other_chips(x, y):
    return [(1 - x, y), (x, 1 - y), (1 - x, 1 - y)]


def _gather_hook(gathered, lo, hi):
    rows, cols = gathered.shape[0] // N_CHIPS, gathered.shape[1]
    half, n = rows // 2, hi - lo
    assert lo % 16 == 0 and n % 16 == 0 and half % 16 == 0

    def region(out, owner_chip, which_half):
        return out.at[pl.ds(pl.multiple_of(owner_chip * rows + which_half * half + lo, 16), n)]

    def parts(outs, sems):
        x, y, c = _mesh_position()
        return outs[0], sems, c, 2 * x + y, (x, y, 1 - c), _other_chips(x, y)

    def start(ops, outs, sems):
        out, (send, recv, fsend, frecv), c, chip, sibling, others = parts(outs, sems)
        mine = region(out, chip, c)
        for k, (px, py) in enumerate(others):
            _remote(mine, mine, send.at[k], recv.at[k], (px, py, c)).start()

    def mid(ops, outs, sems):
        out, (send, recv, fsend, frecv), c, chip, sibling, others = parts(outs, sems)
        for k, (px, py) in enumerate(others):
            landed = region(out, 2 * px + py, c)
            _remote(landed, landed, send.at[k], recv.at[k], (px, py, c)).wait_recv()
            _remote(landed, landed, fsend.at[k], frecv.at[k], sibling).start()

    def finish(ops, outs, sems):
        out, (send, recv, fsend, frecv), c, chip, sibling, others = parts(outs, sems)
        mine = region(out, chip, c)
        for k, (px, py) in enumerate(others):
            passed = region(out, 2 * px + py, 1 - c)
            _remote(passed, passed, fsend.at[k], frecv.at[k], sibling).wait_recv()
        for k, (px, py) in enumerate(others):
            landed = region(out, 2 * px + py, c)
            _remote(landed, landed, fsend.at[k], frecv.at[k], sibling).wait_send()
            _remote(mine, mine, send.at[k], recv.at[k], (px, py, c)).wait_send()

    return _Hook([gathered], [jax.ShapeDtypeStruct(gathered.shape, gathered.dtype)],
                 [pltpu.SemaphoreType.DMA((3,))] * 4, start, finish, mid, aliases={0: 0})


def _own_shard_in_place(shard, chip):
    rows, cols = shard.shape
    return lax.dynamic_update_slice(lax.empty((N_CHIPS * rows, cols), shard.dtype), shard, (chip * rows, 0))


def _exchange_hook(grad):
    rows, cols = grad.shape[0] // N_CHIPS, grad.shape[1]
    half = rows // 2
    assert half % 16 == 0

    def copies(ops, outs, sems):
        x, y, c = _mesh_position()
        send, recv = sems
        return [_remote(ops[0].at[pl.ds(pl.multiple_of(k * rows + (1 - c) * half, 16), half)], outs[0].at[k],
                        send.at[k], recv.at[k], (x, y, 1 - c)) for k in range(N_CHIPS)]

    def start(ops, outs, sems):
        for cp in copies(ops, outs, sems):
            cp.start()

    def finish(ops, outs, sems):
        for cp in copies(ops, outs, sems):
            cp.wait_recv()
            cp.wait_send()

    return _Hook([grad], [jax.ShapeDtypeStruct((N_CHIPS, half, cols), grad.dtype)],
                 [pltpu.SemaphoreType.DMA((N_CHIPS,))] * 2, start, finish)


def _scatter_hook(chip_sum):
    _, half, cols = chip_sum.shape

    def copies(ops, outs, sems):
        x, y, c = _mesh_position()
        send, recv = sems
        return [_remote(ops[0].at[2 * px + py], outs[0].at[k], send.at[k], recv.at[k], (px, py, c))
                for k, (px, py) in enumerate(_other_chips(x, y))]

    def start(ops, outs, sems):
        for cp in copies(ops, outs, sems):
            cp.start()

    def finish(ops, outs, sems):
        for cp in copies(ops, outs, sems):
            cp.wait_recv()
            cp.wait_send()

    return _Hook([chip_sum], [jax.ShapeDtypeStruct((3, half, cols), chip_sum.dtype)],
                 [pltpu.SemaphoreType.DMA((3,))] * 2, start, finish)


def _sum_tile(half):
    return 256 if half % 256 == 0 else half


def _chip_add(name, grad, from_sibling, core):
    n_chips, half, cols = from_sibling.shape
    rows = 2 * half
    tr = _sum_tile(half)

    def body(core_ref, g_ref, s_ref, o_ref):
        o_ref[...] = (g_ref[...].astype(F32) + s_ref[...].astype(F32)).astype(o_ref.dtype)

    tile = pl.BlockSpec((None, tr, cols), lambda k, i, core_ref: (k, i, 0))
    return pl.pallas_call(
        body, name=name,
        grid_spec=pltpu.PrefetchScalarGridSpec(
            num_scalar_prefetch=1, grid=(n_chips, half // tr),
            in_specs=[pl.BlockSpec((tr, cols), lambda k, i, core_ref:
                                   (k * (rows // tr) + core_ref[0] * (half // tr) + i, 0)), tile],
            out_specs=tile),
        out_shape=jax.ShapeDtypeStruct(from_sibling.shape, from_sibling.dtype),
        compiler_params=_params(("parallel", "parallel"), 3 * tr * cols * 4),
    )(core, grad, from_sibling)


def _final_add(name, chip_sum, from_chips, chip, core):
    _, half, cols = chip_sum.shape
    tr = _sum_tile(half)

    def body(chip_ref, core_ref, own_ref, others_ref, o_ref):
        total = own_ref[...].astype(F32)
        for k in range(3):
            total = total + others_ref[k].astype(F32)
        o_ref[...] = total

    return pl.pallas_call(
        body, name=name,
        grid_spec=pltpu.PrefetchScalarGridSpec(
            num_scalar_prefetch=2, grid=(half // tr,),
            in_specs=[pl.BlockSpec((None, tr, cols), lambda i, chip_ref, core_ref: (chip_ref[0], i, 0)),
                      pl.BlockSpec((3, tr, cols), lambda i, chip_ref, core_ref: (0, i, 0))],
            out_specs=pl.BlockSpec((tr, cols),
                                   lambda i, chip_ref, core_ref: (core_ref[0] * (half // tr) + i, 0))),
        out_shape=jax.ShapeDtypeStruct((2 * half, cols), F32),
        compiler_params=_params(("parallel",), 6 * tr * cols * 4),
    )(chip, core, chip_sum, from_chips)


def _share_halves(shards, small):
    n_s = len(shards)
    rows_s = small.shape[0]

    def body(*refs):
        small_ref = refs[n_s]
        outs, small_out = refs[n_s + 1:2 * n_s + 1], refs[2 * n_s + 1]
        small_all, send, recv, small_send, small_recv = refs[2 * n_s + 2:]
        x, y, c = _mesh_position()
        me = 4 * x + 2 * y + c
        sibling = (x, y, 1 - c)
        pending = []
        for i in range(n_s):
            half = shards[i].shape[0] // 2
            mine = outs[i].at[pl.ds(pl.multiple_of(c * half, 16), half)]
            cp = _remote(mine, mine, send.at[i], recv.at[i], sibling)
            cp.start()
            pending.append(cp)
        small_all[me] = small_ref[...]
        for j in range(N_DEV - 1):
            peer = (me + 1 + j) % N_DEV
            cp = _remote(small_all.at[me], small_all.at[me], small_send.at[j], small_recv.at[j],
                         (peer // 4, (peer // 2) % 2, peer % 2))
            cp.start()
            pending.append(cp)
        for i in range(n_s):
            half = shards[i].shape[0] // 2
            theirs = outs[i].at[pl.ds(pl.multiple_of((1 - c) * half, 16), half)]
            _remote(theirs, theirs, send.at[i], recv.at[i], sibling).wait_recv()
        for j in range(N_DEV - 1):
            peer = (me + N_DEV - 1 - j) % N_DEV
            _remote(small_all.at[peer], small_all.at[peer], small_send.at[j], small_recv.at[j],
                    sibling).wait_recv()
        total = small_all[0]
        for dev in range(1, N_DEV):
            total = total + small_all[dev]
        small_out[...] = total
        for cp in pending:
            cp.wait_send()

    res = pl.pallas_call(
        body, name="share_halves",
        in_specs=[HBM_SPEC] * n_s + [VMEM_SPEC], out_specs=[HBM_SPEC] * n_s + [VMEM_SPEC],
        out_shape=[jax.ShapeDtypeStruct(sh.shape, sh.dtype) for sh in shards]
        + [jax.ShapeDtypeStruct((rows_s, 128), F32)],
        scratch_shapes=[pltpu.VMEM((N_DEV, rows_s, 128), F32),
                        pltpu.SemaphoreType.DMA((n_s,)), pltpu.SemaphoreType.DMA((n_s,)),
                        pltpu.SemaphoreType.DMA((N_DEV - 1,)), pltpu.SemaphoreType.DMA((N_DEV - 1,))],
        input_output_aliases={i: i for i in range(n_s)},
    )(*shards, small)
    return res[:n_s], res[n_s]


FINISH_CHUNK_ROWS = 256


def _finish_reduction(chip_sums, shares, small):
    n_w, n_s = len(chip_sums), len(shares)
    halves = [cs.shape[1] for cs in chip_sums]
    cols = shares[0].shape[1] if n_s else chip_sums[0].shape[2]
    wire = chip_sums[0].dtype if n_w else GRAD_WIRE_DTYPE
    rows_s = small.shape[0]
    ch = FINISH_CHUNK_ROWS

    def body(*refs):
        sums_in, small_ref, share_in = refs[:n_w], refs[n_w], refs[n_w + 1:n_w + 1 + n_s]
        o0 = n_w + 1 + n_s
        outs, small_out, share_out = refs[o0:o0 + n_w], refs[o0 + n_w], refs[o0 + n_w + 1:o0 + n_w + 1 + n_s]
        arrived = refs[o0 + n_w + 1 + n_s:o0 + 2 * n_w + 1 + n_s]
        (small_all, buf_in, buf_out, chip_send, chip_recv, fin_send, fin_recv, small_send, small_recv,
         share_send, share_recv, share_local, io_sem) = refs[o0 + 2 * n_w + 1 + n_s:]
        x, y, c = _mesh_position()
        chip = 2 * x + y
        me = 4 * x + 2 * y + c
        sibling = (x, y, 1 - c)
        others = _other_chips(x, y)
        pending, local = [], []

        for w in range(n_w):
            for k, (px, py) in enumerate(others):
                cp = _remote(sums_in[w].at[2 * px + py], arrived[w].at[k], chip_send.at[w, k],
                             chip_recv.at[w, k], (px, py, c))
                cp.start()
                pending.append(cp)
        small_all[me] = small_ref[...]
        for j in range(N_DEV - 1):
            peer = (me + 1 + j) % N_DEV
            cp = _remote(small_all.at[me], small_all.at[me], small_send.at[j], small_recv.at[j],
                         (peer // 4, (peer // 2) % 2, peer % 2))
            cp.start()
            pending.append(cp)

        def halves_of(out, rows):
            return [out.at[pl.ds(pl.multiple_of(which * rows, 16), rows)] for which in (c, 1 - c)]

        for i in range(n_s):
            mine, _ = halves_of(share_out[i], share_in[i].shape[0])
            cp = pltpu.make_async_copy(share_in[i], mine, share_local.at[i])
            cp.start()
            local.append(cp)
            cp = _remote(share_in[i], mine, share_send.at[i], share_recv.at[i], sibling)
            cp.start()
            pending.append(cp)

        def add_chunk(w, dst, start, size):
            total = None
            for src in [sums_in[w].at[chip]] + [arrived[w].at[k] for k in range(3)]:
                cp = pltpu.make_async_copy(src.at[pl.ds(start, size)], buf_in.at[pl.ds(0, size)], io_sem)
                cp.start()
                cp.wait()
                val = buf_in[pl.ds(0, size), :].astype(F32)
                total = val if total is None else total + val
            buf_out[pl.ds(0, size), :] = total
            cp = pltpu.make_async_copy(buf_out.at[pl.ds(0, size)], dst.at[pl.ds(start, size)], io_sem)
            cp.start()
            cp.wait()

        for w in range(n_w):
            for k, (px, py) in enumerate(others):
                _remote(sums_in[w].at[chip], arrived[w].at[k], chip_send.at[w, k], chip_recv.at[w, k],
                        (px, py, c)).wait_recv()
            mine, _ = halves_of(outs[w], halves[w])
            n_full = halves[w] // ch

            def loop_body(i, carry, w=w, mine=mine):
                add_chunk(w, mine, pl.multiple_of(i * ch, ch), ch)
                return carry

            lax.fori_loop(0, n_full, loop_body, 0)
            if halves[w] % ch:
                add_chunk(w, mine, n_full * ch, halves[w] - n_full * ch)
            cp = _remote(mine, mine, fin_send.at[w], fin_recv.at[w], sibling)
            cp.start()
            pending.append(cp)
        for w in range(n_w):
            _, theirs = halves_of(outs[w], halves[w])
            _remote(theirs, theirs, fin_send.at[w], fin_recv.at[w], sibling).wait_recv()
        for i in range(n_s):
            _, theirs = halves_of(share_out[i], share_in[i].shape[0])
            _remote(share_in[i], theirs, share_send.at[i], share_recv.at[i], sibling).wait_recv()

        for j in range(N_DEV - 1):
            peer = (me + N_DEV - 1 - j) % N_DEV
            _remote(small_all.at[peer], small_all.at[peer], small_send.at[j], small_recv.at[j],
                    sibling).wait_recv()
        total = small_all[0]
        for dev in range(1, N_DEV):
            total = total + small_all[dev]
        small_out[...] = total
        for cp in pending:
            cp.wait_send()
        for cp in local:
            cp.wait()

    def sems(n):
        return pltpu.SemaphoreType.DMA((max(n, 1),))

    sds = jax.ShapeDtypeStruct
    out_shape = ([sds((2 * h, cols), F32) for h in halves] + [sds((rows_s, 128), F32)]
                 + [sds((2 * sh.shape[0], cols), F32) for sh in shares]
                 + [sds((3, h, cols), wire) for h in halves])
    res = pl.pallas_call(
        body, name="finish_reduction",
        in_specs=[HBM_SPEC] * n_w + [VMEM_SPEC] + [HBM_SPEC] * n_s,
        out_specs=[HBM_SPEC] * n_w + [VMEM_SPEC] + [HBM_SPEC] * (n_s + n_w),
        out_shape=out_shape,
        scratch_shapes=[
            pltpu.VMEM((N_DEV, rows_s, 128), F32), pltpu.VMEM((ch, cols), wire), pltpu.VMEM((ch, cols), F32),
            pltpu.SemaphoreType.DMA((max(n_w, 1), 3)), pltpu.SemaphoreType.DMA((max(n_w, 1), 3)),
            sems(n_w), sems(n_w), sems(N_DEV - 1), sems(N_DEV - 1), sems(n_s), sems(n_s), sems(n_s),
            pltpu.SemaphoreType.DMA,
        ],
        compiler_params=pltpu.CompilerParams(vmem_limit_bytes=VMEM_LIMIT_MIN),
    )(*chip_sums, small, *shares)
    return res[:n_w], res[n_w + 1:n_w + 1 + n_s], res[n_w]


def _gather_weights(shards):
    n_w = len(shards)
    halves = [s.shape[0] // 2 for s in shards]

    def body(*refs):
        ins, outs = refs[:n_w], refs[n_w:2 * n_w]
        local_sems, send_sems, recv_sems, fsend_sems, frecv_sems = refs[2 * n_w:]
        x, y, c = _mesh_position()
        chip = 2 * x + y
        sibling = (x, y, 1 - c)
        others = _other_chips(x, y)

        def region(w, owner_chip, half):
            start = owner_chip * (2 * halves[w]) + half * halves[w]
            return outs[w].at[pl.ds(pl.multiple_of(start, 16), halves[w])]

        def remote(src, dst, ssem, rsem, dev):
            return pltpu.make_async_remote_copy(src_ref=src, dst_ref=dst, send_sem=ssem, recv_sem=rsem,
                                                device_id=dev, device_id_type=MESH)

        local = []
        for w in range(n_w):
            cp = pltpu.make_async_copy(ins[w], outs[w].at[pl.ds(pl.multiple_of(chip * 2 * halves[w], 16),
                                                                2 * halves[w])], local_sems.at[w])
            cp.start()
            local.append(cp)
        sends = []
        for w in range(n_w):
            mine = ins[w].at[pl.ds(pl.multiple_of(c * halves[w], 16), halves[w])]
            for k, (px, py) in enumerate(others):
                cp = remote(mine, region(w, chip, c), send_sems.at[w, k], recv_sems.at[w, k], (px, py, c))
                cp.start()
                sends.append(cp)
        for k, (px, py) in enumerate(others):
            for w in range(n_w):
                landed = region(w, 2 * px + py, c)
                remote(landed, landed, send_sems.at[w, k], recv_sems.at[w, k], (px, py, c)).wait_recv()
                cp = remote(landed, landed, fsend_sems.at[w, k], frecv_sems.at[w, k], sibling)
                cp.start()
                sends.append(cp)
        for k, (px, py) in enumerate(others):
            for w in range(n_w):
                passed = region(w, 2 * px + py, 1 - c)
                remote(passed, passed, fsend_sems.at[w, k], frecv_sems.at[w, k], sibling).wait_recv()
        for cp in sends:
            cp.wait_send()
        for cp in local:
            cp.wait()

    return pl.pallas_call(
        body, name="gather_weights",
        in_specs=[HBM_SPEC] * n_w, out_specs=[HBM_SPEC] * n_w,
        out_shape=[jax.ShapeDtypeStruct((N_CHIPS * s.shape[0], s.shape[1]), s.dtype) for s in shards],
        scratch_shapes=[pltpu.SemaphoreType.DMA((n_w,))] + [pltpu.SemaphoreType.DMA((n_w, 3))] * 4,
    )(*shards)


REDUCE_CHUNK_ROWS = 256


def _reduce_gradients(grads, shares, small):
    n_w = len(grads)
    rows = [g.shape[0] // N_CHIPS for g in grads]
    halves = [r // 2 for r in rows]
    cols = grads[0].shape[1]
    wire = grads[0].dtype
    ch = REDUCE_CHUNK_ROWS
    for h in halves:
        assert h % 16 == 0
    rows_s = small.shape[0]
    n_s = len(shares)

    def body(*refs):
        g_in = refs[:n_w]
        small_ref = refs[n_w]
        share_in = refs[n_w + 1:n_w + 1 + n_s]
        refs = refs[:n_w + 1] + refs[n_w + 1 + n_s:]
        outs = refs[n_w + 1:2 * n_w + 1]
        small_out = refs[2 * n_w + 1]
        from_sib = refs[2 * n_w + 2:3 * n_w + 2]
        chip_sum = refs[3 * n_w + 2:4 * n_w + 2]
        from_chips = refs[4 * n_w + 2:5 * n_w + 2]
        share_out = refs[5 * n_w + 2:5 * n_w + 2 + n_s]
        (small_all, buf_a, buf_b, buf_o, sib_send, sib_recv, chip_send, chip_recv,
         fin_send, fin_recv, small_send, small_recv, io_sem,
         share_send, share_recv, share_local) = refs[5 * n_w + 2 + n_s:]
        x, y, c = _mesh_position()
        chip = 2 * x + y
        me = 4 * x + 2 * y + c
        sibling = (x, y, 1 - c)
        others = _other_chips(x, y)

        def remote(src, dst, ssem, rsem, dev):
            return pltpu.make_async_remote_copy(src_ref=src, dst_ref=dst, send_sem=ssem, recv_sem=rsem,
                                                device_id=dev, device_id_type=MESH)

        def part(w, owner_chip, half):
            start = owner_chip * rows[w] + half * halves[w]
            return g_in[w].at[pl.ds(pl.multiple_of(start, 16), halves[w])]

        pending = []
        small_all[me] = small_ref[...]
        for j in range(N_DEV - 1):
            peer = (me + 1 + j) % N_DEV
            cp = remote(small_all.at[me], small_all.at[me], small_send.at[j], small_recv.at[j],
                        (peer // 4, (peer // 2) % 2, peer % 2))
            cp.start()
            pending.append(cp)

        local = []
        for i in range(n_s):
            half_rows = share_in[i].shape[0]
            place = share_out[i].at[pl.ds(pl.multiple_of(c * half_rows, 16), half_rows)]
            cp = pltpu.make_async_copy(share_in[i], place, share_local.at[i])
            cp.start()
            local.append(cp)
            cp = remote(share_in[i], place, share_send.at[i], share_recv.at[i], sibling)
            cp.start()
            pending.append(cp)

        for w in range(n_w):
            for k in range(N_CHIPS):
                cp = remote(part(w, k, 1 - c), from_sib[w].at[k], sib_send.at[w, k], sib_recv.at[w, k], sibling)
                cp.start()
                pending.append(cp)

        def add_stream(w, srcs, dst, n_rows):
            def chunk(start, size):
                total = None
                for i, src in enumerate(srcs):
                    buf = buf_a if i % 2 == 0 else buf_b
                    cp = pltpu.make_async_copy(src.at[pl.ds(start, size)], buf.at[pl.ds(0, size)], io_sem)
                    cp.start()
                    cp.wait()
                    val = buf[pl.ds(0, size), :].astype(F32)
                    total = val if total is None else total + val
                return total

            n_full = n_rows // ch
            rem = n_rows - n_full * ch

            def store(total, start, size):
                if dst.dtype == F32:
                    buf_o[pl.ds(0, size), :] = total
                    cp = pltpu.make_async_copy(buf_o.at[pl.ds(0, size)], dst.at[pl.ds(start, size)], io_sem)
                else:
                    buf_a[pl.ds(0, size), :] = total.astype(buf_a.dtype)
                    cp = pltpu.make_async_copy(buf_a.at[pl.ds(0, size)], dst.at[pl.ds(start, size)], io_sem)
                cp.start()
                cp.wait()

            def loop_body(i, carry):
                start = pl.multiple_of(i * ch, ch)
                store(chunk(start, ch), start, ch)
                return carry

            lax.fori_loop(0, n_full, loop_body, 0)
            if rem:
                store(chunk(n_full * ch, rem), n_full * ch, rem)

        order = [2, 0, 1]
        for w in range(n_w):
            for k in range(N_CHIPS):
                remote(part(w, k, 1 - c), from_sib[w].at[k], sib_send.at[w, k], sib_recv.at[w, k],
                       sibling).wait_recv()
        for k in order:
            px, py = others[k]
            owner = 2 * px + py
            for w in range(n_w):
                add_stream(w, [part(w, owner, c), from_sib[w].at[owner]], chip_sum[w].at[owner], halves[w])
                cp = remote(chip_sum[w].at[owner], from_chips[w].at[k], chip_send.at[w, k],
                            chip_recv.at[w, k], (px, py, c))
                cp.start()
                pending.append(cp)
        for w in range(n_w):
            add_stream(w, [part(w, chip, c), from_sib[w].at[chip]], chip_sum[w].at[chip], halves[w])

        for w in range(n_w):
            for k in range(3):
                px, py = others[k]
                remote(chip_sum[w].at[chip], from_chips[w].at[k], chip_send.at[w, k], chip_recv.at[w, k],
                       (px, py, c)).wait_recv()
            mine = outs[w].at[pl.ds(pl.multiple_of(c * halves[w], 16), halves[w])]
            add_stream(w, [chip_sum[w].at[chip], from_chips[w].at[0], from_chips[w].at[1],
                           from_chips[w].at[2]], mine, halves[w])
            cp = remote(mine, mine, fin_send.at[w], fin_recv.at[w], sibling)
            cp.start()
            pending.append(cp)
        for w in range(n_w):
            theirs = outs[w].at[pl.ds(pl.multiple_of((1 - c) * halves[w], 16), halves[w])]
            remote(theirs, theirs, fin_send.at[w], fin_recv.at[w], sibling).wait_recv()

        for j in range(N_DEV - 1):
            peer = (me + N_DEV - 1 - j) % N_DEV
            remote(small_all.at[peer], small_all.at[peer], small_send.at[j], small_recv.at[j],
                   sibling).wait_recv()
        total = small_all[0]
        for d in range(1, N_DEV):
            total = total + small_all[d]
        small_out[...] = total
        for i in range(n_s):
            half_rows = share_in[i].shape[0]
            theirs = share_out[i].at[pl.ds(pl.multiple_of((1 - c) * half_rows, 16), half_rows)]
            remote(share_in[i], theirs, share_send.at[i], share_recv.at[i], sibling).wait_recv()
        for cp in pending:
            cp.wait_send()
        for cp in local:
            cp.wait()

    hbm_scratch = ([jax.ShapeDtypeStruct((N_CHIPS, h, cols), wire) for h in halves] * 2
                   + [jax.ShapeDtypeStruct((3, h, cols), wire) for h in halves])
    out_shape = ([jax.ShapeDtypeStruct((r, cols), F32) for r in rows]
                 + [jax.ShapeDtypeStruct((rows_s, 128), F32)] + hbm_scratch
                 + [jax.ShapeDtypeStruct((2 * sh.shape[0], sh.shape[1]), F32) for sh in shares])
    res = pl.pallas_call(
        body, name="reduce_gradients",
        in_specs=[HBM_SPEC] * n_w + [VMEM_SPEC] + [HBM_SPEC] * n_s,
        out_specs=[HBM_SPEC] * n_w + [VMEM_SPEC] + [HBM_SPEC] * (3 * n_w + n_s),
        out_shape=out_shape,
        scratch_shapes=[
            pltpu.VMEM((N_DEV, rows_s, 128), F32),
            pltpu.VMEM((ch, cols), wire), pltpu.VMEM((ch, cols), wire), pltpu.VMEM((ch, cols), F32),
            pltpu.SemaphoreType.DMA((n_w, N_CHIPS)), pltpu.SemaphoreType.DMA((n_w, N_CHIPS)),
            pltpu.SemaphoreType.DMA((n_w, 3)), pltpu.SemaphoreType.DMA((n_w, 3)),
            pltpu.SemaphoreType.DMA((n_w,)), pltpu.SemaphoreType.DMA((n_w,)),
            pltpu.SemaphoreType.DMA((N_DEV - 1,)), pltpu.SemaphoreType.DMA((N_DEV - 1,)),
            pltpu.SemaphoreType.DMA,
            pltpu.SemaphoreType.DMA((max(n_s, 1),)), pltpu.SemaphoreType.DMA((max(n_s, 1),)),
            pltpu.SemaphoreType.DMA((max(n_s, 1),)),
        ],
        compiler_params=pltpu.CompilerParams(vmem_limit_bytes=VMEM_LIMIT_MIN),
    )(*grads, small, *shares)
    return res[:n_w], res[n_w], res[len(res) - n_s:] if n_s else []


def _pack_small(parts, rows):
    flat = jnp.concatenate([p.reshape(-1) for p in parts])
    flat = jnp.pad(flat, (0, rows * 128 - flat.shape[0]))
    return flat.reshape(rows, 128)


def _unpack_small(packed, shapes):
    flat = packed.reshape(-1)
    out, off = [], 0
    for shp in shapes:
        n = int(np.prod(shp))
        out.append(flat[off:off + n].reshape(shp))
        off += n
    return out


def kernel(x, g_attn, w_in, b_in, sinks_a, g_out_a, g_out_b, w_out, g_mlp, w_1, w_2, g_final, loss_target, m_g_attn, m_w_in, m_b_in, m_sinks_a, m_g_out_a, m_g_out_b, m_w_out, m_g_mlp, m_w_1, m_w_2, m_g_final, v_g_attn, v_w_in, v_b_in, v_sinks_a, v_g_out_a, v_g_out_b, v_w_out, v_g_mlp, v_w_1, v_w_2, v_g_final):
    s, d = x.shape[1], x.shape[2]
    d_in = b_in.shape[1]
    qa = g_out_a.shape[1]
    qb = g_out_b.shape[1]
    kva = 2 * N_KV_GROUPS * HEAD_DIM
    assert d_in == qa + kva + 3 * qb and qa + qb == w_out.shape[1] * N_CHIPS
    d_ff = w_1.shape[2] * N_CHIPS
    ff_shard = w_1.shape[2]
    in_shard = w_in.shape[2]
    n_heads_a, n_heads_b = qa // HEAD_DIM, qb // HEAD_DIM
    slopes_a, slopes_b = alibi_slopes(n_heads_a), alibi_slopes(n_heads_b)

    x2d = x[0]
    target = loss_target[0]

    core_index = lax.axis_index("c").astype(jnp.int32).reshape(1)
    chip_index = (2 * lax.axis_index("x") + lax.axis_index("y")).astype(jnp.int32).reshape(1)
    shards = [w_in[0].T.astype(BF16), w_out[0].astype(BF16), w_1[0].astype(BF16), w_2[0].astype(BF16)]
    halves = [sh.shape[0] // 2 for sh in shards]
    w_in_t, w_out_g, w_1_g, w_2_g = [_own_shard_in_place(sh, chip_index[0]) for sh in shards]

    tm = _tile(s, 1024)

    (h1, r1), (w_in_t,) = _norm_fwd("norm_attn", x2d, g_attn, hook=_gather_hook(w_in_t, 0, halves[0]))

    q_a, = _project_by_class("proj_qa", h1, w_in_t, b_in, 0, qa, (1,))
    kv_a, = _project_by_class("proj_kva", h1, w_in_t, b_in, qa, kva, (1,))
    q_bs, (w_out_g,) = _project_by_class("proj_qb", h1, w_in_t, b_in, qa + kva, qb, DILATIONS,
                                         hook=_gather_hook(w_out_g, 0, halves[1]))
    k_bs = _project_by_class("proj_kb", h1, w_in_t, b_in, qa + kva + qb, qb, DILATIONS)
    v_bs = _project_by_class("proj_vb", h1, w_in_t, b_in, qa + kva + 2 * qb, qb, DILATIONS)

    quarter = halves[2] // 4
    sinks = sinks_a.reshape(-1)
    (o_a, lse_a), (w_1_g,) = _attn_fwd("attn_a_fwd", q_a, kv_a, dil=1, max_steps=WINDOW_A - 1, slopes=slopes_a,
                                       sinks=sinks, hook=_gather_hook(w_1_g, 0, quarter))
    o_a = o_a[0]
    o_bs, lse_bs = [], []
    for n, (window, dil) in enumerate(DILATED_BRANCHES):
        (o, l), (w_1_g,) = _attn_fwd(f"attn_b{dil}_fwd", q_bs[n], (k_bs[n], v_bs[n]), dil=dil,
                                     max_steps=window // dil, slopes=slopes_b,
                                     hook=_gather_hook(w_1_g, (n + 1) * quarter, (n + 2) * quarter))
        o_bs.append(o)
        lse_bs.append(l)
    w_1_g = w_1_g.reshape(N_CHIPS, d, ff_shard)
    mix, o_b, *lse_tot, r_a, r_b = _mix_fwd(o_a, o_bs, lse_bs, g_out_a, g_out_b)

    tn = _tile(d, 512)
    a_spec, b_spec = _mm_specs("nn", tm, tn, d)
    tile_mn = pl.BlockSpec((tm, tn), lambda i, j, k: (i, j))
    x2 = _matmul("out_proj", mix, w_out_g, [x2d], mode="nn", grid=(s // tm, d // tn, 1),
                 a_spec=a_spec, b_spec=b_spec, extra_specs=[tile_mn],
                 out_shapes=[jax.ShapeDtypeStruct((s, d), F32)], out_specs=[tile_mn],
                 epilogue=lambda acc, res: (acc + res,))[0]

    h2, r2 = _norm_fwd("norm_mlp", x2, g_mlp)

    tn = _tile(ff_shard, 512)
    per = ff_shard // tn
    a_spec, _ = _mm_specs("nn", tm, tn, d)
    (u,), (w_2_g,) = _matmul(
        "mlp_up", h2, w_1_g, [], mode="nn", grid=(s // tm, d_ff // tn, 1),
        a_spec=a_spec, b_spec=pl.BlockSpec((None, d, tn), lambda i, j, k: (j // per, 0, j % per)),
        extra_specs=[], out_shapes=[jax.ShapeDtypeStruct((s, d_ff), BF16)], out_specs=[tile_mn],
        epilogue=lambda acc: (jnp.maximum(acc, 0.0),),
        hook=_gather_hook(w_2_g, 0, halves[3]))

    tn = _tile(d, 1024)
    tk = _tile(d_ff, 2048)
    a_spec, b_spec = _mm_specs("nn", tm, tn, tk)
    tile_mn = pl.BlockSpec((tm, tn), lambda i, j, k: (i, j))
    x3 = _matmul("mlp_down", u, w_2_g, [x2], mode="nn", grid=(s // tm, d // tn, d_ff // tk),
                 a_spec=a_spec, b_spec=b_spec, extra_specs=[tile_mn],
                 out_shapes=[jax.ShapeDtypeStruct((s, d), F32)], out_specs=[tile_mn],
                 prologue=lambda a: a * a, epilogue=lambda acc, res: (acc + res,), acc_shape=(tm, tn))[0]

    dx3, dx3b, loss_part, dg_final = _loss_head(x3, target, g_final.reshape(1, d))

    tn = _tile(d_ff, 512)
    a_spec, b_spec = _mm_specs("nt", tm, tn, d)
    tile_mn = pl.BlockSpec((tm, tn), lambda i, j, k: (i, j))
    dpre = _matmul("mlp_down_dx", dx3b, w_2_g, [u], mode="nt", grid=(s // tm, d_ff // tn, 1),
                   a_spec=a_spec, b_spec=b_spec, extra_specs=[tile_mn],
                   out_shapes=[jax.ShapeDtypeStruct((s, d_ff), BF16)], out_specs=[tile_mn],
                   epilogue=lambda acc, uu: (acc * (2.0 * uu.astype(F32)),))[0]

    wire = GRAD_WIRE_DTYPE
    tk_s = _tile(s, 2048)
    tmw = _tile(d_ff, 1024)
    a_spec, b_spec = _mm_specs("tn", tmw, d, tk_s)
    dw_2 = _matmul("mlp_down_dw", u, dx3b, [], mode="tn", grid=(d_ff // tmw, 1, s // tk_s),
                   a_spec=a_spec, b_spec=b_spec, extra_specs=[],
                   out_shapes=[jax.ShapeDtypeStruct((d_ff, d), wire)],
                   out_specs=[pl.BlockSpec((tmw, d), lambda i, j, k: (i, j))],
                   prologue=lambda a: a * a, epilogue=lambda acc: (acc,), acc_shape=(tmw, d))[0]

    tn = _tile(d, 1024)
    tk = _tile(ff_shard, 2048)
    per = ff_shard // tk
    a_spec, _ = _mm_specs("nt", tm, tn, tk)
    tile_mn = pl.BlockSpec((tm, tn), lambda i, j, k: (i, j))
    dh2 = _matmul("mlp_up_dx", dpre, w_1_g, [], mode="nt", grid=(s // tm, d // tn, d_ff // tk),
                  a_spec=a_spec, b_spec=pl.BlockSpec((None, tn, tk), lambda i, j, k: (k // per, j, k % per)),
                  extra_specs=[], out_shapes=[jax.ShapeDtypeStruct((s, d), F32)], out_specs=[tile_mn],
                  epilogue=lambda acc: (acc,), acc_shape=(tm, tn))[0]

    tmw = _tile(d, 1024)
    tnw = _tile(ff_shard, 2048)
    per = ff_shard // tnw
    a_spec, b_spec = _mm_specs("tn", tmw, tnw, tk_s)
    dw_1 = _matmul("mlp_up_dw", h2, dpre, [], mode="tn", grid=(d // tmw, d_ff // tnw, s // tk_s),
                   a_spec=a_spec, b_spec=b_spec, extra_specs=[],
                   out_shapes=[jax.ShapeDtypeStruct((N_CHIPS, d, ff_shard), wire)],
                   out_specs=[pl.BlockSpec((None, tmw, tnw), lambda i, j, k: (j // per, i, j % per))],
                   epilogue=lambda acc: (acc,), acc_shape=(tmw, tnw))[0]

    dw_1 = dw_1.reshape(N_CHIPS * d, ff_shard)
    (dx2, dx2b, dg_mlp), (sib_2, sib_1) = _norm_bwd(
        "norm_mlp_bwd", dh2, x2, r2, g_mlp, dx3, hook=_merge_hooks([_exchange_hook(dw_2), _exchange_hook(dw_1)]))
    chip_sum_2 = _chip_add("chip_add_w_2", dw_2, sib_2, core_index)
    chip_sum_1 = _chip_add("chip_add_w_1", dw_1, sib_1, core_index)

    tn = _tile(d, 512)
    a_spec, b_spec = _mm_specs("nt", tm, tn, d)
    tile_mn = pl.BlockSpec((tm, tn), lambda i, j, k: (i, j))
    dmix = _matmul("out_proj_dx", dx2b, w_out_g, [], mode="nt", grid=(s // tm, d // tn, 1),
                   a_spec=a_spec, b_spec=b_spec, extra_specs=[],
                   out_shapes=[jax.ShapeDtypeStruct((s, d), F32)], out_specs=[tile_mn],
                   epilogue=lambda acc: (acc,))[0]

    tmw = _tile(d, 1024)
    a_spec, b_spec = _mm_specs("tn", tmw, d, tk_s)
    dw_out = _matmul("out_proj_dw", mix, dx2b, [], mode="tn", grid=(d // tmw, 1, s // tk_s),
                     a_spec=a_spec, b_spec=b_spec, extra_specs=[],
                     out_shapes=[jax.ShapeDtypeStruct((d, d), wire)],
                     out_specs=[pl.BlockSpec((tmw, d), lambda i, j, k: (i, j))],
                     epilogue=lambda acc: (acc,), acc_shape=(tmw, d))[0]

    mix_grads, (sib_out,) = _mix_bwd(dmix, o_a, o_b, r_a, r_b, g_out_a, g_out_b, hook=_exchange_hook(dw_out))
    do_a, do_bs, delta_a, delta_bs = mix_grads[0], mix_grads[1:4], mix_grads[4], mix_grads[5:8]
    dg_out_a, dg_out_b = mix_grads[8:]
    chip_sum_out = _chip_add("chip_add_w_out", dw_out, sib_out, core_index)

    (dq_a, dkv_a, dsinks), (chips_2,) = _attn_bwd(
        "attn_a_bwd", q_a, kv_a, do_a[None], lse_a, delta_a[None], dil=1, max_steps=WINDOW_A - 1,
        slopes=slopes_a, sinks=sinks, hook=_scatter_hook(chip_sum_2))
    dqs, dks, dvs = [], [], []
    scatter = {1: chip_sum_1, 4: chip_sum_out}
    arrived = {}
    for n, (window, dil) in enumerate(DILATED_BRANCHES):
        res = _attn_bwd(f"attn_b{dil}_bwd", q_bs[n], (k_bs[n], v_bs[n]), do_bs[n], lse_tot[n],
                        delta_bs[n], dil=dil, max_steps=window // dil, slopes=slopes_b,
                        hook=_scatter_hook(scatter[dil]) if dil in scatter else None)
        if dil in scatter:
            res, (arrived[dil],) = res
        dq, dk, dv = res
        dqs.append(dq)
        dks.append(dk)
        dvs.append(dv)
    half_2 = _final_add("final_add_w_2", chip_sum_2, chips_2, chip_index, core_index)
    half_1 = _final_add("final_add_w_1", chip_sum_1, arrived[1], chip_index, core_index)
    half_out = _final_add("final_add_w_out", chip_sum_out, arrived[4], chip_index, core_index)
    dproj, db_in = _assemble_dproj(dq_a[0], dkv_a[0], dqs, dks, dvs)

    tmw = d_in // 2 if (d_in // 2) % 128 == 0 else d_in
    tnw = _tile(d, 1024)
    tk_s = _tile(s, 1024)
    a_spec, b_spec = _mm_specs("tn", tmw, tnw, tk_s)
    dw_in_t = _matmul("in_proj_dw", dproj, h1, [], mode="tn", grid=(d_in // tmw, d // tnw, s // tk_s),
                      a_spec=a_spec, b_spec=b_spec, extra_specs=[],
                      out_shapes=[jax.ShapeDtypeStruct((d_in, d), wire)],
                      out_specs=[pl.BlockSpec((tmw, tnw), lambda i, j, k: (i, j))],
                      epilogue=lambda acc: (acc,), acc_shape=(tmw, tnw))[0]

    tn = _tile(d, 512)
    a_spec, b_spec = _mm_specs("nn", tm, tn, d_in)
    tile_mn = pl.BlockSpec((tm, tn), lambda i, j, k: (i, j))
    (dh1,), (sib_in,) = _matmul("in_proj_dx", dproj, w_in_t, [], mode="nn", grid=(s // tm, d // tn, 1),
                                a_spec=a_spec, b_spec=b_spec, extra_specs=[],
                                out_shapes=[jax.ShapeDtypeStruct((s, d), F32)], out_specs=[tile_mn],
                                epilogue=lambda acc: (acc,), hook=_exchange_hook(dw_in_t))
    chip_sum_in = _chip_add("chip_add_w_in", dw_in_t, sib_in, core_index)

    (grad_x, _, dg_attn), (chips_in,) = _norm_bwd("norm_attn_bwd", dh1, x2d, r1, g_attn, dx2,
                                                  hook=_scatter_hook(chip_sum_in))
    half_in = _final_add("final_add_w_in", chip_sum_in, chips_in, chip_index, core_index)

    small_parts = [dg_attn, db_in, dsinks[:, :n_heads_a], dg_out_a, dg_out_b, dg_mlp, dg_final]
    small_shapes = [g_attn.shape, b_in.shape, sinks_a.shape, g_out_a.shape, g_out_b.shape, g_mlp.shape,
                    g_final.shape]
    n_small = sum(int(np.prod(shp)) for shp in small_shapes)
    rows_s = -(-n_small // (8 * 128)) * 8
    (gw_in_t, gw_out, gw_1, gw_2), small_sum = _share_halves(
        [half_in, half_out, half_1, half_2], _pack_small(small_parts, rows_s))
    gw_in = gw_in_t.T
    g_small = _unpack_small(small_sum, small_shapes)

    upd_in = _adamw("adamw_w_in", w_in[0], gw_in, m_w_in[0], v_w_in[0])
    upd_out = _adamw("adamw_w_out", w_out[0], gw_out, m_w_out[0], v_w_out[0])
    upd_1 = _adamw("adamw_w_1", w_1[0], gw_1, m_w_1[0], v_w_1[0])
    upd_2 = _adamw("adamw_w_2", w_2[0], gw_2, m_w_2[0], v_w_2[0])
    small_w = [g_attn, b_in, sinks_a, g_out_a, g_out_b, g_mlp, g_final]
    small_m = [m_g_attn, m_b_in, m_sinks_a, m_g_out_a, m_g_out_b, m_g_mlp, m_g_final]
    small_v = [v_g_attn, v_b_in, v_sinks_a, v_g_out_a, v_g_out_b, v_g_mlp, v_g_final]
    upd_small = _adamw("adamw_small", _pack_small(small_w, rows_s), small_sum,
                       _pack_small(small_m, rows_s), _pack_small(small_v, rows_s))
    d_small, m_small, v_small = [_unpack_small(t, small_shapes) for t in upd_small]

    loss = lax.psum(loss_part[0, 0], ("x", "y", "c"))

    def ordered(small, big):
        w_in_v, w_out_v, w_1_v, w_2_v = big
        return [small[0], w_in_v[None], small[1], small[2], small[3], small[4], w_out_v[None], small[5],
                w_1_v[None], w_2_v[None], small[6]]

    grads = ordered(g_small, (gw_in, gw_out, gw_1, gw_2))
    deltas = ordered(d_small, (upd_in[0], upd_out[0], upd_1[0], upd_2[0]))
    new_m = ordered(m_small, (upd_in[1], upd_out[1], upd_1[1], upd_2[1]))
    new_v = ordered(v_small, (upd_in[2], upd_out[2], upd_1[2], upd_2[2]))
    return (loss, grad_x[None], *grads, *deltas, *new_m, *new_v)
```

```python
import functools

import jax
import jax.numpy as jnp
import numpy as np
from jax import lax
from jax.experimental import pallas as pl
from jax.experimental.pallas import tpu as pltpu

F32 = jnp.float32
BF16 = jnp.bfloat16

HEAD_DIM = 64
BLOCK = 128
PAIR = 2 * HEAD_DIM
N_KV_GROUPS = 2
WINDOW_A = 128
DILATED_BRANCHES = ((128, 1), (512, 4), (2048, 16))
EPS = 1e-5
NEG_INF = -1e30
ATT_SCALE = HEAD_DIM ** -0.5

ADAM_LR = 0.001
ADAM_B1 = 0.9
ADAM_B2 = 0.999
ADAM_EPS = 1e-08
ADAM_WD = 0.01
ADAM_STEP = 10

N_CHIPS = 4
N_DEV = 8
MESH = pl.DeviceIdType.MESH
GRAD_WIRE_DTYPE = jnp.bfloat16
BRANCH_DTYPE = jnp.bfloat16

VMEM_CAPACITY_V7X = 64 * 1024 * 1024
VMEM_LIMIT_MAX = 56 * 1024 * 1024
VMEM_LIMIT_MIN = 48 * 1024 * 1024

HBM_SPEC = pl.BlockSpec(memory_space=pltpu.HBM)
VMEM_SPEC = pl.BlockSpec(memory_space=pltpu.VMEM)
SMEM_SPEC = pl.BlockSpec(memory_space=pltpu.SMEM)


def _nbytes(shape, dtype):
    return int(np.prod([s for s in shape if s is not None])) * jnp.dtype(dtype).itemsize


def _params(semantics, block_bytes):
    limit = min(max(2 * block_bytes + (4 << 20), VMEM_LIMIT_MIN), VMEM_LIMIT_MAX)
    return pltpu.CompilerParams(dimension_semantics=semantics, vmem_limit_bytes=limit)


class _Hook:
    def __init__(self, operands, out_shape, sems, start, finish, mid=None, aliases=None):
        self.operands, self.out_shape, self.sems = list(operands), list(out_shape), list(sems)
        self.start, self.mid, self.finish = start, mid, finish
        self.aliases = dict(aliases or {})


def _merge_hooks(hooks):
    hooks = [h for h in hooks if h is not None]
    if len(hooks) <= 1:
        return hooks[0] if hooks else None
    n_op = np.cumsum([0] + [len(h.operands) for h in hooks])
    n_out = np.cumsum([0] + [len(h.out_shape) for h in hooks])
    n_sem = np.cumsum([0] + [len(h.sems) for h in hooks])

    def run(which):
        def fn(ops, outs, sems):
            for i, h in enumerate(hooks):
                f = getattr(h, which)
                if f is not None:
                    f(ops[n_op[i]:n_op[i + 1]], outs[n_out[i]:n_out[i + 1]], sems[n_sem[i]:n_sem[i + 1]])
        return fn

    aliases = {}
    for i, h in enumerate(hooks):
        aliases.update({int(n_op[i]) + a: int(n_out[i]) + b for a, b in h.aliases.items()})
    return _Hook(sum([h.operands for h in hooks], []), sum([h.out_shape for h in hooks], []),
                 sum([h.sems for h in hooks], []), run("start"), run("finish"),
                 run("mid") if any(h.mid for h in hooks) else None, aliases)


HOOK_MID_FRACTION = 0.6


def _call(body, hook, *, name, grid, in_specs, out_specs, out_shape, scratch_shapes=(), compiler_params):
    in_specs, out_specs, out_shape = list(in_specs), list(out_specs), list(out_shape)
    scratch_shapes = list(scratch_shapes)
    if hook is None:
        call = pl.pallas_call(body, name=name, grid=grid, in_specs=in_specs, out_specs=out_specs,
                              out_shape=out_shape, scratch_shapes=scratch_shapes,
                              compiler_params=compiler_params)
        return lambda *operands: (call(*operands), [])
    n_in, n_hin, n_out, n_hout, n_scr = (len(in_specs), len(hook.operands), len(out_specs),
                                         len(hook.out_shape), len(scratch_shapes))
    total = int(np.prod(grid))
    t_mid = min(int(total * HOOK_MID_FRACTION), total - 1)

    def wrapped(*refs):
        ins, h_in = refs[:n_in], refs[n_in:n_in + n_hin]
        o0 = n_in + n_hin
        outs, h_out = refs[o0:o0 + n_out], refs[o0 + n_out:o0 + n_out + n_hout]
        s0 = o0 + n_out + n_hout
        scr, h_sems = refs[s0:s0 + n_scr], refs[s0 + n_scr:]
        t = pl.program_id(0)
        for axis in range(1, len(grid)):
            t = t * grid[axis] + pl.program_id(axis)

        @pl.when(t == 0)
        def _():
            hook.start(h_in, h_out, h_sems)

        body(*ins, *outs, *scr)
        if hook.mid is not None:
            @pl.when(t == t_mid)
            def _():
                hook.mid(h_in, h_out, h_sems)

        @pl.when(t == total - 1)
        def _():
            hook.finish(h_in, h_out, h_sems)

    params = pltpu.CompilerParams(dimension_semantics=("arbitrary",) * len(grid),
                                  vmem_limit_bytes=compiler_params.vmem_limit_bytes)
    call = pl.pallas_call(
        wrapped, name=name, grid=grid,
        in_specs=in_specs + [HBM_SPEC] * n_hin, out_specs=out_specs + [HBM_SPEC] * n_hout,
        out_shape=out_shape + hook.out_shape, scratch_shapes=scratch_shapes + hook.sems,
        input_output_aliases={n_in + a: n_out + b for a, b in hook.aliases.items()},
        compiler_params=params)

    def run(*operands):
        res = call(*operands, *hook.operands)
        return res[:n_out], res[n_out:]

    return run


def _remote(src, dst, send_sem, recv_sem, device):
    return pltpu.make_async_remote_copy(src_ref=src, dst_ref=dst, send_sem=send_sem, recv_sem=recv_sem,
                                        device_id=device, device_id_type=MESH)


def alibi_slopes(n):
    return [float(v) for v in np.asarray(2.0 ** (-8.0 * (np.arange(n) + 1) / n), dtype=np.float32)]


def _matmul(name, a, b, extras, *, mode, grid, a_spec, b_spec, extra_specs, out_shapes, out_specs,
            epilogue, prologue=None, acc_shape=None, hook=None):
    dims = {"nn": ((1,), (0,)), "nt": ((1,), (1,)), "tn": ((0,), (0,))}[mode]
    nk = grid[2]
    n_ex, n_out = len(extras), len(out_shapes)

    def body(a_ref, b_ref, *rest):
        ex, outs = rest[:n_ex], rest[n_ex:n_ex + n_out]
        av = a_ref[...]
        if prologue is not None:
            av = prologue(av)
        part = lax.dot_general(av, b_ref[...], (dims, ((), ())), preferred_element_type=F32)

        def finish(acc):
            res = epilogue(acc, *[e[...] for e in ex])
            for o, r in zip(outs, res):
                o[...] = r.astype(o.dtype)

        if nk == 1:
            finish(part)
        else:
            acc_ref = rest[-1]
            k = pl.program_id(2)

            @pl.when(k == 0)
            def _():
                acc_ref[...] = part

            @pl.when(k > 0)
            def _():
                acc_ref[...] += part

            @pl.when(k == nk - 1)
            def _():
                finish(acc_ref[...])

    blocks = [(a_spec.block_shape, a.dtype), (b_spec.block_shape, b.dtype)]
    blocks += [(s.block_shape, e.dtype) for s, e in zip(extra_specs, extras)]
    blocks += [(s.block_shape, o.dtype) for s, o in zip(out_specs, out_shapes)]
    nbytes = sum(_nbytes(s, d) for s, d in blocks)
    scratch = []
    if nk > 1:
        scratch.append(pltpu.VMEM(acc_shape, F32))
        nbytes += _nbytes(acc_shape, F32)
    res, hook_res = _call(
        body, hook, name=name, grid=grid,
        in_specs=[a_spec, b_spec, *extra_specs], out_specs=list(out_specs), out_shape=list(out_shapes),
        scratch_shapes=scratch,
        compiler_params=_params(("parallel", "parallel", "arbitrary"), nbytes),
    )(a, b, *extras)
    return res if hook is None else (res, hook_res)


def _mm_specs(mode, tm, tn, tk, b_block=None, b_map=None):
    if mode == "tn":
        a_spec = pl.BlockSpec((tk, tm), lambda i, j, k: (k, i))
    else:
        a_spec = pl.BlockSpec((tm, tk), lambda i, j, k: (i, k))
    if b_block is not None:
        b_spec = pl.BlockSpec(b_block, b_map)
    elif mode == "nt":
        b_spec = pl.BlockSpec((tn, tk), lambda i, j, k: (j, k))
    else:
        b_spec = pl.BlockSpec((tk, tn), lambda i, j, k: (k, j))
    return a_spec, b_spec


def _project_by_class(name, h, w_t, bias, row_off, width, dilations, hook=None):
    s, d = h.shape
    tm = _tile(s, 1024)
    tn = 512 if width % 512 == 0 and row_off % 512 == 0 else _tile(width, 256)
    off = row_off // tn
    assert row_off % tn == 0 and tn % 128 == 0
    n_out = len(dilations)

    def body(h_ref, w_ref, b_ref, *rest):
        outs, perm_ref = rest[:n_out], rest[n_out]
        acc = lax.dot_general(h_ref[...], w_ref[...], (((1,), (1,)), ((), ())), preferred_element_type=F32)
        acc = acc + b_ref[...]
        for j in range(tn // 128):
            cols = slice(j * 128, (j + 1) * 128)
            for o_ref, dil in zip(outs, dilations):
                _to_classes(o_ref, cols, acc[:, cols], perm_ref, dil)

    blocks = tm * d * 2 + tn * d * 2 + 3 * tm * tn * 2 + tm * 128 * 4
    res, hook_res = _call(
        body, hook, name=name, grid=(s // tm, width // tn),
        in_specs=[pl.BlockSpec((tm, d), lambda i, j: (i, 0)), pl.BlockSpec((tn, d), lambda i, j: (j + off, 0)),
                  pl.BlockSpec((1, tn), lambda i, j: (0, j + off))],
        out_specs=[pl.BlockSpec((dil, tm // dil, tn), lambda i, j: (0, i, j)) for dil in dilations],
        out_shape=[_class_shape(dil, s, width, BF16) for dil in dilations],
        scratch_shapes=[pltpu.VMEM((tm, 128), F32)],
        compiler_params=_params(("parallel", "parallel"), blocks),
    )(h, w_t, bias)
    return res if hook is None else (res, hook_res)


def _tile(n, want):
    if n <= want:
        return n
    t = (want // 128) * 128
    while t > 128 and n % t:
        t -= 128
    assert n % t == 0, (n, want)
    return t


def _row_tile(s):
    return 256 if s % 256 == 0 else s


def _norm_fwd(name, x, g, hook=None):
    s, d = x.shape
    tm = _row_tile(s)

    def body(x_ref, g_ref, h_ref, r_ref):
        xv = x_ref[...]
        r = lax.rsqrt(jnp.mean(xv * xv, axis=-1, keepdims=True) + EPS)
        h_ref[...] = ((xv * r) * g_ref[...]).astype(BF16)
        r_ref[...] = r

    row = pl.BlockSpec((tm, d), lambda i: (i, 0))
    res, hook_res = _call(
        body, hook, name=name, grid=(s // tm,),
        in_specs=[row, pl.BlockSpec((1, d), lambda i: (0, 0))],
        out_specs=[row, pl.BlockSpec((tm, 1), lambda i: (i, 0))],
        out_shape=[jax.ShapeDtypeStruct((s, d), BF16), jax.ShapeDtypeStruct((s, 1), F32)],
        compiler_params=_params(("parallel",), tm * d * 6),
    )(x, g)
    return res if hook is None else (res, hook_res)


def _norm_bwd(name, dh, x, r, g, dres, hook=None):
    s, d = x.shape
    tm = _row_tile(s)

    def body(dh_ref, x_ref, r_ref, g_ref, dres_ref, dx_ref, dxb_ref, dg_ref):
        rv = r_ref[...]
        xn = x_ref[...] * rv
        dhv = dh_ref[...]
        dxn = dhv * g_ref[...]
        dx = dres_ref[...] + rv * (dxn - xn * jnp.mean(dxn * xn, axis=-1, keepdims=True))
        dx_ref[...] = dx
        dxb_ref[...] = dx.astype(BF16)
        part = jnp.sum(dhv * xn, axis=0, keepdims=True)

        @pl.when(pl.program_id(0) == 0)
        def _():
            dg_ref[...] = part

        @pl.when(pl.program_id(0) > 0)
        def _():
            dg_ref[...] += part

    row = pl.BlockSpec((tm, d), lambda i: (i, 0))
    vec = pl.BlockSpec((1, d), lambda i: (0, 0))
    res, hook_res = _call(
        body, hook, name=name, grid=(s // tm,),
        in_specs=[row, row, pl.BlockSpec((tm, 1), lambda i: (i, 0)), vec, row],
        out_specs=[row, row, vec],
        out_shape=[jax.ShapeDtypeStruct((s, d), F32), jax.ShapeDtypeStruct((s, d), BF16),
                   jax.ShapeDtypeStruct((1, d), F32)],
        compiler_params=_params(("arbitrary",), tm * d * 18),
    )(dh, x, r, g, dres)
    return res if hook is None else (res, hook_res)


def _loss_head(x3, target, g):
    s, d = x3.shape
    tm = _row_tile(s)

    def body(x_ref, t_ref, g_ref, dx_ref, dxb_ref, loss_ref, dg_ref):
        xv = x_ref[...]
        gv = g_ref[...]
        r = lax.rsqrt(jnp.mean(xv * xv, axis=-1, keepdims=True) + EPS)
        xn = xv * r
        err = xn * gv - t_ref[...]
        loss = 0.5 * jnp.sum(jnp.mean(err * err, axis=-1, keepdims=True), axis=0, keepdims=True)
        dy = err / d
        dxn = dy * gv
        dx = r * (dxn - xn * jnp.mean(dxn * xn, axis=-1, keepdims=True))
        dx_ref[...] = dx
        dxb_ref[...] = dx.astype(BF16)
        dg = jnp.sum(dy * xn, axis=0, keepdims=True)
        loss_row = jnp.broadcast_to(loss, (1, 128))

        @pl.when(pl.program_id(0) == 0)
        def _():
            dg_ref[...] = dg
            loss_ref[...] = loss_row

        @pl.when(pl.program_id(0) > 0)
        def _():
            dg_ref[...] += dg
            loss_ref[...] += loss_row

    row = pl.BlockSpec((tm, d), lambda i: (i, 0))
    vec = pl.BlockSpec((1, d), lambda i: (0, 0))
    return _call(
        body, None, name="loss_head", grid=(s // tm,),
        in_specs=[row, row, vec],
        out_specs=[row, row, pl.BlockSpec((1, 128), lambda i: (0, 0)), vec],
        out_shape=[jax.ShapeDtypeStruct((s, d), F32), jax.ShapeDtypeStruct((s, d), BF16),
                   jax.ShapeDtypeStruct((1, 128), F32), jax.ShapeDtypeStruct((1, d), F32)],
        compiler_params=_params(("arbitrary",), tm * d * 14),
    )(x3, target, g)[0]


def _low_lanes(rows):
    return lax.broadcasted_iota(jnp.int32, (rows, PAIR), 1) < HEAD_DIM


def _to_classes(dst_ref, cols, value, perm_ref, dil):
    rows = value.shape[0]
    if dil == 1:
        dst_ref[0, :, cols] = value.astype(dst_ref.dtype)
        return
    perm_ref[...] = value
    for r in range(dil):
        dst_ref[r, :, cols] = perm_ref[pl.ds(r, rows // dil, stride=dil), :].astype(dst_ref.dtype)


def _from_classes(src_ref, cols, perm_ref, dil):
    if dil == 1:
        return src_ref[0, :, cols].astype(F32)
    rows = perm_ref.shape[0]
    for r in range(dil):
        perm_ref[pl.ds(r, rows // dil, stride=dil), :] = src_ref[r, :, cols].astype(F32)
    return perm_ref[...]


def _class_spec(dil, tm, width):
    return pl.BlockSpec((dil, tm // dil, width), lambda i: (0, i, 0))


def _class_shape(dil, s, width, dtype):
    return jax.ShapeDtypeStruct((dil, s // dil, width), dtype)


DILATIONS = tuple(d for _, d in DILATED_BRANCHES)


def _mix_fwd(oa, obs, lses, ga, gb):
    s, qa = oa.shape
    qb = obs[0].shape[2]
    tm = _row_tile(s)
    all_lanes = slice(0, 128)

    def body(oa_ref, o1_ref, o2_ref, o3_ref, l1_ref, l2_ref, l3_ref, ga_ref, gb_ref,
             mix_ref, ob_ref, t1_ref, t2_ref, t3_ref, ra_ref, rb_ref, perm_ref):
        oav = oa_ref[...]
        ra = lax.rsqrt(jnp.mean(oav * oav, axis=-1, keepdims=True) + EPS)
        ra_ref[...] = ra
        mix_ref[:, 0:qa] = ((oav * ra) * ga_ref[...]).astype(BF16)
        l1, l2, l3 = [_from_classes(l_ref, all_lanes, perm_ref, dil)
                      for l_ref, dil in zip((l1_ref, l2_ref, l3_ref), DILATIONS)]
        mx = jnp.maximum(jnp.maximum(l1, l2), l3)
        e1, e2, e3 = jnp.exp(l1 - mx), jnp.exp(l2 - mx), jnp.exp(l3 - mx)
        tot = e1 + e2 + e3
        lse = mx + jnp.log(tot)
        for t_ref, dil in zip((t1_ref, t2_ref, t3_ref), DILATIONS):
            _to_classes(t_ref, all_lanes, lse, perm_ref, dil)
        ws = (e1 / tot, e2 / tot, e3 / tot)
        low = _low_lanes(tm)
        ssq = jnp.zeros((tm, 1), F32)
        for i in range(qb // PAIR):
            sl = slice(i * PAIR, (i + 1) * PAIR)
            acc = jnp.zeros((tm, PAIR), F32)
            for w, o_ref, dil in zip(ws, (o1_ref, o2_ref, o3_ref), DILATIONS):
                wexp = jnp.where(low, w[:, 2 * i:2 * i + 1], w[:, 2 * i + 1:2 * i + 2])
                acc = acc + wexp * _from_classes(o_ref, sl, perm_ref, dil)
            ob_ref[:, sl] = acc
            ssq = ssq + jnp.sum(acc * acc, axis=-1, keepdims=True)
        rb = lax.rsqrt(ssq / qb + EPS)
        rb_ref[...] = rb
        mix_ref[:, qa:qa + qb] = ((ob_ref[...] * rb) * gb_ref[...]).astype(BF16)

    def row(w):
        return pl.BlockSpec((tm, w), lambda i: (i, 0))

    def vec(w):
        return pl.BlockSpec((1, w), lambda i: (0, 0))

    return _call(
        body, None, name="mix_fwd", grid=(s // tm,),
        in_specs=([row(qa)] + [_class_spec(d, tm, qb) for d in DILATIONS]
                  + [_class_spec(d, tm, 128) for d in DILATIONS] + [vec(qa), vec(qb)]),
        out_specs=([row(qa + qb), row(qb)] + [_class_spec(d, tm, 128) for d in DILATIONS] + [row(1), row(1)]),
        out_shape=([jax.ShapeDtypeStruct((s, qa + qb), BF16), jax.ShapeDtypeStruct((s, qb), F32)]
                   + [_class_shape(d, s, 128, F32) for d in DILATIONS]
                   + [jax.ShapeDtypeStruct((s, 1), F32), jax.ShapeDtypeStruct((s, 1), F32)]),
        scratch_shapes=[pltpu.VMEM((tm, 128), F32)],
        compiler_params=_params(("parallel",), tm * (qa + 4 * qb) * 4 + tm * (qa + qb) * 2 + tm * 4096),
    )(oa, *obs, *lses, ga, gb)[0]


def _head_rowsums(prod, rows):
    low = _low_lanes(rows)
    lane = lax.broadcasted_iota(jnp.int32, (rows, 128), 1)
    out = jnp.zeros((rows, 128), F32)
    for i in range(prod.shape[1] // PAIR):
        tile = prod[:, i * PAIR:(i + 1) * PAIR]
        lo = jnp.sum(jnp.where(low, tile, 0.0), axis=-1, keepdims=True)
        hi = jnp.sum(jnp.where(low, 0.0, tile), axis=-1, keepdims=True)
        out = jnp.where(lane == 2 * i, lo, out)
        out = jnp.where(lane == 2 * i + 1, hi, out)
    return out


def _mix_bwd(dmix, oa, ob, ra, rb, ga, gb, hook=None):
    s, qa = oa.shape
    qb = ob.shape[1]
    tm = _row_tile(s)

    def one(dy, o, r, g):
        xn = o * r
        dxn = dy * g
        do = r * (dxn - xn * jnp.mean(dxn * xn, axis=-1, keepdims=True))
        return do, jnp.sum(dy * xn, axis=0, keepdims=True), _head_rowsums(do * o, tm)

    def body(dmix_ref, oa_ref, ob_ref, ra_ref, rb_ref, ga_ref, gb_ref,
             doa_ref, dob1_ref, dob2_ref, dob3_ref, dla_ref, dlb1_ref, dlb2_ref, dlb3_ref,
             dga_ref, dgb_ref, perm_ref):
        doa, dga, dla = one(dmix_ref[:, 0:qa], oa_ref[...], ra_ref[...], ga_ref[...])
        dob, dgb, dlb = one(dmix_ref[:, qa:qa + qb], ob_ref[...], rb_ref[...], gb_ref[...])
        doa_ref[...] = doa.astype(BF16)
        dla_ref[...] = dla
        for dob_ref, dlb_ref, dil in zip((dob1_ref, dob2_ref, dob3_ref), (dlb1_ref, dlb2_ref, dlb3_ref),
                                         DILATIONS):
            _to_classes(dlb_ref, slice(0, 128), dlb, perm_ref, dil)
            for i in range(qb // PAIR):
                sl = slice(i * PAIR, (i + 1) * PAIR)
                _to_classes(dob_ref, sl, dob[:, sl], perm_ref, dil)

        @pl.when(pl.program_id(0) == 0)
        def _():
            dga_ref[...] = dga
            dgb_ref[...] = dgb

        @pl.when(pl.program_id(0) > 0)
        def _():
            dga_ref[...] += dga
            dgb_ref[...] += dgb

    def row(w):
        return pl.BlockSpec((tm, w), lambda i: (i, 0))

    def vec(w):
        return pl.BlockSpec((1, w), lambda i: (0, 0))

    res, hook_res = _call(
        body, hook, name="mix_bwd", grid=(s // tm,),
        in_specs=[row(qa + qb), row(qa), row(qb), row(1), row(1), vec(qa), vec(qb)],
        out_specs=([row(qa)] + [_class_spec(d, tm, qb) for d in DILATIONS] + [row(128)]
                   + [_class_spec(d, tm, 128) for d in DILATIONS] + [vec(qa), vec(qb)]),
        out_shape=([jax.ShapeDtypeStruct((s, qa), BF16)] + [_class_shape(d, s, qb, BF16) for d in DILATIONS]
                   + [jax.ShapeDtypeStruct((s, 128), F32)] + [_class_shape(d, s, 128, F32) for d in DILATIONS]
                   + [jax.ShapeDtypeStruct((1, qa), F32), jax.ShapeDtypeStruct((1, qb), F32)]),
        scratch_shapes=[pltpu.VMEM((tm, 128), F32)],
        compiler_params=_params(("arbitrary",), tm * (qa + qb) * 16),
    )(dmix, oa, ob, ra, rb, ga, gb)
    return res if hook is None else (res, hook_res)


def _assemble_dproj(dqa, dkva, dqs, dks, dvs):
    s, qa = dqa.shape
    kva = dkva.shape[1]
    qb = dqs[0].shape[2]
    width = qa + kva + 3 * qb
    tm = _row_tile(s)

    def body(dqa_ref, dkva_ref, q1, q2, q3, k1, k2, k3, v1, v2, v3, dp_ref, db_ref, perm_ref):
        first = pl.program_id(0) == 0

        def emit(off, val):
            dp_ref[:, off:off + PAIR] = val.astype(BF16)
            col = jnp.sum(val, axis=0, keepdims=True)

            @pl.when(first)
            def _():
                db_ref[:, off:off + PAIR] = col

            @pl.when(jnp.logical_not(first))
            def _():
                db_ref[:, off:off + PAIR] += col

        for i in range(qa // PAIR):
            emit(i * PAIR, dqa_ref[:, i * PAIR:(i + 1) * PAIR])
        for i in range(kva // PAIR):
            emit(qa + i * PAIR, dkva_ref[:, i * PAIR:(i + 1) * PAIR])
        for j, branch_refs in enumerate(((q1, q2, q3), (k1, k2, k3), (v1, v2, v3))):
            for i in range(qb // PAIR):
                sl = slice(i * PAIR, (i + 1) * PAIR)
                total = None
                for ref, dil in zip(branch_refs, DILATIONS):
                    val = _from_classes(ref, sl, perm_ref, dil)
                    total = val if total is None else total + val
                emit(qa + kva + j * qb + i * PAIR, total)

    def row(w):
        return pl.BlockSpec((tm, w), lambda i: (i, 0))

    return _call(
        body, None, name="assemble_dproj", grid=(s // tm,),
        in_specs=[row(qa), row(kva)] + [_class_spec(d, tm, qb) for d in DILATIONS] * 3,
        out_specs=[row(width), pl.BlockSpec((1, width), lambda i: (0, 0))],
        out_shape=[jax.ShapeDtypeStruct((s, width), BF16), jax.ShapeDtypeStruct((1, width), F32)],
        scratch_shapes=[pltpu.VMEM((tm, 128), F32)],
        compiler_params=_params(("arbitrary",), tm * (qa + kva + 9 * qb) * 4 + tm * width * 2),
    )(dqa, dkva, *dqs, *dks, *dvs)[0]


def _fill_bias(bias_ref, n_pairs, max_steps, dil, slopes, sink_ref=None):
    qi = lax.broadcasted_iota(jnp.int32, (BLOCK, 2 * BLOCK), 0)
    kj = lax.broadcasted_iota(jnp.int32, (BLOCK, 2 * BLOCK), 1)
    steps = qi + BLOCK - kj
    dist = (steps * dil).astype(F32)
    band = (steps >= 0) & (steps <= max_steps)
    assert sink_ref is None or max_steps < BLOCK
    for first in (0, 1):
        valid = band & (kj >= BLOCK) if first else band
        for i in range(n_pairs):
            tables = []
            for half in (0, 1):
                table = jnp.where(valid, -(slopes[2 * i + half] * dist), NEG_INF)
                if sink_ref is not None:
                    table = jnp.where(kj == 0, sink_ref[2 * i + half], table)
                tables.append(table)
            bias_ref[first, i] = jnp.concatenate(tables, axis=0)


def _without_sink_row(tile):
    row = lax.broadcasted_iota(jnp.int32, tile.shape, 0)
    return jnp.where(row == 0, jnp.zeros_like(tile), tile)


def _bias_shape(n_pairs):
    return pltpu.VMEM((2, n_pairs, 2 * BLOCK, 2 * BLOCK), F32)


def _stack_heads(tile, low):
    zero = jnp.zeros_like(tile)
    return jnp.concatenate([jnp.where(low, tile, zero), jnp.where(low, zero, tile)], axis=0)


def _unstack_heads(stacked, low):
    return jnp.where(low, stacked[0:BLOCK], stacked[BLOCK:2 * BLOCK])


def _head_columns(ref, i):
    return jnp.concatenate([ref[:, 2 * i:2 * i + 1], ref[:, 2 * i + 1:2 * i + 2]], axis=0)


def _swap_halves(t):
    return pltpu.roll(t, HEAD_DIM, 1)


def _dup_group(t_bf16, group):
    t = t_bf16.astype(F32)
    low = lax.broadcasted_iota(jnp.int32, t.shape, 1) < HEAD_DIM
    keep = low if group == 0 else jnp.logical_not(low)
    return jnp.where(keep, t, _swap_halves(t)).astype(BF16)


def _attn_fwd(name, q, kv, *, dil, max_steps, slopes, sinks=None, hook=None):
    grouped = sinks is not None
    _, length, w = q.shape
    n_pairs = w // PAIR
    nb = length // BLOCK
    heads_per_group = 2 * n_pairs // N_KV_GROUPS

    def body(*refs):
        if grouped:
            sink_ref, q_ref, kvp_ref, kvc_ref, o_ref, lse_ref, bias_ref = refs
        else:
            q_ref, kp_ref, kc_ref, vp_ref, vc_ref, o_ref, lse_ref, bias_ref = refs
        n = pl.program_id(1)

        @pl.when((pl.program_id(0) == 0) & (n == 0))
        def _():
            _fill_bias(bias_ref, n_pairs, max_steps, dil, slopes, sink_ref if grouped else None)

        first = (n == 0).astype(jnp.int32)
        low = _low_lanes(BLOCK)
        lane = lax.broadcasted_iota(jnp.int32, (BLOCK, 128), 1)
        lse_acc = jnp.zeros((BLOCK, 128), F32)
        if grouped:
            kv_all = jnp.concatenate([kvp_ref[...], kvc_ref[...]], axis=0)
            k_dup = [_without_sink_row(_dup_group(kv_all[:, 0:PAIR], g)) for g in range(N_KV_GROUPS)]
            v_dup = [_without_sink_row(_dup_group(kv_all[:, PAIR:2 * PAIR], g)) for g in range(N_KV_GROUPS)]
        for i in range(n_pairs):
            sl = slice(i * PAIR, (i + 1) * PAIR)
            qs = _stack_heads(q_ref[:, sl] * ATT_SCALE, low)
            if grouped:
                kk, vv = k_dup[2 * i // heads_per_group], v_dup[2 * i // heads_per_group]
            else:
                kk = jnp.concatenate([kp_ref[:, sl], kc_ref[:, sl]], axis=0)
                vv = jnp.concatenate([vp_ref[:, sl], vc_ref[:, sl]], axis=0)
            sc = lax.dot_general(qs, kk, (((1,), (1,)), ((), ())), preferred_element_type=F32)
            sc = sc + bias_ref[first, i]
            m = jnp.max(sc, axis=-1, keepdims=True)
            p = jnp.exp(sc - m)
            den = jnp.sum(p, axis=-1, keepdims=True)
            o = jnp.dot(p.astype(BF16), vv, preferred_element_type=F32) / den
            o_ref[:, sl] = _unstack_heads(o, low).astype(o_ref.dtype)
            lse = m + jnp.log(den)
            lse_acc = jnp.where(lane == 2 * i, lse[0:BLOCK], lse_acc)
            lse_acc = jnp.where(lane == 2 * i + 1, lse[BLOCK:2 * BLOCK], lse_acc)
        lse_ref[...] = lse_acc

    def cur(width):
        return pl.BlockSpec((None, BLOCK, width), lambda r, n: (r, n, 0))

    def prev(width):
        return pl.BlockSpec((None, BLOCK, width), lambda r, n: (r, jnp.maximum(n - 1, 0), 0))

    if grouped:
        kvw = kv.shape[2]
        operands = [sinks, q, kv, kv]
        in_specs = [SMEM_SPEC, cur(w), prev(kvw), cur(kvw)]
    else:
        operands = [q, kv[0], kv[0], kv[1], kv[1]]
        in_specs = [cur(w), prev(w), cur(w), prev(w), cur(w)]
    res, hook_res = _call(
        body, hook, name=name, grid=(dil, nb), in_specs=in_specs,
        out_specs=[cur(w), cur(128)],
        out_shape=[jax.ShapeDtypeStruct((dil, length, w), F32 if grouped else BRANCH_DTYPE),
                   jax.ShapeDtypeStruct((dil, length, 128), F32)],
        scratch_shapes=[_bias_shape(n_pairs)],
        compiler_params=_params(("arbitrary", "arbitrary"), BLOCK * w * 16 + n_pairs * BLOCK * BLOCK * 16),
    )(*operands)
    return res if hook is None else (res, hook_res)


def _attn_bwd(name, q, kv, do, lse, delta, *, dil, max_steps, slopes, sinks=None, hook=None):
    grouped = sinks is not None
    _, length, w = q.shape
    n_pairs = w // PAIR
    nb = length // BLOCK
    heads_per_group = 2 * n_pairs // N_KV_GROUPS
    pairs_per_group = n_pairs // N_KV_GROUPS

    def body(*refs):
        if grouped:
            (sink_ref, q_ref, kvp_ref, kvc_ref, do_ref, lse_ref, dl_ref,
             dq_ref, dkv_ref, dsink_ref, acc_ref, bias_ref) = refs
        else:
            (q_ref, kp_ref, kc_ref, vp_ref, vc_ref, do_ref, lse_ref, dl_ref,
             dq_ref, dk_ref, dv_ref, acck_ref, accv_ref, bias_ref) = refs
        n = pl.program_id(1)

        @pl.when((pl.program_id(0) == 0) & (n == 0))
        def _():
            _fill_bias(bias_ref, n_pairs, max_steps, dil, slopes, sink_ref if grouped else None)

        @pl.when(n == 0)
        def _():
            if grouped:
                acc_ref[...] = jnp.zeros_like(acc_ref)

                @pl.when(pl.program_id(0) == 0)
                def _():
                    dsink_ref[...] = jnp.zeros_like(dsink_ref)
            else:
                acck_ref[...] = jnp.zeros_like(acck_ref)
                accv_ref[...] = jnp.zeros_like(accv_ref)

        @pl.when(n == nb)
        def _():
            if grouped:
                dkv_ref[...] = acc_ref[...]
            else:
                dk_ref[...] = acck_ref[...].astype(dk_ref.dtype)
                dv_ref[...] = accv_ref[...].astype(dv_ref.dtype)

        @pl.when(n < nb)
        def _():
            first = (n == 0).astype(jnp.int32)
            low = _low_lanes(BLOCK)
            low_kv = _low_lanes(2 * BLOCK)
            lane1 = lax.broadcasted_iota(jnp.int32, (1, 128), 1)
            if grouped:
                kv_all = jnp.concatenate([kvp_ref[...], kvc_ref[...]], axis=0)
                k_dup = [_without_sink_row(_dup_group(kv_all[:, 0:PAIR], g)) for g in range(N_KV_GROUPS)]
                v_dup = [_without_sink_row(_dup_group(kv_all[:, PAIR:2 * PAIR], g)) for g in range(N_KV_GROUPS)]
                dk_grp =[jnp.zeros((2 * BLOCK, PAIR), F32) for _ in range(N_KV_GROUPS)]
                dv_grp = [jnp.zeros((2 * BLOCK, PAIR), F32) for _ in range(N_KV_GROUPS)]
                dsink = jnp.zeros((1, 128), F32)
            for i in range(n_pairs):
                sl = slice(i * PAIR, (i + 1) * PAIR)
                qs = _stack_heads(q_ref[:, sl] * ATT_SCALE, low)
                dos = _stack_heads(do_ref[:, sl], low)
                if grouped:
                    grp = 2 * i // heads_per_group
                    kk, vv = k_dup[grp], v_dup[grp]
                else:
                    kk = jnp.concatenate([kp_ref[:, sl], kc_ref[:, sl]], axis=0)
                    vv = jnp.concatenate([vp_ref[:, sl], vc_ref[:, sl]], axis=0)
                lse_col = _head_columns(lse_ref, i)
                dl_col = _head_columns(dl_ref, i)
                sc = lax.dot_general(qs, kk, (((1,), (1,)), ((), ())), preferred_element_type=F32)
                p = jnp.exp(sc + bias_ref[first, i] - lse_col)
                dp = lax.dot_general(dos, vv, (((1,), (1,)), ((), ())), preferred_element_type=F32)
                ds_f32 = p * (dp - dl_col)
                ds = ds_f32.astype(BF16)
                dq = jnp.dot(ds, kk, preferred_element_type=F32)
                dkk = lax.dot_general(ds, qs, (((0,), (0,)), ((), ())), preferred_element_type=F32)
                dvv = lax.dot_general(p.astype(BF16), dos, (((0,), (0,)), ((), ())),
                                      preferred_element_type=F32)
                if grouped:
                    for half in (0, 1):
                        contrib = jnp.sum(ds_f32[half * BLOCK:(half + 1) * BLOCK, 0:1], axis=0, keepdims=True)
                        dsink = jnp.where(lane1 == 2 * i + half, dsink + contrib, dsink)
                dq_ref[:, sl] = (_unstack_heads(dq, low) * ATT_SCALE).astype(dq_ref.dtype)
                if grouped:
                    dk_grp[grp] = dk_grp[grp] + dkk
                    dv_grp[grp] = dv_grp[grp] + dvv
                else:
                    dk_ref[:, sl] = (acck_ref[:, sl] + dkk[0:BLOCK]).astype(dk_ref.dtype)
                    acck_ref[:, sl] = dkk[BLOCK:2 * BLOCK]
                    dv_ref[:, sl] = (accv_ref[:, sl] + dvv[0:BLOCK]).astype(dv_ref.dtype)
                    accv_ref[:, sl] = dvv[BLOCK:2 * BLOCK]
            if grouped:
                folded = [_without_sink_row(t + _swap_halves(t)) for t in dk_grp + dv_grp]
                dk_tile = jnp.where(low_kv, folded[0], folded[1])
                dv_tile = jnp.where(low_kv, folded[2], folded[3])
                part = jnp.concatenate([dk_tile, dv_tile], axis=1)
                dkv_ref[...] = acc_ref[...] + part[0:BLOCK]
                acc_ref[...] = part[BLOCK:2 * BLOCK]
                dsink_ref[...] += dsink

    last = nb - 1

    def cur(width):
        return pl.BlockSpec((None, BLOCK, width), lambda r, n: (r, jnp.minimum(n, last), 0))

    def prev(width):
        return pl.BlockSpec((None, BLOCK, width),
                            lambda r, n: (r, jnp.maximum(jnp.minimum(n, last) - 1, 0), 0))

    def done(width):
        return pl.BlockSpec((None, BLOCK, width), lambda r, n: (r, jnp.maximum(n - 1, 0), 0))

    if grouped:
        assert pairs_per_group * N_KV_GROUPS == n_pairs and heads_per_group % 2 == 0
        kvw = kv.shape[2]
        operands = [sinks, q, kv, kv, do, lse, delta]
        in_specs = [SMEM_SPEC, cur(w), prev(kvw), cur(kvw), cur(w), cur(128), cur(128)]
        out_specs = [cur(w), done(kvw), pl.BlockSpec((1, 128), lambda r, n: (0, 0))]
        out_shape = [jax.ShapeDtypeStruct((dil, length, w), F32), jax.ShapeDtypeStruct((dil, length, kvw), F32),
                     jax.ShapeDtypeStruct((1, 128), F32)]
        scratch = [pltpu.VMEM((BLOCK, kvw), F32), _bias_shape(n_pairs)]
    else:
        operands = [q, kv[0], kv[0], kv[1], kv[1], do, lse, delta]
        in_specs = [cur(w), prev(w), cur(w), prev(w), cur(w), cur(w), cur(128), cur(128)]
        out_specs = [cur(w), done(w), done(w)]
        out_shape = [jax.ShapeDtypeStruct((dil, length, w), BRANCH_DTYPE)] * 3
        scratch = [pltpu.VMEM((BLOCK, w), F32), pltpu.VMEM((BLOCK, w), F32), _bias_shape(n_pairs)]
    res, hook_res = _call(
        body, hook, name=name, grid=(dil, nb + 1), in_specs=in_specs, out_specs=out_specs,
        out_shape=out_shape, scratch_shapes=scratch,
        compiler_params=_params(("arbitrary", "arbitrary"), BLOCK * w * 32 + n_pairs * BLOCK * BLOCK * 16),
    )(*operands)
    return res if hook is None else (res, hook_res)


def _adamw(name, w, g, m, v):
    rows, cols = w.shape
    tm = 256 if rows % 256 == 0 else rows

    def body(w_ref, g_ref, m_ref, v_ref, d_ref, nm_ref, nv_ref, g_out_ref):
        gv = g_ref[...]
        mn = ADAM_B1 * m_ref[...] + (1.0 - ADAM_B1) * gv
        vn = ADAM_B2 * v_ref[...] + (1.0 - ADAM_B2) * (gv * gv)
        m_hat = mn / (1.0 - ADAM_B1 ** ADAM_STEP)
        v_hat = vn / (1.0 - ADAM_B2 ** ADAM_STEP)
        d_ref[...] = -ADAM_LR * (m_hat / (jnp.sqrt(v_hat) + ADAM_EPS) + ADAM_WD * w_ref[...])
        nm_ref[...] = mn
        nv_ref[...] = vn
        g_out_ref[...] = gv

    spec = pl.BlockSpec((tm, cols), lambda i: (i, 0))
    return _call(
        body, None, name=name, grid=(rows // tm,), in_specs=[spec] * 4, out_specs=[spec] * 4,
        out_shape=[jax.ShapeDtypeStruct(w.shape, F32)] * 4,
        compiler_params=_params(("parallel",), tm * cols * 32),
    )(w, g, m, v)[0]


def _mesh_position():
    return lax.axis_index("x"), lax.axis_index("y"), lax.axis_index("c")


def _other_chips(x, y):
    return [(1 - x, y), (x, 1 - y), (1 - x, 1 - y)]


def _gather_hook(gathered, lo, hi):
    rows, cols = gathered.shape[0] // N_CHIPS, gathered.shape[1]
    half, n = rows // 2, hi - lo
    assert lo % 16 == 0 and n % 16 == 0 and half % 16 == 0

    def region(out, owner_chip, which_half):
        return out.at[pl.ds(pl.multiple_of(owner_chip * rows + which_half * half + lo, 16), n)]

    def parts(outs, sems):
        x, y, c = _mesh_position()
        return outs[0], sems, c, 2 * x + y, (x, y, 1 - c), _other_chips(x, y)

    def start(ops, outs, sems):
        out, (send, recv, fsend, frecv), c, chip, sibling, others = parts(outs, sems)
        mine = region(out, chip, c)
        for k, (px, py) in enumerate(others):
            _remote(mine, mine, send.at[k], recv.at[k], (px, py, c)).start()

    def mid(ops, outs, sems):
        out, (send, recv, fsend, frecv), c, chip, sibling, others = parts(outs, sems)
        for k, (px, py) in enumerate(others):
            landed = region(out, 2 * px + py, c)
            _remote(landed, landed, send.at[k], recv.at[k], (px, py, c)).wait_recv()
            _remote(landed, landed, fsend.at[k], frecv.at[k], sibling).start()

    def finish(ops, outs, sems):
        out, (send, recv, fsend, frecv), c, chip, sibling, others = parts(outs, sems)
        mine = region(out, chip, c)
        for k, (px, py) in enumerate(others):
            passed = region(out, 2 * px + py, 1 - c)
            _remote(passed, passed, fsend.at[k], frecv.at[k], sibling).wait_recv()
        for k, (px, py) in enumerate(others):
            landed = region(out, 2 * px + py, c)
            _remote(landed, landed, fsend.at[k], frecv.at[k], sibling).wait_send()
            _remote(mine, mine, send.at[k], recv.at[k], (px, py, c)).wait_send()

    return _Hook([gathered], [jax.ShapeDtypeStruct(gathered.shape, gathered.dtype)],
                 [pltpu.SemaphoreType.DMA((3,))] * 4, start, finish, mid, aliases={0: 0})


def _own_shard_in_place(name, shard, chip):
    rows, cols = shard.shape
    tr = next(t for t in (544, 512, 320, 256, 128, 64, 32, 16) if rows % t == 0)

    def body(chip_ref, w_ref, o_ref):
        o_ref[...] = w_ref[...].astype(BF16)

    return pl.pallas_call(
        body, name=name,
        grid_spec=pltpu.PrefetchScalarGridSpec(
            num_scalar_prefetch=1, grid=(rows // tr,),
            in_specs=[pl.BlockSpec((tr, cols), lambda i, chip_ref: (i, 0))],
            out_specs=pl.BlockSpec((tr, cols), lambda i, chip_ref: (chip_ref[0] * (rows // tr) + i, 0))),
        out_shape=jax.ShapeDtypeStruct((N_CHIPS * rows, cols), BF16),
        compiler_params=_params(("parallel",), tr * cols * 6),
    )(chip, shard)


def _exchange_hook(grad):
    rows, cols = grad.shape[0] // N_CHIPS, grad.shape[1]
    half = rows // 2
    assert half % 16 == 0

    def copies(ops, outs, sems):
        x, y, c = _mesh_position()
        send, recv = sems
        return [_remote(ops[0].at[pl.ds(pl.multiple_of(k * rows + (1 - c) * half, 16), half)], outs[0].at[k],
                        send.at[k], recv.at[k], (x, y, 1 - c)) for k in range(N_CHIPS)]

    def start(ops, outs, sems):
        for cp in copies(ops, outs, sems):
            cp.start()

    def finish(ops, outs, sems):
        for cp in copies(ops, outs, sems):
            cp.wait_recv()
            cp.wait_send()

    return _Hook([grad], [jax.ShapeDtypeStruct((N_CHIPS, half, cols), grad.dtype)],
                 [pltpu.SemaphoreType.DMA((N_CHIPS,))] * 2, start, finish)


def _scatter_hook(chip_sum):
    _, half, cols = chip_sum.shape

    def copies(ops, outs, sems):
        x, y, c = _mesh_position()
        send, recv = sems
        return [_remote(ops[0].at[2 * px + py], outs[0].at[k], send.at[k], recv.at[k], (px, py, c))
                for k, (px, py) in enumerate(_other_chips(x, y))]

    def start(ops, outs, sems):
        for cp in copies(ops, outs, sems):
            cp.start()

    def finish(ops, outs, sems):
        for cp in copies(ops, outs, sems):
            cp.wait_recv()
            cp.wait_send()

    return _Hook([chip_sum], [jax.ShapeDtypeStruct((3, half, cols), chip_sum.dtype)],
                 [pltpu.SemaphoreType.DMA((3,))] * 2, start, finish)


def _sum_tile(half):
    return 256 if half % 256 == 0 else half


def _chip_add(name, grad, from_sibling, core):
    n_chips, half, cols = from_sibling.shape
    rows = 2 * half
    tr = _sum_tile(half)

    def body(core_ref, g_ref, s_ref, o_ref):
        o_ref[...] = (g_ref[...].astype(F32) + s_ref[...].astype(F32)).astype(o_ref.dtype)

    tile = pl.BlockSpec((None, tr, cols), lambda k, i, core_ref: (k, i, 0))
    return pl.pallas_call(
        body, name=name,
        grid_spec=pltpu.PrefetchScalarGridSpec(
            num_scalar_prefetch=1, grid=(n_chips, half // tr),
            in_specs=[pl.BlockSpec((tr, cols), lambda k, i, core_ref:
                                   (k * (rows // tr) + core_ref[0] * (half // tr) + i, 0)), tile],
            out_specs=tile),
        out_shape=jax.ShapeDtypeStruct(from_sibling.shape, from_sibling.dtype),
        compiler_params=_params(("parallel", "parallel"), 3 * tr * cols * 4),
    )(core, grad, from_sibling)


def _final_add(name, chip_sum, from_chips, chip, core):
    _, half, cols = chip_sum.shape
    tr = _sum_tile(half)

    def body(chip_ref, core_ref, own_ref, others_ref, o_ref):
        total = own_ref[...].astype(F32)
        for k in range(3):
            total = total + others_ref[k].astype(F32)
        o_ref[...] = total

    return pl.pallas_call(
        body, name=name,
        grid_spec=pltpu.PrefetchScalarGridSpec(
            num_scalar_prefetch=2, grid=(half // tr,),
            in_specs=[pl.BlockSpec((None, tr, cols), lambda i, chip_ref, core_ref: (chip_ref[0], i, 0)),
                      pl.BlockSpec((3, tr, cols), lambda i, chip_ref, core_ref: (0, i, 0))],
            out_specs=pl.BlockSpec((tr, cols),
                                   lambda i, chip_ref, core_ref: (core_ref[0] * (half // tr) + i, 0))),
        out_shape=jax.ShapeDtypeStruct((2 * half, cols), F32),
        compiler_params=_params(("parallel",), 6 * tr * cols * 4),
    )(chip, core, chip_sum, from_chips)


def _share_halves(shards, small):
    n_s = len(shards)
    rows_s = small.shape[0]

    def body(*refs):
        small_ref = refs[n_s]
        outs, small_out = refs[n_s + 1:2 * n_s + 1], refs[2 * n_s + 1]
        small_all, send, recv, small_send, small_recv = refs[2 * n_s + 2:]
        x, y, c = _mesh_position()
        me = 4 * x + 2 * y + c
        sibling = (x, y, 1 - c)
        pending = []
        for i in range(n_s):
            half = shards[i].shape[0] // 2
            mine = outs[i].at[pl.ds(pl.multiple_of(c * half, 16), half)]
            cp = _remote(mine, mine, send.at[i], recv.at[i], sibling)
            cp.start()
            pending.append(cp)
        small_all[me] = small_ref[...]
        for j in range(N_DEV - 1):
            peer = (me + 1 + j) % N_DEV
            cp = _remote(small_all.at[me], small_all.at[me], small_send.at[j], small_recv.at[j],
                         (peer // 4, (peer // 2) % 2, peer % 2))
            cp.start()
            pending.append(cp)
        for i in range(n_s):
            half = shards[i].shape[0] // 2
            theirs = outs[i].at[pl.ds(pl.multiple_of((1 - c) * half, 16), half)]
            _remote(theirs, theirs, send.at[i], recv.at[i], sibling).wait_recv()
        for j in range(N_DEV - 1):
            peer = (me + N_DEV - 1 - j) % N_DEV
            _remote(small_all.at[peer], small_all.at[peer], small_send.at[j], small_recv.at[j],
                    sibling).wait_recv()
        total = small_all[0]
        for dev in range(1, N_DEV):
            total = total + small_all[dev]
        small_out[...] = total
        for cp in pending:
            cp.wait_send()

    res = pl.pallas_call(
        body, name="share_halves",
        in_specs=[HBM_SPEC] * n_s + [VMEM_SPEC], out_specs=[HBM_SPEC] * n_s + [VMEM_SPEC],
        out_shape=[jax.ShapeDtypeStruct(sh.shape, sh.dtype) for sh in shards]
        + [jax.ShapeDtypeStruct((rows_s, 128), F32)],
        scratch_shapes=[pltpu.VMEM((N_DEV, rows_s, 128), F32),
                        pltpu.SemaphoreType.DMA((n_s,)), pltpu.SemaphoreType.DMA((n_s,)),
                        pltpu.SemaphoreType.DMA((N_DEV - 1,)), pltpu.SemaphoreType.DMA((N_DEV - 1,))],
        input_output_aliases={i: i for i in range(n_s)},
    )(*shards, small)
    return res[:n_s], res[n_s]


FINISH_CHUNK_ROWS = 256


def _finish_reduction(chip_sums, shares, small):
    n_w, n_s = len(chip_sums), len(shares)
    halves = [cs.shape[1] for cs in chip_sums]
    cols = shares[0].shape[1] if n_s else chip_sums[0].shape[2]
    wire = chip_sums[0].dtype if n_w else GRAD_WIRE_DTYPE
    rows_s = small.shape[0]
    ch = FINISH_CHUNK_ROWS

    def body(*refs):
        sums_in, small_ref, share_in = refs[:n_w], refs[n_w], refs[n_w + 1:n_w + 1 + n_s]
        o0 = n_w + 1 + n_s
        outs, small_out, share_out = refs[o0:o0 + n_w], refs[o0 + n_w], refs[o0 + n_w + 1:o0 + n_w + 1 + n_s]
        arrived = refs[o0 + n_w + 1 + n_s:o0 + 2 * n_w + 1 + n_s]
        (small_all, buf_in, buf_out, chip_send, chip_recv, fin_send, fin_recv, small_send, small_recv,
         share_send, share_recv, share_local, io_sem) = refs[o0 + 2 * n_w + 1 + n_s:]
        x, y, c = _mesh_position()
        chip = 2 * x + y
        me = 4 * x + 2 * y + c
        sibling = (x, y, 1 - c)
        others = _other_chips(x, y)
        pending, local = [], []

        for w in range(n_w):
            for k, (px, py) in enumerate(others):
                cp = _remote(sums_in[w].at[2 * px + py], arrived[w].at[k], chip_send.at[w, k],
                             chip_recv.at[w, k], (px, py, c))
                cp.start()
                pending.append(cp)
        small_all[me] = small_ref[...]
        for j in range(N_DEV - 1):
            peer = (me + 1 + j) % N_DEV
            cp = _remote(small_all.at[me], small_all.at[me], small_send.at[j], small_recv.at[j],
                         (peer // 4, (peer // 2) % 2, peer % 2))
            cp.start()
            pending.append(cp)

        def halves_of(out, rows):
            return [out.at[pl.ds(pl.multiple_of(which * rows, 16), rows)] for which in (c, 1 - c)]

        for i in range(n_s):
            mine, _ = halves_of(share_out[i], share_in[i].shape[0])
            cp = pltpu.make_async_copy(share_in[i], mine, share_local.at[i])
            cp.start()
            local.append(cp)
            cp = _remote(share_in[i], mine, share_send.at[i], share_recv.at[i], sibling)
            cp.start()
            pending.append(cp)

        def add_chunk(w, dst, start, size):
            total = None
            for src in [sums_in[w].at[chip]] + [arrived[w].at[k] for k in range(3)]:
                cp = pltpu.make_async_copy(src.at[pl.ds(start, size)], buf_in.at[pl.ds(0, size)], io_sem)
                cp.start()
                cp.wait()
                val = buf_in[pl.ds(0, size), :].astype(F32)
                total = val if total is None else total + val
            buf_out[pl.ds(0, size), :] = total
            cp = pltpu.make_async_copy(buf_out.at[pl.ds(0, size)], dst.at[pl.ds(start, size)], io_sem)
            cp.start()
            cp.wait()

        for w in range(n_w):
            for k, (px, py) in enumerate(others):
                _remote(sums_in[w].at[chip], arrived[w].at[k], chip_send.at[w, k], chip_recv.at[w, k],
                        (px, py, c)).wait_recv()
            mine, _ = halves_of(outs[w], halves[w])
            n_full = halves[w] // ch

            def loop_body(i, carry, w=w, mine=mine):
                add_chunk(w, mine, pl.multiple_of(i * ch, ch), ch)
                return carry

            lax.fori_loop(0, n_full, loop_body, 0)
            if halves[w] % ch:
                add_chunk(w, mine, n_full * ch, halves[w] - n_full * ch)
            cp = _remote(mine, mine, fin_send.at[w], fin_recv.at[w], sibling)
            cp.start()
            pending.append(cp)
        for w in range(n_w):
            _, theirs = halves_of(outs[w], halves[w])
            _remote(theirs, theirs, fin_send.at[w], fin_recv.at[w], sibling).wait_recv()
        for i in range(n_s):
            _, theirs = halves_of(share_out[i], share_in[i].shape[0])
            _remote(share_in[i], theirs, share_send.at[i], share_recv.at[i], sibling).wait_recv()

        for j in range(N_DEV - 1):
            peer = (me + N_DEV - 1 - j) % N_DEV
            _remote(small_all.at[peer], small_all.at[peer], small_send.at[j], small_recv.at[j],
                    sibling).wait_recv()
        total = small_all[0]
        for dev in range(1, N_DEV):
            total = total + small_all[dev]
        small_out[...] = total
        for cp in pending:
            cp.wait_send()
        for cp in local:
            cp.wait()

    def sems(n):
        return pltpu.SemaphoreType.DMA((max(n, 1),))

    sds = jax.ShapeDtypeStruct
    out_shape = ([sds((2 * h, cols), F32) for h in halves] + [sds((rows_s, 128), F32)]
                 + [sds((2 * sh.shape[0], cols), F32) for sh in shares]
                 + [sds((3, h, cols), wire) for h in halves])
    res = pl.pallas_call(
        body, name="finish_reduction",
        in_specs=[HBM_SPEC] * n_w + [VMEM_SPEC] + [HBM_SPEC] * n_s,
        out_specs=[HBM_SPEC] * n_w + [VMEM_SPEC] + [HBM_SPEC] * (n_s + n_w),
        out_shape=out_shape,
        scratch_shapes=[
            pltpu.VMEM((N_DEV, rows_s, 128), F32), pltpu.VMEM((ch, cols), wire), pltpu.VMEM((ch, cols), F32),
            pltpu.SemaphoreType.DMA((max(n_w, 1), 3)), pltpu.SemaphoreType.DMA((max(n_w, 1), 3)),
            sems(n_w), sems(n_w), sems(N_DEV - 1), sems(N_DEV - 1), sems(n_s), sems(n_s), sems(n_s),
            pltpu.SemaphoreType.DMA,
        ],
        compiler_params=pltpu.CompilerParams(vmem_limit_bytes=VMEM_LIMIT_MIN),
    )(*chip_sums, small, *shares)
    return res[:n_w], res[n_w + 1:n_w + 1 + n_s], res[n_w]


def _gather_weights(shards):
    n_w = len(shards)
    halves = [s.shape[0] // 2 for s in shards]

    def body(*refs):
        ins, outs = refs[:n_w], refs[n_w:2 * n_w]
        local_sems, send_sems, recv_sems, fsend_sems, frecv_sems = refs[2 * n_w:]
        x, y, c = _mesh_position()
        chip = 2 * x + y
        sibling = (x, y, 1 - c)
        others = _other_chips(x, y)

        def region(w, owner_chip, half):
            start = owner_chip * (2 * halves[w]) + half * halves[w]
            return outs[w].at[pl.ds(pl.multiple_of(start, 16), halves[w])]

        def remote(src, dst, ssem, rsem, dev):
            return pltpu.make_async_remote_copy(src_ref=src, dst_ref=dst, send_sem=ssem, recv_sem=rsem,
                                                device_id=dev, device_id_type=MESH)

        local = []
        for w in range(n_w):
            cp = pltpu.make_async_copy(ins[w], outs[w].at[pl.ds(pl.multiple_of(chip * 2 * halves[w], 16),
                                                                2 * halves[w])], local_sems.at[w])
            cp.start()
            local.append(cp)
        sends = []
        for w in range(n_w):
            mine = ins[w].at[pl.ds(pl.multiple_of(c * halves[w], 16), halves[w])]
            for k, (px, py) in enumerate(others):
                cp = remote(mine, region(w, chip, c), send_sems.at[w, k], recv_sems.at[w, k], (px, py, c))
                cp.start()
                sends.append(cp)
        for k, (px, py) in enumerate(others):
            for w in range(n_w):
                landed = region(w, 2 * px + py, c)
                remote(landed, landed, send_sems.at[w, k], recv_sems.at[w, k], (px, py, c)).wait_recv()
                cp = remote(landed, landed, fsend_sems.at[w, k], frecv_sems.at[w, k], sibling)
                cp.start()
                sends.append(cp)
        for k, (px, py) in enumerate(others):
            for w in range(n_w):
                passed = region(w, 2 * px + py, 1 - c)
                remote(passed, passed, fsend_sems.at[w, k], frecv_sems.at[w, k], sibling).wait_recv()
        for cp in sends:
            cp.wait_send()
        for cp in local:
            cp.wait()

    return pl.pallas_call(
        body, name="gather_weights",
        in_specs=[HBM_SPEC] * n_w, out_specs=[HBM_SPEC] * n_w,
        out_shape=[jax.ShapeDtypeStruct((N_CHIPS * s.shape[0], s.shape[1]), s.dtype) for s in shards],
        scratch_shapes=[pltpu.SemaphoreType.DMA((n_w,))] + [pltpu.SemaphoreType.DMA((n_w, 3))] * 4,
    )(*shards)


REDUCE_CHUNK_ROWS = 256


def _reduce_gradients(grads, shares, small):
    n_w = len(grads)
    rows = [g.shape[0] // N_CHIPS for g in grads]
    halves = [r // 2 for r in rows]
    cols = grads[0].shape[1]
    wire = grads[0].dtype
    ch = REDUCE_CHUNK_ROWS
    for h in halves:
        assert h % 16 == 0
    rows_s = small.shape[0]
    n_s = len(shares)

    def body(*refs):
        g_in = refs[:n_w]
        small_ref = refs[n_w]
        share_in = refs[n_w + 1:n_w + 1 + n_s]
        refs = refs[:n_w + 1] + refs[n_w + 1 + n_s:]
        outs = refs[n_w + 1:2 * n_w + 1]
        small_out = refs[2 * n_w + 1]
        from_sib = refs[2 * n_w + 2:3 * n_w + 2]
        chip_sum = refs[3 * n_w + 2:4 * n_w + 2]
        from_chips = refs[4 * n_w + 2:5 * n_w + 2]
        share_out = refs[5 * n_w + 2:5 * n_w + 2 + n_s]
        (small_all, buf_a, buf_b, buf_o, sib_send, sib_recv, chip_send, chip_recv,
         fin_send, fin_recv, small_send, small_recv, io_sem,
         share_send, share_recv, share_local) = refs[5 * n_w + 2 + n_s:]
        x, y, c = _mesh_position()
        chip = 2 * x + y
        me = 4 * x + 2 * y + c
        sibling = (x, y, 1 - c)
        others = _other_chips(x, y)

        def remote(src, dst, ssem, rsem, dev):
            return pltpu.make_async_remote_copy(src_ref=src, dst_ref=dst, send_sem=ssem, recv_sem=rsem,
                                                device_id=dev, device_id_type=MESH)

        def part(w, owner_chip, half):
            start = owner_chip * rows[w] + half * halves[w]
            return g_in[w].at[pl.ds(pl.multiple_of(start, 16), halves[w])]

        pending = []
        small_all[me] = small_ref[...]
        for j in range(N_DEV - 1):
            peer = (me + 1 + j) % N_DEV
            cp = remote(small_all.at[me], small_all.at[me], small_send.at[j], small_recv.at[j],
                        (peer // 4, (peer // 2) % 2, peer % 2))
            cp.start()
            pending.append(cp)

        local = []
        for i in range(n_s):
            half_rows = share_in[i].shape[0]
            place = share_out[i].at[pl.ds(pl.multiple_of(c * half_rows, 16), half_rows)]
            cp = pltpu.make_async_copy(share_in[i], place, share_local.at[i])
            cp.start()
            local.append(cp)
            cp = remote(share_in[i], place, share_send.at[i], share_recv.at[i], sibling)
            cp.start()
            pending.append(cp)

        for w in range(n_w):
            for k in range(N_CHIPS):
                cp = remote(part(w, k, 1 - c), from_sib[w].at[k], sib_send.at[w, k], sib_recv.at[w, k], sibling)
                cp.start()
                pending.append(cp)

        def add_stream(w, srcs, dst, n_rows):
            def chunk(start, size):
                total = None
                for i, src in enumerate(srcs):
                    buf = buf_a if i % 2 == 0 else buf_b
                    cp = pltpu.make_async_copy(src.at[pl.ds(start, size)], buf.at[pl.ds(0, size)], io_sem)
                    cp.start()
                    cp.wait()
                    val = buf[pl.ds(0, size), :].astype(F32)
                    total = val if total is None else total + val
                return total

            n_full = n_rows // ch
            rem = n_rows - n_full * ch

            def store(total, start, size):
                if dst.dtype == F32:
                    buf_o[pl.ds(0, size), :] = total
                    cp = pltpu.make_async_copy(buf_o.at[pl.ds(0, size)], dst.at[pl.ds(start, size)], io_sem)
                else:
                    buf_a[pl.ds(0, size), :] = total.astype(buf_a.dtype)
                    cp = pltpu.make_async_copy(buf_a.at[pl.ds(0, size)], dst.at[pl.ds(start, size)], io_sem)
                cp.start()
                cp.wait()

            def loop_body(i, carry):
                start = pl.multiple_of(i * ch, ch)
                store(chunk(start, ch), start, ch)
                return carry

            lax.fori_loop(0, n_full, loop_body, 0)
            if rem:
                store(chunk(n_full * ch, rem), n_full * ch, rem)

        order = [2, 0, 1]
        for w in range(n_w):
            for k in range(N_CHIPS):
                remote(part(w, k, 1 - c), from_sib[w].at[k], sib_send.at[w, k], sib_recv.at[w, k],
                       sibling).wait_recv()
        for k in order:
            px, py = others[k]
            owner = 2 * px + py
            for w in range(n_w):
                add_stream(w, [part(w, owner, c), from_sib[w].at[owner]], chip_sum[w].at[owner], halves[w])
                cp = remote(chip_sum[w].at[owner], from_chips[w].at[k], chip_send.at[w, k],
                            chip_recv.at[w, k], (px, py, c))
                cp.start()
                pending.append(cp)
        for w in range(n_w):
            add_stream(w, [part(w, chip, c), from_sib[w].at[chip]], chip_sum[w].at[chip], halves[w])

        for w in range(n_w):
            for k in range(3):
                px, py = others[k]
                remote(chip_sum[w].at[chip], from_chips[w].at[k], chip_send.at[w, k], chip_recv.at[w, k],
                       (px, py, c)).wait_recv()
            mine = outs[w].at[pl.ds(pl.multiple_of(c * halves[w], 16), halves[w])]
            add_stream(w, [chip_sum[w].at[chip], from_chips[w].at[0], from_chips[w].at[1],
                           from_chips[w].at[2]], mine, halves[w])
            cp = remote(mine, mine, fin_send.at[w], fin_recv.at[w], sibling)
            cp.start()
            pending.append(cp)
        for w in range(n_w):
            theirs = outs[w].at[pl.ds(pl.multiple_of((1 - c) * halves[w], 16), halves[w])]
            remote(theirs, theirs, fin_send.at[w], fin_recv.at[w], sibling).wait_recv()

        for j in range(N_DEV - 1):
            peer = (me + N_DEV - 1 - j) % N_DEV
            remote(small_all.at[peer], small_all.at[peer], small_send.at[j], small_recv.at[j],
                   sibling).wait_recv()
        total = small_all[0]
        for d in range(1, N_DEV):
            total = total + small_all[d]
        small_out[...] = total
        for i in range(n_s):
            half_rows = share_in[i].shape[0]
            theirs = share_out[i].at[pl.ds(pl.multiple_of((1 - c) * half_rows, 16), half_rows)]
            remote(share_in[i], theirs, share_send.at[i], share_recv.at[i], sibling).wait_recv()
        for cp in pending:
            cp.wait_send()
        for cp in local:
            cp.wait()

    hbm_scratch = ([jax.ShapeDtypeStruct((N_CHIPS, h, cols), wire) for h in halves] * 2
                   + [jax.ShapeDtypeStruct((3, h, cols), wire) for h in halves])
    out_shape = ([jax.ShapeDtypeStruct((r, cols), F32) for r in rows]
                 + [jax.ShapeDtypeStruct((rows_s, 128), F32)] + hbm_scratch
                 + [jax.ShapeDtypeStruct((2 * sh.shape[0], sh.shape[1]), F32) for sh in shares])
    res = pl.pallas_call(
        body, name="reduce_gradients",
        in_specs=[HBM_SPEC] * n_w + [VMEM_SPEC] + [HBM_SPEC] * n_s,
        out_specs=[HBM_SPEC] * n_w + [VMEM_SPEC] + [HBM_SPEC] * (3 * n_w + n_s),
        out_shape=out_shape,
        scratch_shapes=[
            pltpu.VMEM((N_DEV, rows_s, 128), F32),
            pltpu.VMEM((ch, cols), wire), pltpu.VMEM((ch, cols), wire), pltpu.VMEM((ch, cols), F32),
            pltpu.SemaphoreType.DMA((n_w, N_CHIPS)), pltpu.SemaphoreType.DMA((n_w, N_CHIPS)),
            pltpu.SemaphoreType.DMA((n_w, 3)), pltpu.SemaphoreType.DMA((n_w, 3)),
            pltpu.SemaphoreType.DMA((n_w,)), pltpu.SemaphoreType.DMA((n_w,)),
            pltpu.SemaphoreType.DMA((N_DEV - 1,)), pltpu.SemaphoreType.DMA((N_DEV - 1,)),
            pltpu.SemaphoreType.DMA,
            pltpu.SemaphoreType.DMA((max(n_s, 1),)), pltpu.SemaphoreType.DMA((max(n_s, 1),)),
            pltpu.SemaphoreType.DMA((max(n_s, 1),)),
        ],
        compiler_params=pltpu.CompilerParams(vmem_limit_bytes=VMEM_LIMIT_MIN),
    )(*grads, small, *shares)
    return res[:n_w], res[n_w], res[len(res) - n_s:] if n_s else []


def _pack_small(parts, rows):
    flat = jnp.concatenate([p.reshape(-1) for p in parts])
    flat = jnp.pad(flat, (0, rows * 128 - flat.shape[0]))
    return flat.reshape(rows, 128)


def _unpack_small(packed, shapes):
    flat = packed.reshape(-1)
    out, off = [], 0
    for shp in shapes:
        n = int(np.prod(shp))
        out.append(flat[off:off + n].reshape(shp))
        off += n
    return out


def kernel(x, g_attn, w_in, b_in, sinks_a, g_out_a, g_out_b, w_out, g_mlp, w_1, w_2, g_final, loss_target, m_g_attn, m_w_in, m_b_in, m_sinks_a, m_g_out_a, m_g_out_b, m_w_out, m_g_mlp, m_w_1, m_w_2, m_g_final, v_g_attn, v_w_in, v_b_in, v_sinks_a, v_g_out_a, v_g_out_b, v_w_out, v_g_mlp, v_w_1, v_w_2, v_g_final):
    s, d = x.shape[1], x.shape[2]
    d_in = b_in.shape[1]
    qa = g_out_a.shape[1]
    qb = g_out_b.shape[1]
    kva = 2 * N_KV_GROUPS * HEAD_DIM
    assert d_in == qa + kva + 3 * qb and qa + qb == w_out.shape[1] * N_CHIPS
    d_ff = w_1.shape[2] * N_CHIPS
    ff_shard = w_1.shape[2]
    in_shard = w_in.shape[2]
    n_heads_a, n_heads_b = qa // HEAD_DIM, qb // HEAD_DIM
    slopes_a, slopes_b = alibi_slopes(n_heads_a), alibi_slopes(n_heads_b)

    x2d = x[0]
    target = loss_target[0]

    core_index = lax.axis_index("c").astype(jnp.int32).reshape(1)
    chip_index = (2 * lax.axis_index("x") + lax.axis_index("y")).astype(jnp.int32).reshape(1)
    shards = {"w_in": w_in[0].T, "w_out": w_out[0], "w_1": w_1[0], "w_2": w_2[0]}
    halves = [sh.shape[0] // 2 for sh in shards.values()]
    w_in_t, w_out_g, w_1_g, w_2_g = [_own_shard_in_place(f"place_{n}", sh, chip_index)
                                     for n, sh in shards.items()]

    tm = _tile(s, 1024)

    (h1, r1), (w_in_t,) = _norm_fwd("norm_attn", x2d, g_attn, hook=_gather_hook(w_in_t, 0, halves[0]))

    q_a, = _project_by_class("proj_qa", h1, w_in_t, b_in, 0, qa, (1,))
    kv_a, = _project_by_class("proj_kva", h1, w_in_t, b_in, qa, kva, (1,))
    q_bs, (w_out_g,) = _project_by_class("proj_qb", h1, w_in_t, b_in, qa + kva, qb, DILATIONS,
                                         hook=_gather_hook(w_out_g, 0, halves[1]))
    k_bs = _project_by_class("proj_kb", h1, w_in_t, b_in, qa + kva + qb, qb, DILATIONS)
    v_bs = _project_by_class("proj_vb", h1, w_in_t, b_in, qa + kva + 2 * qb, qb, DILATIONS)

    quarter = halves[2] // 4
    sinks = sinks_a.reshape(-1)
    (o_a, lse_a), (w_1_g,) = _attn_fwd("attn_a_fwd", q_a, kv_a, dil=1, max_steps=WINDOW_A - 1, slopes=slopes_a,
                                       sinks=sinks, hook=_gather_hook(w_1_g, 0, quarter))
    o_a = o_a[0]
    o_bs, lse_bs = [], []
    for n, (window, dil) in enumerate(DILATED_BRANCHES):
        (o, l), (w_1_g,) = _attn_fwd(f"attn_b{dil}_fwd", q_bs[n], (k_bs[n], v_bs[n]), dil=dil,
                                     max_steps=window // dil, slopes=slopes_b,
                                     hook=_gather_hook(w_1_g, (n + 1) * quarter, (n + 2) * quarter))
        o_bs.append(o)
        lse_bs.append(l)
    w_1_g = w_1_g.reshape(N_CHIPS, d, ff_shard)
    mix, o_b, *lse_tot, r_a, r_b = _mix_fwd(o_a, o_bs, lse_bs, g_out_a, g_out_b)

    tn = _tile(d, 512)
    a_spec, b_spec = _mm_specs("nn", tm, tn, d)
    tile_mn = pl.BlockSpec((tm, tn), lambda i, j, k: (i, j))
    x2 = _matmul("out_proj", mix, w_out_g, [x2d], mode="nn", grid=(s // tm, d // tn, 1),
                 a_spec=a_spec, b_spec=b_spec, extra_specs=[tile_mn],
                 out_shapes=[jax.ShapeDtypeStruct((s, d), F32)], out_specs=[tile_mn],
                 epilogue=lambda acc, res: (acc + res,))[0]

    h2, r2 = _norm_fwd("norm_mlp", x2, g_mlp)

    tn = _tile(ff_shard, 512)
    per = ff_shard // tn
    a_spec, _ = _mm_specs("nn", tm, tn, d)
    (u,), (w_2_g,) = _matmul(
        "mlp_up", h2, w_1_g, [], mode="nn", grid=(s // tm, d_ff // tn, 1),
        a_spec=a_spec, b_spec=pl.BlockSpec((None, d, tn), lambda i, j, k: (j // per, 0, j % per)),
        extra_specs=[], out_shapes=[jax.ShapeDtypeStruct((s, d_ff), BF16)], out_specs=[tile_mn],
        epilogue=lambda acc: (jnp.maximum(acc, 0.0),),
        hook=_gather_hook(w_2_g, 0, halves[3]))

    tn = _tile(d, 1024)
    tk = _tile(d_ff, 2048)
    a_spec, b_spec = _mm_specs("nn", tm, tn, tk)
    tile_mn = pl.BlockSpec((tm, tn), lambda i, j, k: (i, j))
    x3 = _matmul("mlp_down", u, w_2_g, [x2], mode="nn", grid=(s // tm, d // tn, d_ff // tk),
                 a_spec=a_spec, b_spec=b_spec, extra_specs=[tile_mn],
                 out_shapes=[jax.ShapeDtypeStruct((s, d), F32)], out_specs=[tile_mn],
                 prologue=lambda a: a * a, epilogue=lambda acc, res: (acc + res,), acc_shape=(tm, tn))[0]

    dx3, dx3b, loss_part, dg_final = _loss_head(x3, target, g_final.reshape(1, d))

    tn = _tile(d_ff, 512)
    a_spec, b_spec = _mm_specs("nt", tm, tn, d)
    tile_mn = pl.BlockSpec((tm, tn), lambda i, j, k: (i, j))
    dpre = _matmul("mlp_down_dx", dx3b, w_2_g, [u], mode="nt", grid=(s // tm, d_ff // tn, 1),
                   a_spec=a_spec, b_spec=b_spec, extra_specs=[tile_mn],
                   out_shapes=[jax.ShapeDtypeStruct((s, d_ff), BF16)], out_specs=[tile_mn],
                   epilogue=lambda acc, uu: (acc * (2.0 * uu.astype(F32)),))[0]

    wire = GRAD_WIRE_DTYPE
    tk_s = _tile(s, 2048)
    tmw = _tile(d_ff, 1024)
    a_spec, b_spec = _mm_specs("tn", tmw, d, tk_s)
    dw_2 = _matmul("mlp_down_dw", u, dx3b, [], mode="tn", grid=(d_ff // tmw, 1, s // tk_s),
                   a_spec=a_spec, b_spec=b_spec, extra_specs=[],
                   out_shapes=[jax.ShapeDtypeStruct((d_ff, d), wire)],
                   out_specs=[pl.BlockSpec((tmw, d), lambda i, j, k: (i, j))],
                   prologue=lambda a: a * a, epilogue=lambda acc: (acc,), acc_shape=(tmw, d))[0]

    tn = _tile(d, 1024)
    tk = _tile(ff_shard, 2048)
    per = ff_shard // tk
    a_spec, _ = _mm_specs("nt", tm, tn, tk)
    tile_mn = pl.BlockSpec((tm, tn), lambda i, j, k: (i, j))
    dh2 = _matmul("mlp_up_dx", dpre, w_1_g, [], mode="nt", grid=(s // tm, d // tn, d_ff // tk),
                  a_spec=a_spec, b_spec=pl.BlockSpec((None, tn, tk), lambda i, j, k: (k // per, j, k % per)),
                  extra_specs=[], out_shapes=[jax.ShapeDtypeStruct((s, d), F32)], out_specs=[tile_mn],
                  epilogue=lambda acc: (acc,), acc_shape=(tm, tn))[0]

    tmw = _tile(d, 1024)
    tnw = _tile(ff_shard, 2048)
    per = ff_shard // tnw
    a_spec, b_spec = _mm_specs("tn", tmw, tnw, tk_s)
    dw_1 = _matmul("mlp_up_dw", h2, dpre, [], mode="tn", grid=(d // tmw, d_ff // tnw, s // tk_s),
                   a_spec=a_spec, b_spec=b_spec, extra_specs=[],
                   out_shapes=[jax.ShapeDtypeStruct((N_CHIPS, d, ff_shard), wire)],
                   out_specs=[pl.BlockSpec((None, tmw, tnw), lambda i, j, k: (j // per, i, j % per))],
                   epilogue=lambda acc: (acc,), acc_shape=(tmw, tnw))[0]

    dw_1 = dw_1.reshape(N_CHIPS * d, ff_shard)
    (dx2, dx2b, dg_mlp), (sib_2, sib_1) = _norm_bwd(
        "norm_mlp_bwd", dh2, x2, r2, g_mlp, dx3, hook=_merge_hooks([_exchange_hook(dw_2), _exchange_hook(dw_1)]))
    chip_sum_2 = _chip_add("chip_add_w_2", dw_2, sib_2, core_index)
    chip_sum_1 = _chip_add("chip_add_w_1", dw_1, sib_1, core_index)

    tn = _tile(d, 512)
    a_spec, b_spec = _mm_specs("nt", tm, tn, d)
    tile_mn = pl.BlockSpec((tm, tn), lambda i, j, k: (i, j))
    dmix = _matmul("out_proj_dx", dx2b, w_out_g, [], mode="nt", grid=(s // tm, d // tn, 1),
                   a_spec=a_spec, b_spec=b_spec, extra_specs=[],
                   out_shapes=[jax.ShapeDtypeStruct((s, d), F32)], out_specs=[tile_mn],
                   epilogue=lambda acc: (acc,))[0]

    tmw = _tile(d, 1024)
    a_spec, b_spec = _mm_specs("tn", tmw, d, tk_s)
    dw_out = _matmul("out_proj_dw", mix, dx2b, [], mode="tn", grid=(d // tmw, 1, s // tk_s),
                     a_spec=a_spec, b_spec=b_spec, extra_specs=[],
                     out_shapes=[jax.ShapeDtypeStruct((d, d), wire)],
                     out_specs=[pl.BlockSpec((tmw, d), lambda i, j, k: (i, j))],
                     epilogue=lambda acc: (acc,), acc_shape=(tmw, d))[0]

    mix_grads, (sib_out,) = _mix_bwd(dmix, o_a, o_b, r_a, r_b, g_out_a, g_out_b, hook=_exchange_hook(dw_out))
    do_a, do_bs, delta_a, delta_bs = mix_grads[0], mix_grads[1:4], mix_grads[4], mix_grads[5:8]
    dg_out_a, dg_out_b = mix_grads[8:]
    chip_sum_out = _chip_add("chip_add_w_out", dw_out, sib_out, core_index)

    (dq_a, dkv_a, dsinks), (chips_2,) = _attn_bwd(
        "attn_a_bwd", q_a, kv_a, do_a[None], lse_a, delta_a[None], dil=1, max_steps=WINDOW_A - 1,
        slopes=slopes_a, sinks=sinks, hook=_scatter_hook(chip_sum_2))
    dqs, dks, dvs = [], [], []
    scatter = {1: chip_sum_1, 4: chip_sum_out}
    arrived = {}
    for n, (window, dil) in enumerate(DILATED_BRANCHES):
        res = _attn_bwd(f"attn_b{dil}_bwd", q_bs[n], (k_bs[n], v_bs[n]), do_bs[n], lse_tot[n],
                        delta_bs[n], dil=dil, max_steps=window // dil, slopes=slopes_b,
                        hook=_scatter_hook(scatter[dil]) if dil in scatter else None)
        if dil in scatter:
            res, (arrived[dil],) = res
        dq, dk, dv = res
        dqs.append(dq)
        dks.append(dk)
        dvs.append(dv)
    half_2 = _final_add("final_add_w_2", chip_sum_2, chips_2, chip_index, core_index)
    half_1 = _final_add("final_add_w_1", chip_sum_1, arrived[1], chip_index, core_index)
    half_out = _final_add("final_add_w_out", chip_sum_out, arrived[4], chip_index, core_index)
    dproj, db_in = _assemble_dproj(dq_a[0], dkv_a[0], dqs, dks, dvs)

    tmw = d_in // 2 if (d_in // 2) % 128 == 0 else d_in
    tnw = _tile(d, 1024)
    tk_s = _tile(s, 1024)
    a_spec, b_spec = _mm_specs("tn", tmw, tnw, tk_s)
    dw_in_t = _matmul("in_proj_dw", dproj, h1, [], mode="tn", grid=(d_in // tmw, d // tnw, s // tk_s),
                      a_spec=a_spec, b_spec=b_spec, extra_specs=[],
                      out_shapes=[jax.ShapeDtypeStruct((d_in, d), wire)],
                      out_specs=[pl.BlockSpec((tmw, tnw), lambda i, j, k: (i, j))],
                      epilogue=lambda acc: (acc,), acc_shape=(tmw, tnw))[0]

    tn = _tile(d, 512)
    a_spec, b_spec = _mm_specs("nn", tm, tn, d_in)
    tile_mn = pl.BlockSpec((tm, tn), lambda i, j, k: (i, j))
    (dh1,), (sib_in,) = _matmul("in_proj_dx", dproj, w_in_t, [], mode="nn", grid=(s // tm, d // tn, 1),
                                a_spec=a_spec, b_spec=b_spec, extra_specs=[],
                                out_shapes=[jax.ShapeDtypeStruct((s, d), F32)], out_specs=[tile_mn],
                                epilogue=lambda acc: (acc,), hook=_exchange_hook(dw_in_t))
    chip_sum_in = _chip_add("chip_add_w_in", dw_in_t, sib_in, core_index)

    (grad_x, _, dg_attn), (chips_in,) = _norm_bwd("norm_attn_bwd", dh1, x2d, r1, g_attn, dx2,
                                                  hook=_scatter_hook(chip_sum_in))
    half_in = _final_add("final_add_w_in", chip_sum_in, chips_in, chip_index, core_index)

    small_parts = [dg_attn, db_in, dsinks[:, :n_heads_a], dg_out_a, dg_out_b, dg_mlp, dg_final]
    small_shapes = [g_attn.shape, b_in.shape, sinks_a.shape, g_out_a.shape, g_out_b.shape, g_mlp.shape,
                    g_final.shape]
    n_small = sum(int(np.prod(shp)) for shp in small_shapes)
    rows_s = -(-n_small // (8 * 128)) * 8
    (gw_in_t, gw_out, gw_1, gw_2), small_sum = _share_halves(
        [half_in, half_out, half_1, half_2], _pack_small(small_parts, rows_s))
    gw_in = gw_in_t.T

    upd_in = _adamw("adamw_w_in", w_in[0], gw_in, m_w_in[0], v_w_in[0])
    upd_out = _adamw("adamw_w_out", w_out[0], gw_out, m_w_out[0], v_w_out[0])
    upd_1 = _adamw("adamw_w_1", w_1[0], gw_1, m_w_1[0], v_w_1[0])
    upd_2 = _adamw("adamw_w_2", w_2[0], gw_2, m_w_2[0], v_w_2[0])
    small_w = [g_attn, b_in, sinks_a, g_out_a, g_out_b, g_mlp, g_final]
    small_m = [m_g_attn, m_b_in, m_sinks_a, m_g_out_a, m_g_out_b, m_g_mlp, m_g_final]
    small_v = [v_g_attn, v_b_in, v_sinks_a, v_g_out_a, v_g_out_b, v_g_mlp, v_g_final]
    upd_small = _adamw("adamw_small", _pack_small(small_w, rows_s), small_sum,
                       _pack_small(small_m, rows_s), _pack_small(small_v, rows_s))
    d_small, m_small, v_small, g_small = [_unpack_small(t, small_shapes) for t in upd_small]

    loss = lax.psum(loss_part[0, 0], ("x", "y", "c"))

    def ordered(small, big):
        w_in_v, w_out_v, w_1_v, w_2_v = big
        return [small[0], w_in_v[None], small[1], small[2], small[3], small[4], w_out_v[None], small[5],
                w_1_v[None], w_2_v[None], small[6]]

    grads = ordered(g_small, (upd_in[3], upd_out[3], upd_1[3], upd_2[3]))
    deltas = ordered(d_small, (upd_in[0], upd_out[0], upd_1[0], upd_2[0]))
    new_m = ordered(m_small, (upd_in[1], upd_out[1], upd_1[1], upd_2[1]))
    new_v = ordered(v_small, (upd_in[2], upd_out[2], upd_1[2], upd_2[2]))
    return (loss, grad_x[None], *grads, *deltas, *new_m, *new_v)
```

```python
import jax
import jax.numpy as jnp
import numpy as np
from jax import lax
from jax.experimental import pallas as pl
from jax.experimental.pallas import tpu as pltpu

F32 = jnp.float32
BF16 = jnp.bfloat16

HEAD_DIM = 64
BLOCK = 128
PAIR = 2 * HEAD_DIM
N_KV_GROUPS = 2
WINDOW_A = 128
DILATED_BRANCHES = ((128, 1), (512, 4), (2048, 16))
EPS = 1e-5
NEG_INF = -1e30
ATT_SCALE = HEAD_DIM ** -0.5

ADAM_LR = 0.001
ADAM_B1 = 0.9
ADAM_B2 = 0.999
ADAM_EPS = 1e-08
ADAM_WD = 0.01
ADAM_STEP = 10

N_CHIPS = 4
N_DEV = 8
MESH = pl.DeviceIdType.MESH
GRAD_WIRE_DTYPE = jnp.bfloat16
BRANCH_DTYPE = jnp.bfloat16

VMEM_CAPACITY_V7X = 64 * 1024 * 1024
VMEM_LIMIT_MAX = VMEM_CAPACITY_V7X - 8 * 1024 * 1024
VMEM_LIMIT_MIN = VMEM_CAPACITY_V7X - 16 * 1024 * 1024

HBM_SPEC = pl.BlockSpec(memory_space=pltpu.HBM)
VMEM_SPEC = pl.BlockSpec(memory_space=pltpu.VMEM)
SMEM_SPEC = pl.BlockSpec(memory_space=pltpu.SMEM)


def _nbytes(shape, dtype):
    return int(np.prod([s for s in shape if s is not None])) * jnp.dtype(dtype).itemsize


def _params(semantics, block_bytes):
    limit = min(max(2 * block_bytes + (4 << 20), VMEM_LIMIT_MIN), VMEM_LIMIT_MAX)
    return pltpu.CompilerParams(dimension_semantics=semantics, vmem_limit_bytes=limit)


class _Hook:
    def __init__(self, operands, out_shape, sems, start, finish, mid=None, aliases=None):
        self.operands, self.out_shape, self.sems = list(operands), list(out_shape), list(sems)
        self.start, self.mid, self.finish = start, mid, finish
        self.aliases = dict(aliases or {})


def _merge_hooks(hooks):
    hooks = [h for h in hooks if h is not None]
    if len(hooks) <= 1:
        return hooks[0] if hooks else None
    n_op = np.cumsum([0] + [len(h.operands) for h in hooks])
    n_out = np.cumsum([0] + [len(h.out_shape) for h in hooks])
    n_sem = np.cumsum([0] + [len(h.sems) for h in hooks])

    def run(which):
        def fn(ops, outs, sems):
            for i, h in enumerate(hooks):
                f = getattr(h, which)
                if f is not None:
                    f(ops[n_op[i]:n_op[i + 1]], outs[n_out[i]:n_out[i + 1]], sems[n_sem[i]:n_sem[i + 1]])
        return fn

    aliases = {}
    for i, h in enumerate(hooks):
        aliases.update({int(n_op[i]) + a: int(n_out[i]) + b for a, b in h.aliases.items()})
    return _Hook(sum([h.operands for h in hooks], []), sum([h.out_shape for h in hooks], []),
                 sum([h.sems for h in hooks], []), run("start"), run("finish"),
                 run("mid") if any(h.mid for h in hooks) else None, aliases)


HOOK_MID_FRACTION = 0.6


def _call(body, hook, *, name, grid, in_specs, out_specs, out_shape, scratch_shapes=(), compiler_params):
    in_specs, out_specs, out_shape = list(in_specs), list(out_specs), list(out_shape)
    scratch_shapes = list(scratch_shapes)
    if hook is None:
        call = pl.pallas_call(body, name=name, grid=grid, in_specs=in_specs, out_specs=out_specs,
                              out_shape=out_shape, scratch_shapes=scratch_shapes,
                              compiler_params=compiler_params)
        return lambda *operands: (call(*operands), [])
    n_in, n_hin, n_out, n_hout, n_scr = (len(in_specs), len(hook.operands), len(out_specs),
                                         len(hook.out_shape), len(scratch_shapes))
    total = int(np.prod(grid))
    t_mid = min(int(total * HOOK_MID_FRACTION), total - 1)

    def wrapped(*refs):
        ins, h_in = refs[:n_in], refs[n_in:n_in + n_hin]
        o0 = n_in + n_hin
        outs, h_out = refs[o0:o0 + n_out], refs[o0 + n_out:o0 + n_out + n_hout]
        s0 = o0 + n_out + n_hout
        scr, h_sems = refs[s0:s0 + n_scr], refs[s0 + n_scr:]
        t = pl.program_id(0)
        for axis in range(1, len(grid)):
            t = t * grid[axis] + pl.program_id(axis)

        @pl.when(t == 0)
        def _():
            hook.start(h_in, h_out, h_sems)

        body(*ins, *outs, *scr)
        if hook.mid is not None:
            @pl.when(t == t_mid)
            def _():
                hook.mid(h_in, h_out, h_sems)

        @pl.when(t == total - 1)
        def _():
            hook.finish(h_in, h_out, h_sems)

    params = pltpu.CompilerParams(dimension_semantics=("arbitrary",) * len(grid),
                                  vmem_limit_bytes=compiler_params.vmem_limit_bytes)
    call = pl.pallas_call(
        wrapped, name=name, grid=grid,
        in_specs=in_specs + [HBM_SPEC] * n_hin, out_specs=out_specs + [HBM_SPEC] * n_hout,
        out_shape=out_shape + hook.out_shape, scratch_shapes=scratch_shapes + hook.sems,
        input_output_aliases={n_in + a: n_out + b for a, b in hook.aliases.items()},
        compiler_params=params)

    def run(*operands):
        res = call(*operands, *hook.operands)
        return res[:n_out], res[n_out:]

    return run


def _remote(src, dst, send_sem, recv_sem, device):
    return pltpu.make_async_remote_copy(src_ref=src, dst_ref=dst, send_sem=send_sem, recv_sem=recv_sem,
                                        device_id=device, device_id_type=MESH)


def alibi_slopes(n):
    return [float(v) for v in np.asarray(2.0 ** (-8.0 * (np.arange(n) + 1) / n), dtype=np.float32)]


def _matmul(name, a, b, extras, *, mode, grid, a_spec, b_spec, extra_specs, out_shapes, out_specs,
            epilogue, prologue=None, acc_shape=None, hook=None):
    dims = {"nn": ((1,), (0,)), "nt": ((1,), (1,)), "tn": ((0,), (0,))}[mode]
    nk = grid[2]
    n_ex, n_out = len(extras), len(out_shapes)

    def body(a_ref, b_ref, *rest):
        ex, outs = rest[:n_ex], rest[n_ex:n_ex + n_out]
        av = a_ref[...]
        if prologue is not None:
            av = prologue(av)
        part = lax.dot_general(av, b_ref[...], (dims, ((), ())), preferred_element_type=F32)

        def finish(acc):
            res = epilogue(acc, *[e[...] for e in ex])
            for o, r in zip(outs, res):
                o[...] = r.astype(o.dtype)

        if nk == 1:
            finish(part)
        else:
            acc_ref = rest[-1]
            k = pl.program_id(2)

            @pl.when(k == 0)
            def _():
                acc_ref[...] = part

            @pl.when(k > 0)
            def _():
                acc_ref[...] += part

            @pl.when(k == nk - 1)
            def _():
                finish(acc_ref[...])

    blocks = [(a_spec.block_shape, a.dtype), (b_spec.block_shape, b.dtype)]
    blocks += [(s.block_shape, e.dtype) for s, e in zip(extra_specs, extras)]
    blocks += [(s.block_shape, o.dtype) for s, o in zip(out_specs, out_shapes)]
    nbytes = sum(_nbytes(s, d) for s, d in blocks)
    scratch = []
    if nk > 1:
        scratch.append(pltpu.VMEM(acc_shape, F32))
        nbytes += _nbytes(acc_shape, F32)
    res, hook_res = _call(
        body, hook, name=name, grid=grid,
        in_specs=[a_spec, b_spec, *extra_specs], out_specs=list(out_specs), out_shape=list(out_shapes),
        scratch_shapes=scratch,
        compiler_params=_params(("parallel", "parallel", "arbitrary"), nbytes),
    )(a, b, *extras)
    return res if hook is None else (res, hook_res)


def _mm_specs(mode, tm, tn, tk, b_block=None, b_map=None):
    if mode == "tn":
        a_spec = pl.BlockSpec((tk, tm), lambda i, j, k: (k, i))
    else:
        a_spec = pl.BlockSpec((tm, tk), lambda i, j, k: (i, k))
    if b_block is not None:
        b_spec = pl.BlockSpec(b_block, b_map)
    elif mode == "nt":
        b_spec = pl.BlockSpec((tn, tk), lambda i, j, k: (j, k))
    else:
        b_spec = pl.BlockSpec((tk, tn), lambda i, j, k: (k, j))
    return a_spec, b_spec


def _project_by_class(name, h, w_t, bias, row_off, width, dilations, hook=None):
    s, d = h.shape
    tm = _tile(s, 1024)
    tn = 512 if width % 512 == 0 and row_off % 512 == 0 else _tile(width, 256)
    off = row_off // tn
    assert row_off % tn == 0 and tn % 128 == 0
    n_out = len(dilations)

    def body(h_ref, w_ref, b_ref, *rest):
        outs, perm_ref = rest[:n_out], rest[n_out]
        acc = lax.dot_general(h_ref[...], w_ref[...], (((1,), (1,)), ((), ())), preferred_element_type=F32)
        acc = acc + b_ref[...]
        for j in range(tn // 128):
            cols = slice(j * 128, (j + 1) * 128)
            for o_ref, dil in zip(outs, dilations):
                _to_classes(o_ref, cols, acc[:, cols], perm_ref, dil)

    blocks = tm * d * 2 + tn * d * 2 + 3 * tm * tn * 2 + tm * 128 * 4
    res, hook_res = _call(
        body, hook, name=name, grid=(s // tm, width // tn),
        in_specs=[pl.BlockSpec((tm, d), lambda i, j: (i, 0)), pl.BlockSpec((tn, d), lambda i, j: (j + off, 0)),
                  pl.BlockSpec((1, tn), lambda i, j: (0, j + off))],
        out_specs=[pl.BlockSpec((dil, tm // dil, tn), lambda i, j: (0, i, j)) for dil in dilations],
        out_shape=[_class_shape(dil, s, width, BF16) for dil in dilations],
        scratch_shapes=[pltpu.VMEM((tm, 128), F32)],
        compiler_params=_params(("parallel", "parallel"), blocks),
    )(h, w_t, bias)
    return res if hook is None else (res, hook_res)


def _tile(n, want):
    if n <= want:
        return n
    t = (want // 128) * 128
    while t > 128 and n % t:
        t -= 128
    assert n % t == 0, (n, want)
    return t


def _row_tile(s):
    return 256 if s % 256 == 0 else s


def _norm_fwd(name, x, g, hook=None):
    s, d = x.shape
    tm = _row_tile(s)

    def body(x_ref, g_ref, h_ref, r_ref):
        xv = x_ref[...]
        r = lax.rsqrt(jnp.mean(xv * xv, axis=-1, keepdims=True) + EPS)
        h_ref[...] = ((xv * r) * g_ref[...]).astype(BF16)
        r_ref[...] = r

    row = pl.BlockSpec((tm, d), lambda i: (i, 0))
    res, hook_res = _call(
        body, hook, name=name, grid=(s // tm,),
        in_specs=[row, pl.BlockSpec((1, d), lambda i: (0, 0))],
        out_specs=[row, pl.BlockSpec((tm, 1), lambda i: (i, 0))],
        out_shape=[jax.ShapeDtypeStruct((s, d), BF16), jax.ShapeDtypeStruct((s, 1), F32)],
        compiler_params=_params(("parallel",), tm * d * 6),
    )(x, g)
    return res if hook is None else (res, hook_res)


def _norm_bwd(name, dh, x, r, g, dres, hook=None):
    s, d = x.shape
    tm = _row_tile(s)

    def body(dh_ref, x_ref, r_ref, g_ref, dres_ref, dx_ref, dxb_ref, dg_ref):
        rv = r_ref[...]
        xn = x_ref[...] * rv
        dhv = dh_ref[...]
        dxn = dhv * g_ref[...]
        dx = dres_ref[...] + rv * (dxn - xn * jnp.mean(dxn * xn, axis=-1, keepdims=True))
        dx_ref[...] = dx
        dxb_ref[...] = dx.astype(BF16)
        part = jnp.sum(dhv * xn, axis=0, keepdims=True)

        @pl.when(pl.program_id(0) == 0)
        def _():
            dg_ref[...] = part

        @pl.when(pl.program_id(0) > 0)
        def _():
            dg_ref[...] += part

    row = pl.BlockSpec((tm, d), lambda i: (i, 0))
    vec = pl.BlockSpec((1, d), lambda i: (0, 0))
    res, hook_res = _call(
        body, hook, name=name, grid=(s // tm,),
        in_specs=[row, row, pl.BlockSpec((tm, 1), lambda i: (i, 0)), vec, row],
        out_specs=[row, row, vec],
        out_shape=[jax.ShapeDtypeStruct((s, d), F32), jax.ShapeDtypeStruct((s, d), BF16),
                   jax.ShapeDtypeStruct((1, d), F32)],
        compiler_params=_params(("arbitrary",), tm * d * 18),
    )(dh, x, r, g, dres)
    return res if hook is None else (res, hook_res)


def _loss_head(x3, target, g):
    s, d = x3.shape
    tm = _row_tile(s)

    def body(x_ref, t_ref, g_ref, dx_ref, dxb_ref, loss_ref, dg_ref):
        xv = x_ref[...]
        gv = g_ref[...]
        r = lax.rsqrt(jnp.mean(xv * xv, axis=-1, keepdims=True) + EPS)
        xn = xv * r
        err = xn * gv - t_ref[...]
        loss = 0.5 * jnp.sum(jnp.mean(err * err, axis=-1, keepdims=True), axis=0, keepdims=True)
        dy = err / d
        dxn = dy * gv
        dx = r * (dxn - xn * jnp.mean(dxn * xn, axis=-1, keepdims=True))
        dx_ref[...] = dx
        dxb_ref[...] = dx.astype(BF16)
        dg = jnp.sum(dy * xn, axis=0, keepdims=True)
        loss_row = jnp.broadcast_to(loss, (1, 128))

        @pl.when(pl.program_id(0) == 0)
        def _():
            dg_ref[...] = dg
            loss_ref[...] = loss_row

        @pl.when(pl.program_id(0) > 0)
        def _():
            dg_ref[...] += dg
            loss_ref[...] += loss_row

    row = pl.BlockSpec((tm, d), lambda i: (i, 0))
    vec = pl.BlockSpec((1, d), lambda i: (0, 0))
    return _call(
        body, None, name="loss_head", grid=(s // tm,),
        in_specs=[row, row, vec],
        out_specs=[row, row, pl.BlockSpec((1, 128), lambda i: (0, 0)), vec],
        out_shape=[jax.ShapeDtypeStruct((s, d), F32), jax.ShapeDtypeStruct((s, d), BF16),
                   jax.ShapeDtypeStruct((1, 128), F32), jax.ShapeDtypeStruct((1, d), F32)],
        compiler_params=_params(("arbitrary",), tm * d * 14),
    )(x3, target, g)[0]


def _low_lanes(rows):
    return lax.broadcasted_iota(jnp.int32, (rows, PAIR), 1) < HEAD_DIM


def _to_classes(dst_ref, cols, value, perm_ref, dil):
    rows = value.shape[0]
    if dil == 1:
        dst_ref[0, :, cols] = value.astype(dst_ref.dtype)
        return
    perm_ref[...] = value
    for r in range(dil):
        dst_ref[r, :, cols] = perm_ref[pl.ds(r, rows // dil, stride=dil), :].astype(dst_ref.dtype)


def _from_classes(src_ref, cols, perm_ref, dil):
    if dil == 1:
        return src_ref[0, :, cols].astype(F32)
    rows = perm_ref.shape[0]
    for r in range(dil):
        perm_ref[pl.ds(r, rows // dil, stride=dil), :] = src_ref[r, :, cols].astype(F32)
    return perm_ref[...]


def _class_spec(dil, tm, width):
    return pl.BlockSpec((dil, tm // dil, width), lambda i: (0, i, 0))


def _class_shape(dil, s, width, dtype):
    return jax.ShapeDtypeStruct((dil, s // dil, width), dtype)


DILATIONS = tuple(d for _, d in DILATED_BRANCHES)


def _mix_fwd(oa, obs, lses, ga, gb):
    s, qa = oa.shape
    qb = obs[0].shape[2]
    tm = _row_tile(s)
    all_lanes = slice(0, 128)

    def body(oa_ref, o1_ref, o2_ref, o3_ref, l1_ref, l2_ref, l3_ref, ga_ref, gb_ref,
             mix_ref, ob_ref, t1_ref, t2_ref, t3_ref, ra_ref, rb_ref, perm_ref):
        oav = oa_ref[...]
        ra = lax.rsqrt(jnp.mean(oav * oav, axis=-1, keepdims=True) + EPS)
        ra_ref[...] = ra
        mix_ref[:, 0:qa] = ((oav * ra) * ga_ref[...]).astype(BF16)
        l1, l2, l3 = [_from_classes(l_ref, all_lanes, perm_ref, dil)
                      for l_ref, dil in zip((l1_ref, l2_ref, l3_ref), DILATIONS)]
        mx = jnp.maximum(jnp.maximum(l1, l2), l3)
        e1, e2, e3 = jnp.exp(l1 - mx), jnp.exp(l2 - mx), jnp.exp(l3 - mx)
        tot = e1 + e2 + e3
        lse = mx + jnp.log(tot)
        for t_ref, dil in zip((t1_ref, t2_ref, t3_ref), DILATIONS):
            _to_classes(t_ref, all_lanes, lse, perm_ref, dil)
        ws = (e1 / tot, e2 / tot, e3 / tot)
        low = _low_lanes(tm)
        ssq = jnp.zeros((tm, 1), F32)
        for i in range(qb // PAIR):
            sl = slice(i * PAIR, (i + 1) * PAIR)
            acc = jnp.zeros((tm, PAIR), F32)
            for w, o_ref, dil in zip(ws, (o1_ref, o2_ref, o3_ref), DILATIONS):
                wexp = jnp.where(low, w[:, 2 * i:2 * i + 1], w[:, 2 * i + 1:2 * i + 2])
                acc = acc + wexp * _from_classes(o_ref, sl, perm_ref, dil)
            ob_ref[:, sl] = acc
            ssq = ssq + jnp.sum(acc * acc, axis=-1, keepdims=True)
        rb = lax.rsqrt(ssq / qb + EPS)
        rb_ref[...] = rb
        mix_ref[:, qa:qa + qb] = ((ob_ref[...] * rb) * gb_ref[...]).astype(BF16)

    def row(w):
        return pl.BlockSpec((tm, w), lambda i: (i, 0))

    def vec(w):
        return pl.BlockSpec((1, w), lambda i: (0, 0))

    return _call(
        body, None, name="mix_fwd", grid=(s // tm,),
        in_specs=([row(qa)] + [_class_spec(d, tm, qb) for d in DILATIONS]
                  + [_class_spec(d, tm, 128) for d in DILATIONS] + [vec(qa), vec(qb)]),
        out_specs=([row(qa + qb), row(qb)] + [_class_spec(d, tm, 128) for d in DILATIONS] + [row(1), row(1)]),
        out_shape=([jax.ShapeDtypeStruct((s, qa + qb), BF16), jax.ShapeDtypeStruct((s, qb), F32)]
                   + [_class_shape(d, s, 128, F32) for d in DILATIONS]
                   + [jax.ShapeDtypeStruct((s, 1), F32), jax.ShapeDtypeStruct((s, 1), F32)]),
        scratch_shapes=[pltpu.VMEM((tm, 128), F32)],
        compiler_params=_params(("parallel",), tm * (qa + 4 * qb) * 4 + tm * (qa + qb) * 2 + tm * 4096),
    )(oa, *obs, *lses, ga, gb)[0]


def _head_rowsums(prod, rows):
    low = _low_lanes(rows)
    lane = lax.broadcasted_iota(jnp.int32, (rows, 128), 1)
    out = jnp.zeros((rows, 128), F32)
    for i in range(prod.shape[1] // PAIR):
        tile = prod[:, i * PAIR:(i + 1) * PAIR]
        lo = jnp.sum(jnp.where(low, tile, 0.0), axis=-1, keepdims=True)
        hi = jnp.sum(jnp.where(low, 0.0, tile), axis=-1, keepdims=True)
        out = jnp.where(lane == 2 * i, lo, out)
        out = jnp.where(lane == 2 * i + 1, hi, out)
    return out


def _mix_bwd(dmix, oa, ob, ra, rb, ga, gb, hook=None):
    s, qa = oa.shape
    qb = ob.shape[1]
    tm = _row_tile(s)

    def one(dy, o, r, g):
        xn = o * r
        dxn = dy * g
        do = r * (dxn - xn * jnp.mean(dxn * xn, axis=-1, keepdims=True))
        return do, jnp.sum(dy * xn, axis=0, keepdims=True), _head_rowsums(do * o, tm)

    def body(dmix_ref, oa_ref, ob_ref, ra_ref, rb_ref, ga_ref, gb_ref,
             doa_ref, dob1_ref, dob2_ref, dob3_ref, dla_ref, dlb1_ref, dlb2_ref, dlb3_ref,
             dga_ref, dgb_ref, perm_ref):
        doa, dga, dla = one(dmix_ref[:, 0:qa], oa_ref[...], ra_ref[...], ga_ref[...])
        dob, dgb, dlb = one(dmix_ref[:, qa:qa + qb], ob_ref[...], rb_ref[...], gb_ref[...])
        doa_ref[...] = doa.astype(BF16)
        dla_ref[...] = dla
        for dob_ref, dlb_ref, dil in zip((dob1_ref, dob2_ref, dob3_ref), (dlb1_ref, dlb2_ref, dlb3_ref),
                                         DILATIONS):
            _to_classes(dlb_ref, slice(0, 128), dlb, perm_ref, dil)
            for i in range(qb // PAIR):
                sl = slice(i * PAIR, (i + 1) * PAIR)
                _to_classes(dob_ref, sl, dob[:, sl], perm_ref, dil)

        @pl.when(pl.program_id(0) == 0)
        def _():
            dga_ref[...] = dga
            dgb_ref[...] = dgb

        @pl.when(pl.program_id(0) > 0)
        def _():
            dga_ref[...] += dga
            dgb_ref[...] += dgb

    def row(w):
        return pl.BlockSpec((tm, w), lambda i: (i, 0))

    def vec(w):
        return pl.BlockSpec((1, w), lambda i: (0, 0))

    res, hook_res = _call(
        body, hook, name="mix_bwd", grid=(s // tm,),
        in_specs=[row(qa + qb), row(qa), row(qb), row(1), row(1), vec(qa), vec(qb)],
        out_specs=([row(qa)] + [_class_spec(d, tm, qb) for d in DILATIONS] + [row(128)]
                   + [_class_spec(d, tm, 128) for d in DILATIONS] + [vec(qa), vec(qb)]),
        out_shape=([jax.ShapeDtypeStruct((s, qa), BF16)] + [_class_shape(d, s, qb, BF16) for d in DILATIONS]
                   + [jax.ShapeDtypeStruct((s, 128), F32)] + [_class_shape(d, s, 128, F32) for d in DILATIONS]
                   + [jax.ShapeDtypeStruct((1, qa), F32), jax.ShapeDtypeStruct((1, qb), F32)]),
        scratch_shapes=[pltpu.VMEM((tm, 128), F32)],
        compiler_params=_params(("arbitrary",), tm * (qa + qb) * 16),
    )(dmix, oa, ob, ra, rb, ga, gb)
    return res if hook is None else (res, hook_res)


def _assemble_dproj(dqa, dkva, dqs, dks, dvs):
    s, qa = dqa.shape
    kva = dkva.shape[1]
    qb = dqs[0].shape[2]
    width = qa + kva + 3 * qb
    tm = _row_tile(s)

    def body(dqa_ref, dkva_ref, q1, q2, q3, k1, k2, k3, v1, v2, v3, dp_ref, db_ref, perm_ref):
        first = pl.program_id(0) == 0

        def emit(off, val):
            dp_ref[:, off:off + PAIR] = val.astype(BF16)
            col = jnp.sum(val, axis=0, keepdims=True)

            @pl.when(first)
            def _():
                db_ref[:, off:off + PAIR] = col

            @pl.when(jnp.logical_not(first))
            def _():
                db_ref[:, off:off + PAIR] += col

        for i in range(qa // PAIR):
            emit(i * PAIR, dqa_ref[:, i * PAIR:(i + 1) * PAIR])
        for i in range(kva // PAIR):
            emit(qa + i * PAIR, dkva_ref[:, i * PAIR:(i + 1) * PAIR])
        for j, branch_refs in enumerate(((q1, q2, q3), (k1, k2, k3), (v1, v2, v3))):
            for i in range(qb // PAIR):
                sl = slice(i * PAIR, (i + 1) * PAIR)
                total = None
                for ref, dil in zip(branch_refs, DILATIONS):
                    val = _from_classes(ref, sl, perm_ref, dil)
                    total = val if total is None else total + val
                emit(qa + kva + j * qb + i * PAIR, total)

    def row(w):
        return pl.BlockSpec((tm, w), lambda i: (i, 0))

    return _call(
        body, None, name="assemble_dproj", grid=(s // tm,),
        in_specs=[row(qa), row(kva)] + [_class_spec(d, tm, qb) for d in DILATIONS] * 3,
        out_specs=[row(width), pl.BlockSpec((1, width), lambda i: (0, 0))],
        out_shape=[jax.ShapeDtypeStruct((s, width), BF16), jax.ShapeDtypeStruct((1, width), F32)],
        scratch_shapes=[pltpu.VMEM((tm, 128), F32)],
        compiler_params=_params(("arbitrary",), tm * (qa + kva + 9 * qb) * 4 + tm * width * 2),
    )(dqa, dkva, *dqs, *dks, *dvs)[0]


def _fill_bias(bias_ref, n_pairs, max_steps, dil, slopes, sink_ref=None):
    qi = lax.broadcasted_iota(jnp.int32, (BLOCK, 2 * BLOCK), 0)
    kj = lax.broadcasted_iota(jnp.int32, (BLOCK, 2 * BLOCK), 1)
    steps = qi + BLOCK - kj
    dist = (steps * dil).astype(F32)
    band = (steps >= 0) & (steps <= max_steps)
    assert sink_ref is None or max_steps < BLOCK
    for first in (0, 1):
        valid = band & (kj >= BLOCK) if first else band
        for i in range(n_pairs):
            tables = []
            for half in (0, 1):
                table = jnp.where(valid, -(slopes[2 * i + half] * dist), NEG_INF)
                if sink_ref is not None:
                    table = jnp.where(kj == 0, sink_ref[2 * i + half], table)
                tables.append(table)
            bias_ref[first, i] = jnp.concatenate(tables, axis=0)


def _without_sink_row(tile):
    row = lax.broadcasted_iota(jnp.int32, tile.shape, 0)
    return jnp.where(row == 0, jnp.zeros_like(tile), tile)


def _bias_shape(n_pairs):
    return pltpu.VMEM((2, n_pairs, 2 * BLOCK, 2 * BLOCK), F32)


def _stack_heads(tile, low):
    zero = jnp.zeros_like(tile)
    return jnp.concatenate([jnp.where(low, tile, zero), jnp.where(low, zero, tile)], axis=0)


def _unstack_heads(stacked, low):
    return jnp.where(low, stacked[0:BLOCK], stacked[BLOCK:2 * BLOCK])


def _head_columns(ref, i):
    return jnp.concatenate([ref[:, 2 * i:2 * i + 1], ref[:, 2 * i + 1:2 * i + 2]], axis=0)


def _swap_halves(t):
    return pltpu.roll(t, HEAD_DIM, 1)


def _dup_group(t_bf16, group):
    t = t_bf16.astype(F32)
    low = lax.broadcasted_iota(jnp.int32, t.shape, 1) < HEAD_DIM
    keep = low if group == 0 else jnp.logical_not(low)
    return jnp.where(keep, t, _swap_halves(t)).astype(BF16)


def _attn_fwd(name, q, kv, *, dil, max_steps, slopes, sinks=None, hook=None):
    grouped = sinks is not None
    _, length, w = q.shape
    n_pairs = w // PAIR
    nb = length // BLOCK
    heads_per_group = 2 * n_pairs // N_KV_GROUPS

    def body(*refs):
        if grouped:
            sink_ref, q_ref, kvp_ref, kvc_ref, o_ref, lse_ref, bias_ref = refs
        else:
            q_ref, kp_ref, kc_ref, vp_ref, vc_ref, o_ref, lse_ref, bias_ref = refs
        n = pl.program_id(1)

        @pl.when((pl.program_id(0) == 0) & (n == 0))
        def _():
            _fill_bias(bias_ref, n_pairs, max_steps, dil, slopes, sink_ref if grouped else None)

        first = (n == 0).astype(jnp.int32)
        low = _low_lanes(BLOCK)
        lane = lax.broadcasted_iota(jnp.int32, (BLOCK, 128), 1)
        lse_acc = jnp.zeros((BLOCK, 128), F32)
        if grouped:
            kv_all = jnp.concatenate([kvp_ref[...], kvc_ref[...]], axis=0)
            k_dup = [_without_sink_row(_dup_group(kv_all[:, 0:PAIR], g)) for g in range(N_KV_GROUPS)]
            v_dup = [_without_sink_row(_dup_group(kv_all[:, PAIR:2 * PAIR], g)) for g in range(N_KV_GROUPS)]
        for i in range(n_pairs):
            sl = slice(i * PAIR, (i + 1) * PAIR)
            qs = _stack_heads(q_ref[:, sl] * ATT_SCALE, low)
            if grouped:
                kk, vv = k_dup[2 * i // heads_per_group], v_dup[2 * i // heads_per_group]
            else:
                kk = jnp.concatenate([kp_ref[:, sl], kc_ref[:, sl]], axis=0)
                vv = jnp.concatenate([vp_ref[:, sl], vc_ref[:, sl]], axis=0)
            sc = lax.dot_general(qs, kk, (((1,), (1,)), ((), ())), preferred_element_type=F32)
            sc = sc + bias_ref[first, i]
            m = jnp.max(sc, axis=-1, keepdims=True)
            p = jnp.exp(sc - m)
            den = jnp.sum(p, axis=-1, keepdims=True)
            o = jnp.dot(p.astype(BF16), vv, preferred_element_type=F32) / den
            o_ref[:, sl] = _unstack_heads(o, low).astype(o_ref.dtype)
            lse = m + jnp.log(den)
            lse_acc = jnp.where(lane == 2 * i, lse[0:BLOCK], lse_acc)
            lse_acc = jnp.where(lane == 2 * i + 1, lse[BLOCK:2 * BLOCK], lse_acc)
        lse_ref[...] = lse_acc

    def cur(width):
        return pl.BlockSpec((None, BLOCK, width), lambda r, n: (r, n, 0))

    def prev(width):
        return pl.BlockSpec((None, BLOCK, width), lambda r, n: (r, jnp.maximum(n - 1, 0), 0))

    if grouped:
        kvw = kv.shape[2]
        operands = [sinks, q, kv, kv]
        in_specs = [SMEM_SPEC, cur(w), prev(kvw), cur(kvw)]
    else:
        operands = [q, kv[0], kv[0], kv[1], kv[1]]
        in_specs = [cur(w), prev(w), cur(w), prev(w), cur(w)]
    res, hook_res = _call(
        body, hook, name=name, grid=(dil, nb), in_specs=in_specs,
        out_specs=[cur(w), cur(128)],
        out_shape=[jax.ShapeDtypeStruct((dil, length, w), F32 if grouped else BRANCH_DTYPE),
                   jax.ShapeDtypeStruct((dil, length, 128), F32)],
        scratch_shapes=[_bias_shape(n_pairs)],
        compiler_params=_params(("arbitrary", "arbitrary"), BLOCK * w * 16 + n_pairs * BLOCK * BLOCK * 16),
    )(*operands)
    return res if hook is None else (res, hook_res)


def _attn_bwd(name, q, kv, do, lse, delta, *, dil, max_steps, slopes, sinks=None, hook=None):
    grouped = sinks is not None
    _, length, w = q.shape
    n_pairs = w // PAIR
    nb = length // BLOCK
    heads_per_group = 2 * n_pairs // N_KV_GROUPS
    pairs_per_group = n_pairs // N_KV_GROUPS

    def body(*refs):
        if grouped:
            (sink_ref, q_ref, kvp_ref, kvc_ref, do_ref, lse_ref, dl_ref,
             dq_ref, dkv_ref, dsink_ref, acc_ref, bias_ref) = refs
        else:
            (q_ref, kp_ref, kc_ref, vp_ref, vc_ref, do_ref, lse_ref, dl_ref,
             dq_ref, dk_ref, dv_ref, acck_ref, accv_ref, bias_ref) = refs
        n = pl.program_id(1)

        @pl.when((pl.program_id(0) == 0) & (n == 0))
        def _():
            _fill_bias(bias_ref, n_pairs, max_steps, dil, slopes, sink_ref if grouped else None)

        @pl.when(n == 0)
        def _():
            if grouped:
                acc_ref[...] = jnp.zeros_like(acc_ref)

                @pl.when(pl.program_id(0) == 0)
                def _():
                    dsink_ref[...] = jnp.zeros_like(dsink_ref)
            else:
                acck_ref[...] = jnp.zeros_like(acck_ref)
                accv_ref[...] = jnp.zeros_like(accv_ref)

        @pl.when(n == nb)
        def _():
            if grouped:
                dkv_ref[...] = acc_ref[...]
            else:
                dk_ref[...] = acck_ref[...].astype(dk_ref.dtype)
                dv_ref[...] = accv_ref[...].astype(dv_ref.dtype)

        @pl.when(n < nb)
        def _():
            first = (n == 0).astype(jnp.int32)
            low = _low_lanes(BLOCK)
            low_kv = _low_lanes(2 * BLOCK)
            lane1 = lax.broadcasted_iota(jnp.int32, (1, 128), 1)
            if grouped:
                kv_all = jnp.concatenate([kvp_ref[...], kvc_ref[...]], axis=0)
                k_dup = [_without_sink_row(_dup_group(kv_all[:, 0:PAIR], g)) for g in range(N_KV_GROUPS)]
                v_dup = [_without_sink_row(_dup_group(kv_all[:, PAIR:2 * PAIR], g)) for g in range(N_KV_GROUPS)]
                dk_grp =[jnp.zeros((2 * BLOCK, PAIR), F32) for _ in range(N_KV_GROUPS)]
                dv_grp = [jnp.zeros((2 * BLOCK, PAIR), F32) for _ in range(N_KV_GROUPS)]
                dsink = jnp.zeros((1, 128), F32)
            for i in range(n_pairs):
                sl = slice(i * PAIR, (i + 1) * PAIR)
                qs = _stack_heads(q_ref[:, sl] * ATT_SCALE, low)
                dos = _stack_heads(do_ref[:, sl], low)
                if grouped:
                    grp = 2 * i // heads_per_group
                    kk, vv = k_dup[grp], v_dup[grp]
                else:
                    kk = jnp.concatenate([kp_ref[:, sl], kc_ref[:, sl]], axis=0)
                    vv = jnp.concatenate([vp_ref[:, sl], vc_ref[:, sl]], axis=0)
                lse_col = _head_columns(lse_ref, i)
                dl_col = _head_columns(dl_ref, i)
                sc = lax.dot_general(qs, kk, (((1,), (1,)), ((), ())), preferred_element_type=F32)
                p = jnp.exp(sc + bias_ref[first, i] - lse_col)
                dp = lax.dot_general(dos, vv, (((1,), (1,)), ((), ())), preferred_element_type=F32)
                ds_f32 = p * (dp - dl_col)
                ds = ds_f32.astype(BF16)
                dq = jnp.dot(ds, kk, preferred_element_type=F32)
                dkk = lax.dot_general(ds, qs, (((0,), (0,)), ((), ())), preferred_element_type=F32)
                dvv = lax.dot_general(p.astype(BF16), dos, (((0,), (0,)), ((), ())),
                                      preferred_element_type=F32)
                if grouped:
                    for half in (0, 1):
                        contrib = jnp.sum(ds_f32[half * BLOCK:(half + 1) * BLOCK, 0:1], axis=0, keepdims=True)
                        dsink = jnp.where(lane1 == 2 * i + half, dsink + contrib, dsink)
                dq_ref[:, sl] = (_unstack_heads(dq, low) * ATT_SCALE).astype(dq_ref.dtype)
                if grouped:
                    dk_grp[grp] = dk_grp[grp] + dkk
                    dv_grp[grp] = dv_grp[grp] + dvv
                else:
                    dk_ref[:, sl] = (acck_ref[:, sl] + dkk[0:BLOCK]).astype(dk_ref.dtype)
                    acck_ref[:, sl] = dkk[BLOCK:2 * BLOCK]
                    dv_ref[:, sl] = (accv_ref[:, sl] + dvv[0:BLOCK]).astype(dv_ref.dtype)
                    accv_ref[:, sl] = dvv[BLOCK:2 * BLOCK]
            if grouped:
                folded = [_without_sink_row(t + _swap_halves(t)) for t in dk_grp + dv_grp]
                dk_tile = jnp.where(low_kv, folded[0], folded[1])
                dv_tile = jnp.where(low_kv, folded[2], folded[3])
                part = jnp.concatenate([dk_tile, dv_tile], axis=1)
                dkv_ref[...] = acc_ref[...] + part[0:BLOCK]
                acc_ref[...] = part[BLOCK:2 * BLOCK]
                dsink_ref[...] += dsink

    last = nb - 1

    def cur(width):
        return pl.BlockSpec((None, BLOCK, width), lambda r, n: (r, jnp.minimum(n, last), 0))

    def prev(width):
        return pl.BlockSpec((None, BLOCK, width),
                            lambda r, n: (r, jnp.maximum(jnp.minimum(n, last) - 1, 0), 0))

    def done(width):
        return pl.BlockSpec((None, BLOCK, width), lambda r, n: (r, jnp.maximum(n - 1, 0), 0))

    if grouped:
        assert pairs_per_group * N_KV_GROUPS == n_pairs and heads_per_group % 2 == 0
        kvw = kv.shape[2]
        operands = [sinks, q, kv, kv, do, lse, delta]
        in_specs = [SMEM_SPEC, cur(w), prev(kvw), cur(kvw), cur(w), cur(128), cur(128)]
        out_specs = [cur(w), done(kvw), pl.BlockSpec((1, 128), lambda r, n: (0, 0))]
        out_shape = [jax.ShapeDtypeStruct((dil, length, w), F32), jax.ShapeDtypeStruct((dil, length, kvw), F32),
                     jax.ShapeDtypeStruct((1, 128), F32)]
        scratch = [pltpu.VMEM((BLOCK, kvw), F32), _bias_shape(n_pairs)]
    else:
        operands = [q, kv[0], kv[0], kv[1], kv[1], do, lse, delta]
        in_specs = [cur(w), prev(w), cur(w), prev(w), cur(w), cur(w), cur(128), cur(128)]
        out_specs = [cur(w), done(w), done(w)]
        out_shape = [jax.ShapeDtypeStruct((dil, length, w), BRANCH_DTYPE)] * 3
        scratch = [pltpu.VMEM((BLOCK, w), F32), pltpu.VMEM((BLOCK, w), F32), _bias_shape(n_pairs)]
    res, hook_res = _call(
        body, hook, name=name, grid=(dil, nb + 1), in_specs=in_specs, out_specs=out_specs,
        out_shape=out_shape, scratch_shapes=scratch,
        compiler_params=_params(("arbitrary", "arbitrary"), BLOCK * w * 32 + n_pairs * BLOCK * BLOCK * 16),
    )(*operands)
    return res if hook is None else (res, hook_res)


def _adamw(name, w, g, m, v):
    rows, cols = w.shape
    tm = 256 if rows % 256 == 0 else rows

    def body(w_ref, g_ref, m_ref, v_ref, d_ref, nm_ref, nv_ref, g_out_ref):
        gv = g_ref[...]
        mn = ADAM_B1 * m_ref[...] + (1.0 - ADAM_B1) * gv
        vn = ADAM_B2 * v_ref[...] + (1.0 - ADAM_B2) * (gv * gv)
        m_hat = mn / (1.0 - ADAM_B1 ** ADAM_STEP)
        v_hat = vn / (1.0 - ADAM_B2 ** ADAM_STEP)
        d_ref[...] = -ADAM_LR * (m_hat / (jnp.sqrt(v_hat) + ADAM_EPS) + ADAM_WD * w_ref[...])
        nm_ref[...] = mn
        nv_ref[...] = vn
        g_out_ref[...] = gv

    spec = pl.BlockSpec((tm, cols), lambda i: (i, 0))
    return _call(
        body, None, name=name, grid=(rows // tm,), in_specs=[spec] * 4, out_specs=[spec] * 4,
        out_shape=[jax.ShapeDtypeStruct(w.shape, F32)] * 4,
        compiler_params=_params(("parallel",), tm * cols * 32),
    )(w, g, m, v)[0]


def _mesh_position():
    return lax.axis_index("x"), lax.axis_index("y"), lax.axis_index("c")


def _other_chips(x, y):
    return [(1 - x, y), (x, 1 - y), (1 - x, 1 - y)]


def _gather_hook(gathered, lo, hi):
    rows, cols = gathered.shape[0] // N_CHIPS, gathered.shape[1]
    half, n = rows // 2, hi - lo
    assert lo % 16 == 0 and n % 16 == 0 and half % 16 == 0

    def region(out, owner_chip, which_half):
        return out.at[pl.ds(pl.multiple_of(owner_chip * rows + which_half * half + lo, 16), n)]

    def parts(outs, sems):
        x, y, c = _mesh_position()
        return outs[0], sems, c, 2 * x + y, (x, y, 1 - c), _other_chips(x, y)

    def start(ops, outs, sems):
        out, (send, recv, fsend, frecv), c, chip, sibling, others = parts(outs, sems)
        mine = region(out, chip, c)
        for k, (px, py) in enumerate(others):
            _remote(mine, mine, send.at[k], recv.at[k], (px, py, c)).start()

    def mid(ops, outs, sems):
        out, (send, recv, fsend, frecv), c, chip, sibling, others = parts(outs, sems)
        for k, (px, py) in enumerate(others):
            landed = region(out, 2 * px + py, c)
            _remote(landed, landed, send.at[k], recv.at[k], (px, py, c)).wait_recv()
            _remote(landed, landed, fsend.at[k], frecv.at[k], sibling).start()

    def finish(ops, outs, sems):
        out, (send, recv, fsend, frecv), c, chip, sibling, others = parts(outs, sems)
        mine = region(out, chip, c)
        for k, (px, py) in enumerate(others):
            passed = region(out, 2 * px + py, 1 - c)
            _remote(passed, passed, fsend.at[k], frecv.at[k], sibling).wait_recv()
        for k, (px, py) in enumerate(others):
            landed = region(out, 2 * px + py, c)
            _remote(landed, landed, fsend.at[k], frecv.at[k], sibling).wait_send()
            _remote(mine, mine, send.at[k], recv.at[k], (px, py, c)).wait_send()

    return _Hook([gathered], [jax.ShapeDtypeStruct(gathered.shape, gathered.dtype)],
                 [pltpu.SemaphoreType.DMA((3,))] * 4, start, finish, mid, aliases={0: 0})


def _own_shard_in_place(name, shard, chip):
    rows, cols = shard.shape
    tr = next(t for t in (544, 512, 320, 256, 128, 64, 32, 16) if rows % t == 0)

    def body(chip_ref, w_ref, o_ref):
        o_ref[...] = w_ref[...].astype(BF16)

    return pl.pallas_call(
        body, name=name,
        grid_spec=pltpu.PrefetchScalarGridSpec(
            num_scalar_prefetch=1, grid=(rows // tr,),
            in_specs=[pl.BlockSpec((tr, cols), lambda i, chip_ref: (i, 0))],
            out_specs=pl.BlockSpec((tr, cols), lambda i, chip_ref: (chip_ref[0] * (rows // tr) + i, 0))),
        out_shape=jax.ShapeDtypeStruct((N_CHIPS * rows, cols), BF16),
        compiler_params=_params(("parallel",), tr * cols * 6),
    )(chip, shard)


def _exchange_hook(grad):
    rows, cols = grad.shape[0] // N_CHIPS, grad.shape[1]
    half = rows // 2
    assert half % 16 == 0

    def copies(ops, outs, sems):
        x, y, c = _mesh_position()
        send, recv = sems
        return [_remote(ops[0].at[pl.ds(pl.multiple_of(k * rows + (1 - c) * half, 16), half)], outs[0].at[k],
                        send.at[k], recv.at[k], (x, y, 1 - c)) for k in range(N_CHIPS)]

    def start(ops, outs, sems):
        for cp in copies(ops, outs, sems):
            cp.start()

    def finish(ops, outs, sems):
        for cp in copies(ops, outs, sems):
            cp.wait_recv()
            cp.wait_send()

    return _Hook([grad], [jax.ShapeDtypeStruct((N_CHIPS, half, cols), grad.dtype)],
                 [pltpu.SemaphoreType.DMA((N_CHIPS,))] * 2, start, finish)


def _scatter_hook(chip_sum):
    _, half, cols = chip_sum.shape

    def copies(ops, outs, sems):
        x, y, c = _mesh_position()
        send, recv = sems
        return [_remote(ops[0].at[2 * px + py], outs[0].at[k], send.at[k], recv.at[k], (px, py, c))
                for k, (px, py) in enumerate(_other_chips(x, y))]

    def start(ops, outs, sems):
        for cp in copies(ops, outs, sems):
            cp.start()

    def finish(ops, outs, sems):
        for cp in copies(ops, outs, sems):
            cp.wait_recv()
            cp.wait_send()

    return _Hook([chip_sum], [jax.ShapeDtypeStruct((3, half, cols), chip_sum.dtype)],
                 [pltpu.SemaphoreType.DMA((3,))] * 2, start, finish)


def _sum_tile(half):
    return 256 if half % 256 == 0 else half


def _chip_add(name, grad, from_sibling, core):
    n_chips, half, cols = from_sibling.shape
    rows = 2 * half
    tr = _sum_tile(half)

    def body(core_ref, g_ref, s_ref, o_ref):
        o_ref[...] = (g_ref[...].astype(F32) + s_ref[...].astype(F32)).astype(o_ref.dtype)

    tile = pl.BlockSpec((None, tr, cols), lambda k, i, core_ref: (k, i, 0))
    return pl.pallas_call(
        body, name=name,
        grid_spec=pltpu.PrefetchScalarGridSpec(
            num_scalar_prefetch=1, grid=(n_chips, half // tr),
            in_specs=[pl.BlockSpec((tr, cols), lambda k, i, core_ref:
                                   (k * (rows // tr) + core_ref[0] * (half // tr) + i, 0)), tile],
            out_specs=tile),
        out_shape=jax.ShapeDtypeStruct(from_sibling.shape, from_sibling.dtype),
        compiler_params=_params(("parallel", "parallel"), 3 * tr * cols * 4),
    )(core, grad, from_sibling)


def _final_add(name, chip_sum, from_chips, chip, core):
    _, half, cols = chip_sum.shape
    tr = _sum_tile(half)

    def body(chip_ref, core_ref, own_ref, others_ref, o_ref):
        total = own_ref[...].astype(F32)
        for k in range(3):
            total = total + others_ref[k].astype(F32)
        o_ref[...] = total

    return pl.pallas_call(
        body, name=name,
        grid_spec=pltpu.PrefetchScalarGridSpec(
            num_scalar_prefetch=2, grid=(half // tr,),
            in_specs=[pl.BlockSpec((None, tr, cols), lambda i, chip_ref, core_ref: (chip_ref[0], i, 0)),
                      pl.BlockSpec((3, tr, cols), lambda i, chip_ref, core_ref: (0, i, 0))],
            out_specs=pl.BlockSpec((tr, cols),
                                   lambda i, chip_ref, core_ref: (core_ref[0] * (half // tr) + i, 0))),
        out_shape=jax.ShapeDtypeStruct((2 * half, cols), F32),
        compiler_params=_params(("parallel",), 6 * tr * cols * 4),
    )(chip, core, chip_sum, from_chips)


def _share_halves(shards, small):
    n_s = len(shards)
    rows_s = small.shape[0]

    def body(*refs):
        small_ref = refs[n_s]
        outs, small_out = refs[n_s + 1:2 * n_s + 1], refs[2 * n_s + 1]
        small_all, send, recv, small_send, small_recv = refs[2 * n_s + 2:]
        x, y, c = _mesh_position()
        me = 4 * x + 2 * y + c
        sibling = (x, y, 1 - c)
        pending = []
        for i in range(n_s):
            half = shards[i].shape[0] // 2
            mine = outs[i].at[pl.ds(pl.multiple_of(c * half, 16), half)]
            cp = _remote(mine, mine, send.at[i], recv.at[i], sibling)
            cp.start()
            pending.append(cp)
        small_all[me] = small_ref[...]
        for j in range(N_DEV - 1):
            peer = (me + 1 + j) % N_DEV
            cp = _remote(small_all.at[me], small_all.at[me], small_send.at[j], small_recv.at[j],
                         (peer // 4, (peer // 2) % 2, peer % 2))
            cp.start()
            pending.append(cp)
        for i in range(n_s):
            half = shards[i].shape[0] // 2
            theirs = outs[i].at[pl.ds(pl.multiple_of((1 - c) * half, 16), half)]
            _remote(theirs, theirs, send.at[i], recv.at[i], sibling).wait_recv()
        for j in range(N_DEV - 1):
            peer = (me + N_DEV - 1 - j) % N_DEV
            _remote(small_all.at[peer], small_all.at[peer], small_send.at[j], small_recv.at[j],
                    sibling).wait_recv()
        total = small_all[0]
        for dev in range(1, N_DEV):
            total = total + small_all[dev]
        small_out[...] = total
        for cp in pending:
            cp.wait_send()

    res = pl.pallas_call(
        body, name="share_halves",
        in_specs=[HBM_SPEC] * n_s + [VMEM_SPEC], out_specs=[HBM_SPEC] * n_s + [VMEM_SPEC],
        out_shape=[jax.ShapeDtypeStruct(sh.shape, sh.dtype) for sh in shards]
        + [jax.ShapeDtypeStruct((rows_s, 128), F32)],
        scratch_shapes=[pltpu.VMEM((N_DEV, rows_s, 128), F32),
                        pltpu.SemaphoreType.DMA((n_s,)), pltpu.SemaphoreType.DMA((n_s,)),
                        pltpu.SemaphoreType.DMA((N_DEV - 1,)), pltpu.SemaphoreType.DMA((N_DEV - 1,))],
        input_output_aliases={i: i for i in range(n_s)},
    )(*shards, small)
    return res[:n_s], res[n_s]


def _pack_small(parts, rows):
    flat = jnp.concatenate([p.reshape(-1) for p in parts])
    flat = jnp.pad(flat, (0, rows * 128 - flat.shape[0]))
    return flat.reshape(rows, 128)


def _unpack_small(packed, shapes):
    flat = packed.reshape(-1)
    out, off = [], 0
    for shp in shapes:
        n = int(np.prod(shp))
        out.append(flat[off:off + n].reshape(shp))
        off += n
    return out


def kernel(x, g_attn, w_in, b_in, sinks_a, g_out_a, g_out_b, w_out, g_mlp, w_1, w_2, g_final, loss_target, m_g_attn, m_w_in, m_b_in, m_sinks_a, m_g_out_a, m_g_out_b, m_w_out, m_g_mlp, m_w_1, m_w_2, m_g_final, v_g_attn, v_w_in, v_b_in, v_sinks_a, v_g_out_a, v_g_out_b, v_w_out, v_g_mlp, v_w_1, v_w_2, v_g_final):
    s, d = x.shape[1], x.shape[2]
    d_in = b_in.shape[1]
    qa = g_out_a.shape[1]
    qb = g_out_b.shape[1]
    kva = 2 * N_KV_GROUPS * HEAD_DIM
    assert d_in == qa + kva + 3 * qb and qa + qb == w_out.shape[1] * N_CHIPS
    d_ff = w_1.shape[2] * N_CHIPS
    ff_shard = w_1.shape[2]
    n_heads_a, n_heads_b = qa // HEAD_DIM, qb // HEAD_DIM
    slopes_a, slopes_b = alibi_slopes(n_heads_a), alibi_slopes(n_heads_b)

    x2d = x[0]
    target = loss_target[0]

    core_index = lax.axis_index("c").astype(jnp.int32).reshape(1)
    chip_index = (2 * lax.axis_index("x") + lax.axis_index("y")).astype(jnp.int32).reshape(1)
    shards = {"w_in": w_in[0].T, "w_out": w_out[0], "w_1": w_1[0], "w_2": w_2[0]}
    halves = [sh.shape[0] // 2 for sh in shards.values()]
    w_in_t, w_out_g, w_1_g, w_2_g = [_own_shard_in_place(f"place_{n}", sh, chip_index)
                                     for n, sh in shards.items()]

    tm = _tile(s, 1024)

    (h1, r1), (w_in_t,) = _norm_fwd("norm_attn", x2d, g_attn, hook=_gather_hook(w_in_t, 0, halves[0]))

    q_a, = _project_by_class("proj_qa", h1, w_in_t, b_in, 0, qa, (1,))
    kv_a, = _project_by_class("proj_kva", h1, w_in_t, b_in, qa, kva, (1,))
    q_bs, (w_out_g,) = _project_by_class("proj_qb", h1, w_in_t, b_in, qa + kva, qb, DILATIONS,
                                         hook=_gather_hook(w_out_g, 0, halves[1]))
    k_bs = _project_by_class("proj_kb", h1, w_in_t, b_in, qa + kva + qb, qb, DILATIONS)
    v_bs = _project_by_class("proj_vb", h1, w_in_t, b_in, qa + kva + 2 * qb, qb, DILATIONS)

    quarter = halves[2] // 4
    sinks = sinks_a.reshape(-1)
    (o_a, lse_a), (w_1_g,) = _attn_fwd("attn_a_fwd", q_a, kv_a, dil=1, max_steps=WINDOW_A - 1, slopes=slopes_a,
                                       sinks=sinks, hook=_gather_hook(w_1_g, 0, quarter))
    o_a = o_a[0]
    o_bs, lse_bs = [], []
    for n, (window, dil) in enumerate(DILATED_BRANCHES):
        (o, l), (w_1_g,) = _attn_fwd(f"attn_b{dil}_fwd", q_bs[n], (k_bs[n], v_bs[n]), dil=dil,
                                     max_steps=window // dil, slopes=slopes_b,
                                     hook=_gather_hook(w_1_g, (n + 1) * quarter, (n + 2) * quarter))
        o_bs.append(o)
        lse_bs.append(l)
    w_1_g = w_1_g.reshape(N_CHIPS, d, ff_shard)
    mix, o_b, *lse_tot, r_a, r_b = _mix_fwd(o_a, o_bs, lse_bs, g_out_a, g_out_b)

    tn = _tile(d, 512)
    a_spec, b_spec = _mm_specs("nn", tm, tn, d)
    tile_mn = pl.BlockSpec((tm, tn), lambda i, j, k: (i, j))
    x2 = _matmul("out_proj", mix, w_out_g, [x2d], mode="nn", grid=(s // tm, d // tn, 1),
                 a_spec=a_spec, b_spec=b_spec, extra_specs=[tile_mn],
                 out_shapes=[jax.ShapeDtypeStruct((s, d), F32)], out_specs=[tile_mn],
                 epilogue=lambda acc, res: (acc + res,))[0]

    h2, r2 = _norm_fwd("norm_mlp", x2, g_mlp)

    tn = _tile(ff_shard, 1024)
    per = ff_shard // tn
    a_spec, _ = _mm_specs("nn", tm, tn, d)
    tile_mn = pl.BlockSpec((tm, tn), lambda i, j, k: (i, j))
    (u,), (w_2_g,) = _matmul(
        "mlp_up", h2, w_1_g, [], mode="nn", grid=(s // tm, d_ff // tn, 1),
        a_spec=a_spec, b_spec=pl.BlockSpec((None, d, tn), lambda i, j, k: (j // per, 0, j % per)),
        extra_specs=[], out_shapes=[jax.ShapeDtypeStruct((s, d_ff), BF16)], out_specs=[tile_mn],
        epilogue=lambda acc: (jnp.maximum(acc, 0.0),),
        hook=_gather_hook(w_2_g, 0, halves[3]))

    tn = _tile(d, 1024)
    tk = _tile(d_ff, 2048)
    a_spec, b_spec = _mm_specs("nn", tm, tn, tk)
    tile_mn = pl.BlockSpec((tm, tn), lambda i, j, k: (i, j))
    x3 = _matmul("mlp_down", u, w_2_g, [x2], mode="nn", grid=(s // tm, d // tn, d_ff // tk),
                 a_spec=a_spec, b_spec=b_spec, extra_specs=[tile_mn],
                 out_shapes=[jax.ShapeDtypeStruct((s, d), F32)], out_specs=[tile_mn],
                 prologue=lambda a: a * a, epilogue=lambda acc, res: (acc + res,), acc_shape=(tm, tn))[0]

    dx3, dx3b, loss_part, dg_final = _loss_head(x3, target, g_final.reshape(1, d))

    tn = _tile(d_ff, 1024)
    a_spec, b_spec = _mm_specs("nt", tm, tn, d)
    tile_mn = pl.BlockSpec((tm, tn), lambda i, j, k: (i, j))
    dpre = _matmul("mlp_down_dx", dx3b, w_2_g, [u], mode="nt", grid=(s // tm, d_ff // tn, 1),
                   a_spec=a_spec, b_spec=b_spec, extra_specs=[tile_mn],
                   out_shapes=[jax.ShapeDtypeStruct((s, d_ff), BF16)], out_specs=[tile_mn],
                   epilogue=lambda acc, uu: (acc * (2.0 * uu.astype(F32)),))[0]

    wire = GRAD_WIRE_DTYPE
    tk_s = _tile(s, 2048)
    tmw = _tile(d_ff, 1024)
    a_spec, b_spec = _mm_specs("tn", tmw, d, tk_s)
    dw_2 = _matmul("mlp_down_dw", u, dx3b, [], mode="tn", grid=(d_ff // tmw, 1, s // tk_s),
                   a_spec=a_spec, b_spec=b_spec, extra_specs=[],
                   out_shapes=[jax.ShapeDtypeStruct((d_ff, d), wire)],
                   out_specs=[pl.BlockSpec((tmw, d), lambda i, j, k: (i, j))],
                   prologue=lambda a: a * a, epilogue=lambda acc: (acc,), acc_shape=(tmw, d))[0]

    tn = _tile(d, 1024)
    tk = _tile(ff_shard, 2048)
    per = ff_shard // tk
    a_spec, _ = _mm_specs("nt", tm, tn, tk)
    tile_mn = pl.BlockSpec((tm, tn), lambda i, j, k: (i, j))
    dh2 = _matmul("mlp_up_dx", dpre, w_1_g, [], mode="nt", grid=(s // tm, d // tn, d_ff // tk),
                  a_spec=a_spec, b_spec=pl.BlockSpec((None, tn, tk), lambda i, j, k: (k // per, j, k % per)),
                  extra_specs=[], out_shapes=[jax.ShapeDtypeStruct((s, d), F32)], out_specs=[tile_mn],
                  epilogue=lambda acc: (acc,), acc_shape=(tm, tn))[0]

    tmw = _tile(d, 1024)
    tnw = _tile(ff_shard, 2048)
    per = ff_shard // tnw
    a_spec, b_spec = _mm_specs("tn", tmw, tnw, tk_s)
    dw_1 = _matmul("mlp_up_dw", h2, dpre, [], mode="tn", grid=(d // tmw, d_ff // tnw, s // tk_s),
                   a_spec=a_spec, b_spec=b_spec, extra_specs=[],
                   out_shapes=[jax.ShapeDtypeStruct((N_CHIPS, d, ff_shard), wire)],
                   out_specs=[pl.BlockSpec((None, tmw, tnw), lambda i, j, k: (j // per, i, j % per))],
                   epilogue=lambda acc: (acc,), acc_shape=(tmw, tnw))[0]

    dw_1 = dw_1.reshape(N_CHIPS * d, ff_shard)
    (dx2, dx2b, dg_mlp), (sib_2, sib_1) = _norm_bwd(
        "norm_mlp_bwd", dh2, x2, r2, g_mlp, dx3, hook=_merge_hooks([_exchange_hook(dw_2), _exchange_hook(dw_1)]))
    chip_sum_2 = _chip_add("chip_add_w_2", dw_2, sib_2, core_index)
    chip_sum_1 = _chip_add("chip_add_w_1", dw_1, sib_1, core_index)

    tn = _tile(d, 512)
    a_spec, b_spec = _mm_specs("nt", tm, tn, d)
    tile_mn = pl.BlockSpec((tm, tn), lambda i, j, k: (i, j))
    dmix = _matmul("out_proj_dx", dx2b, w_out_g, [], mode="nt", grid=(s // tm, d // tn, 1),
                   a_spec=a_spec, b_spec=b_spec, extra_specs=[],
                   out_shapes=[jax.ShapeDtypeStruct((s, d), F32)], out_specs=[tile_mn],
                   epilogue=lambda acc: (acc,))[0]

    tmw = _tile(d, 1024)
    a_spec, b_spec = _mm_specs("tn", tmw, d, tk_s)
    dw_out = _matmul("out_proj_dw", mix, dx2b, [], mode="tn", grid=(d // tmw, 1, s // tk_s),
                     a_spec=a_spec, b_spec=b_spec, extra_specs=[],
                     out_shapes=[jax.ShapeDtypeStruct((d, d), wire)],
                     out_specs=[pl.BlockSpec((tmw, d), lambda i, j, k: (i, j))],
                     epilogue=lambda acc: (acc,), acc_shape=(tmw, d))[0]

    mix_grads, (sib_out,) = _mix_bwd(dmix, o_a, o_b, r_a, r_b, g_out_a, g_out_b, hook=_exchange_hook(dw_out))
    do_a, do_bs, delta_a, delta_bs = mix_grads[0], mix_grads[1:4], mix_grads[4], mix_grads[5:8]
    dg_out_a, dg_out_b = mix_grads[8:]
    chip_sum_out = _chip_add("chip_add_w_out", dw_out, sib_out, core_index)

    (dq_a, dkv_a, dsinks), (chips_2,) = _attn_bwd(
        "attn_a_bwd", q_a, kv_a, do_a[None], lse_a, delta_a[None], dil=1, max_steps=WINDOW_A - 1,
        slopes=slopes_a, sinks=sinks, hook=_scatter_hook(chip_sum_2))
    dqs, dks, dvs = [], [], []
    scatter = {1: chip_sum_1, 4: chip_sum_out}
    arrived = {}
    for n, (window, dil) in enumerate(DILATED_BRANCHES):
        res = _attn_bwd(f"attn_b{dil}_bwd", q_bs[n], (k_bs[n], v_bs[n]), do_bs[n], lse_tot[n],
                        delta_bs[n], dil=dil, max_steps=window // dil, slopes=slopes_b,
                        hook=_scatter_hook(scatter[dil]) if dil in scatter else None)
        if dil in scatter:
            res, (arrived[dil],) = res
        dq, dk, dv = res
        dqs.append(dq)
        dks.append(dk)
        dvs.append(dv)
    half_2 = _final_add("final_add_w_2", chip_sum_2, chips_2, chip_index, core_index)
    half_1 = _final_add("final_add_w_1", chip_sum_1, arrived[1], chip_index, core_index)
    half_out = _final_add("final_add_w_out", chip_sum_out, arrived[4], chip_index, core_index)
    dproj, db_in = _assemble_dproj(dq_a[0], dkv_a[0], dqs, dks, dvs)

    tmw = d_in // 2 if (d_in // 2) % 128 == 0 else d_in
    tnw = _tile(d, 1024)
    tk_s = _tile(s, 1024)
    a_spec, b_spec = _mm_specs("tn", tmw, tnw, tk_s)
    dw_in_t = _matmul("in_proj_dw", dproj, h1, [], mode="tn", grid=(d_in // tmw, d // tnw, s // tk_s),
                      a_spec=a_spec, b_spec=b_spec, extra_specs=[],
                      out_shapes=[jax.ShapeDtypeStruct((d_in, d), wire)],
                      out_specs=[pl.BlockSpec((tmw, tnw), lambda i, j, k: (i, j))],
                      epilogue=lambda acc: (acc,), acc_shape=(tmw, tnw))[0]

    tn = _tile(d, 512)
    a_spec, b_spec = _mm_specs("nn", tm, tn, d_in)
    tile_mn = pl.BlockSpec((tm, tn), lambda i, j, k: (i, j))
    (dh1,), (sib_in,) = _matmul("in_proj_dx", dproj, w_in_t, [], mode="nn", grid=(s // tm, d // tn, 1),
                                a_spec=a_spec, b_spec=b_spec, extra_specs=[],
                                out_shapes=[jax.ShapeDtypeStruct((s, d), F32)], out_specs=[tile_mn],
                                epilogue=lambda acc: (acc,), hook=_exchange_hook(dw_in_t))
    chip_sum_in = _chip_add("chip_add_w_in", dw_in_t, sib_in, core_index)

    (grad_x, _, dg_attn), (chips_in,) = _norm_bwd("norm_attn_bwd", dh1, x2d, r1, g_attn, dx2,
                                                  hook=_scatter_hook(chip_sum_in))
    half_in = _final_add("final_add_w_in", chip_sum_in, chips_in, chip_index, core_index)

    small_parts = [dg_attn, db_in, dsinks[:, :n_heads_a], dg_out_a, dg_out_b, dg_mlp, dg_final]
    small_shapes = [g_attn.shape, b_in.shape, sinks_a.shape, g_out_a.shape, g_out_b.shape, g_mlp.shape,
                    g_final.shape]
    n_small = sum(int(np.prod(shp)) for shp in small_shapes)
    rows_s = -(-n_small // (8 * 128)) * 8
    (gw_in_t, gw_out, gw_1, gw_2), small_sum = _share_halves(
        [half_in, half_out, half_1, half_2], _pack_small(small_parts, rows_s))
    gw_in = gw_in_t.T

    upd_in = _adamw("adamw_w_in", w_in[0], gw_in, m_w_in[0], v_w_in[0])
    upd_out = _adamw("adamw_w_out", w_out[0], gw_out, m_w_out[0], v_w_out[0])
    upd_1 = _adamw("adamw_w_1", w_1[0], gw_1, m_w_1[0], v_w_1[0])
    upd_2 = _adamw("adamw_w_2", w_2[0], gw_2, m_w_2[0], v_w_2[0])
    small_w = [g_attn, b_in, sinks_a, g_out_a, g_out_b, g_mlp, g_final]
    small_m = [m_g_attn, m_b_in, m_sinks_a, m_g_out_a, m_g_out_b, m_g_mlp, m_g_final]
    small_v = [v_g_attn, v_b_in, v_sinks_a, v_g_out_a, v_g_out_b, v_g_mlp, v_g_final]
    upd_small = _adamw("adamw_small", _pack_small(small_w, rows_s), small_sum,
                       _pack_small(small_m, rows_s), _pack_small(small_v, rows_s))
    d_small, m_small, v_small, g_small = [_unpack_small(t, small_shapes) for t in upd_small]

    loss = lax.psum(loss_part[0, 0], ("x", "y", "c"))

    def ordered(small, big):
        w_in_v, w_out_v, w_1_v, w_2_v = big
        return [small[0], w_in_v[None], small[1], small[2], small[3], small[4], w_out_v[None], small[5],
                w_1_v[None], w_2_v[None], small[6]]

    grads = ordered(g_small, (upd_in[3], upd_out[3], upd_1[3], upd_2[3]))
    deltas = ordered(d_small, (upd_in[0], upd_out[0], upd_1[0], upd_2[0]))
    new_m = ordered(m_small, (upd_in[1], upd_out[1], upd_1[1], upd_2[1]))
    new_v = ordered(v_small, (upd_in[2], upd_out[2], upd_1[2], upd_2[2]))
    return (loss, grad_x[None], *grads, *deltas, *new_m, *new_v)
```

```python
import jax
import jax.numpy as jnp
import numpy as np
from jax import lax
from jax.experimental import pallas as pl
from jax.experimental.pallas import tpu as pltpu

F32 = jnp.float32
BF16 = jnp.bfloat16

HEAD_DIM = 64
BLOCK = 128
PAIR = 2 * HEAD_DIM
N_KV_GROUPS = 2
WINDOW_A = 128
DILATED_BRANCHES = ((128, 1), (512, 4), (2048, 16))
EPS = 1e-5
NEG_INF = -1e30
ATT_SCALE = HEAD_DIM ** -0.5

ADAM_LR = 0.001
ADAM_B1 = 0.9
ADAM_B2 = 0.999
ADAM_EPS = 1e-08
ADAM_WD = 0.01
ADAM_STEP = 10

N_CHIPS = 4
N_DEV = 8
MESH = pl.DeviceIdType.MESH
GRAD_WIRE_DTYPE = jnp.bfloat16
BRANCH_DTYPE = jnp.bfloat16

VMEM_CAPACITY_V7X = 64 * 1024 * 1024
VMEM_LIMIT_MAX = VMEM_CAPACITY_V7X - 8 * 1024 * 1024
VMEM_LIMIT_MIN = VMEM_CAPACITY_V7X - 16 * 1024 * 1024

HBM_SPEC = pl.BlockSpec(memory_space=pltpu.HBM)
VMEM_SPEC = pl.BlockSpec(memory_space=pltpu.VMEM)
SMEM_SPEC = pl.BlockSpec(memory_space=pltpu.SMEM)


def _nbytes(shape, dtype):
    return int(np.prod([s for s in shape if s is not None])) * jnp.dtype(dtype).itemsize


def _params(semantics, block_bytes):
    limit = min(max(2 * block_bytes + (4 << 20), VMEM_LIMIT_MIN), VMEM_LIMIT_MAX)
    return pltpu.CompilerParams(dimension_semantics=semantics, vmem_limit_bytes=limit)


class _Hook:
    def __init__(self, operands, out_shape, sems, start, finish, mid=None, aliases=None):
        self.operands, self.out_shape, self.sems = list(operands), list(out_shape), list(sems)
        self.start, self.mid, self.finish = start, mid, finish
        self.aliases = dict(aliases or {})


def _merge_hooks(hooks):
    hooks = [h for h in hooks if h is not None]
    if len(hooks) <= 1:
        return hooks[0] if hooks else None
    n_op = np.cumsum([0] + [len(h.operands) for h in hooks])
    n_out = np.cumsum([0] + [len(h.out_shape) for h in hooks])
    n_sem = np.cumsum([0] + [len(h.sems) for h in hooks])

    def run(which):
        def fn(ops, outs, sems):
            for i, h in enumerate(hooks):
                f = getattr(h, which)
                if f is not None:
                    f(ops[n_op[i]:n_op[i + 1]], outs[n_out[i]:n_out[i + 1]], sems[n_sem[i]:n_sem[i + 1]])
        return fn

    aliases = {}
    for i, h in enumerate(hooks):
        aliases.update({int(n_op[i]) + a: int(n_out[i]) + b for a, b in h.aliases.items()})
    return _Hook(sum([h.operands for h in hooks], []), sum([h.out_shape for h in hooks], []),
                 sum([h.sems for h in hooks], []), run("start"), run("finish"),
                 run("mid") if any(h.mid for h in hooks) else None, aliases)


HOOK_MID_FRACTION = 0.6


def _call(body, hook, *, name, grid, in_specs, out_specs, out_shape, scratch_shapes=(), compiler_params):
    in_specs, out_specs, out_shape = list(in_specs), list(out_specs), list(out_shape)
    scratch_shapes = list(scratch_shapes)
    if hook is None:
        call = pl.pallas_call(body, name=name, grid=grid, in_specs=in_specs, out_specs=out_specs,
                              out_shape=out_shape, scratch_shapes=scratch_shapes,
                              compiler_params=compiler_params)
        return lambda *operands: (call(*operands), [])
    n_in, n_hin, n_out, n_hout, n_scr = (len(in_specs), len(hook.operands), len(out_specs),
                                         len(hook.out_shape), len(scratch_shapes))
    total = int(np.prod(grid))
    t_mid = min(int(total * HOOK_MID_FRACTION), total - 1)

    def wrapped(*refs):
        ins, h_in = refs[:n_in], refs[n_in:n_in + n_hin]
        o0 = n_in + n_hin
        outs, h_out = refs[o0:o0 + n_out], refs[o0 + n_out:o0 + n_out + n_hout]
        s0 = o0 + n_out + n_hout
        scr, h_sems = refs[s0:s0 + n_scr], refs[s0 + n_scr:]
        t = pl.program_id(0)
        for axis in range(1, len(grid)):
            t = t * grid[axis] + pl.program_id(axis)

        @pl.when(t == 0)
        def _():
            hook.start(h_in, h_out, h_sems)

        body(*ins, *outs, *scr)
        if hook.mid is not None:
            @pl.when(t == t_mid)
            def _():
                hook.mid(h_in, h_out, h_sems)

        @pl.when(t == total - 1)
        def _():
            hook.finish(h_in, h_out, h_sems)

    params = pltpu.CompilerParams(dimension_semantics=("arbitrary",) * len(grid),
                                  vmem_limit_bytes=compiler_params.vmem_limit_bytes)
    call = pl.pallas_call(
        wrapped, name=name, grid=grid,
        in_specs=in_specs + [HBM_SPEC] * n_hin, out_specs=out_specs + [HBM_SPEC] * n_hout,
        out_shape=out_shape + hook.out_shape, scratch_shapes=scratch_shapes + hook.sems,
        input_output_aliases={n_in + a: n_out + b for a, b in hook.aliases.items()},
        compiler_params=params)

    def run(*operands):
        res = call(*operands, *hook.operands)
        return res[:n_out], res[n_out:]

    return run


def _remote(src, dst, send_sem, recv_sem, device):
    return pltpu.make_async_remote_copy(src_ref=src, dst_ref=dst, send_sem=send_sem, recv_sem=recv_sem,
                                        device_id=device, device_id_type=MESH)


def alibi_slopes(n):
    return [float(v) for v in np.asarray(2.0 ** (-8.0 * (np.arange(n) + 1) / n), dtype=np.float32)]


def _matmul(name, a, b, extras, *, mode, grid, a_spec, b_spec, extra_specs, out_shapes, out_specs,
            epilogue, prologue=None, acc_shape=None, hook=None):
    dims = {"nn": ((1,), (0,)), "nt": ((1,), (1,)), "tn": ((0,), (0,))}[mode]
    nk = grid[2]
    n_ex, n_out = len(extras), len(out_shapes)

    def body(a_ref, b_ref, *rest):
        ex, outs = rest[:n_ex], rest[n_ex:n_ex + n_out]
        av = a_ref[...]
        if prologue is not None:
            av = prologue(av)
        part = lax.dot_general(av, b_ref[...], (dims, ((), ())), preferred_element_type=F32)

        def finish(acc):
            res = epilogue(acc, *[e[...] for e in ex])
            for o, r in zip(outs, res):
                o[...] = r.astype(o.dtype)

        if nk == 1:
            finish(part)
        else:
            acc_ref = rest[-1]
            k = pl.program_id(2)

            @pl.when(k == 0)
            def _():
                acc_ref[...] = part

            @pl.when(k > 0)
            def _():
                acc_ref[...] += part

            @pl.when(k == nk - 1)
            def _():
                finish(acc_ref[...])

    blocks = [(a_spec.block_shape, a.dtype), (b_spec.block_shape, b.dtype)]
    blocks += [(s.block_shape, e.dtype) for s, e in zip(extra_specs, extras)]
    blocks += [(s.block_shape, o.dtype) for s, o in zip(out_specs, out_shapes)]
    nbytes = sum(_nbytes(s, d) for s, d in blocks)
    scratch = []
    if nk > 1:
        scratch.append(pltpu.VMEM(acc_shape, F32))
        nbytes += _nbytes(acc_shape, F32)
    res, hook_res = _call(
        body, hook, name=name, grid=grid,
        in_specs=[a_spec, b_spec, *extra_specs], out_specs=list(out_specs), out_shape=list(out_shapes),
        scratch_shapes=scratch,
        compiler_params=_params(("parallel", "parallel", "arbitrary"), nbytes),
    )(a, b, *extras)
    return res if hook is None else (res, hook_res)


def _mm_specs(mode, tm, tn, tk, b_block=None, b_map=None):
    if mode == "tn":
        a_spec = pl.BlockSpec((tk, tm), lambda i, j, k: (k, i))
    else:
        a_spec = pl.BlockSpec((tm, tk), lambda i, j, k: (i, k))
    if b_block is not None:
        b_spec = pl.BlockSpec(b_block, b_map)
    elif mode == "nt":
        b_spec = pl.BlockSpec((tn, tk), lambda i, j, k: (j, k))
    else:
        b_spec = pl.BlockSpec((tk, tn), lambda i, j, k: (k, j))
    return a_spec, b_spec


def _project_by_class(name, h, w_t, bias, row_off, width, dilations, hook=None):
    s, d = h.shape
    tm = _tile(s, 2048)
    tn = 512 if width % 512 == 0 and row_off % 512 == 0 else _tile(width, 256)
    off = row_off // tn
    assert row_off % tn == 0 and tn % 128 == 0
    n_out = len(dilations)

    def body(h_ref, w_ref, b_ref, *rest):
        outs, perm_ref = rest[:n_out], rest[n_out]
        acc = lax.dot_general(h_ref[...], w_ref[...], (((1,), (1,)), ((), ())), preferred_element_type=F32)
        acc = acc + b_ref[...]
        for j in range(tn // 128):
            cols = slice(j * 128, (j + 1) * 128)
            for o_ref, dil in zip(outs, dilations):
                _to_classes(o_ref, cols, acc[:, cols], perm_ref, dil)

    blocks = tm * d * 2 + tn * d * 2 + 3 * tm * tn * 2 + tm * 128 * 4
    res, hook_res = _call(
        body, hook, name=name, grid=(s // tm, width // tn),
        in_specs=[pl.BlockSpec((tm, d), lambda i, j: (i, 0)), pl.BlockSpec((tn, d), lambda i, j: (j + off, 0)),
                  pl.BlockSpec((1, tn), lambda i, j: (0, j + off))],
        out_specs=[pl.BlockSpec((dil, tm // dil, tn), lambda i, j: (0, i, j)) for dil in dilations],
        out_shape=[_class_shape(dil, s, width, BF16) for dil in dilations],
        scratch_shapes=[pltpu.VMEM((tm, 128), F32)],
        compiler_params=_params(("parallel", "parallel"), blocks),
    )(h, w_t, bias)
    return res if hook is None else (res, hook_res)


def _tile(n, want):
    if n <= want:
        return n
    t = (want // 128) * 128
    while t > 128 and n % t:
        t -= 128
    assert n % t == 0, (n, want)
    return t


def _row_tile(s):
    return 256 if s % 256 == 0 else s


def _norm_fwd(name, x, g, hook=None):
    s, d = x.shape
    tm = _row_tile(s)

    def body(x_ref, g_ref, h_ref, r_ref):
        xv = x_ref[...]
        r = lax.rsqrt(jnp.mean(xv * xv, axis=-1, keepdims=True) + EPS)
        h_ref[...] = ((xv * r) * g_ref[...]).astype(BF16)
        r_ref[...] = r

    row = pl.BlockSpec((tm, d), lambda i: (i, 0))
    res, hook_res = _call(
        body, hook, name=name, grid=(s // tm,),
        in_specs=[row, pl.BlockSpec((1, d), lambda i: (0, 0))],
        out_specs=[row, pl.BlockSpec((tm, 1), lambda i: (i, 0))],
        out_shape=[jax.ShapeDtypeStruct((s, d), BF16), jax.ShapeDtypeStruct((s, 1), F32)],
        compiler_params=_params(("parallel",), tm * d * 6),
    )(x, g)
    return res if hook is None else (res, hook_res)


def _norm_bwd(name, dh, x, r, g, dres, hook=None):
    s, d = x.shape
    tm = _row_tile(s)

    def body(dh_ref, x_ref, r_ref, g_ref, dres_ref, dx_ref, dxb_ref, dg_ref):
        rv = r_ref[...]
        xn = x_ref[...] * rv
        dhv = dh_ref[...]
        dxn = dhv * g_ref[...]
        dx = dres_ref[...] + rv * (dxn - xn * jnp.mean(dxn * xn, axis=-1, keepdims=True))
        dx_ref[...] = dx
        dxb_ref[...] = dx.astype(BF16)
        part = jnp.sum(dhv * xn, axis=0, keepdims=True)

        @pl.when(pl.program_id(0) == 0)
        def _():
            dg_ref[...] = part

        @pl.when(pl.program_id(0) > 0)
        def _():
            dg_ref[...] += part

    row = pl.BlockSpec((tm, d), lambda i: (i, 0))
    vec = pl.BlockSpec((1, d), lambda i: (0, 0))
    res, hook_res = _call(
        body, hook, name=name, grid=(s // tm,),
        in_specs=[row, row, pl.BlockSpec((tm, 1), lambda i: (i, 0)), vec, row],
        out_specs=[row, row, vec],
        out_shape=[jax.ShapeDtypeStruct((s, d), F32), jax.ShapeDtypeStruct((s, d), BF16),
                   jax.ShapeDtypeStruct((1, d), F32)],
        compiler_params=_params(("arbitrary",), tm * d * 18),
    )(dh, x, r, g, dres)
    return res if hook is None else (res, hook_res)


def _loss_head(x3, target, g):
    s, d = x3.shape
    tm = _row_tile(s)

    def body(x_ref, t_ref, g_ref, dx_ref, dxb_ref, loss_ref, dg_ref):
        xv = x_ref[...]
        gv = g_ref[...]
        r = lax.rsqrt(jnp.mean(xv * xv, axis=-1, keepdims=True) + EPS)
        xn = xv * r
        err = xn * gv - t_ref[...]
        loss = 0.5 * jnp.sum(jnp.mean(err * err, axis=-1, keepdims=True), axis=0, keepdims=True)
        dy = err / d
        dxn = dy * gv
        dx = r * (dxn - xn * jnp.mean(dxn * xn, axis=-1, keepdims=True))
        dx_ref[...] = dx
        dxb_ref[...] = dx.astype(BF16)
        dg = jnp.sum(dy * xn, axis=0, keepdims=True)
        loss_row = jnp.broadcast_to(loss, (1, 128))

        @pl.when(pl.program_id(0) == 0)
        def _():
            dg_ref[...] = dg
            loss_ref[...] = loss_row

        @pl.when(pl.program_id(0) > 0)
        def _():
            dg_ref[...] += dg
            loss_ref[...] += loss_row

    row = pl.BlockSpec((tm, d), lambda i: (i, 0))
    vec = pl.BlockSpec((1, d), lambda i: (0, 0))
    return _call(
        body, None, name="loss_head", grid=(s // tm,),
        in_specs=[row, row, vec],
        out_specs=[row, row, pl.BlockSpec((1, 128), lambda i: (0, 0)), vec],
        out_shape=[jax.ShapeDtypeStruct((s, d), F32), jax.ShapeDtypeStruct((s, d), BF16),
                   jax.ShapeDtypeStruct((1, 128), F32), jax.ShapeDtypeStruct((1, d), F32)],
        compiler_params=_params(("arbitrary",), tm * d * 14),
    )(x3, target, g)[0]


def _low_lanes(rows):
    return lax.broadcasted_iota(jnp.int32, (rows, PAIR), 1) < HEAD_DIM


def _to_classes(dst_ref, cols, value, perm_ref, dil):
    rows = value.shape[0]
    if dil == 1:
        dst_ref[0, :, cols] = value.astype(dst_ref.dtype)
        return
    perm_ref[...] = value
    for r in range(dil):
        dst_ref[r, :, cols] = perm_ref[pl.ds(r, rows // dil, stride=dil), :].astype(dst_ref.dtype)


def _from_classes(src_ref, cols, perm_ref, dil):
    if dil == 1:
        return src_ref[0, :, cols].astype(F32)
    rows = perm_ref.shape[0]
    for r in range(dil):
        perm_ref[pl.ds(r, rows // dil, stride=dil), :] = src_ref[r, :, cols].astype(F32)
    return perm_ref[...]


def _class_spec(dil, tm, width):
    return pl.BlockSpec((dil, tm // dil, width), lambda i: (0, i, 0))


def _class_shape(dil, s, width, dtype):
    return jax.ShapeDtypeStruct((dil, s // dil, width), dtype)


DILATIONS = tuple(d for _, d in DILATED_BRANCHES)


def _mix_fwd(oa, obs, lses, ga, gb):
    s, qa = oa.shape
    qb = obs[0].shape[2]
    tm = _row_tile(s)
    all_lanes = slice(0, 128)

    def body(oa_ref, o1_ref, o2_ref, o3_ref, l1_ref, l2_ref, l3_ref, ga_ref, gb_ref,
             mix_ref, ob_ref, t1_ref, t2_ref, t3_ref, ra_ref, rb_ref, perm_ref):
        oav = oa_ref[...]
        ra = lax.rsqrt(jnp.mean(oav * oav, axis=-1, keepdims=True) + EPS)
        ra_ref[...] = ra
        mix_ref[:, 0:qa] = ((oav * ra) * ga_ref[...]).astype(BF16)
        l1, l2, l3 = [_from_classes(l_ref, all_lanes, perm_ref, dil)
                      for l_ref, dil in zip((l1_ref, l2_ref, l3_ref), DILATIONS)]
        mx = jnp.maximum(jnp.maximum(l1, l2), l3)
        e1, e2, e3 = jnp.exp(l1 - mx), jnp.exp(l2 - mx), jnp.exp(l3 - mx)
        tot = e1 + e2 + e3
        lse = mx + jnp.log(tot)
        for t_ref, dil in zip((t1_ref, t2_ref, t3_ref), DILATIONS):
            _to_classes(t_ref, all_lanes, lse, perm_ref, dil)
        ws = (e1 / tot, e2 / tot, e3 / tot)
        low = _low_lanes(tm)
        ssq = jnp.zeros((tm, 1), F32)
        for i in range(qb // PAIR):
            sl = slice(i * PAIR, (i + 1) * PAIR)
            acc = jnp.zeros((tm, PAIR), F32)
            for w, o_ref, dil in zip(ws, (o1_ref, o2_ref, o3_ref), DILATIONS):
                wexp = jnp.where(low, w[:, 2 * i:2 * i + 1], w[:, 2 * i + 1:2 * i + 2])
                acc = acc + wexp * _from_classes(o_ref, sl, perm_ref, dil)
            ob_ref[:, sl] = acc
            ssq = ssq + jnp.sum(acc * acc, axis=-1, keepdims=True)
        rb = lax.rsqrt(ssq / qb + EPS)
        rb_ref[...] = rb
        mix_ref[:, qa:qa + qb] = ((ob_ref[...] * rb) * gb_ref[...]).astype(BF16)

    def row(w):
        return pl.BlockSpec((tm, w), lambda i: (i, 0))

    def vec(w):
        return pl.BlockSpec((1, w), lambda i: (0, 0))

    return _call(
        body, None, name="mix_fwd", grid=(s // tm,),
        in_specs=([row(qa)] + [_class_spec(d, tm, qb) for d in DILATIONS]
                  + [_class_spec(d, tm, 128) for d in DILATIONS] + [vec(qa), vec(qb)]),
        out_specs=([row(qa + qb), row(qb)] + [_class_spec(d, tm, 128) for d in DILATIONS] + [row(1), row(1)]),
        out_shape=([jax.ShapeDtypeStruct((s, qa + qb), BF16), jax.ShapeDtypeStruct((s, qb), F32)]
                   + [_class_shape(d, s, 128, F32) for d in DILATIONS]
                   + [jax.ShapeDtypeStruct((s, 1), F32), jax.ShapeDtypeStruct((s, 1), F32)]),
        scratch_shapes=[pltpu.VMEM((tm, 128), F32)],
        compiler_params=_params(("parallel",), tm * (qa + 4 * qb) * 4 + tm * (qa + qb) * 2 + tm * 4096),
    )(oa, *obs, *lses, ga, gb)[0]


def _head_rowsums(prod, rows):
    low = _low_lanes(rows)
    lane = lax.broadcasted_iota(jnp.int32, (rows, 128), 1)
    out = jnp.zeros((rows, 128), F32)
    for i in range(prod.shape[1] // PAIR):
        tile = prod[:, i * PAIR:(i + 1) * PAIR]
        lo = jnp.sum(jnp.where(low, tile, 0.0), axis=-1, keepdims=True)
        hi = jnp.sum(jnp.where(low, 0.0, tile), axis=-1, keepdims=True)
        out = jnp.where(lane == 2 * i, lo, out)
        out = jnp.where(lane == 2 * i + 1, hi, out)
    return out


def _mix_bwd(dmix, oa, ob, ra, rb, ga, gb, hook=None):
    s, qa = oa.shape
    qb = ob.shape[1]
    tm = _row_tile(s)

    def one(dy, o, r, g):
        xn = o * r
        dxn = dy * g
        do = r * (dxn - xn * jnp.mean(dxn * xn, axis=-1, keepdims=True))
        return do, jnp.sum(dy * xn, axis=0, keepdims=True), _head_rowsums(do * o, tm)

    def body(dmix_ref, oa_ref, ob_ref, ra_ref, rb_ref, ga_ref, gb_ref,
             doa_ref, dob1_ref, dob2_ref, dob3_ref, dla_ref, dlb1_ref, dlb2_ref, dlb3_ref,
             dga_ref, dgb_ref, perm_ref):
        doa, dga, dla = one(dmix_ref[:, 0:qa], oa_ref[...], ra_ref[...], ga_ref[...])
        dob, dgb, dlb = one(dmix_ref[:, qa:qa + qb], ob_ref[...], rb_ref[...], gb_ref[...])
        doa_ref[...] = doa.astype(BF16)
        dla_ref[...] = dla
        for dob_ref, dlb_ref, dil in zip((dob1_ref, dob2_ref, dob3_ref), (dlb1_ref, dlb2_ref, dlb3_ref),
                                         DILATIONS):
            _to_classes(dlb_ref, slice(0, 128), dlb, perm_ref, dil)
            for i in range(qb // PAIR):
                sl = slice(i * PAIR, (i + 1) * PAIR)
                _to_classes(dob_ref, sl, dob[:, sl], perm_ref, dil)

        @pl.when(pl.program_id(0) == 0)
        def _():
            dga_ref[...] = dga
            dgb_ref[...] = dgb

        @pl.when(pl.program_id(0) > 0)
        def _():
            dga_ref[...] += dga
            dgb_ref[...] += dgb

    def row(w):
        return pl.BlockSpec((tm, w), lambda i: (i, 0))

    def vec(w):
        return pl.BlockSpec((1, w), lambda i: (0, 0))

    res, hook_res = _call(
        body, hook, name="mix_bwd", grid=(s // tm,),
        in_specs=[row(qa + qb), row(qa), row(qb), row(1), row(1), vec(qa), vec(qb)],
        out_specs=([row(qa)] + [_class_spec(d, tm, qb) for d in DILATIONS] + [row(128)]
                   + [_class_spec(d, tm, 128) for d in DILATIONS] + [vec(qa), vec(qb)]),
        out_shape=([jax.ShapeDtypeStruct((s, qa), BF16)] + [_class_shape(d, s, qb, BF16) for d in DILATIONS]
                   + [jax.ShapeDtypeStruct((s, 128), F32)] + [_class_shape(d, s, 128, F32) for d in DILATIONS]
                   + [jax.ShapeDtypeStruct((1, qa), F32), jax.ShapeDtypeStruct((1, qb), F32)]),
        scratch_shapes=[pltpu.VMEM((tm, 128), F32)],
        compiler_params=_params(("arbitrary",), tm * (qa + qb) * 16),
    )(dmix, oa, ob, ra, rb, ga, gb)
    return res if hook is None else (res, hook_res)


def _assemble_dproj(dqa, dkva, dqs, dks, dvs):
    s, qa = dqa.shape
    kva = dkva.shape[1]
    qb = dqs[0].shape[2]
    width = qa + kva + 3 * qb
    tm = _row_tile(s)

    def body(dqa_ref, dkva_ref, q1, q2, q3, k1, k2, k3, v1, v2, v3, dp_ref, db_ref, perm_ref):
        first = pl.program_id(0) == 0

        def emit(off, val):
            dp_ref[:, off:off + PAIR] = val.astype(BF16)
            col = jnp.sum(val, axis=0, keepdims=True)

            @pl.when(first)
            def _():
                db_ref[:, off:off + PAIR] = col

            @pl.when(jnp.logical_not(first))
            def _():
                db_ref[:, off:off + PAIR] += col

        for i in range(qa // PAIR):
            emit(i * PAIR, dqa_ref[:, i * PAIR:(i + 1) * PAIR])
        for i in range(kva // PAIR):
            emit(qa + i * PAIR, dkva_ref[:, i * PAIR:(i + 1) * PAIR])
        for j, branch_refs in enumerate(((q1, q2, q3), (k1, k2, k3), (v1, v2, v3))):
            for i in range(qb // PAIR):
                sl = slice(i * PAIR, (i + 1) * PAIR)
                total = None
                for ref, dil in zip(branch_refs, DILATIONS):
                    val = _from_classes(ref, sl, perm_ref, dil)
                    total = val if total is None else total + val
                emit(qa + kva + j * qb + i * PAIR, total)

    def row(w):
        return pl.BlockSpec((tm, w), lambda i: (i, 0))

    return _call(
        body, None, name="assemble_dproj", grid=(s // tm,),
        in_specs=[row(qa), row(kva)] + [_class_spec(d, tm, qb) for d in DILATIONS] * 3,
        out_specs=[row(width), pl.BlockSpec((1, width), lambda i: (0, 0))],
        out_shape=[jax.ShapeDtypeStruct((s, width), BF16), jax.ShapeDtypeStruct((1, width), F32)],
        scratch_shapes=[pltpu.VMEM((tm, 128), F32)],
        compiler_params=_params(("arbitrary",), tm * (qa + kva + 9 * qb) * 4 + tm * width * 2),
    )(dqa, dkva, *dqs, *dks, *dvs)[0]


def _fill_bias(bias_ref, n_pairs, max_steps, dil, slopes, sink_ref=None):
    qi = lax.broadcasted_iota(jnp.int32, (BLOCK, 2 * BLOCK), 0)
    kj = lax.broadcasted_iota(jnp.int32, (BLOCK, 2 * BLOCK), 1)
    steps = qi + BLOCK - kj
    dist = (steps * dil).astype(F32)
    band = (steps >= 0) & (steps <= max_steps)
    assert sink_ref is None or max_steps < BLOCK
    for first in (0, 1):
        valid = band & (kj >= BLOCK) if first else band
        for i in range(n_pairs):
            tables = []
            for half in (0, 1):
                table = jnp.where(valid, -(slopes[2 * i + half] * dist), NEG_INF)
                if sink_ref is not None:
                    table = jnp.where(kj == 0, sink_ref[2 * i + half], table)
                tables.append(table)
            bias_ref[first, i] = jnp.concatenate(tables, axis=0)


def _without_sink_row(tile):
    row = lax.broadcasted_iota(jnp.int32, tile.shape, 0)
    return jnp.where(row == 0, jnp.zeros_like(tile), tile)


def _bias_shape(n_pairs):
    return pltpu.VMEM((2, n_pairs, 2 * BLOCK, 2 * BLOCK), F32)


def _stack_heads(tile, low):
    zero = jnp.zeros_like(tile)
    return jnp.concatenate([jnp.where(low, tile, zero), jnp.where(low, zero, tile)], axis=0)


def _unstack_heads(stacked, low):
    return jnp.where(low, stacked[0:BLOCK], stacked[BLOCK:2 * BLOCK])


def _head_columns(ref, i):
    return jnp.concatenate([ref[:, 2 * i:2 * i + 1], ref[:, 2 * i + 1:2 * i + 2]], axis=0)


def _swap_halves(t):
    return pltpu.roll(t, HEAD_DIM, 1)


def _dup_group(t_bf16, group):
    t = t_bf16.astype(F32)
    low = lax.broadcasted_iota(jnp.int32, t.shape, 1) < HEAD_DIM
    keep = low if group == 0 else jnp.logical_not(low)
    return jnp.where(keep, t, _swap_halves(t)).astype(BF16)


def _attn_fwd(name, q, kv, *, dil, max_steps, slopes, sinks=None, hook=None):
    grouped = sinks is not None
    _, length, w = q.shape
    n_pairs = w // PAIR
    nb = length // BLOCK
    heads_per_group = 2 * n_pairs // N_KV_GROUPS

    def body(*refs):
        if grouped:
            sink_ref, q_ref, kvp_ref, kvc_ref, o_ref, lse_ref, bias_ref = refs
        else:
            q_ref, kp_ref, kc_ref, vp_ref, vc_ref, o_ref, lse_ref, bias_ref = refs
        n = pl.program_id(1)

        @pl.when((pl.program_id(0) == 0) & (n == 0))
        def _():
            _fill_bias(bias_ref, n_pairs, max_steps, dil, slopes, sink_ref if grouped else None)

        first = (n == 0).astype(jnp.int32)
        low = _low_lanes(BLOCK)
        lane = lax.broadcasted_iota(jnp.int32, (BLOCK, 128), 1)
        lse_acc = jnp.zeros((BLOCK, 128), F32)
        if grouped:
            kv_all = jnp.concatenate([kvp_ref[...], kvc_ref[...]], axis=0)
            k_dup = [_without_sink_row(_dup_group(kv_all[:, 0:PAIR], g)) for g in range(N_KV_GROUPS)]
            v_dup = [_without_sink_row(_dup_group(kv_all[:, PAIR:2 * PAIR], g)) for g in range(N_KV_GROUPS)]
        for i in range(n_pairs):
            sl = slice(i * PAIR, (i + 1) * PAIR)
            qs = _stack_heads(q_ref[:, sl] * ATT_SCALE, low)
            if grouped:
                kk, vv = k_dup[2 * i // heads_per_group], v_dup[2 * i // heads_per_group]
            else:
                kk = jnp.concatenate([kp_ref[:, sl], kc_ref[:, sl]], axis=0)
                vv = jnp.concatenate([vp_ref[:, sl], vc_ref[:, sl]], axis=0)
            sc = lax.dot_general(qs, kk, (((1,), (1,)), ((), ())), preferred_element_type=F32)
            sc = sc + bias_ref[first, i]
            m = jnp.max(sc, axis=-1, keepdims=True)
            p = jnp.exp(sc - m)
            den = jnp.sum(p, axis=-1, keepdims=True)
            o = jnp.dot(p.astype(BF16), vv, preferred_element_type=F32) / den
            o_ref[:, sl] = _unstack_heads(o, low).astype(o_ref.dtype)
            lse = m + jnp.log(den)
            lse_acc = jnp.where(lane == 2 * i, lse[0:BLOCK], lse_acc)
            lse_acc = jnp.where(lane == 2 * i + 1, lse[BLOCK:2 * BLOCK], lse_acc)
        lse_ref[...] = lse_acc

    def cur(width):
        return pl.BlockSpec((None, BLOCK, width), lambda r, n: (r, n, 0))

    def prev(width):
        return pl.BlockSpec((None, BLOCK, width), lambda r, n: (r, jnp.maximum(n - 1, 0), 0))

    if grouped:
        kvw = kv.shape[2]
        operands = [sinks, q, kv, kv]
        in_specs = [SMEM_SPEC, cur(w), prev(kvw), cur(kvw)]
    else:
        operands = [q, kv[0], kv[0], kv[1], kv[1]]
        in_specs = [cur(w), prev(w), cur(w), prev(w), cur(w)]
    res, hook_res = _call(
        body, hook, name=name, grid=(dil, nb), in_specs=in_specs,
        out_specs=[cur(w), cur(128)],
        out_shape=[jax.ShapeDtypeStruct((dil, length, w), F32 if grouped else BRANCH_DTYPE),
                   jax.ShapeDtypeStruct((dil, length, 128), F32)],
        scratch_shapes=[_bias_shape(n_pairs)],
        compiler_params=_params(("arbitrary", "arbitrary"), BLOCK * w * 16 + n_pairs * BLOCK * BLOCK * 16),
    )(*operands)
    return res if hook is None else (res, hook_res)


def _attn_bwd(name, q, kv, do, lse, delta, *, dil, max_steps, slopes, sinks=None, hook=None):
    grouped = sinks is not None
    _, length, w = q.shape
    n_pairs = w // PAIR
    nb = length // BLOCK
    heads_per_group = 2 * n_pairs // N_KV_GROUPS
    pairs_per_group = n_pairs // N_KV_GROUPS

    def body(*refs):
        if grouped:
            (sink_ref, q_ref, kvp_ref, kvc_ref, do_ref, lse_ref, dl_ref,
             dq_ref, dkv_ref, dsink_ref, acc_ref, bias_ref) = refs
        else:
            (q_ref, kp_ref, kc_ref, vp_ref, vc_ref, do_ref, lse_ref, dl_ref,
             dq_ref, dk_ref, dv_ref, acck_ref, accv_ref, bias_ref) = refs
        n = pl.program_id(1)

        @pl.when((pl.program_id(0) == 0) & (n == 0))
        def _():
            _fill_bias(bias_ref, n_pairs, max_steps, dil, slopes, sink_ref if grouped else None)

        @pl.when(n == 0)
        def _():
            if grouped:
                acc_ref[...] = jnp.zeros_like(acc_ref)

                @pl.when(pl.program_id(0) == 0)
                def _():
                    dsink_ref[...] = jnp.zeros_like(dsink_ref)
            else:
                acck_ref[...] = jnp.zeros_like(acck_ref)
                accv_ref[...] = jnp.zeros_like(accv_ref)

        @pl.when(n == nb)
        def _():
            if grouped:
                dkv_ref[...] = acc_ref[...]
            else:
                dk_ref[...] = acck_ref[...].astype(dk_ref.dtype)
                dv_ref[...] = accv_ref[...].astype(dv_ref.dtype)

        @pl.when(n < nb)
        def _():
            first = (n == 0).astype(jnp.int32)
            low = _low_lanes(BLOCK)
            low_kv = _low_lanes(2 * BLOCK)
            lane1 = lax.broadcasted_iota(jnp.int32, (1, 128), 1)
            if grouped:
                kv_all = jnp.concatenate([kvp_ref[...], kvc_ref[...]], axis=0)
                k_dup = [_without_sink_row(_dup_group(kv_all[:, 0:PAIR], g)) for g in range(N_KV_GROUPS)]
                v_dup = [_without_sink_row(_dup_group(kv_all[:, PAIR:2 * PAIR], g)) for g in range(N_KV_GROUPS)]
                dk_grp =[jnp.zeros((2 * BLOCK, PAIR), F32) for _ in range(N_KV_GROUPS)]
                dv_grp = [jnp.zeros((2 * BLOCK, PAIR), F32) for _ in range(N_KV_GROUPS)]
                dsink = jnp.zeros((1, 128), F32)
            for i in range(n_pairs):
                sl = slice(i * PAIR, (i + 1) * PAIR)
                qs = _stack_heads(q_ref[:, sl] * ATT_SCALE, low)
                dos = _stack_heads(do_ref[:, sl], low)
                if grouped:
                    grp = 2 * i // heads_per_group
                    kk, vv = k_dup[grp], v_dup[grp]
                else:
                    kk = jnp.concatenate([kp_ref[:, sl], kc_ref[:, sl]], axis=0)
                    vv = jnp.concatenate([vp_ref[:, sl], vc_ref[:, sl]], axis=0)
                lse_col = _head_columns(lse_ref, i)
                dl_col = _head_columns(dl_ref, i)
                sc = lax.dot_general(qs, kk, (((1,), (1,)), ((), ())), preferred_element_type=F32)
                p = jnp.exp(sc + bias_ref[first, i] - lse_col)
                dp = lax.dot_general(dos, vv, (((1,), (1,)), ((), ())), preferred_element_type=F32)
                ds_f32 = p * (dp - dl_col)
                ds = ds_f32.astype(BF16)
                dq = jnp.dot(ds, kk, preferred_element_type=F32)
                dkk = lax.dot_general(ds, qs, (((0,), (0,)), ((), ())), preferred_element_type=F32)
                dvv = lax.dot_general(p.astype(BF16), dos, (((0,), (0,)), ((), ())),
                                      preferred_element_type=F32)
                if grouped:
                    for half in (0, 1):
                        contrib = jnp.sum(ds_f32[half * BLOCK:(half + 1) * BLOCK, 0:1], axis=0, keepdims=True)
                        dsink = jnp.where(lane1 == 2 * i + half, dsink + contrib, dsink)
                dq_ref[:, sl] = (_unstack_heads(dq, low) * ATT_SCALE).astype(dq_ref.dtype)
                if grouped:
                    dk_grp[grp] = dk_grp[grp] + dkk
                    dv_grp[grp] = dv_grp[grp] + dvv
                else:
                    dk_ref[:, sl] = (acck_ref[:, sl] + dkk[0:BLOCK]).astype(dk_ref.dtype)
                    acck_ref[:, sl] = dkk[BLOCK:2 * BLOCK]
                    dv_ref[:, sl] = (accv_ref[:, sl] + dvv[0:BLOCK]).astype(dv_ref.dtype)
                    accv_ref[:, sl] = dvv[BLOCK:2 * BLOCK]
            if grouped:
                folded = [_without_sink_row(t + _swap_halves(t)) for t in dk_grp + dv_grp]
                dk_tile = jnp.where(low_kv, folded[0], folded[1])
                dv_tile = jnp.where(low_kv, folded[2], folded[3])
                part = jnp.concatenate([dk_tile, dv_tile], axis=1)
                dkv_ref[...] = acc_ref[...] + part[0:BLOCK]
                acc_ref[...] = part[BLOCK:2 * BLOCK]
                dsink_ref[...] += dsink

    last = nb - 1

    def cur(width):
        return pl.BlockSpec((None, BLOCK, width), lambda r, n: (r, jnp.minimum(n, last), 0))

    def prev(width):
        return pl.BlockSpec((None, BLOCK, width),
                            lambda r, n: (r, jnp.maximum(jnp.minimum(n, last) - 1, 0), 0))

    def done(width):
        return pl.BlockSpec((None, BLOCK, width), lambda r, n: (r, jnp.maximum(n - 1, 0), 0))

    if grouped:
        assert pairs_per_group * N_KV_GROUPS == n_pairs and heads_per_group % 2 == 0
        kvw = kv.shape[2]
        operands = [sinks, q, kv, kv, do, lse, delta]
        in_specs = [SMEM_SPEC, cur(w), prev(kvw), cur(kvw), cur(w), cur(128), cur(128)]
        out_specs = [cur(w), done(kvw), pl.BlockSpec((1, 128), lambda r, n: (0, 0))]
        out_shape = [jax.ShapeDtypeStruct((dil, length, w), F32), jax.ShapeDtypeStruct((dil, length, kvw), F32),
                     jax.ShapeDtypeStruct((1, 128), F32)]
        scratch = [pltpu.VMEM((BLOCK, kvw), F32), _bias_shape(n_pairs)]
    else:
        operands = [q, kv[0], kv[0], kv[1], kv[1], do, lse, delta]
        in_specs = [cur(w), prev(w), cur(w), prev(w), cur(w), cur(w), cur(128), cur(128)]
        out_specs = [cur(w), done(w), done(w)]
        out_shape = [jax.ShapeDtypeStruct((dil, length, w), BRANCH_DTYPE)] * 3
        scratch = [pltpu.VMEM((BLOCK, w), F32), pltpu.VMEM((BLOCK, w), F32), _bias_shape(n_pairs)]
    res, hook_res = _call(
        body, hook, name=name, grid=(dil, nb + 1), in_specs=in_specs, out_specs=out_specs,
        out_shape=out_shape, scratch_shapes=scratch,
        compiler_params=_params(("arbitrary", "arbitrary"), BLOCK * w * 32 + n_pairs * BLOCK * BLOCK * 16),
    )(*operands)
    return res if hook is None else (res, hook_res)


def _adamw(name, w, g, m, v):
    rows, cols = w.shape
    tm = 256 if rows % 256 == 0 else rows

    def body(w_ref, g_ref, m_ref, v_ref, d_ref, nm_ref, nv_ref, g_out_ref):
        gv = g_ref[...]
        mn = ADAM_B1 * m_ref[...] + (1.0 - ADAM_B1) * gv
        vn = ADAM_B2 * v_ref[...] + (1.0 - ADAM_B2) * (gv * gv)
        m_hat = mn / (1.0 - ADAM_B1 ** ADAM_STEP)
        v_hat = vn / (1.0 - ADAM_B2 ** ADAM_STEP)
        d_ref[...] = -ADAM_LR * (m_hat / (jnp.sqrt(v_hat) + ADAM_EPS) + ADAM_WD * w_ref[...])
        nm_ref[...] = mn
        nv_ref[...] = vn
        g_out_ref[...] = gv

    spec = pl.BlockSpec((tm, cols), lambda i: (i, 0))
    return _call(
        body, None, name=name, grid=(rows // tm,), in_specs=[spec] * 4, out_specs=[spec] * 4,
        out_shape=[jax.ShapeDtypeStruct(w.shape, F32)] * 4,
        compiler_params=_params(("parallel",), tm * cols * 32),
    )(w, g, m, v)[0]


def _mesh_position():
    return lax.axis_index("x"), lax.axis_index("y"), lax.axis_index("c")


def _other_chips(x, y):
    return [(1 - x, y), (x, 1 - y), (1 - x, 1 - y)]


def _gather_hook(gathered, lo, hi):
    rows, cols = gathered.shape[0] // N_CHIPS, gathered.shape[1]
    half, n = rows // 2, hi - lo
    assert lo % 16 == 0 and n % 16 == 0 and half % 16 == 0

    def region(out, owner_chip, which_half):
        return out.at[pl.ds(pl.multiple_of(owner_chip * rows + which_half * half + lo, 16), n)]

    def parts(outs, sems):
        x, y, c = _mesh_position()
        return outs[0], sems, c, 2 * x + y, (x, y, 1 - c), _other_chips(x, y)

    def start(ops, outs, sems):
        out, (send, recv, fsend, frecv), c, chip, sibling, others = parts(outs, sems)
        mine = region(out, chip, c)
        for k, (px, py) in enumerate(others):
            _remote(mine, mine, send.at[k], recv.at[k], (px, py, c)).start()

    def mid(ops, outs, sems):
        out, (send, recv, fsend, frecv), c, chip, sibling, others = parts(outs, sems)
        for k, (px, py) in enumerate(others):
            landed = region(out, 2 * px + py, c)
            _remote(landed, landed, send.at[k], recv.at[k], (px, py, c)).wait_recv()
            _remote(landed, landed, fsend.at[k], frecv.at[k], sibling).start()

    def finish(ops, outs, sems):
        out, (send, recv, fsend, frecv), c, chip, sibling, others = parts(outs, sems)
        mine = region(out, chip, c)
        for k, (px, py) in enumerate(others):
            passed = region(out, 2 * px + py, 1 - c)
            _remote(passed, passed, fsend.at[k], frecv.at[k], sibling).wait_recv()
        for k, (px, py) in enumerate(others):
            landed = region(out, 2 * px + py, c)
            _remote(landed, landed, fsend.at[k], frecv.at[k], sibling).wait_send()
            _remote(mine, mine, send.at[k], recv.at[k], (px, py, c)).wait_send()

    return _Hook([gathered], [jax.ShapeDtypeStruct(gathered.shape, gathered.dtype)],
                 [pltpu.SemaphoreType.DMA((3,))] * 4, start, finish, mid, aliases={0: 0})


def _own_shard_in_place(name, shard, chip):
    rows, cols = shard.shape
    tr = next(t for t in (544, 512, 320, 256, 128, 64, 32, 16) if rows % t == 0)

    def body(chip_ref, w_ref, o_ref):
        o_ref[...] = w_ref[...].astype(BF16)

    return pl.pallas_call(
        body, name=name,
        grid_spec=pltpu.PrefetchScalarGridSpec(
            num_scalar_prefetch=1, grid=(rows // tr,),
            in_specs=[pl.BlockSpec((tr, cols), lambda i, chip_ref: (i, 0))],
            out_specs=pl.BlockSpec((tr, cols), lambda i, chip_ref: (chip_ref[0] * (rows // tr) + i, 0))),
        out_shape=jax.ShapeDtypeStruct((N_CHIPS * rows, cols), BF16),
        compiler_params=_params(("parallel",), tr * cols * 6),
    )(chip, shard)


def _exchange_hook(grad):
    rows, cols = grad.shape[0] // N_CHIPS, grad.shape[1]
    half = rows // 2
    assert half % 16 == 0

    def copies(ops, outs, sems):
        x, y, c = _mesh_position()
        send, recv = sems
        return [_remote(ops[0].at[pl.ds(pl.multiple_of(k * rows + (1 - c) * half, 16), half)], outs[0].at[k],
                        send.at[k], recv.at[k], (x, y, 1 - c)) for k in range(N_CHIPS)]

    def start(ops, outs, sems):
        for cp in copies(ops, outs, sems):
            cp.start()

    def finish(ops, outs, sems):
        for cp in copies(ops, outs, sems):
            cp.wait_recv()
            cp.wait_send()

    return _Hook([grad], [jax.ShapeDtypeStruct((N_CHIPS, half, cols), grad.dtype)],
                 [pltpu.SemaphoreType.DMA((N_CHIPS,))] * 2, start, finish)


def _scatter_hook(chip_sum):
    _, half, cols = chip_sum.shape

    def copies(ops, outs, sems):
        x, y, c = _mesh_position()
        send, recv = sems
        return [_remote(ops[0].at[2 * px + py], outs[0].at[k], send.at[k], recv.at[k], (px, py, c))
                for k, (px, py) in enumerate(_other_chips(x, y))]

    def start(ops, outs, sems):
        for cp in copies(ops, outs, sems):
            cp.start()

    def finish(ops, outs, sems):
        for cp in copies(ops, outs, sems):
            cp.wait_recv()
            cp.wait_send()

    return _Hook([chip_sum], [jax.ShapeDtypeStruct((3, half, cols), chip_sum.dtype)],
                 [pltpu.SemaphoreType.DMA((3,))] * 2, start, finish)


def _sum_tile(half):
    return 256 if half % 256 == 0 else half


def _chip_add(name, grad, from_sibling, core):
    n_chips, half, cols = from_sibling.shape
    rows = 2 * half
    tr = _sum_tile(half)

    def body(core_ref, g_ref, s_ref, o_ref):
        o_ref[...] = (g_ref[...].astype(F32) + s_ref[...].astype(F32)).astype(o_ref.dtype)

    tile = pl.BlockSpec((None, tr, cols), lambda k, i, core_ref: (k, i, 0))
    return pl.pallas_call(
        body, name=name,
        grid_spec=pltpu.PrefetchScalarGridSpec(
            num_scalar_prefetch=1, grid=(n_chips, half // tr),
            in_specs=[pl.BlockSpec((tr, cols), lambda k, i, core_ref:
                                   (k * (rows // tr) + core_ref[0] * (half // tr) + i, 0)), tile],
            out_specs=tile),
        out_shape=jax.ShapeDtypeStruct(from_sibling.shape, from_sibling.dtype),
        compiler_params=_params(("parallel", "parallel"), 3 * tr * cols * 4),
    )(core, grad, from_sibling)


def _final_add(name, chip_sum, from_chips, chip, core):
    _, half, cols = chip_sum.shape
    tr = _sum_tile(half)

    def body(chip_ref, core_ref, own_ref, others_ref, o_ref):
        total = own_ref[...].astype(F32)
        for k in range(3):
            total = total + others_ref[k].astype(F32)
        o_ref[...] = total

    return pl.pallas_call(
        body, name=name,
        grid_spec=pltpu.PrefetchScalarGridSpec(
            num_scalar_prefetch=2, grid=(half // tr,),
            in_specs=[pl.BlockSpec((None, tr, cols), lambda i, chip_ref, core_ref: (chip_ref[0], i, 0)),
                      pl.BlockSpec((3, tr, cols), lambda i, chip_ref, core_ref: (0, i, 0))],
            out_specs=pl.BlockSpec((tr, cols),
                                   lambda i, chip_ref, core_ref: (core_ref[0] * (half // tr) + i, 0))),
        out_shape=jax.ShapeDtypeStruct((2 * half, cols), F32),
        compiler_params=_params(("parallel",), 6 * tr * cols * 4),
    )(chip, core, chip_sum, from_chips)


def _share_halves(shards, small):
    n_s = len(shards)
    rows_s = small.shape[0]

    def body(*refs):
        small_ref = refs[n_s]
        outs, small_out = refs[n_s + 1:2 * n_s + 1], refs[2 * n_s + 1]
        small_all, send, recv, small_send, small_recv = refs[2 * n_s + 2:]
        x, y, c = _mesh_position()
        me = 4 * x + 2 * y + c
        sibling = (x, y, 1 - c)
        pending = []
        for i in range(n_s):
            half = shards[i].shape[0] // 2
            mine = outs[i].at[pl.ds(pl.multiple_of(c * half, 16), half)]
            cp = _remote(mine, mine, send.at[i], recv.at[i], sibling)
            cp.start()
            pending.append(cp)
        small_all[me] = small_ref[...]
        for j in range(N_DEV - 1):
            peer = (me + 1 + j) % N_DEV
            cp = _remote(small_all.at[me], small_all.at[me], small_send.at[j], small_recv.at[j],
                         (peer // 4, (peer // 2) % 2, peer % 2))
            cp.start()
            pending.append(cp)
        for i in range(n_s):
            half = shards[i].shape[0] // 2
            theirs = outs[i].at[pl.ds(pl.multiple_of((1 - c) * half, 16), half)]
            _remote(theirs, theirs, send.at[i], recv.at[i], sibling).wait_recv()
        for j in range(N_DEV - 1):
            peer = (me + N_DEV - 1 - j) % N_DEV
            _remote(small_all.at[peer], small_all.at[peer], small_send.at[j], small_recv.at[j],
                    sibling).wait_recv()
        total = small_all[0]
        for dev in range(1, N_DEV):
            total = total + small_all[dev]
        small_out[...] = total
        for cp in pending:
            cp.wait_send()

    res = pl.pallas_call(
        body, name="share_halves",
        in_specs=[HBM_SPEC] * n_s + [VMEM_SPEC], out_specs=[HBM_SPEC] * n_s + [VMEM_SPEC],
        out_shape=[jax.ShapeDtypeStruct(sh.shape, sh.dtype) for sh in shards]
        + [jax.ShapeDtypeStruct((rows_s, 128), F32)],
        scratch_shapes=[pltpu.VMEM((N_DEV, rows_s, 128), F32),
                        pltpu.SemaphoreType.DMA((n_s,)), pltpu.SemaphoreType.DMA((n_s,)),
                        pltpu.SemaphoreType.DMA((N_DEV - 1,)), pltpu.SemaphoreType.DMA((N_DEV - 1,))],
        input_output_aliases={i: i for i in range(n_s)},
    )(*shards, small)
    return res[:n_s], res[n_s]


def _pack_small(parts, rows):
    flat = jnp.concatenate([p.reshape(-1) for p in parts])
    flat = jnp.pad(flat, (0, rows * 128 - flat.shape[0]))
    return flat.reshape(rows, 128)


def _unpack_small(packed, shapes):
    flat = packed.reshape(-1)
    out, off = [], 0
    for shp in shapes:
        n = int(np.prod(shp))
        out.append(flat[off:off + n].reshape(shp))
        off += n
    return out


def kernel(x, g_attn, w_in, b_in, sinks_a, g_out_a, g_out_b, w_out, g_mlp, w_1, w_2, g_final, loss_target, m_g_attn, m_w_in, m_b_in, m_sinks_a, m_g_out_a, m_g_out_b, m_w_out, m_g_mlp, m_w_1, m_w_2, m_g_final, v_g_attn, v_w_in, v_b_in, v_sinks_a, v_g_out_a, v_g_out_b, v_w_out, v_g_mlp, v_w_1, v_w_2, v_g_final):
    s, d = x.shape[1], x.shape[2]
    d_in = b_in.shape[1]
    qa = g_out_a.shape[1]
    qb = g_out_b.shape[1]
    kva = 2 * N_KV_GROUPS * HEAD_DIM
    assert d_in == qa + kva + 3 * qb and qa + qb == w_out.shape[1] * N_CHIPS
    d_ff = w_1.shape[2] * N_CHIPS
    ff_shard = w_1.shape[2]
    n_heads_a, n_heads_b = qa // HEAD_DIM, qb // HEAD_DIM
    slopes_a, slopes_b = alibi_slopes(n_heads_a), alibi_slopes(n_heads_b)

    x2d = x[0]
    target = loss_target[0]

    core_index = lax.axis_index("c").astype(jnp.int32).reshape(1)
    chip_index = (2 * lax.axis_index("x") + lax.axis_index("y")).astype(jnp.int32).reshape(1)
    shards = {"w_in": w_in[0].T, "w_out": w_out[0], "w_1": w_1[0], "w_2": w_2[0]}
    halves = [sh.shape[0] // 2 for sh in shards.values()]
    w_in_t, w_out_g, w_1_g, w_2_g = [_own_shard_in_place(f"place_{n}", sh, chip_index)
                                     for n, sh in shards.items()]

    tm = _tile(s, 1024)

    (h1, r1), (w_in_t,) = _norm_fwd("norm_attn", x2d, g_attn, hook=_gather_hook(w_in_t, 0, halves[0]))

    q_a, = _project_by_class("proj_qa", h1, w_in_t, b_in, 0, qa, (1,))
    kv_a, = _project_by_class("proj_kva", h1, w_in_t, b_in, qa, kva, (1,))
    q_bs, (w_out_g,) = _project_by_class("proj_qb", h1, w_in_t, b_in, qa + kva, qb, DILATIONS,
                                         hook=_gather_hook(w_out_g, 0, halves[1]))
    k_bs = _project_by_class("proj_kb", h1, w_in_t, b_in, qa + kva + qb, qb, DILATIONS)
    v_bs = _project_by_class("proj_vb", h1, w_in_t, b_in, qa + kva + 2 * qb, qb, DILATIONS)

    quarter = halves[2] // 4
    sinks = sinks_a.reshape(-1)
    (o_a, lse_a), (w_1_g,) = _attn_fwd("attn_a_fwd", q_a, kv_a, dil=1, max_steps=WINDOW_A - 1, slopes=slopes_a,
                                       sinks=sinks, hook=_gather_hook(w_1_g, 0, quarter))
    o_a = o_a[0]
    o_bs, lse_bs = [], []
    for n, (window, dil) in enumerate(DILATED_BRANCHES):
        (o, l), (w_1_g,) = _attn_fwd(f"attn_b{dil}_fwd", q_bs[n], (k_bs[n], v_bs[n]), dil=dil,
                                     max_steps=window // dil, slopes=slopes_b,
                                     hook=_gather_hook(w_1_g, (n + 1) * quarter, (n + 2) * quarter))
        o_bs.append(o)
        lse_bs.append(l)
    w_1_g = w_1_g.reshape(N_CHIPS, d, ff_shard)
    mix, o_b, *lse_tot, r_a, r_b = _mix_fwd(o_a, o_bs, lse_bs, g_out_a, g_out_b)

    tn = _tile(d, 1024)
    a_spec, b_spec = _mm_specs("nn", tm, tn, d)
    tile_mn = pl.BlockSpec((tm, tn), lambda i, j, k: (i, j))
    x2 = _matmul("out_proj", mix, w_out_g, [x2d], mode="nn", grid=(s // tm, d // tn, 1),
                 a_spec=a_spec, b_spec=b_spec, extra_specs=[tile_mn],
                 out_shapes=[jax.ShapeDtypeStruct((s, d), F32)], out_specs=[tile_mn],
                 epilogue=lambda acc, res: (acc + res,))[0]

    h2, r2 = _norm_fwd("norm_mlp", x2, g_mlp)

    tn = _tile(ff_shard, 1024)
    per = ff_shard // tn
    a_spec, _ = _mm_specs("nn", tm, tn, d)
    tile_mn = pl.BlockSpec((tm, tn), lambda i, j, k: (i, j))
    (u,), (w_2_g,) = _matmul(
        "mlp_up", h2, w_1_g, [], mode="nn", grid=(s // tm, d_ff // tn, 1),
        a_spec=a_spec, b_spec=pl.BlockSpec((None, d, tn), lambda i, j, k: (j // per, 0, j % per)),
        extra_specs=[], out_shapes=[jax.ShapeDtypeStruct((s, d_ff), BF16)], out_specs=[tile_mn],
        epilogue=lambda acc: (jnp.maximum(acc, 0.0),),
        hook=_gather_hook(w_2_g, 0, halves[3]))

    tn = _tile(d, 1024)
    tk = _tile(d_ff, 2048)
    a_spec, b_spec = _mm_specs("nn", tm, tn, tk)
    tile_mn = pl.BlockSpec((tm, tn), lambda i, j, k: (i, j))
    x3 = _matmul("mlp_down", u, w_2_g, [x2], mode="nn", grid=(s // tm, d // tn, d_ff // tk),
                 a_spec=a_spec, b_spec=b_spec, extra_specs=[tile_mn],
                 out_shapes=[jax.ShapeDtypeStruct((s, d), F32)], out_specs=[tile_mn],
                 prologue=lambda a: a * a, epilogue=lambda acc, res: (acc + res,), acc_shape=(tm, tn))[0]

    dx3, dx3b, loss_part, dg_final = _loss_head(x3, target, g_final.reshape(1, d))

    tn = _tile(d_ff, 1024)
    a_spec, b_spec = _mm_specs("nt", tm, tn, d)
    tile_mn = pl.BlockSpec((tm, tn), lambda i, j, k: (i, j))
    dpre = _matmul("mlp_down_dx", dx3b, w_2_g, [u], mode="nt", grid=(s // tm, d_ff // tn, 1),
                   a_spec=a_spec, b_spec=b_spec, extra_specs=[tile_mn],
                   out_shapes=[jax.ShapeDtypeStruct((s, d_ff), BF16)], out_specs=[tile_mn],
                   epilogue=lambda acc, uu: (acc * (2.0 * uu.astype(F32)),))[0]

    wire = GRAD_WIRE_DTYPE
    tk_s = _tile(s, 2048)
    tmw = _tile(d_ff, 1024)
    a_spec, b_spec = _mm_specs("tn", tmw, d, tk_s)
    dw_2 = _matmul("mlp_down_dw", u, dx3b, [], mode="tn", grid=(d_ff // tmw, 1, s // tk_s),
                   a_spec=a_spec, b_spec=b_spec, extra_specs=[],
                   out_shapes=[jax.ShapeDtypeStruct((d_ff, d), wire)],
                   out_specs=[pl.BlockSpec((tmw, d), lambda i, j, k: (i, j))],
                   prologue=lambda a: a * a, epilogue=lambda acc: (acc,), acc_shape=(tmw, d))[0]

    tn = _tile(d, 1024)
    tk = _tile(ff_shard, 2048)
    per = ff_shard // tk
    a_spec, _ = _mm_specs("nt", tm, tn, tk)
    tile_mn = pl.BlockSpec((tm, tn), lambda i, j, k: (i, j))
    dh2 = _matmul("mlp_up_dx", dpre, w_1_g, [], mode="nt", grid=(s // tm, d // tn, d_ff // tk),
                  a_spec=a_spec, b_spec=pl.BlockSpec((None, tn, tk), lambda i, j, k: (k // per, j, k % per)),
                  extra_specs=[], out_shapes=[jax.ShapeDtypeStruct((s, d), F32)], out_specs=[tile_mn],
                  epilogue=lambda acc: (acc,), acc_shape=(tm, tn))[0]

    tmw = _tile(d, 1024)
    tnw = _tile(ff_shard, 2048)
    per = ff_shard // tnw
    a_spec, b_spec = _mm_specs("tn", tmw, tnw, tk_s)
    dw_1 = _matmul("mlp_up_dw", h2, dpre, [], mode="tn", grid=(d // tmw, d_ff // tnw, s // tk_s),
                   a_spec=a_spec, b_spec=b_spec, extra_specs=[],
                   out_shapes=[jax.ShapeDtypeStruct((N_CHIPS, d, ff_shard), wire)],
                   out_specs=[pl.BlockSpec((None, tmw, tnw), lambda i, j, k: (j // per, i, j % per))],
                   epilogue=lambda acc: (acc,), acc_shape=(tmw, tnw))[0]

    dw_1 = dw_1.reshape(N_CHIPS * d, ff_shard)
    (dx2, dx2b, dg_mlp), (sib_2, sib_1) = _norm_bwd(
        "norm_mlp_bwd", dh2, x2, r2, g_mlp, dx3, hook=_merge_hooks([_exchange_hook(dw_2), _exchange_hook(dw_1)]))
    chip_sum_2 = _chip_add("chip_add_w_2", dw_2, sib_2, core_index)
    chip_sum_1 = _chip_add("chip_add_w_1", dw_1, sib_1, core_index)

    tn = _tile(d, 1024)
    a_spec, b_spec = _mm_specs("nt", tm, tn, d)
    tile_mn = pl.BlockSpec((tm, tn), lambda i, j, k: (i, j))
    dmix = _matmul("out_proj_dx", dx2b, w_out_g, [], mode="nt", grid=(s // tm, d // tn, 1),
                   a_spec=a_spec, b_spec=b_spec, extra_specs=[],
                   out_shapes=[jax.ShapeDtypeStruct((s, d), F32)], out_specs=[tile_mn],
                   epilogue=lambda acc: (acc,))[0]

    tmw = _tile(d, 1024)
    a_spec, b_spec = _mm_specs("tn", tmw, d, tk_s)
    dw_out = _matmul("out_proj_dw", mix, dx2b, [], mode="tn", grid=(d // tmw, 1, s // tk_s),
                     a_spec=a_spec, b_spec=b_spec, extra_specs=[],
                     out_shapes=[jax.ShapeDtypeStruct((d, d), wire)],
                     out_specs=[pl.BlockSpec((tmw, d), lambda i, j, k: (i, j))],
                     epilogue=lambda acc: (acc,), acc_shape=(tmw, d))[0]

    mix_grads, (sib_out,) = _mix_bwd(dmix, o_a, o_b, r_a, r_b, g_out_a, g_out_b, hook=_exchange_hook(dw_out))
    do_a, do_bs, delta_a, delta_bs = mix_grads[0], mix_grads[1:4], mix_grads[4], mix_grads[5:8]
    dg_out_a, dg_out_b = mix_grads[8:]
    chip_sum_out = _chip_add("chip_add_w_out", dw_out, sib_out, core_index)

    (dq_a, dkv_a, dsinks), (chips_2,) = _attn_bwd(
        "attn_a_bwd", q_a, kv_a, do_a[None], lse_a, delta_a[None], dil=1, max_steps=WINDOW_A - 1,
        slopes=slopes_a, sinks=sinks, hook=_scatter_hook(chip_sum_2))
    dqs, dks, dvs = [], [], []
    scatter = {1: chip_sum_1, 4: chip_sum_out}
    arrived = {}
    for n, (window, dil) in enumerate(DILATED_BRANCHES):
        res = _attn_bwd(f"attn_b{dil}_bwd", q_bs[n], (k_bs[n], v_bs[n]), do_bs[n], lse_tot[n],
                        delta_bs[n], dil=dil, max_steps=window // dil, slopes=slopes_b,
                        hook=_scatter_hook(scatter[dil]) if dil in scatter else None)
        if dil in scatter:
            res, (arrived[dil],) = res
        dq, dk, dv = res
        dqs.append(dq)
        dks.append(dk)
        dvs.append(dv)
    half_2 = _final_add("final_add_w_2", chip_sum_2, chips_2, chip_index, core_index)
    half_1 = _final_add("final_add_w_1", chip_sum_1, arrived[1], chip_index, core_index)
    half_out = _final_add("final_add_w_out", chip_sum_out, arrived[4], chip_index, core_index)
    dproj, db_in = _assemble_dproj(dq_a[0], dkv_a[0], dqs, dks, dvs)

    tmw = d_in // 2 if (d_in // 2) % 128 == 0 else d_in
    tnw = _tile(d, 1024)
    tk_s = _tile(s, 1024)
    a_spec, b_spec = _mm_specs("tn", tmw, tnw, tk_s)
    dw_in_t = _matmul("in_proj_dw", dproj, h1, [], mode="tn", grid=(d_in // tmw, d // tnw, s // tk_s),
                      a_spec=a_spec, b_spec=b_spec, extra_specs=[],
                      out_shapes=[jax.ShapeDtypeStruct((d_in, d), wire)],
                      out_specs=[pl.BlockSpec((tmw, tnw), lambda i, j, k: (i, j))],
                      epilogue=lambda acc: (acc,), acc_shape=(tmw, tnw))[0]

    tn = _tile(d, 1024)
    a_spec, b_spec = _mm_specs("nn", tm, tn, d_in)
    tile_mn = pl.BlockSpec((tm, tn), lambda i, j, k: (i, j))
    (dh1,), (sib_in,) = _matmul("in_proj_dx", dproj, w_in_t, [], mode="nn", grid=(s // tm, d // tn, 1),
                                a_spec=a_spec, b_spec=b_spec, extra_specs=[],
                                out_shapes=[jax.ShapeDtypeStruct((s, d), F32)], out_specs=[tile_mn],
                                epilogue=lambda acc: (acc,), hook=_exchange_hook(dw_in_t))
    chip_sum_in = _chip_add("chip_add_w_in", dw_in_t, sib_in, core_index)

    (grad_x, _, dg_attn), (chips_in,) = _norm_bwd("norm_attn_bwd", dh1, x2d, r1, g_attn, dx2,
                                                  hook=_scatter_hook(chip_sum_in))
    half_in = _final_add("final_add_w_in", chip_sum_in, chips_in, chip_index, core_index)

    small_parts = [dg_attn, db_in, dsinks[:, :n_heads_a], dg_out_a, dg_out_b, dg_mlp, dg_final]
    small_shapes = [g_attn.shape, b_in.shape, sinks_a.shape, g_out_a.shape, g_out_b.shape, g_mlp.shape,
                    g_final.shape]
    n_small = sum(int(np.prod(shp)) for shp in small_shapes)
    rows_s = -(-n_small // (8 * 128)) * 8
    (gw_in_t, gw_out, gw_1, gw_2), small_sum = _share_halves(
        [half_in, half_out, half_1, half_2], _pack_small(small_parts, rows_s))
    gw_in = gw_in_t.T

    upd_in = _adamw("adamw_w_in", w_in[0], gw_in, m_w_in[0], v_w_in[0])
    upd_out = _adamw("adamw_w_out", w_out[0], gw_out, m_w_out[0], v_w_out[0])
    upd_1 = _adamw("adamw_w_1", w_1[0], gw_1, m_w_1[0], v_w_1[0])
    upd_2 = _adamw("adamw_w_2", w_2[0], gw_2, m_w_2[0], v_w_2[0])
    small_w = [g_attn, b_in, sinks_a, g_out_a, g_out_b, g_mlp, g_final]
    small_m = [m_g_attn, m_b_in, m_sinks_a, m_g_out_a, m_g_out_b, m_g_mlp, m_g_final]
    small_v = [v_g_attn, v_b_in, v_sinks_a, v_g_out_a, v_g_out_b, v_g_mlp, v_g_final]
    upd_small = _adamw("adamw_small", _pack_small(small_w, rows_s), small_sum,
                       _pack_small(small_m, rows_s), _pack_small(small_v, rows_s))
    d_small, m_small, v_small, g_small = [_unpack_small(t, small_shapes) for t in upd_small]

    loss = lax.psum(loss_part[0, 0], ("x", "y", "c"))

    def ordered(small, big):
        w_in_v, w_out_v, w_1_v, w_2_v = big
        return [small[0], w_in_v[None], small[1], small[2], small[3], small[4], w_out_v[None], small[5],
                w_1_v[None], w_2_v[None], small[6]]

    grads = ordered(g_small, (upd_in[3], upd_out[3], upd_1[3], upd_2[3]))
    deltas = ordered(d_small, (upd_in[0], upd_out[0], upd_1[0], upd_2[0]))
    new_m = ordered(m_small, (upd_in[1], upd_out[1], upd_1[1], upd_2[1]))
    new_v = ordered(v_small, (upd_in[2], upd_out[2], upd_1[2], upd_2[2]))
    return (loss, grad_x[None], *grads, *deltas, *new_m, *new_v)
```

```python
import jax
import jax.numpy as jnp
import numpy as np
from jax import lax
from jax.experimental import pallas as pl
from jax.experimental.pallas import tpu as pltpu

F32 = jnp.float32
BF16 = jnp.bfloat16

HEAD_DIM = 64
BLOCK = 128
PAIR = 2 * HEAD_DIM
N_KV_GROUPS = 2
WINDOW_A = 128
DILATED_BRANCHES = ((128, 1), (512, 4), (2048, 16))
EPS = 1e-5
NEG_INF = -1e30
ATT_SCALE = HEAD_DIM ** -0.5

ADAM_LR = 0.001
ADAM_B1 = 0.9
ADAM_B2 = 0.999
ADAM_EPS = 1e-08
ADAM_WD = 0.01
ADAM_STEP = 10

N_CHIPS = 4
N_DEV = 8
MESH = pl.DeviceIdType.MESH
GRAD_WIRE_DTYPE = jnp.bfloat16
BRANCH_DTYPE = jnp.bfloat16

VMEM_CAPACITY_V7X = 64 * 1024 * 1024
VMEM_LIMIT_MAX = VMEM_CAPACITY_V7X - 8 * 1024 * 1024
VMEM_LIMIT_MIN = VMEM_CAPACITY_V7X - 16 * 1024 * 1024

HBM_SPEC = pl.BlockSpec(memory_space=pltpu.HBM)
VMEM_SPEC = pl.BlockSpec(memory_space=pltpu.VMEM)
SMEM_SPEC = pl.BlockSpec(memory_space=pltpu.SMEM)


def _nbytes(shape, dtype):
    return int(np.prod([s for s in shape if s is not None])) * jnp.dtype(dtype).itemsize


def _params(semantics, block_bytes):
    limit = min(max(2 * block_bytes + (4 << 20), VMEM_LIMIT_MIN), VMEM_LIMIT_MAX)
    return pltpu.CompilerParams(dimension_semantics=semantics, vmem_limit_bytes=limit)


class _Hook:
    def __init__(self, operands, out_shape, sems, start, finish, mid=None, aliases=None,
                 mid_fraction=0.6):
        self.operands, self.out_shape, self.sems = list(operands), list(out_shape), list(sems)
        self.start, self.mid, self.finish = start, mid, finish
        self.aliases = dict(aliases or {})
        self.mid_fraction = mid_fraction


def _merge_hooks(hooks):
    hooks = [h for h in hooks if h is not None]
    if len(hooks) <= 1:
        return hooks[0] if hooks else None
    n_op = np.cumsum([0] + [len(h.operands) for h in hooks])
    n_out = np.cumsum([0] + [len(h.out_shape) for h in hooks])
    n_sem = np.cumsum([0] + [len(h.sems) for h in hooks])

    def run(which):
        def fn(ops, outs, sems):
            for i, h in enumerate(hooks):
                f = getattr(h, which)
                if f is not None:
                    f(ops[n_op[i]:n_op[i + 1]], outs[n_out[i]:n_out[i + 1]], sems[n_sem[i]:n_sem[i + 1]])
        return fn

    aliases = {}
    for i, h in enumerate(hooks):
        aliases.update({int(n_op[i]) + a: int(n_out[i]) + b for a, b in h.aliases.items()})
    return _Hook(sum([h.operands for h in hooks], []), sum([h.out_shape for h in hooks], []),
                 sum([h.sems for h in hooks], []), run("start"), run("finish"),
                 run("mid") if any(h.mid for h in hooks) else None, aliases)


def _call(body, hook, *, name, grid, in_specs, out_specs, out_shape, scratch_shapes=(), compiler_params):
    in_specs, out_specs, out_shape = list(in_specs), list(out_specs), list(out_shape)
    scratch_shapes = list(scratch_shapes)
    if hook is None:
        call = pl.pallas_call(body, name=name, grid=grid, in_specs=in_specs, out_specs=out_specs,
                              out_shape=out_shape, scratch_shapes=scratch_shapes,
                              compiler_params=compiler_params)
        return lambda *operands: (call(*operands), [])
    n_in, n_hin, n_out, n_hout, n_scr = (len(in_specs), len(hook.operands), len(out_specs),
                                         len(hook.out_shape), len(scratch_shapes))
    total = int(np.prod(grid))
    t_mid = min(int(total * hook.mid_fraction), total - 1)

    def wrapped(*refs):
        ins, h_in = refs[:n_in], refs[n_in:n_in + n_hin]
        o0 = n_in + n_hin
        outs, h_out = refs[o0:o0 + n_out], refs[o0 + n_out:o0 + n_out + n_hout]
        s0 = o0 + n_out + n_hout
        scr, h_sems = refs[s0:s0 + n_scr], refs[s0 + n_scr:]
        t = pl.program_id(0)
        for axis in range(1, len(grid)):
            t = t * grid[axis] + pl.program_id(axis)

        @pl.when(t == 0)
        def _():
            hook.start(h_in, h_out, h_sems)

        body(*ins, *outs, *scr)
        if hook.mid is not None:
            @pl.when(t == t_mid)
            def _():
                hook.mid(h_in, h_out, h_sems)

        @pl.when(t == total - 1)
        def _():
            hook.finish(h_in, h_out, h_sems)

    params = pltpu.CompilerParams(dimension_semantics=("arbitrary",) * len(grid),
                                  vmem_limit_bytes=compiler_params.vmem_limit_bytes)
    call = pl.pallas_call(
        wrapped, name=name, grid=grid,
        in_specs=in_specs + [HBM_SPEC] * n_hin, out_specs=out_specs + [HBM_SPEC] * n_hout,
        out_shape=out_shape + hook.out_shape, scratch_shapes=scratch_shapes + hook.sems,
        input_output_aliases={n_in + a: n_out + b for a, b in hook.aliases.items()},
        compiler_params=params)

    def run(*operands):
        res = call(*operands, *hook.operands)
        return res[:n_out], res[n_out:]

    return run


def _remote(src, dst, send_sem, recv_sem, device):
    return pltpu.make_async_remote_copy(src_ref=src, dst_ref=dst, send_sem=send_sem, recv_sem=recv_sem,
                                        device_id=device, device_id_type=MESH)


def alibi_slopes(n):
    return [float(v) for v in np.asarray(2.0 ** (-8.0 * (np.arange(n) + 1) / n), dtype=np.float32)]


def _matmul(name, a, b, extras, *, mode, grid, a_spec, b_spec, extra_specs, out_shapes, out_specs,
            epilogue, prologue=None, acc_shape=None, hook=None):
    dims = {"nn": ((1,), (0,)), "nt": ((1,), (1,)), "tn": ((0,), (0,))}[mode]
    nk = grid[2]
    n_ex, n_out = len(extras), len(out_shapes)

    def body(a_ref, b_ref, *rest):
        ex, outs = rest[:n_ex], rest[n_ex:n_ex + n_out]
        av = a_ref[...]
        if prologue is not None:
            av = prologue(av)
        part = lax.dot_general(av, b_ref[...], (dims, ((), ())), preferred_element_type=F32)

        def finish(acc):
            res = epilogue(acc, *[e[...] for e in ex])
            for o, r in zip(outs, res):
                o[...] = r.astype(o.dtype)

        if nk == 1:
            finish(part)
        else:
            acc_ref = rest[-1]
            k = pl.program_id(2)

            @pl.when(k == 0)
            def _():
                acc_ref[...] = part

            @pl.when(k > 0)
            def _():
                acc_ref[...] += part

            @pl.when(k == nk - 1)
            def _():
                finish(acc_ref[...])

    blocks = [(a_spec.block_shape, a.dtype), (b_spec.block_shape, b.dtype)]
    blocks += [(s.block_shape, e.dtype) for s, e in zip(extra_specs, extras)]
    blocks += [(s.block_shape, o.dtype) for s, o in zip(out_specs, out_shapes)]
    nbytes = sum(_nbytes(s, d) for s, d in blocks)
    scratch = []
    if nk > 1:
        scratch.append(pltpu.VMEM(acc_shape, F32))
        nbytes += _nbytes(acc_shape, F32)
    res, hook_res = _call(
        body, hook, name=name, grid=grid,
        in_specs=[a_spec, b_spec, *extra_specs], out_specs=list(out_specs), out_shape=list(out_shapes),
        scratch_shapes=scratch,
        compiler_params=_params(("parallel", "parallel", "arbitrary"), nbytes),
    )(a, b, *extras)
    return res if hook is None else (res, hook_res)


def _mm_specs(mode, tm, tn, tk, b_block=None, b_map=None):
    if mode == "tn":
        a_spec = pl.BlockSpec((tk, tm), lambda i, j, k: (k, i))
    else:
        a_spec = pl.BlockSpec((tm, tk), lambda i, j, k: (i, k))
    if b_block is not None:
        b_spec = pl.BlockSpec(b_block, b_map)
    elif mode == "nt":
        b_spec = pl.BlockSpec((tn, tk), lambda i, j, k: (j, k))
    else:
        b_spec = pl.BlockSpec((tk, tn), lambda i, j, k: (k, j))
    return a_spec, b_spec


def _project_by_class(name, h, w_t, bias, row_off, width, dilations, hook=None):
    s, d = h.shape
    tm = _tile(s, 2048)
    tn = 512 if width % 512 == 0 and row_off % 512 == 0 else _tile(width, 256)
    off = row_off // tn
    assert row_off % tn == 0 and tn % 128 == 0
    n_out = len(dilations)

    def body(h_ref, w_ref, b_ref, *rest):
        outs, perm_ref = rest[:n_out], rest[n_out]
        acc = lax.dot_general(h_ref[...], w_ref[...], (((1,), (1,)), ((), ())), preferred_element_type=F32)
        acc = acc + b_ref[...]
        for j in range(tn // 128):
            cols = slice(j * 128, (j + 1) * 128)
            _to_classes(outs, cols, acc[:, cols], perm_ref, dilations)

    blocks = tm * d * 2 + tn * d * 2 + 3 * tm * tn * 2 + tm * 128 * 4
    res, hook_res = _call(
        body, hook, name=name, grid=(s // tm, width // tn),
        in_specs=[pl.BlockSpec((tm, d), lambda i, j: (i, 0)), pl.BlockSpec((tn, d), lambda i, j: (j + off, 0)),
                  pl.BlockSpec((1, tn), lambda i, j: (0, j + off))],
        out_specs=[pl.BlockSpec((dil, tm // dil, tn), lambda i, j: (0, i, j)) for dil in dilations],
        out_shape=[_class_shape(dil, s, width, BF16) for dil in dilations],
        scratch_shapes=[pltpu.VMEM((tm, 128), F32)],
        compiler_params=_params(("parallel", "parallel"), blocks),
    )(h, w_t, bias)
    return res if hook is None else (res, hook_res)


def _tile(n, want):
    if n <= want:
        return n
    t = (want // 128) * 128
    while t > 128 and n % t:
        t -= 128
    assert n % t == 0, (n, want)
    return t


def _row_tile(s):
    return 256 if s % 256 == 0 else s


def _norm_fwd(name, x, g, hook=None):
    s, d = x.shape
    tm = _row_tile(s)

    def body(x_ref, g_ref, h_ref, r_ref):
        xv = x_ref[...]
        r = lax.rsqrt(jnp.mean(xv * xv, axis=-1, keepdims=True) + EPS)
        h_ref[...] = ((xv * r) * g_ref[...]).astype(BF16)
        r_ref[...] = r

    row = pl.BlockSpec((tm, d), lambda i: (i, 0))
    res, hook_res = _call(
        body, hook, name=name, grid=(s // tm,),
        in_specs=[row, pl.BlockSpec((1, d), lambda i: (0, 0))],
        out_specs=[row, pl.BlockSpec((tm, 1), lambda i: (i, 0))],
        out_shape=[jax.ShapeDtypeStruct((s, d), BF16), jax.ShapeDtypeStruct((s, 1), F32)],
        compiler_params=_params(("parallel",), tm * d * 6),
    )(x, g)
    return res if hook is None else (res, hook_res)


def _norm_bwd(name, dh, x, r, g, dres, hook=None):
    s, d = x.shape
    tm = _row_tile(s)

    def body(dh_ref, x_ref, r_ref, g_ref, dres_ref, dx_ref, dxb_ref, dg_ref):
        rv = r_ref[...]
        xn = x_ref[...] * rv
        dhv = dh_ref[...]
        dxn = dhv * g_ref[...]
        dx = dres_ref[...] + rv * (dxn - xn * jnp.mean(dxn * xn, axis=-1, keepdims=True))
        dx_ref[...] = dx
        dxb_ref[...] = dx.astype(BF16)
        part = jnp.sum(dhv * xn, axis=0, keepdims=True)

        @pl.when(pl.program_id(0) == 0)
        def _():
            dg_ref[...] = part

        @pl.when(pl.program_id(0) > 0)
        def _():
            dg_ref[...] += part

    row = pl.BlockSpec((tm, d), lambda i: (i, 0))
    vec = pl.BlockSpec((1, d), lambda i: (0, 0))
    res, hook_res = _call(
        body, hook, name=name, grid=(s // tm,),
        in_specs=[row, row, pl.BlockSpec((tm, 1), lambda i: (i, 0)), vec, row],
        out_specs=[row, row, vec],
        out_shape=[jax.ShapeDtypeStruct((s, d), F32), jax.ShapeDtypeStruct((s, d), BF16),
                   jax.ShapeDtypeStruct((1, d), F32)],
        compiler_params=_params(("arbitrary",), tm * d * 18),
    )(dh, x, r, g, dres)
    return res if hook is None else (res, hook_res)


def _loss_head(x3, target, g):
    s, d = x3.shape
    tm = _row_tile(s)

    def body(x_ref, t_ref, g_ref, dx_ref, dxb_ref, loss_ref, dg_ref):
        xv = x_ref[...]
        gv = g_ref[...]
        r = lax.rsqrt(jnp.mean(xv * xv, axis=-1, keepdims=True) + EPS)
        xn = xv * r
        err = xn * gv - t_ref[...]
        loss = 0.5 * jnp.sum(jnp.mean(err * err, axis=-1, keepdims=True), axis=0, keepdims=True)
        dy = err / d
        dxn = dy * gv
        dx = r * (dxn - xn * jnp.mean(dxn * xn, axis=-1, keepdims=True))
        dx_ref[...] = dx
        dxb_ref[...] = dx.astype(BF16)
        dg = jnp.sum(dy * xn, axis=0, keepdims=True)
        loss_row = jnp.broadcast_to(loss, (1, 128))

        @pl.when(pl.program_id(0) == 0)
        def _():
            dg_ref[...] = dg
            loss_ref[...] = loss_row

        @pl.when(pl.program_id(0) > 0)
        def _():
            dg_ref[...] += dg
            loss_ref[...] += loss_row

    row = pl.BlockSpec((tm, d), lambda i: (i, 0))
    vec = pl.BlockSpec((1, d), lambda i: (0, 0))
    return _call(
        body, None, name="loss_head", grid=(s // tm,),
        in_specs=[row, row, vec],
        out_specs=[row, row, pl.BlockSpec((1, 128), lambda i: (0, 0)), vec],
        out_shape=[jax.ShapeDtypeStruct((s, d), F32), jax.ShapeDtypeStruct((s, d), BF16),
                   jax.ShapeDtypeStruct((1, 128), F32), jax.ShapeDtypeStruct((1, d), F32)],
        compiler_params=_params(("arbitrary",), tm * d * 14),
    )(x3, target, g)[0]


def _low_lanes(rows):
    return lax.broadcasted_iota(jnp.int32, (rows, PAIR), 1) < HEAD_DIM


def _to_classes(dst_refs, cols, value, perm_ref, dils):
    rows = value.shape[0]
    if any(dil > 1 for dil in dils):
        perm_ref[...] = value
    for dst_ref, dil in zip(dst_refs, dils):
        if dil == 1:
            dst_ref[0, :, cols] = value.astype(dst_ref.dtype)
            continue
        for r in range(dil):
            dst_ref[r, :, cols] = perm_ref[pl.ds(r, rows // dil, stride=dil), :].astype(dst_ref.dtype)


def _from_classes(src_ref, cols, perm_ref, dil):
    if dil == 1:
        return src_ref[0, :, cols].astype(F32)
    rows = perm_ref.shape[0]
    for r in range(dil):
        perm_ref[pl.ds(r, rows // dil, stride=dil), :] = src_ref[r, :, cols].astype(F32)
    return perm_ref[...]


def _class_spec(dil, tm, width):
    return pl.BlockSpec((dil, tm // dil, width), lambda i: (0, i, 0))


def _class_shape(dil, s, width, dtype):
    return jax.ShapeDtypeStruct((dil, s // dil, width), dtype)


DILATIONS = tuple(d for _, d in DILATED_BRANCHES)


def _mix_fwd(oa, obs, lses, ga, gb):
    s, qa = oa.shape
    qb = obs[0].shape[2]
    tm = _row_tile(s)
    all_lanes = slice(0, 128)

    def body(oa_ref, o1_ref, o2_ref, o3_ref, l1_ref, l2_ref, l3_ref, ga_ref, gb_ref,
             mix_ref, ob_ref, t1_ref, t2_ref, t3_ref, ra_ref, rb_ref, perm_ref):
        oav = oa_ref[...]
        ra = lax.rsqrt(jnp.mean(oav * oav, axis=-1, keepdims=True) + EPS)
        ra_ref[...] = ra
        mix_ref[:, 0:qa] = ((oav * ra) * ga_ref[...]).astype(BF16)
        l1, l2, l3 = [_from_classes(l_ref, all_lanes, perm_ref, dil)
                      for l_ref, dil in zip((l1_ref, l2_ref, l3_ref), DILATIONS)]
        mx = jnp.maximum(jnp.maximum(l1, l2), l3)
        e1, e2, e3 = jnp.exp(l1 - mx), jnp.exp(l2 - mx), jnp.exp(l3 - mx)
        tot = e1 + e2 + e3
        lse = mx + jnp.log(tot)
        _to_classes((t1_ref, t2_ref, t3_ref), all_lanes, lse, perm_ref, DILATIONS)
        ws = (e1 / tot, e2 / tot, e3 / tot)
        low = _low_lanes(tm)
        ssq = jnp.zeros((tm, 1), F32)
        for i in range(qb // PAIR):
            sl = slice(i * PAIR, (i + 1) * PAIR)
            acc = jnp.zeros((tm, PAIR), F32)
            for w, o_ref, dil in zip(ws, (o1_ref, o2_ref, o3_ref), DILATIONS):
                wexp = jnp.where(low, w[:, 2 * i:2 * i + 1], w[:, 2 * i + 1:2 * i + 2])
                acc = acc + wexp * _from_classes(o_ref, sl, perm_ref, dil)
            ob_ref[:, sl] = acc
            ssq = ssq + jnp.sum(acc * acc, axis=-1, keepdims=True)
        rb = lax.rsqrt(ssq / qb + EPS)
        rb_ref[...] = rb
        mix_ref[:, qa:qa + qb] = ((ob_ref[...] * rb) * gb_ref[...]).astype(BF16)

    def row(w):
        return pl.BlockSpec((tm, w), lambda i: (i, 0))

    def vec(w):
        return pl.BlockSpec((1, w), lambda i: (0, 0))

    return _call(
        body, None, name="mix_fwd", grid=(s // tm,),
        in_specs=([row(qa)] + [_class_spec(d, tm, qb) for d in DILATIONS]
                  + [_class_spec(d, tm, 128) for d in DILATIONS] + [vec(qa), vec(qb)]),
        out_specs=([row(qa + qb), row(qb)] + [_class_spec(d, tm, 128) for d in DILATIONS] + [row(1), row(1)]),
        out_shape=([jax.ShapeDtypeStruct((s, qa + qb), BF16), jax.ShapeDtypeStruct((s, qb), F32)]
                   + [_class_shape(d, s, 128, F32) for d in DILATIONS]
                   + [jax.ShapeDtypeStruct((s, 1), F32), jax.ShapeDtypeStruct((s, 1), F32)]),
        scratch_shapes=[pltpu.VMEM((tm, 128), F32)],
        compiler_params=_params(("parallel",), tm * (qa + 4 * qb) * 4 + tm * (qa + qb) * 2 + tm * 4096),
    )(oa, *obs, *lses, ga, gb)[0]


def _head_rowsums(prod, rows):
    low = _low_lanes(rows)
    lane = lax.broadcasted_iota(jnp.int32, (rows, 128), 1)
    out = jnp.zeros((rows, 128), F32)
    for i in range(prod.shape[1] // PAIR):
        tile = prod[:, i * PAIR:(i + 1) * PAIR]
        lo = jnp.sum(jnp.where(low, tile, 0.0), axis=-1, keepdims=True)
        hi = jnp.sum(jnp.where(low, 0.0, tile), axis=-1, keepdims=True)
        out = jnp.where(lane == 2 * i, lo, out)
        out = jnp.where(lane == 2 * i + 1, hi, out)
    return out


def _mix_bwd(dmix, oa, ob, ra, rb, ga, gb, hook=None):
    s, qa = oa.shape
    qb = ob.shape[1]
    tm = _row_tile(s)

    def one(dy, o, r, g):
        xn = o * r
        dxn = dy * g
        do = r * (dxn - xn * jnp.mean(dxn * xn, axis=-1, keepdims=True))
        return do, jnp.sum(dy * xn, axis=0, keepdims=True), _head_rowsums(do * o, tm)

    def body(dmix_ref, oa_ref, ob_ref, ra_ref, rb_ref, ga_ref, gb_ref,
             doa_ref, dob1_ref, dob2_ref, dob3_ref, dla_ref, dlb1_ref, dlb2_ref, dlb3_ref,
             dga_ref, dgb_ref, perm_ref):
        doa, dga, dla = one(dmix_ref[:, 0:qa], oa_ref[...], ra_ref[...], ga_ref[...])
        dob, dgb, dlb = one(dmix_ref[:, qa:qa + qb], ob_ref[...], rb_ref[...], gb_ref[...])
        doa_ref[...] = doa.astype(BF16)
        dla_ref[...] = dla
        _to_classes((dlb1_ref, dlb2_ref, dlb3_ref), slice(0, 128), dlb, perm_ref, DILATIONS)
        for i in range(qb // PAIR):
            sl = slice(i * PAIR, (i + 1) * PAIR)
            _to_classes((dob1_ref, dob2_ref, dob3_ref), sl, dob[:, sl], perm_ref, DILATIONS)

        @pl.when(pl.program_id(0) == 0)
        def _():
            dga_ref[...] = dga
            dgb_ref[...] = dgb

        @pl.when(pl.program_id(0) > 0)
        def _():
            dga_ref[...] += dga
            dgb_ref[...] += dgb

    def row(w):
        return pl.BlockSpec((tm, w), lambda i: (i, 0))

    def vec(w):
        return pl.BlockSpec((1, w), lambda i: (0, 0))

    res, hook_res = _call(
        body, hook, name="mix_bwd", grid=(s // tm,),
        in_specs=[row(qa + qb), row(qa), row(qb), row(1), row(1), vec(qa), vec(qb)],
        out_specs=([row(qa)] + [_class_spec(d, tm, qb) for d in DILATIONS] + [row(128)]
                   + [_class_spec(d, tm, 128) for d in DILATIONS] + [vec(qa), vec(qb)]),
        out_shape=([jax.ShapeDtypeStruct((s, qa), BF16)] + [_class_shape(d, s, qb, BF16) for d in DILATIONS]
                   + [jax.ShapeDtypeStruct((s, 128), F32)] + [_class_shape(d, s, 128, F32) for d in DILATIONS]
                   + [jax.ShapeDtypeStruct((1, qa), F32), jax.ShapeDtypeStruct((1, qb), F32)]),
        scratch_shapes=[pltpu.VMEM((tm, 128), F32)],
        compiler_params=_params(("arbitrary",), tm * (qa + qb) * 16),
    )(dmix, oa, ob, ra, rb, ga, gb)
    return res if hook is None else (res, hook_res)


def _assemble_dproj(dqa, dkva, dqs, dks, dvs):
    s, qa = dqa.shape
    kva = dkva.shape[1]
    qb = dqs[0].shape[2]
    width = qa + kva + 3 * qb
    tm = _row_tile(s)

    def body(dqa_ref, dkva_ref, q1, q2, q3, k1, k2, k3, v1, v2, v3, dp_ref, db_ref, perm_ref):
        first = pl.program_id(0) == 0

        def emit(off, val):
            dp_ref[:, off:off + PAIR] = val.astype(BF16)
            col = jnp.sum(val, axis=0, keepdims=True)

            @pl.when(first)
            def _():
                db_ref[:, off:off + PAIR] = col

            @pl.when(jnp.logical_not(first))
            def _():
                db_ref[:, off:off + PAIR] += col

        for i in range(qa // PAIR):
            emit(i * PAIR, dqa_ref[:, i * PAIR:(i + 1) * PAIR])
        for i in range(kva // PAIR):
            emit(qa + i * PAIR, dkva_ref[:, i * PAIR:(i + 1) * PAIR])
        for j, branch_refs in enumerate(((q1, q2, q3), (k1, k2, k3), (v1, v2, v3))):
            for i in range(qb // PAIR):
                sl = slice(i * PAIR, (i + 1) * PAIR)
                total = None
                for ref, dil in zip(branch_refs, DILATIONS):
                    val = _from_classes(ref, sl, perm_ref, dil)
                    total = val if total is None else total + val
                emit(qa + kva + j * qb + i * PAIR, total)

    def row(w):
        return pl.BlockSpec((tm, w), lambda i: (i, 0))

    return _call(
        body, None, name="assemble_dproj", grid=(s // tm,),
        in_specs=[row(qa), row(kva)] + [_class_spec(d, tm, qb) for d in DILATIONS] * 3,
        out_specs=[row(width), pl.BlockSpec((1, width), lambda i: (0, 0))],
        out_shape=[jax.ShapeDtypeStruct((s, width), BF16), jax.ShapeDtypeStruct((1, width), F32)],
        scratch_shapes=[pltpu.VMEM((tm, 128), F32)],
        compiler_params=_params(("arbitrary",), tm * (qa + kva + 9 * qb) * 4 + tm * width * 2),
    )(dqa, dkva, *dqs, *dks, *dvs)[0]


def _fill_bias(bias_ref, n_pairs, max_steps, dil, slopes, sink_ref=None):
    qi = lax.broadcasted_iota(jnp.int32, (BLOCK, 2 * BLOCK), 0)
    kj = lax.broadcasted_iota(jnp.int32, (BLOCK, 2 * BLOCK), 1)
    steps = qi + BLOCK - kj
    dist = (steps * dil).astype(F32)
    band = (steps >= 0) & (steps <= max_steps)
    assert sink_ref is None or max_steps < BLOCK
    for first in (0, 1):
        valid = band & (kj >= BLOCK) if first else band
        for i in range(n_pairs):
            tables = []
            for half in (0, 1):
                table = jnp.where(valid, -(slopes[2 * i + half] * dist), NEG_INF)
                if sink_ref is not None:
                    table = jnp.where(kj == 0, sink_ref[2 * i + half], table)
                tables.append(table)
            bias_ref[first, i] = jnp.concatenate(tables, axis=0)


def _without_sink_row(tile):
    row = lax.broadcasted_iota(jnp.int32, tile.shape, 0)
    return jnp.where(row == 0, jnp.zeros_like(tile), tile)


def _bias_shape(n_pairs):
    return pltpu.VMEM((2, n_pairs, 2 * BLOCK, 2 * BLOCK), F32)


def _stack_heads(tile, low):
    zero = jnp.zeros_like(tile)
    return jnp.concatenate([jnp.where(low, tile, zero), jnp.where(low, zero, tile)], axis=0)


def _unstack_heads(stacked, low):
    return jnp.where(low, stacked[0:BLOCK], stacked[BLOCK:2 * BLOCK])


def _head_columns(ref, i):
    return jnp.concatenate([ref[:, 2 * i:2 * i + 1], ref[:, 2 * i + 1:2 * i + 2]], axis=0)


def _swap_halves(t):
    return pltpu.roll(t, HEAD_DIM, 1)


def _dup_group(t_bf16, group):
    t = t_bf16.astype(F32)
    low = lax.broadcasted_iota(jnp.int32, t.shape, 1) < HEAD_DIM
    keep = low if group == 0 else jnp.logical_not(low)
    return jnp.where(keep, t, _swap_halves(t)).astype(BF16)


def _attn_fwd(name, q, kv, *, dil, max_steps, slopes, sinks=None, hook=None):
    grouped = sinks is not None
    _, length, w = q.shape
    n_pairs = w // PAIR
    nb = length // BLOCK
    heads_per_group = 2 * n_pairs // N_KV_GROUPS

    def body(*refs):
        if grouped:
            sink_ref, q_ref, kvp_ref, kvc_ref, o_ref, lse_ref, bias_ref = refs
        else:
            q_ref, kp_ref, kc_ref, vp_ref, vc_ref, o_ref, lse_ref, bias_ref = refs
        n = pl.program_id(1)

        @pl.when((pl.program_id(0) == 0) & (n == 0))
        def _():
            _fill_bias(bias_ref, n_pairs, max_steps, dil, slopes, sink_ref if grouped else None)

        first = (n == 0).astype(jnp.int32)
        low = _low_lanes(BLOCK)
        lane = lax.broadcasted_iota(jnp.int32, (BLOCK, 128), 1)
        lse_acc = jnp.zeros((BLOCK, 128), F32)
        if grouped:
            kv_all = jnp.concatenate([kvp_ref[...], kvc_ref[...]], axis=0)
            k_dup = [_without_sink_row(_dup_group(kv_all[:, 0:PAIR], g)) for g in range(N_KV_GROUPS)]
            v_dup = [_without_sink_row(_dup_group(kv_all[:, PAIR:2 * PAIR], g)) for g in range(N_KV_GROUPS)]
        for i in range(n_pairs):
            sl = slice(i * PAIR, (i + 1) * PAIR)
            qs = _stack_heads(q_ref[:, sl] * ATT_SCALE, low)
            if grouped:
                kk, vv = k_dup[2 * i // heads_per_group], v_dup[2 * i // heads_per_group]
            else:
                kk = jnp.concatenate([kp_ref[:, sl], kc_ref[:, sl]], axis=0)
                vv = jnp.concatenate([vp_ref[:, sl], vc_ref[:, sl]], axis=0)
            sc = lax.dot_general(qs, kk, (((1,), (1,)), ((), ())), preferred_element_type=F32)
            sc = sc + bias_ref[first, i]
            m = jnp.max(sc, axis=-1, keepdims=True)
            p = jnp.exp(sc - m)
            den = jnp.sum(p, axis=-1, keepdims=True)
            o = jnp.dot(p.astype(BF16), vv, preferred_element_type=F32) / den
            o_ref[:, sl] = _unstack_heads(o, low).astype(o_ref.dtype)
            lse = m + jnp.log(den)
            lse_acc = jnp.where(lane == 2 * i, lse[0:BLOCK], lse_acc)
            lse_acc = jnp.where(lane == 2 * i + 1, lse[BLOCK:2 * BLOCK], lse_acc)
        lse_ref[...] = lse_acc

    def cur(width):
        return pl.BlockSpec((None, BLOCK, width), lambda r, n: (r, n, 0))

    def prev(width):
        return pl.BlockSpec((None, BLOCK, width), lambda r, n: (r, jnp.maximum(n - 1, 0), 0))

    if grouped:
        kvw = kv.shape[2]
        operands = [sinks, q, kv, kv]
        in_specs = [SMEM_SPEC, cur(w), prev(kvw), cur(kvw)]
    else:
        operands = [q, kv[0], kv[0], kv[1], kv[1]]
        in_specs = [cur(w), prev(w), cur(w), prev(w), cur(w)]
    res, hook_res = _call(
        body, hook, name=name, grid=(dil, nb), in_specs=in_specs,
        out_specs=[cur(w), cur(128)],
        out_shape=[jax.ShapeDtypeStruct((dil, length, w), F32 if grouped else BRANCH_DTYPE),
                   jax.ShapeDtypeStruct((dil, length, 128), F32)],
        scratch_shapes=[_bias_shape(n_pairs)],
        compiler_params=_params(("arbitrary", "arbitrary"), BLOCK * w * 16 + n_pairs * BLOCK * BLOCK * 16),
    )(*operands)
    return res if hook is None else (res, hook_res)


def _attn_bwd(name, q, kv, do, lse, delta, *, dil, max_steps, slopes, sinks=None, hook=None):
    grouped = sinks is not None
    _, length, w = q.shape
    n_pairs = w // PAIR
    nb = length // BLOCK
    heads_per_group = 2 * n_pairs // N_KV_GROUPS
    pairs_per_group = n_pairs // N_KV_GROUPS

    def body(*refs):
        if grouped:
            (sink_ref, q_ref, kvp_ref, kvc_ref, do_ref, lse_ref, dl_ref,
             dq_ref, dkv_ref, dsink_ref, acc_ref, bias_ref) = refs
        else:
            (q_ref, kp_ref, kc_ref, vp_ref, vc_ref, do_ref, lse_ref, dl_ref,
             dq_ref, dk_ref, dv_ref, acck_ref, accv_ref, bias_ref) = refs
        n = pl.program_id(1)

        @pl.when((pl.program_id(0) == 0) & (n == 0))
        def _():
            _fill_bias(bias_ref, n_pairs, max_steps, dil, slopes, sink_ref if grouped else None)

        @pl.when(n == 0)
        def _():
            if grouped:
                acc_ref[...] = jnp.zeros_like(acc_ref)

                @pl.when(pl.program_id(0) == 0)
                def _():
                    dsink_ref[...] = jnp.zeros_like(dsink_ref)
            else:
                acck_ref[...] = jnp.zeros_like(acck_ref)
                accv_ref[...] = jnp.zeros_like(accv_ref)

        @pl.when(n == nb)
        def _():
            if grouped:
                dkv_ref[...] = acc_ref[...]
            else:
                dk_ref[...] = acck_ref[...].astype(dk_ref.dtype)
                dv_ref[...] = accv_ref[...].astype(dv_ref.dtype)

        @pl.when(n < nb)
        def _():
            first = (n == 0).astype(jnp.int32)
            low = _low_lanes(BLOCK)
            low_kv = _low_lanes(2 * BLOCK)
            lane1 = lax.broadcasted_iota(jnp.int32, (1, 128), 1)
            if grouped:
                kv_all = jnp.concatenate([kvp_ref[...], kvc_ref[...]], axis=0)
                k_dup = [_without_sink_row(_dup_group(kv_all[:, 0:PAIR], g)) for g in range(N_KV_GROUPS)]
                v_dup = [_without_sink_row(_dup_group(kv_all[:, PAIR:2 * PAIR], g)) for g in range(N_KV_GROUPS)]
                dk_grp =[jnp.zeros((2 * BLOCK, PAIR), F32) for _ in range(N_KV_GROUPS)]
                dv_grp = [jnp.zeros((2 * BLOCK, PAIR), F32) for _ in range(N_KV_GROUPS)]
                dsink = jnp.zeros((1, 128), F32)
            for i in range(n_pairs):
                sl = slice(i * PAIR, (i + 1) * PAIR)
                qs = _stack_heads(q_ref[:, sl] * ATT_SCALE, low)
                dos = _stack_heads(do_ref[:, sl], low)
                if grouped:
                    grp = 2 * i // heads_per_group
                    kk, vv = k_dup[grp], v_dup[grp]
                else:
                    kk = jnp.concatenate([kp_ref[:, sl], kc_ref[:, sl]], axis=0)
                    vv = jnp.concatenate([vp_ref[:, sl], vc_ref[:, sl]], axis=0)
                lse_col = _head_columns(lse_ref, i)
                dl_col = _head_columns(dl_ref, i)
                sc = lax.dot_general(qs, kk, (((1,), (1,)), ((), ())), preferred_element_type=F32)
                p = jnp.exp(sc + bias_ref[first, i] - lse_col)
                dp = lax.dot_general(dos, vv, (((1,), (1,)), ((), ())), preferred_element_type=F32)
                ds_f32 = p * (dp - dl_col)
                ds = ds_f32.astype(BF16)
                dq = jnp.dot(ds, kk, preferred_element_type=F32)
                dkk = lax.dot_general(ds, qs, (((0,), (0,)), ((), ())), preferred_element_type=F32)
                dvv = lax.dot_general(p.astype(BF16), dos, (((0,), (0,)), ((), ())),
                                      preferred_element_type=F32)
                if grouped:
                    for half in (0, 1):
                        contrib = jnp.sum(ds_f32[half * BLOCK:(half + 1) * BLOCK, 0:1], axis=0, keepdims=True)
                        dsink = jnp.where(lane1 == 2 * i + half, dsink + contrib, dsink)
                dq_ref[:, sl] = (_unstack_heads(dq, low) * ATT_SCALE).astype(dq_ref.dtype)
                if grouped:
                    dk_grp[grp] = dk_grp[grp] + dkk
                    dv_grp[grp] = dv_grp[grp] + dvv
                else:
                    dk_ref[:, sl] = (acck_ref[:, sl] + dkk[0:BLOCK]).astype(dk_ref.dtype)
                    acck_ref[:, sl] = dkk[BLOCK:2 * BLOCK]
                    dv_ref[:, sl] = (accv_ref[:, sl] + dvv[0:BLOCK]).astype(dv_ref.dtype)
                    accv_ref[:, sl] = dvv[BLOCK:2 * BLOCK]
            if grouped:
                folded = [_without_sink_row(t + _swap_halves(t)) for t in dk_grp + dv_grp]
                dk_tile = jnp.where(low_kv, folded[0], folded[1])
                dv_tile = jnp.where(low_kv, folded[2], folded[3])
                part = jnp.concatenate([dk_tile, dv_tile], axis=1)
                dkv_ref[...] = acc_ref[...] + part[0:BLOCK]
                acc_ref[...] = part[BLOCK:2 * BLOCK]
                dsink_ref[...] += dsink

    last = nb - 1

    def cur(width):
        return pl.BlockSpec((None, BLOCK, width), lambda r, n: (r, jnp.minimum(n, last), 0))

    def prev(width):
        return pl.BlockSpec((None, BLOCK, width),
                            lambda r, n: (r, jnp.maximum(jnp.minimum(n, last) - 1, 0), 0))

    def done(width):
        return pl.BlockSpec((None, BLOCK, width), lambda r, n: (r, jnp.maximum(n - 1, 0), 0))

    if grouped:
        assert pairs_per_group * N_KV_GROUPS == n_pairs and heads_per_group % 2 == 0
        kvw = kv.shape[2]
        operands = [sinks, q, kv, kv, do, lse, delta]
        in_specs = [SMEM_SPEC, cur(w), prev(kvw), cur(kvw), cur(w), cur(128), cur(128)]
        out_specs = [cur(w), done(kvw), pl.BlockSpec((1, 128), lambda r, n: (0, 0))]
        out_shape = [jax.ShapeDtypeStruct((dil, length, w), F32), jax.ShapeDtypeStruct((dil, length, kvw), F32),
                     jax.ShapeDtypeStruct((1, 128), F32)]
        scratch = [pltpu.VMEM((BLOCK, kvw), F32), _bias_shape(n_pairs)]
    else:
        operands = [q, kv[0], kv[0], kv[1], kv[1], do, lse, delta]
        in_specs = [cur(w), prev(w), cur(w), prev(w), cur(w), cur(w), cur(128), cur(128)]
        out_specs = [cur(w), done(w), done(w)]
        out_shape = [jax.ShapeDtypeStruct((dil, length, w), BRANCH_DTYPE)] * 3
        scratch = [pltpu.VMEM((BLOCK, w), F32), pltpu.VMEM((BLOCK, w), F32), _bias_shape(n_pairs)]
    res, hook_res = _call(
        body, hook, name=name, grid=(dil, nb + 1), in_specs=in_specs, out_specs=out_specs,
        out_shape=out_shape, scratch_shapes=scratch,
        compiler_params=_params(("arbitrary", "arbitrary"), BLOCK * w * 32 + n_pairs * BLOCK * BLOCK * 16),
    )(*operands)
    return res if hook is None else (res, hook_res)


def _adamw(name, w, g, m, v):
    rows, cols = w.shape
    tm = 256 if rows % 256 == 0 else rows

    def body(w_ref, g_ref, m_ref, v_ref, d_ref, nm_ref, nv_ref, g_out_ref):
        gv = g_ref[...]
        mn = ADAM_B1 * m_ref[...] + (1.0 - ADAM_B1) * gv
        vn = ADAM_B2 * v_ref[...] + (1.0 - ADAM_B2) * (gv * gv)
        m_hat = mn / (1.0 - ADAM_B1 ** ADAM_STEP)
        v_hat = vn / (1.0 - ADAM_B2 ** ADAM_STEP)
        d_ref[...] = -ADAM_LR * (m_hat / (jnp.sqrt(v_hat) + ADAM_EPS) + ADAM_WD * w_ref[...])
        nm_ref[...] = mn
        nv_ref[...] = vn
        g_out_ref[...] = gv

    spec = pl.BlockSpec((tm, cols), lambda i: (i, 0))
    return _call(
        body, None, name=name, grid=(rows // tm,), in_specs=[spec] * 4, out_specs=[spec] * 4,
        out_shape=[jax.ShapeDtypeStruct(w.shape, F32)] * 4,
        compiler_params=_params(("parallel",), tm * cols * 32),
    )(w, g, m, v)[0]


def _mesh_position():
    return lax.axis_index("x"), lax.axis_index("y"), lax.axis_index("c")


def _other_chips(x, y):
    return [(1 - x, y), (x, 1 - y), (1 - x, 1 - y)]


def _gather_hook(gathered, lo, hi, mid_fraction=0.6):
    rows, cols = gathered.shape[0] // N_CHIPS, gathered.shape[1]
    half, n = rows // 2, hi - lo
    assert lo % 16 == 0 and n % 16 == 0 and half % 16 == 0

    def region(out, owner_chip, which_half):
        return out.at[pl.ds(pl.multiple_of(owner_chip * rows + which_half * half + lo, 16), n)]

    def parts(outs, sems):
        x, y, c = _mesh_position()
        return outs[0], sems, c, 2 * x + y, (x, y, 1 - c), _other_chips(x, y)

    def start(ops, outs, sems):
        out, (send, recv, fsend, frecv), c, chip, sibling, others = parts(outs, sems)
        mine = region(out, chip, c)
        for k, (px, py) in enumerate(others):
            _remote(mine, mine, send.at[k], recv.at[k], (px, py, c)).start()

    def mid(ops, outs, sems):
        out, (send, recv, fsend, frecv), c, chip, sibling, others = parts(outs, sems)
        for k, (px, py) in enumerate(others):
            landed = region(out, 2 * px + py, c)
            _remote(landed, landed, send.at[k], recv.at[k], (px, py, c)).wait_recv()
            _remote(landed, landed, fsend.at[k], frecv.at[k], sibling).start()

    def finish(ops, outs, sems):
        out, (send, recv, fsend, frecv), c, chip, sibling, others = parts(outs, sems)
        mine = region(out, chip, c)
        for k, (px, py) in enumerate(others):
            passed = region(out, 2 * px + py, 1 - c)
            _remote(passed, passed, fsend.at[k], frecv.at[k], sibling).wait_recv()
        for k, (px, py) in enumerate(others):
            landed = region(out, 2 * px + py, c)
            _remote(landed, landed, fsend.at[k], frecv.at[k], sibling).wait_send()
            _remote(mine, mine, send.at[k], recv.at[k], (px, py, c)).wait_send()

    return _Hook([gathered], [jax.ShapeDtypeStruct(gathered.shape, gathered.dtype)],
                 [pltpu.SemaphoreType.DMA((3,))] * 4, start, finish, mid, aliases={0: 0},
                 mid_fraction=mid_fraction)


def _own_shard_in_place(name, shard, chip):
    rows, cols = shard.shape
    tr = next(t for t in (544, 512, 320, 256, 128, 64, 32, 16) if rows % t == 0)

    def body(chip_ref, w_ref, o_ref):
        o_ref[...] = w_ref[...].astype(BF16)

    return pl.pallas_call(
        body, name=name,
        grid_spec=pltpu.PrefetchScalarGridSpec(
            num_scalar_prefetch=1, grid=(rows // tr,),
            in_specs=[pl.BlockSpec((tr, cols), lambda i, chip_ref: (i, 0))],
            out_specs=pl.BlockSpec((tr, cols), lambda i, chip_ref: (chip_ref[0] * (rows // tr) + i, 0))),
        out_shape=jax.ShapeDtypeStruct((N_CHIPS * rows, cols), BF16),
        compiler_params=_params(("parallel",), tr * cols * 6),
    )(chip, shard)


def _exchange_hook(grad):
    rows, cols = grad.shape[0] // N_CHIPS, grad.shape[1]
    half = rows // 2
    assert half % 16 == 0

    def copies(ops, outs, sems):
        x, y, c = _mesh_position()
        send, recv = sems
        return [_remote(ops[0].at[pl.ds(pl.multiple_of(k * rows + (1 - c) * half, 16), half)], outs[0].at[k],
                        send.at[k], recv.at[k], (x, y, 1 - c)) for k in range(N_CHIPS)]

    def start(ops, outs, sems):
        for cp in copies(ops, outs, sems):
            cp.start()

    def finish(ops, outs, sems):
        for cp in copies(ops, outs, sems):
            cp.wait_recv()
            cp.wait_send()

    return _Hook([grad], [jax.ShapeDtypeStruct((N_CHIPS, half, cols), grad.dtype)],
                 [pltpu.SemaphoreType.DMA((N_CHIPS,))] * 2, start, finish)


def _scatter_hook(chip_sum):
    _, half, cols = chip_sum.shape

    def copies(ops, outs, sems):
        x, y, c = _mesh_position()
        send, recv = sems
        return [_remote(ops[0].at[2 * px + py], outs[0].at[k], send.at[k], recv.at[k], (px, py, c))
                for k, (px, py) in enumerate(_other_chips(x, y))]

    def start(ops, outs, sems):
        for cp in copies(ops, outs, sems):
            cp.start()

    def finish(ops, outs, sems):
        for cp in copies(ops, outs, sems):
            cp.wait_recv()
            cp.wait_send()

    return _Hook([chip_sum], [jax.ShapeDtypeStruct((3, half, cols), chip_sum.dtype)],
                 [pltpu.SemaphoreType.DMA((3,))] * 2, start, finish)


def _sum_tile(half):
    return 256 if half % 256 == 0 else half


def _chip_add(name, grad, from_sibling, core):
    n_chips, half, cols = from_sibling.shape
    rows = 2 * half
    tr = _sum_tile(half)

    def body(core_ref, g_ref, s_ref, o_ref):
        o_ref[...] = (g_ref[...].astype(F32) + s_ref[...].astype(F32)).astype(o_ref.dtype)

    tile = pl.BlockSpec((None, tr, cols), lambda k, i, core_ref: (k, i, 0))
    return pl.pallas_call(
        body, name=name,
        grid_spec=pltpu.PrefetchScalarGridSpec(
            num_scalar_prefetch=1, grid=(n_chips, half // tr),
            in_specs=[pl.BlockSpec((tr, cols), lambda k, i, core_ref:
                                   (k * (rows // tr) + core_ref[0] * (half // tr) + i, 0)), tile],
            out_specs=tile),
        out_shape=jax.ShapeDtypeStruct(from_sibling.shape, from_sibling.dtype),
        compiler_params=_params(("parallel", "parallel"), 3 * tr * cols * 4),
    )(core, grad, from_sibling)


def _final_add(name, chip_sum, from_chips, chip, core):
    _, half, cols = chip_sum.shape
    tr = _sum_tile(half)

    def body(chip_ref, core_ref, own_ref, others_ref, o_ref):
        total = own_ref[...].astype(F32)
        for k in range(3):
            total = total + others_ref[k].astype(F32)
        o_ref[...] = total

    return pl.pallas_call(
        body, name=name,
        grid_spec=pltpu.PrefetchScalarGridSpec(
            num_scalar_prefetch=2, grid=(half // tr,),
            in_specs=[pl.BlockSpec((None, tr, cols), lambda i, chip_ref, core_ref: (chip_ref[0], i, 0)),
                      pl.BlockSpec((3, tr, cols), lambda i, chip_ref, core_ref: (0, i, 0))],
            out_specs=pl.BlockSpec((tr, cols),
                                   lambda i, chip_ref, core_ref: (core_ref[0] * (half // tr) + i, 0))),
        out_shape=jax.ShapeDtypeStruct((2 * half, cols), F32),
        compiler_params=_params(("parallel",), 6 * tr * cols * 4),
    )(chip, core, chip_sum, from_chips)


def _share_halves(shards, small):
    n_s = len(shards)
    rows_s = small.shape[0]

    def body(*refs):
        small_ref = refs[n_s]
        outs, small_out = refs[n_s + 1:2 * n_s + 1], refs[2 * n_s + 1]
        small_all, send, recv, small_send, small_recv = refs[2 * n_s + 2:]
        x, y, c = _mesh_position()
        me = 4 * x + 2 * y + c
        sibling = (x, y, 1 - c)
        pending = []
        for i in range(n_s):
            half = shards[i].shape[0] // 2
            mine = outs[i].at[pl.ds(pl.multiple_of(c * half, 16), half)]
            cp = _remote(mine, mine, send.at[i], recv.at[i], sibling)
            cp.start()
            pending.append(cp)
        small_all[me] = small_ref[...]
        for j in range(N_DEV - 1):
            peer = (me + 1 + j) % N_DEV
            cp = _remote(small_all.at[me], small_all.at[me], small_send.at[j], small_recv.at[j],
                         (peer // 4, (peer // 2) % 2, peer % 2))
            cp.start()
            pending.append(cp)
        for i in range(n_s):
            half = shards[i].shape[0] // 2
            theirs = outs[i].at[pl.ds(pl.multiple_of((1 - c) * half, 16), half)]
            _remote(theirs, theirs, send.at[i], recv.at[i], sibling).wait_recv()
        for j in range(N_DEV - 1):
            peer = (me + N_DEV - 1 - j) % N_DEV
            _remote(small_all.at[peer], small_all.at[peer], small_send.at[j], small_recv.at[j],
                    sibling).wait_recv()
        total = small_all[0]
        for dev in range(1, N_DEV):
            total = total + small_all[dev]
        small_out[...] = total
        for cp in pending:
            cp.wait_send()

    res = pl.pallas_call(
        body, name="share_halves",
        in_specs=[HBM_SPEC] * n_s + [VMEM_SPEC], out_specs=[HBM_SPEC] * n_s + [VMEM_SPEC],
        out_shape=[jax.ShapeDtypeStruct(sh.shape, sh.dtype) for sh in shards]
        + [jax.ShapeDtypeStruct((rows_s, 128), F32)],
        scratch_shapes=[pltpu.VMEM((N_DEV, rows_s, 128), F32),
                        pltpu.SemaphoreType.DMA((n_s,)), pltpu.SemaphoreType.DMA((n_s,)),
                        pltpu.SemaphoreType.DMA((N_DEV - 1,)), pltpu.SemaphoreType.DMA((N_DEV - 1,))],
        input_output_aliases={i: i for i in range(n_s)},
    )(*shards, small)
    return res[:n_s], res[n_s]


def _pack_small(parts, rows):
    flat = jnp.concatenate([p.reshape(-1) for p in parts])
    flat = jnp.pad(flat, (0, rows * 128 - flat.shape[0]))
    return flat.reshape(rows, 128)


def _unpack_small(packed, shapes):
    flat = packed.reshape(-1)
    out, off = [], 0
    for shp in shapes:
        n = int(np.prod(shp))
        out.append(flat[off:off + n].reshape(shp))
        off += n
    return out


def kernel(x, g_attn, w_in, b_in, sinks_a, g_out_a, g_out_b, w_out, g_mlp, w_1, w_2, g_final, loss_target, m_g_attn, m_w_in, m_b_in, m_sinks_a, m_g_out_a, m_g_out_b, m_w_out, m_g_mlp, m_w_1, m_w_2, m_g_final, v_g_attn, v_w_in, v_b_in, v_sinks_a, v_g_out_a, v_g_out_b, v_w_out, v_g_mlp, v_w_1, v_w_2, v_g_final):
    s, d = x.shape[1], x.shape[2]
    d_in = b_in.shape[1]
    qa = g_out_a.shape[1]
    qb = g_out_b.shape[1]
    kva = 2 * N_KV_GROUPS * HEAD_DIM
    assert d_in == qa + kva + 3 * qb and qa + qb == w_out.shape[1] * N_CHIPS
    d_ff = w_1.shape[2] * N_CHIPS
    ff_shard = w_1.shape[2]
    n_heads_a, n_heads_b = qa // HEAD_DIM, qb // HEAD_DIM
    slopes_a, slopes_b = alibi_slopes(n_heads_a), alibi_slopes(n_heads_b)

    x2d = x[0]
    target = loss_target[0]

    core_index = lax.axis_index("c").astype(jnp.int32).reshape(1)
    chip_index = (2 * lax.axis_index("x") + lax.axis_index("y")).astype(jnp.int32).reshape(1)
    shards = {"w_in": w_in[0].T, "w_out": w_out[0], "w_1": w_1[0], "w_2": w_2[0]}
    halves = [sh.shape[0] // 2 for sh in shards.values()]
    w_in_t, w_out_g, w_1_g, w_2_g = [_own_shard_in_place(f"place_{n}", sh, chip_index)
                                     for n, sh in shards.items()]

    tm = _tile(s, 1024)

    (h1, r1), (w_in_t,) = _norm_fwd("norm_attn", x2d, g_attn,
                                    hook=_gather_hook(w_in_t, 0, halves[0], mid_fraction=1.0))

    q_a, = _project_by_class("proj_qa", h1, w_in_t, b_in, 0, qa, (1,))
    kv_a, = _project_by_class("proj_kva", h1, w_in_t, b_in, qa, kva, (1,))
    q_bs, (w_out_g,) = _project_by_class("proj_qb", h1, w_in_t, b_in, qa + kva, qb, DILATIONS,
                                         hook=_gather_hook(w_out_g, 0, halves[1]))
    k_bs = _project_by_class("proj_kb", h1, w_in_t, b_in, qa + kva + qb, qb, DILATIONS)
    v_bs = _project_by_class("proj_vb", h1, w_in_t, b_in, qa + kva + 2 * qb, qb, DILATIONS)

    quarter = halves[2] // 4
    sinks = sinks_a.reshape(-1)
    (o_a, lse_a), (w_1_g,) = _attn_fwd("attn_a_fwd", q_a, kv_a, dil=1, max_steps=WINDOW_A - 1, slopes=slopes_a,
                                       sinks=sinks, hook=_gather_hook(w_1_g, 0, quarter))
    o_a = o_a[0]
    o_bs, lse_bs = [], []
    for n, (window, dil) in enumerate(DILATED_BRANCHES):
        (o, l), (w_1_g,) = _attn_fwd(f"attn_b{dil}_fwd", q_bs[n], (k_bs[n], v_bs[n]), dil=dil,
                                     max_steps=window // dil, slopes=slopes_b,
                                     hook=_gather_hook(w_1_g, (n + 1) * quarter, (n + 2) * quarter))
        o_bs.append(o)
        lse_bs.append(l)
    w_1_g = w_1_g.reshape(N_CHIPS, d, ff_shard)
    mix, o_b, *lse_tot, r_a, r_b = _mix_fwd(o_a, o_bs, lse_bs, g_out_a, g_out_b)

    tn = _tile(d, 1024)
    a_spec, b_spec = _mm_specs("nn", tm, tn, d)
    tile_mn = pl.BlockSpec((tm, tn), lambda i, j, k: (i, j))
    x2 = _matmul("out_proj", mix, w_out_g, [x2d], mode="nn", grid=(s // tm, d // tn, 1),
                 a_spec=a_spec, b_spec=b_spec, extra_specs=[tile_mn],
                 out_shapes=[jax.ShapeDtypeStruct((s, d), F32)], out_specs=[tile_mn],
                 epilogue=lambda acc, res: (acc + res,))[0]

    h2, r2 = _norm_fwd("norm_mlp", x2, g_mlp)

    tn = _tile(ff_shard, 1024)
    per = ff_shard // tn
    a_spec, _ = _mm_specs("nn", tm, tn, d)
    tile_mn = pl.BlockSpec((tm, tn), lambda i, j, k: (i, j))
    (u,), (w_2_g,) = _matmul(
        "mlp_up", h2, w_1_g, [], mode="nn", grid=(s // tm, d_ff // tn, 1),
        a_spec=a_spec, b_spec=pl.BlockSpec((None, d, tn), lambda i, j, k: (j // per, 0, j % per)),
        extra_specs=[], out_shapes=[jax.ShapeDtypeStruct((s, d_ff), BF16)], out_specs=[tile_mn],
        epilogue=lambda acc: (jnp.maximum(acc, 0.0),),
        hook=_gather_hook(w_2_g, 0, halves[3]))

    tn = _tile(d, 1024)
    tk = _tile(d_ff, 2048)
    a_spec, b_spec = _mm_specs("nn", tm, tn, tk)
    tile_mn = pl.BlockSpec((tm, tn), lambda i, j, k: (i, j))
    x3 = _matmul("mlp_down", u, w_2_g, [x2], mode="nn", grid=(s // tm, d // tn, d_ff // tk),
                 a_spec=a_spec, b_spec=b_spec, extra_specs=[tile_mn],
                 out_shapes=[jax.ShapeDtypeStruct((s, d), F32)], out_specs=[tile_mn],
                 prologue=lambda a: a * a, epilogue=lambda acc, res: (acc + res,), acc_shape=(tm, tn))[0]

    dx3, dx3b, loss_part, dg_final = _loss_head(x3, target, g_final.reshape(1, d))

    tn = _tile(d_ff, 1024)
    a_spec, b_spec = _mm_specs("nt", tm, tn, d)
    tile_mn = pl.BlockSpec((tm, tn), lambda i, j, k: (i, j))
    dpre = _matmul("mlp_down_dx", dx3b, w_2_g, [u], mode="nt", grid=(s // tm, d_ff // tn, 1),
                   a_spec=a_spec, b_spec=b_spec, extra_specs=[tile_mn],
                   out_shapes=[jax.ShapeDtypeStruct((s, d_ff), BF16)], out_specs=[tile_mn],
                   epilogue=lambda acc, uu: (acc * (2.0 * uu.astype(F32)),))[0]

    wire = GRAD_WIRE_DTYPE
    tk_s = _tile(s, 2048)
    tmw = _tile(d_ff, 1024)
    a_spec, b_spec = _mm_specs("tn", tmw, d, tk_s)
    dw_2 = _matmul("mlp_down_dw", u, dx3b, [], mode="tn", grid=(d_ff // tmw, 1, s // tk_s),
                   a_spec=a_spec, b_spec=b_spec, extra_specs=[],
                   out_shapes=[jax.ShapeDtypeStruct((d_ff, d), wire)],
                   out_specs=[pl.BlockSpec((tmw, d), lambda i, j, k: (i, j))],
                   prologue=lambda a: a * a, epilogue=lambda acc: (acc,), acc_shape=(tmw, d))[0]

    tn = _tile(d, 1024)
    tk = _tile(ff_shard, 2048)
    per = ff_shard // tk
    a_spec, _ = _mm_specs("nt", tm, tn, tk)
    tile_mn = pl.BlockSpec((tm, tn), lambda i, j, k: (i, j))
    dh2 = _matmul("mlp_up_dx", dpre, w_1_g, [], mode="nt", grid=(s // tm, d // tn, d_ff // tk),
                  a_spec=a_spec, b_spec=pl.BlockSpec((None, tn, tk), lambda i, j, k: (k // per, j, k % per)),
                  extra_specs=[], out_shapes=[jax.ShapeDtypeStruct((s, d), F32)], out_specs=[tile_mn],
                  epilogue=lambda acc: (acc,), acc_shape=(tm, tn))[0]

    tmw = _tile(d, 1024)
    tnw = _tile(ff_shard, 2048)
    per = ff_shard // tnw
    a_spec, b_spec = _mm_specs("tn", tmw, tnw, tk_s)
    dw_1 = _matmul("mlp_up_dw", h2, dpre, [], mode="tn", grid=(d // tmw, d_ff // tnw, s // tk_s),
                   a_spec=a_spec, b_spec=b_spec, extra_specs=[],
                   out_shapes=[jax.ShapeDtypeStruct((N_CHIPS, d, ff_shard), wire)],
                   out_specs=[pl.BlockSpec((None, tmw, tnw), lambda i, j, k: (j // per, i, j % per))],
                   epilogue=lambda acc: (acc,), acc_shape=(tmw, tnw))[0]

    dw_1 = dw_1.reshape(N_CHIPS * d, ff_shard)
    (dx2, dx2b, dg_mlp), (sib_2, sib_1) = _norm_bwd(
        "norm_mlp_bwd", dh2, x2, r2, g_mlp, dx3, hook=_merge_hooks([_exchange_hook(dw_2), _exchange_hook(dw_1)]))
    chip_sum_2 = _chip_add("chip_add_w_2", dw_2, sib_2, core_index)
    chip_sum_1 = _chip_add("chip_add_w_1", dw_1, sib_1, core_index)

    tn = _tile(d, 1024)
    a_spec, b_spec = _mm_specs("nt", tm, tn, d)
    tile_mn = pl.BlockSpec((tm, tn), lambda i, j, k: (i, j))
    dmix = _matmul("out_proj_dx", dx2b, w_out_g, [], mode="nt", grid=(s // tm, d // tn, 1),
                   a_spec=a_spec, b_spec=b_spec, extra_specs=[],
                   out_shapes=[jax.ShapeDtypeStruct((s, d), F32)], out_specs=[tile_mn],
                   epilogue=lambda acc: (acc,))[0]

    tmw = _tile(d, 1024)
    a_spec, b_spec = _mm_specs("tn", tmw, d, tk_s)
    dw_out = _matmul("out_proj_dw", mix, dx2b, [], mode="tn", grid=(d // tmw, 1, s // tk_s),
                     a_spec=a_spec, b_spec=b_spec, extra_specs=[],
                     out_shapes=[jax.ShapeDtypeStruct((d, d), wire)],
                     out_specs=[pl.BlockSpec((tmw, d), lambda i, j, k: (i, j))],
                     epilogue=lambda acc: (acc,), acc_shape=(tmw, d))[0]

    mix_grads, (sib_out,) = _mix_bwd(dmix, o_a, o_b, r_a, r_b, g_out_a, g_out_b, hook=_exchange_hook(dw_out))
    do_a, do_bs, delta_a, delta_bs = mix_grads[0], mix_grads[1:4], mix_grads[4], mix_grads[5:8]
    dg_out_a, dg_out_b = mix_grads[8:]
    chip_sum_out = _chip_add("chip_add_w_out", dw_out, sib_out, core_index)

    (dq_a, dkv_a, dsinks), (chips_2,) = _attn_bwd(
        "attn_a_bwd", q_a, kv_a, do_a[None], lse_a, delta_a[None], dil=1, max_steps=WINDOW_A - 1,
        slopes=slopes_a, sinks=sinks, hook=_scatter_hook(chip_sum_2))
    dqs, dks, dvs = [], [], []
    scatter = {1: chip_sum_1, 4: chip_sum_out}
    arrived = {}
    for n, (window, dil) in enumerate(DILATED_BRANCHES):
        res = _attn_bwd(f"attn_b{dil}_bwd", q_bs[n], (k_bs[n], v_bs[n]), do_bs[n], lse_tot[n],
                        delta_bs[n], dil=dil, max_steps=window // dil, slopes=slopes_b,
                        hook=_scatter_hook(scatter[dil]) if dil in scatter else None)
        if dil in scatter:
            res, (arrived[dil],) = res
        dq, dk, dv = res
        dqs.append(dq)
        dks.append(dk)
        dvs.append(dv)
    half_2 = _final_add("final_add_w_2", chip_sum_2, chips_2, chip_index, core_index)
    half_1 = _final_add("final_add_w_1", chip_sum_1, arrived[1], chip_index, core_index)
    half_out = _final_add("final_add_w_out", chip_sum_out, arrived[4], chip_index, core_index)
    dproj, db_in = _assemble_dproj(dq_a[0], dkv_a[0], dqs, dks, dvs)

    tmw = d_in // 2 if (d_in // 2) % 128 == 0 else d_in
    tnw = _tile(d, 1024)
    tk_s = _tile(s, 1024)
    a_spec, b_spec = _mm_specs("tn", tmw, tnw, tk_s)
    dw_in_t = _matmul("in_proj_dw", dproj, h1, [], mode="tn", grid=(d_in // tmw, d // tnw, s // tk_s),
                      a_spec=a_spec, b_spec=b_spec, extra_specs=[],
                      out_shapes=[jax.ShapeDtypeStruct((d_in, d), wire)],
                      out_specs=[pl.BlockSpec((tmw, tnw), lambda i, j, k: (i, j))],
                      epilogue=lambda acc: (acc,), acc_shape=(tmw, tnw))[0]

    tn = _tile(d, 1024)
    a_spec, b_spec = _mm_specs("nn", tm, tn, d_in)
    tile_mn = pl.BlockSpec((tm, tn), lambda i, j, k: (i, j))
    (dh1,), (sib_in,) = _matmul("in_proj_dx", dproj, w_in_t, [], mode="nn", grid=(s // tm, d // tn, 1),
                                a_spec=a_spec, b_spec=b_spec, extra_specs=[],
                                out_shapes=[jax.ShapeDtypeStruct((s, d), F32)], out_specs=[tile_mn],
                                epilogue=lambda acc: (acc,), hook=_exchange_hook(dw_in_t))
    chip_sum_in = _chip_add("chip_add_w_in", dw_in_t, sib_in, core_index)

    (grad_x, _, dg_attn), (chips_in,) = _norm_bwd("norm_attn_bwd", dh1, x2d, r1, g_attn, dx2,
                                                  hook=_scatter_hook(chip_sum_in))
    half_in = _final_add("final_add_w_in", chip_sum_in, chips_in, chip_index, core_index)

    small_parts = [dg_attn, db_in, dsinks[:, :n_heads_a], dg_out_a, dg_out_b, dg_mlp, dg_final]
    small_shapes = [g_attn.shape, b_in.shape, sinks_a.shape, g_out_a.shape, g_out_b.shape, g_mlp.shape,
                    g_final.shape]
    n_small = sum(int(np.prod(shp)) for shp in small_shapes)
    rows_s = -(-n_small // (8 * 128)) * 8
    (gw_in_t, gw_out, gw_1, gw_2), small_sum = _share_halves(
        [half_in, half_out, half_1, half_2], _pack_small(small_parts, rows_s))
    gw_in = gw_in_t.T

    upd_in = _adamw("adamw_w_in", w_in[0], gw_in, m_w_in[0], v_w_in[0])
    upd_out = _adamw("adamw_w_out", w_out[0], gw_out, m_w_out[0], v_w_out[0])
    upd_1 = _adamw("adamw_w_1", w_1[0], gw_1, m_w_1[0], v_w_1[0])
    upd_2 = _adamw("adamw_w_2", w_2[0], gw_2, m_w_2[0], v_w_2[0])
    small_w = [g_attn, b_in, sinks_a, g_out_a, g_out_b, g_mlp, g_final]
    small_m = [m_g_attn, m_b_in, m_sinks_a, m_g_out_a, m_g_out_b, m_g_mlp, m_g_final]
    small_v = [v_g_attn, v_b_in, v_sinks_a, v_g_out_a, v_g_out_b, v_g_mlp, v_g_final]
    upd_small = _adamw("adamw_small", _pack_small(small_w, rows_s), small_sum,
                       _pack_small(small_m, rows_s), _pack_small(small_v, rows_s))
    d_small, m_small, v_small, g_small = [_unpack_small(t, small_shapes) for t in upd_small]

    loss = lax.psum(loss_part[0, 0], ("x", "y", "c"))

    def ordered(small, big):
        w_in_v, w_out_v, w_1_v, w_2_v = big
        return [small[0], w_in_v[None], small[1], small[2], small[3], small[4], w_out_v[None], small[5],
                w_1_v[None], w_2_v[None], small[6]]

    grads = ordered(g_small, (upd_in[3], upd_out[3], upd_1[3], upd_2[3]))
    deltas = ordered(d_small, (upd_in[0], upd_out[0], upd_1[0], upd_2[0]))
    new_m = ordered(m_small, (upd_in[1], upd_out[1], upd_1[1], upd_2[1]))
    new_v = ordered(v_small, (upd_in[2], upd_out[2], upd_1[2], upd_2[2]))
    return (loss, grad_x[None], *grads, *deltas, *new_m, *new_v)
```

```python
import jax
import jax.numpy as jnp
import numpy as np
from jax import lax
from jax.experimental import pallas as pl
from jax.experimental.pallas import tpu as pltpu

F32 = jnp.float32
BF16 = jnp.bfloat16

HEAD_DIM = 64
BLOCK = 128
PAIR = 2 * HEAD_DIM
N_KV_GROUPS = 2
WINDOW_A = 128
DILATED_BRANCHES = ((128, 1), (512, 4), (2048, 16))
EPS = 1e-5
NEG_INF = -1e30
ATT_SCALE = HEAD_DIM ** -0.5

ADAM_LR = 0.001
ADAM_B1 = 0.9
ADAM_B2 = 0.999
ADAM_EPS = 1e-08
ADAM_WD = 0.01
ADAM_STEP = 10

N_CHIPS = 4
N_DEV = 8
MESH = pl.DeviceIdType.MESH
GRAD_WIRE_DTYPE = jnp.bfloat16
BRANCH_DTYPE = jnp.bfloat16

VMEM_CAPACITY_V7X = 64 * 1024 * 1024
VMEM_LIMIT_MAX = VMEM_CAPACITY_V7X - 8 * 1024 * 1024
VMEM_LIMIT_MIN = VMEM_CAPACITY_V7X - 16 * 1024 * 1024

HBM_SPEC = pl.BlockSpec(memory_space=pltpu.HBM)
VMEM_SPEC = pl.BlockSpec(memory_space=pltpu.VMEM)
SMEM_SPEC = pl.BlockSpec(memory_space=pltpu.SMEM)


def _nbytes(shape, dtype):
    return int(np.prod([s for s in shape if s is not None])) * jnp.dtype(dtype).itemsize


def _params(semantics, block_bytes):
    limit = min(max(2 * block_bytes + (4 << 20), VMEM_LIMIT_MIN), VMEM_LIMIT_MAX)
    return pltpu.CompilerParams(dimension_semantics=semantics, vmem_limit_bytes=limit)


class _Hook:
    def __init__(self, operands, out_shape, sems, start, finish, mid=None, aliases=None,
                 mid_fraction=0.6):
        self.operands, self.out_shape, self.sems = list(operands), list(out_shape), list(sems)
        self.start, self.mid, self.finish = start, mid, finish
        self.aliases = dict(aliases or {})
        self.mid_fraction = mid_fraction


def _merge_hooks(hooks):
    hooks = [h for h in hooks if h is not None]
    if len(hooks) <= 1:
        return hooks[0] if hooks else None
    n_op = np.cumsum([0] + [len(h.operands) for h in hooks])
    n_out = np.cumsum([0] + [len(h.out_shape) for h in hooks])
    n_sem = np.cumsum([0] + [len(h.sems) for h in hooks])

    def run(which):
        def fn(ops, outs, sems):
            for i, h in enumerate(hooks):
                f = getattr(h, which)
                if f is not None:
                    f(ops[n_op[i]:n_op[i + 1]], outs[n_out[i]:n_out[i + 1]], sems[n_sem[i]:n_sem[i + 1]])
        return fn

    aliases = {}
    for i, h in enumerate(hooks):
        aliases.update({int(n_op[i]) + a: int(n_out[i]) + b for a, b in h.aliases.items()})
    return _Hook(sum([h.operands for h in hooks], []), sum([h.out_shape for h in hooks], []),
                 sum([h.sems for h in hooks], []), run("start"), run("finish"),
                 run("mid") if any(h.mid for h in hooks) else None, aliases)


def _call(body, hook, *, name, grid, in_specs, out_specs, out_shape, scratch_shapes=(), compiler_params):
    in_specs, out_specs, out_shape = list(in_specs), list(out_specs), list(out_shape)
    scratch_shapes = list(scratch_shapes)
    if hook is None:
        call = pl.pallas_call(body, name=name, grid=grid, in_specs=in_specs, out_specs=out_specs,
                              out_shape=out_shape, scratch_shapes=scratch_shapes,
                              compiler_params=compiler_params)
        return lambda *operands: (call(*operands), [])
    n_in, n_hin, n_out, n_hout, n_scr = (len(in_specs), len(hook.operands), len(out_specs),
                                         len(hook.out_shape), len(scratch_shapes))
    total = int(np.prod(grid))
    t_mid = min(int(total * hook.mid_fraction), total - 1)

    def wrapped(*refs):
        ins, h_in = refs[:n_in], refs[n_in:n_in + n_hin]
        o0 = n_in + n_hin
        outs, h_out = refs[o0:o0 + n_out], refs[o0 + n_out:o0 + n_out + n_hout]
        s0 = o0 + n_out + n_hout
        scr, h_sems = refs[s0:s0 + n_scr], refs[s0 + n_scr:]
        t = pl.program_id(0)
        for axis in range(1, len(grid)):
            t = t * grid[axis] + pl.program_id(axis)

        @pl.when(t == 0)
        def _():
            hook.start(h_in, h_out, h_sems)

        body(*ins, *outs, *scr)
        if hook.mid is not None:
            @pl.when(t == t_mid)
            def _():
                hook.mid(h_in, h_out, h_sems)

        @pl.when(t == total - 1)
        def _():
            hook.finish(h_in, h_out, h_sems)

    params = pltpu.CompilerParams(dimension_semantics=("arbitrary",) * len(grid),
                                  vmem_limit_bytes=compiler_params.vmem_limit_bytes)
    call = pl.pallas_call(
        wrapped, name=name, grid=grid,
        in_specs=in_specs + [HBM_SPEC] * n_hin, out_specs=out_specs + [HBM_SPEC] * n_hout,
        out_shape=out_shape + hook.out_shape, scratch_shapes=scratch_shapes + hook.sems,
        input_output_aliases={n_in + a: n_out + b for a, b in hook.aliases.items()},
        compiler_params=params)

    def run(*operands):
        res = call(*operands, *hook.operands)
        return res[:n_out], res[n_out:]

    return run


def _remote(src, dst, send_sem, recv_sem, device):
    return pltpu.make_async_remote_copy(src_ref=src, dst_ref=dst, send_sem=send_sem, recv_sem=recv_sem,
                                        device_id=device, device_id_type=MESH)


def alibi_slopes(n):
    return [float(v) for v in np.asarray(2.0 ** (-8.0 * (np.arange(n) + 1) / n), dtype=np.float32)]


def _matmul(name, a, b, extras, *, mode, grid, a_spec, b_spec, extra_specs, out_shapes, out_specs,
            epilogue, prologue=None, acc_shape=None, hook=None):
    dims = {"nn": ((1,), (0,)), "nt": ((1,), (1,)), "tn": ((0,), (0,))}[mode]
    nk = grid[2]
    n_ex, n_out = len(extras), len(out_shapes)

    def body(a_ref, b_ref, *rest):
        ex, outs = rest[:n_ex], rest[n_ex:n_ex + n_out]
        av = a_ref[...]
        if prologue is not None:
            av = prologue(av)
        part = lax.dot_general(av, b_ref[...], (dims, ((), ())), preferred_element_type=F32)

        def finish(acc):
            res = epilogue(acc, *[e[...] for e in ex])
            for o, r in zip(outs, res):
                o[...] = r.astype(o.dtype)

        if nk == 1:
            finish(part)
        else:
            acc_ref = rest[-1]
            k = pl.program_id(2)

            @pl.when(k == 0)
            def _():
                acc_ref[...] = part

            @pl.when(k > 0)
            def _():
                acc_ref[...] += part

            @pl.when(k == nk - 1)
            def _():
                finish(acc_ref[...])

    blocks = [(a_spec.block_shape, a.dtype), (b_spec.block_shape, b.dtype)]
    blocks += [(s.block_shape, e.dtype) for s, e in zip(extra_specs, extras)]
    blocks += [(s.block_shape, o.dtype) for s, o in zip(out_specs, out_shapes)]
    nbytes = sum(_nbytes(s, d) for s, d in blocks)
    scratch = []
    if nk > 1:
        scratch.append(pltpu.VMEM(acc_shape, F32))
        nbytes += _nbytes(acc_shape, F32)
    res, hook_res = _call(
        body, hook, name=name, grid=grid,
        in_specs=[a_spec, b_spec, *extra_specs], out_specs=list(out_specs), out_shape=list(out_shapes),
        scratch_shapes=scratch,
        compiler_params=_params(("parallel", "parallel", "arbitrary"), nbytes),
    )(a, b, *extras)
    return res if hook is None else (res, hook_res)


def _mm_specs(mode, tm, tn, tk, b_block=None, b_map=None):
    if mode == "tn":
        a_spec = pl.BlockSpec((tk, tm), lambda i, j, k: (k, i))
    else:
        a_spec = pl.BlockSpec((tm, tk), lambda i, j, k: (i, k))
    if b_block is not None:
        b_spec = pl.BlockSpec(b_block, b_map)
    elif mode == "nt":
        b_spec = pl.BlockSpec((tn, tk), lambda i, j, k: (j, k))
    else:
        b_spec = pl.BlockSpec((tk, tn), lambda i, j, k: (k, j))
    return a_spec, b_spec


def _project_by_class(name, h, w_t, bias, row_off, width, dilations, hook=None):
    s, d = h.shape
    tm = _tile(s, 2048)
    tn = 512 if width % 512 == 0 and row_off % 512 == 0 else _tile(width, 256)
    off = row_off // tn
    assert row_off % tn == 0 and tn % 128 == 0
    n_out = len(dilations)

    def body(h_ref, w_ref, b_ref, *rest):
        outs, perm_ref = rest[:n_out], rest[n_out]
        acc = lax.dot_general(h_ref[...], w_ref[...], (((1,), (1,)), ((), ())), preferred_element_type=F32)
        acc = acc + b_ref[...]
        for j in range(tn // 128):
            cols = slice(j * 128, (j + 1) * 128)
            _to_classes(outs, cols, acc[:, cols], perm_ref, dilations)

    blocks = tm * d * 2 + tn * d * 2 + 3 * tm * tn * 2 + tm * 128 * 4
    res, hook_res = _call(
        body, hook, name=name, grid=(s // tm, width // tn),
        in_specs=[pl.BlockSpec((tm, d), lambda i, j: (i, 0)), pl.BlockSpec((tn, d), lambda i, j: (j + off, 0)),
                  pl.BlockSpec((1, tn), lambda i, j: (0, j + off))],
        out_specs=[pl.BlockSpec((dil, tm // dil, tn), lambda i, j: (0, i, j)) for dil in dilations],
        out_shape=[_class_shape(dil, s, width, BF16) for dil in dilations],
        scratch_shapes=[pltpu.VMEM((tm, 128), F32)],
        compiler_params=_params(("parallel", "parallel"), blocks),
    )(h, w_t, bias)
    return res if hook is None else (res, hook_res)


def _tile(n, want):
    if n <= want:
        return n
    t = (want // 128) * 128
    while t > 128 and n % t:
        t -= 128
    assert n % t == 0, (n, want)
    return t


def _row_tile(s):
    return 256 if s % 256 == 0 else s


def _norm_fwd(name, x, g, hook=None):
    s, d = x.shape
    tm = _row_tile(s)

    def body(x_ref, g_ref, h_ref, r_ref):
        xv = x_ref[...]
        r = lax.rsqrt(jnp.mean(xv * xv, axis=-1, keepdims=True) + EPS)
        h_ref[...] = ((xv * r) * g_ref[...]).astype(BF16)
        r_ref[...] = r

    row = pl.BlockSpec((tm, d), lambda i: (i, 0))
    res, hook_res = _call(
        body, hook, name=name, grid=(s // tm,),
        in_specs=[row, pl.BlockSpec((1, d), lambda i: (0, 0))],
        out_specs=[row, pl.BlockSpec((tm, 1), lambda i: (i, 0))],
        out_shape=[jax.ShapeDtypeStruct((s, d), BF16), jax.ShapeDtypeStruct((s, 1), F32)],
        compiler_params=_params(("parallel",), tm * d * 6),
    )(x, g)
    return res if hook is None else (res, hook_res)


def _norm_bwd(name, dh, x, r, g, dres, hook=None):
    s, d = x.shape
    tm = _row_tile(s)

    def body(dh_ref, x_ref, r_ref, g_ref, dres_ref, dx_ref, dxb_ref, dg_ref):
        rv = r_ref[...]
        xn = x_ref[...] * rv
        dhv = dh_ref[...]
        dxn = dhv * g_ref[...]
        dx = dres_ref[...] + rv * (dxn - xn * jnp.mean(dxn * xn, axis=-1, keepdims=True))
        dx_ref[...] = dx
        dxb_ref[...] = dx.astype(BF16)
        part = jnp.sum(dhv * xn, axis=0, keepdims=True)

        @pl.when(pl.program_id(0) == 0)
        def _():
            dg_ref[...] = part

        @pl.when(pl.program_id(0) > 0)
        def _():
            dg_ref[...] += part

    row = pl.BlockSpec((tm, d), lambda i: (i, 0))
    vec = pl.BlockSpec((1, d), lambda i: (0, 0))
    res, hook_res = _call(
        body, hook, name=name, grid=(s // tm,),
        in_specs=[row, row, pl.BlockSpec((tm, 1), lambda i: (i, 0)), vec, row],
        out_specs=[row, row, vec],
        out_shape=[jax.ShapeDtypeStruct((s, d), F32), jax.ShapeDtypeStruct((s, d), BF16),
                   jax.ShapeDtypeStruct((1, d), F32)],
        compiler_params=_params(("arbitrary",), tm * d * 18),
    )(dh, x, r, g, dres)
    return res if hook is None else (res, hook_res)


def _loss_head(x3, target, g):
    s, d = x3.shape
    tm = _row_tile(s)

    def body(x_ref, t_ref, g_ref, dx_ref, dxb_ref, loss_ref, dg_ref):
        xv = x_ref[...]
        gv = g_ref[...]
        r = lax.rsqrt(jnp.mean(xv * xv, axis=-1, keepdims=True) + EPS)
        xn = xv * r
        err = xn * gv - t_ref[...]
        loss = 0.5 * jnp.sum(jnp.mean(err * err, axis=-1, keepdims=True), axis=0, keepdims=True)
        dy = err / d
        dxn = dy * gv
        dx = r * (dxn - xn * jnp.mean(dxn * xn, axis=-1, keepdims=True))
        dx_ref[...] = dx
        dxb_ref[...] = dx.astype(BF16)
        dg = jnp.sum(dy * xn, axis=0, keepdims=True)
        loss_row = jnp.broadcast_to(loss, (1, 128))

        @pl.when(pl.program_id(0) == 0)
        def _():
            dg_ref[...] = dg
            loss_ref[...] = loss_row

        @pl.when(pl.program_id(0) > 0)
        def _():
            dg_ref[...] += dg
            loss_ref[...] += loss_row

    row = pl.BlockSpec((tm, d), lambda i: (i, 0))
    vec = pl.BlockSpec((1, d), lambda i: (0, 0))
    return _call(
        body, None, name="loss_head", grid=(s // tm,),
        in_specs=[row, row, vec],
        out_specs=[row, row, pl.BlockSpec((1, 128), lambda i: (0, 0)), vec],
        out_shape=[jax.ShapeDtypeStruct((s, d), F32), jax.ShapeDtypeStruct((s, d), BF16),
                   jax.ShapeDtypeStruct((1, 128), F32), jax.ShapeDtypeStruct((1, d), F32)],
        compiler_params=_params(("arbitrary",), tm * d * 14),
    )(x3, target, g)[0]


def _low_lanes(rows):
    return lax.broadcasted_iota(jnp.int32, (rows, PAIR), 1) < HEAD_DIM


def _to_classes(dst_refs, cols, value, perm_ref, dils):
    rows = value.shape[0]
    if any(dil > 1 for dil in dils):
        perm_ref[...] = value
    for dst_ref, dil in zip(dst_refs, dils):
        if dil == 1:
            dst_ref[0, :, cols] = value.astype(dst_ref.dtype)
            continue
        for r in range(dil):
            dst_ref[r, :, cols] = perm_ref[pl.ds(r, rows // dil, stride=dil), :].astype(dst_ref.dtype)


def _from_classes(src_ref, cols, perm_ref, dil):
    if dil == 1:
        return src_ref[0, :, cols].astype(F32)
    rows = perm_ref.shape[0]
    for r in range(dil):
        perm_ref[pl.ds(r, rows // dil, stride=dil), :] = src_ref[r, :, cols].astype(F32)
    return perm_ref[...]


def _class_spec(dil, tm, width):
    return pl.BlockSpec((dil, tm // dil, width), lambda i: (0, i, 0))


def _class_shape(dil, s, width, dtype):
    return jax.ShapeDtypeStruct((dil, s // dil, width), dtype)


DILATIONS = tuple(d for _, d in DILATED_BRANCHES)


def _mix_fwd(oa, obs, lses, ga, gb):
    s, qa = oa.shape
    qb = obs[0].shape[2]
    tm = _row_tile(s)
    all_lanes = slice(0, 128)

    def body(oa_ref, o1_ref, o2_ref, o3_ref, l1_ref, l2_ref, l3_ref, ga_ref, gb_ref,
             mix_ref, ob_ref, t1_ref, t2_ref, t3_ref, ra_ref, rb_ref, perm_ref):
        oav = oa_ref[...]
        ra = lax.rsqrt(jnp.mean(oav * oav, axis=-1, keepdims=True) + EPS)
        ra_ref[...] = ra
        mix_ref[:, 0:qa] = ((oav * ra) * ga_ref[...]).astype(BF16)
        l1, l2, l3 = [_from_classes(l_ref, all_lanes, perm_ref, dil)
                      for l_ref, dil in zip((l1_ref, l2_ref, l3_ref), DILATIONS)]
        mx = jnp.maximum(jnp.maximum(l1, l2), l3)
        e1, e2, e3 = jnp.exp(l1 - mx), jnp.exp(l2 - mx), jnp.exp(l3 - mx)
        tot = e1 + e2 + e3
        lse = mx + jnp.log(tot)
        _to_classes((t1_ref, t2_ref, t3_ref), all_lanes, lse, perm_ref, DILATIONS)
        ws = (e1 / tot, e2 / tot, e3 / tot)
        low = _low_lanes(tm)
        ssq = jnp.zeros((tm, 1), F32)
        for i in range(qb // PAIR):
            sl = slice(i * PAIR, (i + 1) * PAIR)
            acc = jnp.zeros((tm, PAIR), F32)
            for w, o_ref, dil in zip(ws, (o1_ref, o2_ref, o3_ref), DILATIONS):
                wexp = jnp.where(low, w[:, 2 * i:2 * i + 1], w[:, 2 * i + 1:2 * i + 2])
                acc = acc + wexp * _from_classes(o_ref, sl, perm_ref, dil)
            ob_ref[:, sl] = acc
            ssq = ssq + jnp.sum(acc * acc, axis=-1, keepdims=True)
        rb = lax.rsqrt(ssq / qb + EPS)
        rb_ref[...] = rb
        mix_ref[:, qa:qa + qb] = ((ob_ref[...] * rb) * gb_ref[...]).astype(BF16)

    def row(w):
        return pl.BlockSpec((tm, w), lambda i: (i, 0))

    def vec(w):
        return pl.BlockSpec((1, w), lambda i: (0, 0))

    return _call(
        body, None, name="mix_fwd", grid=(s // tm,),
        in_specs=([row(qa)] + [_class_spec(d, tm, qb) for d in DILATIONS]
                  + [_class_spec(d, tm, 128) for d in DILATIONS] + [vec(qa), vec(qb)]),
        out_specs=([row(qa + qb), row(qb)] + [_class_spec(d, tm, 128) for d in DILATIONS] + [row(1), row(1)]),
        out_shape=([jax.ShapeDtypeStruct((s, qa + qb), BF16), jax.ShapeDtypeStruct((s, qb), F32)]
                   + [_class_shape(d, s, 128, F32) for d in DILATIONS]
                   + [jax.ShapeDtypeStruct((s, 1), F32), jax.ShapeDtypeStruct((s, 1), F32)]),
        scratch_shapes=[pltpu.VMEM((tm, 128), F32)],
        compiler_params=_params(("parallel",), tm * (qa + 4 * qb) * 4 + tm * (qa + qb) * 2 + tm * 4096),
    )(oa, *obs, *lses, ga, gb)[0]


def _head_rowsums(prod, rows):
    low = _low_lanes(rows)
    lane = lax.broadcasted_iota(jnp.int32, (rows, 128), 1)
    out = jnp.zeros((rows, 128), F32)
    for i in range(prod.shape[1] // PAIR):
        tile = prod[:, i * PAIR:(i + 1) * PAIR]
        lo = jnp.sum(jnp.where(low, tile, 0.0), axis=-1, keepdims=True)
        hi = jnp.sum(jnp.where(low, 0.0, tile), axis=-1, keepdims=True)
        out = jnp.where(lane == 2 * i, lo, out)
        out = jnp.where(lane == 2 * i + 1, hi, out)
    return out


def _mix_bwd(dmix, oa, ob, ra, rb, ga, gb, hook=None):
    s, qa = oa.shape
    qb = ob.shape[1]
    tm = _row_tile(s)

    def one(dy, o, r, g):
        xn = o * r
        dxn = dy * g
        do = r * (dxn - xn * jnp.mean(dxn * xn, axis=-1, keepdims=True))
        return do, jnp.sum(dy * xn, axis=0, keepdims=True), _head_rowsums(do * o, tm)

    def body(dmix_ref, oa_ref, ob_ref, ra_ref, rb_ref, ga_ref, gb_ref,
             doa_ref, dob1_ref, dob2_ref, dob3_ref, dla_ref, dlb1_ref, dlb2_ref, dlb3_ref,
             dga_ref, dgb_ref, perm_ref):
        doa, dga, dla = one(dmix_ref[:, 0:qa], oa_ref[...], ra_ref[...], ga_ref[...])
        dob, dgb, dlb = one(dmix_ref[:, qa:qa + qb], ob_ref[...], rb_ref[...], gb_ref[...])
        doa_ref[...] = doa.astype(BF16)
        dla_ref[...] = dla
        _to_classes((dlb1_ref, dlb2_ref, dlb3_ref), slice(0, 128), dlb, perm_ref, DILATIONS)
        for i in range(qb // PAIR):
            sl = slice(i * PAIR, (i + 1) * PAIR)
            _to_classes((dob1_ref, dob2_ref, dob3_ref), sl, dob[:, sl], perm_ref, DILATIONS)

        @pl.when(pl.program_id(0) == 0)
        def _():
            dga_ref[...] = dga
            dgb_ref[...] = dgb

        @pl.when(pl.program_id(0) > 0)
        def _():
            dga_ref[...] += dga
            dgb_ref[...] += dgb

    def row(w):
        return pl.BlockSpec((tm, w), lambda i: (i, 0))

    def vec(w):
        return pl.BlockSpec((1, w), lambda i: (0, 0))

    res, hook_res = _call(
        body, hook, name="mix_bwd", grid=(s // tm,),
        in_specs=[row(qa + qb), row(qa), row(qb), row(1), row(1), vec(qa), vec(qb)],
        out_specs=([row(qa)] + [_class_spec(d, tm, qb) for d in DILATIONS] + [row(128)]
                   + [_class_spec(d, tm, 128) for d in DILATIONS] + [vec(qa), vec(qb)]),
        out_shape=([jax.ShapeDtypeStruct((s, qa), BF16)] + [_class_shape(d, s, qb, BF16) for d in DILATIONS]
                   + [jax.ShapeDtypeStruct((s, 128), F32)] + [_class_shape(d, s, 128, F32) for d in DILATIONS]
                   + [jax.ShapeDtypeStruct((1, qa), F32), jax.ShapeDtypeStruct((1, qb), F32)]),
        scratch_shapes=[pltpu.VMEM((tm, 128), F32)],
        compiler_params=_params(("arbitrary",), tm * (qa + qb) * 16),
    )(dmix, oa, ob, ra, rb, ga, gb)
    return res if hook is None else (res, hook_res)


def _assemble_dproj(dqa, dkva, dqs, dks, dvs):
    s, qa = dqa.shape
    kva = dkva.shape[1]
    qb = dqs[0].shape[2]
    width = qa + kva + 3 * qb
    tm = _row_tile(s)

    def body(dqa_ref, dkva_ref, q1, q2, q3, k1, k2, k3, v1, v2, v3, dp_ref, db_ref, perm_ref):
        first = pl.program_id(0) == 0

        def emit(off, val):
            dp_ref[:, off:off + PAIR] = val.astype(BF16)
            col = jnp.sum(val, axis=0, keepdims=True)

            @pl.when(first)
            def _():
                db_ref[:, off:off + PAIR] = col

            @pl.when(jnp.logical_not(first))
            def _():
                db_ref[:, off:off + PAIR] += col

        for i in range(qa // PAIR):
            emit(i * PAIR, dqa_ref[:, i * PAIR:(i + 1) * PAIR])
        for i in range(kva // PAIR):
            emit(qa + i * PAIR, dkva_ref[:, i * PAIR:(i + 1) * PAIR])
        for j, branch_refs in enumerate(((q1, q2, q3), (k1, k2, k3), (v1, v2, v3))):
            for i in range(qb // PAIR):
                sl = slice(i * PAIR, (i + 1) * PAIR)
                total = None
                for ref, dil in zip(branch_refs, DILATIONS):
                    val = _from_classes(ref, sl, perm_ref, dil)
                    total = val if total is None else total + val
                emit(qa + kva + j * qb + i * PAIR, total)

    def row(w):
        return pl.BlockSpec((tm, w), lambda i: (i, 0))

    return _call(
        body, None, name="assemble_dproj", grid=(s // tm,),
        in_specs=[row(qa), row(kva)] + [_class_spec(d, tm, qb) for d in DILATIONS] * 3,
        out_specs=[row(width), pl.BlockSpec((1, width), lambda i: (0, 0))],
        out_shape=[jax.ShapeDtypeStruct((s, width), BF16), jax.ShapeDtypeStruct((1, width), F32)],
        scratch_shapes=[pltpu.VMEM((tm, 128), F32)],
        compiler_params=_params(("arbitrary",), tm * (qa + kva + 9 * qb) * 4 + tm * width * 2),
    )(dqa, dkva, *dqs, *dks, *dvs)[0]


def _fill_bias(bias_ref, n_pairs, max_steps, dil, slopes, sink_ref=None):
    qi = lax.broadcasted_iota(jnp.int32, (BLOCK, 2 * BLOCK), 0)
    kj = lax.broadcasted_iota(jnp.int32, (BLOCK, 2 * BLOCK), 1)
    steps = qi + BLOCK - kj
    dist = (steps * dil).astype(F32)
    band = (steps >= 0) & (steps <= max_steps)
    assert sink_ref is None or max_steps < BLOCK
    for first in (0, 1):
        valid = band & (kj >= BLOCK) if first else band
        for i in range(n_pairs):
            tables = []
            for half in (0, 1):
                table = jnp.where(valid, -(slopes[2 * i + half] * dist), NEG_INF)
                if sink_ref is not None:
                    table = jnp.where(kj == 0, sink_ref[2 * i + half], table)
                tables.append(table)
            bias_ref[first, i] = jnp.concatenate(tables, axis=0)


def _without_sink_row(tile):
    row = lax.broadcasted_iota(jnp.int32, tile.shape, 0)
    return jnp.where(row == 0, jnp.zeros_like(tile), tile)


def _bias_shape(n_pairs):
    return pltpu.VMEM((2, n_pairs, 2 * BLOCK, 2 * BLOCK), F32)


def _stack_heads(tile, low):
    zero = jnp.zeros_like(tile)
    return jnp.concatenate([jnp.where(low, tile, zero), jnp.where(low, zero, tile)], axis=0)


def _unstack_heads(stacked, low):
    return jnp.where(low, stacked[0:BLOCK], stacked[BLOCK:2 * BLOCK])


def _head_columns(ref, i):
    return jnp.concatenate([ref[:, 2 * i:2 * i + 1], ref[:, 2 * i + 1:2 * i + 2]], axis=0)


def _swap_halves(t):
    return pltpu.roll(t, HEAD_DIM, 1)


def _dup_group(t_bf16, group):
    t = t_bf16.astype(F32)
    low = lax.broadcasted_iota(jnp.int32, t.shape, 1) < HEAD_DIM
    keep = low if group == 0 else jnp.logical_not(low)
    return jnp.where(keep, t, _swap_halves(t)).astype(BF16)


def _attn_fwd(name, q, kv, *, dil, max_steps, slopes, sinks=None, hook=None):
    grouped = sinks is not None
    _, length, w = q.shape
    n_pairs = w // PAIR
    nb = length // BLOCK
    heads_per_group = 2 * n_pairs // N_KV_GROUPS

    def body(*refs):
        if grouped:
            sink_ref, q_ref, kvp_ref, kvc_ref, o_ref, lse_ref, bias_ref = refs
        else:
            q_ref, kp_ref, kc_ref, vp_ref, vc_ref, o_ref, lse_ref, bias_ref = refs
        n = pl.program_id(1)

        @pl.when((pl.program_id(0) == 0) & (n == 0))
        def _():
            _fill_bias(bias_ref, n_pairs, max_steps, dil, slopes, sink_ref if grouped else None)

        first = (n == 0).astype(jnp.int32)
        low = _low_lanes(BLOCK)
        lane = lax.broadcasted_iota(jnp.int32, (BLOCK, 128), 1)
        lse_acc = jnp.zeros((BLOCK, 128), F32)
        if grouped:
            kv_all = jnp.concatenate([kvp_ref[...], kvc_ref[...]], axis=0)
            k_dup = [_without_sink_row(_dup_group(kv_all[:, 0:PAIR], g)) for g in range(N_KV_GROUPS)]
            v_dup = [_without_sink_row(_dup_group(kv_all[:, PAIR:2 * PAIR], g)) for g in range(N_KV_GROUPS)]
        for i in range(n_pairs):
            sl = slice(i * PAIR, (i + 1) * PAIR)
            qs = _stack_heads(q_ref[:, sl] * ATT_SCALE, low)
            if grouped:
                kk, vv = k_dup[2 * i // heads_per_group], v_dup[2 * i // heads_per_group]
            else:
                kk = jnp.concatenate([kp_ref[:, sl], kc_ref[:, sl]], axis=0)
                vv = jnp.concatenate([vp_ref[:, sl], vc_ref[:, sl]], axis=0)
            sc = lax.dot_general(qs, kk, (((1,), (1,)), ((), ())), preferred_element_type=F32)
            sc = sc + bias_ref[first, i]
            m = jnp.max(sc, axis=-1, keepdims=True)
            p = jnp.exp(sc - m)
            den = jnp.sum(p, axis=-1, keepdims=True)
            o = jnp.dot(p.astype(BF16), vv, preferred_element_type=F32) / den
            o_ref[:, sl] = _unstack_heads(o, low).astype(o_ref.dtype)
            lse = m + jnp.log(den)
            lse_acc = jnp.where(lane == 2 * i, lse[0:BLOCK], lse_acc)
            lse_acc = jnp.where(lane == 2 * i + 1, lse[BLOCK:2 * BLOCK], lse_acc)
        lse_ref[...] = lse_acc

    def cur(width):
        return pl.BlockSpec((None, BLOCK, width), lambda r, n: (r, n, 0))

    def prev(width):
        return pl.BlockSpec((None, BLOCK, width), lambda r, n: (r, jnp.maximum(n - 1, 0), 0))

    if grouped:
        kvw = kv.shape[2]
        operands = [sinks, q, kv, kv]
        in_specs = [SMEM_SPEC, cur(w), prev(kvw), cur(kvw)]
    else:
        operands = [q, kv[0], kv[0], kv[1], kv[1]]
        in_specs = [cur(w), prev(w), cur(w), prev(w), cur(w)]
    res, hook_res = _call(
        body, hook, name=name, grid=(dil, nb), in_specs=in_specs,
        out_specs=[cur(w), cur(128)],
        out_shape=[jax.ShapeDtypeStruct((dil, length, w), F32 if grouped else BRANCH_DTYPE),
                   jax.ShapeDtypeStruct((dil, length, 128), F32)],
        scratch_shapes=[_bias_shape(n_pairs)],
        compiler_params=_params(("arbitrary", "arbitrary"), BLOCK * w * 16 + n_pairs * BLOCK * BLOCK * 16),
    )(*operands)
    return res if hook is None else (res, hook_res)


def _attn_bwd(name, q, kv, do, lse, delta, *, dil, max_steps, slopes, sinks=None, hook=None):
    grouped = sinks is not None
    _, length, w = q.shape
    n_pairs = w // PAIR
    nb = length // BLOCK
    heads_per_group = 2 * n_pairs // N_KV_GROUPS
    pairs_per_group = n_pairs // N_KV_GROUPS

    def body(*refs):
        if grouped:
            (sink_ref, q_ref, kvp_ref, kvc_ref, do_ref, lse_ref, dl_ref,
             dq_ref, dkv_ref, dsink_ref, acc_ref, bias_ref) = refs
        else:
            (q_ref, kp_ref, kc_ref, vp_ref, vc_ref, do_ref, lse_ref, dl_ref,
             dq_ref, dk_ref, dv_ref, acck_ref, accv_ref, bias_ref) = refs
        n = pl.program_id(1)

        @pl.when((pl.program_id(0) == 0) & (n == 0))
        def _():
            _fill_bias(bias_ref, n_pairs, max_steps, dil, slopes, sink_ref if grouped else None)

        @pl.when(n == 0)
        def _():
            if grouped:
                acc_ref[...] = jnp.zeros_like(acc_ref)

                @pl.when(pl.program_id(0) == 0)
                def _():
                    dsink_ref[...] = jnp.zeros_like(dsink_ref)
            else:
                acck_ref[...] = jnp.zeros_like(acck_ref)
                accv_ref[...] = jnp.zeros_like(accv_ref)

        @pl.when(n == nb)
        def _():
            if grouped:
                dkv_ref[...] = acc_ref[...]
            else:
                dk_ref[...] = acck_ref[...].astype(dk_ref.dtype)
                dv_ref[...] = accv_ref[...].astype(dv_ref.dtype)

        @pl.when(n < nb)
        def _():
            first = (n == 0).astype(jnp.int32)
            low = _low_lanes(BLOCK)
            low_kv = _low_lanes(2 * BLOCK)
            lane1 = lax.broadcasted_iota(jnp.int32, (1, 128), 1)
            if grouped:
                kv_all = jnp.concatenate([kvp_ref[...], kvc_ref[...]], axis=0)
                k_dup = [_without_sink_row(_dup_group(kv_all[:, 0:PAIR], g)) for g in range(N_KV_GROUPS)]
                v_dup = [_without_sink_row(_dup_group(kv_all[:, PAIR:2 * PAIR], g)) for g in range(N_KV_GROUPS)]
                dk_grp =[jnp.zeros((2 * BLOCK, PAIR), F32) for _ in range(N_KV_GROUPS)]
                dv_grp = [jnp.zeros((2 * BLOCK, PAIR), F32) for _ in range(N_KV_GROUPS)]
                dsink = jnp.zeros((1, 128), F32)
            for i in range(n_pairs):
                sl = slice(i * PAIR, (i + 1) * PAIR)
                qs = _stack_heads(q_ref[:, sl] * ATT_SCALE, low)
                dos = _stack_heads(do_ref[:, sl], low)
                if grouped:
                    grp = 2 * i // heads_per_group
                    kk, vv = k_dup[grp], v_dup[grp]
                else:
                    kk = jnp.concatenate([kp_ref[:, sl], kc_ref[:, sl]], axis=0)
                    vv = jnp.concatenate([vp_ref[:, sl], vc_ref[:, sl]], axis=0)
                lse_col = _head_columns(lse_ref, i)
                dl_col = _head_columns(dl_ref, i)
                sc = lax.dot_general(qs, kk, (((1,), (1,)), ((), ())), preferred_element_type=F32)
                p = jnp.exp(sc + bias_ref[first, i] - lse_col)
                dp = lax.dot_general(dos, vv, (((1,), (1,)), ((), ())), preferred_element_type=F32)
                ds_f32 = p * (dp - dl_col)
                ds = ds_f32.astype(BF16)
                dq = jnp.dot(ds, kk, preferred_element_type=F32)
                dkk = lax.dot_general(ds, qs, (((0,), (0,)), ((), ())), preferred_element_type=F32)
                dvv = lax.dot_general(p.astype(BF16), dos, (((0,), (0,)), ((), ())),
                                      preferred_element_type=F32)
                if grouped:
                    for half in (0, 1):
                        contrib = jnp.sum(ds_f32[half * BLOCK:(half + 1) * BLOCK, 0:1], axis=0, keepdims=True)
                        dsink = jnp.where(lane1 == 2 * i + half, dsink + contrib, dsink)
                dq_ref[:, sl] = (_unstack_heads(dq, low) * ATT_SCALE).astype(dq_ref.dtype)
                if grouped:
                    dk_grp[grp] = dk_grp[grp] + dkk
                    dv_grp[grp] = dv_grp[grp] + dvv
                else:
                    dk_ref[:, sl] = (acck_ref[:, sl] + dkk[0:BLOCK]).astype(dk_ref.dtype)
                    acck_ref[:, sl] = dkk[BLOCK:2 * BLOCK]
                    dv_ref[:, sl] = (accv_ref[:, sl] + dvv[0:BLOCK]).astype(dv_ref.dtype)
                    accv_ref[:, sl] = dvv[BLOCK:2 * BLOCK]
            if grouped:
                folded = [_without_sink_row(t + _swap_halves(t)) for t in dk_grp + dv_grp]
                dk_tile = jnp.where(low_kv, folded[0], folded[1])
                dv_tile = jnp.where(low_kv, folded[2], folded[3])
                part = jnp.concatenate([dk_tile, dv_tile], axis=1)
                dkv_ref[...] = acc_ref[...] + part[0:BLOCK]
                acc_ref[...] = part[BLOCK:2 * BLOCK]
                dsink_ref[...] += dsink

    last = nb - 1

    def cur(width):
        return pl.BlockSpec((None, BLOCK, width), lambda r, n: (r, jnp.minimum(n, last), 0))

    def prev(width):
        return pl.BlockSpec((None, BLOCK, width),
                            lambda r, n: (r, jnp.maximum(jnp.minimum(n, last) - 1, 0), 0))

    def done(width):
        return pl.BlockSpec((None, BLOCK, width), lambda r, n: (r, jnp.maximum(n - 1, 0), 0))

    if grouped:
        assert pairs_per_group * N_KV_GROUPS == n_pairs and heads_per_group % 2 == 0
        kvw = kv.shape[2]
        operands = [sinks, q, kv, kv, do, lse, delta]
        in_specs = [SMEM_SPEC, cur(w), prev(kvw), cur(kvw), cur(w), cur(128), cur(128)]
        out_specs = [cur(w), done(kvw), pl.BlockSpec((1, 128), lambda r, n: (0, 0))]
        out_shape = [jax.ShapeDtypeStruct((dil, length, w), F32), jax.ShapeDtypeStruct((dil, length, kvw), F32),
                     jax.ShapeDtypeStruct((1, 128), F32)]
        scratch = [pltpu.VMEM((BLOCK, kvw), F32), _bias_shape(n_pairs)]
    else:
        operands = [q, kv[0], kv[0], kv[1], kv[1], do, lse, delta]
        in_specs = [cur(w), prev(w), cur(w), prev(w), cur(w), cur(w), cur(128), cur(128)]
        out_specs = [cur(w), done(w), done(w)]
        out_shape = [jax.ShapeDtypeStruct((dil, length, w), BRANCH_DTYPE)] * 3
        scratch = [pltpu.VMEM((BLOCK, w), F32), pltpu.VMEM((BLOCK, w), F32), _bias_shape(n_pairs)]
    res, hook_res = _call(
        body, hook, name=name, grid=(dil, nb + 1), in_specs=in_specs, out_specs=out_specs,
        out_shape=out_shape, scratch_shapes=scratch,
        compiler_params=_params(("arbitrary", "arbitrary"), BLOCK * w * 32 + n_pairs * BLOCK * BLOCK * 16),
    )(*operands)
    return res if hook is None else (res, hook_res)


def _adamw(name, w, g, m, v):
    rows, cols = w.shape
    tm = next((t for t in (256, 128, 64, 32, 16, 8) if rows % t == 0), rows)

    def body(w_ref, g_ref, m_ref, v_ref, d_ref, nm_ref, nv_ref, g_out_ref):
        gv = g_ref[...]
        mn = ADAM_B1 * m_ref[...] + (1.0 - ADAM_B1) * gv
        vn = ADAM_B2 * v_ref[...] + (1.0 - ADAM_B2) * (gv * gv)
        m_hat = mn / (1.0 - ADAM_B1 ** ADAM_STEP)
        v_hat = vn / (1.0 - ADAM_B2 ** ADAM_STEP)
        d_ref[...] = -ADAM_LR * (m_hat / (jnp.sqrt(v_hat) + ADAM_EPS) + ADAM_WD * w_ref[...])
        nm_ref[...] = mn
        nv_ref[...] = vn
        g_out_ref[...] = gv

    spec = pl.BlockSpec((tm, cols), lambda i: (i, 0))
    return _call(
        body, None, name=name, grid=(rows // tm,), in_specs=[spec] * 4, out_specs=[spec] * 4,
        out_shape=[jax.ShapeDtypeStruct(w.shape, F32)] * 4,
        compiler_params=_params(("parallel",), tm * cols * 32),
    )(w, g, m, v)[0]


def _mesh_position():
    return lax.axis_index("x"), lax.axis_index("y"), lax.axis_index("c")


def _other_chips(x, y):
    return [(1 - x, y), (x, 1 - y), (1 - x, 1 - y)]


def _gather_hook(gathered, lo, hi, mid_fraction=0.6):
    rows, cols = gathered.shape[0] // N_CHIPS, gathered.shape[1]
    half, n = rows // 2, hi - lo
    assert lo % 16 == 0 and n % 16 == 0 and half % 16 == 0

    def region(out, owner_chip, which_half):
        return out.at[pl.ds(pl.multiple_of(owner_chip * rows + which_half * half + lo, 16), n)]

    def parts(outs, sems):
        x, y, c = _mesh_position()
        return outs[0], sems, c, 2 * x + y, (x, y, 1 - c), _other_chips(x, y)

    def start(ops, outs, sems):
        out, (send, recv, fsend, frecv), c, chip, sibling, others = parts(outs, sems)
        mine = region(out, chip, c)
        for k, (px, py) in enumerate(others):
            _remote(mine, mine, send.at[k], recv.at[k], (px, py, c)).start()

    def mid(ops, outs, sems):
        out, (send, recv, fsend, frecv), c, chip, sibling, others = parts(outs, sems)
        for k, (px, py) in enumerate(others):
            landed = region(out, 2 * px + py, c)
            _remote(landed, landed, send.at[k], recv.at[k], (px, py, c)).wait_recv()
            _remote(landed, landed, fsend.at[k], frecv.at[k], sibling).start()

    def finish(ops, outs, sems):
        out, (send, recv, fsend, frecv), c, chip, sibling, others = parts(outs, sems)
        mine = region(out, chip, c)
        for k, (px, py) in enumerate(others):
            passed = region(out, 2 * px + py, 1 - c)
            _remote(passed, passed, fsend.at[k], frecv.at[k], sibling).wait_recv()
        for k, (px, py) in enumerate(others):
            landed = region(out, 2 * px + py, c)
            _remote(landed, landed, fsend.at[k], frecv.at[k], sibling).wait_send()
            _remote(mine, mine, send.at[k], recv.at[k], (px, py, c)).wait_send()

    return _Hook([gathered], [jax.ShapeDtypeStruct(gathered.shape, gathered.dtype)],
                 [pltpu.SemaphoreType.DMA((3,))] * 4, start, finish, mid, aliases={0: 0},
                 mid_fraction=mid_fraction)


def _own_shard_in_place(name, shard, chip):
    rows, cols = shard.shape
    tr = next(t for t in (544, 512, 320, 256, 128, 64, 32, 16) if rows % t == 0)

    def body(chip_ref, w_ref, o_ref):
        o_ref[...] = w_ref[...].astype(BF16)

    return pl.pallas_call(
        body, name=name,
        grid_spec=pltpu.PrefetchScalarGridSpec(
            num_scalar_prefetch=1, grid=(rows // tr,),
            in_specs=[pl.BlockSpec((tr, cols), lambda i, chip_ref: (i, 0))],
            out_specs=pl.BlockSpec((tr, cols), lambda i, chip_ref: (chip_ref[0] * (rows // tr) + i, 0))),
        out_shape=jax.ShapeDtypeStruct((N_CHIPS * rows, cols), BF16),
        compiler_params=_params(("parallel",), tr * cols * 6),
    )(chip, shard)


def _exchange_hook(grad):
    rows, cols = grad.shape[0] // N_CHIPS, grad.shape[1]
    half = rows // 2
    assert half % 16 == 0

    def copies(ops, outs, sems):
        x, y, c = _mesh_position()
        send, recv = sems
        return [_remote(ops[0].at[pl.ds(pl.multiple_of(k * rows + (1 - c) * half, 16), half)], outs[0].at[k],
                        send.at[k], recv.at[k], (x, y, 1 - c)) for k in range(N_CHIPS)]

    def start(ops, outs, sems):
        for cp in copies(ops, outs, sems):
            cp.start()

    def finish(ops, outs, sems):
        for cp in copies(ops, outs, sems):
            cp.wait_recv()
            cp.wait_send()

    return _Hook([grad], [jax.ShapeDtypeStruct((N_CHIPS, half, cols), grad.dtype)],
                 [pltpu.SemaphoreType.DMA((N_CHIPS,))] * 2, start, finish)


def _scatter_hook(chip_sum):
    _, half, cols = chip_sum.shape

    def copies(ops, outs, sems):
        x, y, c = _mesh_position()
        send, recv = sems
        return [_remote(ops[0].at[2 * px + py], outs[0].at[k], send.at[k], recv.at[k], (px, py, c))
                for k, (px, py) in enumerate(_other_chips(x, y))]

    def start(ops, outs, sems):
        for cp in copies(ops, outs, sems):
            cp.start()

    def finish(ops, outs, sems):
        for cp in copies(ops, outs, sems):
            cp.wait_recv()
            cp.wait_send()

    return _Hook([chip_sum], [jax.ShapeDtypeStruct((3, half, cols), chip_sum.dtype)],
                 [pltpu.SemaphoreType.DMA((3,))] * 2, start, finish)


def _sum_tile(half):
    return 256 if half % 256 == 0 else half


def _chip_add(name, grad, from_sibling, core):
    n_chips, half, cols = from_sibling.shape
    rows = 2 * half
    tr = _sum_tile(half)

    def body(core_ref, g_ref, s_ref, o_ref):
        o_ref[...] = (g_ref[...].astype(F32) + s_ref[...].astype(F32)).astype(o_ref.dtype)

    tile = pl.BlockSpec((None, tr, cols), lambda k, i, core_ref: (k, i, 0))
    return pl.pallas_call(
        body, name=name,
        grid_spec=pltpu.PrefetchScalarGridSpec(
            num_scalar_prefetch=1, grid=(n_chips, half // tr),
            in_specs=[pl.BlockSpec((tr, cols), lambda k, i, core_ref:
                                   (k * (rows // tr) + core_ref[0] * (half // tr) + i, 0)), tile],
            out_specs=tile),
        out_shape=jax.ShapeDtypeStruct(from_sibling.shape, from_sibling.dtype),
        compiler_params=_params(("parallel", "parallel"), 3 * tr * cols * 4),
    )(core, grad, from_sibling)


def _final_add(name, chip_sum, from_chips, chip, core):
    _, half, cols = chip_sum.shape
    tr = _sum_tile(half)

    def body(chip_ref, core_ref, own_ref, others_ref, o_ref):
        total = own_ref[...].astype(F32)
        for k in range(3):
            total = total + others_ref[k].astype(F32)
        o_ref[...] = total

    return pl.pallas_call(
        body, name=name,
        grid_spec=pltpu.PrefetchScalarGridSpec(
            num_scalar_prefetch=2, grid=(half // tr,),
            in_specs=[pl.BlockSpec((None, tr, cols), lambda i, chip_ref, core_ref: (chip_ref[0], i, 0)),
                      pl.BlockSpec((3, tr, cols), lambda i, chip_ref, core_ref: (0, i, 0))],
            out_specs=pl.BlockSpec((tr, cols),
                                   lambda i, chip_ref, core_ref: (core_ref[0] * (half // tr) + i, 0))),
        out_shape=jax.ShapeDtypeStruct((2 * half, cols), F32),
        compiler_params=_params(("parallel",), 6 * tr * cols * 4),
    )(chip, core, chip_sum, from_chips)


def _share_halves(shards, small):
    n_s = len(shards)
    rows_s = small.shape[0]

    def body(*refs):
        small_ref = refs[n_s]
        outs, small_out = refs[n_s + 1:2 * n_s + 1], refs[2 * n_s + 1]
        small_all, send, recv, small_send, small_recv = refs[2 * n_s + 2:]
        x, y, c = _mesh_position()
        me = 4 * x + 2 * y + c
        sibling = (x, y, 1 - c)
        pending = []
        for i in range(n_s):
            half = shards[i].shape[0] // 2
            mine = outs[i].at[pl.ds(pl.multiple_of(c * half, 16), half)]
            cp = _remote(mine, mine, send.at[i], recv.at[i], sibling)
            cp.start()
            pending.append(cp)
        small_all[me] = small_ref[...]
        for j in range(N_DEV - 1):
            peer = (me + 1 + j) % N_DEV
            cp = _remote(small_all.at[me], small_all.at[me], small_send.at[j], small_recv.at[j],
                         (peer // 4, (peer // 2) % 2, peer % 2))
            cp.start()
            pending.append(cp)
        for i in range(n_s):
            half = shards[i].shape[0] // 2
            theirs = outs[i].at[pl.ds(pl.multiple_of((1 - c) * half, 16), half)]
            _remote(theirs, theirs, send.at[i], recv.at[i], sibling).wait_recv()
        for j in range(N_DEV - 1):
            peer = (me + N_DEV - 1 - j) % N_DEV
            _remote(small_all.at[peer], small_all.at[peer], small_send.at[j], small_recv.at[j],
                    sibling).wait_recv()
        total = small_all[0]
        for dev in range(1, N_DEV):
            total = total + small_all[dev]
        small_out[...] = total
        for cp in pending:
            cp.wait_send()

    res = pl.pallas_call(
        body, name="share_halves",
        in_specs=[HBM_SPEC] * n_s + [VMEM_SPEC], out_specs=[HBM_SPEC] * n_s + [VMEM_SPEC],
        out_shape=[jax.ShapeDtypeStruct(sh.shape, sh.dtype) for sh in shards]
        + [jax.ShapeDtypeStruct((rows_s, 128), F32)],
        scratch_shapes=[pltpu.VMEM((N_DEV, rows_s, 128), F32),
                        pltpu.SemaphoreType.DMA((n_s,)), pltpu.SemaphoreType.DMA((n_s,)),
                        pltpu.SemaphoreType.DMA((N_DEV - 1,)), pltpu.SemaphoreType.DMA((N_DEV - 1,))],
        input_output_aliases={i: i for i in range(n_s)},
    )(*shards, small)
    return res[:n_s], res[n_s]


def _pack_small(parts, rows):
    flat = jnp.concatenate([p.reshape(-1) for p in parts])
    flat = jnp.pad(flat, (0, rows * 128 - flat.shape[0]))
    return flat.reshape(rows, 128)


def _unpack_small(packed, shapes):
    flat = packed.reshape(-1)
    out, off = [], 0
    for shp in shapes:
        n = int(np.prod(shp))
        out.append(flat[off:off + n].reshape(shp))
        off += n
    return out


def kernel(x, g_attn, w_in, b_in, sinks_a, g_out_a, g_out_b, w_out, g_mlp, w_1, w_2, g_final, loss_target, m_g_attn, m_w_in, m_b_in, m_sinks_a, m_g_out_a, m_g_out_b, m_w_out, m_g_mlp, m_w_1, m_w_2, m_g_final, v_g_attn, v_w_in, v_b_in, v_sinks_a, v_g_out_a, v_g_out_b, v_w_out, v_g_mlp, v_w_1, v_w_2, v_g_final):
    s, d = x.shape[1], x.shape[2]
    d_in = b_in.shape[1]
    qa = g_out_a.shape[1]
    qb = g_out_b.shape[1]
    kva = 2 * N_KV_GROUPS * HEAD_DIM
    assert d_in == qa + kva + 3 * qb and qa + qb == w_out.shape[1] * N_CHIPS
    d_ff = w_1.shape[2] * N_CHIPS
    ff_shard = w_1.shape[2]
    n_heads_a, n_heads_b = qa // HEAD_DIM, qb // HEAD_DIM
    slopes_a, slopes_b = alibi_slopes(n_heads_a), alibi_slopes(n_heads_b)

    x2d = x[0]
    target = loss_target[0]

    core_index = lax.axis_index("c").astype(jnp.int32).reshape(1)
    chip_index = (2 * lax.axis_index("x") + lax.axis_index("y")).astype(jnp.int32).reshape(1)
    shards = {"w_in": w_in[0].T, "w_out": w_out[0], "w_1": w_1[0], "w_2": w_2[0]}
    halves = [sh.shape[0] // 2 for sh in shards.values()]
    w_in_t, w_out_g, w_1_g, w_2_g = [_own_shard_in_place(f"place_{n}", sh, chip_index)
                                     for n, sh in shards.items()]

    tm = _tile(s, 1024)

    (h1, r1), (w_in_t,) = _norm_fwd("norm_attn", x2d, g_attn,
                                    hook=_gather_hook(w_in_t, 0, halves[0], mid_fraction=1.0))

    q_a, = _project_by_class("proj_qa", h1, w_in_t, b_in, 0, qa, (1,))
    kv_a, = _project_by_class("proj_kva", h1, w_in_t, b_in, qa, kva, (1,))
    q_bs, (w_out_g,) = _project_by_class("proj_qb", h1, w_in_t, b_in, qa + kva, qb, DILATIONS,
                                         hook=_gather_hook(w_out_g, 0, halves[1]))
    k_bs = _project_by_class("proj_kb", h1, w_in_t, b_in, qa + kva + qb, qb, DILATIONS)
    v_bs = _project_by_class("proj_vb", h1, w_in_t, b_in, qa + kva + 2 * qb, qb, DILATIONS)

    quarter = halves[2] // 4
    sinks = sinks_a.reshape(-1)
    (o_a, lse_a), (w_1_g,) = _attn_fwd("attn_a_fwd", q_a, kv_a, dil=1, max_steps=WINDOW_A - 1, slopes=slopes_a,
                                       sinks=sinks, hook=_gather_hook(w_1_g, 0, quarter))
    o_a = o_a[0]
    o_bs, lse_bs = [], []
    for n, (window, dil) in enumerate(DILATED_BRANCHES):
        (o, l), (w_1_g,) = _attn_fwd(f"attn_b{dil}_fwd", q_bs[n], (k_bs[n], v_bs[n]), dil=dil,
                                     max_steps=window // dil, slopes=slopes_b,
                                     hook=_gather_hook(w_1_g, (n + 1) * quarter, (n + 2) * quarter))
        o_bs.append(o)
        lse_bs.append(l)
    w_1_g = w_1_g.reshape(N_CHIPS, d, ff_shard)
    mix, o_b, *lse_tot, r_a, r_b = _mix_fwd(o_a, o_bs, lse_bs, g_out_a, g_out_b)

    tn = _tile(d, 1024)
    a_spec, b_spec = _mm_specs("nn", tm, tn, d)
    tile_mn = pl.BlockSpec((tm, tn), lambda i, j, k: (i, j))
    x2 = _matmul("out_proj", mix, w_out_g, [x2d], mode="nn", grid=(s // tm, d // tn, 1),
                 a_spec=a_spec, b_spec=b_spec, extra_specs=[tile_mn],
                 out_shapes=[jax.ShapeDtypeStruct((s, d), F32)], out_specs=[tile_mn],
                 epilogue=lambda acc, res: (acc + res,))[0]

    h2, r2 = _norm_fwd("norm_mlp", x2, g_mlp)

    tn = _tile(ff_shard, 1024)
    per = ff_shard // tn
    a_spec, _ = _mm_specs("nn", tm, tn, d)
    tile_mn = pl.BlockSpec((tm, tn), lambda i, j, k: (i, j))
    (u,), (w_2_g,) = _matmul(
        "mlp_up", h2, w_1_g, [], mode="nn", grid=(s // tm, d_ff // tn, 1),
        a_spec=a_spec, b_spec=pl.BlockSpec((None, d, tn), lambda i, j, k: (j // per, 0, j % per)),
        extra_specs=[], out_shapes=[jax.ShapeDtypeStruct((s, d_ff), BF16)], out_specs=[tile_mn],
        epilogue=lambda acc: (jnp.maximum(acc, 0.0),),
        hook=_gather_hook(w_2_g, 0, halves[3]))

    tn = _tile(d, 1024)
    tk = _tile(d_ff, 2048)
    a_spec, b_spec = _mm_specs("nn", tm, tn, tk)
    tile_mn = pl.BlockSpec((tm, tn), lambda i, j, k: (i, j))
    x3 = _matmul("mlp_down", u, w_2_g, [x2], mode="nn", grid=(s // tm, d // tn, d_ff // tk),
                 a_spec=a_spec, b_spec=b_spec, extra_specs=[tile_mn],
                 out_shapes=[jax.ShapeDtypeStruct((s, d), F32)], out_specs=[tile_mn],
                 prologue=lambda a: a * a, epilogue=lambda acc, res: (acc + res,), acc_shape=(tm, tn))[0]

    dx3, dx3b, loss_part, dg_final = _loss_head(x3, target, g_final.reshape(1, d))

    tn = _tile(d_ff, 1024)
    a_spec, b_spec = _mm_specs("nt", tm, tn, d)
    tile_mn = pl.BlockSpec((tm, tn), lambda i, j, k: (i, j))
    dpre = _matmul("mlp_down_dx", dx3b, w_2_g, [u], mode="nt", grid=(s // tm, d_ff // tn, 1),
                   a_spec=a_spec, b_spec=b_spec, extra_specs=[tile_mn],
                   out_shapes=[jax.ShapeDtypeStruct((s, d_ff), BF16)], out_specs=[tile_mn],
                   epilogue=lambda acc, uu: (acc * (2.0 * uu.astype(F32)),))[0]

    wire = GRAD_WIRE_DTYPE
    tk_s = _tile(s, 2048)
    tmw = _tile(d_ff, 1024)
    a_spec, b_spec = _mm_specs("tn", tmw, d, tk_s)
    dw_2 = _matmul("mlp_down_dw", u, dx3b, [], mode="tn", grid=(d_ff // tmw, 1, s // tk_s),
                   a_spec=a_spec, b_spec=b_spec, extra_specs=[],
                   out_shapes=[jax.ShapeDtypeStruct((d_ff, d), wire)],
                   out_specs=[pl.BlockSpec((tmw, d), lambda i, j, k: (i, j))],
                   prologue=lambda a: a * a, epilogue=lambda acc: (acc,), acc_shape=(tmw, d))[0]

    tn = _tile(d, 1024)
    tk = _tile(ff_shard, 2048)
    per = ff_shard // tk
    a_spec, _ = _mm_specs("nt", tm, tn, tk)
    tile_mn = pl.BlockSpec((tm, tn), lambda i, j, k: (i, j))
    dh2 = _matmul("mlp_up_dx", dpre, w_1_g, [], mode="nt", grid=(s // tm, d // tn, d_ff // tk),
                  a_spec=a_spec, b_spec=pl.BlockSpec((None, tn, tk), lambda i, j, k: (k // per, j, k % per)),
                  extra_specs=[], out_shapes=[jax.ShapeDtypeStruct((s, d), F32)], out_specs=[tile_mn],
                  epilogue=lambda acc: (acc,), acc_shape=(tm, tn))[0]

    tmw = _tile(d, 1024)
    tnw = _tile(ff_shard, 2048)
    per = ff_shard // tnw
    a_spec, b_spec = _mm_specs("tn", tmw, tnw, tk_s)
    dw_1 = _matmul("mlp_up_dw", h2, dpre, [], mode="tn", grid=(d // tmw, d_ff // tnw, s // tk_s),
                   a_spec=a_spec, b_spec=b_spec, extra_specs=[],
                   out_shapes=[jax.ShapeDtypeStruct((N_CHIPS, d, ff_shard), wire)],
                   out_specs=[pl.BlockSpec((None, tmw, tnw), lambda i, j, k: (j // per, i, j % per))],
                   epilogue=lambda acc: (acc,), acc_shape=(tmw, tnw))[0]

    dw_1 = dw_1.reshape(N_CHIPS * d, ff_shard)
    (dx2, dx2b, dg_mlp), (sib_2, sib_1) = _norm_bwd(
        "norm_mlp_bwd", dh2, x2, r2, g_mlp, dx3, hook=_merge_hooks([_exchange_hook(dw_2), _exchange_hook(dw_1)]))
    chip_sum_2 = _chip_add("chip_add_w_2", dw_2, sib_2, core_index)
    chip_sum_1 = _chip_add("chip_add_w_1", dw_1, sib_1, core_index)

    tn = _tile(d, 1024)
    a_spec, b_spec = _mm_specs("nt", tm, tn, d)
    tile_mn = pl.BlockSpec((tm, tn), lambda i, j, k: (i, j))
    dmix = _matmul("out_proj_dx", dx2b, w_out_g, [], mode="nt", grid=(s // tm, d // tn, 1),
                   a_spec=a_spec, b_spec=b_spec, extra_specs=[],
                   out_shapes=[jax.ShapeDtypeStruct((s, d), F32)], out_specs=[tile_mn],
                   epilogue=lambda acc: (acc,))[0]

    tmw = _tile(d, 1024)
    a_spec, b_spec = _mm_specs("tn", tmw, d, tk_s)
    dw_out = _matmul("out_proj_dw", mix, dx2b, [], mode="tn", grid=(d // tmw, 1, s // tk_s),
                     a_spec=a_spec, b_spec=b_spec, extra_specs=[],
                     out_shapes=[jax.ShapeDtypeStruct((d, d), wire)],
                     out_specs=[pl.BlockSpec((tmw, d), lambda i, j, k: (i, j))],
                     epilogue=lambda acc: (acc,), acc_shape=(tmw, d))[0]

    mix_grads, (sib_out,) = _mix_bwd(dmix, o_a, o_b, r_a, r_b, g_out_a, g_out_b, hook=_exchange_hook(dw_out))
    do_a, do_bs, delta_a, delta_bs = mix_grads[0], mix_grads[1:4], mix_grads[4], mix_grads[5:8]
    dg_out_a, dg_out_b = mix_grads[8:]
    chip_sum_out = _chip_add("chip_add_w_out", dw_out, sib_out, core_index)

    (dq_a, dkv_a, dsinks), (chips_2,) = _attn_bwd(
        "attn_a_bwd", q_a, kv_a, do_a[None], lse_a, delta_a[None], dil=1, max_steps=WINDOW_A - 1,
        slopes=slopes_a, sinks=sinks, hook=_scatter_hook(chip_sum_2))
    dqs, dks, dvs = [], [], []
    scatter = {1: chip_sum_1, 4: chip_sum_out}
    arrived = {}
    for n, (window, dil) in enumerate(DILATED_BRANCHES):
        res = _attn_bwd(f"attn_b{dil}_bwd", q_bs[n], (k_bs[n], v_bs[n]), do_bs[n], lse_tot[n],
                        delta_bs[n], dil=dil, max_steps=window // dil, slopes=slopes_b,
                        hook=_scatter_hook(scatter[dil]) if dil in scatter else None)
        if dil in scatter:
            res, (arrived[dil],) = res
        dq, dk, dv = res
        dqs.append(dq)
        dks.append(dk)
        dvs.append(dv)
    half_2 = _final_add("final_add_w_2", chip_sum_2, chips_2, chip_index, core_index)
    half_1 = _final_add("final_add_w_1", chip_sum_1, arrived[1], chip_index, core_index)
    half_out = _final_add("final_add_w_out", chip_sum_out, arrived[4], chip_index, core_index)
    dproj, db_in = _assemble_dproj(dq_a[0], dkv_a[0], dqs, dks, dvs)

    tmw = d_in // 2 if (d_in // 2) % 128 == 0 else d_in
    tnw = _tile(d, 1024)
    tk_s = _tile(s, 1024)
    a_spec, b_spec = _mm_specs("tn", tmw, tnw, tk_s)
    dw_in_t = _matmul("in_proj_dw", dproj, h1, [], mode="tn", grid=(d_in // tmw, d // tnw, s // tk_s),
                      a_spec=a_spec, b_spec=b_spec, extra_specs=[],
                      out_shapes=[jax.ShapeDtypeStruct((d_in, d), wire)],
                      out_specs=[pl.BlockSpec((tmw, tnw), lambda i, j, k: (i, j))],
                      epilogue=lambda acc: (acc,), acc_shape=(tmw, tnw))[0]

    tn = _tile(d, 1024)
    a_spec, b_spec = _mm_specs("nn", tm, tn, d_in)
    tile_mn = pl.BlockSpec((tm, tn), lambda i, j, k: (i, j))
    (dh1,), (sib_in,) = _matmul("in_proj_dx", dproj, w_in_t, [], mode="nn", grid=(s // tm, d // tn, 1),
                                a_spec=a_spec, b_spec=b_spec, extra_specs=[],
                                out_shapes=[jax.ShapeDtypeStruct((s, d), F32)], out_specs=[tile_mn],
                                epilogue=lambda acc: (acc,), hook=_exchange_hook(dw_in_t))
    chip_sum_in = _chip_add("chip_add_w_in", dw_in_t, sib_in, core_index)

    (grad_x, _, dg_attn), (chips_in,) = _norm_bwd("norm_attn_bwd", dh1, x2d, r1, g_attn, dx2,
                                                  hook=_scatter_hook(chip_sum_in))
    half_in = _final_add("final_add_w_in", chip_sum_in, chips_in, chip_index, core_index)

    small_parts = [dg_attn, db_in, dsinks[:, :n_heads_a], dg_out_a, dg_out_b, dg_mlp, dg_final]
    small_shapes = [g_attn.shape, b_in.shape, sinks_a.shape, g_out_a.shape, g_out_b.shape, g_mlp.shape,
                    g_final.shape]
    n_small = sum(int(np.prod(shp)) for shp in small_shapes)
    rows_s = -(-n_small // (8 * 128)) * 8
    (gw_in_t, gw_out, gw_1, gw_2), small_sum = _share_halves(
        [half_in, half_out, half_1, half_2], _pack_small(small_parts, rows_s))

    upd_in = [t.T for t in _adamw("adamw_w_in", w_in[0].T, gw_in_t, m_w_in[0].T, v_w_in[0].T)]
    upd_out = _adamw("adamw_w_out", w_out[0], gw_out, m_w_out[0], v_w_out[0])
    upd_1 = _adamw("adamw_w_1", w_1[0], gw_1, m_w_1[0], v_w_1[0])
    upd_2 = _adamw("adamw_w_2", w_2[0], gw_2, m_w_2[0], v_w_2[0])
    small_w = [g_attn, b_in, sinks_a, g_out_a, g_out_b, g_mlp, g_final]
    small_m = [m_g_attn, m_b_in, m_sinks_a, m_g_out_a, m_g_out_b, m_g_mlp, m_g_final]
    small_v = [v_g_attn, v_b_in, v_sinks_a, v_g_out_a, v_g_out_b, v_g_mlp, v_g_final]
    upd_small = _adamw("adamw_small", _pack_small(small_w, rows_s), small_sum,
                       _pack_small(small_m, rows_s), _pack_small(small_v, rows_s))
    d_small, m_small, v_small, g_small = [_unpack_small(t, small_shapes) for t in upd_small]

    loss = lax.psum(loss_part[0, 0], ("x", "y", "c"))

    def ordered(small, big):
        w_in_v, w_out_v, w_1_v, w_2_v = big
        return [small[0], w_in_v[None], small[1], small[2], small[3], small[4], w_out_v[None], small[5],
                w_1_v[None], w_2_v[None], small[6]]

    grads = ordered(g_small, (upd_in[3], upd_out[3], upd_1[3], upd_2[3]))
    deltas = ordered(d_small, (upd_in[0], upd_out[0], upd_1[0], upd_2[0]))
    new_m = ordered(m_small, (upd_in[1], upd_out[1], upd_1[1], upd_2[1]))
    new_v = ordered(v_small, (upd_in[2], upd_out[2], upd_1[2], upd_2[2]))
    return (loss, grad_x[None], *grads, *deltas, *new_m, *new_v)
```

```python
import jax
import jax.numpy as jnp
import numpy as np
from jax import lax
from jax.experimental import pallas as pl
from jax.experimental.pallas import tpu as pltpu

F32 = jnp.float32
BF16 = jnp.bfloat16

HEAD_DIM = 64
BLOCK = 128
PAIR = 2 * HEAD_DIM
N_KV_GROUPS = 2
WINDOW_A = 128
DILATED_BRANCHES = ((128, 1), (512, 4), (2048, 16))
EPS = 1e-5
NEG_INF = -1e30
ATT_SCALE = HEAD_DIM ** -0.5

ADAM_LR = 0.001
ADAM_B1 = 0.9
ADAM_B2 = 0.999
ADAM_EPS = 1e-08
ADAM_WD = 0.01
ADAM_STEP = 10

N_CHIPS = 4
N_DEV = 8
MESH = pl.DeviceIdType.MESH
GRAD_WIRE_DTYPE = jnp.bfloat16
BRANCH_DTYPE = jnp.bfloat16

VMEM_CAPACITY_V7X = 64 * 1024 * 1024
VMEM_LIMIT_MAX = VMEM_CAPACITY_V7X - 8 * 1024 * 1024
VMEM_LIMIT_MIN = VMEM_CAPACITY_V7X - 16 * 1024 * 1024

HBM_SPEC = pl.BlockSpec(memory_space=pltpu.HBM)
VMEM_SPEC = pl.BlockSpec(memory_space=pltpu.VMEM)
SMEM_SPEC = pl.BlockSpec(memory_space=pltpu.SMEM)


def _nbytes(shape, dtype):
    return int(np.prod([s for s in shape if s is not None])) * jnp.dtype(dtype).itemsize


def _params(semantics, block_bytes):
    limit = min(max(2 * block_bytes + (4 << 20), VMEM_LIMIT_MIN), VMEM_LIMIT_MAX)
    return pltpu.CompilerParams(dimension_semantics=semantics, vmem_limit_bytes=limit)


class _Hook:
    def __init__(self, operands, out_shape, sems, start, finish, mid=None, aliases=None,
                 mid_fraction=0.6):
        self.operands, self.out_shape, self.sems = list(operands), list(out_shape), list(sems)
        self.start, self.mid, self.finish = start, mid, finish
        self.aliases = dict(aliases or {})
        self.mid_fraction = mid_fraction


def _call(body, hook, *, name, grid, in_specs, out_specs, out_shape, scratch_shapes=(), compiler_params):
    in_specs, out_specs, out_shape = list(in_specs), list(out_specs), list(out_shape)
    scratch_shapes = list(scratch_shapes)
    if hook is None:
        call = pl.pallas_call(body, name=name, grid=grid, in_specs=in_specs, out_specs=out_specs,
                              out_shape=out_shape, scratch_shapes=scratch_shapes,
                              compiler_params=compiler_params)
        return lambda *operands: (call(*operands), [])
    n_in, n_hin, n_out, n_hout, n_scr = (len(in_specs), len(hook.operands), len(out_specs),
                                         len(hook.out_shape), len(scratch_shapes))
    total = int(np.prod(grid))
    t_mid = min(int(total * hook.mid_fraction), total - 1)

    def wrapped(*refs):
        ins, h_in = refs[:n_in], refs[n_in:n_in + n_hin]
        o0 = n_in + n_hin
        outs, h_out = refs[o0:o0 + n_out], refs[o0 + n_out:o0 + n_out + n_hout]
        s0 = o0 + n_out + n_hout
        scr, h_sems = refs[s0:s0 + n_scr], refs[s0 + n_scr:]
        t = pl.program_id(0)
        for axis in range(1, len(grid)):
            t = t * grid[axis] + pl.program_id(axis)

        @pl.when(t == 0)
        def _():
            hook.start(h_in, h_out, h_sems)

        body(*ins, *outs, *scr)
        if hook.mid is not None:
            @pl.when(t == t_mid)
            def _():
                hook.mid(h_in, h_out, h_sems)

        @pl.when(t == total - 1)
        def _():
            hook.finish(h_in, h_out, h_sems)

    params = pltpu.CompilerParams(dimension_semantics=("arbitrary",) * len(grid),
                                  vmem_limit_bytes=compiler_params.vmem_limit_bytes)
    call = pl.pallas_call(
        wrapped, name=name, grid=grid,
        in_specs=in_specs + [HBM_SPEC] * n_hin, out_specs=out_specs + [HBM_SPEC] * n_hout,
        out_shape=out_shape + hook.out_shape, scratch_shapes=scratch_shapes + hook.sems,
        input_output_aliases={n_in + a: n_out + b for a, b in hook.aliases.items()},
        compiler_params=params)

    def run(*operands):
        res = call(*operands, *hook.operands)
        return res[:n_out], res[n_out:]

    return run


def _remote(src, dst, send_sem, recv_sem, device):
    return pltpu.make_async_remote_copy(src_ref=src, dst_ref=dst, send_sem=send_sem, recv_sem=recv_sem,
                                        device_id=device, device_id_type=MESH)


def alibi_slopes(n):
    return [float(v) for v in np.asarray(2.0 ** (-8.0 * (np.arange(n) + 1) / n), dtype=np.float32)]


def _matmul(name, a, b, extras, *, mode, grid, a_spec, b_spec, extra_specs, out_shapes, out_specs,
            epilogue, prologue=None, acc_shape=None, hook=None):
    dims = {"nn": ((1,), (0,)), "nt": ((1,), (1,)), "tn": ((0,), (0,))}[mode]
    nk = grid[2]
    n_ex, n_out = len(extras), len(out_shapes)

    def body(a_ref, b_ref, *rest):
        ex, outs = rest[:n_ex], rest[n_ex:n_ex + n_out]
        av = a_ref[...]
        if prologue is not None:
            av = prologue(av)
        part = lax.dot_general(av, b_ref[...], (dims, ((), ())), preferred_element_type=F32)

        def finish(acc):
            res = epilogue(acc, *[e[...] for e in ex])
            for o, r in zip(outs, res):
                o[...] = r.astype(o.dtype)

        if nk == 1:
            finish(part)
        else:
            acc_ref = rest[-1]
            k = pl.program_id(2)

            @pl.when(k == 0)
            def _():
                acc_ref[...] = part

            @pl.when(k > 0)
            def _():
                acc_ref[...] += part

            @pl.when(k == nk - 1)
            def _():
                finish(acc_ref[...])

    blocks = [(a_spec.block_shape, a.dtype), (b_spec.block_shape, b.dtype)]
    blocks += [(s.block_shape, e.dtype) for s, e in zip(extra_specs, extras)]
    blocks += [(s.block_shape, o.dtype) for s, o in zip(out_specs, out_shapes)]
    nbytes = sum(_nbytes(s, d) for s, d in blocks)
    scratch = []
    if nk > 1:
        scratch.append(pltpu.VMEM(acc_shape, F32))
        nbytes += _nbytes(acc_shape, F32)
    res, hook_res = _call(
        body, hook, name=name, grid=grid,
        in_specs=[a_spec, b_spec, *extra_specs], out_specs=list(out_specs), out_shape=list(out_shapes),
        scratch_shapes=scratch,
        compiler_params=_params(("parallel", "parallel", "arbitrary"), nbytes),
    )(a, b, *extras)
    return res if hook is None else (res, hook_res)


def _mm_specs(mode, tm, tn, tk, b_block=None, b_map=None):
    if mode == "tn":
        a_spec = pl.BlockSpec((tk, tm), lambda i, j, k: (k, i))
    else:
        a_spec = pl.BlockSpec((tm, tk), lambda i, j, k: (i, k))
    if b_block is not None:
        b_spec = pl.BlockSpec(b_block, b_map)
    elif mode == "nt":
        b_spec = pl.BlockSpec((tn, tk), lambda i, j, k: (j, k))
    else:
        b_spec = pl.BlockSpec((tk, tn), lambda i, j, k: (k, j))
    return a_spec, b_spec


def _project_by_class(name, h, w_t, bias, row_off, width, dilations, hook=None):
    s, d = h.shape
    tm = _tile(s, 2048)
    tn = 512 if width % 512 == 0 and row_off % 512 == 0 else _tile(width, 256)
    off = row_off // tn
    assert row_off % tn == 0 and tn % 128 == 0
    n_out = len(dilations)

    def body(h_ref, w_ref, b_ref, *rest):
        outs, perm_ref = rest[:n_out], rest[n_out]
        acc = lax.dot_general(h_ref[...], w_ref[...], (((1,), (1,)), ((), ())), preferred_element_type=F32)
        acc = acc + b_ref[...]
        for j in range(tn // 128):
            cols = slice(j * 128, (j + 1) * 128)
            _to_classes(outs, cols, acc[:, cols], perm_ref, dilations)

    blocks = tm * d * 2 + tn * d * 2 + 3 * tm * tn * 2 + tm * 128 * 4
    res, hook_res = _call(
        body, hook, name=name, grid=(s // tm, width // tn),
        in_specs=[pl.BlockSpec((tm, d), lambda i, j: (i, 0)), pl.BlockSpec((tn, d), lambda i, j: (j + off, 0)),
                  pl.BlockSpec((1, tn), lambda i, j: (0, j + off))],
        out_specs=[pl.BlockSpec((dil, tm // dil, tn), lambda i, j: (0, i, j)) for dil in dilations],
        out_shape=[_class_shape(dil, s, width, BF16) for dil in dilations],
        scratch_shapes=[pltpu.VMEM((tm, 128), F32)],
        compiler_params=_params(("parallel", "parallel"), blocks),
    )(h, w_t, bias)
    return res if hook is None else (res, hook_res)


def _tile(n, want):
    if n <= want:
        return n
    t = (want // 128) * 128
    while t > 128 and n % t:
        t -= 128
    assert n % t == 0, (n, want)
    return t


def _row_tile(s):
    return 256 if s % 256 == 0 else s


def _norm_fwd(name, x, g, hook=None):
    s, d = x.shape
    tm = _row_tile(s)

    def body(x_ref, g_ref, h_ref, r_ref):
        xv = x_ref[...]
        r = lax.rsqrt(jnp.mean(xv * xv, axis=-1, keepdims=True) + EPS)
        h_ref[...] = ((xv * r) * g_ref[...]).astype(BF16)
        r_ref[...] = r

    row = pl.BlockSpec((tm, d), lambda i: (i, 0))
    res, hook_res = _call(
        body, hook, name=name, grid=(s // tm,),
        in_specs=[row, pl.BlockSpec((1, d), lambda i: (0, 0))],
        out_specs=[row, pl.BlockSpec((tm, 1), lambda i: (i, 0))],
        out_shape=[jax.ShapeDtypeStruct((s, d), BF16), jax.ShapeDtypeStruct((s, 1), F32)],
        compiler_params=_params(("parallel",), tm * d * 6),
    )(x, g)
    return res if hook is None else (res, hook_res)


def _norm_bwd(name, dh, x, r, g, dres, hook=None):
    s, d = x.shape
    tm = _row_tile(s)

    def body(dh_ref, x_ref, r_ref, g_ref, dres_ref, dx_ref, dxb_ref, dg_ref):
        rv = r_ref[...]
        xn = x_ref[...] * rv
        dhv = dh_ref[...]
        dxn = dhv * g_ref[...]
        dx = dres_ref[...] + rv * (dxn - xn * jnp.mean(dxn * xn, axis=-1, keepdims=True))
        dx_ref[...] = dx
        dxb_ref[...] = dx.astype(BF16)
        part = jnp.sum(dhv * xn, axis=0, keepdims=True)

        @pl.when(pl.program_id(0) == 0)
        def _():
            dg_ref[...] = part

        @pl.when(pl.program_id(0) > 0)
        def _():
            dg_ref[...] += part

    row = pl.BlockSpec((tm, d), lambda i: (i, 0))
    vec = pl.BlockSpec((1, d), lambda i: (0, 0))
    res, hook_res = _call(
        body, hook, name=name, grid=(s // tm,),
        in_specs=[row, row, pl.BlockSpec((tm, 1), lambda i: (i, 0)), vec, row],
        out_specs=[row, row, vec],
        out_shape=[jax.ShapeDtypeStruct((s, d), F32), jax.ShapeDtypeStruct((s, d), BF16),
                   jax.ShapeDtypeStruct((1, d), F32)],
        compiler_params=_params(("arbitrary",), tm * d * 18),
    )(dh, x, r, g, dres)
    return res if hook is None else (res, hook_res)


def _loss_head(x3, target, g):
    s, d = x3.shape
    tm = _row_tile(s)

    def body(x_ref, t_ref, g_ref, dx_ref, dxb_ref, loss_ref, dg_ref):
        xv = x_ref[...]
        gv = g_ref[...]
        r = lax.rsqrt(jnp.mean(xv * xv, axis=-1, keepdims=True) + EPS)
        xn = xv * r
        err = xn * gv - t_ref[...]
        loss = 0.5 * jnp.sum(jnp.mean(err * err, axis=-1, keepdims=True), axis=0, keepdims=True)
        dy = err / d
        dxn = dy * gv
        dx = r * (dxn - xn * jnp.mean(dxn * xn, axis=-1, keepdims=True))
        dx_ref[...] = dx
        dxb_ref[...] = dx.astype(BF16)
        dg = jnp.sum(dy * xn, axis=0, keepdims=True)
        loss_row = jnp.broadcast_to(loss, (1, 128))

        @pl.when(pl.program_id(0) == 0)
        def _():
            dg_ref[...] = dg
            loss_ref[...] = loss_row

        @pl.when(pl.program_id(0) > 0)
        def _():
            dg_ref[...] += dg
            loss_ref[...] += loss_row

    row = pl.BlockSpec((tm, d), lambda i: (i, 0))
    vec = pl.BlockSpec((1, d), lambda i: (0, 0))
    return _call(
        body, None, name="loss_head", grid=(s // tm,),
        in_specs=[row, row, vec],
        out_specs=[row, row, pl.BlockSpec((1, 128), lambda i: (0, 0)), vec],
        out_shape=[jax.ShapeDtypeStruct((s, d), F32), jax.ShapeDtypeStruct((s, d), BF16),
                   jax.ShapeDtypeStruct((1, 128), F32), jax.ShapeDtypeStruct((1, d), F32)],
        compiler_params=_params(("arbitrary",), tm * d * 14),
    )(x3, target, g)[0]


def _low_lanes(rows):
    return lax.broadcasted_iota(jnp.int32, (rows, PAIR), 1) < HEAD_DIM


def _to_classes(dst_refs, cols, value, perm_ref, dils):
    rows = value.shape[0]
    if any(dil > 1 for dil in dils):
        perm_ref[...] = value
    for dst_ref, dil in zip(dst_refs, dils):
        if dil == 1:
            dst_ref[0, :, cols] = value.astype(dst_ref.dtype)
            continue
        for r in range(dil):
            dst_ref[r, :, cols] = perm_ref[pl.ds(r, rows // dil, stride=dil), :].astype(dst_ref.dtype)


def _from_classes(src_ref, cols, perm_ref, dil):
    if dil == 1:
        return src_ref[0, :, cols].astype(F32)
    rows = perm_ref.shape[0]
    for r in range(dil):
        perm_ref[pl.ds(r, rows // dil, stride=dil), :] = src_ref[r, :, cols].astype(F32)
    return perm_ref[...]


def _class_spec(dil, tm, width):
    return pl.BlockSpec((dil, tm // dil, width), lambda i: (0, i, 0))


def _class_shape(dil, s, width, dtype):
    return jax.ShapeDtypeStruct((dil, s // dil, width), dtype)


DILATIONS = tuple(d for _, d in DILATED_BRANCHES)


def _mix_fwd(oa, obs, lses, ga, gb):
    s, qa = oa.shape
    qb = obs[0].shape[2]
    tm = _row_tile(s)
    all_lanes = slice(0, 128)

    def body(oa_ref, o1_ref, o2_ref, o3_ref, l1_ref, l2_ref, l3_ref, ga_ref, gb_ref,
             mix_ref, ob_ref, t1_ref, t2_ref, t3_ref, ra_ref, rb_ref, perm_ref):
        oav = oa_ref[...]
        ra = lax.rsqrt(jnp.mean(oav * oav, axis=-1, keepdims=True) + EPS)
        ra_ref[...] = ra
        mix_ref[:, 0:qa] = ((oav * ra) * ga_ref[...]).astype(BF16)
        l1, l2, l3 = [_from_classes(l_ref, all_lanes, perm_ref, dil)
                      for l_ref, dil in zip((l1_ref, l2_ref, l3_ref), DILATIONS)]
        mx = jnp.maximum(jnp.maximum(l1, l2), l3)
        e1, e2, e3 = jnp.exp(l1 - mx), jnp.exp(l2 - mx), jnp.exp(l3 - mx)
        tot = e1 + e2 + e3
        lse = mx + jnp.log(tot)
        _to_classes((t1_ref, t2_ref, t3_ref), all_lanes, lse, perm_ref, DILATIONS)
        ws = (e1 / tot, e2 / tot, e3 / tot)
        low = _low_lanes(tm)
        ssq = jnp.zeros((tm, 1), F32)
        for i in range(qb // PAIR):
            sl = slice(i * PAIR, (i + 1) * PAIR)
            acc = jnp.zeros((tm, PAIR), F32)
            for w, o_ref, dil in zip(ws, (o1_ref, o2_ref, o3_ref), DILATIONS):
                wexp = jnp.where(low, w[:, 2 * i:2 * i + 1], w[:, 2 * i + 1:2 * i + 2])
                acc = acc + wexp * _from_classes(o_ref, sl, perm_ref, dil)
            ob_ref[:, sl] = acc
            ssq = ssq + jnp.sum(acc * acc, axis=-1, keepdims=True)
        rb = lax.rsqrt(ssq / qb + EPS)
        rb_ref[...] = rb
        mix_ref[:, qa:qa + qb] = ((ob_ref[...] * rb) * gb_ref[...]).astype(BF16)

    def row(w):
        return pl.BlockSpec((tm, w), lambda i: (i, 0))

    def vec(w):
        return pl.BlockSpec((1, w), lambda i: (0, 0))

    return _call(
        body, None, name="mix_fwd", grid=(s // tm,),
        in_specs=([row(qa)] + [_class_spec(d, tm, qb) for d in DILATIONS]
                  + [_class_spec(d, tm, 128) for d in DILATIONS] + [vec(qa), vec(qb)]),
        out_specs=([row(qa + qb), row(qb)] + [_class_spec(d, tm, 128) for d in DILATIONS] + [row(1), row(1)]),
        out_shape=([jax.ShapeDtypeStruct((s, qa + qb), BF16), jax.ShapeDtypeStruct((s, qb), F32)]
                   + [_class_shape(d, s, 128, F32) for d in DILATIONS]
                   + [jax.ShapeDtypeStruct((s, 1), F32), jax.ShapeDtypeStruct((s, 1), F32)]),
        scratch_shapes=[pltpu.VMEM((tm, 128), F32)],
        compiler_params=_params(("parallel",), tm * (qa + 4 * qb) * 4 + tm * (qa + qb) * 2 + tm * 4096),
    )(oa, *obs, *lses, ga, gb)[0]


def _head_rowsums(prod, rows):
    low = _low_lanes(rows)
    lane = lax.broadcasted_iota(jnp.int32, (rows, 128), 1)
    out = jnp.zeros((rows, 128), F32)
    for i in range(prod.shape[1] // PAIR):
        tile = prod[:, i * PAIR:(i + 1) * PAIR]
        lo = jnp.sum(jnp.where(low, tile, 0.0), axis=-1, keepdims=True)
        hi = jnp.sum(jnp.where(low, 0.0, tile), axis=-1, keepdims=True)
        out = jnp.where(lane == 2 * i, lo, out)
        out = jnp.where(lane == 2 * i + 1, hi, out)
    return out


def _mix_bwd(dmix, oa, ob, ra, rb, ga, gb, hook=None):
    s, qa = oa.shape
    qb = ob.shape[1]
    tm = _row_tile(s)

    def one(dy, o, r, g):
        xn = o * r
        dxn = dy * g
        do = r * (dxn - xn * jnp.mean(dxn * xn, axis=-1, keepdims=True))
        return do, jnp.sum(dy * xn, axis=0, keepdims=True), _head_rowsums(do * o, tm)

    def body(dmix_ref, oa_ref, ob_ref, ra_ref, rb_ref, ga_ref, gb_ref,
             doa_ref, dob1_ref, dob2_ref, dob3_ref, dla_ref, dlb1_ref, dlb2_ref, dlb3_ref,
             dga_ref, dgb_ref, perm_ref):
        doa, dga, dla = one(dmix_ref[:, 0:qa], oa_ref[...], ra_ref[...], ga_ref[...])
        dob, dgb, dlb = one(dmix_ref[:, qa:qa + qb], ob_ref[...], rb_ref[...], gb_ref[...])
        doa_ref[...] = doa.astype(BF16)
        dla_ref[...] = dla
        _to_classes((dlb1_ref, dlb2_ref, dlb3_ref), slice(0, 128), dlb, perm_ref, DILATIONS)
        for i in range(qb // PAIR):
            sl = slice(i * PAIR, (i + 1) * PAIR)
            _to_classes((dob1_ref, dob2_ref, dob3_ref), sl, dob[:, sl], perm_ref, DILATIONS)

        @pl.when(pl.program_id(0) == 0)
        def _():
            dga_ref[...] = dga
            dgb_ref[...] = dgb

        @pl.when(pl.program_id(0) > 0)
        def _():
            dga_ref[...] += dga
            dgb_ref[...] += dgb

    def row(w):
        return pl.BlockSpec((tm, w), lambda i: (i, 0))

    def vec(w):
        return pl.BlockSpec((1, w), lambda i: (0, 0))

    res, hook_res = _call(
        body, hook, name="mix_bwd", grid=(s // tm,),
        in_specs=[row(qa + qb), row(qa), row(qb), row(1), row(1), vec(qa), vec(qb)],
        out_specs=([row(qa)] + [_class_spec(d, tm, qb) for d in DILATIONS] + [row(128)]
                   + [_class_spec(d, tm, 128) for d in DILATIONS] + [vec(qa), vec(qb)]),
        out_shape=([jax.ShapeDtypeStruct((s, qa), BF16)] + [_class_shape(d, s, qb, BF16) for d in DILATIONS]
                   + [jax.ShapeDtypeStruct((s, 128), F32)] + [_class_shape(d, s, 128, F32) for d in DILATIONS]
                   + [jax.ShapeDtypeStruct((1, qa), F32), jax.ShapeDtypeStruct((1, qb), F32)]),
        scratch_shapes=[pltpu.VMEM((tm, 128), F32)],
        compiler_params=_params(("arbitrary",), tm * (qa + qb) * 16),
    )(dmix, oa, ob, ra, rb, ga, gb)
    return res if hook is None else (res, hook_res)


def _assemble_dproj(dqa, dkva, dqs, dks, dvs):
    s, qa = dqa.shape
    kva = dkva.shape[1]
    qb = dqs[0].shape[2]
    width = qa + kva + 3 * qb
    tm = _row_tile(s)

    def body(dqa_ref, dkva_ref, q1, q2, q3, k1, k2, k3, v1, v2, v3, dp_ref, db_ref, perm_ref):
        first = pl.program_id(0) == 0

        def emit(off, val):
            dp_ref[:, off:off + PAIR] = val.astype(BF16)
            col = jnp.sum(val, axis=0, keepdims=True)

            @pl.when(first)
            def _():
                db_ref[:, off:off + PAIR] = col

            @pl.when(jnp.logical_not(first))
            def _():
                db_ref[:, off:off + PAIR] += col

        for i in range(qa // PAIR):
            emit(i * PAIR, dqa_ref[:, i * PAIR:(i + 1) * PAIR])
        for i in range(kva // PAIR):
            emit(qa + i * PAIR, dkva_ref[:, i * PAIR:(i + 1) * PAIR])
        for j, branch_refs in enumerate(((q1, q2, q3), (k1, k2, k3), (v1, v2, v3))):
            for i in range(qb // PAIR):
                sl = slice(i * PAIR, (i + 1) * PAIR)
                total = None
                for ref, dil in zip(branch_refs, DILATIONS):
                    val = _from_classes(ref, sl, perm_ref, dil)
                    total = val if total is None else total + val
                emit(qa + kva + j * qb + i * PAIR, total)

    def row(w):
        return pl.BlockSpec((tm, w), lambda i: (i, 0))

    return _call(
        body, None, name="assemble_dproj", grid=(s // tm,),
        in_specs=[row(qa), row(kva)] + [_class_spec(d, tm, qb) for d in DILATIONS] * 3,
        out_specs=[row(width), pl.BlockSpec((1, width), lambda i: (0, 0))],
        out_shape=[jax.ShapeDtypeStruct((s, width), BF16), jax.ShapeDtypeStruct((1, width), F32)],
        scratch_shapes=[pltpu.VMEM((tm, 128), F32)],
        compiler_params=_params(("arbitrary",), tm * (qa + kva + 9 * qb) * 4 + tm * width * 2),
    )(dqa, dkva, *dqs, *dks, *dvs)[0]


def _fill_bias(bias_ref, n_pairs, max_steps, dil, slopes, sink_ref=None):
    qi = lax.broadcasted_iota(jnp.int32, (BLOCK, 2 * BLOCK), 0)
    kj = lax.broadcasted_iota(jnp.int32, (BLOCK, 2 * BLOCK), 1)
    steps = qi + BLOCK - kj
    dist = (steps * dil).astype(F32)
    band = (steps >= 0) & (steps <= max_steps)
    assert sink_ref is None or max_steps < BLOCK
    for first in (0, 1):
        valid = band & (kj >= BLOCK) if first else band
        for i in range(n_pairs):
            tables = []
            for half in (0, 1):
                table = jnp.where(valid, -(slopes[2 * i + half] * dist), NEG_INF)
                if sink_ref is not None:
                    table = jnp.where(kj == 0, sink_ref[2 * i + half], table)
                tables.append(table)
            bias_ref[first, i] = jnp.concatenate(tables, axis=0)


def _without_sink_row(tile):
    row = lax.broadcasted_iota(jnp.int32, tile.shape, 0)
    return jnp.where(row == 0, jnp.zeros_like(tile), tile)


def _bias_shape(n_pairs):
    return pltpu.VMEM((2, n_pairs, 2 * BLOCK, 2 * BLOCK), F32)


def _stack_heads(tile, low):
    zero = jnp.zeros_like(tile)
    return jnp.concatenate([jnp.where(low, tile, zero), jnp.where(low, zero, tile)], axis=0)


def _unstack_heads(stacked, low):
    return jnp.where(low, stacked[0:BLOCK], stacked[BLOCK:2 * BLOCK])


def _head_columns(ref, i):
    return jnp.concatenate([ref[:, 2 * i:2 * i + 1], ref[:, 2 * i + 1:2 * i + 2]], axis=0)


def _swap_halves(t):
    return pltpu.roll(t, HEAD_DIM, 1)


def _dup_group(t_bf16, group):
    t = t_bf16.astype(F32)
    low = lax.broadcasted_iota(jnp.int32, t.shape, 1) < HEAD_DIM
    keep = low if group == 0 else jnp.logical_not(low)
    return jnp.where(keep, t, _swap_halves(t)).astype(BF16)


def _attn_fwd(name, q, kv, *, dil, max_steps, slopes, sinks=None, hook=None):
    grouped = sinks is not None
    _, length, w = q.shape
    n_pairs = w // PAIR
    nb = length // BLOCK
    heads_per_group = 2 * n_pairs // N_KV_GROUPS

    def body(*refs):
        if grouped:
            sink_ref, q_ref, kvp_ref, kvc_ref, o_ref, lse_ref, bias_ref = refs
        else:
            q_ref, kp_ref, kc_ref, vp_ref, vc_ref, o_ref, lse_ref, bias_ref = refs
        n = pl.program_id(1)

        @pl.when((pl.program_id(0) == 0) & (n == 0))
        def _():
            _fill_bias(bias_ref, n_pairs, max_steps, dil, slopes, sink_ref if grouped else None)

        first = (n == 0).astype(jnp.int32)
        low = _low_lanes(BLOCK)
        lane = lax.broadcasted_iota(jnp.int32, (BLOCK, 128), 1)
        lse_acc = jnp.zeros((BLOCK, 128), F32)
        if grouped:
            kv_all = jnp.concatenate([kvp_ref[...], kvc_ref[...]], axis=0)
            k_dup = [_without_sink_row(_dup_group(kv_all[:, 0:PAIR], g)) for g in range(N_KV_GROUPS)]
            v_dup = [_without_sink_row(_dup_group(kv_all[:, PAIR:2 * PAIR], g)) for g in range(N_KV_GROUPS)]
        for i in range(n_pairs):
            sl = slice(i * PAIR, (i + 1) * PAIR)
            qs = _stack_heads(q_ref[:, sl] * ATT_SCALE, low)
            if grouped:
                kk, vv = k_dup[2 * i // heads_per_group], v_dup[2 * i // heads_per_group]
            else:
                kk = jnp.concatenate([kp_ref[:, sl], kc_ref[:, sl]], axis=0)
                vv = jnp.concatenate([vp_ref[:, sl], vc_ref[:, sl]], axis=0)
            sc = lax.dot_general(qs, kk, (((1,), (1,)), ((), ())), preferred_element_type=F32)
            sc = sc + bias_ref[first, i]
            m = jnp.max(sc, axis=-1, keepdims=True)
            p = jnp.exp(sc - m)
            den = jnp.sum(p, axis=-1, keepdims=True)
            o = jnp.dot(p.astype(BF16), vv, preferred_element_type=F32) / den
            o_ref[:, sl] = _unstack_heads(o, low).astype(o_ref.dtype)
            lse = m + jnp.log(den)
            lse_acc = jnp.where(lane == 2 * i, lse[0:BLOCK], lse_acc)
            lse_acc = jnp.where(lane == 2 * i + 1, lse[BLOCK:2 * BLOCK], lse_acc)
        lse_ref[...] = lse_acc

    def cur(width):
        return pl.BlockSpec((None, BLOCK, width), lambda r, n: (r, n, 0))

    def prev(width):
        return pl.BlockSpec((None, BLOCK, width), lambda r, n: (r, jnp.maximum(n - 1, 0), 0))

    if grouped:
        kvw = kv.shape[2]
        operands = [sinks, q, kv, kv]
        in_specs = [SMEM_SPEC, cur(w), prev(kvw), cur(kvw)]
    else:
        operands = [q, kv[0], kv[0], kv[1], kv[1]]
        in_specs = [cur(w), prev(w), cur(w), prev(w), cur(w)]
    res, hook_res = _call(
        body, hook, name=name, grid=(dil, nb), in_specs=in_specs,
        out_specs=[cur(w), cur(128)],
        out_shape=[jax.ShapeDtypeStruct((dil, length, w), F32 if grouped else BRANCH_DTYPE),
                   jax.ShapeDtypeStruct((dil, length, 128), F32)],
        scratch_shapes=[_bias_shape(n_pairs)],
        compiler_params=_params(("arbitrary", "arbitrary"), BLOCK * w * 16 + n_pairs * BLOCK * BLOCK * 16),
    )(*operands)
    return res if hook is None else (res, hook_res)


def _attn_bwd(name, q, kv, do, lse, delta, *, dil, max_steps, slopes, sinks=None, hook=None):
    grouped = sinks is not None
    _, length, w = q.shape
    n_pairs = w // PAIR
    nb = length // BLOCK
    heads_per_group = 2 * n_pairs // N_KV_GROUPS
    pairs_per_group = n_pairs // N_KV_GROUPS

    def body(*refs):
        if grouped:
            (sink_ref, q_ref, kvp_ref, kvc_ref, do_ref, lse_ref, dl_ref,
             dq_ref, dkv_ref, dsink_ref, acc_ref, bias_ref) = refs
        else:
            (q_ref, kp_ref, kc_ref, vp_ref, vc_ref, do_ref, lse_ref, dl_ref,
             dq_ref, dk_ref, dv_ref, acck_ref, accv_ref, bias_ref) = refs
        n = pl.program_id(1)

        @pl.when((pl.program_id(0) == 0) & (n == 0))
        def _():
            _fill_bias(bias_ref, n_pairs, max_steps, dil, slopes, sink_ref if grouped else None)

        @pl.when(n == 0)
        def _():
            if grouped:
                acc_ref[...] = jnp.zeros_like(acc_ref)

                @pl.when(pl.program_id(0) == 0)
                def _():
                    dsink_ref[...] = jnp.zeros_like(dsink_ref)
            else:
                acck_ref[...] = jnp.zeros_like(acck_ref)
                accv_ref[...] = jnp.zeros_like(accv_ref)

        @pl.when(n == nb)
        def _():
            if grouped:
                dkv_ref[...] = acc_ref[...]
            else:
                dk_ref[...] = acck_ref[...].astype(dk_ref.dtype)
                dv_ref[...] = accv_ref[...].astype(dv_ref.dtype)

        @pl.when(n < nb)
        def _():
            first = (n == 0).astype(jnp.int32)
            low = _low_lanes(BLOCK)
            low_kv = _low_lanes(2 * BLOCK)
            lane1 = lax.broadcasted_iota(jnp.int32, (1, 128), 1)
            if grouped:
                kv_all = jnp.concatenate([kvp_ref[...], kvc_ref[...]], axis=0)
                k_dup = [_without_sink_row(_dup_group(kv_all[:, 0:PAIR], g)) for g in range(N_KV_GROUPS)]
                v_dup = [_without_sink_row(_dup_group(kv_all[:, PAIR:2 * PAIR], g)) for g in range(N_KV_GROUPS)]
                dk_grp =[jnp.zeros((2 * BLOCK, PAIR), F32) for _ in range(N_KV_GROUPS)]
                dv_grp = [jnp.zeros((2 * BLOCK, PAIR), F32) for _ in range(N_KV_GROUPS)]
                dsink = jnp.zeros((1, 128), F32)
            for i in range(n_pairs):
                sl = slice(i * PAIR, (i + 1) * PAIR)
                qs = _stack_heads(q_ref[:, sl] * ATT_SCALE, low)
                dos = _stack_heads(do_ref[:, sl], low)
                if grouped:
                    grp = 2 * i // heads_per_group
                    kk, vv = k_dup[grp], v_dup[grp]
                else:
                    kk = jnp.concatenate([kp_ref[:, sl], kc_ref[:, sl]], axis=0)
                    vv = jnp.concatenate([vp_ref[:, sl], vc_ref[:, sl]], axis=0)
                lse_col = _head_columns(lse_ref, i)
                dl_col = _head_columns(dl_ref, i)
                sc = lax.dot_general(qs, kk, (((1,), (1,)), ((), ())), preferred_element_type=F32)
                p = jnp.exp(sc + bias_ref[first, i] - lse_col)
                dp = lax.dot_general(dos, vv, (((1,), (1,)), ((), ())), preferred_element_type=F32)
                ds_f32 = p * (dp - dl_col)
                ds = ds_f32.astype(BF16)
                dq = jnp.dot(ds, kk, preferred_element_type=F32)
                dkk = lax.dot_general(ds, qs, (((0,), (0,)), ((), ())), preferred_element_type=F32)
                dvv = lax.dot_general(p.astype(BF16), dos, (((0,), (0,)), ((), ())),
                                      preferred_element_type=F32)
                if grouped:
                    for half in (0, 1):
                        contrib = jnp.sum(ds_f32[half * BLOCK:(half + 1) * BLOCK, 0:1], axis=0, keepdims=True)
                        dsink = jnp.where(lane1 == 2 * i + half, dsink + contrib, dsink)
                dq_ref[:, sl] = (_unstack_heads(dq, low) * ATT_SCALE).astype(dq_ref.dtype)
                if grouped:
                    dk_grp[grp] = dk_grp[grp] + dkk
                    dv_grp[grp] = dv_grp[grp] + dvv
                else:
                    dk_ref[:, sl] = (acck_ref[:, sl] + dkk[0:BLOCK]).astype(dk_ref.dtype)
                    acck_ref[:, sl] = dkk[BLOCK:2 * BLOCK]
                    dv_ref[:, sl] = (accv_ref[:, sl] + dvv[0:BLOCK]).astype(dv_ref.dtype)
                    accv_ref[:, sl] = dvv[BLOCK:2 * BLOCK]
            if grouped:
                folded = [_without_sink_row(t + _swap_halves(t)) for t in dk_grp + dv_grp]
                dk_tile = jnp.where(low_kv, folded[0], folded[1])
                dv_tile = jnp.where(low_kv, folded[2], folded[3])
                part = jnp.concatenate([dk_tile, dv_tile], axis=1)
                dkv_ref[...] = acc_ref[...] + part[0:BLOCK]
                acc_ref[...] = part[BLOCK:2 * BLOCK]
                dsink_ref[...] += dsink

    last = nb - 1

    def cur(width):
        return pl.BlockSpec((None, BLOCK, width), lambda r, n: (r, jnp.minimum(n, last), 0))

    def prev(width):
        return pl.BlockSpec((None, BLOCK, width),
                            lambda r, n: (r, jnp.maximum(jnp.minimum(n, last) - 1, 0), 0))

    def done(width):
        return pl.BlockSpec((None, BLOCK, width), lambda r, n: (r, jnp.maximum(n - 1, 0), 0))

    if grouped:
        assert pairs_per_group * N_KV_GROUPS == n_pairs and heads_per_group % 2 == 0
        kvw = kv.shape[2]
        operands = [sinks, q, kv, kv, do, lse, delta]
        in_specs = [SMEM_SPEC, cur(w), prev(kvw), cur(kvw), cur(w), cur(128), cur(128)]
        out_specs = [cur(w), done(kvw), pl.BlockSpec((1, 128), lambda r, n: (0, 0))]
        out_shape = [jax.ShapeDtypeStruct((dil, length, w), F32), jax.ShapeDtypeStruct((dil, length, kvw), F32),
                     jax.ShapeDtypeStruct((1, 128), F32)]
        scratch = [pltpu.VMEM((BLOCK, kvw), F32), _bias_shape(n_pairs)]
    else:
        operands = [q, kv[0], kv[0], kv[1], kv[1], do, lse, delta]
        in_specs = [cur(w), prev(w), cur(w), prev(w), cur(w), cur(w), cur(128), cur(128)]
        out_specs = [cur(w), done(w), done(w)]
        out_shape = [jax.ShapeDtypeStruct((dil, length, w), BRANCH_DTYPE)] * 3
        scratch = [pltpu.VMEM((BLOCK, w), F32), pltpu.VMEM((BLOCK, w), F32), _bias_shape(n_pairs)]
    res, hook_res = _call(
        body, hook, name=name, grid=(dil, nb + 1), in_specs=in_specs, out_specs=out_specs,
        out_shape=out_shape, scratch_shapes=scratch,
        compiler_params=_params(("arbitrary", "arbitrary"), BLOCK * w * 32 + n_pairs * BLOCK * BLOCK * 16),
    )(*operands)
    return res if hook is None else (res, hook_res)


def _adamw(name, w, g, m, v):
    rows, cols = w.shape
    tm = next((t for t in (256, 128, 64, 32, 16, 8) if rows % t == 0), rows)

    def body(w_ref, g_ref, m_ref, v_ref, d_ref, nm_ref, nv_ref, g_out_ref):
        gv = g_ref[...]
        mn = ADAM_B1 * m_ref[...] + (1.0 - ADAM_B1) * gv
        vn = ADAM_B2 * v_ref[...] + (1.0 - ADAM_B2) * (gv * gv)
        m_hat = mn / (1.0 - ADAM_B1 ** ADAM_STEP)
        v_hat = vn / (1.0 - ADAM_B2 ** ADAM_STEP)
        d_ref[...] = -ADAM_LR * (m_hat / (jnp.sqrt(v_hat) + ADAM_EPS) + ADAM_WD * w_ref[...])
        nm_ref[...] = mn
        nv_ref[...] = vn
        g_out_ref[...] = gv

    spec = pl.BlockSpec((tm, cols), lambda i: (i, 0))
    return _call(
        body, None, name=name, grid=(rows // tm,), in_specs=[spec] * 4, out_specs=[spec] * 4,
        out_shape=[jax.ShapeDtypeStruct(w.shape, F32)] * 4,
        compiler_params=_params(("parallel",), tm * cols * 32),
    )(w, g, m, v)[0]


def _mesh_position():
    return lax.axis_index("x"), lax.axis_index("y"), lax.axis_index("c")


def _other_chips(x, y):
    return [(1 - x, y), (x, 1 - y), (1 - x, 1 - y)]


def _gather_hook(gathered, lo, hi, mid_fraction=0.6):
    rows, cols = gathered.shape[0] // N_CHIPS, gathered.shape[1]
    half, n = rows // 2, hi - lo
    assert lo % 16 == 0 and n % 16 == 0 and half % 16 == 0

    def region(out, owner_chip, which_half):
        return out.at[pl.ds(pl.multiple_of(owner_chip * rows + which_half * half + lo, 16), n)]

    def parts(outs, sems):
        x, y, c = _mesh_position()
        return outs[0], sems, c, 2 * x + y, (x, y, 1 - c), _other_chips(x, y)

    def start(ops, outs, sems):
        out, (send, recv, fsend, frecv), c, chip, sibling, others = parts(outs, sems)
        mine = region(out, chip, c)
        for k, (px, py) in enumerate(others):
            _remote(mine, mine, send.at[k], recv.at[k], (px, py, c)).start()

    def mid(ops, outs, sems):
        out, (send, recv, fsend, frecv), c, chip, sibling, others = parts(outs, sems)
        for k, (px, py) in enumerate(others):
            landed = region(out, 2 * px + py, c)
            _remote(landed, landed, send.at[k], recv.at[k], (px, py, c)).wait_recv()
            _remote(landed, landed, fsend.at[k], frecv.at[k], sibling).start()

    def finish(ops, outs, sems):
        out, (send, recv, fsend, frecv), c, chip, sibling, others = parts(outs, sems)
        mine = region(out, chip, c)
        for k, (px, py) in enumerate(others):
            passed = region(out, 2 * px + py, 1 - c)
            _remote(passed, passed, fsend.at[k], frecv.at[k], sibling).wait_recv()
        for k, (px, py) in enumerate(others):
            landed = region(out, 2 * px + py, c)
            _remote(landed, landed, fsend.at[k], frecv.at[k], sibling).wait_send()
            _remote(mine, mine, send.at[k], recv.at[k], (px, py, c)).wait_send()

    return _Hook([gathered], [jax.ShapeDtypeStruct(gathered.shape, gathered.dtype)],
                 [pltpu.SemaphoreType.DMA((3,))] * 4, start, finish, mid, aliases={0: 0},
                 mid_fraction=mid_fraction)


def _own_shard_in_place(name, shard, chip):
    rows, cols = shard.shape
    tr = next(t for t in (544, 512, 320, 256, 128, 64, 32, 16) if rows % t == 0)

    def body(chip_ref, w_ref, o_ref):
        o_ref[...] = w_ref[...].astype(BF16)

    return pl.pallas_call(
        body, name=name,
        grid_spec=pltpu.PrefetchScalarGridSpec(
            num_scalar_prefetch=1, grid=(rows // tr,),
            in_specs=[pl.BlockSpec((tr, cols), lambda i, chip_ref: (i, 0))],
            out_specs=pl.BlockSpec((tr, cols), lambda i, chip_ref: (chip_ref[0] * (rows // tr) + i, 0))),
        out_shape=jax.ShapeDtypeStruct((N_CHIPS * rows, cols), BF16),
        compiler_params=_params(("parallel",), tr * cols * 6),
    )(chip, shard)


def _exchange_hook(grad):
    rows, cols = grad.shape[0] // N_CHIPS, grad.shape[1]
    half = rows // 2
    assert half % 16 == 0

    def copies(ops, outs, sems):
        x, y, c = _mesh_position()
        send, recv = sems
        return [_remote(ops[0].at[pl.ds(pl.multiple_of(k * rows + (1 - c) * half, 16), half)], outs[0].at[k],
                        send.at[k], recv.at[k], (x, y, 1 - c)) for k in range(N_CHIPS)]

    def start(ops, outs, sems):
        for cp in copies(ops, outs, sems):
            cp.start()

    def finish(ops, outs, sems):
        for cp in copies(ops, outs, sems):
            cp.wait_recv()
            cp.wait_send()

    return _Hook([grad], [jax.ShapeDtypeStruct((N_CHIPS, half, cols), grad.dtype)],
                 [pltpu.SemaphoreType.DMA((N_CHIPS,))] * 2, start, finish)


def _scatter_hook(chip_sum):
    _, half, cols = chip_sum.shape

    def copies(ops, outs, sems):
        x, y, c = _mesh_position()
        send, recv = sems
        return [_remote(ops[0].at[2 * px + py], outs[0].at[k], send.at[k], recv.at[k], (px, py, c))
                for k, (px, py) in enumerate(_other_chips(x, y))]

    def start(ops, outs, sems):
        for cp in copies(ops, outs, sems):
            cp.start()

    def finish(ops, outs, sems):
        for cp in copies(ops, outs, sems):
            cp.wait_recv()
            cp.wait_send()

    return _Hook([chip_sum], [jax.ShapeDtypeStruct((3, half, cols), chip_sum.dtype)],
                 [pltpu.SemaphoreType.DMA((3,))] * 2, start, finish)


def _sum_tile(half):
    return 256 if half % 256 == 0 else half


def _chip_add(name, grad, from_sibling, core):
    n_chips, half, cols = from_sibling.shape
    rows = 2 * half
    tr = _sum_tile(half)

    def body(core_ref, g_ref, s_ref, o_ref):
        o_ref[...] = (g_ref[...].astype(F32) + s_ref[...].astype(F32)).astype(o_ref.dtype)

    tile = pl.BlockSpec((None, tr, cols), lambda k, i, core_ref: (k, i, 0))
    return pl.pallas_call(
        body, name=name,
        grid_spec=pltpu.PrefetchScalarGridSpec(
            num_scalar_prefetch=1, grid=(n_chips, half // tr),
            in_specs=[pl.BlockSpec((tr, cols), lambda k, i, core_ref:
                                   (k * (rows // tr) + core_ref[0] * (half // tr) + i, 0)), tile],
            out_specs=tile),
        out_shape=jax.ShapeDtypeStruct(from_sibling.shape, from_sibling.dtype),
        compiler_params=_params(("parallel", "parallel"), 3 * tr * cols * 4),
    )(core, grad, from_sibling)


def _final_add(name, chip_sum, from_chips, chip, core):
    _, half, cols = chip_sum.shape
    tr = _sum_tile(half)

    def body(chip_ref, core_ref, own_ref, others_ref, o_ref):
        total = own_ref[...].astype(F32)
        for k in range(3):
            total = total + others_ref[k].astype(F32)
        o_ref[...] = total

    return pl.pallas_call(
        body, name=name,
        grid_spec=pltpu.PrefetchScalarGridSpec(
            num_scalar_prefetch=2, grid=(half // tr,),
            in_specs=[pl.BlockSpec((None, tr, cols), lambda i, chip_ref, core_ref: (chip_ref[0], i, 0)),
                      pl.BlockSpec((3, tr, cols), lambda i, chip_ref, core_ref: (0, i, 0))],
            out_specs=pl.BlockSpec((tr, cols),
                                   lambda i, chip_ref, core_ref: (core_ref[0] * (half // tr) + i, 0))),
        out_shape=jax.ShapeDtypeStruct((2 * half, cols), F32),
        compiler_params=_params(("parallel",), 6 * tr * cols * 4),
    )(chip, core, chip_sum, from_chips)


def _share_halves(shards, small):
    n_s = len(shards)
    rows_s = small.shape[0]

    def body(*refs):
        small_ref = refs[n_s]
        outs, small_out = refs[n_s + 1:2 * n_s + 1], refs[2 * n_s + 1]
        small_all, send, recv, small_send, small_recv = refs[2 * n_s + 2:]
        x, y, c = _mesh_position()
        me = 4 * x + 2 * y + c
        sibling = (x, y, 1 - c)
        pending = []
        for i in range(n_s):
            half = shards[i].shape[0] // 2
            mine = outs[i].at[pl.ds(pl.multiple_of(c * half, 16), half)]
            cp = _remote(mine, mine, send.at[i], recv.at[i], sibling)
            cp.start()
            pending.append(cp)
        small_all[me] = small_ref[...]
        for j in range(N_DEV - 1):
            peer = (me + 1 + j) % N_DEV
            cp = _remote(small_all.at[me], small_all.at[me], small_send.at[j], small_recv.at[j],
                         (peer // 4, (peer // 2) % 2, peer % 2))
            cp.start()
            pending.append(cp)
        for i in range(n_s):
            half = shards[i].shape[0] // 2
            theirs = outs[i].at[pl.ds(pl.multiple_of((1 - c) * half, 16), half)]
            _remote(theirs, theirs, send.at[i], recv.at[i], sibling).wait_recv()
        for j in range(N_DEV - 1):
            peer = (me + N_DEV - 1 - j) % N_DEV
            _remote(small_all.at[peer], small_all.at[peer], small_send.at[j], small_recv.at[j],
                    sibling).wait_recv()
        total = small_all[0]
        for dev in range(1, N_DEV):
            total = total + small_all[dev]
        small_out[...] = total
        for cp in pending:
            cp.wait_send()

    res = pl.pallas_call(
        body, name="share_halves",
        in_specs=[HBM_SPEC] * n_s + [VMEM_SPEC], out_specs=[HBM_SPEC] * n_s + [VMEM_SPEC],
        out_shape=[jax.ShapeDtypeStruct(sh.shape, sh.dtype) for sh in shards]
        + [jax.ShapeDtypeStruct((rows_s, 128), F32)],
        scratch_shapes=[pltpu.VMEM((N_DEV, rows_s, 128), F32),
                        pltpu.SemaphoreType.DMA((n_s,)), pltpu.SemaphoreType.DMA((n_s,)),
                        pltpu.SemaphoreType.DMA((N_DEV - 1,)), pltpu.SemaphoreType.DMA((N_DEV - 1,))],
        input_output_aliases={i: i for i in range(n_s)},
    )(*shards, small)
    return res[:n_s], res[n_s]


def _pack_small(parts, rows):
    flat = jnp.concatenate([p.reshape(-1) for p in parts])
    flat = jnp.pad(flat, (0, rows * 128 - flat.shape[0]))
    return flat.reshape(rows, 128)


def _unpack_small(packed, shapes):
    flat = packed.reshape(-1)
    out, off = [], 0
    for shp in shapes:
        n = int(np.prod(shp))
        out.append(flat[off:off + n].reshape(shp))
        off += n
    return out


def kernel(x, g_attn, w_in, b_in, sinks_a, g_out_a, g_out_b, w_out, g_mlp, w_1, w_2, g_final, loss_target, m_g_attn, m_w_in, m_b_in, m_sinks_a, m_g_out_a, m_g_out_b, m_w_out, m_g_mlp, m_w_1, m_w_2, m_g_final, v_g_attn, v_w_in, v_b_in, v_sinks_a, v_g_out_a, v_g_out_b, v_w_out, v_g_mlp, v_w_1, v_w_2, v_g_final):
    s, d = x.shape[1], x.shape[2]
    d_in = b_in.shape[1]
    qa = g_out_a.shape[1]
    qb = g_out_b.shape[1]
    kva = 2 * N_KV_GROUPS * HEAD_DIM
    assert d_in == qa + kva + 3 * qb and qa + qb == w_out.shape[1] * N_CHIPS
    d_ff = w_1.shape[2] * N_CHIPS
    ff_shard = w_1.shape[2]
    n_heads_a, n_heads_b = qa // HEAD_DIM, qb // HEAD_DIM
    slopes_a, slopes_b = alibi_slopes(n_heads_a), alibi_slopes(n_heads_b)

    x2d = x[0]
    target = loss_target[0]

    core_index = lax.axis_index("c").astype(jnp.int32).reshape(1)
    chip_index = (2 * lax.axis_index("x") + lax.axis_index("y")).astype(jnp.int32).reshape(1)
    shards = {"w_in": w_in[0].T, "w_out": w_out[0], "w_1": w_1[0], "w_2": w_2[0]}
    halves = [sh.shape[0] // 2 for sh in shards.values()]
    w_in_t, w_out_g, w_1_g, w_2_g = [_own_shard_in_place(f"place_{n}", sh, chip_index)
                                     for n, sh in shards.items()]

    tm = _tile(s, 1024)

    (h1, r1), (w_in_t,) = _norm_fwd("norm_attn", x2d, g_attn,
                                    hook=_gather_hook(w_in_t, 0, halves[0], mid_fraction=1.0))

    q_a, = _project_by_class("proj_qa", h1, w_in_t, b_in, 0, qa, (1,))
    kv_a, = _project_by_class("proj_kva", h1, w_in_t, b_in, qa, kva, (1,))
    q_bs, (w_out_g,) = _project_by_class("proj_qb", h1, w_in_t, b_in, qa + kva, qb, DILATIONS,
                                         hook=_gather_hook(w_out_g, 0, halves[1]))
    k_bs = _project_by_class("proj_kb", h1, w_in_t, b_in, qa + kva + qb, qb, DILATIONS)
    v_bs = _project_by_class("proj_vb", h1, w_in_t, b_in, qa + kva + 2 * qb, qb, DILATIONS)

    quarter = halves[2] // 4
    sinks = sinks_a.reshape(-1)
    (o_a, lse_a), (w_1_g,) = _attn_fwd("attn_a_fwd", q_a, kv_a, dil=1, max_steps=WINDOW_A - 1, slopes=slopes_a,
                                       sinks=sinks, hook=_gather_hook(w_1_g, 0, quarter))
    o_a = o_a[0]
    o_bs, lse_bs = [], []
    for n, (window, dil) in enumerate(DILATED_BRANCHES):
        (o, l), (w_1_g,) = _attn_fwd(f"attn_b{dil}_fwd", q_bs[n], (k_bs[n], v_bs[n]), dil=dil,
                                     max_steps=window // dil, slopes=slopes_b,
                                     hook=_gather_hook(w_1_g, (n + 1) * quarter, (n + 2) * quarter))
        o_bs.append(o)
        lse_bs.append(l)
    w_1_g = w_1_g.reshape(N_CHIPS, d, ff_shard)
    mix, o_b, *lse_tot, r_a, r_b = _mix_fwd(o_a, o_bs, lse_bs, g_out_a, g_out_b)

    tn = _tile(d, 1024)
    a_spec, b_spec = _mm_specs("nn", tm, tn, d)
    tile_mn = pl.BlockSpec((tm, tn), lambda i, j, k: (i, j))
    x2 = _matmul("out_proj", mix, w_out_g, [x2d], mode="nn", grid=(s // tm, d // tn, 1),
                 a_spec=a_spec, b_spec=b_spec, extra_specs=[tile_mn],
                 out_shapes=[jax.ShapeDtypeStruct((s, d), F32)], out_specs=[tile_mn],
                 epilogue=lambda acc, res: (acc + res,))[0]

    h2, r2 = _norm_fwd("norm_mlp", x2, g_mlp)

    tn = _tile(ff_shard, 1024)
    per = ff_shard // tn
    a_spec, _ = _mm_specs("nn", tm, tn, d)
    tile_mn = pl.BlockSpec((tm, tn), lambda i, j, k: (i, j))
    (u,), (w_2_g,) = _matmul(
        "mlp_up", h2, w_1_g, [], mode="nn", grid=(s // tm, d_ff // tn, 1),
        a_spec=a_spec, b_spec=pl.BlockSpec((None, d, tn), lambda i, j, k: (j // per, 0, j % per)),
        extra_specs=[], out_shapes=[jax.ShapeDtypeStruct((s, d_ff), BF16)], out_specs=[tile_mn],
        epilogue=lambda acc: (jnp.maximum(acc, 0.0),),
        hook=_gather_hook(w_2_g, 0, halves[3]))

    tn = _tile(d, 1024)
    tk = _tile(d_ff, 2048)
    a_spec, b_spec = _mm_specs("nn", tm, tn, tk)
    tile_mn = pl.BlockSpec((tm, tn), lambda i, j, k: (i, j))
    x3 = _matmul("mlp_down", u, w_2_g, [x2], mode="nn", grid=(s // tm, d // tn, d_ff // tk),
                 a_spec=a_spec, b_spec=b_spec, extra_specs=[tile_mn],
                 out_shapes=[jax.ShapeDtypeStruct((s, d), F32)], out_specs=[tile_mn],
                 prologue=lambda a: a * a, epilogue=lambda acc, res: (acc + res,), acc_shape=(tm, tn))[0]

    dx3, dx3b, loss_part, dg_final = _loss_head(x3, target, g_final.reshape(1, d))

    tn = _tile(d_ff, 1024)
    a_spec, b_spec = _mm_specs("nt", tm, tn, d)
    tile_mn = pl.BlockSpec((tm, tn), lambda i, j, k: (i, j))
    dpre = _matmul("mlp_down_dx", dx3b, w_2_g, [u], mode="nt", grid=(s // tm, d_ff // tn, 1),
                   a_spec=a_spec, b_spec=b_spec, extra_specs=[tile_mn],
                   out_shapes=[jax.ShapeDtypeStruct((s, d_ff), BF16)], out_specs=[tile_mn],
                   epilogue=lambda acc, uu: (acc * (2.0 * uu.astype(F32)),))[0]

    wire = GRAD_WIRE_DTYPE
    tk_s = _tile(s, 2048)
    tmw = _tile(d_ff, 1024)
    a_spec, b_spec = _mm_specs("tn", tmw, d, tk_s)
    dw_2 = _matmul("mlp_down_dw", u, dx3b, [], mode="tn", grid=(d_ff // tmw, 1, s // tk_s),
                   a_spec=a_spec, b_spec=b_spec, extra_specs=[],
                   out_shapes=[jax.ShapeDtypeStruct((d_ff, d), wire)],
                   out_specs=[pl.BlockSpec((tmw, d), lambda i, j, k: (i, j))],
                   prologue=lambda a: a * a, epilogue=lambda acc: (acc,), acc_shape=(tmw, d))[0]

    tn = _tile(d, 1024)
    tk = _tile(ff_shard, 2048)
    per = ff_shard // tk
    a_spec, _ = _mm_specs("nt", tm, tn, tk)
    tile_mn = pl.BlockSpec((tm, tn), lambda i, j, k: (i, j))
    (dh2,), (sib_2,) = _matmul(
        "mlp_up_dx", dpre, w_1_g, [], mode="nt", grid=(s // tm, d // tn, d_ff // tk),
        a_spec=a_spec, b_spec=pl.BlockSpec((None, tn, tk), lambda i, j, k: (k // per, j, k % per)),
        extra_specs=[], out_shapes=[jax.ShapeDtypeStruct((s, d), F32)], out_specs=[tile_mn],
        epilogue=lambda acc: (acc,), acc_shape=(tm, tn), hook=_exchange_hook(dw_2))
    chip_sum_2 = _chip_add("chip_add_w_2", dw_2, sib_2, core_index)

    tmw = _tile(d, 1024)
    tnw = _tile(ff_shard, 2048)
    per = ff_shard // tnw
    a_spec, b_spec = _mm_specs("tn", tmw, tnw, tk_s)
    dw_1 = _matmul("mlp_up_dw", h2, dpre, [], mode="tn", grid=(d // tmw, d_ff // tnw, s // tk_s),
                   a_spec=a_spec, b_spec=b_spec, extra_specs=[],
                   out_shapes=[jax.ShapeDtypeStruct((N_CHIPS, d, ff_shard), wire)],
                   out_specs=[pl.BlockSpec((None, tmw, tnw), lambda i, j, k: (j // per, i, j % per))],
                   epilogue=lambda acc: (acc,), acc_shape=(tmw, tnw))[0]

    dw_1 = dw_1.reshape(N_CHIPS * d, ff_shard)
    (dx2, dx2b, dg_mlp), (sib_1,) = _norm_bwd("norm_mlp_bwd", dh2, x2, r2, g_mlp, dx3,
                                              hook=_exchange_hook(dw_1))
    chip_sum_1 = _chip_add("chip_add_w_1", dw_1, sib_1, core_index)

    tn = _tile(d, 1024)
    a_spec, b_spec = _mm_specs("nt", tm, tn, d)
    tile_mn = pl.BlockSpec((tm, tn), lambda i, j, k: (i, j))
    dmix = _matmul("out_proj_dx", dx2b, w_out_g, [], mode="nt", grid=(s // tm, d // tn, 1),
                   a_spec=a_spec, b_spec=b_spec, extra_specs=[],
                   out_shapes=[jax.ShapeDtypeStruct((s, d), F32)], out_specs=[tile_mn],
                   epilogue=lambda acc: (acc,))[0]

    tmw = _tile(d, 1024)
    a_spec, b_spec = _mm_specs("tn", tmw, d, tk_s)
    dw_out = _matmul("out_proj_dw", mix, dx2b, [], mode="tn", grid=(d // tmw, 1, s // tk_s),
                     a_spec=a_spec, b_spec=b_spec, extra_specs=[],
                     out_shapes=[jax.ShapeDtypeStruct((d, d), wire)],
                     out_specs=[pl.BlockSpec((tmw, d), lambda i, j, k: (i, j))],
                     epilogue=lambda acc: (acc,), acc_shape=(tmw, d))[0]

    mix_grads, (sib_out,) = _mix_bwd(dmix, o_a, o_b, r_a, r_b, g_out_a, g_out_b, hook=_exchange_hook(dw_out))
    do_a, do_bs, delta_a, delta_bs = mix_grads[0], mix_grads[1:4], mix_grads[4], mix_grads[5:8]
    dg_out_a, dg_out_b = mix_grads[8:]
    chip_sum_out = _chip_add("chip_add_w_out", dw_out, sib_out, core_index)

    (dq_a, dkv_a, dsinks), (chips_2,) = _attn_bwd(
        "attn_a_bwd", q_a, kv_a, do_a[None], lse_a, delta_a[None], dil=1, max_steps=WINDOW_A - 1,
        slopes=slopes_a, sinks=sinks, hook=_scatter_hook(chip_sum_2))
    dqs, dks, dvs = [], [], []
    scatter = {1: chip_sum_1, 4: chip_sum_out}
    arrived = {}
    for n, (window, dil) in enumerate(DILATED_BRANCHES):
        res = _attn_bwd(f"attn_b{dil}_bwd", q_bs[n], (k_bs[n], v_bs[n]), do_bs[n], lse_tot[n],
                        delta_bs[n], dil=dil, max_steps=window // dil, slopes=slopes_b,
                        hook=_scatter_hook(scatter[dil]) if dil in scatter else None)
        if dil in scatter:
            res, (arrived[dil],) = res
        dq, dk, dv = res
        dqs.append(dq)
        dks.append(dk)
        dvs.append(dv)
    half_2 = _final_add("final_add_w_2", chip_sum_2, chips_2, chip_index, core_index)
    half_1 = _final_add("final_add_w_1", chip_sum_1, arrived[1], chip_index, core_index)
    half_out = _final_add("final_add_w_out", chip_sum_out, arrived[4], chip_index, core_index)
    dproj, db_in = _assemble_dproj(dq_a[0], dkv_a[0], dqs, dks, dvs)

    tmw = d_in // 2 if (d_in // 2) % 128 == 0 else d_in
    tnw = _tile(d, 1024)
    tk_s = _tile(s, 1024)
    a_spec, b_spec = _mm_specs("tn", tmw, tnw, tk_s)
    dw_in_t = _matmul("in_proj_dw", dproj, h1, [], mode="tn", grid=(d_in // tmw, d // tnw, s // tk_s),
                      a_spec=a_spec, b_spec=b_spec, extra_specs=[],
                      out_shapes=[jax.ShapeDtypeStruct((d_in, d), wire)],
                      out_specs=[pl.BlockSpec((tmw, tnw), lambda i, j, k: (i, j))],
                      epilogue=lambda acc: (acc,), acc_shape=(tmw, tnw))[0]

    tn = _tile(d, 1024)
    a_spec, b_spec = _mm_specs("nn", tm, tn, d_in)
    tile_mn = pl.BlockSpec((tm, tn), lambda i, j, k: (i, j))
    (dh1,), (sib_in,) = _matmul("in_proj_dx", dproj, w_in_t, [], mode="nn", grid=(s // tm, d // tn, 1),
                                a_spec=a_spec, b_spec=b_spec, extra_specs=[],
                                out_shapes=[jax.ShapeDtypeStruct((s, d), F32)], out_specs=[tile_mn],
                                epilogue=lambda acc: (acc,), hook=_exchange_hook(dw_in_t))
    chip_sum_in = _chip_add("chip_add_w_in", dw_in_t, sib_in, core_index)

    (grad_x, _, dg_attn), (chips_in,) = _norm_bwd("norm_attn_bwd", dh1, x2d, r1, g_attn, dx2,
                                                  hook=_scatter_hook(chip_sum_in))
    half_in = _final_add("final_add_w_in", chip_sum_in, chips_in, chip_index, core_index)

    small_parts = [dg_attn, db_in, dsinks[:, :n_heads_a], dg_out_a, dg_out_b, dg_mlp, dg_final]
    small_shapes = [g_attn.shape, b_in.shape, sinks_a.shape, g_out_a.shape, g_out_b.shape, g_mlp.shape,
                    g_final.shape]
    n_small = sum(int(np.prod(shp)) for shp in small_shapes)
    rows_s = -(-n_small // (8 * 128)) * 8
    (gw_in_t, gw_out, gw_1, gw_2), small_sum = _share_halves(
        [half_in, half_out, half_1, half_2], _pack_small(small_parts, rows_s))

    upd_in = [t.T for t in _adamw("adamw_w_in", w_in[0].T, gw_in_t, m_w_in[0].T, v_w_in[0].T)]
    upd_out = _adamw("adamw_w_out", w_out[0], gw_out, m_w_out[0], v_w_out[0])
    upd_1 = _adamw("adamw_w_1", w_1[0], gw_1, m_w_1[0], v_w_1[0])
    upd_2 = _adamw("adamw_w_2", w_2[0], gw_2, m_w_2[0], v_w_2[0])
    small_w = [g_attn, b_in, sinks_a, g_out_a, g_out_b, g_mlp, g_final]
    small_m = [m_g_attn, m_b_in, m_sinks_a, m_g_out_a, m_g_out_b, m_g_mlp, m_g_final]
    small_v = [v_g_attn, v_b_in, v_sinks_a, v_g_out_a, v_g_out_b, v_g_mlp, v_g_final]
    upd_small = _adamw("adamw_small", _pack_small(small_w, rows_s), small_sum,
                       _pack_small(small_m, rows_s), _pack_small(small_v, rows_s))
    d_small, m_small, v_small, g_small = [_unpack_small(t, small_shapes) for t in upd_small]

    loss = lax.psum(loss_part[0, 0], ("x", "y", "c"))

    def ordered(small, big):
        w_in_v, w_out_v, w_1_v, w_2_v = big
        return [small[0], w_in_v[None], small[1], small[2], small[3], small[4], w_out_v[None], small[5],
                w_1_v[None], w_2_v[None], small[6]]

    grads = ordered(g_small, (upd_in[3], upd_out[3], upd_1[3], upd_2[3]))
    deltas = ordered(d_small, (upd_in[0], upd_out[0], upd_1[0], upd_2[0]))
    new_m = ordered(m_small, (upd_in[1], upd_out[1], upd_1[1], upd_2[1]))
    new_v = ordered(v_small, (upd_in[2], upd_out[2], upd_1[2], upd_2[2]))
    return (loss, grad_x[None], *grads, *deltas, *new_m, *new_v)
```

```python
import jax
import jax.numpy as jnp
import numpy as np
from jax import lax
from jax.experimental import pallas as pl
from jax.experimental.pallas import tpu as pltpu

F32 = jnp.float32
BF16 = jnp.bfloat16

HEAD_DIM = 64
BLOCK = 128
PAIR = 2 * HEAD_DIM
N_KV_GROUPS = 2
WINDOW_A = 128
DILATED_BRANCHES = ((128, 1), (512, 4), (2048, 16))
EPS = 1e-5
NEG_INF = -1e30
ATT_SCALE = HEAD_DIM ** -0.5

ADAM_LR = 0.001
ADAM_B1 = 0.9
ADAM_B2 = 0.999
ADAM_EPS = 1e-08
ADAM_WD = 0.01
ADAM_STEP = 10

N_CHIPS = 4
N_DEV = 8
MESH = pl.DeviceIdType.MESH
GRAD_WIRE_DTYPE = jnp.bfloat16
BRANCH_DTYPE = jnp.bfloat16

VMEM_CAPACITY_V7X = 64 * 1024 * 1024
VMEM_LIMIT_MAX = VMEM_CAPACITY_V7X - 8 * 1024 * 1024
VMEM_LIMIT_MIN = VMEM_CAPACITY_V7X - 16 * 1024 * 1024

HBM_SPEC = pl.BlockSpec(memory_space=pltpu.HBM)
VMEM_SPEC = pl.BlockSpec(memory_space=pltpu.VMEM)
SMEM_SPEC = pl.BlockSpec(memory_space=pltpu.SMEM)


def _nbytes(shape, dtype):
    return int(np.prod([s for s in shape if s is not None])) * jnp.dtype(dtype).itemsize


def _params(semantics, block_bytes):
    limit = min(max(2 * block_bytes + (4 << 20), VMEM_LIMIT_MIN), VMEM_LIMIT_MAX)
    return pltpu.CompilerParams(dimension_semantics=semantics, vmem_limit_bytes=limit)


class _Hook:
    def __init__(self, operands, out_shape, sems, start, finish, mid=None, aliases=None,
                 mid_fraction=0.6):
        self.operands, self.out_shape, self.sems = list(operands), list(out_shape), list(sems)
        self.start, self.mid, self.finish = start, mid, finish
        self.aliases = dict(aliases or {})
        self.mid_fraction = mid_fraction


def _call(body, hook, *, name, grid, in_specs, out_specs, out_shape, scratch_shapes=(), compiler_params):
    in_specs, out_specs, out_shape = list(in_specs), list(out_specs), list(out_shape)
    scratch_shapes = list(scratch_shapes)
    if hook is None:
        call = pl.pallas_call(body, name=name, grid=grid, in_specs=in_specs, out_specs=out_specs,
                              out_shape=out_shape, scratch_shapes=scratch_shapes,
                              compiler_params=compiler_params)
        return lambda *operands: (call(*operands), [])
    n_in, n_hin, n_out, n_hout, n_scr = (len(in_specs), len(hook.operands), len(out_specs),
                                         len(hook.out_shape), len(scratch_shapes))
    total = int(np.prod(grid))
    t_mid = min(int(total * hook.mid_fraction), total - 1)

    def wrapped(*refs):
        ins, h_in = refs[:n_in], refs[n_in:n_in + n_hin]
        o0 = n_in + n_hin
        outs, h_out = refs[o0:o0 + n_out], refs[o0 + n_out:o0 + n_out + n_hout]
        s0 = o0 + n_out + n_hout
        scr, h_sems = refs[s0:s0 + n_scr], refs[s0 + n_scr:]
        t = pl.program_id(0)
        for axis in range(1, len(grid)):
            t = t * grid[axis] + pl.program_id(axis)

        @pl.when(t == 0)
        def _():
            hook.start(h_in, h_out, h_sems)

        body(*ins, *outs, *scr)
        if hook.mid is not None:
            @pl.when(t == t_mid)
            def _():
                hook.mid(h_in, h_out, h_sems)

        @pl.when(t == total - 1)
        def _():
            hook.finish(h_in, h_out, h_sems)

    params = pltpu.CompilerParams(dimension_semantics=("arbitrary",) * len(grid),
                                  vmem_limit_bytes=compiler_params.vmem_limit_bytes)
    call = pl.pallas_call(
        wrapped, name=name, grid=grid,
        in_specs=in_specs + [HBM_SPEC] * n_hin, out_specs=out_specs + [HBM_SPEC] * n_hout,
        out_shape=out_shape + hook.out_shape, scratch_shapes=scratch_shapes + hook.sems,
        input_output_aliases={n_in + a: n_out + b for a, b in hook.aliases.items()},
        compiler_params=params)

    def run(*operands):
        res = call(*operands, *hook.operands)
        return res[:n_out], res[n_out:]

    return run


def _remote(src, dst, send_sem, recv_sem, device):
    return pltpu.make_async_remote_copy(src_ref=src, dst_ref=dst, send_sem=send_sem, recv_sem=recv_sem,
                                        device_id=device, device_id_type=MESH)


def alibi_slopes(n):
    return [float(v) for v in np.asarray(2.0 ** (-8.0 * (np.arange(n) + 1) / n), dtype=np.float32)]


def _matmul(name, a, b, extras, *, mode, grid, a_spec, b_spec, extra_specs, out_shapes, out_specs,
            epilogue, prologue=None, acc_shape=None, hook=None):
    dims = {"nn": ((1,), (0,)), "nt": ((1,), (1,)), "tn": ((0,), (0,))}[mode]
    nk = grid[2]
    n_ex, n_out = len(extras), len(out_shapes)

    def body(a_ref, b_ref, *rest):
        ex, outs = rest[:n_ex], rest[n_ex:n_ex + n_out]
        av = a_ref[...]
        if prologue is not None:
            av = prologue(av)
        part = lax.dot_general(av, b_ref[...], (dims, ((), ())), preferred_element_type=F32)

        def finish(acc):
            res = epilogue(acc, *[e[...] for e in ex])
            for o, r in zip(outs, res):
                o[...] = r.astype(o.dtype)

        if nk == 1:
            finish(part)
        else:
            acc_ref = rest[-1]
            k = pl.program_id(2)

            @pl.when(k == 0)
            def _():
                acc_ref[...] = part

            @pl.when(k > 0)
            def _():
                acc_ref[...] += part

            @pl.when(k == nk - 1)
            def _():
                finish(acc_ref[...])

    blocks = [(a_spec.block_shape, a.dtype), (b_spec.block_shape, b.dtype)]
    blocks += [(s.block_shape, e.dtype) for s, e in zip(extra_specs, extras)]
    blocks += [(s.block_shape, o.dtype) for s, o in zip(out_specs, out_shapes)]
    nbytes = sum(_nbytes(s, d) for s, d in blocks)
    scratch = []
    if nk > 1:
        scratch.append(pltpu.VMEM(acc_shape, F32))
        nbytes += _nbytes(acc_shape, F32)
    res, hook_res = _call(
        body, hook, name=name, grid=grid,
        in_specs=[a_spec, b_spec, *extra_specs], out_specs=list(out_specs), out_shape=list(out_shapes),
        scratch_shapes=scratch,
        compiler_params=_params(("parallel", "parallel", "arbitrary"), nbytes),
    )(a, b, *extras)
    return res if hook is None else (res, hook_res)


def _mm_specs(mode, tm, tn, tk, b_block=None, b_map=None):
    if mode == "tn":
        a_spec = pl.BlockSpec((tk, tm), lambda i, j, k: (k, i))
    else:
        a_spec = pl.BlockSpec((tm, tk), lambda i, j, k: (i, k))
    if b_block is not None:
        b_spec = pl.BlockSpec(b_block, b_map)
    elif mode == "nt":
        b_spec = pl.BlockSpec((tn, tk), lambda i, j, k: (j, k))
    else:
        b_spec = pl.BlockSpec((tk, tn), lambda i, j, k: (k, j))
    return a_spec, b_spec


def _project_by_class(name, h, w_t, bias, row_off, width, dilations, hook=None):
    s, d = h.shape
    tm = _tile(s, 2048)
    tn = 512 if width % 512 == 0 and row_off % 512 == 0 else _tile(width, 256)
    off = row_off // tn
    assert row_off % tn == 0 and tn % 128 == 0
    n_out = len(dilations)

    def body(h_ref, w_ref, b_ref, *rest):
        outs, perm_ref = rest[:n_out], rest[n_out]
        acc = lax.dot_general(h_ref[...], w_ref[...], (((1,), (1,)), ((), ())), preferred_element_type=F32)
        acc = acc + b_ref[...]
        for j in range(tn // 128):
            cols = slice(j * 128, (j + 1) * 128)
            _to_classes(outs, cols, acc[:, cols], perm_ref, dilations)

    blocks = tm * d * 2 + tn * d * 2 + 3 * tm * tn * 2 + tm * 128 * 4
    res, hook_res = _call(
        body, hook, name=name, grid=(s // tm, width // tn),
        in_specs=[pl.BlockSpec((tm, d), lambda i, j: (i, 0)), pl.BlockSpec((tn, d), lambda i, j: (j + off, 0)),
                  pl.BlockSpec((1, tn), lambda i, j: (0, j + off))],
        out_specs=[pl.BlockSpec((dil, tm // dil, tn), lambda i, j: (0, i, j)) for dil in dilations],
        out_shape=[_class_shape(dil, s, width, BF16) for dil in dilations],
        scratch_shapes=[pltpu.VMEM((tm, 128), F32)],
        compiler_params=_params(("parallel", "parallel"), blocks),
    )(h, w_t, bias)
    return res if hook is None else (res, hook_res)


def _tile(n, want):
    if n <= want:
        return n
    t = (want // 128) * 128
    while t > 128 and n % t:
        t -= 128
    assert n % t == 0, (n, want)
    return t


def _row_tile(s):
    return next((t for t in (512, 256) if s % t == 0), s)


def _norm_fwd(name, x, g, hook=None):
    s, d = x.shape
    tm = _row_tile(s)

    def body(x_ref, g_ref, h_ref, r_ref):
        xv = x_ref[...]
        r = lax.rsqrt(jnp.mean(xv * xv, axis=-1, keepdims=True) + EPS)
        h_ref[...] = ((xv * r) * g_ref[...]).astype(BF16)
        r_ref[...] = r

    row = pl.BlockSpec((tm, d), lambda i: (i, 0))
    res, hook_res = _call(
        body, hook, name=name, grid=(s // tm,),
        in_specs=[row, pl.BlockSpec((1, d), lambda i: (0, 0))],
        out_specs=[row, pl.BlockSpec((tm, 1), lambda i: (i, 0))],
        out_shape=[jax.ShapeDtypeStruct((s, d), BF16), jax.ShapeDtypeStruct((s, 1), F32)],
        compiler_params=_params(("parallel",), tm * d * 6),
    )(x, g)
    return res if hook is None else (res, hook_res)


def _norm_bwd(name, dh, x, r, g, dres, hook=None):
    s, d = x.shape
    tm = _row_tile(s)

    def body(dh_ref, x_ref, r_ref, g_ref, dres_ref, dx_ref, dxb_ref, dg_ref):
        rv = r_ref[...]
        xn = x_ref[...] * rv
        dhv = dh_ref[...]
        dxn = dhv * g_ref[...]
        dx = dres_ref[...] + rv * (dxn - xn * jnp.mean(dxn * xn, axis=-1, keepdims=True))
        dx_ref[...] = dx
        dxb_ref[...] = dx.astype(BF16)
        part = jnp.sum(dhv * xn, axis=0, keepdims=True)

        @pl.when(pl.program_id(0) == 0)
        def _():
            dg_ref[...] = part

        @pl.when(pl.program_id(0) > 0)
        def _():
            dg_ref[...] += part

    row = pl.BlockSpec((tm, d), lambda i: (i, 0))
    vec = pl.BlockSpec((1, d), lambda i: (0, 0))
    res, hook_res = _call(
        body, hook, name=name, grid=(s // tm,),
        in_specs=[row, row, pl.BlockSpec((tm, 1), lambda i: (i, 0)), vec, row],
        out_specs=[row, row, vec],
        out_shape=[jax.ShapeDtypeStruct((s, d), F32), jax.ShapeDtypeStruct((s, d), BF16),
                   jax.ShapeDtypeStruct((1, d), F32)],
        compiler_params=_params(("arbitrary",), tm * d * 18),
    )(dh, x, r, g, dres)
    return res if hook is None else (res, hook_res)


def _loss_head(x3, target, g):
    s, d = x3.shape
    tm = _row_tile(s)

    def body(x_ref, t_ref, g_ref, dx_ref, dxb_ref, loss_ref, dg_ref):
        xv = x_ref[...]
        gv = g_ref[...]
        r = lax.rsqrt(jnp.mean(xv * xv, axis=-1, keepdims=True) + EPS)
        xn = xv * r
        err = xn * gv - t_ref[...]
        loss = 0.5 * jnp.sum(jnp.mean(err * err, axis=-1, keepdims=True), axis=0, keepdims=True)
        dy = err / d
        dxn = dy * gv
        dx = r * (dxn - xn * jnp.mean(dxn * xn, axis=-1, keepdims=True))
        dx_ref[...] = dx
        dxb_ref[...] = dx.astype(BF16)
        dg = jnp.sum(dy * xn, axis=0, keepdims=True)
        loss_row = jnp.broadcast_to(loss, (1, 128))

        @pl.when(pl.program_id(0) == 0)
        def _():
            dg_ref[...] = dg
            loss_ref[...] = loss_row

        @pl.when(pl.program_id(0) > 0)
        def _():
            dg_ref[...] += dg
            loss_ref[...] += loss_row

    row = pl.BlockSpec((tm, d), lambda i: (i, 0))
    vec = pl.BlockSpec((1, d), lambda i: (0, 0))
    return _call(
        body, None, name="loss_head", grid=(s // tm,),
        in_specs=[row, row, vec],
        out_specs=[row, row, pl.BlockSpec((1, 128), lambda i: (0, 0)), vec],
        out_shape=[jax.ShapeDtypeStruct((s, d), F32), jax.ShapeDtypeStruct((s, d), BF16),
                   jax.ShapeDtypeStruct((1, 128), F32), jax.ShapeDtypeStruct((1, d), F32)],
        compiler_params=_params(("arbitrary",), tm * d * 14),
    )(x3, target, g)[0]


def _low_lanes(rows):
    return lax.broadcasted_iota(jnp.int32, (rows, PAIR), 1) < HEAD_DIM


def _to_classes(dst_refs, cols, value, perm_ref, dils):
    rows = value.shape[0]
    if any(dil > 1 for dil in dils):
        perm_ref[...] = value
    for dst_ref, dil in zip(dst_refs, dils):
        if dil == 1:
            dst_ref[0, :, cols] = value.astype(dst_ref.dtype)
            continue
        for r in range(dil):
            dst_ref[r, :, cols] = perm_ref[pl.ds(r, rows // dil, stride=dil), :].astype(dst_ref.dtype)


def _from_classes(src_ref, cols, perm_ref, dil):
    if dil == 1:
        return src_ref[0, :, cols].astype(F32)
    rows = perm_ref.shape[0]
    for r in range(dil):
        perm_ref[pl.ds(r, rows // dil, stride=dil), :] = src_ref[r, :, cols].astype(F32)
    return perm_ref[...]


def _class_spec(dil, tm, width):
    return pl.BlockSpec((dil, tm // dil, width), lambda i: (0, i, 0))


def _class_shape(dil, s, width, dtype):
    return jax.ShapeDtypeStruct((dil, s // dil, width), dtype)


DILATIONS = tuple(d for _, d in DILATED_BRANCHES)


def _mix_fwd(oa, obs, lses, ga, gb):
    s, qa = oa.shape
    qb = obs[0].shape[2]
    tm = _row_tile(s)
    all_lanes = slice(0, 128)

    def body(oa_ref, o1_ref, o2_ref, o3_ref, l1_ref, l2_ref, l3_ref, ga_ref, gb_ref,
             mix_ref, ob_ref, t1_ref, t2_ref, t3_ref, ra_ref, rb_ref, perm_ref):
        oav = oa_ref[...]
        ra = lax.rsqrt(jnp.mean(oav * oav, axis=-1, keepdims=True) + EPS)
        ra_ref[...] = ra
        mix_ref[:, 0:qa] = ((oav * ra) * ga_ref[...]).astype(BF16)
        l1, l2, l3 = [_from_classes(l_ref, all_lanes, perm_ref, dil)
                      for l_ref, dil in zip((l1_ref, l2_ref, l3_ref), DILATIONS)]
        mx = jnp.maximum(jnp.maximum(l1, l2), l3)
        e1, e2, e3 = jnp.exp(l1 - mx), jnp.exp(l2 - mx), jnp.exp(l3 - mx)
        tot = e1 + e2 + e3
        lse = mx + jnp.log(tot)
        _to_classes((t1_ref, t2_ref, t3_ref), all_lanes, lse, perm_ref, DILATIONS)
        ws = (e1 / tot, e2 / tot, e3 / tot)
        low = _low_lanes(tm)
        ssq = jnp.zeros((tm, 1), F32)
        for i in range(qb // PAIR):
            sl = slice(i * PAIR, (i + 1) * PAIR)
            acc = jnp.zeros((tm, PAIR), F32)
            for w, o_ref, dil in zip(ws, (o1_ref, o2_ref, o3_ref), DILATIONS):
                wexp = jnp.where(low, w[:, 2 * i:2 * i + 1], w[:, 2 * i + 1:2 * i + 2])
                acc = acc + wexp * _from_classes(o_ref, sl, perm_ref, dil)
            ob_ref[:, sl] = acc
            ssq = ssq + jnp.sum(acc * acc, axis=-1, keepdims=True)
        rb = lax.rsqrt(ssq / qb + EPS)
        rb_ref[...] = rb
        mix_ref[:, qa:qa + qb] = ((ob_ref[...] * rb) * gb_ref[...]).astype(BF16)

    def row(w):
        return pl.BlockSpec((tm, w), lambda i: (i, 0))

    def vec(w):
        return pl.BlockSpec((1, w), lambda i: (0, 0))

    return _call(
        body, None, name="mix_fwd", grid=(s // tm,),
        in_specs=([row(qa)] + [_class_spec(d, tm, qb) for d in DILATIONS]
                  + [_class_spec(d, tm, 128) for d in DILATIONS] + [vec(qa), vec(qb)]),
        out_specs=([row(qa + qb), row(qb)] + [_class_spec(d, tm, 128) for d in DILATIONS] + [row(1), row(1)]),
        out_shape=([jax.ShapeDtypeStruct((s, qa + qb), BF16), jax.ShapeDtypeStruct((s, qb), F32)]
                   + [_class_shape(d, s, 128, F32) for d in DILATIONS]
                   + [jax.ShapeDtypeStruct((s, 1), F32), jax.ShapeDtypeStruct((s, 1), F32)]),
        scratch_shapes=[pltpu.VMEM((tm, 128), F32)],
        compiler_params=_params(("parallel",), tm * (qa + 4 * qb) * 4 + tm * (qa + qb) * 2 + tm * 4096),
    )(oa, *obs, *lses, ga, gb)[0]


def _head_rowsums(prod, rows):
    low = _low_lanes(rows)
    lane = lax.broadcasted_iota(jnp.int32, (rows, 128), 1)
    out = jnp.zeros((rows, 128), F32)
    for i in range(prod.shape[1] // PAIR):
        tile = prod[:, i * PAIR:(i + 1) * PAIR]
        lo = jnp.sum(jnp.where(low, tile, 0.0), axis=-1, keepdims=True)
        hi = jnp.sum(jnp.where(low, 0.0, tile), axis=-1, keepdims=True)
        out = jnp.where(lane == 2 * i, lo, out)
        out = jnp.where(lane == 2 * i + 1, hi, out)
    return out


def _mix_bwd(dmix, oa, ob, ra, rb, ga, gb, hook=None):
    s, qa = oa.shape
    qb = ob.shape[1]
    tm = _row_tile(s)

    def one(dy, o, r, g):
        xn = o * r
        dxn = dy * g
        do = r * (dxn - xn * jnp.mean(dxn * xn, axis=-1, keepdims=True))
        return do, jnp.sum(dy * xn, axis=0, keepdims=True), _head_rowsums(do * o, tm)

    def body(dmix_ref, oa_ref, ob_ref, ra_ref, rb_ref, ga_ref, gb_ref,
             doa_ref, dob1_ref, dob2_ref, dob3_ref, dla_ref, dlb1_ref, dlb2_ref, dlb3_ref,
             dga_ref, dgb_ref, perm_ref):
        doa, dga, dla = one(dmix_ref[:, 0:qa], oa_ref[...], ra_ref[...], ga_ref[...])
        dob, dgb, dlb = one(dmix_ref[:, qa:qa + qb], ob_ref[...], rb_ref[...], gb_ref[...])
        doa_ref[...] = doa.astype(BF16)
        dla_ref[...] = dla
        _to_classes((dlb1_ref, dlb2_ref, dlb3_ref), slice(0, 128), dlb, perm_ref, DILATIONS)
        for i in range(qb // PAIR):
            sl = slice(i * PAIR, (i + 1) * PAIR)
            _to_classes((dob1_ref, dob2_ref, dob3_ref), sl, dob[:, sl], perm_ref, DILATIONS)

        @pl.when(pl.program_id(0) == 0)
        def _():
            dga_ref[...] = dga
            dgb_ref[...] = dgb

        @pl.when(pl.program_id(0) > 0)
        def _():
            dga_ref[...] += dga
            dgb_ref[...] += dgb

    def row(w):
        return pl.BlockSpec((tm, w), lambda i: (i, 0))

    def vec(w):
        return pl.BlockSpec((1, w), lambda i: (0, 0))

    res, hook_res = _call(
        body, hook, name="mix_bwd", grid=(s // tm,),
        in_specs=[row(qa + qb), row(qa), row(qb), row(1), row(1), vec(qa), vec(qb)],
        out_specs=([row(qa)] + [_class_spec(d, tm, qb) for d in DILATIONS] + [row(128)]
                   + [_class_spec(d, tm, 128) for d in DILATIONS] + [vec(qa), vec(qb)]),
        out_shape=([jax.ShapeDtypeStruct((s, qa), BF16)] + [_class_shape(d, s, qb, BF16) for d in DILATIONS]
                   + [jax.ShapeDtypeStruct((s, 128), F32)] + [_class_shape(d, s, 128, F32) for d in DILATIONS]
                   + [jax.ShapeDtypeStruct((1, qa), F32), jax.ShapeDtypeStruct((1, qb), F32)]),
        scratch_shapes=[pltpu.VMEM((tm, 128), F32)],
        compiler_params=_params(("arbitrary",), tm * (qa + qb) * 16),
    )(dmix, oa, ob, ra, rb, ga, gb)
    return res if hook is None else (res, hook_res)


def _assemble_dproj(dqa, dkva, dqs, dks, dvs):
    s, qa = dqa.shape
    kva = dkva.shape[1]
    qb = dqs[0].shape[2]
    width = qa + kva + 3 * qb
    tm = _row_tile(s)

    def body(dqa_ref, dkva_ref, q1, q2, q3, k1, k2, k3, v1, v2, v3, dp_ref, db_ref, perm_ref):
        first = pl.program_id(0) == 0

        def emit(off, val):
            dp_ref[:, off:off + PAIR] = val.astype(BF16)
            col = jnp.sum(val, axis=0, keepdims=True)

            @pl.when(first)
            def _():
                db_ref[:, off:off + PAIR] = col

            @pl.when(jnp.logical_not(first))
            def _():
                db_ref[:, off:off + PAIR] += col

        for i in range(qa // PAIR):
            emit(i * PAIR, dqa_ref[:, i * PAIR:(i + 1) * PAIR])
        for i in range(kva // PAIR):
            emit(qa + i * PAIR, dkva_ref[:, i * PAIR:(i + 1) * PAIR])
        for j, branch_refs in enumerate(((q1, q2, q3), (k1, k2, k3), (v1, v2, v3))):
            for i in range(qb // PAIR):
                sl = slice(i * PAIR, (i + 1) * PAIR)
                total = None
                for ref, dil in zip(branch_refs, DILATIONS):
                    val = _from_classes(ref, sl, perm_ref, dil)
                    total = val if total is None else total + val
                emit(qa + kva + j * qb + i * PAIR, total)

    def row(w):
        return pl.BlockSpec((tm, w), lambda i: (i, 0))

    return _call(
        body, None, name="assemble_dproj", grid=(s // tm,),
        in_specs=[row(qa), row(kva)] + [_class_spec(d, tm, qb) for d in DILATIONS] * 3,
        out_specs=[row(width), pl.BlockSpec((1, width), lambda i: (0, 0))],
        out_shape=[jax.ShapeDtypeStruct((s, width), BF16), jax.ShapeDtypeStruct((1, width), F32)],
        scratch_shapes=[pltpu.VMEM((tm, 128), F32)],
        compiler_params=_params(("arbitrary",), tm * (qa + kva + 9 * qb) * 4 + tm * width * 2),
    )(dqa, dkva, *dqs, *dks, *dvs)[0]


def _fill_bias(bias_ref, n_pairs, max_steps, dil, slopes, sink_ref=None):
    qi = lax.broadcasted_iota(jnp.int32, (BLOCK, 2 * BLOCK), 0)
    kj = lax.broadcasted_iota(jnp.int32, (BLOCK, 2 * BLOCK), 1)
    steps = qi + BLOCK - kj
    dist = (steps * dil).astype(F32)
    band = (steps >= 0) & (steps <= max_steps)
    assert sink_ref is None or max_steps < BLOCK
    for first in (0, 1):
        valid = band & (kj >= BLOCK) if first else band
        for i in range(n_pairs):
            tables = []
            for half in (0, 1):
                table = jnp.where(valid, -(slopes[2 * i + half] * dist), NEG_INF)
                if sink_ref is not None:
                    table = jnp.where(kj == 0, sink_ref[2 * i + half], table)
                tables.append(table)
            bias_ref[first, i] = jnp.concatenate(tables, axis=0)


def _without_sink_row(tile):
    row = lax.broadcasted_iota(jnp.int32, tile.shape, 0)
    return jnp.where(row == 0, jnp.zeros_like(tile), tile)


def _bias_shape(n_pairs):
    return pltpu.VMEM((2, n_pairs, 2 * BLOCK, 2 * BLOCK), F32)


def _stack_heads(tile, low):
    zero = jnp.zeros_like(tile)
    return jnp.concatenate([jnp.where(low, tile, zero), jnp.where(low, zero, tile)], axis=0)


def _unstack_heads(stacked, low):
    return jnp.where(low, stacked[0:BLOCK], stacked[BLOCK:2 * BLOCK])


def _head_columns(ref, i):
    return jnp.concatenate([ref[:, 2 * i:2 * i + 1], ref[:, 2 * i + 1:2 * i + 2]], axis=0)


def _swap_halves(t):
    return pltpu.roll(t, HEAD_DIM, 1)


def _dup_group(t_bf16, group):
    t = t_bf16.astype(F32)
    low = lax.broadcasted_iota(jnp.int32, t.shape, 1) < HEAD_DIM
    keep = low if group == 0 else jnp.logical_not(low)
    return jnp.where(keep, t, _swap_halves(t)).astype(BF16)


def _attn_fwd(name, q, kv, *, dil, max_steps, slopes, sinks=None, hook=None):
    grouped = sinks is not None
    _, length, w = q.shape
    n_pairs = w // PAIR
    nb = length // BLOCK
    heads_per_group = 2 * n_pairs // N_KV_GROUPS

    def body(*refs):
        if grouped:
            sink_ref, q_ref, kvp_ref, kvc_ref, o_ref, lse_ref, bias_ref = refs
        else:
            q_ref, kp_ref, kc_ref, vp_ref, vc_ref, o_ref, lse_ref, bias_ref = refs
        n = pl.program_id(1)

        @pl.when((pl.program_id(0) == 0) & (n == 0))
        def _():
            _fill_bias(bias_ref, n_pairs, max_steps, dil, slopes, sink_ref if grouped else None)

        first = (n == 0).astype(jnp.int32)
        low = _low_lanes(BLOCK)
        lane = lax.broadcasted_iota(jnp.int32, (BLOCK, 128), 1)
        lse_acc = jnp.zeros((BLOCK, 128), F32)
        if grouped:
            kv_all = jnp.concatenate([kvp_ref[...], kvc_ref[...]], axis=0)
            k_dup = [_without_sink_row(_dup_group(kv_all[:, 0:PAIR], g)) for g in range(N_KV_GROUPS)]
            v_dup = [_without_sink_row(_dup_group(kv_all[:, PAIR:2 * PAIR], g)) for g in range(N_KV_GROUPS)]
        for i in range(n_pairs):
            sl = slice(i * PAIR, (i + 1) * PAIR)
            qs = _stack_heads(q_ref[:, sl] * ATT_SCALE, low)
            if grouped:
                kk, vv = k_dup[2 * i // heads_per_group], v_dup[2 * i // heads_per_group]
            else:
                kk = jnp.concatenate([kp_ref[:, sl], kc_ref[:, sl]], axis=0)
                vv = jnp.concatenate([vp_ref[:, sl], vc_ref[:, sl]], axis=0)
            sc = lax.dot_general(qs, kk, (((1,), (1,)), ((), ())), preferred_element_type=F32)
            sc = sc + bias_ref[first, i]
            m = jnp.max(sc, axis=-1, keepdims=True)
            p = jnp.exp(sc - m)
            den = jnp.sum(p, axis=-1, keepdims=True)
            o = jnp.dot(p.astype(BF16), vv, preferred_element_type=F32) / den
            o_ref[:, sl] = _unstack_heads(o, low).astype(o_ref.dtype)
            lse = m + jnp.log(den)
            lse_acc = jnp.where(lane == 2 * i, lse[0:BLOCK], lse_acc)
            lse_acc = jnp.where(lane == 2 * i + 1, lse[BLOCK:2 * BLOCK], lse_acc)
        lse_ref[...] = lse_acc

    def cur(width):
        return pl.BlockSpec((None, BLOCK, width), lambda r, n: (r, n, 0))

    def prev(width):
        return pl.BlockSpec((None, BLOCK, width), lambda r, n: (r, jnp.maximum(n - 1, 0), 0))

    if grouped:
        kvw = kv.shape[2]
        operands = [sinks, q, kv, kv]
        in_specs = [SMEM_SPEC, cur(w), prev(kvw), cur(kvw)]
    else:
        operands = [q, kv[0], kv[0], kv[1], kv[1]]
        in_specs = [cur(w), prev(w), cur(w), prev(w), cur(w)]
    res, hook_res = _call(
        body, hook, name=name, grid=(dil, nb), in_specs=in_specs,
        out_specs=[cur(w), cur(128)],
        out_shape=[jax.ShapeDtypeStruct((dil, length, w), F32 if grouped else BRANCH_DTYPE),
                   jax.ShapeDtypeStruct((dil, length, 128), F32)],
        scratch_shapes=[_bias_shape(n_pairs)],
        compiler_params=_params(("arbitrary", "arbitrary"), BLOCK * w * 16 + n_pairs * BLOCK * BLOCK * 16),
    )(*operands)
    return res if hook is None else (res, hook_res)


def _attn_bwd(name, q, kv, do, lse, delta, *, dil, max_steps, slopes, sinks=None, hook=None):
    grouped = sinks is not None
    _, length, w = q.shape
    n_pairs = w // PAIR
    nb = length // BLOCK
    heads_per_group = 2 * n_pairs // N_KV_GROUPS
    pairs_per_group = n_pairs // N_KV_GROUPS

    def body(*refs):
        if grouped:
            (sink_ref, q_ref, kvp_ref, kvc_ref, do_ref, lse_ref, dl_ref,
             dq_ref, dkv_ref, dsink_ref, acc_ref, bias_ref) = refs
        else:
            (q_ref, kp_ref, kc_ref, vp_ref, vc_ref, do_ref, lse_ref, dl_ref,
             dq_ref, dk_ref, dv_ref, acck_ref, accv_ref, bias_ref) = refs
        n = pl.program_id(1)

        @pl.when((pl.program_id(0) == 0) & (n == 0))
        def _():
            _fill_bias(bias_ref, n_pairs, max_steps, dil, slopes, sink_ref if grouped else None)

        @pl.when(n == 0)
        def _():
            if grouped:
                acc_ref[...] = jnp.zeros_like(acc_ref)

                @pl.when(pl.program_id(0) == 0)
                def _():
                    dsink_ref[...] = jnp.zeros_like(dsink_ref)
            else:
                acck_ref[...] = jnp.zeros_like(acck_ref)
                accv_ref[...] = jnp.zeros_like(accv_ref)

        @pl.when(n == nb)
        def _():
            if grouped:
                dkv_ref[...] = acc_ref[...]
            else:
                dk_ref[...] = acck_ref[...].astype(dk_ref.dtype)
                dv_ref[...] = accv_ref[...].astype(dv_ref.dtype)

        @pl.when(n < nb)
        def _():
            first = (n == 0).astype(jnp.int32)
            low = _low_lanes(BLOCK)
            low_kv = _low_lanes(2 * BLOCK)
            lane1 = lax.broadcasted_iota(jnp.int32, (1, 128), 1)
            if grouped:
                kv_all = jnp.concatenate([kvp_ref[...], kvc_ref[...]], axis=0)
                k_dup = [_without_sink_row(_dup_group(kv_all[:, 0:PAIR], g)) for g in range(N_KV_GROUPS)]
                v_dup = [_without_sink_row(_dup_group(kv_all[:, PAIR:2 * PAIR], g)) for g in range(N_KV_GROUPS)]
                dk_grp =[jnp.zeros((2 * BLOCK, PAIR), F32) for _ in range(N_KV_GROUPS)]
                dv_grp = [jnp.zeros((2 * BLOCK, PAIR), F32) for _ in range(N_KV_GROUPS)]
                dsink = jnp.zeros((1, 128), F32)
            for i in range(n_pairs):
                sl = slice(i * PAIR, (i + 1) * PAIR)
                qs = _stack_heads(q_ref[:, sl] * ATT_SCALE, low)
                dos = _stack_heads(do_ref[:, sl], low)
                if grouped:
                    grp = 2 * i // heads_per_group
                    kk, vv = k_dup[grp], v_dup[grp]
                else:
                    kk = jnp.concatenate([kp_ref[:, sl], kc_ref[:, sl]], axis=0)
                    vv = jnp.concatenate([vp_ref[:, sl], vc_ref[:, sl]], axis=0)
                lse_col = _head_columns(lse_ref, i)
                dl_col = _head_columns(dl_ref, i)
                sc = lax.dot_general(qs, kk, (((1,), (1,)), ((), ())), preferred_element_type=F32)
                p = jnp.exp(sc + bias_ref[first, i] - lse_col)
                dp = lax.dot_general(dos, vv, (((1,), (1,)), ((), ())), preferred_element_type=F32)
                ds_f32 = p * (dp - dl_col)
                ds = ds_f32.astype(BF16)
                dq = jnp.dot(ds, kk, preferred_element_type=F32)
                dkk = lax.dot_general(ds, qs, (((0,), (0,)), ((), ())), preferred_element_type=F32)
                dvv = lax.dot_general(p.astype(BF16), dos, (((0,), (0,)), ((), ())),
                                      preferred_element_type=F32)
                if grouped:
                    for half in (0, 1):
                        contrib = jnp.sum(ds_f32[half * BLOCK:(half + 1) * BLOCK, 0:1], axis=0, keepdims=True)
                        dsink = jnp.where(lane1 == 2 * i + half, dsink + contrib, dsink)
                dq_ref[:, sl] = (_unstack_heads(dq, low) * ATT_SCALE).astype(dq_ref.dtype)
                if grouped:
                    dk_grp[grp] = dk_grp[grp] + dkk
                    dv_grp[grp] = dv_grp[grp] + dvv
                else:
                    dk_ref[:, sl] = (acck_ref[:, sl] + dkk[0:BLOCK]).astype(dk_ref.dtype)
                    acck_ref[:, sl] = dkk[BLOCK:2 * BLOCK]
                    dv_ref[:, sl] = (accv_ref[:, sl] + dvv[0:BLOCK]).astype(dv_ref.dtype)
                    accv_ref[:, sl] = dvv[BLOCK:2 * BLOCK]
            if grouped:
                folded = [_without_sink_row(t + _swap_halves(t)) for t in dk_grp + dv_grp]
                dk_tile = jnp.where(low_kv, folded[0], folded[1])
                dv_tile = jnp.where(low_kv, folded[2], folded[3])
                part = jnp.concatenate([dk_tile, dv_tile], axis=1)
                dkv_ref[...] = acc_ref[...] + part[0:BLOCK]
                acc_ref[...] = part[BLOCK:2 * BLOCK]
                dsink_ref[...] += dsink

    last = nb - 1

    def cur(width):
        return pl.BlockSpec((None, BLOCK, width), lambda r, n: (r, jnp.minimum(n, last), 0))

    def prev(width):
        return pl.BlockSpec((None, BLOCK, width),
                            lambda r, n: (r, jnp.maximum(jnp.minimum(n, last) - 1, 0), 0))

    def done(width):
        return pl.BlockSpec((None, BLOCK, width), lambda r, n: (r, jnp.maximum(n - 1, 0), 0))

    if grouped:
        assert pairs_per_group * N_KV_GROUPS == n_pairs and heads_per_group % 2 == 0
        kvw = kv.shape[2]
        operands = [sinks, q, kv, kv, do, lse, delta]
        in_specs = [SMEM_SPEC, cur(w), prev(kvw), cur(kvw), cur(w), cur(128), cur(128)]
        out_specs = [cur(w), done(kvw), pl.BlockSpec((1, 128), lambda r, n: (0, 0))]
        out_shape = [jax.ShapeDtypeStruct((dil, length, w), F32), jax.ShapeDtypeStruct((dil, length, kvw), F32),
                     jax.ShapeDtypeStruct((1, 128), F32)]
        scratch = [pltpu.VMEM((BLOCK, kvw), F32), _bias_shape(n_pairs)]
    else:
        operands = [q, kv[0], kv[0], kv[1], kv[1], do, lse, delta]
        in_specs = [cur(w), prev(w), cur(w), prev(w), cur(w), cur(w), cur(128), cur(128)]
        out_specs = [cur(w), done(w), done(w)]
        out_shape = [jax.ShapeDtypeStruct((dil, length, w), BRANCH_DTYPE)] * 3
        scratch = [pltpu.VMEM((BLOCK, w), F32), pltpu.VMEM((BLOCK, w), F32), _bias_shape(n_pairs)]
    res, hook_res = _call(
        body, hook, name=name, grid=(dil, nb + 1), in_specs=in_specs, out_specs=out_specs,
        out_shape=out_shape, scratch_shapes=scratch,
        compiler_params=_params(("arbitrary", "arbitrary"), BLOCK * w * 32 + n_pairs * BLOCK * BLOCK * 16),
    )(*operands)
    return res if hook is None else (res, hook_res)


def _adamw(name, w, g, m, v):
    rows, cols = w.shape
    tm = next((t for t in (256, 128, 64, 32, 16, 8) if rows % t == 0), rows)

    def body(w_ref, g_ref, m_ref, v_ref, d_ref, nm_ref, nv_ref, g_out_ref):
        gv = g_ref[...]
        mn = ADAM_B1 * m_ref[...] + (1.0 - ADAM_B1) * gv
        vn = ADAM_B2 * v_ref[...] + (1.0 - ADAM_B2) * (gv * gv)
        m_hat = mn / (1.0 - ADAM_B1 ** ADAM_STEP)
        v_hat = vn / (1.0 - ADAM_B2 ** ADAM_STEP)
        d_ref[...] = -ADAM_LR * (m_hat / (jnp.sqrt(v_hat) + ADAM_EPS) + ADAM_WD * w_ref[...])
        nm_ref[...] = mn
        nv_ref[...] = vn
        g_out_ref[...] = gv

    spec = pl.BlockSpec((tm, cols), lambda i: (i, 0))
    return _call(
        body, None, name=name, grid=(rows // tm,), in_specs=[spec] * 4, out_specs=[spec] * 4,
        out_shape=[jax.ShapeDtypeStruct(w.shape, F32)] * 4,
        compiler_params=_params(("parallel",), tm * cols * 32),
    )(w, g, m, v)[0]


def _mesh_position():
    return lax.axis_index("x"), lax.axis_index("y"), lax.axis_index("c")


def _other_chips(x, y):
    return [(1 - x, y), (x, 1 - y), (1 - x, 1 - y)]


def _gather_hook(gathered, lo, hi, mid_fraction=0.6):
    rows, cols = gathered.shape[0] // N_CHIPS, gathered.shape[1]
    half, n = rows // 2, hi - lo
    assert lo % 16 == 0 and n % 16 == 0 and half % 16 == 0

    def region(out, owner_chip, which_half):
        return out.at[pl.ds(pl.multiple_of(owner_chip * rows + which_half * half + lo, 16), n)]

    def parts(outs, sems):
        x, y, c = _mesh_position()
        return outs[0], sems, c, 2 * x + y, (x, y, 1 - c), _other_chips(x, y)

    def start(ops, outs, sems):
        out, (send, recv, fsend, frecv), c, chip, sibling, others = parts(outs, sems)
        mine = region(out, chip, c)
        for k, (px, py) in enumerate(others):
            _remote(mine, mine, send.at[k], recv.at[k], (px, py, c)).start()

    def mid(ops, outs, sems):
        out, (send, recv, fsend, frecv), c, chip, sibling, others = parts(outs, sems)
        for k, (px, py) in enumerate(others):
            landed = region(out, 2 * px + py, c)
            _remote(landed, landed, send.at[k], recv.at[k], (px, py, c)).wait_recv()
            _remote(landed, landed, fsend.at[k], frecv.at[k], sibling).start()

    def finish(ops, outs, sems):
        out, (send, recv, fsend, frecv), c, chip, sibling, others = parts(outs, sems)
        mine = region(out, chip, c)
        for k, (px, py) in enumerate(others):
            passed = region(out, 2 * px + py, 1 - c)
            _remote(passed, passed, fsend.at[k], frecv.at[k], sibling).wait_recv()
        for k, (px, py) in enumerate(others):
            landed = region(out, 2 * px + py, c)
            _remote(landed, landed, fsend.at[k], frecv.at[k], sibling).wait_send()
            _remote(mine, mine, send.at[k], recv.at[k], (px, py, c)).wait_send()

    return _Hook([gathered], [jax.ShapeDtypeStruct(gathered.shape, gathered.dtype)],
                 [pltpu.SemaphoreType.DMA((3,))] * 4, start, finish, mid, aliases={0: 0},
                 mid_fraction=mid_fraction)


def _own_shard_in_place(name, shard, chip):
    rows, cols = shard.shape
    tr = next(t for t in (544, 512, 320, 256, 128, 64, 32, 16) if rows % t == 0)

    def body(chip_ref, w_ref, o_ref):
        o_ref[...] = w_ref[...].astype(BF16)

    return pl.pallas_call(
        body, name=name,
        grid_spec=pltpu.PrefetchScalarGridSpec(
            num_scalar_prefetch=1, grid=(rows // tr,),
            in_specs=[pl.BlockSpec((tr, cols), lambda i, chip_ref: (i, 0))],
            out_specs=pl.BlockSpec((tr, cols), lambda i, chip_ref: (chip_ref[0] * (rows // tr) + i, 0))),
        out_shape=jax.ShapeDtypeStruct((N_CHIPS * rows, cols), BF16),
        compiler_params=_params(("parallel",), tr * cols * 6),
    )(chip, shard)


def _exchange_hook(grad):
    rows, cols = grad.shape[0] // N_CHIPS, grad.shape[1]
    half = rows // 2
    assert half % 16 == 0

    def copies(ops, outs, sems):
        x, y, c = _mesh_position()
        send, recv = sems
        return [_remote(ops[0].at[pl.ds(pl.multiple_of(k * rows + (1 - c) * half, 16), half)], outs[0].at[k],
                        send.at[k], recv.at[k], (x, y, 1 - c)) for k in range(N_CHIPS)]

    def start(ops, outs, sems):
        for cp in copies(ops, outs, sems):
            cp.start()

    def finish(ops, outs, sems):
        for cp in copies(ops, outs, sems):
            cp.wait_recv()
            cp.wait_send()

    return _Hook([grad], [jax.ShapeDtypeStruct((N_CHIPS, half, cols), grad.dtype)],
                 [pltpu.SemaphoreType.DMA((N_CHIPS,))] * 2, start, finish)


def _scatter_hook(chip_sum):
    _, half, cols = chip_sum.shape

    def copies(ops, outs, sems):
        x, y, c = _mesh_position()
        send, recv = sems
        return [_remote(ops[0].at[2 * px + py], outs[0].at[k], send.at[k], recv.at[k], (px, py, c))
                for k, (px, py) in enumerate(_other_chips(x, y))]

    def start(ops, outs, sems):
        for cp in copies(ops, outs, sems):
            cp.start()

    def finish(ops, outs, sems):
        for cp in copies(ops, outs, sems):
            cp.wait_recv()
            cp.wait_send()

    return _Hook([chip_sum], [jax.ShapeDtypeStruct((3, half, cols), chip_sum.dtype)],
                 [pltpu.SemaphoreType.DMA((3,))] * 2, start, finish)


def _sum_tile(half):
    return 256 if half % 256 == 0 else half


def _chip_add(name, grad, from_sibling, core):
    n_chips, half, cols = from_sibling.shape
    rows = 2 * half
    tr = _sum_tile(half)

    def body(core_ref, g_ref, s_ref, o_ref):
        o_ref[...] = (g_ref[...].astype(F32) + s_ref[...].astype(F32)).astype(o_ref.dtype)

    tile = pl.BlockSpec((None, tr, cols), lambda k, i, core_ref: (k, i, 0))
    return pl.pallas_call(
        body, name=name,
        grid_spec=pltpu.PrefetchScalarGridSpec(
            num_scalar_prefetch=1, grid=(n_chips, half // tr),
            in_specs=[pl.BlockSpec((tr, cols), lambda k, i, core_ref:
                                   (k * (rows // tr) + core_ref[0] * (half // tr) + i, 0)), tile],
            out_specs=tile),
        out_shape=jax.ShapeDtypeStruct(from_sibling.shape, from_sibling.dtype),
        compiler_params=_params(("parallel", "parallel"), 3 * tr * cols * 4),
    )(core, grad, from_sibling)


def _final_add(name, chip_sum, from_chips, chip, core):
    _, half, cols = chip_sum.shape
    tr = _sum_tile(half)

    def body(chip_ref, core_ref, own_ref, others_ref, o_ref):
        total = own_ref[...].astype(F32)
        for k in range(3):
            total = total + others_ref[k].astype(F32)
        o_ref[...] = total

    return pl.pallas_call(
        body, name=name,
        grid_spec=pltpu.PrefetchScalarGridSpec(
            num_scalar_prefetch=2, grid=(half // tr,),
            in_specs=[pl.BlockSpec((None, tr, cols), lambda i, chip_ref, core_ref: (chip_ref[0], i, 0)),
                      pl.BlockSpec((3, tr, cols), lambda i, chip_ref, core_ref: (0, i, 0))],
            out_specs=pl.BlockSpec((tr, cols),
                                   lambda i, chip_ref, core_ref: (core_ref[0] * (half // tr) + i, 0))),
        out_shape=jax.ShapeDtypeStruct((2 * half, cols), F32),
        compiler_params=_params(("parallel",), 6 * tr * cols * 4),
    )(chip, core, chip_sum, from_chips)


def _share_halves(shards, small):
    n_s = len(shards)
    rows_s = small.shape[0]

    def body(*refs):
        small_ref = refs[n_s]
        outs, small_out = refs[n_s + 1:2 * n_s + 1], refs[2 * n_s + 1]
        small_all, send, recv, small_send, small_recv = refs[2 * n_s + 2:]
        x, y, c = _mesh_position()
        me = 4 * x + 2 * y + c
        sibling = (x, y, 1 - c)
        pending = []
        for i in range(n_s):
            half = shards[i].shape[0] // 2
            mine = outs[i].at[pl.ds(pl.multiple_of(c * half, 16), half)]
            cp = _remote(mine, mine, send.at[i], recv.at[i], sibling)
            cp.start()
            pending.append(cp)
        small_all[me] = small_ref[...]
        for j in range(N_DEV - 1):
            peer = (me + 1 + j) % N_DEV
            cp = _remote(small_all.at[me], small_all.at[me], small_send.at[j], small_recv.at[j],
                         (peer // 4, (peer // 2) % 2, peer % 2))
            cp.start()
            pending.append(cp)
        for i in range(n_s):
            half = shards[i].shape[0] // 2
            theirs = outs[i].at[pl.ds(pl.multiple_of((1 - c) * half, 16), half)]
            _remote(theirs, theirs, send.at[i], recv.at[i], sibling).wait_recv()
        for j in range(N_DEV - 1):
            peer = (me + N_DEV - 1 - j) % N_DEV
            _remote(small_all.at[peer], small_all.at[peer], small_send.at[j], small_recv.at[j],
                    sibling).wait_recv()
        total = small_all[0]
        for dev in range(1, N_DEV):
            total = total + small_all[dev]
        small_out[...] = total
        for cp in pending:
            cp.wait_send()

    res = pl.pallas_call(
        body, name="share_halves",
        in_specs=[HBM_SPEC] * n_s + [VMEM_SPEC], out_specs=[HBM_SPEC] * n_s + [VMEM_SPEC],
        out_shape=[jax.ShapeDtypeStruct(sh.shape, sh.dtype) for sh in shards]
        + [jax.ShapeDtypeStruct((rows_s, 128), F32)],
        scratch_shapes=[pltpu.VMEM((N_DEV, rows_s, 128), F32),
                        pltpu.SemaphoreType.DMA((n_s,)), pltpu.SemaphoreType.DMA((n_s,)),
                        pltpu.SemaphoreType.DMA((N_DEV - 1,)), pltpu.SemaphoreType.DMA((N_DEV - 1,))],
        input_output_aliases={i: i for i in range(n_s)},
    )(*shards, small)
    return res[:n_s], res[n_s]


def _pack_small(parts, rows):
    flat = jnp.concatenate([p.reshape(-1) for p in parts])
    flat = jnp.pad(flat, (0, rows * 128 - flat.shape[0]))
    return flat.reshape(rows, 128)


def _unpack_small(packed, shapes):
    flat = packed.reshape(-1)
    out, off = [], 0
    for shp in shapes:
        n = int(np.prod(shp))
        out.append(flat[off:off + n].reshape(shp))
        off += n
    return out


def kernel(x, g_attn, w_in, b_in, sinks_a, g_out_a, g_out_b, w_out, g_mlp, w_1, w_2, g_final, loss_target, m_g_attn, m_w_in, m_b_in, m_sinks_a, m_g_out_a, m_g_out_b, m_w_out, m_g_mlp, m_w_1, m_w_2, m_g_final, v_g_attn, v_w_in, v_b_in, v_sinks_a, v_g_out_a, v_g_out_b, v_w_out, v_g_mlp, v_w_1, v_w_2, v_g_final):
    s, d = x.shape[1], x.shape[2]
    d_in = b_in.shape[1]
    qa = g_out_a.shape[1]
    qb = g_out_b.shape[1]
    kva = 2 * N_KV_GROUPS * HEAD_DIM
    assert d_in == qa + kva + 3 * qb and qa + qb == w_out.shape[1] * N_CHIPS
    d_ff = w_1.shape[2] * N_CHIPS
    ff_shard = w_1.shape[2]
    n_heads_a, n_heads_b = qa // HEAD_DIM, qb // HEAD_DIM
    slopes_a, slopes_b = alibi_slopes(n_heads_a), alibi_slopes(n_heads_b)

    x2d = x[0]
    target = loss_target[0]

    core_index = lax.axis_index("c").astype(jnp.int32).reshape(1)
    chip_index = (2 * lax.axis_index("x") + lax.axis_index("y")).astype(jnp.int32).reshape(1)
    shards = {"w_in": w_in[0].T, "w_out": w_out[0], "w_1": w_1[0], "w_2": w_2[0]}
    halves = [sh.shape[0] // 2 for sh in shards.values()]
    w_in_t, w_out_g, w_1_g, w_2_g = [_own_shard_in_place(f"place_{n}", sh, chip_index)
                                     for n, sh in shards.items()]

    tm = _tile(s, 1024)

    (h1, r1), (w_in_t,) = _norm_fwd("norm_attn", x2d, g_attn,
                                    hook=_gather_hook(w_in_t, 0, halves[0], mid_fraction=1.0))

    q_a, = _project_by_class("proj_qa", h1, w_in_t, b_in, 0, qa, (1,))
    kv_a, = _project_by_class("proj_kva", h1, w_in_t, b_in, qa, kva, (1,))
    q_bs, (w_out_g,) = _project_by_class("proj_qb", h1, w_in_t, b_in, qa + kva, qb, DILATIONS,
                                         hook=_gather_hook(w_out_g, 0, halves[1]))
    k_bs = _project_by_class("proj_kb", h1, w_in_t, b_in, qa + kva + qb, qb, DILATIONS)
    v_bs = _project_by_class("proj_vb", h1, w_in_t, b_in, qa + kva + 2 * qb, qb, DILATIONS)

    quarter = halves[2] // 4
    sinks = sinks_a.reshape(-1)
    (o_a, lse_a), (w_1_g,) = _attn_fwd("attn_a_fwd", q_a, kv_a, dil=1, max_steps=WINDOW_A - 1, slopes=slopes_a,
                                       sinks=sinks, hook=_gather_hook(w_1_g, 0, quarter))
    o_a = o_a[0]
    o_bs, lse_bs = [], []
    for n, (window, dil) in enumerate(DILATED_BRANCHES):
        (o, l), (w_1_g,) = _attn_fwd(f"attn_b{dil}_fwd", q_bs[n], (k_bs[n], v_bs[n]), dil=dil,
                                     max_steps=window // dil, slopes=slopes_b,
                                     hook=_gather_hook(w_1_g, (n + 1) * quarter, (n + 2) * quarter))
        o_bs.append(o)
        lse_bs.append(l)
    w_1_g = w_1_g.reshape(N_CHIPS, d, ff_shard)
    mix, o_b, *lse_tot, r_a, r_b = _mix_fwd(o_a, o_bs, lse_bs, g_out_a, g_out_b)

    tn = _tile(d, 1024)
    a_spec, b_spec = _mm_specs("nn", tm, tn, d)
    tile_mn = pl.BlockSpec((tm, tn), lambda i, j, k: (i, j))
    x2 = _matmul("out_proj", mix, w_out_g, [x2d], mode="nn", grid=(s // tm, d // tn, 1),
                 a_spec=a_spec, b_spec=b_spec, extra_specs=[tile_mn],
                 out_shapes=[jax.ShapeDtypeStruct((s, d), F32)], out_specs=[tile_mn],
                 epilogue=lambda acc, res: (acc + res,))[0]

    h2, r2 = _norm_fwd("norm_mlp", x2, g_mlp)

    tn = _tile(ff_shard, 1024)
    per = ff_shard // tn
    a_spec, _ = _mm_specs("nn", tm, tn, d)
    tile_mn = pl.BlockSpec((tm, tn), lambda i, j, k: (i, j))
    (u,), (w_2_g,) = _matmul(
        "mlp_up", h2, w_1_g, [], mode="nn", grid=(s // tm, d_ff // tn, 1),
        a_spec=a_spec, b_spec=pl.BlockSpec((None, d, tn), lambda i, j, k: (j // per, 0, j % per)),
        extra_specs=[], out_shapes=[jax.ShapeDtypeStruct((s, d_ff), BF16)], out_specs=[tile_mn],
        epilogue=lambda acc: (jnp.maximum(acc, 0.0),),
        hook=_gather_hook(w_2_g, 0, halves[3]))

    tn = _tile(d, 1024)
    tk = _tile(d_ff, 2048)
    a_spec, b_spec = _mm_specs("nn", tm, tn, tk)
    tile_mn = pl.BlockSpec((tm, tn), lambda i, j, k: (i, j))
    x3 = _matmul("mlp_down", u, w_2_g, [x2], mode="nn", grid=(s // tm, d // tn, d_ff // tk),
                 a_spec=a_spec, b_spec=b_spec, extra_specs=[tile_mn],
                 out_shapes=[jax.ShapeDtypeStruct((s, d), F32)], out_specs=[tile_mn],
                 prologue=lambda a: a * a, epilogue=lambda acc, res: (acc + res,), acc_shape=(tm, tn))[0]

    dx3, dx3b, loss_part, dg_final = _loss_head(x3, target, g_final.reshape(1, d))

    tn = _tile(d_ff, 1024)
    a_spec, b_spec = _mm_specs("nt", tm, tn, d)
    tile_mn = pl.BlockSpec((tm, tn), lambda i, j, k: (i, j))
    dpre = _matmul("mlp_down_dx", dx3b, w_2_g, [u], mode="nt", grid=(s // tm, d_ff // tn, 1),
                   a_spec=a_spec, b_spec=b_spec, extra_specs=[tile_mn],
                   out_shapes=[jax.ShapeDtypeStruct((s, d_ff), BF16)], out_specs=[tile_mn],
                   epilogue=lambda acc, uu: (acc * (2.0 * uu.astype(F32)),))[0]

    wire = GRAD_WIRE_DTYPE
    tk_s = _tile(s, 2048)
    tmw = _tile(d_ff, 1024)
    a_spec, b_spec = _mm_specs("tn", tmw, d, tk_s)
    dw_2 = _matmul("mlp_down_dw", u, dx3b, [], mode="tn", grid=(d_ff // tmw, 1, s // tk_s),
                   a_spec=a_spec, b_spec=b_spec, extra_specs=[],
                   out_shapes=[jax.ShapeDtypeStruct((d_ff, d), wire)],
                   out_specs=[pl.BlockSpec((tmw, d), lambda i, j, k: (i, j))],
                   prologue=lambda a: a * a, epilogue=lambda acc: (acc,), acc_shape=(tmw, d))[0]

    tn = _tile(d, 1024)
    tk = _tile(ff_shard, 2048)
    per = ff_shard // tk
    a_spec, _ = _mm_specs("nt", tm, tn, tk)
    tile_mn = pl.BlockSpec((tm, tn), lambda i, j, k: (i, j))
    (dh2,), (sib_2,) = _matmul(
        "mlp_up_dx", dpre, w_1_g, [], mode="nt", grid=(s // tm, d // tn, d_ff // tk),
        a_spec=a_spec, b_spec=pl.BlockSpec((None, tn, tk), lambda i, j, k: (k // per, j, k % per)),
        extra_specs=[], out_shapes=[jax.ShapeDtypeStruct((s, d), F32)], out_specs=[tile_mn],
        epilogue=lambda acc: (acc,), acc_shape=(tm, tn), hook=_exchange_hook(dw_2))
    chip_sum_2 = _chip_add("chip_add_w_2", dw_2, sib_2, core_index)

    tmw = _tile(d, 1024)
    tnw = _tile(ff_shard, 2048)
    per = ff_shard // tnw
    a_spec, b_spec = _mm_specs("tn", tmw, tnw, tk_s)
    dw_1 = _matmul("mlp_up_dw", h2, dpre, [], mode="tn", grid=(d // tmw, d_ff // tnw, s // tk_s),
                   a_spec=a_spec, b_spec=b_spec, extra_specs=[],
                   out_shapes=[jax.ShapeDtypeStruct((N_CHIPS, d, ff_shard), wire)],
                   out_specs=[pl.BlockSpec((None, tmw, tnw), lambda i, j, k: (j // per, i, j % per))],
                   epilogue=lambda acc: (acc,), acc_shape=(tmw, tnw))[0]

    dw_1 = dw_1.reshape(N_CHIPS * d, ff_shard)
    (dx2, dx2b, dg_mlp), (sib_1,) = _norm_bwd("norm_mlp_bwd", dh2, x2, r2, g_mlp, dx3,
                                              hook=_exchange_hook(dw_1))
    chip_sum_1 = _chip_add("chip_add_w_1", dw_1, sib_1, core_index)

    tn = _tile(d, 1024)
    a_spec, b_spec = _mm_specs("nt", tm, tn, d)
    tile_mn = pl.BlockSpec((tm, tn), lambda i, j, k: (i, j))
    dmix = _matmul("out_proj_dx", dx2b, w_out_g, [], mode="nt", grid=(s // tm, d // tn, 1),
                   a_spec=a_spec, b_spec=b_spec, extra_specs=[],
                   out_shapes=[jax.ShapeDtypeStruct((s, d), F32)], out_specs=[tile_mn],
                   epilogue=lambda acc: (acc,))[0]

    tmw = _tile(d, 1024)
    a_spec, b_spec = _mm_specs("tn", tmw, d, tk_s)
    dw_out = _matmul("out_proj_dw", mix, dx2b, [], mode="tn", grid=(d // tmw, 1, s // tk_s),
                     a_spec=a_spec, b_spec=b_spec, extra_specs=[],
                     out_shapes=[jax.ShapeDtypeStruct((d, d), wire)],
                     out_specs=[pl.BlockSpec((tmw, d), lambda i, j, k: (i, j))],
                     epilogue=lambda acc: (acc,), acc_shape=(tmw, d))[0]

    mix_grads, (sib_out,) = _mix_bwd(dmix, o_a, o_b, r_a, r_b, g_out_a, g_out_b, hook=_exchange_hook(dw_out))
    do_a, do_bs, delta_a, delta_bs = mix_grads[0], mix_grads[1:4], mix_grads[4], mix_grads[5:8]
    dg_out_a, dg_out_b = mix_grads[8:]
    chip_sum_out = _chip_add("chip_add_w_out", dw_out, sib_out, core_index)

    (dq_a, dkv_a, dsinks), (chips_2,) = _attn_bwd(
        "attn_a_bwd", q_a, kv_a, do_a[None], lse_a, delta_a[None], dil=1, max_steps=WINDOW_A - 1,
        slopes=slopes_a, sinks=sinks, hook=_scatter_hook(chip_sum_2))
    dqs, dks, dvs = [], [], []
    scatter = {1: chip_sum_1, 4: chip_sum_out}
    arrived = {}
    for n, (window, dil) in enumerate(DILATED_BRANCHES):
        res = _attn_bwd(f"attn_b{dil}_bwd", q_bs[n], (k_bs[n], v_bs[n]), do_bs[n], lse_tot[n],
                        delta_bs[n], dil=dil, max_steps=window // dil, slopes=slopes_b,
                        hook=_scatter_hook(scatter[dil]) if dil in scatter else None)
        if dil in scatter:
            res, (arrived[dil],) = res
        dq, dk, dv = res
        dqs.append(dq)
        dks.append(dk)
        dvs.append(dv)
    half_2 = _final_add("final_add_w_2", chip_sum_2, chips_2, chip_index, core_index)
    half_1 = _final_add("final_add_w_1", chip_sum_1, arrived[1], chip_index, core_index)
    half_out = _final_add("final_add_w_out", chip_sum_out, arrived[4], chip_index, core_index)
    dproj, db_in = _assemble_dproj(dq_a[0], dkv_a[0], dqs, dks, dvs)

    tmw = d_in // 2 if (d_in // 2) % 128 == 0 else d_in
    tnw = _tile(d, 1024)
    tk_s = _tile(s, 1024)
    a_spec, b_spec = _mm_specs("tn", tmw, tnw, tk_s)
    dw_in_t = _matmul("in_proj_dw", dproj, h1, [], mode="tn", grid=(d_in // tmw, d // tnw, s // tk_s),
                      a_spec=a_spec, b_spec=b_spec, extra_specs=[],
                      out_shapes=[jax.ShapeDtypeStruct((d_in, d), wire)],
                      out_specs=[pl.BlockSpec((tmw, tnw), lambda i, j, k: (i, j))],
                      epilogue=lambda acc: (acc,), acc_shape=(tmw, tnw))[0]

    tn = _tile(d, 1024)
    a_spec, b_spec = _mm_specs("nn", tm, tn, d_in)
    tile_mn = pl.BlockSpec((tm, tn), lambda i, j, k: (i, j))
    (dh1,), (sib_in,) = _matmul("in_proj_dx", dproj, w_in_t, [], mode="nn", grid=(s // tm, d // tn, 1),
                                a_spec=a_spec, b_spec=b_spec, extra_specs=[],
                                out_shapes=[jax.ShapeDtypeStruct((s, d), F32)], out_specs=[tile_mn],
                                epilogue=lambda acc: (acc,), hook=_exchange_hook(dw_in_t))
    chip_sum_in = _chip_add("chip_add_w_in", dw_in_t, sib_in, core_index)

    (grad_x, _, dg_attn), (chips_in,) = _norm_bwd("norm_attn_bwd", dh1, x2d, r1, g_attn, dx2,
                                                  hook=_scatter_hook(chip_sum_in))
    half_in = _final_add("final_add_w_in", chip_sum_in, chips_in, chip_index, core_index)

    small_parts = [dg_attn, db_in, dsinks[:, :n_heads_a], dg_out_a, dg_out_b, dg_mlp, dg_final]
    small_shapes = [g_attn.shape, b_in.shape, sinks_a.shape, g_out_a.shape, g_out_b.shape, g_mlp.shape,
                    g_final.shape]
    n_small = sum(int(np.prod(shp)) for shp in small_shapes)
    rows_s = -(-n_small // (8 * 128)) * 8
    (gw_in_t, gw_out, gw_1, gw_2), small_sum = _share_halves(
        [half_in, half_out, half_1, half_2], _pack_small(small_parts, rows_s))

    upd_in = [t.T for t in _adamw("adamw_w_in", w_in[0].T, gw_in_t, m_w_in[0].T, v_w_in[0].T)]
    upd_out = _adamw("adamw_w_out", w_out[0], gw_out, m_w_out[0], v_w_out[0])
    upd_1 = _adamw("adamw_w_1", w_1[0], gw_1, m_w_1[0], v_w_1[0])
    upd_2 = _adamw("adamw_w_2", w_2[0], gw_2, m_w_2[0], v_w_2[0])
    small_w = [g_attn, b_in, sinks_a, g_out_a, g_out_b, g_mlp, g_final]
    small_m = [m_g_attn, m_b_in, m_sinks_a, m_g_out_a, m_g_out_b, m_g_mlp, m_g_final]
    small_v = [v_g_attn, v_b_in, v_sinks_a, v_g_out_a, v_g_out_b, v_g_mlp, v_g_final]
    upd_small = _adamw("adamw_small", _pack_small(small_w, rows_s), small_sum,
                       _pack_small(small_m, rows_s), _pack_small(small_v, rows_s))
    d_small, m_small, v_small, g_small = [_unpack_small(t, small_shapes) for t in upd_small]

    loss = lax.psum(loss_part[0, 0], ("x", "y", "c"))

    def ordered(small, big):
        w_in_v, w_out_v, w_1_v, w_2_v = big
        return [small[0], w_in_v[None], small[1], small[2], small[3], small[4], w_out_v[None], small[5],
                w_1_v[None], w_2_v[None], small[6]]

    grads = ordered(g_small, (upd_in[3], upd_out[3], upd_1[3], upd_2[3]))
    deltas = ordered(d_small, (upd_in[0], upd_out[0], upd_1[0], upd_2[0]))
    new_m = ordered(m_small, (upd_in[1], upd_out[1], upd_1[1], upd_2[1]))
    new_v = ordered(v_small, (upd_in[2], upd_out[2], upd_1[2], upd_2[2]))
    return (loss, grad_x[None], *grads, *deltas, *new_m, *new_v)
```

```python
import jax
import jax.numpy as jnp
import numpy as np
from jax import lax
from jax.experimental import pallas as pl
from jax.experimental.pallas import tpu as pltpu

F32 = jnp.float32
BF16 = jnp.bfloat16

HEAD_DIM = 64
BLOCK = 128
PAIR = 2 * HEAD_DIM
N_KV_GROUPS = 2
WINDOW_A = 128
DILATED_BRANCHES = ((128, 1), (512, 4), (2048, 16))
EPS = 1e-5
NEG_INF = -1e30
ATT_SCALE = HEAD_DIM ** -0.5

ADAM_LR = 0.001
ADAM_B1 = 0.9
ADAM_B2 = 0.999
ADAM_EPS = 1e-08
ADAM_WD = 0.01
ADAM_STEP = 10

N_CHIPS = 4
N_DEV = 8
MESH = pl.DeviceIdType.MESH
GRAD_WIRE_DTYPE = jnp.bfloat16
BRANCH_DTYPE = jnp.bfloat16

VMEM_CAPACITY_V7X = 64 * 1024 * 1024
VMEM_LIMIT_MAX = VMEM_CAPACITY_V7X - 8 * 1024 * 1024
VMEM_LIMIT_MIN = VMEM_CAPACITY_V7X - 16 * 1024 * 1024

HBM_SPEC = pl.BlockSpec(memory_space=pltpu.HBM)
VMEM_SPEC = pl.BlockSpec(memory_space=pltpu.VMEM)
SMEM_SPEC = pl.BlockSpec(memory_space=pltpu.SMEM)


def _nbytes(shape, dtype):
    return int(np.prod([s for s in shape if s is not None])) * jnp.dtype(dtype).itemsize


def _params(semantics, block_bytes):
    limit = min(max(2 * block_bytes + (4 << 20), VMEM_LIMIT_MIN), VMEM_LIMIT_MAX)
    return pltpu.CompilerParams(dimension_semantics=semantics, vmem_limit_bytes=limit)


class _Hook:
    def __init__(self, operands, out_shape, sems, start, finish, mid=None, aliases=None,
                 mid_fraction=0.6):
        self.operands, self.out_shape, self.sems = list(operands), list(out_shape), list(sems)
        self.start, self.mid, self.finish = start, mid, finish
        self.aliases = dict(aliases or {})
        self.mid_fraction = mid_fraction


def _call(body, hook, *, name, grid, in_specs, out_specs, out_shape, scratch_shapes=(), compiler_params):
    in_specs, out_specs, out_shape = list(in_specs), list(out_specs), list(out_shape)
    scratch_shapes = list(scratch_shapes)
    if hook is None:
        call = pl.pallas_call(body, name=name, grid=grid, in_specs=in_specs, out_specs=out_specs,
                              out_shape=out_shape, scratch_shapes=scratch_shapes,
                              compiler_params=compiler_params)
        return lambda *operands: (call(*operands), [])
    n_in, n_hin, n_out, n_hout, n_scr = (len(in_specs), len(hook.operands), len(out_specs),
                                         len(hook.out_shape), len(scratch_shapes))
    total = int(np.prod(grid))
    t_mid = min(int(total * hook.mid_fraction), total - 1)

    def wrapped(*refs):
        ins, h_in = refs[:n_in], refs[n_in:n_in + n_hin]
        o0 = n_in + n_hin
        outs, h_out = refs[o0:o0 + n_out], refs[o0 + n_out:o0 + n_out + n_hout]
        s0 = o0 + n_out + n_hout
        scr, h_sems = refs[s0:s0 + n_scr], refs[s0 + n_scr:]
        t = pl.program_id(0)
        for axis in range(1, len(grid)):
            t = t * grid[axis] + pl.program_id(axis)

        @pl.when(t == 0)
        def _():
            hook.start(h_in, h_out, h_sems)

        body(*ins, *outs, *scr)
        if hook.mid is not None:
            @pl.when(t == t_mid)
            def _():
                hook.mid(h_in, h_out, h_sems)

        @pl.when(t == total - 1)
        def _():
            hook.finish(h_in, h_out, h_sems)

    params = pltpu.CompilerParams(dimension_semantics=("arbitrary",) * len(grid),
                                  vmem_limit_bytes=compiler_params.vmem_limit_bytes)
    call = pl.pallas_call(
        wrapped, name=name, grid=grid,
        in_specs=in_specs + [HBM_SPEC] * n_hin, out_specs=out_specs + [HBM_SPEC] * n_hout,
        out_shape=out_shape + hook.out_shape, scratch_shapes=scratch_shapes + hook.sems,
        input_output_aliases={n_in + a: n_out + b for a, b in hook.aliases.items()},
        compiler_params=params)

    def run(*operands):
        res = call(*operands, *hook.operands)
        return res[:n_out], res[n_out:]

    return run


def _remote(src, dst, send_sem, recv_sem, device):
    return pltpu.make_async_remote_copy(src_ref=src, dst_ref=dst, send_sem=send_sem, recv_sem=recv_sem,
                                        device_id=device, device_id_type=MESH)


def alibi_slopes(n):
    return [float(v) for v in np.asarray(2.0 ** (-8.0 * (np.arange(n) + 1) / n), dtype=np.float32)]


def _matmul(name, a, b, extras, *, mode, grid, a_spec, b_spec, extra_specs, out_shapes, out_specs,
            epilogue, prologue=None, acc_shape=None, hook=None):
    dims = {"nn": ((1,), (0,)), "nt": ((1,), (1,)), "tn": ((0,), (0,))}[mode]
    nk = grid[2]
    n_ex, n_out = len(extras), len(out_shapes)

    def body(a_ref, b_ref, *rest):
        ex, outs = rest[:n_ex], rest[n_ex:n_ex + n_out]
        av = a_ref[...]
        if prologue is not None:
            av = prologue(av)
        part = lax.dot_general(av, b_ref[...], (dims, ((), ())), preferred_element_type=F32)

        def finish(acc):
            res = epilogue(acc, *[e[...] for e in ex])
            for o, r in zip(outs, res):
                o[...] = r.astype(o.dtype)

        if nk == 1:
            finish(part)
        else:
            acc_ref = rest[-1]
            k = pl.program_id(2)

            @pl.when(k == 0)
            def _():
                acc_ref[...] = part

            @pl.when(k > 0)
            def _():
                acc_ref[...] += part

            @pl.when(k == nk - 1)
            def _():
                finish(acc_ref[...])

    blocks = [(a_spec.block_shape, a.dtype), (b_spec.block_shape, b.dtype)]
    blocks += [(s.block_shape, e.dtype) for s, e in zip(extra_specs, extras)]
    blocks += [(s.block_shape, o.dtype) for s, o in zip(out_specs, out_shapes)]
    nbytes = sum(_nbytes(s, d) for s, d in blocks)
    scratch = []
    if nk > 1:
        scratch.append(pltpu.VMEM(acc_shape, F32))
        nbytes += _nbytes(acc_shape, F32)
    res, hook_res = _call(
        body, hook, name=name, grid=grid,
        in_specs=[a_spec, b_spec, *extra_specs], out_specs=list(out_specs), out_shape=list(out_shapes),
        scratch_shapes=scratch,
        compiler_params=_params(("parallel", "parallel", "arbitrary"), nbytes),
    )(a, b, *extras)
    return res if hook is None else (res, hook_res)


def _mm_specs(mode, tm, tn, tk, b_block=None, b_map=None):
    if mode == "tn":
        a_spec = pl.BlockSpec((tk, tm), lambda i, j, k: (k, i))
    else:
        a_spec = pl.BlockSpec((tm, tk), lambda i, j, k: (i, k))
    if b_block is not None:
        b_spec = pl.BlockSpec(b_block, b_map)
    elif mode == "nt":
        b_spec = pl.BlockSpec((tn, tk), lambda i, j, k: (j, k))
    else:
        b_spec = pl.BlockSpec((tk, tn), lambda i, j, k: (k, j))
    return a_spec, b_spec


def _project_by_class(name, h, w_t, bias, row_off, width, dilations, hook=None):
    s, d = h.shape
    tm = _tile(s, 2048)
    tn = 512 if width % 512 == 0 and row_off % 512 == 0 else _tile(width, 256)
    off = row_off // tn
    assert row_off % tn == 0 and tn % 128 == 0
    n_out = len(dilations)

    def body(h_ref, w_ref, b_ref, *rest):
        outs, perm_ref = rest[:n_out], rest[n_out]
        acc = lax.dot_general(h_ref[...], w_ref[...], (((1,), (1,)), ((), ())), preferred_element_type=F32)
        acc = acc + b_ref[...]
        for j in range(tn // 128):
            cols = slice(j * 128, (j + 1) * 128)
            _to_classes(outs, cols, acc[:, cols], perm_ref, dilations)

    blocks = tm * d * 2 + tn * d * 2 + 3 * tm * tn * 2 + tm * 128 * 4
    res, hook_res = _call(
        body, hook, name=name, grid=(s // tm, width // tn),
        in_specs=[pl.BlockSpec((tm, d), lambda i, j: (i, 0)), pl.BlockSpec((tn, d), lambda i, j: (j + off, 0)),
                  pl.BlockSpec((1, tn), lambda i, j: (0, j + off))],
        out_specs=[pl.BlockSpec((dil, tm // dil, tn), lambda i, j: (0, i, j)) for dil in dilations],
        out_shape=[_class_shape(dil, s, width, BF16) for dil in dilations],
        scratch_shapes=[pltpu.VMEM((tm, 128), F32)],
        compiler_params=_params(("parallel", "parallel"), blocks),
    )(h, w_t, bias)
    return res if hook is None else (res, hook_res)


def _tile(n, want):
    if n <= want:
        return n
    t = (want // 128) * 128
    while t > 128 and n % t:
        t -= 128
    assert n % t == 0, (n, want)
    return t


def _row_tile(s):
    return next((t for t in (512, 256) if s % t == 0), s)


def _norm_fwd(name, x, g, hook=None):
    s, d = x.shape
    tm = _row_tile(s)

    def body(x_ref, g_ref, h_ref, r_ref):
        xv = x_ref[...]
        r = lax.rsqrt(jnp.mean(xv * xv, axis=-1, keepdims=True) + EPS)
        h_ref[...] = ((xv * r) * g_ref[...]).astype(BF16)
        r_ref[...] = r

    row = pl.BlockSpec((tm, d), lambda i: (i, 0))
    res, hook_res = _call(
        body, hook, name=name, grid=(s // tm,),
        in_specs=[row, pl.BlockSpec((1, d), lambda i: (0, 0))],
        out_specs=[row, pl.BlockSpec((tm, 1), lambda i: (i, 0))],
        out_shape=[jax.ShapeDtypeStruct((s, d), BF16), jax.ShapeDtypeStruct((s, 1), F32)],
        compiler_params=_params(("parallel",), tm * d * 6),
    )(x, g)
    return res if hook is None else (res, hook_res)


def _norm_bwd(name, dh, x, r, g, dres, hook=None):
    s, d = x.shape
    tm = _row_tile(s)

    def body(dh_ref, x_ref, r_ref, g_ref, dres_ref, dx_ref, dxb_ref, dg_ref):
        rv = r_ref[...]
        xn = x_ref[...] * rv
        dhv = dh_ref[...]
        dxn = dhv * g_ref[...]
        dx = dres_ref[...] + rv * (dxn - xn * jnp.mean(dxn * xn, axis=-1, keepdims=True))
        dx_ref[...] = dx
        dxb_ref[...] = dx.astype(BF16)
        part = jnp.sum(dhv * xn, axis=0, keepdims=True)

        @pl.when(pl.program_id(0) == 0)
        def _():
            dg_ref[...] = part

        @pl.when(pl.program_id(0) > 0)
        def _():
            dg_ref[...] += part

    row = pl.BlockSpec((tm, d), lambda i: (i, 0))
    vec = pl.BlockSpec((1, d), lambda i: (0, 0))
    res, hook_res = _call(
        body, hook, name=name, grid=(s // tm,),
        in_specs=[row, row, pl.BlockSpec((tm, 1), lambda i: (i, 0)), vec, row],
        out_specs=[row, row, vec],
        out_shape=[jax.ShapeDtypeStruct((s, d), F32), jax.ShapeDtypeStruct((s, d), BF16),
                   jax.ShapeDtypeStruct((1, d), F32)],
        compiler_params=_params(("arbitrary",), tm * d * 18),
    )(dh, x, r, g, dres)
    return res if hook is None else (res, hook_res)


def _loss_head(x3, target, g):
    s, d = x3.shape
    tm = _row_tile(s)

    def body(x_ref, t_ref, g_ref, dx_ref, dxb_ref, loss_ref, dg_ref):
        xv = x_ref[...]
        gv = g_ref[...]
        r = lax.rsqrt(jnp.mean(xv * xv, axis=-1, keepdims=True) + EPS)
        xn = xv * r
        err = xn * gv - t_ref[...]
        loss = 0.5 * jnp.sum(jnp.mean(err * err, axis=-1, keepdims=True), axis=0, keepdims=True)
        dy = err / d
        dxn = dy * gv
        dx = r * (dxn - xn * jnp.mean(dxn * xn, axis=-1, keepdims=True))
        dx_ref[...] = dx
        dxb_ref[...] = dx.astype(BF16)
        dg = jnp.sum(dy * xn, axis=0, keepdims=True)
        loss_row = jnp.broadcast_to(loss, (1, 128))

        @pl.when(pl.program_id(0) == 0)
        def _():
            dg_ref[...] = dg
            loss_ref[...] = loss_row

        @pl.when(pl.program_id(0) > 0)
        def _():
            dg_ref[...] += dg
            loss_ref[...] += loss_row

    row = pl.BlockSpec((tm, d), lambda i: (i, 0))
    vec = pl.BlockSpec((1, d), lambda i: (0, 0))
    return _call(
        body, None, name="loss_head", grid=(s // tm,),
        in_specs=[row, row, vec],
        out_specs=[row, row, pl.BlockSpec((1, 128), lambda i: (0, 0)), vec],
        out_shape=[jax.ShapeDtypeStruct((s, d), F32), jax.ShapeDtypeStruct((s, d), BF16),
                   jax.ShapeDtypeStruct((1, 128), F32), jax.ShapeDtypeStruct((1, d), F32)],
        compiler_params=_params(("arbitrary",), tm * d * 14),
    )(x3, target, g)[0]


def _low_lanes(rows):
    return lax.broadcasted_iota(jnp.int32, (rows, PAIR), 1) < HEAD_DIM


def _to_classes(dst_refs, cols, value, perm_ref, dils):
    rows = value.shape[0]
    if any(dil > 1 for dil in dils):
        perm_ref[...] = value
    for dst_ref, dil in zip(dst_refs, dils):
        if dil == 1:
            dst_ref[0, :, cols] = value.astype(dst_ref.dtype)
            continue
        for r in range(dil):
            dst_ref[r, :, cols] = perm_ref[pl.ds(r, rows // dil, stride=dil), :].astype(dst_ref.dtype)


def _from_classes(src_ref, cols, perm_ref, dil):
    if dil == 1:
        return src_ref[0, :, cols].astype(F32)
    rows = perm_ref.shape[0]
    for r in range(dil):
        perm_ref[pl.ds(r, rows // dil, stride=dil), :] = src_ref[r, :, cols].astype(F32)
    return perm_ref[...]


def _class_spec(dil, tm, width):
    return pl.BlockSpec((dil, tm // dil, width), lambda i: (0, i, 0))


def _class_shape(dil, s, width, dtype):
    return jax.ShapeDtypeStruct((dil, s // dil, width), dtype)


DILATIONS = tuple(d for _, d in DILATED_BRANCHES)


def _mix_fwd(oa, obs, lses, ga, gb):
    s, qa = oa.shape
    qb = obs[0].shape[2]
    tm = _row_tile(s)
    all_lanes = slice(0, 128)

    def body(oa_ref, o1_ref, o2_ref, o3_ref, l1_ref, l2_ref, l3_ref, ga_ref, gb_ref,
             mix_ref, ob_ref, t1_ref, t2_ref, t3_ref, ra_ref, rb_ref, perm_ref):
        oav = oa_ref[...]
        ra = lax.rsqrt(jnp.mean(oav * oav, axis=-1, keepdims=True) + EPS)
        ra_ref[...] = ra
        mix_ref[:, 0:qa] = ((oav * ra) * ga_ref[...]).astype(BF16)
        l1, l2, l3 = [_from_classes(l_ref, all_lanes, perm_ref, dil)
                      for l_ref, dil in zip((l1_ref, l2_ref, l3_ref), DILATIONS)]
        mx = jnp.maximum(jnp.maximum(l1, l2), l3)
        e1, e2, e3 = jnp.exp(l1 - mx), jnp.exp(l2 - mx), jnp.exp(l3 - mx)
        tot = e1 + e2 + e3
        lse = mx + jnp.log(tot)
        _to_classes((t1_ref, t2_ref, t3_ref), all_lanes, lse, perm_ref, DILATIONS)
        ws = (e1 / tot, e2 / tot, e3 / tot)
        low = _low_lanes(tm)
        ssq = jnp.zeros((tm, 1), F32)
        for i in range(qb // PAIR):
            sl = slice(i * PAIR, (i + 1) * PAIR)
            acc = jnp.zeros((tm, PAIR), F32)
            for w, o_ref, dil in zip(ws, (o1_ref, o2_ref, o3_ref), DILATIONS):
                wexp = jnp.where(low, w[:, 2 * i:2 * i + 1], w[:, 2 * i + 1:2 * i + 2])
                acc = acc + wexp * _from_classes(o_ref, sl, perm_ref, dil)
            ob_ref[:, sl] = acc
            ssq = ssq + jnp.sum(acc * acc, axis=-1, keepdims=True)
        rb = lax.rsqrt(ssq / qb + EPS)
        rb_ref[...] = rb
        mix_ref[:, qa:qa + qb] = ((ob_ref[...] * rb) * gb_ref[...]).astype(BF16)

    def row(w):
        return pl.BlockSpec((tm, w), lambda i: (i, 0))

    def vec(w):
        return pl.BlockSpec((1, w), lambda i: (0, 0))

    return _call(
        body, None, name="mix_fwd", grid=(s // tm,),
        in_specs=([row(qa)] + [_class_spec(d, tm, qb) for d in DILATIONS]
                  + [_class_spec(d, tm, 128) for d in DILATIONS] + [vec(qa), vec(qb)]),
        out_specs=([row(qa + qb), row(qb)] + [_class_spec(d, tm, 128) for d in DILATIONS] + [row(1), row(1)]),
        out_shape=([jax.ShapeDtypeStruct((s, qa + qb), BF16), jax.ShapeDtypeStruct((s, qb), F32)]
                   + [_class_shape(d, s, 128, F32) for d in DILATIONS]
                   + [jax.ShapeDtypeStruct((s, 1), F32), jax.ShapeDtypeStruct((s, 1), F32)]),
        scratch_shapes=[pltpu.VMEM((tm, 128), F32)],
        compiler_params=_params(("parallel",), tm * (qa + 4 * qb) * 4 + tm * (qa + qb) * 2 + tm * 4096),
    )(oa, *obs, *lses, ga, gb)[0]


def _head_rowsums(prod, rows):
    low = _low_lanes(rows)
    lane = lax.broadcasted_iota(jnp.int32, (rows, 128), 1)
    out = jnp.zeros((rows, 128), F32)
    for i in range(prod.shape[1] // PAIR):
        tile = prod[:, i * PAIR:(i + 1) * PAIR]
        lo = jnp.sum(jnp.where(low, tile, 0.0), axis=-1, keepdims=True)
        hi = jnp.sum(jnp.where(low, 0.0, tile), axis=-1, keepdims=True)
        out = jnp.where(lane == 2 * i, lo, out)
        out = jnp.where(lane == 2 * i + 1, hi, out)
    return out


def _mix_bwd(dmix, oa, ob, ra, rb, ga, gb, hook=None):
    s, qa = oa.shape
    qb = ob.shape[1]
    tm = _row_tile(s)

    def one(dy, o, r, g):
        xn = o * r
        dxn = dy * g
        do = r * (dxn - xn * jnp.mean(dxn * xn, axis=-1, keepdims=True))
        return do, jnp.sum(dy * xn, axis=0, keepdims=True), _head_rowsums(do * o, tm)

    def body(dmix_ref, oa_ref, ob_ref, ra_ref, rb_ref, ga_ref, gb_ref,
             doa_ref, dob1_ref, dob2_ref, dob3_ref, dla_ref, dlb1_ref, dlb2_ref, dlb3_ref,
             dga_ref, dgb_ref, perm_ref):
        doa, dga, dla = one(dmix_ref[:, 0:qa], oa_ref[...], ra_ref[...], ga_ref[...])
        dob, dgb, dlb = one(dmix_ref[:, qa:qa + qb], ob_ref[...], rb_ref[...], gb_ref[...])
        doa_ref[...] = doa.astype(BF16)
        dla_ref[...] = dla
        _to_classes((dlb1_ref, dlb2_ref, dlb3_ref), slice(0, 128), dlb, perm_ref, DILATIONS)
        for i in range(qb // PAIR):
            sl = slice(i * PAIR, (i + 1) * PAIR)
            _to_classes((dob1_ref, dob2_ref, dob3_ref), sl, dob[:, sl], perm_ref, DILATIONS)

        @pl.when(pl.program_id(0) == 0)
        def _():
            dga_ref[...] = dga
            dgb_ref[...] = dgb

        @pl.when(pl.program_id(0) > 0)
        def _():
            dga_ref[...] += dga
            dgb_ref[...] += dgb

    def row(w):
        return pl.BlockSpec((tm, w), lambda i: (i, 0))

    def vec(w):
        return pl.BlockSpec((1, w), lambda i: (0, 0))

    res, hook_res = _call(
        body, hook, name="mix_bwd", grid=(s // tm,),
        in_specs=[row(qa + qb), row(qa), row(qb), row(1), row(1), vec(qa), vec(qb)],
        out_specs=([row(qa)] + [_class_spec(d, tm, qb) for d in DILATIONS] + [row(128)]
                   + [_class_spec(d, tm, 128) for d in DILATIONS] + [vec(qa), vec(qb)]),
        out_shape=([jax.ShapeDtypeStruct((s, qa), BF16)] + [_class_shape(d, s, qb, BF16) for d in DILATIONS]
                   + [jax.ShapeDtypeStruct((s, 128), F32)] + [_class_shape(d, s, 128, F32) for d in DILATIONS]
                   + [jax.ShapeDtypeStruct((1, qa), F32), jax.ShapeDtypeStruct((1, qb), F32)]),
        scratch_shapes=[pltpu.VMEM((tm, 128), F32)],
        compiler_params=_params(("arbitrary",), tm * (qa + qb) * 16),
    )(dmix, oa, ob, ra, rb, ga, gb)
    return res if hook is None else (res, hook_res)


def _assemble_dproj(dqa, dkva, dqs, dks, dvs, hook=None):
    s, qa = dqa.shape
    kva = dkva.shape[1]
    qb = dqs[0].shape[2]
    width = qa + kva + 3 * qb
    tm = _row_tile(s)

    def body(dqa_ref, dkva_ref, q1, q2, q3, k1, k2, k3, v1, v2, v3, dp_ref, db_ref, perm_ref):
        first = pl.program_id(0) == 0

        def emit(off, val):
            dp_ref[:, off:off + PAIR] = val.astype(BF16)
            col = jnp.sum(val, axis=0, keepdims=True)

            @pl.when(first)
            def _():
                db_ref[:, off:off + PAIR] = col

            @pl.when(jnp.logical_not(first))
            def _():
                db_ref[:, off:off + PAIR] += col

        for i in range(qa // PAIR):
            emit(i * PAIR, dqa_ref[:, i * PAIR:(i + 1) * PAIR])
        for i in range(kva // PAIR):
            emit(qa + i * PAIR, dkva_ref[:, i * PAIR:(i + 1) * PAIR])
        for j, branch_refs in enumerate(((q1, q2, q3), (k1, k2, k3), (v1, v2, v3))):
            for i in range(qb // PAIR):
                sl = slice(i * PAIR, (i + 1) * PAIR)
                total = None
                for ref, dil in zip(branch_refs, DILATIONS):
                    val = _from_classes(ref, sl, perm_ref, dil)
                    total = val if total is None else total + val
                emit(qa + kva + j * qb + i * PAIR, total)

    def row(w):
        return pl.BlockSpec((tm, w), lambda i: (i, 0))

    res, hook_res = _call(
        body, hook, name="assemble_dproj", grid=(s // tm,),
        in_specs=[row(qa), row(kva)] + [_class_spec(d, tm, qb) for d in DILATIONS] * 3,
        out_specs=[row(width), pl.BlockSpec((1, width), lambda i: (0, 0))],
        out_shape=[jax.ShapeDtypeStruct((s, width), BF16), jax.ShapeDtypeStruct((1, width), F32)],
        scratch_shapes=[pltpu.VMEM((tm, 128), F32)],
        compiler_params=_params(("arbitrary",), tm * (qa + kva + 9 * qb) * 4 + tm * width * 2),
    )(dqa, dkva, *dqs, *dks, *dvs)
    return res if hook is None else (res, hook_res)


def _fill_bias(bias_ref, n_pairs, max_steps, dil, slopes, sink_ref=None):
    qi = lax.broadcasted_iota(jnp.int32, (BLOCK, 2 * BLOCK), 0)
    kj = lax.broadcasted_iota(jnp.int32, (BLOCK, 2 * BLOCK), 1)
    steps = qi + BLOCK - kj
    dist = (steps * dil).astype(F32)
    band = (steps >= 0) & (steps <= max_steps)
    assert sink_ref is None or max_steps < BLOCK
    for first in (0, 1):
        valid = band & (kj >= BLOCK) if first else band
        for i in range(n_pairs):
            tables = []
            for half in (0, 1):
                table = jnp.where(valid, -(slopes[2 * i + half] * dist), NEG_INF)
                if sink_ref is not None:
                    table = jnp.where(kj == 0, sink_ref[2 * i + half], table)
                tables.append(table)
            bias_ref[first, i] = jnp.concatenate(tables, axis=0)


def _without_sink_row(tile):
    row = lax.broadcasted_iota(jnp.int32, tile.shape, 0)
    return jnp.where(row == 0, jnp.zeros_like(tile), tile)


def _bias_shape(n_pairs):
    return pltpu.VMEM((2, n_pairs, 2 * BLOCK, 2 * BLOCK), F32)


def _stack_heads(tile, low):
    zero = jnp.zeros_like(tile)
    return jnp.concatenate([jnp.where(low, tile, zero), jnp.where(low, zero, tile)], axis=0)


def _unstack_heads(stacked, low):
    return jnp.where(low, stacked[0:BLOCK], stacked[BLOCK:2 * BLOCK])


def _head_columns(ref, i):
    return jnp.concatenate([ref[:, 2 * i:2 * i + 1], ref[:, 2 * i + 1:2 * i + 2]], axis=0)


def _swap_halves(t):
    return pltpu.roll(t, HEAD_DIM, 1)


def _dup_group(t_bf16, group):
    t = t_bf16.astype(F32)
    low = lax.broadcasted_iota(jnp.int32, t.shape, 1) < HEAD_DIM
    keep = low if group == 0 else jnp.logical_not(low)
    return jnp.where(keep, t, _swap_halves(t)).astype(BF16)


def _attn_fwd(name, q, kv, *, dil, max_steps, slopes, sinks=None, hook=None):
    grouped = sinks is not None
    _, length, w = q.shape
    n_pairs = w // PAIR
    nb = length // BLOCK
    heads_per_group = 2 * n_pairs // N_KV_GROUPS

    def body(*refs):
        if grouped:
            sink_ref, q_ref, kvp_ref, kvc_ref, o_ref, lse_ref, bias_ref = refs
        else:
            q_ref, kp_ref, kc_ref, vp_ref, vc_ref, o_ref, lse_ref, bias_ref = refs
        n = pl.program_id(1)

        @pl.when((pl.program_id(0) == 0) & (n == 0))
        def _():
            _fill_bias(bias_ref, n_pairs, max_steps, dil, slopes, sink_ref if grouped else None)

        first = (n == 0).astype(jnp.int32)
        low = _low_lanes(BLOCK)
        lane = lax.broadcasted_iota(jnp.int32, (BLOCK, 128), 1)
        lse_acc = jnp.zeros((BLOCK, 128), F32)
        if grouped:
            kv_all = jnp.concatenate([kvp_ref[...], kvc_ref[...]], axis=0)
            k_dup = [_without_sink_row(_dup_group(kv_all[:, 0:PAIR], g)) for g in range(N_KV_GROUPS)]
            v_dup = [_without_sink_row(_dup_group(kv_all[:, PAIR:2 * PAIR], g)) for g in range(N_KV_GROUPS)]
        for i in range(n_pairs):
            sl = slice(i * PAIR, (i + 1) * PAIR)
            qs = _stack_heads(q_ref[:, sl] * ATT_SCALE, low)
            if grouped:
                kk, vv = k_dup[2 * i // heads_per_group], v_dup[2 * i // heads_per_group]
            else:
                kk = jnp.concatenate([kp_ref[:, sl], kc_ref[:, sl]], axis=0)
                vv = jnp.concatenate([vp_ref[:, sl], vc_ref[:, sl]], axis=0)
            sc = lax.dot_general(qs, kk, (((1,), (1,)), ((), ())), preferred_element_type=F32)
            sc = sc + bias_ref[first, i]
            m = jnp.max(sc, axis=-1, keepdims=True)
            p = jnp.exp(sc - m)
            den = jnp.sum(p, axis=-1, keepdims=True)
            o = jnp.dot(p.astype(BF16), vv, preferred_element_type=F32) / den
            o_ref[:, sl] = _unstack_heads(o, low).astype(o_ref.dtype)
            lse = m + jnp.log(den)
            lse_acc = jnp.where(lane == 2 * i, lse[0:BLOCK], lse_acc)
            lse_acc = jnp.where(lane == 2 * i + 1, lse[BLOCK:2 * BLOCK], lse_acc)
        lse_ref[...] = lse_acc

    def cur(width):
        return pl.BlockSpec((None, BLOCK, width), lambda r, n: (r, n, 0))

    def prev(width):
        return pl.BlockSpec((None, BLOCK, width), lambda r, n: (r, jnp.maximum(n - 1, 0), 0))

    if grouped:
        kvw = kv.shape[2]
        operands = [sinks, q, kv, kv]
        in_specs = [SMEM_SPEC, cur(w), prev(kvw), cur(kvw)]
    else:
        operands = [q, kv[0], kv[0], kv[1], kv[1]]
        in_specs = [cur(w), prev(w), cur(w), prev(w), cur(w)]
    res, hook_res = _call(
        body, hook, name=name, grid=(dil, nb), in_specs=in_specs,
        out_specs=[cur(w), cur(128)],
        out_shape=[jax.ShapeDtypeStruct((dil, length, w), F32 if grouped else BRANCH_DTYPE),
                   jax.ShapeDtypeStruct((dil, length, 128), F32)],
        scratch_shapes=[_bias_shape(n_pairs)],
        compiler_params=_params(("arbitrary", "arbitrary"), BLOCK * w * 16 + n_pairs * BLOCK * BLOCK * 16),
    )(*operands)
    return res if hook is None else (res, hook_res)


def _attn_bwd(name, q, kv, do, lse, delta, *, dil, max_steps, slopes, sinks=None, hook=None):
    grouped = sinks is not None
    _, length, w = q.shape
    n_pairs = w // PAIR
    nb = length // BLOCK
    heads_per_group = 2 * n_pairs // N_KV_GROUPS
    pairs_per_group = n_pairs // N_KV_GROUPS

    def body(*refs):
        if grouped:
            (sink_ref, q_ref, kvp_ref, kvc_ref, do_ref, lse_ref, dl_ref,
             dq_ref, dkv_ref, dsink_ref, acc_ref, bias_ref) = refs
        else:
            (q_ref, kp_ref, kc_ref, vp_ref, vc_ref, do_ref, lse_ref, dl_ref,
             dq_ref, dk_ref, dv_ref, acck_ref, accv_ref, bias_ref) = refs
        n = pl.program_id(1)

        @pl.when((pl.program_id(0) == 0) & (n == 0))
        def _():
            _fill_bias(bias_ref, n_pairs, max_steps, dil, slopes, sink_ref if grouped else None)

        @pl.when(n == 0)
        def _():
            if grouped:
                acc_ref[...] = jnp.zeros_like(acc_ref)

                @pl.when(pl.program_id(0) == 0)
                def _():
                    dsink_ref[...] = jnp.zeros_like(dsink_ref)
            else:
                acck_ref[...] = jnp.zeros_like(acck_ref)
                accv_ref[...] = jnp.zeros_like(accv_ref)

        @pl.when(n == nb)
        def _():
            if grouped:
                dkv_ref[...] = acc_ref[...]
            else:
                dk_ref[...] = acck_ref[...].astype(dk_ref.dtype)
                dv_ref[...] = accv_ref[...].astype(dv_ref.dtype)

        @pl.when(n < nb)
        def _():
            first = (n == 0).astype(jnp.int32)
            low = _low_lanes(BLOCK)
            low_kv = _low_lanes(2 * BLOCK)
            lane1 = lax.broadcasted_iota(jnp.int32, (1, 128), 1)
            if grouped:
                kv_all = jnp.concatenate([kvp_ref[...], kvc_ref[...]], axis=0)
                k_dup = [_without_sink_row(_dup_group(kv_all[:, 0:PAIR], g)) for g in range(N_KV_GROUPS)]
                v_dup = [_without_sink_row(_dup_group(kv_all[:, PAIR:2 * PAIR], g)) for g in range(N_KV_GROUPS)]
                dk_grp =[jnp.zeros((2 * BLOCK, PAIR), F32) for _ in range(N_KV_GROUPS)]
                dv_grp = [jnp.zeros((2 * BLOCK, PAIR), F32) for _ in range(N_KV_GROUPS)]
                dsink = jnp.zeros((1, 128), F32)
            for i in range(n_pairs):
                sl = slice(i * PAIR, (i + 1) * PAIR)
                qs = _stack_heads(q_ref[:, sl] * ATT_SCALE, low)
                dos = _stack_heads(do_ref[:, sl], low)
                if grouped:
                    grp = 2 * i // heads_per_group
                    kk, vv = k_dup[grp], v_dup[grp]
                else:
                    kk = jnp.concatenate([kp_ref[:, sl], kc_ref[:, sl]], axis=0)
                    vv = jnp.concatenate([vp_ref[:, sl], vc_ref[:, sl]], axis=0)
                lse_col = _head_columns(lse_ref, i)
                dl_col = _head_columns(dl_ref, i)
                sc = lax.dot_general(qs, kk, (((1,), (1,)), ((), ())), preferred_element_type=F32)
                p = jnp.exp(sc + bias_ref[first, i] - lse_col)
                dp = lax.dot_general(dos, vv, (((1,), (1,)), ((), ())), preferred_element_type=F32)
                ds_f32 = p * (dp - dl_col)
                ds = ds_f32.astype(BF16)
                dq = jnp.dot(ds, kk, preferred_element_type=F32)
                dkk = lax.dot_general(ds, qs, (((0,), (0,)), ((), ())), preferred_element_type=F32)
                dvv = lax.dot_general(p.astype(BF16), dos, (((0,), (0,)), ((), ())),
                                      preferred_element_type=F32)
                if grouped:
                    for half in (0, 1):
                        contrib = jnp.sum(ds_f32[half * BLOCK:(half + 1) * BLOCK, 0:1], axis=0, keepdims=True)
                        dsink = jnp.where(lane1 == 2 * i + half, dsink + contrib, dsink)
                dq_ref[:, sl] = (_unstack_heads(dq, low) * ATT_SCALE).astype(dq_ref.dtype)
                if grouped:
                    dk_grp[grp] = dk_grp[grp] + dkk
                    dv_grp[grp] = dv_grp[grp] + dvv
                else:
                    dk_ref[:, sl] = (acck_ref[:, sl] + dkk[0:BLOCK]).astype(dk_ref.dtype)
                    acck_ref[:, sl] = dkk[BLOCK:2 * BLOCK]
                    dv_ref[:, sl] = (accv_ref[:, sl] + dvv[0:BLOCK]).astype(dv_ref.dtype)
                    accv_ref[:, sl] = dvv[BLOCK:2 * BLOCK]
            if grouped:
                folded = [_without_sink_row(t + _swap_halves(t)) for t in dk_grp + dv_grp]
                dk_tile = jnp.where(low_kv, folded[0], folded[1])
                dv_tile = jnp.where(low_kv, folded[2], folded[3])
                part = jnp.concatenate([dk_tile, dv_tile], axis=1)
                dkv_ref[...] = acc_ref[...] + part[0:BLOCK]
                acc_ref[...] = part[BLOCK:2 * BLOCK]
                dsink_ref[...] += dsink

    last = nb - 1

    def cur(width):
        return pl.BlockSpec((None, BLOCK, width), lambda r, n: (r, jnp.minimum(n, last), 0))

    def prev(width):
        return pl.BlockSpec((None, BLOCK, width),
                            lambda r, n: (r, jnp.maximum(jnp.minimum(n, last) - 1, 0), 0))

    def done(width):
        return pl.BlockSpec((None, BLOCK, width), lambda r, n: (r, jnp.maximum(n - 1, 0), 0))

    if grouped:
        assert pairs_per_group * N_KV_GROUPS == n_pairs and heads_per_group % 2 == 0
        kvw = kv.shape[2]
        operands = [sinks, q, kv, kv, do, lse, delta]
        in_specs = [SMEM_SPEC, cur(w), prev(kvw), cur(kvw), cur(w), cur(128), cur(128)]
        out_specs = [cur(w), done(kvw), pl.BlockSpec((1, 128), lambda r, n: (0, 0))]
        out_shape = [jax.ShapeDtypeStruct((dil, length, w), F32), jax.ShapeDtypeStruct((dil, length, kvw), F32),
                     jax.ShapeDtypeStruct((1, 128), F32)]
        scratch = [pltpu.VMEM((BLOCK, kvw), F32), _bias_shape(n_pairs)]
    else:
        operands = [q, kv[0], kv[0], kv[1], kv[1], do, lse, delta]
        in_specs = [cur(w), prev(w), cur(w), prev(w), cur(w), cur(w), cur(128), cur(128)]
        out_specs = [cur(w), done(w), done(w)]
        out_shape = [jax.ShapeDtypeStruct((dil, length, w), BRANCH_DTYPE)] * 3
        scratch = [pltpu.VMEM((BLOCK, w), F32), pltpu.VMEM((BLOCK, w), F32), _bias_shape(n_pairs)]
    res, hook_res = _call(
        body, hook, name=name, grid=(dil, nb + 1), in_specs=in_specs, out_specs=out_specs,
        out_shape=out_shape, scratch_shapes=scratch,
        compiler_params=_params(("arbitrary", "arbitrary"), BLOCK * w * 32 + n_pairs * BLOCK * BLOCK * 16),
    )(*operands)
    return res if hook is None else (res, hook_res)


def _adamw(name, w, g, m, v):
    rows, cols = w.shape
    tm = next((t for t in (256, 128, 64, 32, 16, 8) if rows % t == 0), rows)

    def body(w_ref, g_ref, m_ref, v_ref, d_ref, nm_ref, nv_ref, g_out_ref):
        gv = g_ref[...]
        mn = ADAM_B1 * m_ref[...] + (1.0 - ADAM_B1) * gv
        vn = ADAM_B2 * v_ref[...] + (1.0 - ADAM_B2) * (gv * gv)
        m_hat = mn / (1.0 - ADAM_B1 ** ADAM_STEP)
        v_hat = vn / (1.0 - ADAM_B2 ** ADAM_STEP)
        d_ref[...] = -ADAM_LR * (m_hat / (jnp.sqrt(v_hat) + ADAM_EPS) + ADAM_WD * w_ref[...])
        nm_ref[...] = mn
        nv_ref[...] = vn
        g_out_ref[...] = gv

    spec = pl.BlockSpec((tm, cols), lambda i: (i, 0))
    return _call(
        body, None, name=name, grid=(rows // tm,), in_specs=[spec] * 4, out_specs=[spec] * 4,
        out_shape=[jax.ShapeDtypeStruct(w.shape, F32)] * 4,
        compiler_params=_params(("parallel",), tm * cols * 32),
    )(w, g, m, v)[0]


def _mesh_position():
    return lax.axis_index("x"), lax.axis_index("y"), lax.axis_index("c")


def _other_chips(x, y):
    return [(1 - x, y), (x, 1 - y), (1 - x, 1 - y)]


def _gather_hook(gathered, lo, hi, mid_fraction=0.6):
    rows, cols = gathered.shape[0] // N_CHIPS, gathered.shape[1]
    half, n = rows // 2, hi - lo
    assert lo % 16 == 0 and n % 16 == 0 and half % 16 == 0

    def region(out, owner_chip, which_half):
        return out.at[pl.ds(pl.multiple_of(owner_chip * rows + which_half * half + lo, 16), n)]

    def parts(outs, sems):
        x, y, c = _mesh_position()
        return outs[0], sems, c, 2 * x + y, (x, y, 1 - c), _other_chips(x, y)

    def start(ops, outs, sems):
        out, (send, recv, fsend, frecv), c, chip, sibling, others = parts(outs, sems)
        mine = region(out, chip, c)
        for k, (px, py) in enumerate(others):
            _remote(mine, mine, send.at[k], recv.at[k], (px, py, c)).start()

    def mid(ops, outs, sems):
        out, (send, recv, fsend, frecv), c, chip, sibling, others = parts(outs, sems)
        for k, (px, py) in enumerate(others):
            landed = region(out, 2 * px + py, c)
            _remote(landed, landed, send.at[k], recv.at[k], (px, py, c)).wait_recv()
            _remote(landed, landed, fsend.at[k], frecv.at[k], sibling).start()

    def finish(ops, outs, sems):
        out, (send, recv, fsend, frecv), c, chip, sibling, others = parts(outs, sems)
        mine = region(out, chip, c)
        for k, (px, py) in enumerate(others):
            passed = region(out, 2 * px + py, 1 - c)
            _remote(passed, passed, fsend.at[k], frecv.at[k], sibling).wait_recv()
        for k, (px, py) in enumerate(others):
            landed = region(out, 2 * px + py, c)
            _remote(landed, landed, fsend.at[k], frecv.at[k], sibling).wait_send()
            _remote(mine, mine, send.at[k], recv.at[k], (px, py, c)).wait_send()

    return _Hook([gathered], [jax.ShapeDtypeStruct(gathered.shape, gathered.dtype)],
                 [pltpu.SemaphoreType.DMA((3,))] * 4, start, finish, mid, aliases={0: 0},
                 mid_fraction=mid_fraction)


def _own_shard_in_place(name, shard, chip):
    rows, cols = shard.shape
    tr = next(t for t in (544, 512, 320, 256, 128, 64, 32, 16) if rows % t == 0)

    def body(chip_ref, w_ref, o_ref):
        o_ref[...] = w_ref[...].astype(BF16)

    return pl.pallas_call(
        body, name=name,
        grid_spec=pltpu.PrefetchScalarGridSpec(
            num_scalar_prefetch=1, grid=(rows // tr,),
            in_specs=[pl.BlockSpec((tr, cols), lambda i, chip_ref: (i, 0))],
            out_specs=pl.BlockSpec((tr, cols), lambda i, chip_ref: (chip_ref[0] * (rows // tr) + i, 0))),
        out_shape=jax.ShapeDtypeStruct((N_CHIPS * rows, cols), BF16),
        compiler_params=_params(("parallel",), tr * cols * 6),
    )(chip, shard)


def _exchange_hook(grad):
    rows, cols = grad.shape[0] // N_CHIPS, grad.shape[1]
    half = rows // 2
    assert half % 16 == 0

    def copies(ops, outs, sems):
        x, y, c = _mesh_position()
        send, recv = sems
        return [_remote(ops[0].at[pl.ds(pl.multiple_of(k * rows + (1 - c) * half, 16), half)], outs[0].at[k],
                        send.at[k], recv.at[k], (x, y, 1 - c)) for k in range(N_CHIPS)]

    def start(ops, outs, sems):
        for cp in copies(ops, outs, sems):
            cp.start()

    def finish(ops, outs, sems):
        for cp in copies(ops, outs, sems):
            cp.wait_recv()
            cp.wait_send()

    return _Hook([grad], [jax.ShapeDtypeStruct((N_CHIPS, half, cols), grad.dtype)],
                 [pltpu.SemaphoreType.DMA((N_CHIPS,))] * 2, start, finish)


def _scatter_hook(chip_sum):
    _, half, cols = chip_sum.shape

    def copies(ops, outs, sems):
        x, y, c = _mesh_position()
        send, recv = sems
        return [_remote(ops[0].at[2 * px + py], outs[0].at[k], send.at[k], recv.at[k], (px, py, c))
                for k, (px, py) in enumerate(_other_chips(x, y))]

    def start(ops, outs, sems):
        for cp in copies(ops, outs, sems):
            cp.start()

    def finish(ops, outs, sems):
        for cp in copies(ops, outs, sems):
            cp.wait_recv()
            cp.wait_send()

    return _Hook([chip_sum], [jax.ShapeDtypeStruct((3, half, cols), chip_sum.dtype)],
                 [pltpu.SemaphoreType.DMA((3,))] * 2, start, finish)


def _share_hook(shard):
    half = shard.shape[0] // 2

    def copy(outs, sems, which):
        x, y, c = _mesh_position()
        rows = outs[0].at[pl.ds(pl.multiple_of((c if which == "mine" else 1 - c) * half, 16), half)]
        return _remote(rows, rows, sems[0].at[0], sems[1].at[0], (x, y, 1 - c))

    def start(ops, outs, sems):
        copy(outs, sems, "mine").start()

    def finish(ops, outs, sems):
        copy(outs, sems, "theirs").wait_recv()
        copy(outs, sems, "mine").wait_send()

    return _Hook([shard], [jax.ShapeDtypeStruct(shard.shape, shard.dtype)],
                 [pltpu.SemaphoreType.DMA((1,))] * 2, start, finish, aliases={0: 0})


def _sum_tile(half):
    return 256 if half % 256 == 0 else half


def _chip_add(name, grad, from_sibling, core):
    n_chips, half, cols = from_sibling.shape
    rows = 2 * half
    tr = _sum_tile(half)

    def body(core_ref, g_ref, s_ref, o_ref):
        o_ref[...] = (g_ref[...].astype(F32) + s_ref[...].astype(F32)).astype(o_ref.dtype)

    tile = pl.BlockSpec((None, tr, cols), lambda k, i, core_ref: (k, i, 0))
    return pl.pallas_call(
        body, name=name,
        grid_spec=pltpu.PrefetchScalarGridSpec(
            num_scalar_prefetch=1, grid=(n_chips, half // tr),
            in_specs=[pl.BlockSpec((tr, cols), lambda k, i, core_ref:
                                   (k * (rows // tr) + core_ref[0] * (half // tr) + i, 0)), tile],
            out_specs=tile),
        out_shape=jax.ShapeDtypeStruct(from_sibling.shape, from_sibling.dtype),
        compiler_params=_params(("parallel", "parallel"), 3 * tr * cols * 4),
    )(core, grad, from_sibling)


def _final_add(name, chip_sum, from_chips, chip, core):
    _, half, cols = chip_sum.shape
    tr = _sum_tile(half)

    def body(chip_ref, core_ref, own_ref, others_ref, o_ref):
        total = own_ref[...].astype(F32)
        for k in range(3):
            total = total + others_ref[k].astype(F32)
        o_ref[...] = total

    return pl.pallas_call(
        body, name=name,
        grid_spec=pltpu.PrefetchScalarGridSpec(
            num_scalar_prefetch=2, grid=(half // tr,),
            in_specs=[pl.BlockSpec((None, tr, cols), lambda i, chip_ref, core_ref: (chip_ref[0], i, 0)),
                      pl.BlockSpec((3, tr, cols), lambda i, chip_ref, core_ref: (0, i, 0))],
            out_specs=pl.BlockSpec((tr, cols),
                                   lambda i, chip_ref, core_ref: (core_ref[0] * (half // tr) + i, 0))),
        out_shape=jax.ShapeDtypeStruct((2 * half, cols), F32),
        compiler_params=_params(("parallel",), 6 * tr * cols * 4),
    )(chip, core, chip_sum, from_chips)


def _share_halves(shards, small):
    n_s = len(shards)
    rows_s = small.shape[0]

    def body(*refs):
        small_ref = refs[n_s]
        outs, small_out = refs[n_s + 1:2 * n_s + 1], refs[2 * n_s + 1]
        small_all, send, recv, small_send, small_recv = refs[2 * n_s + 2:]
        x, y, c = _mesh_position()
        me = 4 * x + 2 * y + c
        sibling = (x, y, 1 - c)
        pending = []
        for i in range(n_s):
            half = shards[i].shape[0] // 2
            mine = outs[i].at[pl.ds(pl.multiple_of(c * half, 16), half)]
            cp = _remote(mine, mine, send.at[i], recv.at[i], sibling)
            cp.start()
            pending.append(cp)
        small_all[me] = small_ref[...]
        for j in range(N_DEV - 1):
            peer = (me + 1 + j) % N_DEV
            cp = _remote(small_all.at[me], small_all.at[me], small_send.at[j], small_recv.at[j],
                         (peer // 4, (peer // 2) % 2, peer % 2))
            cp.start()
            pending.append(cp)
        for i in range(n_s):
            half = shards[i].shape[0] // 2
            theirs = outs[i].at[pl.ds(pl.multiple_of((1 - c) * half, 16), half)]
            _remote(theirs, theirs, send.at[i], recv.at[i], sibling).wait_recv()
        for j in range(N_DEV - 1):
            peer = (me + N_DEV - 1 - j) % N_DEV
            _remote(small_all.at[peer], small_all.at[peer], small_send.at[j], small_recv.at[j],
                    sibling).wait_recv()
        total = small_all[0]
        for dev in range(1, N_DEV):
            total = total + small_all[dev]
        small_out[...] = total
        for cp in pending:
            cp.wait_send()

    res = pl.pallas_call(
        body, name="share_halves",
        in_specs=[HBM_SPEC] * n_s + [VMEM_SPEC], out_specs=[HBM_SPEC] * n_s + [VMEM_SPEC],
        out_shape=[jax.ShapeDtypeStruct(sh.shape, sh.dtype) for sh in shards]
        + [jax.ShapeDtypeStruct((rows_s, 128), F32)],
        scratch_shapes=[pltpu.VMEM((N_DEV, rows_s, 128), F32),
                        pltpu.SemaphoreType.DMA((n_s,)), pltpu.SemaphoreType.DMA((n_s,)),
                        pltpu.SemaphoreType.DMA((N_DEV - 1,)), pltpu.SemaphoreType.DMA((N_DEV - 1,))],
        input_output_aliases={i: i for i in range(n_s)},
    )(*shards, small)
    return res[:n_s], res[n_s]


def _pack_small(parts, rows):
    flat = jnp.concatenate([p.reshape(-1) for p in parts])
    flat = jnp.pad(flat, (0, rows * 128 - flat.shape[0]))
    return flat.reshape(rows, 128)


def _unpack_small(packed, shapes):
    flat = packed.reshape(-1)
    out, off = [], 0
    for shp in shapes:
        n = int(np.prod(shp))
        out.append(flat[off:off + n].reshape(shp))
        off += n
    return out


def kernel(x, g_attn, w_in, b_in, sinks_a, g_out_a, g_out_b, w_out, g_mlp, w_1, w_2, g_final, loss_target, m_g_attn, m_w_in, m_b_in, m_sinks_a, m_g_out_a, m_g_out_b, m_w_out, m_g_mlp, m_w_1, m_w_2, m_g_final, v_g_attn, v_w_in, v_b_in, v_sinks_a, v_g_out_a, v_g_out_b, v_w_out, v_g_mlp, v_w_1, v_w_2, v_g_final):
    s, d = x.shape[1], x.shape[2]
    d_in = b_in.shape[1]
    qa = g_out_a.shape[1]
    qb = g_out_b.shape[1]
    kva = 2 * N_KV_GROUPS * HEAD_DIM
    assert d_in == qa + kva + 3 * qb and qa + qb == w_out.shape[1] * N_CHIPS
    d_ff = w_1.shape[2] * N_CHIPS
    ff_shard = w_1.shape[2]
    n_heads_a, n_heads_b = qa // HEAD_DIM, qb // HEAD_DIM
    slopes_a, slopes_b = alibi_slopes(n_heads_a), alibi_slopes(n_heads_b)

    x2d = x[0]
    target = loss_target[0]

    core_index = lax.axis_index("c").astype(jnp.int32).reshape(1)
    chip_index = (2 * lax.axis_index("x") + lax.axis_index("y")).astype(jnp.int32).reshape(1)
    shards = {"w_in": w_in[0].T, "w_out": w_out[0], "w_1": w_1[0], "w_2": w_2[0]}
    halves = [sh.shape[0] // 2 for sh in shards.values()]
    w_in_t, w_out_g, w_1_g, w_2_g = [_own_shard_in_place(f"place_{n}", sh, chip_index)
                                     for n, sh in shards.items()]

    tm = _tile(s, 1024)

    (h1, r1), (w_in_t,) = _norm_fwd("norm_attn", x2d, g_attn,
                                    hook=_gather_hook(w_in_t, 0, halves[0], mid_fraction=1.0))

    q_a, = _project_by_class("proj_qa", h1, w_in_t, b_in, 0, qa, (1,))
    kv_a, = _project_by_class("proj_kva", h1, w_in_t, b_in, qa, kva, (1,))
    q_bs, (w_out_g,) = _project_by_class("proj_qb", h1, w_in_t, b_in, qa + kva, qb, DILATIONS,
                                         hook=_gather_hook(w_out_g, 0, halves[1]))
    k_bs = _project_by_class("proj_kb", h1, w_in_t, b_in, qa + kva + qb, qb, DILATIONS)
    v_bs = _project_by_class("proj_vb", h1, w_in_t, b_in, qa + kva + 2 * qb, qb, DILATIONS)

    quarter = halves[2] // 4
    sinks = sinks_a.reshape(-1)
    (o_a, lse_a), (w_1_g,) = _attn_fwd("attn_a_fwd", q_a, kv_a, dil=1, max_steps=WINDOW_A - 1, slopes=slopes_a,
                                       sinks=sinks, hook=_gather_hook(w_1_g, 0, quarter))
    o_a = o_a[0]
    o_bs, lse_bs = [], []
    for n, (window, dil) in enumerate(DILATED_BRANCHES):
        (o, l), (w_1_g,) = _attn_fwd(f"attn_b{dil}_fwd", q_bs[n], (k_bs[n], v_bs[n]), dil=dil,
                                     max_steps=window // dil, slopes=slopes_b,
                                     hook=_gather_hook(w_1_g, (n + 1) * quarter, (n + 2) * quarter))
        o_bs.append(o)
        lse_bs.append(l)
    w_1_g = w_1_g.reshape(N_CHIPS, d, ff_shard)
    mix, o_b, *lse_tot, r_a, r_b = _mix_fwd(o_a, o_bs, lse_bs, g_out_a, g_out_b)

    tn = _tile(d, 1024)
    a_spec, b_spec = _mm_specs("nn", tm, tn, d)
    tile_mn = pl.BlockSpec((tm, tn), lambda i, j, k: (i, j))
    x2 = _matmul("out_proj", mix, w_out_g, [x2d], mode="nn", grid=(s // tm, d // tn, 1),
                 a_spec=a_spec, b_spec=b_spec, extra_specs=[tile_mn],
                 out_shapes=[jax.ShapeDtypeStruct((s, d), F32)], out_specs=[tile_mn],
                 epilogue=lambda acc, res: (acc + res,))[0]

    h2, r2 = _norm_fwd("norm_mlp", x2, g_mlp)

    tn = _tile(ff_shard, 1024)
    per = ff_shard // tn
    a_spec, _ = _mm_specs("nn", tm, tn, d)
    tile_mn = pl.BlockSpec((tm, tn), lambda i, j, k: (i, j))
    (u,), (w_2_g,) = _matmul(
        "mlp_up", h2, w_1_g, [], mode="nn", grid=(s // tm, d_ff // tn, 1),
        a_spec=a_spec, b_spec=pl.BlockSpec((None, d, tn), lambda i, j, k: (j // per, 0, j % per)),
        extra_specs=[], out_shapes=[jax.ShapeDtypeStruct((s, d_ff), BF16)], out_specs=[tile_mn],
        epilogue=lambda acc: (jnp.maximum(acc, 0.0),),
        hook=_gather_hook(w_2_g, 0, halves[3]))

    tn = _tile(d, 1024)
    tk = _tile(d_ff, 2048)
    a_spec, b_spec = _mm_specs("nn", tm, tn, tk)
    tile_mn = pl.BlockSpec((tm, tn), lambda i, j, k: (i, j))
    x3 = _matmul("mlp_down", u, w_2_g, [x2], mode="nn", grid=(s // tm, d // tn, d_ff // tk),
                 a_spec=a_spec, b_spec=b_spec, extra_specs=[tile_mn],
                 out_shapes=[jax.ShapeDtypeStruct((s, d), F32)], out_specs=[tile_mn],
                 prologue=lambda a: a * a, epilogue=lambda acc, res: (acc + res,), acc_shape=(tm, tn))[0]

    dx3, dx3b, loss_part, dg_final = _loss_head(x3, target, g_final.reshape(1, d))

    tn = _tile(d_ff, 1024)
    a_spec, b_spec = _mm_specs("nt", tm, tn, d)
    tile_mn = pl.BlockSpec((tm, tn), lambda i, j, k: (i, j))
    dpre = _matmul("mlp_down_dx", dx3b, w_2_g, [u], mode="nt", grid=(s // tm, d_ff // tn, 1),
                   a_spec=a_spec, b_spec=b_spec, extra_specs=[tile_mn],
                   out_shapes=[jax.ShapeDtypeStruct((s, d_ff), BF16)], out_specs=[tile_mn],
                   epilogue=lambda acc, uu: (acc * (2.0 * uu.astype(F32)),))[0]

    wire = GRAD_WIRE_DTYPE
    tk_s = _tile(s, 2048)
    tmw = _tile(d_ff, 1024)
    a_spec, b_spec = _mm_specs("tn", tmw, d, tk_s)
    dw_2 = _matmul("mlp_down_dw", u, dx3b, [], mode="tn", grid=(d_ff // tmw, 1, s // tk_s),
                   a_spec=a_spec, b_spec=b_spec, extra_specs=[],
                   out_shapes=[jax.ShapeDtypeStruct((d_ff, d), wire)],
                   out_specs=[pl.BlockSpec((tmw, d), lambda i, j, k: (i, j))],
                   prologue=lambda a: a * a, epilogue=lambda acc: (acc,), acc_shape=(tmw, d))[0]

    tn = _tile(d, 1024)
    tk = _tile(ff_shard, 2048)
    per = ff_shard // tk
    a_spec, _ = _mm_specs("nt", tm, tn, tk)
    tile_mn = pl.BlockSpec((tm, tn), lambda i, j, k: (i, j))
    (dh2,), (sib_2,) = _matmul(
        "mlp_up_dx", dpre, w_1_g, [], mode="nt", grid=(s // tm, d // tn, d_ff // tk),
        a_spec=a_spec, b_spec=pl.BlockSpec((None, tn, tk), lambda i, j, k: (k // per, j, k % per)),
        extra_specs=[], out_shapes=[jax.ShapeDtypeStruct((s, d), F32)], out_specs=[tile_mn],
        epilogue=lambda acc: (acc,), acc_shape=(tm, tn), hook=_exchange_hook(dw_2))
    chip_sum_2 = _chip_add("chip_add_w_2", dw_2, sib_2, core_index)

    tmw = _tile(d, 1024)
    tnw = _tile(ff_shard, 2048)
    per = ff_shard // tnw
    a_spec, b_spec = _mm_specs("tn", tmw, tnw, tk_s)
    dw_1 = _matmul("mlp_up_dw", h2, dpre, [], mode="tn", grid=(d // tmw, d_ff // tnw, s // tk_s),
                   a_spec=a_spec, b_spec=b_spec, extra_specs=[],
                   out_shapes=[jax.ShapeDtypeStruct((N_CHIPS, d, ff_shard), wire)],
                   out_specs=[pl.BlockSpec((None, tmw, tnw), lambda i, j, k: (j // per, i, j % per))],
                   epilogue=lambda acc: (acc,), acc_shape=(tmw, tnw))[0]

    dw_1 = dw_1.reshape(N_CHIPS * d, ff_shard)
    (dx2, dx2b, dg_mlp), (sib_1,) = _norm_bwd("norm_mlp_bwd", dh2, x2, r2, g_mlp, dx3,
                                              hook=_exchange_hook(dw_1))
    chip_sum_1 = _chip_add("chip_add_w_1", dw_1, sib_1, core_index)

    tn = _tile(d, 1024)
    a_spec, b_spec = _mm_specs("nt", tm, tn, d)
    tile_mn = pl.BlockSpec((tm, tn), lambda i, j, k: (i, j))
    dmix = _matmul("out_proj_dx", dx2b, w_out_g, [], mode="nt", grid=(s // tm, d // tn, 1),
                   a_spec=a_spec, b_spec=b_spec, extra_specs=[],
                   out_shapes=[jax.ShapeDtypeStruct((s, d), F32)], out_specs=[tile_mn],
                   epilogue=lambda acc: (acc,))[0]

    tmw = _tile(d, 1024)
    a_spec, b_spec = _mm_specs("tn", tmw, d, tk_s)
    dw_out = _matmul("out_proj_dw", mix, dx2b, [], mode="tn", grid=(d // tmw, 1, s // tk_s),
                     a_spec=a_spec, b_spec=b_spec, extra_specs=[],
                     out_shapes=[jax.ShapeDtypeStruct((d, d), wire)],
                     out_specs=[pl.BlockSpec((tmw, d), lambda i, j, k: (i, j))],
                     epilogue=lambda acc: (acc,), acc_shape=(tmw, d))[0]

    mix_grads, (sib_out,) = _mix_bwd(dmix, o_a, o_b, r_a, r_b, g_out_a, g_out_b, hook=_exchange_hook(dw_out))
    do_a, do_bs, delta_a, delta_bs = mix_grads[0], mix_grads[1:4], mix_grads[4], mix_grads[5:8]
    dg_out_a, dg_out_b = mix_grads[8:]
    chip_sum_out = _chip_add("chip_add_w_out", dw_out, sib_out, core_index)

    (dq_a, dkv_a, dsinks), (chips_2,) = _attn_bwd(
        "attn_a_bwd", q_a, kv_a, do_a[None], lse_a, delta_a[None], dil=1, max_steps=WINDOW_A - 1,
        slopes=slopes_a, sinks=sinks, hook=_scatter_hook(chip_sum_2))
    dqs, dks, dvs = [], [], []
    scatter = {1: chip_sum_1, 4: chip_sum_out}
    arrived = {}
    for n, (window, dil) in enumerate(DILATED_BRANCHES):
        res = _attn_bwd(f"attn_b{dil}_bwd", q_bs[n], (k_bs[n], v_bs[n]), do_bs[n], lse_tot[n],
                        delta_bs[n], dil=dil, max_steps=window // dil, slopes=slopes_b,
                        hook=_scatter_hook(scatter[dil]) if dil in scatter else None)
        if dil in scatter:
            res, (arrived[dil],) = res
        dq, dk, dv = res
        dqs.append(dq)
        dks.append(dk)
        dvs.append(dv)
    half_2 = _final_add("final_add_w_2", chip_sum_2, chips_2, chip_index, core_index)
    half_1 = _final_add("final_add_w_1", chip_sum_1, arrived[1], chip_index, core_index)
    half_out = _final_add("final_add_w_out", chip_sum_out, arrived[4], chip_index, core_index)
    (dproj, db_in), (gw_2,) = _assemble_dproj(dq_a[0], dkv_a[0], dqs, dks, dvs, hook=_share_hook(half_2))

    tmw = d_in // 2 if (d_in // 2) % 128 == 0 else d_in
    tnw = _tile(d, 1024)
    tk_s = _tile(s, 1024)
    a_spec, b_spec = _mm_specs("tn", tmw, tnw, tk_s)
    (dw_in_t,), (gw_1,) = _matmul(
        "in_proj_dw", dproj, h1, [], mode="tn", grid=(d_in // tmw, d // tnw, s // tk_s),
        a_spec=a_spec, b_spec=b_spec, extra_specs=[],
        out_shapes=[jax.ShapeDtypeStruct((d_in, d), wire)],
        out_specs=[pl.BlockSpec((tmw, tnw), lambda i, j, k: (i, j))],
        epilogue=lambda acc: (acc,), acc_shape=(tmw, tnw), hook=_share_hook(half_1))

    tn = _tile(d, 1024)
    a_spec, b_spec = _mm_specs("nn", tm, tn, d_in)
    tile_mn = pl.BlockSpec((tm, tn), lambda i, j, k: (i, j))
    (dh1,), (sib_in,) = _matmul("in_proj_dx", dproj, w_in_t, [], mode="nn", grid=(s // tm, d // tn, 1),
                                a_spec=a_spec, b_spec=b_spec, extra_specs=[],
                                out_shapes=[jax.ShapeDtypeStruct((s, d), F32)], out_specs=[tile_mn],
                                epilogue=lambda acc: (acc,), hook=_exchange_hook(dw_in_t))
    chip_sum_in = _chip_add("chip_add_w_in", dw_in_t, sib_in, core_index)

    (grad_x, _, dg_attn), (chips_in,) = _norm_bwd("norm_attn_bwd", dh1, x2d, r1, g_attn, dx2,
                                                  hook=_scatter_hook(chip_sum_in))
    half_in = _final_add("final_add_w_in", chip_sum_in, chips_in, chip_index, core_index)

    small_parts = [dg_attn, db_in, dsinks[:, :n_heads_a], dg_out_a, dg_out_b, dg_mlp, dg_final]
    small_shapes = [g_attn.shape, b_in.shape, sinks_a.shape, g_out_a.shape, g_out_b.shape, g_mlp.shape,
                    g_final.shape]
    n_small = sum(int(np.prod(shp)) for shp in small_shapes)
    rows_s = -(-n_small // (8 * 128)) * 8
    (gw_in_t, gw_out), small_sum = _share_halves([half_in, half_out], _pack_small(small_parts, rows_s))

    upd_in = [t.T for t in _adamw("adamw_w_in", w_in[0].T, gw_in_t, m_w_in[0].T, v_w_in[0].T)]
    upd_out = _adamw("adamw_w_out", w_out[0], gw_out, m_w_out[0], v_w_out[0])
    upd_1 = _adamw("adamw_w_1", w_1[0], gw_1, m_w_1[0], v_w_1[0])
    upd_2 = _adamw("adamw_w_2", w_2[0], gw_2, m_w_2[0], v_w_2[0])
    small_w = [g_attn, b_in, sinks_a, g_out_a, g_out_b, g_mlp, g_final]
    small_m = [m_g_attn, m_b_in, m_sinks_a, m_g_out_a, m_g_out_b, m_g_mlp, m_g_final]
    small_v = [v_g_attn, v_b_in, v_sinks_a, v_g_out_a, v_g_out_b, v_g_mlp, v_g_final]
    upd_small = _adamw("adamw_small", _pack_small(small_w, rows_s), small_sum,
                       _pack_small(small_m, rows_s), _pack_small(small_v, rows_s))
    d_small, m_small, v_small, g_small = [_unpack_small(t, small_shapes) for t in upd_small]

    loss = lax.psum(loss_part[0, 0], ("x", "y", "c"))

    def ordered(small, big):
        w_in_v, w_out_v, w_1_v, w_2_v = big
        return [small[0], w_in_v[None], small[1], small[2], small[3], small[4], w_out_v[None], small[5],
                w_1_v[None], w_2_v[None], small[6]]

    grads = ordered(g_small, (upd_in[3], upd_out[3], upd_1[3], upd_2[3]))
    deltas = ordered(d_small, (upd_in[0], upd_out[0], upd_1[0], upd_2[0]))
    new_m = ordered(m_small, (upd_in[1], upd_out[1], upd_1[1], upd_2[1]))
    new_v = ordered(v_small, (upd_in[2], upd_out[2], upd_1[2], upd_2[2]))
    return (loss, grad_x[None], *grads, *deltas, *new_m, *new_v)
```
